```python
import jax, jax.numpy as jnp
from jax import lax
import numpy as np

D_MODEL = 1024
BATCH = 8
SEQ = 16384
DEPTH = 4

N_MIXERS = 2
N_POOL_LAYERS = (DEPTH + 1) // 2
N_MLA_LAYERS = DEPTH // 2
POOL_WINDOWS = (2, 4, 8, 16)
N_POOL_GROUPS = len(POOL_WINDOWS)
POOL_GROUP = D_MODEL // N_POOL_GROUPS
N_HEADS = D_MODEL // 128
QK_NOPE = 128
QK_ROPE = 64
QK_HEAD = QK_NOPE + QK_ROPE
V_HEAD = 128
Q_LORA = 3 * D_MODEL // 4
KV_LORA = D_MODEL // 4
ROPE_THETA = 10000.0
Q_BLOCK = 128
D_FF = 2816
FFN_HALF = 0.5
EPS = 1e-6

kernel_name = "hybrid_pool_mla_macaron_trunk"


def rmsnorm(x, gain):
    x32 = x.astype(jnp.float32)
    y = x32 * lax.rsqrt(jnp.mean(x32 * x32, axis=-1, keepdims=True) + EPS)
    return (y * gain.astype(jnp.float32)).astype(x.dtype)


def swiglu(h, w_gate, w_up, w_down):
    return (jax.nn.silu(h @ w_gate) * (h @ w_up)) @ w_down


def pool_mixer(h, w, scale):
    B, S, D = h.shape
    u = h.astype(jnp.float32).reshape(B, S, N_POOL_GROUPS, POOL_GROUP)
    cs = jnp.pad(jnp.cumsum(u, axis=1), ((0, 0), (1, 0), (0, 0), (0, 0)))
    sums = []
    for g, w_len in enumerate(POOL_WINDOWS):
        c = cs[:, :, g]
        lower = jnp.pad(c[:, :S + 1 - w_len], ((0, 0), (w_len - 1, 0), (0, 0)))
        sums.append(c[:, 1:] - lower)
    window_sum = jnp.stack(sums, axis=2)
    t = jnp.arange(S)
    count = jnp.minimum(t[:, None] + 1, jnp.array(POOL_WINDOWS, jnp.int32)[None, :])
    pooled = window_sum / count.astype(jnp.float32)[None, :, :, None] - u
    y = jnp.einsum('bsgc,gcd->bsgd', pooled.astype(h.dtype), w).reshape(B, S, D)
    return y * scale


def rope_tail(x, cos, sin):
    x_nope, x_pe = jnp.split(x, [QK_NOPE], axis=-1)
    x1, x2 = jnp.split(x_pe, 2, axis=-1)
    c = cos[:, :, None, :].astype(x.dtype)
    s = sin[:, :, None, :].astype(x.dtype)
    return jnp.concatenate([x_nope, x1 * c - x2 * s, x2 * c + x1 * s], axis=-1)


def causal_attention(q, k, v):
    B, S, H, Dq = q.shape
    nb = S // Q_BLOCK
    qb = q.reshape(B, nb, Q_BLOCK, H, Dq).transpose(1, 0, 3, 2, 4)
    kt = k.transpose(0, 2, 1, 3)
    vt = v.transpose(0, 2, 1, 3)
    kpos = jnp.arange(S)
    scale = QK_HEAD ** -0.5

    def one_block(args):
        q_blk, blk = args
        s = jnp.einsum('bhqd,bhkd->bhqk', q_blk, kt,
                       preferred_element_type=jnp.float32) * scale
        qpos = blk * Q_BLOCK + jnp.arange(Q_BLOCK)
        s = jnp.where(kpos[None, :] <= qpos[:, None], s, -jnp.inf)
        p = jax.nn.softmax(s, axis=-1).astype(vt.dtype)
        return jnp.einsum('bhqk,bhkv->bhqv', p, vt)

    out = lax.map(one_block, (qb, jnp.arange(nb)))
    return out.transpose(1, 0, 3, 2, 4).reshape(B, S, H, V_HEAD)


def mla_mixer(h, cos, sin, w_in, q_norm, w_q_up, kv_norm, w_kv_up,
              q_head_norm, k_head_norm, w_out):
    B, S, _ = h.shape
    lat = h @ w_in
    cq, ckv, k_pe = jnp.split(lat, [Q_LORA, Q_LORA + KV_LORA], axis=-1)
    q = (rmsnorm(cq, q_norm) @ w_q_up).reshape(B, S, N_HEADS, QK_HEAD)
    kv = (rmsnorm(ckv, kv_norm) @ w_kv_up).reshape(B, S, N_HEADS, QK_NOPE + V_HEAD)
    k_nope, v = jnp.split(kv, [QK_NOPE], axis=-1)
    k = jnp.concatenate(
        [k_nope, jnp.broadcast_to(k_pe[:, :, None, :], (B, S, N_HEADS, QK_ROPE))], axis=-1)
    q = rope_tail(rmsnorm(q, q_head_norm), cos, sin)
    k = rope_tail(rmsnorm(k, k_head_norm), cos, sin)
    o = causal_attention(q, k, v)
    return o.reshape(B, S, N_HEADS * V_HEAD) @ w_out


def _fwd_setup_inputs(seed: int = 0) -> dict:
    key = jax.random.key(seed)
    ks = jax.random.split(key, 24)
    f32 = jnp.float32

    def dense(k, shape, fan_in):
        return jax.random.normal(k, shape, f32) * fan_in ** -0.5

    def gain(k, shape):
        return 1.0 + 0.05 * jax.random.normal(k, shape, f32)

    x = jax.random.normal(ks[0], (BATCH, SEQ, D_MODEL), f32)
    offsets = jax.random.randint(ks[1], (BATCH, 1), 0, 4096, dtype=jnp.int32)
    positions = (jnp.arange(SEQ, dtype=jnp.int32)[None, :] + offsets).astype(jnp.int32)
    return {
        "x": x,
        "positions": positions,
        "ffn1_norm": gain(ks[2], (DEPTH, D_MODEL)),
        "ffn1_w_gate": dense(ks[3], (DEPTH, D_MODEL, D_FF), D_MODEL),
        "ffn1_w_up": dense(ks[4], (DEPTH, D_MODEL, D_FF), D_MODEL),
        "ffn1_w_down": dense(ks[5], (DEPTH, D_FF, D_MODEL), D_FF),
        "mix_norm": gain(ks[6], (DEPTH, D_MODEL)),
        "pool_w": dense(ks[7], (N_POOL_LAYERS, N_POOL_GROUPS, POOL_GROUP, POOL_GROUP), POOL_GROUP),
        "pool_scale": gain(ks[8], (N_POOL_LAYERS, D_MODEL)),
        "mla_w_in": dense(ks[9], (N_MLA_LAYERS, D_MODEL, Q_LORA + KV_LORA + QK_ROPE), D_MODEL),
        "mla_q_norm": gain(ks[10], (N_MLA_LAYERS, Q_LORA)),
        "mla_w_q_up": dense(ks[11], (N_MLA_LAYERS, Q_LORA, N_HEADS * QK_HEAD), Q_LORA),
        "mla_kv_norm": gain(ks[12], (N_MLA_LAYERS, KV_LORA)),
        "mla_w_kv_up": dense(ks[13], (N_MLA_LAYERS, KV_LORA, N_HEADS * (QK_NOPE + V_HEAD)), KV_LORA),
        "mla_q_head_norm": gain(ks[14], (N_MLA_LAYERS, QK_HEAD)),
        "mla_k_head_norm": gain(ks[15], (N_MLA_LAYERS, QK_HEAD)),
        "mla_w_out": dense(ks[16], (N_MLA_LAYERS, N_HEADS * V_HEAD, D_MODEL), N_HEADS * V_HEAD),
        "ffn2_norm": gain(ks[17], (DEPTH, D_MODEL)),
        "ffn2_w_gate": dense(ks[18], (DEPTH, D_MODEL, D_FF), D_MODEL),
        "ffn2_w_up": dense(ks[19], (DEPTH, D_MODEL, D_FF), D_MODEL),
        "ffn2_w_down": dense(ks[20], (DEPTH, D_FF, D_MODEL), D_FF),
    }


def _fwd_reference(x, positions, ffn1_norm, ffn1_w_gate, ffn1_w_up, ffn1_w_down, mix_norm,
              pool_w, pool_scale, mla_w_in, mla_q_norm, mla_w_q_up, mla_kv_norm,
              mla_w_kv_up, mla_q_head_norm, mla_k_head_norm, mla_w_out,
              ffn2_norm, ffn2_w_gate, ffn2_w_up, ffn2_w_down):
    inv_freq = 1.0 / (ROPE_THETA ** (jnp.arange(0, QK_ROPE, 2, dtype=jnp.float32) / QK_ROPE))
    ang = positions.astype(jnp.float32)[..., None] * inv_freq
    cos, sin = jnp.cos(ang), jnp.sin(ang)

    for i in range(DEPTH):
        h = rmsnorm(x, ffn1_norm[i])
        x = x + FFN_HALF * swiglu(h, ffn1_w_gate[i], ffn1_w_up[i], ffn1_w_down[i])
        h = rmsnorm(x, mix_norm[i])
        j = i // N_MIXERS
        if i % N_MIXERS == 0:
            x = x + pool_mixer(h, pool_w[j], pool_scale[j])
        else:
            x = x + mla_mixer(h, cos, sin, mla_w_in[j], mla_q_norm[j], mla_w_q_up[j],
                              mla_kv_norm[j], mla_w_kv_up[j], mla_q_head_norm[j],
                              mla_k_head_norm[j], mla_w_out[j])
        h = rmsnorm(x, ffn2_norm[i])
        x = x + FFN_HALF * swiglu(h, ffn2_w_gate[i], ffn2_w_up[i], ffn2_w_down[i])
    return x


import jax as _jax
import jax.numpy as _jnp

TWIN_FORMAT = 'train_step'
FWD_PARAMS = ['x', 'positions', 'ffn1_norm', 'ffn1_w_gate', 'ffn1_w_up', 'ffn1_w_down', 'mix_norm', 'pool_w', 'pool_scale', 'mla_w_in', 'mla_q_norm', 'mla_w_q_up', 'mla_kv_norm', 'mla_w_kv_up', 'mla_q_head_norm', 'mla_k_head_norm', 'mla_w_out', 'ffn2_norm', 'ffn2_w_gate', 'ffn2_w_up', 'ffn2_w_down']
TWIN_WEIGHTS = ['ffn1_norm', 'ffn1_w_gate', 'ffn1_w_up', 'ffn1_w_down', 'mix_norm', 'pool_w', 'pool_scale', 'mla_w_in', 'mla_q_norm', 'mla_w_q_up', 'mla_kv_norm', 'mla_w_kv_up', 'mla_q_head_norm', 'mla_k_head_norm', 'mla_w_out', 'ffn2_norm', 'ffn2_w_gate', 'ffn2_w_up', 'ffn2_w_down']
TWIN_DIFF_INPUT = 'x'
TWIN_INPUTS = ['x', 'positions', 'ffn1_norm', 'ffn1_w_gate', 'ffn1_w_up', 'ffn1_w_down', 'mix_norm', 'pool_w', 'pool_scale', 'mla_w_in', 'mla_q_norm', 'mla_w_q_up', 'mla_kv_norm', 'mla_w_kv_up', 'mla_q_head_norm', 'mla_k_head_norm', 'mla_w_out', 'ffn2_norm', 'ffn2_w_gate', 'ffn2_w_up', 'ffn2_w_down', 'loss_target', 'm_ffn1_norm', 'm_ffn1_w_gate', 'm_ffn1_w_up', 'm_ffn1_w_down', 'm_mix_norm', 'm_pool_w', 'm_pool_scale', 'm_mla_w_in', 'm_mla_q_norm', 'm_mla_w_q_up', 'm_mla_kv_norm', 'm_mla_w_kv_up', 'm_mla_q_head_norm', 'm_mla_k_head_norm', 'm_mla_w_out', 'm_ffn2_norm', 'm_ffn2_w_gate', 'm_ffn2_w_up', 'm_ffn2_w_down', 'v_ffn1_norm', 'v_ffn1_w_gate', 'v_ffn1_w_up', 'v_ffn1_w_down', 'v_mix_norm', 'v_pool_w', 'v_pool_scale', 'v_mla_w_in', 'v_mla_q_norm', 'v_mla_w_q_up', 'v_mla_kv_norm', 'v_mla_w_kv_up', 'v_mla_q_head_norm', 'v_mla_k_head_norm', 'v_mla_w_out', 'v_ffn2_norm', 'v_ffn2_w_gate', 'v_ffn2_w_up', 'v_ffn2_w_down']
TWIN_OUTPUTS = ['loss', 'grad_x', 'grad_ffn1_norm', 'grad_ffn1_w_gate', 'grad_ffn1_w_up', 'grad_ffn1_w_down', 'grad_mix_norm', 'grad_pool_w', 'grad_pool_scale', 'grad_mla_w_in', 'grad_mla_q_norm', 'grad_mla_w_q_up', 'grad_mla_kv_norm', 'grad_mla_w_kv_up', 'grad_mla_q_head_norm', 'grad_mla_k_head_norm', 'grad_mla_w_out', 'grad_ffn2_norm', 'grad_ffn2_w_gate', 'grad_ffn2_w_up', 'grad_ffn2_w_down', 'delta_ffn1_norm', 'delta_ffn1_w_gate', 'delta_ffn1_w_up', 'delta_ffn1_w_down', 'delta_mix_norm', 'delta_pool_w', 'delta_pool_scale', 'delta_mla_w_in', 'delta_mla_q_norm', 'delta_mla_w_q_up', 'delta_mla_kv_norm', 'delta_mla_w_kv_up', 'delta_mla_q_head_norm', 'delta_mla_k_head_norm', 'delta_mla_w_out', 'delta_ffn2_norm', 'delta_ffn2_w_gate', 'delta_ffn2_w_up', 'delta_ffn2_w_down', 'new_m_ffn1_norm', 'new_m_ffn1_w_gate', 'new_m_ffn1_w_up', 'new_m_ffn1_w_down', 'new_m_mix_norm', 'new_m_pool_w', 'new_m_pool_scale', 'new_m_mla_w_in', 'new_m_mla_q_norm', 'new_m_mla_w_q_up', 'new_m_mla_kv_norm', 'new_m_mla_w_kv_up', 'new_m_mla_q_head_norm', 'new_m_mla_k_head_norm', 'new_m_mla_w_out', 'new_m_ffn2_norm', 'new_m_ffn2_w_gate', 'new_m_ffn2_w_up', 'new_m_ffn2_w_down', 'new_v_ffn1_norm', 'new_v_ffn1_w_gate', 'new_v_ffn1_w_up', 'new_v_ffn1_w_down', 'new_v_mix_norm', 'new_v_pool_w', 'new_v_pool_scale', 'new_v_mla_w_in', 'new_v_mla_q_norm', 'new_v_mla_w_q_up', 'new_v_mla_kv_norm', 'new_v_mla_w_kv_up', 'new_v_mla_q_head_norm', 'new_v_mla_k_head_norm', 'new_v_mla_w_out', 'new_v_ffn2_norm', 'new_v_ffn2_w_gate', 'new_v_ffn2_w_up', 'new_v_ffn2_w_down']
TWIN_LEAF_KINDS = {'loss': 'loss', 'grad_x': 'grad_x', 'grad_ffn1_norm': 'grad_w', 'grad_ffn1_w_gate': 'grad_w', 'grad_ffn1_w_up': 'grad_w', 'grad_ffn1_w_down': 'grad_w', 'grad_mix_norm': 'grad_w', 'grad_pool_w': 'grad_w', 'grad_pool_scale': 'grad_w', 'grad_mla_w_in': 'grad_w', 'grad_mla_q_norm': 'grad_w', 'grad_mla_w_q_up': 'grad_w', 'grad_mla_kv_norm': 'grad_w', 'grad_mla_w_kv_up': 'grad_w', 'grad_mla_q_head_norm': 'grad_w', 'grad_mla_k_head_norm': 'grad_w', 'grad_mla_w_out': 'grad_w', 'grad_ffn2_norm': 'grad_w', 'grad_ffn2_w_gate': 'grad_w', 'grad_ffn2_w_up': 'grad_w', 'grad_ffn2_w_down': 'grad_w', 'delta_ffn1_norm': 'delta_w', 'delta_ffn1_w_gate': 'delta_w', 'delta_ffn1_w_up': 'delta_w', 'delta_ffn1_w_down': 'delta_w', 'delta_mix_norm': 'delta_w', 'delta_pool_w': 'delta_w', 'delta_pool_scale': 'delta_w', 'delta_mla_w_in': 'delta_w', 'delta_mla_q_norm': 'delta_w', 'delta_mla_w_q_up': 'delta_w', 'delta_mla_kv_norm': 'delta_w', 'delta_mla_w_kv_up': 'delta_w', 'delta_mla_q_head_norm': 'delta_w', 'delta_mla_k_head_norm': 'delta_w', 'delta_mla_w_out': 'delta_w', 'delta_ffn2_norm': 'delta_w', 'delta_ffn2_w_gate': 'delta_w', 'delta_ffn2_w_up': 'delta_w', 'delta_ffn2_w_down': 'delta_w', 'new_m_ffn1_norm': 'new_m', 'new_m_ffn1_w_gate': 'new_m', 'new_m_ffn1_w_up': 'new_m', 'new_m_ffn1_w_down': 'new_m', 'new_m_mix_norm': 'new_m', 'new_m_pool_w': 'new_m', 'new_m_pool_scale': 'new_m', 'new_m_mla_w_in': 'new_m', 'new_m_mla_q_norm': 'new_m', 'new_m_mla_w_q_up': 'new_m', 'new_m_mla_kv_norm': 'new_m', 'new_m_mla_w_kv_up': 'new_m', 'new_m_mla_q_head_norm': 'new_m', 'new_m_mla_k_head_norm': 'new_m', 'new_m_mla_w_out': 'new_m', 'new_m_ffn2_norm': 'new_m', 'new_m_ffn2_w_gate': 'new_m', 'new_m_ffn2_w_up': 'new_m', 'new_m_ffn2_w_down': 'new_m', 'new_v_ffn1_norm': 'new_v', 'new_v_ffn1_w_gate': 'new_v', 'new_v_ffn1_w_up': 'new_v', 'new_v_ffn1_w_down': 'new_v', 'new_v_mix_norm': 'new_v', 'new_v_pool_w': 'new_v', 'new_v_pool_scale': 'new_v', 'new_v_mla_w_in': 'new_v', 'new_v_mla_q_norm': 'new_v', 'new_v_mla_w_q_up': 'new_v', 'new_v_mla_kv_norm': 'new_v', 'new_v_mla_w_kv_up': 'new_v', 'new_v_mla_q_head_norm': 'new_v', 'new_v_mla_k_head_norm': 'new_v', 'new_v_mla_w_out': 'new_v', 'new_v_ffn2_norm': 'new_v', 'new_v_ffn2_w_gate': 'new_v', 'new_v_ffn2_w_up': 'new_v', 'new_v_ffn2_w_down': 'new_v'}


def _forward(args):
    return _fwd_reference(*[args[k] for k in FWD_PARAMS])


def _output_shape():
    def fwd():
        inp = _fwd_setup_inputs(0)
        return _fwd_reference(*[inp[k] for k in FWD_PARAMS])
    out = _jax.eval_shape(fwd)
    return out.shape, out.dtype

N_MICROBATCH = 1
ADAM_LR = 0.001
ADAM_B1 = 0.9
ADAM_B2 = 0.999
ADAM_EPS = 1e-08
ADAM_WD = 0.01
ADAM_STEP = 10
PER_EXAMPLE_BATCH_AXIS = {'x': 0, 'positions': 0, 'loss_target': 0}
SHARED_INPUTS = []
_WEIGHT_DTYPES = {'ffn1_norm': _jnp.float32, 'ffn1_w_gate': _jnp.float32, 'ffn1_w_up': _jnp.float32, 'ffn1_w_down': _jnp.float32, 'mix_norm': _jnp.float32, 'pool_w': _jnp.float32, 'pool_scale': _jnp.float32, 'mla_w_in': _jnp.float32, 'mla_q_norm': _jnp.float32, 'mla_w_q_up': _jnp.float32, 'mla_kv_norm': _jnp.float32, 'mla_w_kv_up': _jnp.float32, 'mla_q_head_norm': _jnp.float32, 'mla_k_head_norm': _jnp.float32, 'mla_w_out': _jnp.float32, 'ffn2_norm': _jnp.float32, 'ffn2_w_gate': _jnp.float32, 'ffn2_w_up': _jnp.float32, 'ffn2_w_down': _jnp.float32}
MOMENT_SCALE = {'ffn1_norm': 2.408514e+01, 'ffn1_w_gate': 2.231471e-01, 'ffn1_w_up': 3.411454e-01, 'ffn1_w_down': 5.757564e-01, 'mix_norm': 7.677915e+01, 'pool_w': 1.234665e+01, 'pool_scale': 1.084718e+02, 'mla_w_in': 4.252459e-01, 'mla_q_norm': 2.463520e-01, 'mla_w_q_up': 1.736659e-01, 'mla_kv_norm': 3.444706e+00, 'mla_w_kv_up': 2.567411e-01, 'mla_q_head_norm': 2.532067e+00, 'mla_k_head_norm': 2.535808e+00, 'mla_w_out': 2.912188e-01, 'ffn2_norm': 2.460000e+01, 'ffn2_w_gate': 1.890778e-01, 'ffn2_w_up': 3.473199e-01, 'ffn2_w_down': 5.728716e-01}


def _to_microbatches(a, axis):
    t = _jnp.moveaxis(a, axis, 0)
    t = t.reshape((N_MICROBATCH, t.shape[0] // N_MICROBATCH) + t.shape[1:])
    return _jnp.moveaxis(t, 1, axis + 1)


def setup_inputs(seed: int = 0) -> dict:
    inp = _fwd_setup_inputs(seed)
    key = _jax.random.fold_in(_jax.random.key(seed), 7919)
    shape, _ = _output_shape()
    out = dict(inp)
    out["loss_target"] = _jax.random.normal(_jax.random.fold_in(key, 0), shape, _jnp.float32)
    for i, name in enumerate(TWIN_WEIGHTS):
        w = inp[name].astype(_jnp.float32)
        if MOMENT_SCALE is None:
            s = _jnp.sqrt(_jnp.mean(_jnp.square(w)) + 1e-30)
        else:
            s = MOMENT_SCALE[name]
        km, kv = _jax.random.split(_jax.random.fold_in(key, i + 1))
        out[name] = w
        out["m_" + name] = s * _jax.random.normal(km, w.shape, _jnp.float32)
        out["v_" + name] = (s * s) * _jax.random.uniform(kv, w.shape, _jnp.float32, 0.5, 1.5)
    if N_MICROBATCH > 1:
        for name, axis in PER_EXAMPLE_BATCH_AXIS.items():
            out[name] = _to_microbatches(out[name], axis)
    return {'x': out['x'], 'positions': out['positions'], 'ffn1_norm': out['ffn1_norm'], 'ffn1_w_gate': out['ffn1_w_gate'], 'ffn1_w_up': out['ffn1_w_up'], 'ffn1_w_down': out['ffn1_w_down'], 'mix_norm': out['mix_norm'], 'pool_w': out['pool_w'], 'pool_scale': out['pool_scale'], 'mla_w_in': out['mla_w_in'], 'mla_q_norm': out['mla_q_norm'], 'mla_w_q_up': out['mla_w_q_up'], 'mla_kv_norm': out['mla_kv_norm'], 'mla_w_kv_up': out['mla_w_kv_up'], 'mla_q_head_norm': out['mla_q_head_norm'], 'mla_k_head_norm': out['mla_k_head_norm'], 'mla_w_out': out['mla_w_out'], 'ffn2_norm': out['ffn2_norm'], 'ffn2_w_gate': out['ffn2_w_gate'], 'ffn2_w_up': out['ffn2_w_up'], 'ffn2_w_down': out['ffn2_w_down'], 'loss_target': out['loss_target'], 'm_ffn1_norm': out['m_ffn1_norm'], 'm_ffn1_w_gate': out['m_ffn1_w_gate'], 'm_ffn1_w_up': out['m_ffn1_w_up'], 'm_ffn1_w_down': out['m_ffn1_w_down'], 'm_mix_norm': out['m_mix_norm'], 'm_pool_w': out['m_pool_w'], 'm_pool_scale': out['m_pool_scale'], 'm_mla_w_in': out['m_mla_w_in'], 'm_mla_q_norm': out['m_mla_q_norm'], 'm_mla_w_q_up': out['m_mla_w_q_up'], 'm_mla_kv_norm': out['m_mla_kv_norm'], 'm_mla_w_kv_up': out['m_mla_w_kv_up'], 'm_mla_q_head_norm': out['m_mla_q_head_norm'], 'm_mla_k_head_norm': out['m_mla_k_head_norm'], 'm_mla_w_out': out['m_mla_w_out'], 'm_ffn2_norm': out['m_ffn2_norm'], 'm_ffn2_w_gate': out['m_ffn2_w_gate'], 'm_ffn2_w_up': out['m_ffn2_w_up'], 'm_ffn2_w_down': out['m_ffn2_w_down'], 'v_ffn1_norm': out['v_ffn1_norm'], 'v_ffn1_w_gate': out['v_ffn1_w_gate'], 'v_ffn1_w_up': out['v_ffn1_w_up'], 'v_ffn1_w_down': out['v_ffn1_w_down'], 'v_mix_norm': out['v_mix_norm'], 'v_pool_w': out['v_pool_w'], 'v_pool_scale': out['v_pool_scale'], 'v_mla_w_in': out['v_mla_w_in'], 'v_mla_q_norm': out['v_mla_q_norm'], 'v_mla_w_q_up': out['v_mla_w_q_up'], 'v_mla_kv_norm': out['v_mla_kv_norm'], 'v_mla_w_kv_up': out['v_mla_w_kv_up'], 'v_mla_q_head_norm': out['v_mla_q_head_norm'], 'v_mla_k_head_norm': out['v_mla_k_head_norm'], 'v_mla_w_out': out['v_mla_w_out'], 'v_ffn2_norm': out['v_ffn2_norm'], 'v_ffn2_w_gate': out['v_ffn2_w_gate'], 'v_ffn2_w_up': out['v_ffn2_w_up'], 'v_ffn2_w_down': out['v_ffn2_w_down']}


def _loss(weights, diff, rest, loss_target):
    with _jax.named_scope("forward"):
        args = {**rest, TWIN_DIFF_INPUT: diff, **{k: w.astype(_WEIGHT_DTYPES[k]) for k, w in weights.items()}}
        y = _forward(args)
    with _jax.named_scope("loss_head"):
        err = _jnp.square(y.astype(_jnp.float32) - loss_target)
        return 0.5 * _jnp.sum(_jnp.mean(err, axis=-1)) if err.ndim else 0.5 * err


def _adamw(w, g, m, v):
    m = ADAM_B1 * m + (1.0 - ADAM_B1) * g
    v = ADAM_B2 * v + (1.0 - ADAM_B2) * _jnp.square(g)
    m_hat = m / (1.0 - ADAM_B1 ** ADAM_STEP)
    v_hat = v / (1.0 - ADAM_B2 ** ADAM_STEP)
    delta = -ADAM_LR * (m_hat / (_jnp.sqrt(v_hat) + ADAM_EPS) + ADAM_WD * w)
    return delta, m, v


def reference(x, positions, ffn1_norm, ffn1_w_gate, ffn1_w_up, ffn1_w_down, mix_norm, pool_w, pool_scale, mla_w_in, mla_q_norm, mla_w_q_up, mla_kv_norm, mla_w_kv_up, mla_q_head_norm, mla_k_head_norm, mla_w_out, ffn2_norm, ffn2_w_gate, ffn2_w_up, ffn2_w_down, loss_target, m_ffn1_norm, m_ffn1_w_gate, m_ffn1_w_up, m_ffn1_w_down, m_mix_norm, m_pool_w, m_pool_scale, m_mla_w_in, m_mla_q_norm, m_mla_w_q_up, m_mla_kv_norm, m_mla_w_kv_up, m_mla_q_head_norm, m_mla_k_head_norm, m_mla_w_out, m_ffn2_norm, m_ffn2_w_gate, m_ffn2_w_up, m_ffn2_w_down, v_ffn1_norm, v_ffn1_w_gate, v_ffn1_w_up, v_ffn1_w_down, v_mix_norm, v_pool_w, v_pool_scale, v_mla_w_in, v_mla_q_norm, v_mla_w_q_up, v_mla_kv_norm, v_mla_w_kv_up, v_mla_q_head_norm, v_mla_k_head_norm, v_mla_w_out, v_ffn2_norm, v_ffn2_w_gate, v_ffn2_w_up, v_ffn2_w_down):
    given = dict(x=x, positions=positions, ffn1_norm=ffn1_norm, ffn1_w_gate=ffn1_w_gate, ffn1_w_up=ffn1_w_up, ffn1_w_down=ffn1_w_down, mix_norm=mix_norm, pool_w=pool_w, pool_scale=pool_scale, mla_w_in=mla_w_in, mla_q_norm=mla_q_norm, mla_w_q_up=mla_w_q_up, mla_kv_norm=mla_kv_norm, mla_w_kv_up=mla_w_kv_up, mla_q_head_norm=mla_q_head_norm, mla_k_head_norm=mla_k_head_norm, mla_w_out=mla_w_out, ffn2_norm=ffn2_norm, ffn2_w_gate=ffn2_w_gate, ffn2_w_up=ffn2_w_up, ffn2_w_down=ffn2_w_down, loss_target=loss_target, m_ffn1_norm=m_ffn1_norm, m_ffn1_w_gate=m_ffn1_w_gate, m_ffn1_w_up=m_ffn1_w_up, m_ffn1_w_down=m_ffn1_w_down, m_mix_norm=m_mix_norm, m_pool_w=m_pool_w, m_pool_scale=m_pool_scale, m_mla_w_in=m_mla_w_in, m_mla_q_norm=m_mla_q_norm, m_mla_w_q_up=m_mla_w_q_up, m_mla_kv_norm=m_mla_kv_norm, m_mla_w_kv_up=m_mla_w_kv_up, m_mla_q_head_norm=m_mla_q_head_norm, m_mla_k_head_norm=m_mla_k_head_norm, m_mla_w_out=m_mla_w_out, m_ffn2_norm=m_ffn2_norm, m_ffn2_w_gate=m_ffn2_w_gate, m_ffn2_w_up=m_ffn2_w_up, m_ffn2_w_down=m_ffn2_w_down, v_ffn1_norm=v_ffn1_norm, v_ffn1_w_gate=v_ffn1_w_gate, v_ffn1_w_up=v_ffn1_w_up, v_ffn1_w_down=v_ffn1_w_down, v_mix_norm=v_mix_norm, v_pool_w=v_pool_w, v_pool_scale=v_pool_scale, v_mla_w_in=v_mla_w_in, v_mla_q_norm=v_mla_q_norm, v_mla_w_q_up=v_mla_w_q_up, v_mla_kv_norm=v_mla_kv_norm, v_mla_w_kv_up=v_mla_w_kv_up, v_mla_q_head_norm=v_mla_q_head_norm, v_mla_k_head_norm=v_mla_k_head_norm, v_mla_w_out=v_mla_w_out, v_ffn2_norm=v_ffn2_norm, v_ffn2_w_gate=v_ffn2_w_gate, v_ffn2_w_up=v_ffn2_w_up, v_ffn2_w_down=v_ffn2_w_down)
    weights = {n: given[n] for n in TWIN_WEIGHTS}
    shared = {n: given[n] for n in SHARED_INPUTS}
    per_example = {n: given[n] for n in ['x', 'positions']}
    grad_fn = _jax.value_and_grad(_loss, argnums=(0, 1))

    def one_microbatch(ex, loss_target):
        ex = dict(ex)
        diff = ex.pop(TWIN_DIFF_INPUT)
        return grad_fn(weights, diff, {**shared, **ex}, loss_target)

    if N_MICROBATCH == 1:
        loss, (grad_w, grad_x) = one_microbatch(per_example, given["loss_target"])
    else:
        def body(carry, xs):
            loss_sum, grad_sum = carry
            l_k, (gw_k, gx_k) = one_microbatch(xs[0], xs[1])
            with _jax.named_scope("update"):
                return (loss_sum + l_k, _jax.tree.map(_jnp.add, grad_sum, gw_k)), gx_k

        init = (_jnp.zeros((), _jnp.float32), _jax.tree.map(_jnp.zeros_like, weights))
        (loss, grad_w), grad_x = _jax.lax.scan(body, init, (per_example, given["loss_target"]))
    with _jax.named_scope("update"):
        delta_w, new_m, new_v = {}, {}, {}
        for n in TWIN_WEIGHTS:
            delta_w[n], new_m[n], new_v[n] = _adamw(weights[n], grad_w[n], given["m_" + n], given["v_" + n])
    return (loss, grad_x, *[grad_w[n] for n in TWIN_WEIGHTS], *[delta_w[n] for n in TWIN_WEIGHTS],
            *[new_m[n] for n in TWIN_WEIGHTS], *[new_v[n] for n in TWIN_WEIGHTS])
```

```python
import functools

import jax
import jax.numpy as jnp
from jax import lax
from jax.experimental import pallas as pl
from jax.experimental.pallas import tpu as pltpu

F32 = jnp.float32
BF16 = jnp.bfloat16

D_MODEL = 1024
DEPTH = 4
D_FF = 2816
POOL_WINDOWS = (2, 4, 8, 16)
POOL_GROUP = 256
POOL_HALO = 16
N_HEADS = 8
QK_NOPE = 128
QK_ROPE = 64
QK_HEAD = 192
V_HEAD = 128
Q_LORA = 768
KV_LORA = 256
ROPE_THETA = 10000.0
EPS = 1e-6
FFN_HALF = 0.5
ADAM_LR = 0.001
ADAM_B1 = 0.9
ADAM_B2 = 0.999
ADAM_EPS = 1e-08
ADAM_WD = 0.01
ADAM_STEP = 10

N_DEV = 8
HEAD_PAD = 256
LAT_PAD = 1152
V7X_VMEM_LIMIT = 56 * 1024 * 1024
FF_CHUNK = 256
SM_ROWS = 24

NAMES = ['x', 'positions', 'ffn1_norm', 'ffn1_w_gate', 'ffn1_w_up', 'ffn1_w_down', 'mix_norm', 'pool_w',
         'pool_scale', 'mla_w_in', 'mla_q_norm', 'mla_w_q_up', 'mla_kv_norm', 'mla_w_kv_up', 'mla_q_head_norm',
         'mla_k_head_norm', 'mla_w_out', 'ffn2_norm', 'ffn2_w_gate', 'ffn2_w_up', 'ffn2_w_down']
WEIGHTS = NAMES[2:]


def _tiles(n):
    return dict(ffn_fwd=min(512, n), ffn_bwd=min(256, n), attn=min(512, n), mla_bwd=min(256, n),
                mla_fwd=min(512, n), pool=min(512, n), tn=min(512, n), rows=min(1024, n))


def _params(*sem):
    return pltpu.CompilerParams(dimension_semantics=sem, vmem_limit_bytes=V7X_VMEM_LIMIT)


def _dot(a, b):
    return jnp.dot(a, b, preferred_element_type=F32)


def _dot_nt(a, b):
    return lax.dot_general(a, b, (((1,), (1,)), ((), ())), preferred_element_type=F32)


def _dot_tn(a, b):
    return lax.dot_general(a, b, (((0,), (0,)), ((), ())), preferred_element_type=F32)


def _rowsum8(v):
    rows, w = v.shape
    return jnp.sum(v.reshape(rows // 8, 8, w), axis=0)


def _sigmoid(a):
    return 1.0 / (1.0 + jnp.exp(-a))


def _rms_fwd(x, width=None):
    width = x.shape[-1] if width is None else width
    r = lax.rsqrt(jnp.sum(x * x, axis=-1, keepdims=True) * (1.0 / width) + EPS)
    return x * r, r


def _rms_bwd(dy, xhat, r, gain, width=None):
    width = xhat.shape[-1] if width is None else width
    t = dy * gain
    return r * (t - xhat * (jnp.sum(t * xhat, axis=-1, keepdims=True) * (1.0 / width)))


def _full(shape):
    return pl.BlockSpec(shape, lambda *_: (0,) * len(shape))


def _load_weights(srcs, dsts, sems):
    copies = [pltpu.make_async_copy(s, d, sems.at[i]) for i, (s, d) in enumerate(zip(srcs, dsts))]
    for cp in copies:
        cp.start()
    for cp in copies:
        cp.wait()


def _ffn_fwd(x, gain, wg_t, wu_t, wd, name):
    n = x.shape[0]
    tm = _tiles(n)["ffn_fwd"]

    def body(x_ref, g_ref, wg_hbm, wu_hbm, wd_hbm, out_ref, a_ref, u_ref, wg_v, wu_v, wd_v, sems):
        @pl.when(pl.program_id(0) == 0)
        def _():
            _load_weights((wg_hbm, wu_hbm, wd_hbm), (wg_v, wu_v, wd_v), sems)

        xt = x_ref[...]
        xhat, _ = _rms_fwd(xt)
        h = (xhat * g_ref[...]).astype(BF16)
        acc = jnp.zeros((tm, D_MODEL), F32)
        for c in range(D_FF // FF_CHUNK):
            sl = pl.ds(c * FF_CHUNK, FF_CHUNK)
            a = _dot_nt(h, wg_v[sl, :])
            u = _dot_nt(h, wu_v[sl, :])
            a_ref[:, sl] = a.astype(BF16)
            u_ref[:, sl] = u.astype(BF16)
            y = (a * _sigmoid(a) * u).astype(BF16)
            acc = acc + _dot(y, wd_v[sl, :])
        out_ref[...] = xt + FFN_HALF * acc

    any_spec = pl.BlockSpec(memory_space=pl.ANY)
    return pl.pallas_call(
        body, name=name, grid=(n // tm,),
        in_specs=[pl.BlockSpec((tm, D_MODEL), lambda i: (i, 0)), _full((1, D_MODEL)), any_spec, any_spec, any_spec],
        out_specs=[pl.BlockSpec((tm, D_MODEL), lambda i: (i, 0)), pl.BlockSpec((tm, D_FF), lambda i: (i, 0)),
                   pl.BlockSpec((tm, D_FF), lambda i: (i, 0))],
        out_shape=[jax.ShapeDtypeStruct((n, D_MODEL), F32), jax.ShapeDtypeStruct((n, D_FF), BF16),
                   jax.ShapeDtypeStruct((n, D_FF), BF16)],
        scratch_shapes=[pltpu.VMEM((D_FF, D_MODEL), BF16)] * 3 + [pltpu.SemaphoreType.DMA((3,))],
        compiler_params=_params("arbitrary"),
    )(x, gain, wg_t, wu_t, wd)


def _ffn_bwd(x, dout, a, u, gain, wg_t, wu_t, wd, name):
    n = x.shape[0]
    tm = _tiles(n)["ffn_bwd"]

    def body(x_ref, do_ref, a_ref, u_ref, g_ref, wg_hbm, wu_hbm, wd_hbm,
             dx_ref, h_ref, dob_ref, y_ref, da_ref, du_ref, dg_ref, wg_v, wu_v, wd_v, sems):
        @pl.when(pl.program_id(0) == 0)
        def _():
            _load_weights((wg_hbm, wu_hbm, wd_hbm), (wg_v, wu_v, wd_v), sems)
            dg_ref[...] = jnp.zeros_like(dg_ref)

        xt = x_ref[...]
        g = g_ref[...]
        xhat, r = _rms_fwd(xt)
        h_ref[...] = (xhat * g).astype(BF16)
        dout = do_ref[...]
        dob = (FFN_HALF * dout).astype(BF16)
        dob_ref[...] = dob
        dh = jnp.zeros((tm, D_MODEL), F32)
        for c in range(D_FF // FF_CHUNK):
            sl = pl.ds(c * FF_CHUNK, FF_CHUNK)
            dy = _dot_nt(dob, wd_v[sl, :])
            av = a_ref[:, sl].astype(F32)
            uv = u_ref[:, sl].astype(F32)
            s = _sigmoid(av)
            silu = av * s
            y_ref[:, sl] = (silu * uv).astype(BF16)
            du = (dy * silu).astype(BF16)
            da = (dy * uv * (s * (1.0 + av * (1.0 - s)))).astype(BF16)
            du_ref[:, sl] = du
            da_ref[:, sl] = da
            dh = dh + _dot(da, wg_v[sl, :]) + _dot(du, wu_v[sl, :])
        dg_ref[...] += _rowsum8(dh * xhat)
        dx_ref[...] = dout + _rms_bwd(dh, xhat, r, g)

    any_spec = pl.BlockSpec(memory_space=pl.ANY)
    row_d = pl.BlockSpec((tm, D_MODEL), lambda i: (i, 0))
    row_f = pl.BlockSpec((tm, D_FF), lambda i: (i, 0))
    return pl.pallas_call(
        body, name=name, grid=(n // tm,),
        in_specs=[row_d, row_d, row_f, row_f, _full((1, D_MODEL)), any_spec, any_spec, any_spec],
        out_specs=[row_d, row_d, row_d, row_f, row_f, row_f, _full((8, D_MODEL))],
        out_shape=[jax.ShapeDtypeStruct((n, D_MODEL), F32), jax.ShapeDtypeStruct((n, D_MODEL), BF16),
                   jax.ShapeDtypeStruct((n, D_MODEL), BF16), jax.ShapeDtypeStruct((n, D_FF), BF16),
                   jax.ShapeDtypeStruct((n, D_FF), BF16), jax.ShapeDtypeStruct((n, D_FF), BF16),
                   jax.ShapeDtypeStruct((8, D_MODEL), F32)],
        scratch_shapes=[pltpu.VMEM((D_FF, D_MODEL), BF16)] * 3 + [pltpu.SemaphoreType.DMA((3,))],
        compiler_params=_params("arbitrary"),
    )(x, dout, a, u, gain, wg_t, wu_t, wd)


def _tn_matmul(a, b, name):
    n, fa = a.shape
    db = b.shape[1]
    tk = _tiles(n)["tn"]
    tf = fa // 2 if (fa // 2) % 128 == 0 and fa > 1024 else fa

    def body(a_ref, b_ref, o_ref):
        @pl.when(pl.program_id(1) == 0)
        def _():
            o_ref[...] = jnp.zeros_like(o_ref)

        o_ref[...] += _dot_tn(a_ref[...], b_ref[...])

    return pl.pallas_call(
        body, name=name, grid=(fa // tf, n // tk),
        in_specs=[pl.BlockSpec((tk, tf), lambda i, k: (k, i)), pl.BlockSpec((tk, db), lambda i, k: (k, 0))],
        out_specs=pl.BlockSpec((tf, db), lambda i, k: (i, 0)),
        out_shape=jax.ShapeDtypeStruct((fa, db), F32),
        compiler_params=_params("arbitrary", "arbitrary"),
    )(a, b)


def _loss_head(y, target, name):
    n = y.shape[0]
    tm = _tiles(n)["rows"]

    def body(y_ref, t_ref, d_ref, acc_ref):
        @pl.when(pl.program_id(0) == 0)
        def _():
            acc_ref[...] = jnp.zeros_like(acc_ref)

        d = y_ref[...] - t_ref[...]
        d_ref[...] = d * (1.0 / D_MODEL)
        acc_ref[...] += _rowsum8(d * d)

    row = pl.BlockSpec((tm, D_MODEL), lambda i: (i, 0))
    return pl.pallas_call(
        body, name=name, grid=(n // tm,), in_specs=[row, row], out_specs=[row, _full((8, D_MODEL))],
        out_shape=[jax.ShapeDtypeStruct((n, D_MODEL), F32), jax.ShapeDtypeStruct((8, D_MODEL), F32)],
        compiler_params=_params("arbitrary"),
    )(y, target)


def _window_sum(v, w, rows, forward):
    s = v
    sh = 1
    while sh < w:
        s = s + pltpu.roll(s, (rows - sh) if forward else sh, 0)
        sh *= 2
    return s


def _pool_fwd(x, gain, w, scale, name):
    n = x.shape[0]
    tm = _tiles(n)["pool"]
    hb = tm // POOL_HALO
    rows = tm + POOL_HALO

    def body(x_ref, xh_ref, g_ref, w_ref, sc_ref, out_ref):
        i = pl.program_id(0)
        xt = x_ref[...]
        e = jnp.concatenate([xh_ref[...], xt], axis=0)
        xhat, _ = _rms_fwd(e)
        row = lax.broadcasted_iota(jnp.int32, (rows, 1), 0)
        hn = jnp.where((row >= POOL_HALO) | (i > 0), xhat * g_ref[...], 0.0)
        t_glob = i * tm + row - POOL_HALO
        outs = []
        for gi, win in enumerate(POOL_WINDOWS):
            ug = hn[:, gi * POOL_GROUP:(gi + 1) * POOL_GROUP]
            cnt = jnp.maximum(jnp.minimum(t_glob + 1, win), 1).astype(F32)
            pooled = (_window_sum(ug, win, rows, False) / cnt - ug)[POOL_HALO:]
            outs.append(_dot(pooled.astype(BF16), w_ref[gi]))
        out_ref[...] = xt + jnp.concatenate(outs, axis=1) * sc_ref[...]

    return pl.pallas_call(
        body, name=name, grid=(n // tm,),
        in_specs=[pl.BlockSpec((tm, D_MODEL), lambda i: (i, 0)),
                  pl.BlockSpec((POOL_HALO, D_MODEL), lambda i: (jnp.maximum(i * hb - 1, 0), 0)),
                  _full((1, D_MODEL)), _full((4, POOL_GROUP, POOL_GROUP)), _full((1, D_MODEL))],
        out_specs=pl.BlockSpec((tm, D_MODEL), lambda i: (i, 0)),
        out_shape=jax.ShapeDtypeStruct((n, D_MODEL), F32),
        compiler_params=_params("arbitrary"),
    )(x, x, gain, w, scale)


def _pool_bwd(x, dout, gain, w, scale, name):
    n = x.shape[0]
    tm = _tiles(n)["pool"]
    hb = tm // POOL_HALO
    rows = tm + POOL_HALO
    nt = n // tm

    def body(x_ref, xh_ref, do_ref, doh_ref, g_ref, w_ref, sc_ref, dx_ref, dw_ref, dsc_ref, dg_ref):
        i = pl.program_id(0)

        @pl.when(i == 0)
        def _():
            dw_ref[...] = jnp.zeros_like(dw_ref)
            dsc_ref[...] = jnp.zeros_like(dsc_ref)
            dg_ref[...] = jnp.zeros_like(dg_ref)

        xt = x_ref[...]
        g = g_ref[...]
        e = jnp.concatenate([xh_ref[...], xt], axis=0)
        xhat_e, r_e = _rms_fwd(e)
        row = lax.broadcasted_iota(jnp.int32, (rows, 1), 0)
        hn = jnp.where((row >= POOL_HALO) | (i > 0), xhat_e * g, 0.0)
        t_prev = i * tm + row - POOL_HALO
        t_next = i * tm + row
        dout = do_ref[...]
        dt = jnp.concatenate([dout, doh_ref[...]], axis=0)
        dt = jnp.where((row < tm) | (i < nt - 1), dt, 0.0)
        dyr = dt * sc_ref[...]
        dus, dscs = [], []
        for gi, win in enumerate(POOL_WINDOWS):
            lanes = slice(gi * POOL_GROUP, (gi + 1) * POOL_GROUP)
            ug = hn[:, lanes]
            cnt = jnp.maximum(jnp.minimum(t_prev + 1, win), 1).astype(F32)
            pooled = (_window_sum(ug, win, rows, False) / cnt - ug)[POOL_HALO:].astype(BF16)
            yraw = _dot(pooled, w_ref[gi])
            dscs.append(_rowsum8(dout[:, lanes] * yraw))
            dyr_b = dyr[:, lanes].astype(BF16)
            dw_ref[gi] += _dot_tn(pooled, dyr_b[:tm])
            dpool = _dot_nt(dyr_b, w_ref[gi])
            cnt2 = jnp.minimum(t_next + 1, win).astype(F32)
            dus.append((_window_sum(dpool / cnt2, win, rows, True) - dpool)[:tm])
        dsc_ref[...] += jnp.concatenate(dscs, axis=1)
        dh = jnp.concatenate(dus, axis=1)
        xhat = xhat_e[POOL_HALO:]
        dg_ref[...] += _rowsum8(dh * xhat)
        dx_ref[...] = dout + _rms_bwd(dh, xhat, r_e[POOL_HALO:], g)

    row_d = pl.BlockSpec((tm, D_MODEL), lambda i: (i, 0))
    prev_h = pl.BlockSpec((POOL_HALO, D_MODEL), lambda i: (jnp.maximum(i * hb - 1, 0), 0))
    next_h = pl.BlockSpec((POOL_HALO, D_MODEL), lambda i: (jnp.minimum((i + 1) * hb, n // POOL_HALO - 1), 0))
    return pl.pallas_call(
        body, name=name, grid=(nt,),
        in_specs=[row_d, prev_h, row_d, next_h, _full((1, D_MODEL)), _full((4, POOL_GROUP, POOL_GROUP)),
                  _full((1, D_MODEL))],
        out_specs=[row_d, _full((4, POOL_GROUP, POOL_GROUP)), _full((8, D_MODEL)), _full((8, D_MODEL))],
        out_shape=[jax.ShapeDtypeStruct((n, D_MODEL), F32), jax.ShapeDtypeStruct((4, POOL_GROUP, POOL_GROUP), F32),
                   jax.ShapeDtypeStruct((8, D_MODEL), F32), jax.ShapeDtypeStruct((8, D_MODEL), F32)],
        compiler_params=_params("arbitrary"),
    )(x, x, dout, dout, gain, w, scale)


def _rope(v, cos, sin_signed):
    lo, hi = v[:, :128], v[:, 128:]
    lane = lax.broadcasted_iota(jnp.int32, hi.shape, 1)
    swapped = jnp.where(lane < 32, pltpu.roll(hi, 96, 1), pltpu.roll(hi, 32, 1))
    return jnp.concatenate([lo, hi * cos + swapped * sin_signed], axis=1)


def _rope_bwd(gr, cos, sin_signed):
    lo, hi = gr[:, :128], gr[:, 128:]
    t = hi * sin_signed
    lane = lax.broadcasted_iota(jnp.int32, hi.shape, 1)
    swapped = jnp.where(lane < 32, pltpu.roll(t, 96, 1), pltpu.roll(t, 32, 1))
    return jnp.concatenate([lo, hi * cos + swapped], axis=1)


def _mla_latents(h, win_ref):
    cq = _dot_nt(h, win_ref[0:Q_LORA, :])
    ckv = _dot_nt(h, win_ref[Q_LORA:Q_LORA + KV_LORA, :])
    kpe = _dot_nt(h, win_ref[Q_LORA + KV_LORA:LAT_PAD, :])
    return cq, ckv, kpe


def _mla_pre_fwd(x, gain, win, gq, gkv, wq, wkv, ghq, ghk, cos, sin_signed, name):
    n = x.shape[0]
    tm = _tiles(n)["mla_fwd"]

    def body(x_ref, g_ref, win_ref, gq_ref, gkv_ref, wq_ref, wkv_ref, ghq_ref, ghk_ref, c_ref, s_ref,
             q_ref, k_ref, v_ref):
        xhat, _ = _rms_fwd(x_ref[...])
        h = (xhat * g_ref[...]).astype(BF16)
        cq, ckv, kpe = _mla_latents(h, win_ref)
        cqn = (_rms_fwd(cq)[0] * gq_ref[...]).astype(BF16)
        ckvn = (_rms_fwd(ckv)[0] * gkv_ref[...]).astype(BF16)
        cos, sn = c_ref[...], s_ref[...]
        for hd in range(N_HEADS):
            rws = pl.ds(hd * HEAD_PAD, HEAD_PAD)
            qh = _dot_nt(cqn, wq_ref[rws, :])
            qn = _rms_fwd(qh, QK_HEAD)[0] * ghq_ref[...]
            q_ref[hd] = _rope(qn, cos, sn).astype(BF16)
            kvh = _dot_nt(ckvn, wkv_ref[rws, :])
            kpre = jnp.concatenate([kvh[:, :QK_NOPE], kpe], axis=1)
            kn = _rms_fwd(kpre, QK_HEAD)[0] * ghk_ref[...]
            k_ref[hd] = _rope(kn, cos, sn).astype(BF16)
            v_ref[hd] = kvh[:, QK_NOPE:].astype(BF16)

    row = lambda w: pl.BlockSpec((tm, w), lambda i: (i, 0))
    head = lambda w: pl.BlockSpec((N_HEADS, tm, w), lambda i: (0, i, 0))
    return pl.pallas_call(
        body, name=name, grid=(n // tm,),
        in_specs=[row(D_MODEL), _full((1, D_MODEL)), _full((LAT_PAD, D_MODEL)), _full((1, Q_LORA)),
                  _full((1, KV_LORA)), _full((N_HEADS * HEAD_PAD, Q_LORA)), _full((N_HEADS * HEAD_PAD, KV_LORA)),
                  _full((1, HEAD_PAD)), _full((1, HEAD_PAD)), row(128), row(128)],
        out_specs=[head(HEAD_PAD), head(HEAD_PAD), head(V_HEAD)],
        out_shape=[jax.ShapeDtypeStruct((N_HEADS, n, HEAD_PAD), BF16), jax.ShapeDtypeStruct((N_HEADS, n, HEAD_PAD), BF16),
                   jax.ShapeDtypeStruct((N_HEADS, n, V_HEAD), BF16)],
        compiler_params=_params("arbitrary"),
    )(x, gain, win, gq, gkv, wq, wkv, ghq, ghk, cos, sin_signed)


def _flash_fwd(q, k, v, name):
    n = q.shape[1]
    t = _tiles(n)["attn"]
    nq = n // t
    scale = QK_HEAD ** -0.5

    def body(q_ref, k_ref, v_ref, o_ref, lse_ref):
        i = pl.program_id(1)
        qi = q_ref[0]

        def step(j, carry, masked):
            m, l, acc = carry
            kj = k_ref[0, pl.ds(pl.multiple_of(j * t, t), t), :]
            vj = v_ref[0, pl.ds(pl.multiple_of(j * t, t), t), :]
            s = _dot_nt(kj, qi) * scale
            if masked:
                krow = lax.broadcasted_iota(jnp.int32, (t, t), 0)
                qcol = lax.broadcasted_iota(jnp.int32, (t, t), 1)
                s = jnp.where(krow <= qcol, s, -jnp.inf)
            m_new = jnp.maximum(m, jnp.max(s, axis=0, keepdims=True))
            alpha = jnp.exp(m - m_new)
            p = jnp.exp(s - m_new)
            l = alpha * l + jnp.sum(p, axis=0, keepdims=True)
            acc = alpha * acc + _dot_tn(vj, p.astype(BF16))
            return m_new, l, acc

        init = (jnp.full((1, t), -jnp.inf, F32), jnp.zeros((1, t), F32), jnp.zeros((V_HEAD, t), F32))
        carry = lax.fori_loop(0, i, lambda j, c: step(j, c, False), init)
        m, l, acc = step(i, carry, True)
        o_ref[...] = (acc / l).T.astype(BF16)
        lse_ref[0, 0] = m + jnp.log(l)

    return pl.pallas_call(
        body, name=name, grid=(N_HEADS, nq),
        in_specs=[pl.BlockSpec((1, t, HEAD_PAD), lambda h, i: (h, i, 0)),
                  pl.BlockSpec((1, n, HEAD_PAD), lambda h, i: (h, 0, 0)),
                  pl.BlockSpec((1, n, V_HEAD), lambda h, i: (h, 0, 0))],
        out_specs=[pl.BlockSpec((t, V_HEAD), lambda h, i: (i, h)),
                   pl.BlockSpec((1, 1, 1, t), lambda h, i: (h, i, 0, 0))],
        out_shape=[jax.ShapeDtypeStruct((n, N_HEADS * V_HEAD), BF16), jax.ShapeDtypeStruct((N_HEADS, nq, 1, t), F32)],
        compiler_params=_params("arbitrary", "arbitrary"),
    )(q, k, v)


def _mla_post_fwd(o, x, wout, name):
    n = x.shape[0]
    tm = _tiles(n)["rows"]

    def body(o_ref, x_ref, w_ref, out_ref):
        out_ref[...] = x_ref[...] + _dot(o_ref[...], w_ref[...])

    row = pl.BlockSpec((tm, D_MODEL), lambda i: (i, 0))
    return pl.pallas_call(
        body, name=name, grid=(n // tm,), in_specs=[row, row, _full((D_MODEL, D_MODEL))], out_specs=row,
        out_shape=jax.ShapeDtypeStruct((n, D_MODEL), F32), compiler_params=_params("arbitrary"),
    )(o, x, wout)


def _mla_out_bwd(dout, o, wout, name):
    n = dout.shape[0]
    t = _tiles(n)["attn"]
    nq = n // t

    def body(do_ref, o_ref, w_ref, dob_ref, dpo_ref, dl_ref):
        dob = do_ref[...].astype(BF16)
        dob_ref[...] = dob
        dpo = _dot_nt(dob, w_ref[...])
        dpo_ref[...] = dpo.astype(BF16)
        ov = o_ref[...].astype(F32)
        for hd in range(N_HEADS):
            lanes = slice(hd * V_HEAD, (hd + 1) * V_HEAD)
            prod = dpo[:, lanes] * ov[:, lanes]
            dl_ref[hd, 0] = jnp.sum(prod.T, axis=0, keepdims=True)

    row = pl.BlockSpec((t, D_MODEL), lambda i: (i, 0))
    return pl.pallas_call(
        body, name=name, grid=(nq,), in_specs=[row, row, _full((D_MODEL, D_MODEL))],
        out_specs=[row, row, pl.BlockSpec((N_HEADS, 1, 1, t), lambda i: (0, i, 0, 0))],
        out_shape=[jax.ShapeDtypeStruct((n, D_MODEL), BF16), jax.ShapeDtypeStruct((n, D_MODEL), BF16),
                   jax.ShapeDtypeStruct((N_HEADS, nq, 1, t), F32)],
        compiler_params=_params("arbitrary"),
    )(dout, o, wout)


def _flash_bwd(q, k, v, dpo, lse, delta, name):
    n = q.shape[1]
    t = _tiles(n)["attn"]
    nq = n // t
    scale = QK_HEAD ** -0.5

    def body(k_ref, v_ref, q_ref, do_ref, lse_ref, dl_ref, dk_ref, dv_ref, dq_hbm, dq_acc, sem):
        h = pl.program_id(0)
        j = pl.program_id(1)

        @pl.when(j == 0)
        def _():
            dq_acc[...] = jnp.zeros_like(dq_acc)

        kj = k_ref[0]
        vj = v_ref[0]

        def step(i, carry, masked):
            dk, dv = carry
            rws = pl.ds(pl.multiple_of(i * t, t), t)
            qi = q_ref[0, rws, :]
            doi = do_ref[rws, :]
            s = _dot_nt(kj, qi) * scale
            p = jnp.exp(s - lse_ref[0, i])
            if masked:
                krow = lax.broadcasted_iota(jnp.int32, (t, t), 0)
                qcol = lax.broadcasted_iota(jnp.int32, (t, t), 1)
                p = jnp.where(krow <= qcol, p, 0.0)
            dv = dv + _dot(p.astype(BF16), doi)
            dp = _dot_nt(vj, doi)
            ds = (p * (dp - dl_ref[0, i]) * scale).astype(BF16)
            dk = dk + _dot(ds, qi)
            dq_acc[rws, :] += _dot_tn(ds, kj)
            return dk, dv

        carry = step(j, (jnp.zeros((t, HEAD_PAD), F32), jnp.zeros((t, V_HEAD), F32)), True)
        dk, dv = lax.fori_loop(j + 1, nq, lambda i, c: step(i, c, False), carry)
        dk_ref[0] = dk
        dv_ref[0] = dv

        @pl.when(j == nq - 1)
        def _():
            cp = pltpu.make_async_copy(dq_acc, dq_hbm.at[h], sem)
            cp.start()
            cp.wait()

    return pl.pallas_call(
        body, name=name, grid=(N_HEADS, nq),
        in_specs=[pl.BlockSpec((1, t, HEAD_PAD), lambda h, j: (h, j, 0)),
                  pl.BlockSpec((1, t, V_HEAD), lambda h, j: (h, j, 0)),
                  pl.BlockSpec((1, n, HEAD_PAD), lambda h, j: (h, 0, 0)),
                  pl.BlockSpec((n, V_HEAD), lambda h, j: (0, h)),
                  pl.BlockSpec((1, nq, 1, t), lambda h, j: (h, 0, 0, 0)),
                  pl.BlockSpec((1, nq, 1, t), lambda h, j: (h, 0, 0, 0))],
        out_specs=[pl.BlockSpec((1, t, HEAD_PAD), lambda h, j: (h, j, 0)),
                   pl.BlockSpec((1, t, V_HEAD), lambda h, j: (h, j, 0)),
                   pl.BlockSpec(memory_space=pl.ANY)],
        out_shape=[jax.ShapeDtypeStruct((N_HEADS, n, HEAD_PAD), F32), jax.ShapeDtypeStruct((N_HEADS, n, V_HEAD), F32),
                   jax.ShapeDtypeStruct((N_HEADS, n, HEAD_PAD), F32)],
        scratch_shapes=[pltpu.VMEM((n, HEAD_PAD), F32), pltpu.SemaphoreType.DMA(())],
        compiler_params=_params("arbitrary", "arbitrary"),
    )(k, v, q, dpo, lse, delta)


def _mla_pre_bwd(x, dout, dq, dk, dv, gain, win, gq, gkv, wq, wkv, ghq, ghk, cos, sin_signed, name):
    n = x.shape[0]
    tm = _tiles(n)["mla_bwd"]
    hw = N_HEADS * HEAD_PAD

    def body(x_ref, do_ref, dq_ref, dk_ref, dv_ref, g_ref, win_ref, gq_ref, gkv_ref, wq_ref, wkv_ref, ghq_ref, ghk_ref,
             c_ref, s_ref, dx_ref, h_ref, dlat_ref, cqn_ref, dqp_ref, ckvn_ref, dkv_ref,
             dg_ref, dgq_ref, dgkv_ref, dghq_ref, dghk_ref):
        @pl.when(pl.program_id(0) == 0)
        def _():
            for ref in (dg_ref, dgq_ref, dgkv_ref, dghq_ref, dghk_ref):
                ref[...] = jnp.zeros_like(ref)

        g = g_ref[...]
        xhat, r = _rms_fwd(x_ref[...])
        h = (xhat * g).astype(BF16)
        h_ref[...] = h
        cq, ckv, kpe = _mla_latents(h, win_ref)
        cqhat, rcq = _rms_fwd(cq)
        ckvhat, rckv = _rms_fwd(ckv)
        cqn = (cqhat * gq_ref[...]).astype(BF16)
        ckvn = (ckvhat * gkv_ref[...]).astype(BF16)
        cqn_ref[...] = cqn
        ckvn_ref[...] = ckvn
        cos, sn = c_ref[...], s_ref[...]
        ghq, ghk = ghq_ref[...], ghk_ref[...]
        dcqn = jnp.zeros((tm, Q_LORA), F32)
        dckvn = jnp.zeros((tm, KV_LORA), F32)
        dkpe = jnp.zeros((tm, 128), F32)
        dghq = jnp.zeros((8, HEAD_PAD), F32)
        dghk = jnp.zeros((8, HEAD_PAD), F32)
        for hd in range(N_HEADS):
            rws = pl.ds(hd * HEAD_PAD, HEAD_PAD)
            lanes = slice(hd * HEAD_PAD, (hd + 1) * HEAD_PAD)
            qhat, rq = _rms_fwd(_dot_nt(cqn, wq_ref[rws, :]), QK_HEAD)
            gqn = _rope_bwd(dq_ref[hd], cos, sn)
            dghq = dghq + _rowsum8(gqn * qhat)
            dqpre = _rms_bwd(gqn, qhat, rq, ghq, QK_HEAD).astype(BF16)
            dqp_ref[:, lanes] = dqpre
            dcqn = dcqn + _dot(dqpre, wq_ref[rws, :])
            kvh = _dot_nt(ckvn, wkv_ref[rws, :])
            khat, rk = _rms_fwd(jnp.concatenate([kvh[:, :QK_NOPE], kpe], axis=1), QK_HEAD)
            gkn = _rope_bwd(dk_ref[hd], cos, sn)
            dghk = dghk + _rowsum8(gkn * khat)
            dkpre = _rms_bwd(gkn, khat, rk, ghk, QK_HEAD)
            dkpe = dkpe + dkpre[:, QK_NOPE:]
            dkvh = jnp.concatenate([dkpre[:, :QK_NOPE], dv_ref[hd]], axis=1).astype(BF16)
            dkv_ref[:, lanes] = dkvh
            dckvn = dckvn + _dot(dkvh, wkv_ref[rws, :])
        dghq_ref[...] += dghq
        dghk_ref[...] += dghk
        dgq_ref[...] += _rowsum8(dcqn * cqhat)
        dgkv_ref[...] += _rowsum8(dckvn * ckvhat)
        dlat = jnp.concatenate([_rms_bwd(dcqn, cqhat, rcq, gq_ref[...]), _rms_bwd(dckvn, ckvhat, rckv, gkv_ref[...]),
                                dkpe], axis=1).astype(BF16)
        dlat_ref[...] = dlat
        dh = _dot(dlat, win_ref[...])
        dg_ref[...] += _rowsum8(dh * xhat)
        dx_ref[...] = do_ref[...] + _rms_bwd(dh, xhat, r, g)

    row = lambda w: pl.BlockSpec((tm, w), lambda i: (i, 0))
    head = lambda w: pl.BlockSpec((N_HEADS, tm, w), lambda i: (0, i, 0))
    sds = jax.ShapeDtypeStruct
    return pl.pallas_call(
        body, name=name, grid=(n // tm,),
        in_specs=[row(D_MODEL), row(D_MODEL), head(HEAD_PAD), head(HEAD_PAD), head(V_HEAD), _full((1, D_MODEL)),
                  _full((LAT_PAD, D_MODEL)), _full((1, Q_LORA)), _full((1, KV_LORA)), _full((hw, Q_LORA)),
                  _full((hw, KV_LORA)), _full((1, HEAD_PAD)), _full((1, HEAD_PAD)), row(128), row(128)],
        out_specs=[row(D_MODEL), row(D_MODEL), row(LAT_PAD), row(Q_LORA), row(hw), row(KV_LORA), row(hw),
                   _full((8, D_MODEL)), _full((8, Q_LORA)), _full((8, KV_LORA)), _full((8, HEAD_PAD)),
                   _full((8, HEAD_PAD))],
        out_shape=[sds((n, D_MODEL), F32), sds((n, D_MODEL), BF16), sds((n, LAT_PAD), BF16), sds((n, Q_LORA), BF16),
                   sds((n, hw), BF16), sds((n, KV_LORA), BF16), sds((n, hw), BF16), sds((8, D_MODEL), F32),
                   sds((8, Q_LORA), F32), sds((8, KV_LORA), F32), sds((8, HEAD_PAD), F32), sds((8, HEAD_PAD), F32)],
        compiler_params=_params("arbitrary"),
    )(x, dout, dq, dk, dv, gain, win, gq, gkv, wq, wkv, ghq, ghk, cos, sin_signed)


def _adamw(w, g, m, v, name):
    rows, cols = w.shape
    tr = rows
    for cand in (512, 256, 128, 64, 32, 16, 8):
        if rows % cand == 0 and rows > cand:
            tr = cand
            break

    def body(w_ref, g_ref, m_ref, v_ref, d_ref, mo_ref, vo_ref):
        gv = g_ref[...]
        mn = ADAM_B1 * m_ref[...] + (1.0 - ADAM_B1) * gv
        vn = ADAM_B2 * v_ref[...] + (1.0 - ADAM_B2) * (gv * gv)
        m_hat = mn / (1.0 - ADAM_B1 ** ADAM_STEP)
        v_hat = vn / (1.0 - ADAM_B2 ** ADAM_STEP)
        d_ref[...] = -ADAM_LR * (m_hat / (jnp.sqrt(v_hat) + ADAM_EPS) + ADAM_WD * w_ref[...])
        mo_ref[...] = mn
        vo_ref[...] = vn

    blk = pl.BlockSpec((tr, cols), lambda i: (i, 0))
    return pl.pallas_call(
        body, name=name, grid=(rows // tr,), in_specs=[blk] * 4, out_specs=[blk] * 3,
        out_shape=[jax.ShapeDtypeStruct((rows, cols), F32)] * 3, compiler_params=_params("arbitrary"),
    )(w, g, m, v)


def _sum_parts(parts, name):
    k, r, c = parts.shape
    tr = min(r, 512)

    def body(p_ref, o_ref):
        acc = p_ref[0]
        for j in range(1, k):
            acc = acc + p_ref[j]
        o_ref[...] = acc

    return pl.pallas_call(
        body, name=name, grid=(r // tr,), in_specs=[pl.BlockSpec((k, tr, c), lambda i: (0, i, 0))],
        out_specs=pl.BlockSpec((tr, c), lambda i: (i, 0)), out_shape=jax.ShapeDtypeStruct((r, c), parts.dtype),
        compiler_params=_params("arbitrary"),
    )(parts)


def _add_pairs(grads, recv, core, name):
    _, r, c = grads.shape
    tr = 592 if r % 592 == 0 else r

    def body(core_ref, g_ref, r_ref, o_ref):
        o_ref[...] = g_ref[...] + r_ref[...]

    blk = pl.BlockSpec((1, tr, c), lambda k, i, core_ref: (k, i, 0))
    return pl.pallas_call(
        body, name=name,
        grid_spec=pltpu.PrefetchScalarGridSpec(
            num_scalar_prefetch=1, grid=(4, r // tr),
            in_specs=[pl.BlockSpec((1, tr, c), lambda k, i, core_ref: (2 * k + core_ref[0], i, 0)), blk],
            out_specs=blk),
        out_shape=jax.ShapeDtypeStruct((4, r, c), grads.dtype), compiler_params=_params("arbitrary", "arbitrary"),
    )(core, grads, recv)


def _add_chips(pairs, recv, chip, name):
    _, r, c = pairs.shape
    tr = 592 if r % 592 == 0 else r

    def body(chip_ref, p_ref, r_ref, o_ref):
        o_ref[...] = ((p_ref[0] + r_ref[0]) + r_ref[1]) + r_ref[2]

    return pl.pallas_call(
        body, name=name,
        grid_spec=pltpu.PrefetchScalarGridSpec(
            num_scalar_prefetch=1, grid=(r // tr,),
            in_specs=[pl.BlockSpec((1, tr, c), lambda i, chip_ref: (chip_ref[0], i, 0)),
                      pl.BlockSpec((3, tr, c), lambda i, chip_ref: (0, i, 0))],
            out_specs=pl.BlockSpec((tr, c), lambda i, chip_ref: (i, 0))),
        out_shape=jax.ShapeDtypeStruct((r, c), pairs.dtype), compiler_params=_params("arbitrary"),
    )(chip, pairs, recv)


MESH = pl.DeviceIdType.MESH


def _all_gather(x, name):
    r, c = x.shape

    def body(x_ref, out_ref, send_sems, recv_sems, local_sem):
        mx, my, mc = lax.axis_index("x"), lax.axis_index("y"), lax.axis_index("c")
        me, sibling = (mx, my, mc), (mx, my, 1 - mc)
        chips = [(1 - mx, my), (mx, 1 - my), (1 - mx, 1 - my)]

        def slot(px, py, pc):
            return out_ref.at[4 * px + 2 * py + pc]

        def copy(k, block, to, src=None):
            return pltpu.make_async_remote_copy(
                src_ref=slot(*block) if src is None else src, dst_ref=slot(*block),
                send_sem=send_sems.at[k], recv_sem=recv_sems.at[k], device_id=to, device_id_type=MESH)

        mine = pltpu.make_async_copy(x_ref, slot(*me), local_sem)
        mine.start()
        first = [copy(0, me, sibling, src=x_ref)]
        first += [copy(1 + j, me, (*chip, mc), src=x_ref) for j, chip in enumerate(chips)]
        for cp in first:
            cp.start()
        passed = [copy(4 + j, (*chip, mc), sibling) for j, chip in enumerate(chips)]
        for j, chip in enumerate(chips):
            copy(1 + j, (*chip, mc), me).wait_recv()
            passed[j].start()
        copy(0, sibling, me).wait_recv()
        for j, chip in enumerate(chips):
            copy(4 + j, (*chip, 1 - mc), me).wait_recv()
        for cp in first + passed:
            cp.wait_send()
        mine.wait()

    any_spec = pl.BlockSpec(memory_space=pl.ANY)
    return pl.pallas_call(
        body, name=name, in_specs=[any_spec], out_specs=any_spec,
        out_shape=jax.ShapeDtypeStruct((N_DEV, r, c), x.dtype),
        scratch_shapes=[pltpu.SemaphoreType.DMA((7,)), pltpu.SemaphoreType.DMA((7,)), pltpu.SemaphoreType.DMA(())],
    )(x)


def _exchange_sibling(grads, name):
    _, r, c = grads.shape

    def body(g_ref, recv_ref, send_sems, recv_sems):
        mx, my, mc = lax.axis_index("x"), lax.axis_index("y"), lax.axis_index("c")
        copies = [pltpu.make_async_remote_copy(
            src_ref=g_ref.at[2 * k + (1 - mc)], dst_ref=recv_ref.at[k], send_sem=send_sems.at[k],
            recv_sem=recv_sems.at[k], device_id=(mx, my, 1 - mc), device_id_type=MESH) for k in range(4)]
        for cp in copies:
            cp.start()
        for cp in copies:
            cp.wait()

    any_spec = pl.BlockSpec(memory_space=pl.ANY)
    return pl.pallas_call(
        body, name=name, in_specs=[any_spec], out_specs=any_spec,
        out_shape=jax.ShapeDtypeStruct((4, r, c), grads.dtype),
        scratch_shapes=[pltpu.SemaphoreType.DMA((4,)), pltpu.SemaphoreType.DMA((4,))],
    )(grads)


def _exchange_chips(pairs, name):
    _, r, c = pairs.shape

    def body(p_ref, recv_ref, send_sems, recv_sems):
        mx, my, mc = lax.axis_index("x"), lax.axis_index("y"), lax.axis_index("c")
        chips = [(1 - mx, my), (mx, 1 - my), (1 - mx, 1 - my)]
        copies = [pltpu.make_async_remote_copy(
            src_ref=p_ref.at[2 * cx + cy], dst_ref=recv_ref.at[j], send_sem=send_sems.at[j],
            recv_sem=recv_sems.at[j], device_id=(cx, cy, mc), device_id_type=MESH)
            for j, (cx, cy) in enumerate(chips)]
        for cp in copies:
            cp.start()
        for cp in copies:
            cp.wait()

    any_spec = pl.BlockSpec(memory_space=pl.ANY)
    return pl.pallas_call(
        body, name=name, in_specs=[any_spec], out_specs=any_spec,
        out_shape=jax.ShapeDtypeStruct((3, r, c), pairs.dtype),
        scratch_shapes=[pltpu.SemaphoreType.DMA((3,)), pltpu.SemaphoreType.DMA((3,))],
    )(pairs)


def _reduce_scatter(grads):
    mx, my, mc = lax.axis_index("x"), lax.axis_index("y"), lax.axis_index("c")
    core = jnp.reshape(mc, (1,)).astype(jnp.int32)
    chip = jnp.reshape(2 * mx + my, (1,)).astype(jnp.int32)
    from_sibling = _exchange_sibling(grads, "rs_exchange_sibling")
    pairs = _add_pairs(grads, from_sibling, core, "rs_add_pairs")
    from_chips = _exchange_chips(pairs, "rs_exchange_chips")
    return _add_chips(pairs, from_chips, chip, "rs_add_chips")


FFN_ROWS = D_FF // N_DEV
WIN_ROWS = (Q_LORA + KV_LORA + QK_ROPE) // N_DEV
WIN_ROWS_PAD = 144
WQ_ROWS = QK_HEAD * Q_LORA // D_MODEL
WKV_ROWS = 256 * KV_LORA // D_MODEL
WOUT_ROWS = V_HEAD
POOL_ROWS = 4 * 32 * POOL_GROUP // D_MODEL


def _t(w):
    return jnp.swapaxes(w, -1, -2)


def _pack_shards(p, dtype):
    segs = []
    for i in range(DEPTH):
        for f in ("ffn1", "ffn2"):
            segs += [_t(p[f + "_w_gate"][i]), _t(p[f + "_w_up"][i]), p[f + "_w_down"][i]]
    for j in range(DEPTH // 2):
        segs += [jnp.pad(_t(p["mla_w_in"][j]), ((0, WIN_ROWS_PAD - WIN_ROWS), (0, 0))),
                 _t(p["mla_w_q_up"][j]).reshape(WQ_ROWS, D_MODEL),
                 _t(p["mla_w_kv_up"][j]).reshape(WKV_ROWS, D_MODEL),
                 p["mla_w_out"][j]]
    for j in range(DEPTH // 2):
        segs.append(p["pool_w"][j].reshape(POOL_ROWS, D_MODEL))
    return jnp.concatenate([s.astype(dtype) for s in segs], axis=0)


def _segments():
    out = []
    for i in range(DEPTH):
        for f in ("ffn1", "ffn2"):
            out += [(f + "_w_gate", i, FFN_ROWS), (f + "_w_up", i, FFN_ROWS), (f + "_w_down", i, FFN_ROWS)]
    for j in range(DEPTH // 2):
        out += [("mla_w_in", j, WIN_ROWS_PAD), ("mla_w_q_up", j, WQ_ROWS), ("mla_w_kv_up", j, WKV_ROWS),
                ("mla_w_out", j, WOUT_ROWS)]
    for j in range(DEPTH // 2):
        out.append(("pool_w", j, POOL_ROWS))
    return out


def _unpack_gathered(g):
    out = {}
    off = 0
    for name, layer, rows in _segments():
        seg = g[:, off:off + rows, :]
        off += rows
        if name == "mla_w_in":
            w = seg[:, :WIN_ROWS].reshape(N_DEV * WIN_ROWS, D_MODEL)
            w = jnp.pad(w, ((0, LAT_PAD - N_DEV * WIN_ROWS), (0, 0)))
        elif name == "mla_w_q_up":
            w = seg.reshape(N_HEADS, QK_HEAD, Q_LORA)
            w = jnp.pad(w, ((0, 0), (0, HEAD_PAD - QK_HEAD), (0, 0))).reshape(N_HEADS * HEAD_PAD, Q_LORA)
        elif name == "mla_w_kv_up":
            w = seg.reshape(N_HEADS * 256, KV_LORA)
        elif name == "pool_w":
            w = seg.reshape(N_DEV, 4, 32, POOL_GROUP).transpose(1, 0, 2, 3).reshape(4, POOL_GROUP, POOL_GROUP)
        else:
            w = seg.reshape(N_DEV * rows, D_MODEL)
        out[(name, layer)] = w
    return out


def _pack_grads(gr):
    segs = []
    for name, layer, rows in _segments():
        g = gr[(name, layer)]
        if name == "mla_w_in":
            g = g[:N_DEV * WIN_ROWS].reshape(N_DEV, WIN_ROWS, D_MODEL)
            g = jnp.pad(g, ((0, 0), (0, WIN_ROWS_PAD - WIN_ROWS), (0, 0)))
        elif name == "mla_w_q_up":
            g = g.reshape(N_HEADS, HEAD_PAD, Q_LORA)[:, :QK_HEAD].reshape(N_DEV, WQ_ROWS, D_MODEL)
        elif name == "mla_w_kv_up":
            g = g.reshape(N_DEV, WKV_ROWS, D_MODEL)
        elif name == "pool_w":
            g = g.reshape(4, N_DEV, 32, POOL_GROUP).transpose(1, 0, 2, 3).reshape(N_DEV, POOL_ROWS, D_MODEL)
        else:
            g = g.reshape(N_DEV, rows, D_MODEL)
        segs.append(g)
    return jnp.concatenate(segs, axis=1)


def _unpack_shard_grads(flat):
    per = {}
    off = 0
    for name, layer, rows in _segments():
        seg = flat[off:off + rows]
        off += rows
        if name.endswith("w_gate") or name.endswith("w_up"):
            g = _t(seg)
        elif name == "mla_w_in":
            g = _t(seg[:WIN_ROWS])
        elif name == "mla_w_q_up":
            g = _t(seg.reshape(QK_HEAD, Q_LORA))
        elif name == "mla_w_kv_up":
            g = _t(seg.reshape(256, KV_LORA))
        elif name == "pool_w":
            g = seg.reshape(4, 32, POOL_GROUP)
        else:
            g = seg
        per.setdefault(name, []).append(g)
    return {k: jnp.stack(v) for k, v in per.items()}


def _pad_lanes(v, width):
    return jnp.pad(v, ((0, 0), (0, width - v.shape[-1])))


def kernel(x, positions, ffn1_norm, ffn1_w_gate, ffn1_w_up, ffn1_w_down, mix_norm, pool_w, pool_scale, mla_w_in, mla_q_norm, mla_w_q_up, mla_kv_norm, mla_w_kv_up, mla_q_head_norm, mla_k_head_norm, mla_w_out, ffn2_norm, ffn2_w_gate, ffn2_w_up, ffn2_w_down, loss_target, m_ffn1_norm, m_ffn1_w_gate, m_ffn1_w_up, m_ffn1_w_down, m_mix_norm, m_pool_w, m_pool_scale, m_mla_w_in, m_mla_q_norm, m_mla_w_q_up, m_mla_kv_norm, m_mla_w_kv_up, m_mla_q_head_norm, m_mla_k_head_norm, m_mla_w_out, m_ffn2_norm, m_ffn2_w_gate, m_ffn2_w_up, m_ffn2_w_down, v_ffn1_norm, v_ffn1_w_gate, v_ffn1_w_up, v_ffn1_w_down, v_mix_norm, v_pool_w, v_pool_scale, v_mla_w_in, v_mla_q_norm, v_mla_w_q_up, v_mla_kv_norm, v_mla_w_kv_up, v_mla_q_head_norm, v_mla_k_head_norm, v_mla_w_out, v_ffn2_norm, v_ffn2_w_gate, v_ffn2_w_up, v_ffn2_w_down):
    args = (x, positions, ffn1_norm, ffn1_w_gate, ffn1_w_up, ffn1_w_down, mix_norm, pool_w, pool_scale, mla_w_in,
            mla_q_norm, mla_w_q_up, mla_kv_norm, mla_w_kv_up, mla_q_head_norm, mla_k_head_norm, mla_w_out, ffn2_norm,
            ffn2_w_gate, ffn2_w_up, ffn2_w_down)
    p = dict(zip(NAMES, args))
    moments_m = dict(zip(WEIGHTS, (m_ffn1_norm, m_ffn1_w_gate, m_ffn1_w_up, m_ffn1_w_down, m_mix_norm, m_pool_w, m_pool_scale, m_mla_w_in, m_mla_q_norm, m_mla_w_q_up, m_mla_kv_norm, m_mla_w_kv_up, m_mla_q_head_norm, m_mla_k_head_norm, m_mla_w_out, m_ffn2_norm, m_ffn2_w_gate, m_ffn2_w_up, m_ffn2_w_down)))
    moments_v = dict(zip(WEIGHTS, (v_ffn1_norm, v_ffn1_w_gate, v_ffn1_w_up, v_ffn1_w_down, v_mix_norm, v_pool_w, v_pool_scale, v_mla_w_in, v_mla_q_norm, v_mla_w_q_up, v_mla_kv_norm, v_mla_w_kv_up, v_mla_q_head_norm, v_mla_k_head_norm, v_mla_w_out, v_ffn2_norm, v_ffn2_w_gate, v_ffn2_w_up, v_ffn2_w_down)))
    dev = 4 * lax.axis_index("x") + 2 * lax.axis_index("y") + lax.axis_index("c")

    xs = x[0]
    n = xs.shape[0]
    target = loss_target[0]

    inv_freq = 1.0 / (ROPE_THETA ** (jnp.arange(0, QK_ROPE, 2, dtype=F32) / QK_ROPE))
    ang = positions[0].astype(F32)[..., None] * inv_freq
    cos, sin = jnp.cos(ang), jnp.sin(ang)
    zero = jnp.zeros((n, 128 - QK_ROPE), F32)
    rope_cos = jnp.concatenate([cos, cos, zero], axis=1)
    rope_sin = jnp.concatenate([-sin, sin, zero], axis=1)

    w = _unpack_gathered(_all_gather(_pack_shards(p, BF16), "weights_all_gather"))

    gains_local = jnp.concatenate([_pad_lanes(mla_q_norm, 128), _pad_lanes(mla_kv_norm, 128)], axis=0)
    gains_all = _all_gather(jnp.pad(gains_local, ((0, 4), (0, 0))), "gains_all_gather")
    q_norm_full = gains_all[:, 0:2, :Q_LORA // N_DEV].transpose(1, 0, 2).reshape(2, Q_LORA)
    kv_norm_full = gains_all[:, 2:4, :KV_LORA // N_DEV].transpose(1, 0, 2).reshape(2, KV_LORA)
    ghq = _pad_lanes(mla_q_head_norm, HEAD_PAD)
    ghk = _pad_lanes(mla_k_head_norm, HEAD_PAD)

    def mla_weights(j):
        return (w[("mla_w_in", j)], q_norm_full[j:j + 1], kv_norm_full[j:j + 1], w[("mla_w_q_up", j)],
                w[("mla_w_kv_up", j)], ghq[j:j + 1], ghk[j:j + 1], rope_cos, rope_sin)

    saved = []
    cur = xs
    for i in range(DEPTH):
        j = i // 2
        st = {"x0": cur}
        cur, st["a1"], st["u1"] = _ffn_fwd(cur, ffn1_norm[i:i + 1], w[("ffn1_w_gate", i)], w[("ffn1_w_up", i)],
                                           w[("ffn1_w_down", i)], f"ffn1_fwd_{i}")
        st["x1"] = cur
        if i % 2 == 0:
            cur = _pool_fwd(cur, mix_norm[i:i + 1], w[("pool_w", j)], pool_scale[j:j + 1], f"pool_fwd_{i}")
        else:
            st["q"], st["k"], st["v"] = _mla_pre_fwd(cur, mix_norm[i:i + 1], *mla_weights(j), f"mla_pre_fwd_{i}")
            st["o"], st["lse"] = _flash_fwd(st["q"], st["k"], st["v"], f"flash_fwd_{i}")
            cur = _mla_post_fwd(st["o"], cur, w[("mla_w_out", j)], f"mla_post_fwd_{i}")
        st["x2"] = cur
        cur, st["a2"], st["u2"] = _ffn_fwd(cur, ffn2_norm[i:i + 1], w[("ffn2_w_gate", i)], w[("ffn2_w_up", i)],
                                           w[("ffn2_w_down", i)], f"ffn2_fwd_{i}")
        saved.append(st)

    dcur, sq_err = _loss_head(cur, target, "loss_head")
    loss_part = 0.5 * jnp.sum(sq_err) * (1.0 / D_MODEL)

    gr = {}
    small = {k: [None] * DEPTH for k in ("ffn1_norm", "mix_norm", "ffn2_norm")}
    small.update({k: [None] * (DEPTH // 2) for k in ("pool_scale", "mla_q_norm", "mla_kv_norm", "mla_q_head_norm",
                                                     "mla_k_head_norm")})

    def ffn_backward(f, i, x_in, a, u, gain, dout):
        dx, h, dob, y, da, du, dg = _ffn_bwd(x_in, dout, a, u, gain, w[(f + "_w_gate", i)], w[(f + "_w_up", i)],
                                             w[(f + "_w_down", i)], f"{f}_bwd_{i}")
        gr[(f + "_w_gate", i)] = _tn_matmul(da, h, f"{f}_dgate_{i}")
        gr[(f + "_w_up", i)] = _tn_matmul(du, h, f"{f}_dup_{i}")
        gr[(f + "_w_down", i)] = _tn_matmul(y, dob, f"{f}_ddown_{i}")
        small[f + "_norm"][i] = jnp.sum(dg, axis=0)
        return dx

    for i in reversed(range(DEPTH)):
        j = i // 2
        st = saved[i]
        dcur = ffn_backward("ffn2", i, st["x2"], st["a2"], st["u2"], ffn2_norm[i:i + 1], dcur)
        if i % 2 == 0:
            dcur, dpw, dsc, dg = _pool_bwd(st["x1"], dcur, mix_norm[i:i + 1], w[("pool_w", j)], pool_scale[j:j + 1],
                                           f"pool_bwd_{i}")
            gr[("pool_w", j)] = dpw
            small["pool_scale"][j] = jnp.sum(dsc, axis=0)
            small["mix_norm"][i] = jnp.sum(dg, axis=0)
        else:
            dob, dpo, delta = _mla_out_bwd(dcur, st["o"], w[("mla_w_out", j)], f"mla_out_bwd_{i}")
            gr[("mla_w_out", j)] = _tn_matmul(st["o"], dob, f"mla_dout_{i}")
            dk, dv, dq = _flash_bwd(st["q"], st["k"], st["v"], dpo, st["lse"], delta, f"flash_bwd_{i}")
            (dcur, h, dlat, cqn, dqp, ckvn, dkv, dg, dgq, dgkv, dghq, dghk) = _mla_pre_bwd(
                st["x1"], dcur, dq, dk, dv, mix_norm[i:i + 1], *mla_weights(j), f"mla_pre_bwd_{i}")
            gr[("mla_w_in", j)] = _tn_matmul(dlat, h, f"mla_din_{i}")
            gr[("mla_w_q_up", j)] = _tn_matmul(dqp, cqn, f"mla_dqup_{i}")
            gr[("mla_w_kv_up", j)] = _tn_matmul(dkv, ckvn, f"mla_dkvup_{i}")
            small["mix_norm"][i] = jnp.sum(dg, axis=0)
            small["mla_q_norm"][j] = _pad_lanes(jnp.sum(dgq, axis=0)[None], D_MODEL)[0]
            small["mla_kv_norm"][j] = _pad_lanes(jnp.sum(dgkv, axis=0)[None], D_MODEL)[0]
            small["mla_q_head_norm"][j] = _pad_lanes(jnp.sum(dghq, axis=0)[None], D_MODEL)[0]
            small["mla_k_head_norm"][j] = _pad_lanes(jnp.sum(dghk, axis=0)[None], D_MODEL)[0]
        dcur = ffn_backward("ffn1", i, st["x0"], st["a1"], st["u1"], ffn1_norm[i:i + 1], dcur)
    grad_x = dcur[None]

    flat = _reduce_scatter(_pack_grads(gr))
    grads = _unpack_shard_grads(flat)

    small_order = ("ffn1_norm", "mix_norm", "ffn2_norm", "pool_scale", "mla_q_norm", "mla_kv_norm",
                   "mla_q_head_norm", "mla_k_head_norm")
    rows = [r for k in small_order for r in small[k]]
    rows.append(jnp.zeros((D_MODEL,), F32).at[0].set(loss_part))
    rows.append(jnp.zeros((D_MODEL,), F32))
    small_sum = _sum_parts(_all_gather(jnp.stack(rows), "small_all_gather"), "small_sum")
    loss = small_sum[SM_ROWS - 2, 0]
    off = 0
    for k in small_order:
        cnt = len(small[k])
        g = small_sum[off:off + cnt]
        off += cnt
        if k == "mla_q_norm":
            g = lax.dynamic_slice_in_dim(g[:, :Q_LORA], dev * (Q_LORA // N_DEV), Q_LORA // N_DEV, axis=1)
        elif k == "mla_kv_norm":
            g = lax.dynamic_slice_in_dim(g[:, :KV_LORA], dev * (KV_LORA // N_DEV), KV_LORA // N_DEV, axis=1)
        elif k in ("mla_q_head_norm", "mla_k_head_norm"):
            g = g[:, :QK_HEAD]
        grads[k] = g

    deltas, new_m, new_v = {}, {}, {}
    for k in WEIGHTS:
        shape = p[k].shape
        view = (-1, shape[-1])
        d, mn, vn = _adamw(p[k].reshape(view), grads[k].reshape(view), moments_m[k].reshape(view),
                           moments_v[k].reshape(view), "adamw_" + k)
        deltas[k], new_m[k], new_v[k] = d.reshape(shape), mn.reshape(shape), vn.reshape(shape)
        grads[k] = grads[k].reshape(shape)

    return (loss, grad_x, *[grads[k] for k in WEIGHTS], *[deltas[k] for k in WEIGHTS],
            *[new_m[k] for k in WEIGHTS], *[new_v[k] for k in WEIGHTS])
```

```python
import functools

import jax
import jax.numpy as jnp
from jax import lax
from jax.experimental import pallas as pl
from jax.experimental.pallas import tpu as pltpu

F32 = jnp.float32
BF16 = jnp.bfloat16

D_MODEL = 1024
DEPTH = 4
D_FF = 2816
POOL_WINDOWS = (2, 4, 8, 16)
POOL_GROUP = 256
POOL_HALO = 16
N_HEADS = 8
QK_NOPE = 128
QK_ROPE = 64
QK_HEAD = 192
V_HEAD = 128
Q_LORA = 768
KV_LORA = 256
ROPE_THETA = 10000.0
EPS = 1e-6
FFN_HALF = 0.5
ADAM_LR = 0.001
ADAM_B1 = 0.9
ADAM_B2 = 0.999
ADAM_EPS = 1e-08
ADAM_WD = 0.01
ADAM_STEP = 10

N_DEV = 8
HEAD_PAD = 256
LAT_PAD = 1152
VT_ROWS = 144
LOG2_E = 1.4426950408889634
ATTN_SCALE = QK_HEAD ** -0.5
LOGIT_SCALE = ATTN_SCALE * LOG2_E
V7X_VMEM_LIMIT = 56 * 1024 * 1024
FF_CHUNK = 256
SM_ROWS = 24

NAMES = ['x', 'positions', 'ffn1_norm', 'ffn1_w_gate', 'ffn1_w_up', 'ffn1_w_down', 'mix_norm', 'pool_w',
         'pool_scale', 'mla_w_in', 'mla_q_norm', 'mla_w_q_up', 'mla_kv_norm', 'mla_w_kv_up', 'mla_q_head_norm',
         'mla_k_head_norm', 'mla_w_out', 'ffn2_norm', 'ffn2_w_gate', 'ffn2_w_up', 'ffn2_w_down']
WEIGHTS = NAMES[2:]


def _tiles(n):
    return dict(ffn_fwd=min(512, n), ffn_bwd=min(256, n), fwd_k=min(512, n // 2), bwd_q=min(512, n // 2),
                mla_bwd=min(256, n), pool=min(512, n), tn=min(512, n), rows=min(1024, n))


def _params(*sem):
    return pltpu.CompilerParams(dimension_semantics=sem, vmem_limit_bytes=V7X_VMEM_LIMIT)


def _dot(a, b):
    return jnp.dot(a, b, preferred_element_type=F32)


def _dot_nt(a, b):
    return lax.dot_general(a, b, (((1,), (1,)), ((), ())), preferred_element_type=F32)


def _dot_tn(a, b):
    return lax.dot_general(a, b, (((0,), (0,)), ((), ())), preferred_element_type=F32)


def _rowsum8(v):
    rows, w = v.shape
    return jnp.sum(v.reshape(rows // 8, 8, w), axis=0)


def _sigmoid(a):
    return 1.0 / (1.0 + jnp.exp(-a))


def _rms_fwd(x, width=None):
    width = x.shape[-1] if width is None else width
    r = lax.rsqrt(jnp.sum(x * x, axis=-1, keepdims=True) * (1.0 / width) + EPS)
    return x * r, r


def _rms_bwd(dy, xhat, r, gain, width=None):
    width = xhat.shape[-1] if width is None else width
    t = dy * gain
    return r * (t - xhat * (jnp.sum(t * xhat, axis=-1, keepdims=True) * (1.0 / width)))


def _full(shape):
    return pl.BlockSpec(shape, lambda *_: (0,) * len(shape))


def _load_weights(srcs, dsts, sems):
    copies = [pltpu.make_async_copy(s, d, sems.at[i]) for i, (s, d) in enumerate(zip(srcs, dsts))]
    for cp in copies:
        cp.start()
    for cp in copies:
        cp.wait()


def _ffn_fwd(x, gain, wg_t, wu_t, wd, name):
    n = x.shape[0]
    tm = _tiles(n)["ffn_fwd"]

    def body(x_ref, g_ref, wg_hbm, wu_hbm, wd_hbm, out_ref, a_ref, u_ref, wg_v, wu_v, wd_v, sems):
        @pl.when(pl.program_id(0) == 0)
        def _():
            _load_weights((wg_hbm, wu_hbm, wd_hbm), (wg_v, wu_v, wd_v), sems)

        xt = x_ref[...]
        xhat, _ = _rms_fwd(xt)
        h = (xhat * g_ref[...]).astype(BF16)
        acc = jnp.zeros((tm, D_MODEL), F32)
        for c in range(D_FF // FF_CHUNK):
            sl = pl.ds(c * FF_CHUNK, FF_CHUNK)
            a = _dot_nt(h, wg_v[sl, :])
            u = _dot_nt(h, wu_v[sl, :])
            a_ref[:, sl] = a.astype(BF16)
            u_ref[:, sl] = u.astype(BF16)
            y = (a * _sigmoid(a) * u).astype(BF16)
            acc = acc + _dot(y, wd_v[sl, :])
        out_ref[...] = xt + FFN_HALF * acc

    any_spec = pl.BlockSpec(memory_space=pl.ANY)
    return pl.pallas_call(
        body, name=name, grid=(n // tm,),
        in_specs=[pl.BlockSpec((tm, D_MODEL), lambda i: (i, 0)), _full((1, D_MODEL)), any_spec, any_spec, any_spec],
        out_specs=[pl.BlockSpec((tm, D_MODEL), lambda i: (i, 0)), pl.BlockSpec((tm, D_FF), lambda i: (i, 0)),
                   pl.BlockSpec((tm, D_FF), lambda i: (i, 0))],
        out_shape=[jax.ShapeDtypeStruct((n, D_MODEL), F32), jax.ShapeDtypeStruct((n, D_FF), BF16),
                   jax.ShapeDtypeStruct((n, D_FF), BF16)],
        scratch_shapes=[pltpu.VMEM((D_FF, D_MODEL), BF16)] * 3 + [pltpu.SemaphoreType.DMA((3,))],
        compiler_params=_params("arbitrary"),
    )(x, gain, wg_t, wu_t, wd)


def _ffn_bwd(x, dout, a, u, gain, wg_t, wu_t, wd, name):
    n = x.shape[0]
    tm = _tiles(n)["ffn_bwd"]

    def body(x_ref, do_ref, a_ref, u_ref, g_ref, wg_hbm, wu_hbm, wd_hbm,
             dx_ref, h_ref, dob_ref, y_ref, da_ref, du_ref, dg_ref, wg_v, wu_v, wd_v, sems):
        @pl.when(pl.program_id(0) == 0)
        def _():
            _load_weights((wg_hbm, wu_hbm, wd_hbm), (wg_v, wu_v, wd_v), sems)
            dg_ref[...] = jnp.zeros_like(dg_ref)

        xt = x_ref[...]
        g = g_ref[...]
        xhat, r = _rms_fwd(xt)
        h_ref[...] = (xhat * g).astype(BF16)
        dout = do_ref[...]
        dob = (FFN_HALF * dout).astype(BF16)
        dob_ref[...] = dob
        dh = jnp.zeros((tm, D_MODEL), F32)
        for c in range(D_FF // FF_CHUNK):
            sl = pl.ds(c * FF_CHUNK, FF_CHUNK)
            dy = _dot_nt(dob, wd_v[sl, :])
            av = a_ref[:, sl].astype(F32)
            uv = u_ref[:, sl].astype(F32)
            s = _sigmoid(av)
            silu = av * s
            y_ref[:, sl] = (silu * uv).astype(BF16)
            du = (dy * silu).astype(BF16)
            da = (dy * uv * (s * (1.0 + av * (1.0 - s)))).astype(BF16)
            du_ref[:, sl] = du
            da_ref[:, sl] = da
            dh = dh + _dot(da, wg_v[sl, :]) + _dot(du, wu_v[sl, :])
        dg_ref[...] += _rowsum8(dh * xhat)
        dx_ref[...] = dout + _rms_bwd(dh, xhat, r, g)

    any_spec = pl.BlockSpec(memory_space=pl.ANY)
    row_d = pl.BlockSpec((tm, D_MODEL), lambda i: (i, 0))
    row_f = pl.BlockSpec((tm, D_FF), lambda i: (i, 0))
    return pl.pallas_call(
        body, name=name, grid=(n // tm,),
        in_specs=[row_d, row_d, row_f, row_f, _full((1, D_MODEL)), any_spec, any_spec, any_spec],
        out_specs=[row_d, row_d, row_d, row_f, row_f, row_f, _full((8, D_MODEL))],
        out_shape=[jax.ShapeDtypeStruct((n, D_MODEL), F32), jax.ShapeDtypeStruct((n, D_MODEL), BF16),
                   jax.ShapeDtypeStruct((n, D_MODEL), BF16), jax.ShapeDtypeStruct((n, D_FF), BF16),
                   jax.ShapeDtypeStruct((n, D_FF), BF16), jax.ShapeDtypeStruct((n, D_FF), BF16),
                   jax.ShapeDtypeStruct((8, D_MODEL), F32)],
        scratch_shapes=[pltpu.VMEM((D_FF, D_MODEL), BF16)] * 3 + [pltpu.SemaphoreType.DMA((3,))],
        compiler_params=_params("arbitrary"),
    )(x, dout, a, u, gain, wg_t, wu_t, wd)


def _tn_matmul(a, b, name):
    n, fa = a.shape
    db = b.shape[1]
    tk = _tiles(n)["tn"]
    tf = fa // 2 if (fa // 2) % 128 == 0 and fa > 1024 else fa

    def body(a_ref, b_ref, o_ref):
        @pl.when(pl.program_id(1) == 0)
        def _():
            o_ref[...] = jnp.zeros_like(o_ref)

        o_ref[...] += _dot_tn(a_ref[...], b_ref[...])

    return pl.pallas_call(
        body, name=name, grid=(fa // tf, n // tk),
        in_specs=[pl.BlockSpec((tk, tf), lambda i, k: (k, i)), pl.BlockSpec((tk, db), lambda i, k: (k, 0))],
        out_specs=pl.BlockSpec((tf, db), lambda i, k: (i, 0)),
        out_shape=jax.ShapeDtypeStruct((fa, db), F32),
        compiler_params=_params("arbitrary", "arbitrary"),
    )(a, b)


def _loss_head(y, target, name):
    n = y.shape[0]
    tm = _tiles(n)["rows"]

    def body(y_ref, t_ref, d_ref, acc_ref):
        @pl.when(pl.program_id(0) == 0)
        def _():
            acc_ref[...] = jnp.zeros_like(acc_ref)

        d = y_ref[...] - t_ref[...]
        d_ref[...] = d * (1.0 / D_MODEL)
        acc_ref[...] += _rowsum8(d * d)

    row = pl.BlockSpec((tm, D_MODEL), lambda i: (i, 0))
    return pl.pallas_call(
        body, name=name, grid=(n // tm,), in_specs=[row, row], out_specs=[row, _full((8, D_MODEL))],
        out_shape=[jax.ShapeDtypeStruct((n, D_MODEL), F32), jax.ShapeDtypeStruct((8, D_MODEL), F32)],
        compiler_params=_params("arbitrary"),
    )(y, target)


def _window_sum(v, w, rows, forward):
    s = v
    sh = 1
    while sh < w:
        s = s + pltpu.roll(s, (rows - sh) if forward else sh, 0)
        sh *= 2
    return s


def _pool_fwd(x, gain, w, scale, name):
    n = x.shape[0]
    tm = _tiles(n)["pool"]
    hb = tm // POOL_HALO
    rows = tm + POOL_HALO

    def body(x_ref, xh_ref, g_ref, w_ref, sc_ref, out_ref):
        i = pl.program_id(0)
        xt = x_ref[...]
        e = jnp.concatenate([xh_ref[...], xt], axis=0)
        xhat, _ = _rms_fwd(e)
        row = lax.broadcasted_iota(jnp.int32, (rows, 1), 0)
        hn = jnp.where((row >= POOL_HALO) | (i > 0), xhat * g_ref[...], 0.0)
        t_glob = i * tm + row - POOL_HALO
        outs = []
        for gi, win in enumerate(POOL_WINDOWS):
            ug = hn[:, gi * POOL_GROUP:(gi + 1) * POOL_GROUP]
            cnt = jnp.maximum(jnp.minimum(t_glob + 1, win), 1).astype(F32)
            pooled = (_window_sum(ug, win, rows, False) / cnt - ug)[POOL_HALO:]
            outs.append(_dot(pooled.astype(BF16), w_ref[gi]))
        out_ref[...] = xt + jnp.concatenate(outs, axis=1) * sc_ref[...]

    return pl.pallas_call(
        body, name=name, grid=(n // tm,),
        in_specs=[pl.BlockSpec((tm, D_MODEL), lambda i: (i, 0)),
                  pl.BlockSpec((POOL_HALO, D_MODEL), lambda i: (jnp.maximum(i * hb - 1, 0), 0)),
                  _full((1, D_MODEL)), _full((4, POOL_GROUP, POOL_GROUP)), _full((1, D_MODEL))],
        out_specs=pl.BlockSpec((tm, D_MODEL), lambda i: (i, 0)),
        out_shape=jax.ShapeDtypeStruct((n, D_MODEL), F32),
        compiler_params=_params("arbitrary"),
    )(x, x, gain, w, scale)


def _pool_bwd(x, dout, gain, w, scale, name):
    n = x.shape[0]
    tm = _tiles(n)["pool"]
    hb = tm // POOL_HALO
    rows = tm + POOL_HALO
    nt = n // tm

    def body(x_ref, xh_ref, do_ref, doh_ref, g_ref, w_ref, sc_ref, dx_ref, dw_ref, dsc_ref, dg_ref):
        i = pl.program_id(0)

        @pl.when(i == 0)
        def _():
            dw_ref[...] = jnp.zeros_like(dw_ref)
            dsc_ref[...] = jnp.zeros_like(dsc_ref)
            dg_ref[...] = jnp.zeros_like(dg_ref)

        xt = x_ref[...]
        g = g_ref[...]
        e = jnp.concatenate([xh_ref[...], xt], axis=0)
        xhat_e, r_e = _rms_fwd(e)
        row = lax.broadcasted_iota(jnp.int32, (rows, 1), 0)
        hn = jnp.where((row >= POOL_HALO) | (i > 0), xhat_e * g, 0.0)
        t_prev = i * tm + row - POOL_HALO
        t_next = i * tm + row
        dout = do_ref[...]
        dt = jnp.concatenate([dout, doh_ref[...]], axis=0)
        dt = jnp.where((row < tm) | (i < nt - 1), dt, 0.0)
        dyr = dt * sc_ref[...]
        dus, dscs = [], []
        for gi, win in enumerate(POOL_WINDOWS):
            lanes = slice(gi * POOL_GROUP, (gi + 1) * POOL_GROUP)
            ug = hn[:, lanes]
            cnt = jnp.maximum(jnp.minimum(t_prev + 1, win), 1).astype(F32)
            pooled = (_window_sum(ug, win, rows, False) / cnt - ug)[POOL_HALO:].astype(BF16)
            yraw = _dot(pooled, w_ref[gi])
            dscs.append(_rowsum8(dout[:, lanes] * yraw))
            dyr_b = dyr[:, lanes].astype(BF16)
            dw_ref[gi] += _dot_tn(pooled, dyr_b[:tm])
            dpool = _dot_nt(dyr_b, w_ref[gi])
            cnt2 = jnp.minimum(t_next + 1, win).astype(F32)
            dus.append((_window_sum(dpool / cnt2, win, rows, True) - dpool)[:tm])
        dsc_ref[...] += jnp.concatenate(dscs, axis=1)
        dh = jnp.concatenate(dus, axis=1)
        xhat = xhat_e[POOL_HALO:]
        dg_ref[...] += _rowsum8(dh * xhat)
        dx_ref[...] = dout + _rms_bwd(dh, xhat, r_e[POOL_HALO:], g)

    row_d = pl.BlockSpec((tm, D_MODEL), lambda i: (i, 0))
    prev_h = pl.BlockSpec((POOL_HALO, D_MODEL), lambda i: (jnp.maximum(i * hb - 1, 0), 0))
    next_h = pl.BlockSpec((POOL_HALO, D_MODEL), lambda i: (jnp.minimum((i + 1) * hb, n // POOL_HALO - 1), 0))
    return pl.pallas_call(
        body, name=name, grid=(nt,),
        in_specs=[row_d, prev_h, row_d, next_h, _full((1, D_MODEL)), _full((4, POOL_GROUP, POOL_GROUP)),
                  _full((1, D_MODEL))],
        out_specs=[row_d, _full((4, POOL_GROUP, POOL_GROUP)), _full((8, D_MODEL)), _full((8, D_MODEL))],
        out_shape=[jax.ShapeDtypeStruct((n, D_MODEL), F32), jax.ShapeDtypeStruct((4, POOL_GROUP, POOL_GROUP), F32),
                   jax.ShapeDtypeStruct((8, D_MODEL), F32), jax.ShapeDtypeStruct((8, D_MODEL), F32)],
        compiler_params=_params("arbitrary"),
    )(x, x, dout, dout, gain, w, scale)


def _rope(v, cos, sin_signed):
    lo, hi = v[:, :128], v[:, 128:]
    lane = lax.broadcasted_iota(jnp.int32, hi.shape, 1)
    swapped = jnp.where(lane < 32, pltpu.roll(hi, 96, 1), pltpu.roll(hi, 32, 1))
    return jnp.concatenate([lo, hi * cos + swapped * sin_signed], axis=1)


def _rope_bwd(gr, cos, sin_signed):
    lo, hi = gr[:, :128], gr[:, 128:]
    t = hi * sin_signed
    lane = lax.broadcasted_iota(jnp.int32, hi.shape, 1)
    swapped = jnp.where(lane < 32, pltpu.roll(t, 96, 1), pltpu.roll(t, 32, 1))
    return jnp.concatenate([lo, hi * cos + swapped], axis=1)


def _mla_latents(h, win_ref):
    cq = _dot_nt(h, win_ref[0:Q_LORA, :])
    ckv = _dot_nt(h, win_ref[Q_LORA:Q_LORA + KV_LORA, :])
    kpe = _dot_nt(h, win_ref[Q_LORA + KV_LORA:LAT_PAD, :])
    return cq, ckv, kpe


def _mla_pre_fwd(x, gain, win, gq, gkv, wq, wkv, ghq, ghk, cos, sin_signed, name):
    n = x.shape[0]
    tm = _tiles(n)["fwd_k"]

    def body(x_ref, g_ref, win_ref, gq_ref, gkv_ref, wq_ref, wkv_ref, ghq_ref, ghk_ref, c_ref, s_ref,
             q_ref, k_ref, v_ref, vt_ref):
        xhat, _ = _rms_fwd(x_ref[...])
        h = (xhat * g_ref[...]).astype(BF16)
        cq, ckv, kpe = _mla_latents(h, win_ref)
        cqn = (_rms_fwd(cq)[0] * gq_ref[...]).astype(BF16)
        ckvn = (_rms_fwd(ckv)[0] * gkv_ref[...]).astype(BF16)
        cos, sn = c_ref[...], s_ref[...]
        for hd in range(N_HEADS):
            rws = pl.ds(hd * HEAD_PAD, HEAD_PAD)
            qh = _dot_nt(cqn, wq_ref[rws, :])
            qn = _rms_fwd(qh, QK_HEAD)[0] * ghq_ref[...]
            q_ref[hd] = (_rope(qn, cos, sn) * LOGIT_SCALE).astype(BF16)
            kvh = _dot_nt(ckvn, wkv_ref[rws, :])
            kpre = jnp.concatenate([kvh[:, :QK_NOPE], kpe], axis=1)
            kn = _rms_fwd(kpre, QK_HEAD)[0] * ghk_ref[...]
            k_ref[hd] = _rope(kn, cos, sn).astype(BF16)
            vh = kvh[:, QK_NOPE:]
            v_ref[hd] = vh.astype(BF16)
            vt_ref[hd, 0] = jnp.concatenate([vh.T, jnp.ones((VT_ROWS - V_HEAD, tm), F32)], axis=0).astype(BF16)

    row = lambda w: pl.BlockSpec((tm, w), lambda i: (i, 0))
    head = lambda w: pl.BlockSpec((N_HEADS, tm, w), lambda i: (0, i, 0))
    return pl.pallas_call(
        body, name=name, grid=(n // tm,),
        in_specs=[row(D_MODEL), _full((1, D_MODEL)), _full((LAT_PAD, D_MODEL)), _full((1, Q_LORA)),
                  _full((1, KV_LORA)), _full((N_HEADS * HEAD_PAD, Q_LORA)), _full((N_HEADS * HEAD_PAD, KV_LORA)),
                  _full((1, HEAD_PAD)), _full((1, HEAD_PAD)), row(128), row(128)],
        out_specs=[head(HEAD_PAD), head(HEAD_PAD), head(V_HEAD),
                   pl.BlockSpec((N_HEADS, 1, VT_ROWS, tm), lambda i: (0, i, 0, 0))],
        out_shape=[jax.ShapeDtypeStruct((N_HEADS, n, HEAD_PAD), BF16), jax.ShapeDtypeStruct((N_HEADS, n, HEAD_PAD), BF16),
                   jax.ShapeDtypeStruct((N_HEADS, n, V_HEAD), BF16),
                   jax.ShapeDtypeStruct((N_HEADS, n // tm, VT_ROWS, tm), BF16)],
        compiler_params=_params("arbitrary"),
    )(x, gain, win, gq, gkv, wq, wkv, ghq, ghk, cos, sin_signed)


def _flash_fwd(q, k, vt, name):
    n = q.shape[1]
    tk = _tiles(n)["fwd_k"]
    tq = 2 * tk
    nq = n // tq

    def body(q_ref, k_ref, vt_ref, o_ref, lse_ref, s_scr, m_scr, acc_scr):
        i = pl.program_id(1)
        qi = q_ref[0]

        def scores(j, slot):
            s_scr[slot] = _dot_nt(k_ref[0, pl.ds(pl.multiple_of(j * tk, tk), tk), :], qi)

        def update(j, slot, diagonal=None):
            s = s_scr[slot]
            if diagonal is not None:
                krow = lax.broadcasted_iota(jnp.int32, (tk, tq), 0) + diagonal * tk
                qcol = lax.broadcasted_iota(jnp.int32, (tk, tq), 1)
                s = jnp.where(krow <= qcol, s, -jnp.inf)
            m = m_scr[...]
            m_new = jnp.maximum(m, jnp.max(s, axis=0, keepdims=True))
            p = jnp.exp2(s - m_new).astype(BF16)
            acc_scr[...] = jnp.exp2(m - m_new) * acc_scr[...] + _dot(vt_ref[0, j], p)
            m_scr[...] = m_new

        m_scr[...] = jnp.full((1, tq), -jnp.inf, F32)
        acc_scr[...] = jnp.zeros((VT_ROWS, tq), F32)
        scores(0, 0)

        def pair(jj, carry):
            scores(2 * jj + 1, 1)
            update(2 * jj, 0)
            scores(2 * jj + 2, 0)
            update(2 * jj + 1, 1)
            return carry

        lax.fori_loop(0, i, pair, 0)
        scores(2 * i + 1, 1)
        update(2 * i, 0, diagonal=0)
        update(2 * i + 1, 1, diagonal=1)
        l = acc_scr[V_HEAD:V_HEAD + 1, :]
        o_ref[...] = (acc_scr[0:V_HEAD, :] / l).T.astype(BF16)
        lse_ref[0, 0] = m_scr[...] + jnp.log2(l)

    return pl.pallas_call(
        body, name=name, grid=(N_HEADS, nq),
        in_specs=[pl.BlockSpec((1, tq, HEAD_PAD), lambda h, i: (h, i, 0)),
                  pl.BlockSpec((1, n, HEAD_PAD), lambda h, i: (h, 0, 0)),
                  pl.BlockSpec((1, n // tk, VT_ROWS, tk), lambda h, i: (h, 0, 0, 0))],
        out_specs=[pl.BlockSpec((tq, V_HEAD), lambda h, i: (i, h)),
                   pl.BlockSpec((1, 1, 1, tq), lambda h, i: (h, i, 0, 0))],
        out_shape=[jax.ShapeDtypeStruct((n, N_HEADS * V_HEAD), BF16), jax.ShapeDtypeStruct((N_HEADS, nq, 1, tq), F32)],
        scratch_shapes=[pltpu.VMEM((2, tk, tq), F32), pltpu.VMEM((1, tq), F32), pltpu.VMEM((VT_ROWS, tq), F32)],
        compiler_params=_params("arbitrary", "arbitrary"),
    )(q, k, vt)


def _mla_post_fwd(o, x, wout, name):
    n = x.shape[0]
    tm = _tiles(n)["rows"]

    def body(o_ref, x_ref, w_ref, out_ref):
        out_ref[...] = x_ref[...] + _dot(o_ref[...], w_ref[...])

    row = pl.BlockSpec((tm, D_MODEL), lambda i: (i, 0))
    return pl.pallas_call(
        body, name=name, grid=(n // tm,), in_specs=[row, row, _full((D_MODEL, D_MODEL))], out_specs=row,
        out_shape=jax.ShapeDtypeStruct((n, D_MODEL), F32), compiler_params=_params("arbitrary"),
    )(o, x, wout)


def _mla_out_bwd(dout, o, wout, name):
    n = dout.shape[0]
    t = _tiles(n)["bwd_q"]
    nq = n // t

    def body(do_ref, o_ref, w_ref, dob_ref, dpo_ref, dl_ref):
        dob = do_ref[...].astype(BF16)
        dob_ref[...] = dob
        dpo = _dot_nt(dob, w_ref[...])
        dpo_ref[...] = dpo.astype(BF16)
        ov = o_ref[...].astype(F32)
        for hd in range(N_HEADS):
            lanes = slice(hd * V_HEAD, (hd + 1) * V_HEAD)
            prod = dpo[:, lanes] * ov[:, lanes]
            dl_ref[hd, 0] = jnp.sum(prod.T, axis=0, keepdims=True)

    row = pl.BlockSpec((t, D_MODEL), lambda i: (i, 0))
    return pl.pallas_call(
        body, name=name, grid=(nq,), in_specs=[row, row, _full((D_MODEL, D_MODEL))],
        out_specs=[row, row, pl.BlockSpec((N_HEADS, 1, 1, t), lambda i: (0, i, 0, 0))],
        out_shape=[jax.ShapeDtypeStruct((n, D_MODEL), BF16), jax.ShapeDtypeStruct((n, D_MODEL), BF16),
                   jax.ShapeDtypeStruct((N_HEADS, nq, 1, t), F32)],
        compiler_params=_params("arbitrary"),
    )(dout, o, wout)


def _flash_bwd(q, k, v, dpo, lse, delta, name):
    n = q.shape[1]
    tq = _tiles(n)["bwd_q"]
    tk = 2 * tq
    nk = n // tk
    nqb = n // tq

    def body(k_ref, v_ref, q_ref, do_ref, lse_ref, dl_ref, dk_ref, dv_ref, dq_hbm,
             dq_acc, s_scr, dp_scr, dk_acc, dv_acc, sem):
        h = pl.program_id(0)
        j = pl.program_id(1)

        @pl.when(j == 0)
        def _():
            dq_acc[...] = jnp.zeros_like(dq_acc)

        dk_acc[...] = jnp.zeros_like(dk_acc)
        dv_acc[...] = jnp.zeros_like(dv_acc)
        kj = k_ref[0]
        vj = v_ref[0]
        npairs = nk - 1 - j

        def block(t):
            return jnp.where(t < 2 * npairs, 2 * j + 2 + t, 2 * j + (t - 2 * npairs))

        def scores(i, slot):
            rws = pl.ds(pl.multiple_of(i * tq, tq), tq)
            s_scr[slot] = _dot_nt(kj, q_ref[0, rws, :])
            dp_scr[slot] = _dot_nt(vj, do_ref[rws, :])

        def update(i, slot, diagonal=None):
            rws = pl.ds(pl.multiple_of(i * tq, tq), tq)
            p = jnp.exp2(s_scr[slot] - lse_ref[0, i])
            if diagonal is not None:
                krow = lax.broadcasted_iota(jnp.int32, (tk, tq), 0)
                qcol = lax.broadcasted_iota(jnp.int32, (tk, tq), 1) + diagonal * tq
                p = jnp.where(krow <= qcol, p, 0.0)
            dv_acc[...] += _dot(p.astype(BF16), do_ref[rws, :])
            ds = (p * (dp_scr[slot] - dl_ref[0, i])).astype(BF16)
            dk_acc[...] += _dot(ds, q_ref[0, rws, :])
            dq_acc[rws, :] += _dot_tn(ds, kj)

        scores(block(0), 0)

        def pair(jj, carry):
            scores(block(2 * jj + 1), 1)
            update(block(2 * jj), 0)
            scores(block(2 * jj + 2), 0)
            update(block(2 * jj + 1), 1)
            return carry

        lax.fori_loop(0, npairs, pair, 0)
        scores(2 * j + 1, 1)
        update(2 * j, 0, diagonal=0)
        update(2 * j + 1, 1, diagonal=1)
        dk_ref[0] = dk_acc[...] * (ATTN_SCALE / LOGIT_SCALE)
        dv_ref[0] = dv_acc[...]

        @pl.when(j == nk - 1)
        def _():
            dq_acc[...] = dq_acc[...] * ATTN_SCALE
            cp = pltpu.make_async_copy(dq_acc, dq_hbm.at[h], sem)
            cp.start()
            cp.wait()

    resident = dict(pipeline_mode=pl.Buffered(1))
    return pl.pallas_call(
        body, name=name, grid=(N_HEADS, nk),
        in_specs=[pl.BlockSpec((1, tk, HEAD_PAD), lambda h, j: (h, j, 0)),
                  pl.BlockSpec((1, tk, V_HEAD), lambda h, j: (h, j, 0)),
                  pl.BlockSpec((1, n, HEAD_PAD), lambda h, j: (h, 0, 0), **resident),
                  pl.BlockSpec((n, V_HEAD), lambda h, j: (0, h), **resident),
                  pl.BlockSpec((1, nqb, 1, tq), lambda h, j: (h, 0, 0, 0)),
                  pl.BlockSpec((1, nqb, 1, tq), lambda h, j: (h, 0, 0, 0))],
        out_specs=[pl.BlockSpec((1, tk, HEAD_PAD), lambda h, j: (h, j, 0)),
                   pl.BlockSpec((1, tk, V_HEAD), lambda h, j: (h, j, 0)),
                   pl.BlockSpec(memory_space=pl.ANY)],
        out_shape=[jax.ShapeDtypeStruct((N_HEADS, n, HEAD_PAD), F32), jax.ShapeDtypeStruct((N_HEADS, n, V_HEAD), F32),
                   jax.ShapeDtypeStruct((N_HEADS, n, HEAD_PAD), F32)],
        scratch_shapes=[pltpu.VMEM((n, HEAD_PAD), F32), pltpu.VMEM((2, tk, tq), F32), pltpu.VMEM((2, tk, tq), F32),
                        pltpu.VMEM((tk, HEAD_PAD), F32), pltpu.VMEM((tk, V_HEAD), F32), pltpu.SemaphoreType.DMA(())],
        compiler_params=_params("arbitrary", "arbitrary"),
    )(k, v, q, dpo, lse, delta)


def _mla_pre_bwd(x, dout, dq, dk, dv, gain, win, gq, gkv, wq, wkv, ghq, ghk, cos, sin_signed, name):
    n = x.shape[0]
    tm = _tiles(n)["mla_bwd"]
    hw = N_HEADS * HEAD_PAD

    def body(x_ref, do_ref, dq_ref, dk_ref, dv_ref, g_ref, win_ref, gq_ref, gkv_ref, wq_ref, wkv_ref, ghq_ref, ghk_ref,
             c_ref, s_ref, dx_ref, h_ref, dlat_ref, cqn_ref, dqp_ref, ckvn_ref, dkv_ref,
             dg_ref, dgq_ref, dgkv_ref, dghq_ref, dghk_ref):
        @pl.when(pl.program_id(0) == 0)
        def _():
            for ref in (dg_ref, dgq_ref, dgkv_ref, dghq_ref, dghk_ref):
                ref[...] = jnp.zeros_like(ref)

        g = g_ref[...]
        xhat, r = _rms_fwd(x_ref[...])
        h = (xhat * g).astype(BF16)
        h_ref[...] = h
        cq, ckv, kpe = _mla_latents(h, win_ref)
        cqhat, rcq = _rms_fwd(cq)
        ckvhat, rckv = _rms_fwd(ckv)
        cqn = (cqhat * gq_ref[...]).astype(BF16)
        ckvn = (ckvhat * gkv_ref[...]).astype(BF16)
        cqn_ref[...] = cqn
        ckvn_ref[...] = ckvn
        cos, sn = c_ref[...], s_ref[...]
        ghq, ghk = ghq_ref[...], ghk_ref[...]
        dcqn = jnp.zeros((tm, Q_LORA), F32)
        dckvn = jnp.zeros((tm, KV_LORA), F32)
        dkpe = jnp.zeros((tm, 128), F32)
        dghq = jnp.zeros((8, HEAD_PAD), F32)
        dghk = jnp.zeros((8, HEAD_PAD), F32)
        for hd in range(N_HEADS):
            rws = pl.ds(hd * HEAD_PAD, HEAD_PAD)
            lanes = slice(hd * HEAD_PAD, (hd + 1) * HEAD_PAD)
            qhat, rq = _rms_fwd(_dot_nt(cqn, wq_ref[rws, :]), QK_HEAD)
            gqn = _rope_bwd(dq_ref[hd], cos, sn)
            dghq = dghq + _rowsum8(gqn * qhat)
            dqpre = _rms_bwd(gqn, qhat, rq, ghq, QK_HEAD).astype(BF16)
            dqp_ref[:, lanes] = dqpre
            dcqn = dcqn + _dot(dqpre, wq_ref[rws, :])
            kvh = _dot_nt(ckvn, wkv_ref[rws, :])
            khat, rk = _rms_fwd(jnp.concatenate([kvh[:, :QK_NOPE], kpe], axis=1), QK_HEAD)
            gkn = _rope_bwd(dk_ref[hd], cos, sn)
            dghk = dghk + _rowsum8(gkn * khat)
            dkpre = _rms_bwd(gkn, khat, rk, ghk, QK_HEAD)
            dkpe = dkpe + dkpre[:, QK_NOPE:]
            dkvh = jnp.concatenate([dkpre[:, :QK_NOPE], dv_ref[hd]], axis=1).astype(BF16)
            dkv_ref[:, lanes] = dkvh
            dckvn = dckvn + _dot(dkvh, wkv_ref[rws, :])
        dghq_ref[...] += dghq
        dghk_ref[...] += dghk
        dgq_ref[...] += _rowsum8(dcqn * cqhat)
        dgkv_ref[...] += _rowsum8(dckvn * ckvhat)
        dlat = jnp.concatenate([_rms_bwd(dcqn, cqhat, rcq, gq_ref[...]), _rms_bwd(dckvn, ckvhat, rckv, gkv_ref[...]),
                                dkpe], axis=1).astype(BF16)
        dlat_ref[...] = dlat
        dh = _dot(dlat, win_ref[...])
        dg_ref[...] += _rowsum8(dh * xhat)
        dx_ref[...] = do_ref[...] + _rms_bwd(dh, xhat, r, g)

    row = lambda w: pl.BlockSpec((tm, w), lambda i: (i, 0))
    head = lambda w: pl.BlockSpec((N_HEADS, tm, w), lambda i: (0, i, 0))
    sds = jax.ShapeDtypeStruct
    return pl.pallas_call(
        body, name=name, grid=(n // tm,),
        in_specs=[row(D_MODEL), row(D_MODEL), head(HEAD_PAD), head(HEAD_PAD), head(V_HEAD), _full((1, D_MODEL)),
                  _full((LAT_PAD, D_MODEL)), _full((1, Q_LORA)), _full((1, KV_LORA)), _full((hw, Q_LORA)),
                  _full((hw, KV_LORA)), _full((1, HEAD_PAD)), _full((1, HEAD_PAD)), row(128), row(128)],
        out_specs=[row(D_MODEL), row(D_MODEL), row(LAT_PAD), row(Q_LORA), row(hw), row(KV_LORA), row(hw),
                   _full((8, D_MODEL)), _full((8, Q_LORA)), _full((8, KV_LORA)), _full((8, HEAD_PAD)),
                   _full((8, HEAD_PAD))],
        out_shape=[sds((n, D_MODEL), F32), sds((n, D_MODEL), BF16), sds((n, LAT_PAD), BF16), sds((n, Q_LORA), BF16),
                   sds((n, hw), BF16), sds((n, KV_LORA), BF16), sds((n, hw), BF16), sds((8, D_MODEL), F32),
                   sds((8, Q_LORA), F32), sds((8, KV_LORA), F32), sds((8, HEAD_PAD), F32), sds((8, HEAD_PAD), F32)],
        compiler_params=_params("arbitrary"),
    )(x, dout, dq, dk, dv, gain, win, gq, gkv, wq, wkv, ghq, ghk, cos, sin_signed)


def _adamw(w, g, m, v, name):
    rows, cols = w.shape
    tr = rows
    for cand in (512, 256, 128, 64, 32, 16, 8):
        if rows % cand == 0 and rows > cand:
            tr = cand
            break

    def body(w_ref, g_ref, m_ref, v_ref, d_ref, mo_ref, vo_ref):
        gv = g_ref[...]
        mn = ADAM_B1 * m_ref[...] + (1.0 - ADAM_B1) * gv
        vn = ADAM_B2 * v_ref[...] + (1.0 - ADAM_B2) * (gv * gv)
        m_hat = mn / (1.0 - ADAM_B1 ** ADAM_STEP)
        v_hat = vn / (1.0 - ADAM_B2 ** ADAM_STEP)
        d_ref[...] = -ADAM_LR * (m_hat / (jnp.sqrt(v_hat) + ADAM_EPS) + ADAM_WD * w_ref[...])
        mo_ref[...] = mn
        vo_ref[...] = vn

    blk = pl.BlockSpec((tr, cols), lambda i: (i, 0))
    return pl.pallas_call(
        body, name=name, grid=(rows // tr,), in_specs=[blk] * 4, out_specs=[blk] * 3,
        out_shape=[jax.ShapeDtypeStruct((rows, cols), F32)] * 3, compiler_params=_params("arbitrary"),
    )(w, g, m, v)


def _sum_parts(parts, name):
    k, r, c = parts.shape
    tr = min(r, 512)

    def body(p_ref, o_ref):
        acc = p_ref[0]
        for j in range(1, k):
            acc = acc + p_ref[j]
        o_ref[...] = acc

    return pl.pallas_call(
        body, name=name, grid=(r // tr,), in_specs=[pl.BlockSpec((k, tr, c), lambda i: (0, i, 0))],
        out_specs=pl.BlockSpec((tr, c), lambda i: (i, 0)), out_shape=jax.ShapeDtypeStruct((r, c), parts.dtype),
        compiler_params=_params("arbitrary"),
    )(parts)


def _add_pairs(grads, recv, core, name):
    _, r, c = grads.shape
    tr = 592 if r % 592 == 0 else r

    def body(core_ref, g_ref, r_ref, o_ref):
        o_ref[...] = g_ref[...] + r_ref[...]

    blk = pl.BlockSpec((1, tr, c), lambda k, i, core_ref: (k, i, 0))
    return pl.pallas_call(
        body, name=name,
        grid_spec=pltpu.PrefetchScalarGridSpec(
            num_scalar_prefetch=1, grid=(4, r // tr),
            in_specs=[pl.BlockSpec((1, tr, c), lambda k, i, core_ref: (2 * k + core_ref[0], i, 0)), blk],
            out_specs=blk),
        out_shape=jax.ShapeDtypeStruct((4, r, c), grads.dtype), compiler_params=_params("arbitrary", "arbitrary"),
    )(core, grads, recv)


def _add_chips(pairs, recv, chip, name):
    _, r, c = pairs.shape
    tr = 592 if r % 592 == 0 else r

    def body(chip_ref, p_ref, r_ref, o_ref):
        o_ref[...] = ((p_ref[0] + r_ref[0]) + r_ref[1]) + r_ref[2]

    return pl.pallas_call(
        body, name=name,
        grid_spec=pltpu.PrefetchScalarGridSpec(
            num_scalar_prefetch=1, grid=(r // tr,),
            in_specs=[pl.BlockSpec((1, tr, c), lambda i, chip_ref: (chip_ref[0], i, 0)),
                      pl.BlockSpec((3, tr, c), lambda i, chip_ref: (0, i, 0))],
            out_specs=pl.BlockSpec((tr, c), lambda i, chip_ref: (i, 0))),
        out_shape=jax.ShapeDtypeStruct((r, c), pairs.dtype), compiler_params=_params("arbitrary"),
    )(chip, pairs, recv)


MESH = pl.DeviceIdType.MESH


def _all_gather(x, name):
    r, c = x.shape

    def body(x_ref, out_ref, send_sems, recv_sems, local_sem):
        mx, my, mc = lax.axis_index("x"), lax.axis_index("y"), lax.axis_index("c")
        me, sibling = (mx, my, mc), (mx, my, 1 - mc)
        chips = [(1 - mx, my), (mx, 1 - my), (1 - mx, 1 - my)]

        def slot(px, py, pc):
            return out_ref.at[4 * px + 2 * py + pc]

        def copy(k, block, to, src=None):
            return pltpu.make_async_remote_copy(
                src_ref=slot(*block) if src is None else src, dst_ref=slot(*block),
                send_sem=send_sems.at[k], recv_sem=recv_sems.at[k], device_id=to, device_id_type=MESH)

        mine = pltpu.make_async_copy(x_ref, slot(*me), local_sem)
        mine.start()
        first = [copy(0, me, sibling, src=x_ref)]
        first += [copy(1 + j, me, (*chip, mc), src=x_ref) for j, chip in enumerate(chips)]
        for cp in first:
            cp.start()
        passed = [copy(4 + j, (*chip, mc), sibling) for j, chip in enumerate(chips)]
        for j, chip in enumerate(chips):
            copy(1 + j, (*chip, mc), me).wait_recv()
            passed[j].start()
        copy(0, sibling, me).wait_recv()
        for j, chip in enumerate(chips):
            copy(4 + j, (*chip, 1 - mc), me).wait_recv()
        for cp in first + passed:
            cp.wait_send()
        mine.wait()

    any_spec = pl.BlockSpec(memory_space=pl.ANY)
    return pl.pallas_call(
        body, name=name, in_specs=[any_spec], out_specs=any_spec,
        out_shape=jax.ShapeDtypeStruct((N_DEV, r, c), x.dtype),
        scratch_shapes=[pltpu.SemaphoreType.DMA((7,)), pltpu.SemaphoreType.DMA((7,)), pltpu.SemaphoreType.DMA(())],
    )(x)


def _exchange_sibling(grads, name):
    _, r, c = grads.shape

    def body(g_ref, recv_ref, send_sems, recv_sems):
        mx, my, mc = lax.axis_index("x"), lax.axis_index("y"), lax.axis_index("c")
        copies = [pltpu.make_async_remote_copy(
            src_ref=g_ref.at[2 * k + (1 - mc)], dst_ref=recv_ref.at[k], send_sem=send_sems.at[k],
            recv_sem=recv_sems.at[k], device_id=(mx, my, 1 - mc), device_id_type=MESH) for k in range(4)]
        for cp in copies:
            cp.start()
        for cp in copies:
            cp.wait()

    any_spec = pl.BlockSpec(memory_space=pl.ANY)
    return pl.pallas_call(
        body, name=name, in_specs=[any_spec], out_specs=any_spec,
        out_shape=jax.ShapeDtypeStruct((4, r, c), grads.dtype),
        scratch_shapes=[pltpu.SemaphoreType.DMA((4,)), pltpu.SemaphoreType.DMA((4,))],
    )(grads)


def _exchange_chips(pairs, name):
    _, r, c = pairs.shape

    def body(p_ref, recv_ref, send_sems, recv_sems):
        mx, my, mc = lax.axis_index("x"), lax.axis_index("y"), lax.axis_index("c")
        chips = [(1 - mx, my), (mx, 1 - my), (1 - mx, 1 - my)]
        copies = [pltpu.make_async_remote_copy(
            src_ref=p_ref.at[2 * cx + cy], dst_ref=recv_ref.at[j], send_sem=send_sems.at[j],
            recv_sem=recv_sems.at[j], device_id=(cx, cy, mc), device_id_type=MESH)
            for j, (cx, cy) in enumerate(chips)]
        for cp in copies:
            cp.start()
        for cp in copies:
            cp.wait()

    any_spec = pl.BlockSpec(memory_space=pl.ANY)
    return pl.pallas_call(
        body, name=name, in_specs=[any_spec], out_specs=any_spec,
        out_shape=jax.ShapeDtypeStruct((3, r, c), pairs.dtype),
        scratch_shapes=[pltpu.SemaphoreType.DMA((3,)), pltpu.SemaphoreType.DMA((3,))],
    )(pairs)


def _reduce_scatter(grads):
    mx, my, mc = lax.axis_index("x"), lax.axis_index("y"), lax.axis_index("c")
    core = jnp.reshape(mc, (1,)).astype(jnp.int32)
    chip = jnp.reshape(2 * mx + my, (1,)).astype(jnp.int32)
    from_sibling = _exchange_sibling(grads, "rs_exchange_sibling")
    pairs = _add_pairs(grads, from_sibling, core, "rs_add_pairs")
    from_chips = _exchange_chips(pairs, "rs_exchange_chips")
    return _add_chips(pairs, from_chips, chip, "rs_add_chips")


FFN_ROWS = D_FF // N_DEV
WIN_ROWS = (Q_LORA + KV_LORA + QK_ROPE) // N_DEV
WIN_ROWS_PAD = 144
WQ_ROWS = QK_HEAD * Q_LORA // D_MODEL
WKV_ROWS = 256 * KV_LORA // D_MODEL
WOUT_ROWS = V_HEAD
POOL_ROWS = 4 * 32 * POOL_GROUP // D_MODEL


def _t(w):
    return jnp.swapaxes(w, -1, -2)


def _pack_shards(p, dtype):
    segs = []
    for i in range(DEPTH):
        for f in ("ffn1", "ffn2"):
            segs += [_t(p[f + "_w_gate"][i]), _t(p[f + "_w_up"][i]), p[f + "_w_down"][i]]
    for j in range(DEPTH // 2):
        segs += [jnp.pad(_t(p["mla_w_in"][j]), ((0, WIN_ROWS_PAD - WIN_ROWS), (0, 0))),
                 _t(p["mla_w_q_up"][j]).reshape(WQ_ROWS, D_MODEL),
                 _t(p["mla_w_kv_up"][j]).reshape(WKV_ROWS, D_MODEL),
                 p["mla_w_out"][j]]
    for j in range(DEPTH // 2):
        segs.append(p["pool_w"][j].reshape(POOL_ROWS, D_MODEL))
    return jnp.concatenate([s.astype(dtype) for s in segs], axis=0)


def _segments():
    out = []
    for i in range(DEPTH):
        for f in ("ffn1", "ffn2"):
            out += [(f + "_w_gate", i, FFN_ROWS), (f + "_w_up", i, FFN_ROWS), (f + "_w_down", i, FFN_ROWS)]
    for j in range(DEPTH // 2):
        out += [("mla_w_in", j, WIN_ROWS_PAD), ("mla_w_q_up", j, WQ_ROWS), ("mla_w_kv_up", j, WKV_ROWS),
                ("mla_w_out", j, WOUT_ROWS)]
    for j in range(DEPTH // 2):
        out.append(("pool_w", j, POOL_ROWS))
    return out


def _unpack_gathered(g):
    out = {}
    off = 0
    for name, layer, rows in _segments():
        seg = g[:, off:off + rows, :]
        off += rows
        if name == "mla_w_in":
            w = seg[:, :WIN_ROWS].reshape(N_DEV * WIN_ROWS, D_MODEL)
            w = jnp.pad(w, ((0, LAT_PAD - N_DEV * WIN_ROWS), (0, 0)))
        elif name == "mla_w_q_up":
            w = seg.reshape(N_HEADS, QK_HEAD, Q_LORA)
            w = jnp.pad(w, ((0, 0), (0, HEAD_PAD - QK_HEAD), (0, 0))).reshape(N_HEADS * HEAD_PAD, Q_LORA)
        elif name == "mla_w_kv_up":
            w = seg.reshape(N_HEADS * 256, KV_LORA)
        elif name == "pool_w":
            w = seg.reshape(N_DEV, 4, 32, POOL_GROUP).transpose(1, 0, 2, 3).reshape(4, POOL_GROUP, POOL_GROUP)
        else:
            w = seg.reshape(N_DEV * rows, D_MODEL)
        out[(name, layer)] = w
    return out


def _pack_grads(gr):
    segs = []
    for name, layer, rows in _segments():
        g = gr[(name, layer)]
        if name == "mla_w_in":
            g = g[:N_DEV * WIN_ROWS].reshape(N_DEV, WIN_ROWS, D_MODEL)
            g = jnp.pad(g, ((0, 0), (0, WIN_ROWS_PAD - WIN_ROWS), (0, 0)))
        elif name == "mla_w_q_up":
            g = g.reshape(N_HEADS, HEAD_PAD, Q_LORA)[:, :QK_HEAD].reshape(N_DEV, WQ_ROWS, D_MODEL)
        elif name == "mla_w_kv_up":
            g = g.reshape(N_DEV, WKV_ROWS, D_MODEL)
        elif name == "pool_w":
            g = g.reshape(4, N_DEV, 32, POOL_GROUP).transpose(1, 0, 2, 3).reshape(N_DEV, POOL_ROWS, D_MODEL)
        else:
            g = g.reshape(N_DEV, rows, D_MODEL)
        segs.append(g)
    return jnp.concatenate(segs, axis=1)


def _unpack_shard_grads(flat):
    per = {}
    off = 0
    for name, layer, rows in _segments():
        seg = flat[off:off + rows]
        off += rows
        if name.endswith("w_gate") or name.endswith("w_up"):
            g = _t(seg)
        elif name == "mla_w_in":
            g = _t(seg[:WIN_ROWS])
        elif name == "mla_w_q_up":
            g = _t(seg.reshape(QK_HEAD, Q_LORA))
        elif name == "mla_w_kv_up":
            g = _t(seg.reshape(256, KV_LORA))
        elif name == "pool_w":
            g = seg.reshape(4, 32, POOL_GROUP)
        else:
            g = seg
        per.setdefault(name, []).append(g)
    return {k: jnp.stack(v) for k, v in per.items()}


def _pad_lanes(v, width):
    return jnp.pad(v, ((0, 0), (0, width - v.shape[-1])))


def kernel(x, positions, ffn1_norm, ffn1_w_gate, ffn1_w_up, ffn1_w_down, mix_norm, pool_w, pool_scale, mla_w_in, mla_q_norm, mla_w_q_up, mla_kv_norm, mla_w_kv_up, mla_q_head_norm, mla_k_head_norm, mla_w_out, ffn2_norm, ffn2_w_gate, ffn2_w_up, ffn2_w_down, loss_target, m_ffn1_norm, m_ffn1_w_gate, m_ffn1_w_up, m_ffn1_w_down, m_mix_norm, m_pool_w, m_pool_scale, m_mla_w_in, m_mla_q_norm, m_mla_w_q_up, m_mla_kv_norm, m_mla_w_kv_up, m_mla_q_head_norm, m_mla_k_head_norm, m_mla_w_out, m_ffn2_norm, m_ffn2_w_gate, m_ffn2_w_up, m_ffn2_w_down, v_ffn1_norm, v_ffn1_w_gate, v_ffn1_w_up, v_ffn1_w_down, v_mix_norm, v_pool_w, v_pool_scale, v_mla_w_in, v_mla_q_norm, v_mla_w_q_up, v_mla_kv_norm, v_mla_w_kv_up, v_mla_q_head_norm, v_mla_k_head_norm, v_mla_w_out, v_ffn2_norm, v_ffn2_w_gate, v_ffn2_w_up, v_ffn2_w_down):
    args = (x, positions, ffn1_norm, ffn1_w_gate, ffn1_w_up, ffn1_w_down, mix_norm, pool_w, pool_scale, mla_w_in,
            mla_q_norm, mla_w_q_up, mla_kv_norm, mla_w_kv_up, mla_q_head_norm, mla_k_head_norm, mla_w_out, ffn2_norm,
            ffn2_w_gate, ffn2_w_up, ffn2_w_down)
    p = dict(zip(NAMES, args))
    moments_m = dict(zip(WEIGHTS, (m_ffn1_norm, m_ffn1_w_gate, m_ffn1_w_up, m_ffn1_w_down, m_mix_norm, m_pool_w, m_pool_scale, m_mla_w_in, m_mla_q_norm, m_mla_w_q_up, m_mla_kv_norm, m_mla_w_kv_up, m_mla_q_head_norm, m_mla_k_head_norm, m_mla_w_out, m_ffn2_norm, m_ffn2_w_gate, m_ffn2_w_up, m_ffn2_w_down)))
    moments_v = dict(zip(WEIGHTS, (v_ffn1_norm, v_ffn1_w_gate, v_ffn1_w_up, v_ffn1_w_down, v_mix_norm, v_pool_w, v_pool_scale, v_mla_w_in, v_mla_q_norm, v_mla_w_q_up, v_mla_kv_norm, v_mla_w_kv_up, v_mla_q_head_norm, v_mla_k_head_norm, v_mla_w_out, v_ffn2_norm, v_ffn2_w_gate, v_ffn2_w_up, v_ffn2_w_down)))
    dev = 4 * lax.axis_index("x") + 2 * lax.axis_index("y") + lax.axis_index("c")

    xs = x[0]
    n = xs.shape[0]
    target = loss_target[0]

    inv_freq = 1.0 / (ROPE_THETA ** (jnp.arange(0, QK_ROPE, 2, dtype=F32) / QK_ROPE))
    ang = positions[0].astype(F32)[..., None] * inv_freq
    cos, sin = jnp.cos(ang), jnp.sin(ang)
    zero = jnp.zeros((n, 128 - QK_ROPE), F32)
    rope_cos = jnp.concatenate([cos, cos, zero], axis=1)
    rope_sin = jnp.concatenate([-sin, sin, zero], axis=1)

    w = _unpack_gathered(_all_gather(_pack_shards(p, BF16), "weights_all_gather"))

    gains_local = jnp.concatenate([_pad_lanes(mla_q_norm, 128), _pad_lanes(mla_kv_norm, 128)], axis=0)
    gains_all = _all_gather(jnp.pad(gains_local, ((0, 4), (0, 0))), "gains_all_gather")
    q_norm_full = gains_all[:, 0:2, :Q_LORA // N_DEV].transpose(1, 0, 2).reshape(2, Q_LORA)
    kv_norm_full = gains_all[:, 2:4, :KV_LORA // N_DEV].transpose(1, 0, 2).reshape(2, KV_LORA)
    ghq = _pad_lanes(mla_q_head_norm, HEAD_PAD)
    ghk = _pad_lanes(mla_k_head_norm, HEAD_PAD)

    def mla_weights(j):
        return (w[("mla_w_in", j)], q_norm_full[j:j + 1], kv_norm_full[j:j + 1], w[("mla_w_q_up", j)],
                w[("mla_w_kv_up", j)], ghq[j:j + 1], ghk[j:j + 1], rope_cos, rope_sin)

    saved = []
    cur = xs
    for i in range(DEPTH):
        j = i // 2
        st = {"x0": cur}
        cur, st["a1"], st["u1"] = _ffn_fwd(cur, ffn1_norm[i:i + 1], w[("ffn1_w_gate", i)], w[("ffn1_w_up", i)],
                                           w[("ffn1_w_down", i)], f"ffn1_fwd_{i}")
        st["x1"] = cur
        if i % 2 == 0:
            cur = _pool_fwd(cur, mix_norm[i:i + 1], w[("pool_w", j)], pool_scale[j:j + 1], f"pool_fwd_{i}")
        else:
            st["q"], st["k"], st["v"], vt = _mla_pre_fwd(cur, mix_norm[i:i + 1], *mla_weights(j), f"mla_pre_fwd_{i}")
            st["o"], lse = _flash_fwd(st["q"], st["k"], vt, f"flash_fwd_{i}")
            tq_bwd = _tiles(n)["bwd_q"]
            st["lse"] = lse.reshape(N_HEADS, n // tq_bwd, 1, tq_bwd)
            cur = _mla_post_fwd(st["o"], cur, w[("mla_w_out", j)], f"mla_post_fwd_{i}")
        st["x2"] = cur
        cur, st["a2"], st["u2"] = _ffn_fwd(cur, ffn2_norm[i:i + 1], w[("ffn2_w_gate", i)], w[("ffn2_w_up", i)],
                                           w[("ffn2_w_down", i)], f"ffn2_fwd_{i}")
        saved.append(st)

    dcur, sq_err = _loss_head(cur, target, "loss_head")
    loss_part = 0.5 * jnp.sum(sq_err) * (1.0 / D_MODEL)

    gr = {}
    small = {k: [None] * DEPTH for k in ("ffn1_norm", "mix_norm", "ffn2_norm")}
    small.update({k: [None] * (DEPTH // 2) for k in ("pool_scale", "mla_q_norm", "mla_kv_norm", "mla_q_head_norm",
                                                     "mla_k_head_norm")})

    def ffn_backward(f, i, x_in, a, u, gain, dout):
        dx, h, dob, y, da, du, dg = _ffn_bwd(x_in, dout, a, u, gain, w[(f + "_w_gate", i)], w[(f + "_w_up", i)],
                                             w[(f + "_w_down", i)], f"{f}_bwd_{i}")
        gr[(f + "_w_gate", i)] = _tn_matmul(da, h, f"{f}_dgate_{i}")
        gr[(f + "_w_up", i)] = _tn_matmul(du, h, f"{f}_dup_{i}")
        gr[(f + "_w_down", i)] = _tn_matmul(y, dob, f"{f}_ddown_{i}")
        small[f + "_norm"][i] = jnp.sum(dg, axis=0)
        return dx

    for i in reversed(range(DEPTH)):
        j = i // 2
        st = saved[i]
        dcur = ffn_backward("ffn2", i, st["x2"], st["a2"], st["u2"], ffn2_norm[i:i + 1], dcur)
        if i % 2 == 0:
            dcur, dpw, dsc, dg = _pool_bwd(st["x1"], dcur, mix_norm[i:i + 1], w[("pool_w", j)], pool_scale[j:j + 1],
                                           f"pool_bwd_{i}")
            gr[("pool_w", j)] = dpw
            small["pool_scale"][j] = jnp.sum(dsc, axis=0)
            small["mix_norm"][i] = jnp.sum(dg, axis=0)
        else:
            dob, dpo, delta = _mla_out_bwd(dcur, st["o"], w[("mla_w_out", j)], f"mla_out_bwd_{i}")
            gr[("mla_w_out", j)] = _tn_matmul(st["o"], dob, f"mla_dout_{i}")
            dk, dv, dq = _flash_bwd(st["q"], st["k"], st["v"], dpo, st["lse"], delta, f"flash_bwd_{i}")
            (dcur, h, dlat, cqn, dqp, ckvn, dkv, dg, dgq, dgkv, dghq, dghk) = _mla_pre_bwd(
                st["x1"], dcur, dq, dk, dv, mix_norm[i:i + 1], *mla_weights(j), f"mla_pre_bwd_{i}")
            gr[("mla_w_in", j)] = _tn_matmul(dlat, h, f"mla_din_{i}")
            gr[("mla_w_q_up", j)] = _tn_matmul(dqp, cqn, f"mla_dqup_{i}")
            gr[("mla_w_kv_up", j)] = _tn_matmul(dkv, ckvn, f"mla_dkvup_{i}")
            small["mix_norm"][i] = jnp.sum(dg, axis=0)
            small["mla_q_norm"][j] = _pad_lanes(jnp.sum(dgq, axis=0)[None], D_MODEL)[0]
            small["mla_kv_norm"][j] = _pad_lanes(jnp.sum(dgkv, axis=0)[None], D_MODEL)[0]
            small["mla_q_head_norm"][j] = _pad_lanes(jnp.sum(dghq, axis=0)[None], D_MODEL)[0]
            small["mla_k_head_norm"][j] = _pad_lanes(jnp.sum(dghk, axis=0)[None], D_MODEL)[0]
        dcur = ffn_backward("ffn1", i, st["x0"], st["a1"], st["u1"], ffn1_norm[i:i + 1], dcur)
    grad_x = dcur[None]

    flat = _reduce_scatter(_pack_grads(gr))
    grads = _unpack_shard_grads(flat)

    small_order = ("ffn1_norm", "mix_norm", "ffn2_norm", "pool_scale", "mla_q_norm", "mla_kv_norm",
                   "mla_q_head_norm", "mla_k_head_norm")
    rows = [r for k in small_order for r in small[k]]
    rows.append(jnp.zeros((D_MODEL,), F32).at[0].set(loss_part))
    rows.append(jnp.zeros((D_MODEL,), F32))
    small_sum = _sum_parts(_all_gather(jnp.stack(rows), "small_all_gather"), "small_sum")
    loss = small_sum[SM_ROWS - 2, 0]
    off = 0
    for k in small_order:
        cnt = len(small[k])
        g = small_sum[off:off + cnt]
        off += cnt
        if k == "mla_q_norm":
            g = lax.dynamic_slice_in_dim(g[:, :Q_LORA], dev * (Q_LORA // N_DEV), Q_LORA // N_DEV, axis=1)
        elif k == "mla_kv_norm":
            g = lax.dynamic_slice_in_dim(g[:, :KV_LORA], dev * (KV_LORA // N_DEV), KV_LORA // N_DEV, axis=1)
        elif k in ("mla_q_head_norm", "mla_k_head_norm"):
            g = g[:, :QK_HEAD]
        grads[k] = g

    deltas, new_m, new_v = {}, {}, {}
    for k in WEIGHTS:
        shape = p[k].shape
        view = (-1, shape[-1])
        d, mn, vn = _adamw(p[k].reshape(view), grads[k].reshape(view), moments_m[k].reshape(view),
                           moments_v[k].reshape(view), "adamw_" + k)
        deltas[k], new_m[k], new_v[k] = d.reshape(shape), mn.reshape(shape), vn.reshape(shape)
        grads[k] = grads[k].reshape(shape)

    return (loss, grad_x, *[grads[k] for k in WEIGHTS], *[deltas[k] for k in WEIGHTS],
            *[new_m[k] for k in WEIGHTS], *[new_v[k] for k in WEIGHTS])
```

```python
import functools

import jax
import jax.numpy as jnp
from jax import lax
from jax.experimental import pallas as pl
from jax.experimental.pallas import tpu as pltpu

F32 = jnp.float32
BF16 = jnp.bfloat16

D_MODEL = 1024
DEPTH = 4
D_FF = 2816
POOL_WINDOWS = (2, 4, 8, 16)
POOL_GROUP = 256
POOL_HALO = 16
N_HEADS = 8
QK_NOPE = 128
QK_ROPE = 64
QK_HEAD = 192
V_HEAD = 128
Q_LORA = 768
KV_LORA = 256
ROPE_THETA = 10000.0
EPS = 1e-6
FFN_HALF = 0.5
ADAM_LR = 0.001
ADAM_B1 = 0.9
ADAM_B2 = 0.999
ADAM_EPS = 1e-08
ADAM_WD = 0.01
ADAM_STEP = 10

N_DEV = 8
HEAD_PAD = 256
LAT_PAD = 1152
VT_ROWS = 144
LOG2_E = 1.4426950408889634
ATTN_SCALE = QK_HEAD ** -0.5
LOGIT_SCALE = ATTN_SCALE * LOG2_E
V7X_VMEM_LIMIT = 56 * 1024 * 1024
FF_CHUNK = 256
SM_ROWS = 24

NAMES = ['x', 'positions', 'ffn1_norm', 'ffn1_w_gate', 'ffn1_w_up', 'ffn1_w_down', 'mix_norm', 'pool_w',
         'pool_scale', 'mla_w_in', 'mla_q_norm', 'mla_w_q_up', 'mla_kv_norm', 'mla_w_kv_up', 'mla_q_head_norm',
         'mla_k_head_norm', 'mla_w_out', 'ffn2_norm', 'ffn2_w_gate', 'ffn2_w_up', 'ffn2_w_down']
WEIGHTS = NAMES[2:]


def _tiles(n):
    return dict(ffn_fwd=min(512, n), ffn_bwd=min(256, n), fwd_k=min(512, n // 2), bwd_q=min(512, n // 2),
                mla_bwd=min(256, n), pool=min(512, n), tn=min(512, n), rows=min(1024, n))


def _params(*sem):
    return pltpu.CompilerParams(dimension_semantics=sem, vmem_limit_bytes=V7X_VMEM_LIMIT)


def _dot(a, b):
    return jnp.dot(a, b, preferred_element_type=F32)


def _dot_nt(a, b):
    return lax.dot_general(a, b, (((1,), (1,)), ((), ())), preferred_element_type=F32)


def _dot_tn(a, b):
    return lax.dot_general(a, b, (((0,), (0,)), ((), ())), preferred_element_type=F32)


def _rowsum8(v):
    rows, w = v.shape
    return jnp.sum(v.reshape(rows // 8, 8, w), axis=0)


def _sigmoid(a):
    return 1.0 / (1.0 + jnp.exp(-a))


def _rms_fwd(x, width=None):
    width = x.shape[-1] if width is None else width
    r = lax.rsqrt(jnp.sum(x * x, axis=-1, keepdims=True) * (1.0 / width) + EPS)
    return x * r, r


def _rms_bwd(dy, xhat, r, gain, width=None):
    width = xhat.shape[-1] if width is None else width
    t = dy * gain
    return r * (t - xhat * (jnp.sum(t * xhat, axis=-1, keepdims=True) * (1.0 / width)))


def _full(shape):
    return pl.BlockSpec(shape, lambda *_: (0,) * len(shape))


def _load_weights(srcs, dsts, sems):
    copies = [pltpu.make_async_copy(s, d, sems.at[i]) for i, (s, d) in enumerate(zip(srcs, dsts))]
    for cp in copies:
        cp.start()
    for cp in copies:
        cp.wait()


def _ffn_fwd(x, gain, wg_t, wu_t, wd, name):
    n = x.shape[0]
    tm = _tiles(n)["ffn_fwd"]

    def body(x_ref, g_ref, wg_hbm, wu_hbm, wd_hbm, out_ref, a_ref, u_ref, wg_v, wu_v, wd_v, sems):
        @pl.when(pl.program_id(0) == 0)
        def _():
            _load_weights((wg_hbm, wu_hbm, wd_hbm), (wg_v, wu_v, wd_v), sems)

        xt = x_ref[...]
        xhat, _ = _rms_fwd(xt)
        h = (xhat * g_ref[...]).astype(BF16)
        acc = jnp.zeros((tm, D_MODEL), F32)
        for c in range(D_FF // FF_CHUNK):
            sl = pl.ds(c * FF_CHUNK, FF_CHUNK)
            a = _dot_nt(h, wg_v[sl, :])
            u = _dot_nt(h, wu_v[sl, :])
            a_ref[:, sl] = a.astype(BF16)
            u_ref[:, sl] = u.astype(BF16)
            y = (a * _sigmoid(a) * u).astype(BF16)
            acc = acc + _dot(y, wd_v[sl, :])
        out_ref[...] = xt + FFN_HALF * acc

    any_spec = pl.BlockSpec(memory_space=pl.ANY)
    return pl.pallas_call(
        body, name=name, grid=(n // tm,),
        in_specs=[pl.BlockSpec((tm, D_MODEL), lambda i: (i, 0)), _full((1, D_MODEL)), any_spec, any_spec, any_spec],
        out_specs=[pl.BlockSpec((tm, D_MODEL), lambda i: (i, 0)), pl.BlockSpec((tm, D_FF), lambda i: (i, 0)),
                   pl.BlockSpec((tm, D_FF), lambda i: (i, 0))],
        out_shape=[jax.ShapeDtypeStruct((n, D_MODEL), F32), jax.ShapeDtypeStruct((n, D_FF), BF16),
                   jax.ShapeDtypeStruct((n, D_FF), BF16)],
        scratch_shapes=[pltpu.VMEM((D_FF, D_MODEL), BF16)] * 3 + [pltpu.SemaphoreType.DMA((3,))],
        compiler_params=_params("arbitrary"),
    )(x, gain, wg_t, wu_t, wd)


def _ffn_bwd(x, dout, a, u, gain, wg_t, wu_t, wd, name):
    n = x.shape[0]
    tm = _tiles(n)["ffn_bwd"]

    def body(x_ref, do_ref, a_ref, u_ref, g_ref, wg_hbm, wu_hbm, wd_hbm,
             dx_ref, h_ref, dob_ref, y_ref, da_ref, du_ref, dg_ref, wg_v, wu_v, wd_v, sems):
        @pl.when(pl.program_id(0) == 0)
        def _():
            _load_weights((wg_hbm, wu_hbm, wd_hbm), (wg_v, wu_v, wd_v), sems)
            dg_ref[...] = jnp.zeros_like(dg_ref)

        xt = x_ref[...]
        g = g_ref[...]
        xhat, r = _rms_fwd(xt)
        h_ref[...] = (xhat * g).astype(BF16)
        dout = do_ref[...]
        dob = (FFN_HALF * dout).astype(BF16)
        dob_ref[...] = dob
        dh = jnp.zeros((tm, D_MODEL), F32)
        for c in range(D_FF // FF_CHUNK):
            sl = pl.ds(c * FF_CHUNK, FF_CHUNK)
            dy = _dot_nt(dob, wd_v[sl, :])
            av = a_ref[:, sl].astype(F32)
            uv = u_ref[:, sl].astype(F32)
            s = _sigmoid(av)
            silu = av * s
            y_ref[:, sl] = (silu * uv).astype(BF16)
            du = (dy * silu).astype(BF16)
            da = (dy * uv * (s * (1.0 + av * (1.0 - s)))).astype(BF16)
            du_ref[:, sl] = du
            da_ref[:, sl] = da
            dh = dh + _dot(da, wg_v[sl, :]) + _dot(du, wu_v[sl, :])
        dg_ref[...] += _rowsum8(dh * xhat)
        dx_ref[...] = dout + _rms_bwd(dh, xhat, r, g)

    any_spec = pl.BlockSpec(memory_space=pl.ANY)
    row_d = pl.BlockSpec((tm, D_MODEL), lambda i: (i, 0))
    row_f = pl.BlockSpec((tm, D_FF), lambda i: (i, 0))
    return pl.pallas_call(
        body, name=name, grid=(n // tm,),
        in_specs=[row_d, row_d, row_f, row_f, _full((1, D_MODEL)), any_spec, any_spec, any_spec],
        out_specs=[row_d, row_d, row_d, row_f, row_f, row_f, _full((8, D_MODEL))],
        out_shape=[jax.ShapeDtypeStruct((n, D_MODEL), F32), jax.ShapeDtypeStruct((n, D_MODEL), BF16),
                   jax.ShapeDtypeStruct((n, D_MODEL), BF16), jax.ShapeDtypeStruct((n, D_FF), BF16),
                   jax.ShapeDtypeStruct((n, D_FF), BF16), jax.ShapeDtypeStruct((n, D_FF), BF16),
                   jax.ShapeDtypeStruct((8, D_MODEL), F32)],
        scratch_shapes=[pltpu.VMEM((D_FF, D_MODEL), BF16)] * 3 + [pltpu.SemaphoreType.DMA((3,))],
        compiler_params=_params("arbitrary"),
    )(x, dout, a, u, gain, wg_t, wu_t, wd)


def _tn_matmul(a, b, name):
    n, fa = a.shape
    db = b.shape[1]
    tk = _tiles(n)["tn"]
    tf = fa // 2 if (fa // 2) % 128 == 0 and fa > 1024 else fa

    def body(a_ref, b_ref, o_ref):
        @pl.when(pl.program_id(1) == 0)
        def _():
            o_ref[...] = jnp.zeros_like(o_ref)

        o_ref[...] += _dot_tn(a_ref[...], b_ref[...])

    return pl.pallas_call(
        body, name=name, grid=(fa // tf, n // tk),
        in_specs=[pl.BlockSpec((tk, tf), lambda i, k: (k, i)), pl.BlockSpec((tk, db), lambda i, k: (k, 0))],
        out_specs=pl.BlockSpec((tf, db), lambda i, k: (i, 0)),
        out_shape=jax.ShapeDtypeStruct((fa, db), F32),
        compiler_params=_params("arbitrary", "arbitrary"),
    )(a, b)


def _loss_head(y, target, name):
    n = y.shape[0]
    tm = _tiles(n)["rows"]

    def body(y_ref, t_ref, d_ref, acc_ref):
        @pl.when(pl.program_id(0) == 0)
        def _():
            acc_ref[...] = jnp.zeros_like(acc_ref)

        d = y_ref[...] - t_ref[...]
        d_ref[...] = d * (1.0 / D_MODEL)
        acc_ref[...] += _rowsum8(d * d)

    row = pl.BlockSpec((tm, D_MODEL), lambda i: (i, 0))
    return pl.pallas_call(
        body, name=name, grid=(n // tm,), in_specs=[row, row], out_specs=[row, _full((8, D_MODEL))],
        out_shape=[jax.ShapeDtypeStruct((n, D_MODEL), F32), jax.ShapeDtypeStruct((8, D_MODEL), F32)],
        compiler_params=_params("arbitrary"),
    )(y, target)


def _window_sum(v, w, rows, forward):
    s = v
    sh = 1
    while sh < w:
        s = s + pltpu.roll(s, (rows - sh) if forward else sh, 0)
        sh *= 2
    return s


def _pool_fwd(x, gain, w, scale, name):
    n = x.shape[0]
    tm = _tiles(n)["pool"]
    hb = tm // POOL_HALO
    rows = tm + POOL_HALO

    def body(x_ref, xh_ref, g_ref, w_ref, sc_ref, out_ref):
        i = pl.program_id(0)
        xt = x_ref[...]
        e = jnp.concatenate([xh_ref[...], xt], axis=0)
        xhat, _ = _rms_fwd(e)
        row = lax.broadcasted_iota(jnp.int32, (rows, 1), 0)
        hn = jnp.where((row >= POOL_HALO) | (i > 0), xhat * g_ref[...], 0.0)
        t_glob = i * tm + row - POOL_HALO
        outs = []
        for gi, win in enumerate(POOL_WINDOWS):
            ug = hn[:, gi * POOL_GROUP:(gi + 1) * POOL_GROUP]
            cnt = jnp.maximum(jnp.minimum(t_glob + 1, win), 1).astype(F32)
            pooled = (_window_sum(ug, win, rows, False) / cnt - ug)[POOL_HALO:]
            outs.append(_dot(pooled.astype(BF16), w_ref[gi]))
        out_ref[...] = xt + jnp.concatenate(outs, axis=1) * sc_ref[...]

    return pl.pallas_call(
        body, name=name, grid=(n // tm,),
        in_specs=[pl.BlockSpec((tm, D_MODEL), lambda i: (i, 0)),
                  pl.BlockSpec((POOL_HALO, D_MODEL), lambda i: (jnp.maximum(i * hb - 1, 0), 0)),
                  _full((1, D_MODEL)), _full((4, POOL_GROUP, POOL_GROUP)), _full((1, D_MODEL))],
        out_specs=pl.BlockSpec((tm, D_MODEL), lambda i: (i, 0)),
        out_shape=jax.ShapeDtypeStruct((n, D_MODEL), F32),
        compiler_params=_params("arbitrary"),
    )(x, x, gain, w, scale)


def _pool_bwd(x, dout, gain, w, scale, name):
    n = x.shape[0]
    tm = _tiles(n)["pool"]
    hb = tm // POOL_HALO
    rows = tm + POOL_HALO
    nt = n // tm

    def body(x_ref, xh_ref, do_ref, doh_ref, g_ref, w_ref, sc_ref, dx_ref, dw_ref, dsc_ref, dg_ref):
        i = pl.program_id(0)

        @pl.when(i == 0)
        def _():
            dw_ref[...] = jnp.zeros_like(dw_ref)
            dsc_ref[...] = jnp.zeros_like(dsc_ref)
            dg_ref[...] = jnp.zeros_like(dg_ref)

        xt = x_ref[...]
        g = g_ref[...]
        e = jnp.concatenate([xh_ref[...], xt], axis=0)
        xhat_e, r_e = _rms_fwd(e)
        row = lax.broadcasted_iota(jnp.int32, (rows, 1), 0)
        hn = jnp.where((row >= POOL_HALO) | (i > 0), xhat_e * g, 0.0)
        t_prev = i * tm + row - POOL_HALO
        t_next = i * tm + row
        dout = do_ref[...]
        dt = jnp.concatenate([dout, doh_ref[...]], axis=0)
        dt = jnp.where((row < tm) | (i < nt - 1), dt, 0.0)
        dyr = dt * sc_ref[...]
        dus, dscs = [], []
        for gi, win in enumerate(POOL_WINDOWS):
            lanes = slice(gi * POOL_GROUP, (gi + 1) * POOL_GROUP)
            ug = hn[:, lanes]
            cnt = jnp.maximum(jnp.minimum(t_prev + 1, win), 1).astype(F32)
            pooled = (_window_sum(ug, win, rows, False) / cnt - ug)[POOL_HALO:].astype(BF16)
            yraw = _dot(pooled, w_ref[gi])
            dscs.append(_rowsum8(dout[:, lanes] * yraw))
            dyr_b = dyr[:, lanes].astype(BF16)
            dw_ref[gi] += _dot_tn(pooled, dyr_b[:tm])
            dpool = _dot_nt(dyr_b, w_ref[gi])
            cnt2 = jnp.minimum(t_next + 1, win).astype(F32)
            dus.append((_window_sum(dpool / cnt2, win, rows, True) - dpool)[:tm])
        dsc_ref[...] += jnp.concatenate(dscs, axis=1)
        dh = jnp.concatenate(dus, axis=1)
        xhat = xhat_e[POOL_HALO:]
        dg_ref[...] += _rowsum8(dh * xhat)
        dx_ref[...] = dout + _rms_bwd(dh, xhat, r_e[POOL_HALO:], g)

    row_d = pl.BlockSpec((tm, D_MODEL), lambda i: (i, 0))
    prev_h = pl.BlockSpec((POOL_HALO, D_MODEL), lambda i: (jnp.maximum(i * hb - 1, 0), 0))
    next_h = pl.BlockSpec((POOL_HALO, D_MODEL), lambda i: (jnp.minimum((i + 1) * hb, n // POOL_HALO - 1), 0))
    return pl.pallas_call(
        body, name=name, grid=(nt,),
        in_specs=[row_d, prev_h, row_d, next_h, _full((1, D_MODEL)), _full((4, POOL_GROUP, POOL_GROUP)),
                  _full((1, D_MODEL))],
        out_specs=[row_d, _full((4, POOL_GROUP, POOL_GROUP)), _full((8, D_MODEL)), _full((8, D_MODEL))],
        out_shape=[jax.ShapeDtypeStruct((n, D_MODEL), F32), jax.ShapeDtypeStruct((4, POOL_GROUP, POOL_GROUP), F32),
                   jax.ShapeDtypeStruct((8, D_MODEL), F32), jax.ShapeDtypeStruct((8, D_MODEL), F32)],
        compiler_params=_params("arbitrary"),
    )(x, x, dout, dout, gain, w, scale)


def _rope(v, cos, sin_signed):
    lo, hi = v[:, :128], v[:, 128:]
    lane = lax.broadcasted_iota(jnp.int32, hi.shape, 1)
    swapped = jnp.where(lane < 32, pltpu.roll(hi, 96, 1), pltpu.roll(hi, 32, 1))
    return jnp.concatenate([lo, hi * cos + swapped * sin_signed], axis=1)


def _rope_bwd(gr, cos, sin_signed):
    lo, hi = gr[:, :128], gr[:, 128:]
    t = hi * sin_signed
    lane = lax.broadcasted_iota(jnp.int32, hi.shape, 1)
    swapped = jnp.where(lane < 32, pltpu.roll(t, 96, 1), pltpu.roll(t, 32, 1))
    return jnp.concatenate([lo, hi * cos + swapped], axis=1)


def _mla_latents(h, win_ref):
    cq = _dot_nt(h, win_ref[0:Q_LORA, :])
    ckv = _dot_nt(h, win_ref[Q_LORA:Q_LORA + KV_LORA, :])
    kpe = _dot_nt(h, win_ref[Q_LORA + KV_LORA:LAT_PAD, :])
    return cq, ckv, kpe


def _mla_pre_fwd(x, gain, win, gq, gkv, wq, wkv, ghq, ghk, cos, sin_signed, name):
    n = x.shape[0]
    tm = _tiles(n)["fwd_k"]

    def body(x_ref, g_ref, win_ref, gq_ref, gkv_ref, wq_ref, wkv_ref, ghq_ref, ghk_ref, c_ref, s_ref,
             q_ref, k_ref, v_ref, vt_ref):
        xhat, _ = _rms_fwd(x_ref[...])
        h = (xhat * g_ref[...]).astype(BF16)
        cq, ckv, kpe = _mla_latents(h, win_ref)
        cqn = (_rms_fwd(cq)[0] * gq_ref[...]).astype(BF16)
        ckvn = (_rms_fwd(ckv)[0] * gkv_ref[...]).astype(BF16)
        cos, sn = c_ref[...], s_ref[...]
        for hd in range(N_HEADS):
            rws = pl.ds(hd * HEAD_PAD, HEAD_PAD)
            qh = _dot_nt(cqn, wq_ref[rws, :])
            qn = _rms_fwd(qh, QK_HEAD)[0] * ghq_ref[...]
            q_ref[hd] = (_rope(qn, cos, sn) * LOGIT_SCALE).astype(BF16)
            kvh = _dot_nt(ckvn, wkv_ref[rws, :])
            kpre = jnp.concatenate([kvh[:, :QK_NOPE], kpe], axis=1)
            kn = _rms_fwd(kpre, QK_HEAD)[0] * ghk_ref[...]
            k_ref[hd] = _rope(kn, cos, sn).astype(BF16)
            vh = kvh[:, QK_NOPE:]
            v_ref[hd] = vh.astype(BF16)
            vt_ref[hd, 0] = jnp.concatenate([vh.T, jnp.ones((VT_ROWS - V_HEAD, tm), F32)], axis=0).astype(BF16)

    row = lambda w: pl.BlockSpec((tm, w), lambda i: (i, 0))
    head = lambda w: pl.BlockSpec((N_HEADS, tm, w), lambda i: (0, i, 0))
    return pl.pallas_call(
        body, name=name, grid=(n // tm,),
        in_specs=[row(D_MODEL), _full((1, D_MODEL)), _full((LAT_PAD, D_MODEL)), _full((1, Q_LORA)),
                  _full((1, KV_LORA)), _full((N_HEADS * HEAD_PAD, Q_LORA)), _full((N_HEADS * HEAD_PAD, KV_LORA)),
                  _full((1, HEAD_PAD)), _full((1, HEAD_PAD)), row(128), row(128)],
        out_specs=[head(HEAD_PAD), head(HEAD_PAD), head(V_HEAD),
                   pl.BlockSpec((N_HEADS, 1, VT_ROWS, tm), lambda i: (0, i, 0, 0))],
        out_shape=[jax.ShapeDtypeStruct((N_HEADS, n, HEAD_PAD), BF16), jax.ShapeDtypeStruct((N_HEADS, n, HEAD_PAD), BF16),
                   jax.ShapeDtypeStruct((N_HEADS, n, V_HEAD), BF16),
                   jax.ShapeDtypeStruct((N_HEADS, n // tm, VT_ROWS, tm), BF16)],
        compiler_params=_params("arbitrary"),
    )(x, gain, win, gq, gkv, wq, wkv, ghq, ghk, cos, sin_signed)


def _flash_fwd(q, k, vt, name):
    n = q.shape[1]
    tk = _tiles(n)["fwd_k"]
    tq = 2 * tk
    nq = n // tq

    def body(q_ref, k_ref, vt_ref, o_ref, lse_ref, s_scr, m_scr, acc_scr):
        i = pl.program_id(1)
        qi = q_ref[0]

        def scores(j, slot):
            s_scr[slot] = _dot_nt(k_ref[0, pl.ds(pl.multiple_of(j * tk, tk), tk), :], qi)

        def update(j, slot, diagonal=None):
            s = s_scr[slot]
            if diagonal is not None:
                krow = lax.broadcasted_iota(jnp.int32, (tk, tq), 0) + diagonal * tk
                qcol = lax.broadcasted_iota(jnp.int32, (tk, tq), 1)
                s = jnp.where(krow <= qcol, s, -jnp.inf)
            m = m_scr[...]
            m_new = jnp.maximum(m, jnp.max(s, axis=0, keepdims=True))
            p = jnp.exp2(s - m_new).astype(BF16)
            acc_scr[...] = jnp.exp2(m - m_new) * acc_scr[...] + _dot(vt_ref[0, j], p)
            m_scr[...] = m_new

        m_scr[...] = jnp.full((1, tq), -jnp.inf, F32)
        acc_scr[...] = jnp.zeros((VT_ROWS, tq), F32)
        scores(0, 0)

        def pair(jj, carry):
            scores(2 * jj + 1, 1)
            update(2 * jj, 0)
            scores(2 * jj + 2, 0)
            update(2 * jj + 1, 1)
            return carry

        lax.fori_loop(0, i, pair, 0)
        scores(2 * i + 1, 1)
        update(2 * i, 0, diagonal=0)
        update(2 * i + 1, 1, diagonal=1)
        l = acc_scr[V_HEAD:V_HEAD + 1, :]
        o_ref[...] = (acc_scr[0:V_HEAD, :] / l).T.astype(BF16)
        lse_ref[0, 0] = m_scr[...] + jnp.log2(l)

    return pl.pallas_call(
        body, name=name, grid=(N_HEADS, nq),
        in_specs=[pl.BlockSpec((1, tq, HEAD_PAD), lambda h, i: (h, i, 0)),
                  pl.BlockSpec((1, n, HEAD_PAD), lambda h, i: (h, 0, 0)),
                  pl.BlockSpec((1, n // tk, VT_ROWS, tk), lambda h, i: (h, 0, 0, 0))],
        out_specs=[pl.BlockSpec((tq, V_HEAD), lambda h, i: (i, h)),
                   pl.BlockSpec((1, 1, 1, tq), lambda h, i: (h, i, 0, 0))],
        out_shape=[jax.ShapeDtypeStruct((n, N_HEADS * V_HEAD), BF16), jax.ShapeDtypeStruct((N_HEADS, nq, 1, tq), F32)],
        scratch_shapes=[pltpu.VMEM((2, tk, tq), F32), pltpu.VMEM((1, tq), F32), pltpu.VMEM((VT_ROWS, tq), F32)],
        compiler_params=_params("arbitrary", "arbitrary"),
    )(q, k, vt)


def _mla_post_fwd(o, x, wout, name):
    n = x.shape[0]
    tm = _tiles(n)["rows"]

    def body(o_ref, x_ref, w_ref, out_ref):
        out_ref[...] = x_ref[...] + _dot(o_ref[...], w_ref[...])

    row = pl.BlockSpec((tm, D_MODEL), lambda i: (i, 0))
    return pl.pallas_call(
        body, name=name, grid=(n // tm,), in_specs=[row, row, _full((D_MODEL, D_MODEL))], out_specs=row,
        out_shape=jax.ShapeDtypeStruct((n, D_MODEL), F32), compiler_params=_params("arbitrary"),
    )(o, x, wout)


def _mla_out_bwd(dout, o, wout, name):
    n = dout.shape[0]
    t = _tiles(n)["bwd_q"]
    nq = n // t

    def body(do_ref, o_ref, w_ref, dob_ref, dpo_ref, dl_ref):
        dob = do_ref[...].astype(BF16)
        dob_ref[...] = dob
        dpo = _dot_nt(dob, w_ref[...])
        dpo_ref[...] = dpo.astype(BF16)
        ov = o_ref[...].astype(F32)
        for hd in range(N_HEADS):
            lanes = slice(hd * V_HEAD, (hd + 1) * V_HEAD)
            prod = dpo[:, lanes] * ov[:, lanes]
            dl_ref[hd, 0] = jnp.sum(prod.T, axis=0, keepdims=True)

    row = pl.BlockSpec((t, D_MODEL), lambda i: (i, 0))
    return pl.pallas_call(
        body, name=name, grid=(nq,), in_specs=[row, row, _full((D_MODEL, D_MODEL))],
        out_specs=[row, row, pl.BlockSpec((N_HEADS, 1, 1, t), lambda i: (0, i, 0, 0))],
        out_shape=[jax.ShapeDtypeStruct((n, D_MODEL), BF16), jax.ShapeDtypeStruct((n, D_MODEL), BF16),
                   jax.ShapeDtypeStruct((N_HEADS, nq, 1, t), F32)],
        compiler_params=_params("arbitrary"),
    )(dout, o, wout)


def _flash_bwd(q, k, v, dpo, lse, delta, name):
    n = q.shape[1]
    tq = _tiles(n)["bwd_q"]
    tk = 2 * tq
    nk = n // tk
    nqb = n // tq

    def body(k_ref, v_ref, q_ref, do_ref, lse_ref, dl_ref, dk_ref, dv_ref, dq_hbm,
             dq_acc, s_scr, dp_scr, dk_acc, dv_acc, sem):
        h = pl.program_id(0)
        j = pl.program_id(1)

        @pl.when(j == 0)
        def _():
            dq_acc[...] = jnp.zeros_like(dq_acc)

        dk_acc[...] = jnp.zeros_like(dk_acc)
        dv_acc[...] = jnp.zeros_like(dv_acc)
        kj = k_ref[0]
        vj = v_ref[0]
        npairs = nk - 1 - j

        def block(t):
            return jnp.where(t < 2 * npairs, 2 * j + 2 + t, 2 * j + (t - 2 * npairs))

        def scores(i, slot):
            rws = pl.ds(pl.multiple_of(i * tq, tq), tq)
            s_scr[slot] = _dot_nt(kj, q_ref[0, rws, :])
            dp_scr[slot] = _dot_nt(vj, do_ref[rws, :])

        def update(i, slot, diagonal=None):
            rws = pl.ds(pl.multiple_of(i * tq, tq), tq)
            p = jnp.exp2(s_scr[slot] - lse_ref[0, i])
            if diagonal is not None:
                krow = lax.broadcasted_iota(jnp.int32, (tk, tq), 0)
                qcol = lax.broadcasted_iota(jnp.int32, (tk, tq), 1) + diagonal * tq
                p = jnp.where(krow <= qcol, p, 0.0)
            dv_acc[...] += _dot(p.astype(BF16), do_ref[rws, :])
            ds = (p * (dp_scr[slot] - dl_ref[0, i])).astype(BF16)
            dk_acc[...] += _dot(ds, q_ref[0, rws, :])
            dq_acc[rws, :] += _dot_tn(ds, kj)

        scores(block(0), 0)

        def pair(jj, carry):
            scores(block(2 * jj + 1), 1)
            update(block(2 * jj), 0)
            scores(block(2 * jj + 2), 0)
            update(block(2 * jj + 1), 1)
            return carry

        lax.fori_loop(0, npairs, pair, 0)
        scores(2 * j + 1, 1)
        update(2 * j, 0, diagonal=0)
        update(2 * j + 1, 1, diagonal=1)
        dk_ref[0] = dk_acc[...] * (ATTN_SCALE / LOGIT_SCALE)
        dv_ref[0] = dv_acc[...]

        @pl.when(j == nk - 1)
        def _():
            dq_acc[...] = dq_acc[...] * ATTN_SCALE
            cp = pltpu.make_async_copy(dq_acc, dq_hbm.at[h], sem)
            cp.start()
            cp.wait()

    resident = dict(pipeline_mode=pl.Buffered(1))
    return pl.pallas_call(
        body, name=name, grid=(N_HEADS, nk),
        in_specs=[pl.BlockSpec((1, tk, HEAD_PAD), lambda h, j: (h, j, 0)),
                  pl.BlockSpec((1, tk, V_HEAD), lambda h, j: (h, j, 0)),
                  pl.BlockSpec((1, n, HEAD_PAD), lambda h, j: (h, 0, 0), **resident),
                  pl.BlockSpec((n, V_HEAD), lambda h, j: (0, h), **resident),
                  pl.BlockSpec((1, nqb, 1, tq), lambda h, j: (h, 0, 0, 0)),
                  pl.BlockSpec((1, nqb, 1, tq), lambda h, j: (h, 0, 0, 0))],
        out_specs=[pl.BlockSpec((1, tk, HEAD_PAD), lambda h, j: (h, j, 0)),
                   pl.BlockSpec((1, tk, V_HEAD), lambda h, j: (h, j, 0)),
                   pl.BlockSpec(memory_space=pl.ANY)],
        out_shape=[jax.ShapeDtypeStruct((N_HEADS, n, HEAD_PAD), F32), jax.ShapeDtypeStruct((N_HEADS, n, V_HEAD), F32),
                   jax.ShapeDtypeStruct((N_HEADS, n, HEAD_PAD), F32)],
        scratch_shapes=[pltpu.VMEM((n, HEAD_PAD), F32), pltpu.VMEM((2, tk, tq), F32), pltpu.VMEM((2, tk, tq), F32),
                        pltpu.VMEM((tk, HEAD_PAD), F32), pltpu.VMEM((tk, V_HEAD), F32), pltpu.SemaphoreType.DMA(())],
        compiler_params=_params("arbitrary", "arbitrary"),
    )(k, v, q, dpo, lse, delta)


def _mla_pre_bwd(x, dout, dq, dk, dv, gain, win, gq, gkv, wq, wkv, ghq, ghk, cos, sin_signed, name):
    n = x.shape[0]
    tm = _tiles(n)["mla_bwd"]
    hw = N_HEADS * HEAD_PAD

    def body(x_ref, do_ref, dq_ref, dk_ref, dv_ref, g_ref, win_ref, gq_ref, gkv_ref, wq_ref, wkv_ref, ghq_ref, ghk_ref,
             c_ref, s_ref, dx_ref, h_ref, dlat_ref, cqn_ref, dqp_ref, ckvn_ref, dkv_ref,
             dg_ref, dgq_ref, dgkv_ref, dghq_ref, dghk_ref):
        @pl.when(pl.program_id(0) == 0)
        def _():
            for ref in (dg_ref, dgq_ref, dgkv_ref, dghq_ref, dghk_ref):
                ref[...] = jnp.zeros_like(ref)

        g = g_ref[...]
        xhat, r = _rms_fwd(x_ref[...])
        h = (xhat * g).astype(BF16)
        h_ref[...] = h
        cq, ckv, kpe = _mla_latents(h, win_ref)
        cqhat, rcq = _rms_fwd(cq)
        ckvhat, rckv = _rms_fwd(ckv)
        cqn = (cqhat * gq_ref[...]).astype(BF16)
        ckvn = (ckvhat * gkv_ref[...]).astype(BF16)
        cqn_ref[...] = cqn
        ckvn_ref[...] = ckvn
        cos, sn = c_ref[...], s_ref[...]
        ghq, ghk = ghq_ref[...], ghk_ref[...]
        dcqn = jnp.zeros((tm, Q_LORA), F32)
        dckvn = jnp.zeros((tm, KV_LORA), F32)
        dkpe = jnp.zeros((tm, 128), F32)
        dghq = jnp.zeros((8, HEAD_PAD), F32)
        dghk = jnp.zeros((8, HEAD_PAD), F32)
        for hd in range(N_HEADS):
            rws = pl.ds(hd * HEAD_PAD, HEAD_PAD)
            lanes = slice(hd * HEAD_PAD, (hd + 1) * HEAD_PAD)
            qhat, rq = _rms_fwd(_dot_nt(cqn, wq_ref[rws, :]), QK_HEAD)
            gqn = _rope_bwd(dq_ref[hd], cos, sn)
            dghq = dghq + _rowsum8(gqn * qhat)
            dqpre = _rms_bwd(gqn, qhat, rq, ghq, QK_HEAD).astype(BF16)
            dqp_ref[:, lanes] = dqpre
            dcqn = dcqn + _dot(dqpre, wq_ref[rws, :])
            kvh = _dot_nt(ckvn, wkv_ref[rws, :])
            khat, rk = _rms_fwd(jnp.concatenate([kvh[:, :QK_NOPE], kpe], axis=1), QK_HEAD)
            gkn = _rope_bwd(dk_ref[hd], cos, sn)
            dghk = dghk + _rowsum8(gkn * khat)
            dkpre = _rms_bwd(gkn, khat, rk, ghk, QK_HEAD)
            dkpe = dkpe + dkpre[:, QK_NOPE:]
            dkvh = jnp.concatenate([dkpre[:, :QK_NOPE], dv_ref[hd]], axis=1).astype(BF16)
            dkv_ref[:, lanes] = dkvh
            dckvn = dckvn + _dot(dkvh, wkv_ref[rws, :])
        dghq_ref[...] += dghq
        dghk_ref[...] += dghk
        dgq_ref[...] += _rowsum8(dcqn * cqhat)
        dgkv_ref[...] += _rowsum8(dckvn * ckvhat)
        dlat = jnp.concatenate([_rms_bwd(dcqn, cqhat, rcq, gq_ref[...]), _rms_bwd(dckvn, ckvhat, rckv, gkv_ref[...]),
                                dkpe], axis=1).astype(BF16)
        dlat_ref[...] = dlat
        dh = _dot(dlat, win_ref[...])
        dg_ref[...] += _rowsum8(dh * xhat)
        dx_ref[...] = do_ref[...] + _rms_bwd(dh, xhat, r, g)

    row = lambda w: pl.BlockSpec((tm, w), lambda i: (i, 0))
    head = lambda w: pl.BlockSpec((N_HEADS, tm, w), lambda i: (0, i, 0))
    sds = jax.ShapeDtypeStruct
    return pl.pallas_call(
        body, name=name, grid=(n // tm,),
        in_specs=[row(D_MODEL), row(D_MODEL), head(HEAD_PAD), head(HEAD_PAD), head(V_HEAD), _full((1, D_MODEL)),
                  _full((LAT_PAD, D_MODEL)), _full((1, Q_LORA)), _full((1, KV_LORA)), _full((hw, Q_LORA)),
                  _full((hw, KV_LORA)), _full((1, HEAD_PAD)), _full((1, HEAD_PAD)), row(128), row(128)],
        out_specs=[row(D_MODEL), row(D_MODEL), row(LAT_PAD), row(Q_LORA), row(hw), row(KV_LORA), row(hw),
                   _full((8, D_MODEL)), _full((8, Q_LORA)), _full((8, KV_LORA)), _full((8, HEAD_PAD)),
                   _full((8, HEAD_PAD))],
        out_shape=[sds((n, D_MODEL), F32), sds((n, D_MODEL), BF16), sds((n, LAT_PAD), BF16), sds((n, Q_LORA), BF16),
                   sds((n, hw), BF16), sds((n, KV_LORA), BF16), sds((n, hw), BF16), sds((8, D_MODEL), F32),
                   sds((8, Q_LORA), F32), sds((8, KV_LORA), F32), sds((8, HEAD_PAD), F32), sds((8, HEAD_PAD), F32)],
        compiler_params=_params("arbitrary"),
    )(x, dout, dq, dk, dv, gain, win, gq, gkv, wq, wkv, ghq, ghk, cos, sin_signed)


def _adamw(w, g, m, v, name):
    rows, cols = w.shape
    tr = rows
    for cand in (512, 256, 128, 64, 32, 16, 8):
        if rows % cand == 0 and rows > cand:
            tr = cand
            break

    def body(w_ref, g_ref, m_ref, v_ref, d_ref, mo_ref, vo_ref):
        gv = g_ref[...]
        mn = ADAM_B1 * m_ref[...] + (1.0 - ADAM_B1) * gv
        vn = ADAM_B2 * v_ref[...] + (1.0 - ADAM_B2) * (gv * gv)
        m_hat = mn / (1.0 - ADAM_B1 ** ADAM_STEP)
        v_hat = vn / (1.0 - ADAM_B2 ** ADAM_STEP)
        d_ref[...] = -ADAM_LR * (m_hat / (jnp.sqrt(v_hat) + ADAM_EPS) + ADAM_WD * w_ref[...])
        mo_ref[...] = mn
        vo_ref[...] = vn

    blk = pl.BlockSpec((tr, cols), lambda i: (i, 0))
    return pl.pallas_call(
        body, name=name, grid=(rows // tr,), in_specs=[blk] * 4, out_specs=[blk] * 3,
        out_shape=[jax.ShapeDtypeStruct((rows, cols), F32)] * 3, compiler_params=_params("arbitrary"),
    )(w, g, m, v)


def _sum_parts(parts, name):
    k, r, c = parts.shape
    tr = min(r, 512)

    def body(p_ref, o_ref):
        acc = p_ref[0]
        for j in range(1, k):
            acc = acc + p_ref[j]
        o_ref[...] = acc

    return pl.pallas_call(
        body, name=name, grid=(r // tr,), in_specs=[pl.BlockSpec((k, tr, c), lambda i: (0, i, 0))],
        out_specs=pl.BlockSpec((tr, c), lambda i: (i, 0)), out_shape=jax.ShapeDtypeStruct((r, c), parts.dtype),
        compiler_params=_params("arbitrary"),
    )(parts)


def _row_tile(r, most=256):
    best = r
    for cand in range(8, most + 1, 8):
        if r % cand == 0:
            best = cand
    return best


def _sum_exchange(mine, landed, me, name):
    _, r, c = mine.shape
    tr = _row_tile(r)

    def body(me_ref, m_ref, l_ref, o_ref):
        acc = m_ref[0]
        for k in range(1, N_DEV):
            acc = acc + l_ref[k]
        o_ref[...] = acc

    return pl.pallas_call(
        body, name=name,
        grid_spec=pltpu.PrefetchScalarGridSpec(
            num_scalar_prefetch=1, grid=(r // tr,),
            in_specs=[pl.BlockSpec((1, tr, c), lambda i, me_ref: (me_ref[0], i, 0)),
                      pl.BlockSpec((N_DEV, tr, c), lambda i, me_ref: (0, i, 0))],
            out_specs=pl.BlockSpec((tr, c), lambda i, me_ref: (i, 0))),
        out_shape=jax.ShapeDtypeStruct((r, c), mine.dtype), compiler_params=_params("arbitrary"),
    )(me, mine, landed)


MESH = pl.DeviceIdType.MESH


def _all_gather(x, name):
    r, c = x.shape

    def body(x_ref, out_ref, send_sems, recv_sems, local_sem):
        mx, my, mc = lax.axis_index("x"), lax.axis_index("y"), lax.axis_index("c")
        me, sibling = (mx, my, mc), (mx, my, 1 - mc)
        chips = [(1 - mx, my), (mx, 1 - my), (1 - mx, 1 - my)]

        def slot(px, py, pc):
            return out_ref.at[4 * px + 2 * py + pc]

        def copy(k, block, to, src=None):
            return pltpu.make_async_remote_copy(
                src_ref=slot(*block) if src is None else src, dst_ref=slot(*block),
                send_sem=send_sems.at[k], recv_sem=recv_sems.at[k], device_id=to, device_id_type=MESH)

        mine = pltpu.make_async_copy(x_ref, slot(*me), local_sem)
        mine.start()
        first = [copy(0, me, sibling, src=x_ref)]
        first += [copy(1 + j, me, (*chip, mc), src=x_ref) for j, chip in enumerate(chips)]
        for cp in first:
            cp.start()
        passed = [copy(4 + j, (*chip, mc), sibling) for j, chip in enumerate(chips)]
        for j, chip in enumerate(chips):
            copy(1 + j, (*chip, mc), me).wait_recv()
            passed[j].start()
        copy(0, sibling, me).wait_recv()
        for j, chip in enumerate(chips):
            copy(4 + j, (*chip, 1 - mc), me).wait_recv()
        for cp in first + passed:
            cp.wait_send()
        mine.wait()

    any_spec = pl.BlockSpec(memory_space=pl.ANY)
    return pl.pallas_call(
        body, name=name, in_specs=[any_spec], out_specs=any_spec,
        out_shape=jax.ShapeDtypeStruct((N_DEV, r, c), x.dtype),
        scratch_shapes=[pltpu.SemaphoreType.DMA((7,)), pltpu.SemaphoreType.DMA((7,)), pltpu.SemaphoreType.DMA(())],
    )(x)


HBM_SPEC = pl.BlockSpec(memory_space=pltpu.HBM)
SEM_SPEC = pl.BlockSpec(memory_space=pltpu.SEMAPHORE)
SPLIT_EFFECT = pltpu.SideEffectType.DATAFLOW_SIDE_EFFECTING


def _exchange_copies(src_ref, land_ref, send_sems, recv_sems, gather):
    mx, my, mc = lax.axis_index("x"), lax.axis_index("y"), lax.axis_index("c")
    me = 4 * mx + 2 * my + mc
    copies = []
    for k in range(1, N_DEV):
        px = 1 - mx if k & 4 else mx
        py = 1 - my if k & 2 else my
        pc = 1 - mc if k & 1 else mc
        src = src_ref if gather else src_ref.at[4 * px + 2 * py + pc]
        dst = land_ref.at[me] if gather else land_ref.at[k]
        copies.append(pltpu.make_async_remote_copy(
            src_ref=src, dst_ref=dst, send_sem=send_sems.at[k - 1], recv_sem=recv_sems.at[k - 1],
            device_id=(px, py, pc), device_id_type=MESH))
    return copies


def _exchange_start(src, after, gather, name):
    land_shape = (N_DEV,) + src.shape[-2:]

    def body(src_ref, land_ref, after_ref, send_sems, recv_sems, src_thru, land_thru, token):
        for cp in _exchange_copies(src_ref, land_ref, send_sems, recv_sems, gather):
            cp.start()
        token[...] = jnp.zeros_like(token)

    return pl.pallas_call(
        body, name=name,
        out_shape=(pltpu.SemaphoreType.DMA((N_DEV - 1,)), pltpu.SemaphoreType.DMA((N_DEV - 1,)),
                   pltpu.HBM(src.shape, src.dtype), pltpu.HBM(land_shape, src.dtype),
                   jax.ShapeDtypeStruct((8, 128), F32)),
        in_specs=(HBM_SPEC, HBM_SPEC, pl.BlockSpec(memory_space=pl.ANY)),
        out_specs=(SEM_SPEC, SEM_SPEC, HBM_SPEC, HBM_SPEC, pl.BlockSpec(memory_space=pltpu.VMEM)),
        input_output_aliases={0: 2, 1: 3},
        compiler_params=pltpu.CompilerParams(has_side_effects=SPLIT_EFFECT),
    )(pltpu.with_memory_space_constraint(src, pltpu.HBM),
      pltpu.with_memory_space_constraint(lax.empty(land_shape, src.dtype), pltpu.HBM), after)


def _exchange_wait(started, after, gather, name):
    send_sems, recv_sems, src_thru, land_thru, _ = started

    def body(src_ref, land_ref, send_sems, recv_sems, after_ref, src_out, land_out):
        for cp in _exchange_copies(src_ref, land_ref, send_sems, recv_sems, gather):
            cp.wait_send()
            cp.wait_recv()

    return pl.pallas_call(
        body, name=name,
        out_shape=(pltpu.HBM(src_thru.shape, src_thru.dtype), pltpu.HBM(land_thru.shape, land_thru.dtype)),
        in_specs=(HBM_SPEC, HBM_SPEC, SEM_SPEC, SEM_SPEC, pl.BlockSpec(memory_space=pl.ANY)),
        out_specs=(HBM_SPEC, HBM_SPEC), input_output_aliases={0: 0, 1: 1},
        compiler_params=pltpu.CompilerParams(has_side_effects=SPLIT_EFFECT),
    )(src_thru, land_thru, send_sems, recv_sems, after)


FFN_ROWS = D_FF // N_DEV
WIN_ROWS = (Q_LORA + KV_LORA + QK_ROPE) // N_DEV
WIN_ROWS_PAD = 144
WQ_ROWS = QK_HEAD * Q_LORA // D_MODEL
WKV_ROWS = 256 * KV_LORA // D_MODEL
WOUT_ROWS = V_HEAD
POOL_ROWS = 4 * 32 * POOL_GROUP // D_MODEL


def _t(w):
    return jnp.swapaxes(w, -1, -2)


def _ffn_segments(f, i):
    return [(f + "_w_gate", i, FFN_ROWS), (f + "_w_up", i, FFN_ROWS), (f + "_w_down", i, FFN_ROWS)]


def _mixer_segments(i):
    j = i // 2
    if i % 2 == 0:
        return [("pool_w", j, POOL_ROWS)]
    return [("mla_w_in", j, WIN_ROWS_PAD), ("mla_w_q_up", j, WQ_ROWS), ("mla_w_kv_up", j, WKV_ROWS),
            ("mla_w_out", j, WOUT_ROWS)]


def _pack_shards(p, segs, dtype):
    parts = []
    for name, idx, _ in segs:
        w = p[name][idx]
        if name.endswith("w_gate") or name.endswith("w_up"):
            w = _t(w)
        elif name == "mla_w_in":
            w = jnp.pad(_t(w), ((0, WIN_ROWS_PAD - WIN_ROWS), (0, 0)))
        elif name == "mla_w_q_up":
            w = _t(w).reshape(WQ_ROWS, D_MODEL)
        elif name == "mla_w_kv_up":
            w = _t(w).reshape(WKV_ROWS, D_MODEL)
        elif name == "pool_w":
            w = w.reshape(POOL_ROWS, D_MODEL)
        parts.append(w.astype(dtype))
    return jnp.concatenate(parts, axis=0)


def _unpack_gathered(g, segs):
    out = {}
    off = 0
    for name, layer, rows in segs:
        seg = g[:, off:off + rows, :]
        off += rows
        if name == "mla_w_in":
            w = seg[:, :WIN_ROWS].reshape(N_DEV * WIN_ROWS, D_MODEL)
            w = jnp.pad(w, ((0, LAT_PAD - N_DEV * WIN_ROWS), (0, 0)))
        elif name == "mla_w_q_up":
            w = seg.reshape(N_HEADS, QK_HEAD, Q_LORA)
            w = jnp.pad(w, ((0, 0), (0, HEAD_PAD - QK_HEAD), (0, 0))).reshape(N_HEADS * HEAD_PAD, Q_LORA)
        elif name == "mla_w_kv_up":
            w = seg.reshape(N_HEADS * 256, KV_LORA)
        elif name == "pool_w":
            w = seg.reshape(N_DEV, 4, 32, POOL_GROUP).transpose(1, 0, 2, 3).reshape(4, POOL_GROUP, POOL_GROUP)
        else:
            w = seg.reshape(N_DEV * rows, D_MODEL)
        out[(name, layer)] = w
    return out


def _pack_grads(gr, segments):
    segs = []
    for name, layer, rows in segments:
        g = gr[(name, layer)]
        if name == "mla_w_in":
            g = g[:N_DEV * WIN_ROWS].reshape(N_DEV, WIN_ROWS, D_MODEL)
            g = jnp.pad(g, ((0, 0), (0, WIN_ROWS_PAD - WIN_ROWS), (0, 0)))
        elif name == "mla_w_q_up":
            g = g.reshape(N_HEADS, HEAD_PAD, Q_LORA)[:, :QK_HEAD].reshape(N_DEV, WQ_ROWS, D_MODEL)
        elif name == "mla_w_kv_up":
            g = g.reshape(N_DEV, WKV_ROWS, D_MODEL)
        elif name == "pool_w":
            g = g.reshape(4, N_DEV, 32, POOL_GROUP).transpose(1, 0, 2, 3).reshape(N_DEV, POOL_ROWS, D_MODEL)
        else:
            g = g.reshape(N_DEV, rows, D_MODEL)
        segs.append(g)
    return jnp.concatenate(segs, axis=1)


def _unpack_shard_grads(flat, segments):
    per = {}
    off = 0
    for name, layer, rows in segments:
        seg = flat[off:off + rows]
        off += rows
        if name.endswith("w_gate") or name.endswith("w_up"):
            g = _t(seg)
        elif name == "mla_w_in":
            g = _t(seg[:WIN_ROWS])
        elif name == "mla_w_q_up":
            g = _t(seg.reshape(QK_HEAD, Q_LORA))
        elif name == "mla_w_kv_up":
            g = _t(seg.reshape(256, KV_LORA))
        elif name == "pool_w":
            g = seg.reshape(4, 32, POOL_GROUP)
        else:
            g = seg
        per[(name, layer)] = g
    return per


def _pad_lanes(v, width):
    return jnp.pad(v, ((0, 0), (0, width - v.shape[-1])))


def kernel(x, positions, ffn1_norm, ffn1_w_gate, ffn1_w_up, ffn1_w_down, mix_norm, pool_w, pool_scale, mla_w_in, mla_q_norm, mla_w_q_up, mla_kv_norm, mla_w_kv_up, mla_q_head_norm, mla_k_head_norm, mla_w_out, ffn2_norm, ffn2_w_gate, ffn2_w_up, ffn2_w_down, loss_target, m_ffn1_norm, m_ffn1_w_gate, m_ffn1_w_up, m_ffn1_w_down, m_mix_norm, m_pool_w, m_pool_scale, m_mla_w_in, m_mla_q_norm, m_mla_w_q_up, m_mla_kv_norm, m_mla_w_kv_up, m_mla_q_head_norm, m_mla_k_head_norm, m_mla_w_out, m_ffn2_norm, m_ffn2_w_gate, m_ffn2_w_up, m_ffn2_w_down, v_ffn1_norm, v_ffn1_w_gate, v_ffn1_w_up, v_ffn1_w_down, v_mix_norm, v_pool_w, v_pool_scale, v_mla_w_in, v_mla_q_norm, v_mla_w_q_up, v_mla_kv_norm, v_mla_w_kv_up, v_mla_q_head_norm, v_mla_k_head_norm, v_mla_w_out, v_ffn2_norm, v_ffn2_w_gate, v_ffn2_w_up, v_ffn2_w_down):
    args = (x, positions, ffn1_norm, ffn1_w_gate, ffn1_w_up, ffn1_w_down, mix_norm, pool_w, pool_scale, mla_w_in,
            mla_q_norm, mla_w_q_up, mla_kv_norm, mla_w_kv_up, mla_q_head_norm, mla_k_head_norm, mla_w_out, ffn2_norm,
            ffn2_w_gate, ffn2_w_up, ffn2_w_down)
    p = dict(zip(NAMES, args))
    moments_m = dict(zip(WEIGHTS, (m_ffn1_norm, m_ffn1_w_gate, m_ffn1_w_up, m_ffn1_w_down, m_mix_norm, m_pool_w, m_pool_scale, m_mla_w_in, m_mla_q_norm, m_mla_w_q_up, m_mla_kv_norm, m_mla_w_kv_up, m_mla_q_head_norm, m_mla_k_head_norm, m_mla_w_out, m_ffn2_norm, m_ffn2_w_gate, m_ffn2_w_up, m_ffn2_w_down)))
    moments_v = dict(zip(WEIGHTS, (v_ffn1_norm, v_ffn1_w_gate, v_ffn1_w_up, v_ffn1_w_down, v_mix_norm, v_pool_w, v_pool_scale, v_mla_w_in, v_mla_q_norm, v_mla_w_q_up, v_mla_kv_norm, v_mla_w_kv_up, v_mla_q_head_norm, v_mla_k_head_norm, v_mla_w_out, v_ffn2_norm, v_ffn2_w_gate, v_ffn2_w_up, v_ffn2_w_down)))
    dev = 4 * lax.axis_index("x") + 2 * lax.axis_index("y") + lax.axis_index("c")

    xs = x[0]
    n = xs.shape[0]
    target = loss_target[0]

    inv_freq = 1.0 / (ROPE_THETA ** (jnp.arange(0, QK_ROPE, 2, dtype=F32) / QK_ROPE))
    ang = positions[0].astype(F32)[..., None] * inv_freq
    cos, sin = jnp.cos(ang), jnp.sin(ang)
    zero = jnp.zeros((n, 128 - QK_ROPE), F32)
    rope_cos = jnp.concatenate([cos, cos, zero], axis=1)
    rope_sin = jnp.concatenate([-sin, sin, zero], axis=1)

    layer_segs = [_ffn_segments("ffn1", i) + _mixer_segments(i) + _ffn_segments("ffn2", i) for i in range(DEPTH)]
    shards = [_pack_shards(p, segs, BF16) for segs in layer_segs]
    w = {}

    def tied(gain, token):
        return gain + token[0, 0]

    gains_local = jnp.concatenate([_pad_lanes(mla_q_norm, 128), _pad_lanes(mla_kv_norm, 128)], axis=0)
    gains_all = _all_gather(jnp.pad(gains_local, ((0, 4), (0, 0))), "gains_all_gather")
    q_norm_full = gains_all[:, 0:2, :Q_LORA // N_DEV].transpose(1, 0, 2).reshape(2, Q_LORA)
    kv_norm_full = gains_all[:, 2:4, :KV_LORA // N_DEV].transpose(1, 0, 2).reshape(2, KV_LORA)
    ghq = _pad_lanes(mla_q_head_norm, HEAD_PAD)
    ghk = _pad_lanes(mla_k_head_norm, HEAD_PAD)

    def mla_weights(j):
        return (w[("mla_w_in", j)], q_norm_full[j:j + 1], kv_norm_full[j:j + 1], w[("mla_w_q_up", j)],
                w[("mla_w_kv_up", j)], ghq[j:j + 1], ghk[j:j + 1], rope_cos, rope_sin)

    saved = []
    cur = xs
    gather = _exchange_start(shards[0], xs, True, "ag_start_0")
    for i in range(DEPTH):
        j = i // 2
        st = {"x0": cur}
        _, landed = _exchange_wait(gather, cur, True, f"ag_wait_{i}")
        w.update(_unpack_gathered(lax.dynamic_update_slice(landed, shards[i][None], (dev, 0, 0)), layer_segs[i]))
        gain1 = ffn1_norm[i:i + 1]
        if i + 1 < DEPTH:
            gather = _exchange_start(shards[i + 1], landed, True, f"ag_start_{i + 1}")
            gain1 = tied(gain1, gather[4])
        cur, st["a1"], st["u1"] = _ffn_fwd(cur, gain1, w[("ffn1_w_gate", i)], w[("ffn1_w_up", i)],
                                           w[("ffn1_w_down", i)], f"ffn1_fwd_{i}")
        st["x1"] = cur
        if i % 2 == 0:
            cur = _pool_fwd(cur, mix_norm[i:i + 1], w[("pool_w", j)], pool_scale[j:j + 1], f"pool_fwd_{i}")
        else:
            st["q"], st["k"], st["v"], vt = _mla_pre_fwd(cur, mix_norm[i:i + 1], *mla_weights(j), f"mla_pre_fwd_{i}")
            st["o"], lse = _flash_fwd(st["q"], st["k"], vt, f"flash_fwd_{i}")
            tq_bwd = _tiles(n)["bwd_q"]
            st["lse"] = lse.reshape(N_HEADS, n // tq_bwd, 1, tq_bwd)
            cur = _mla_post_fwd(st["o"], cur, w[("mla_w_out", j)], f"mla_post_fwd_{i}")
        st["x2"] = cur
        cur, st["a2"], st["u2"] = _ffn_fwd(cur, ffn2_norm[i:i + 1], w[("ffn2_w_gate", i)], w[("ffn2_w_up", i)],
                                           w[("ffn2_w_down", i)], f"ffn2_fwd_{i}")
        saved.append(st)

    dcur, sq_err = _loss_head(cur, target, "loss_head")
    loss_part = 0.5 * jnp.sum(sq_err) * (1.0 / D_MODEL)

    gr = {}
    small = {k: [None] * DEPTH for k in ("ffn1_norm", "mix_norm", "ffn2_norm")}
    small.update({k: [None] * (DEPTH // 2) for k in ("pool_scale", "mla_q_norm", "mla_kv_norm", "mla_q_head_norm",
                                                     "mla_k_head_norm")})

    me = jnp.reshape(dev, (1,)).astype(jnp.int32)
    grads = {}
    in_flight = []

    def reduce_start(segs, tag):
        started = _exchange_start(_pack_grads(gr, segs), dcur, False, f"rs_start_{tag}")
        reduce_finish(started[4])
        in_flight.append((started, segs, tag))
        return started[4]

    def reduce_finish(after):
        if in_flight:
            started, segs, tag = in_flight.pop()
            mine, landed = _exchange_wait(started, after, False, f"rs_wait_{tag}")
            grads.update(_unpack_shard_grads(_sum_exchange(mine, landed, me, f"rs_sum_{tag}"), segs))

    def ffn_backward(f, i, x_in, a, u, gain, dout):
        dx, h, dob, y, da, du, dg = _ffn_bwd(x_in, dout, a, u, gain, w[(f + "_w_gate", i)], w[(f + "_w_up", i)],
                                             w[(f + "_w_down", i)], f"{f}_bwd_{i}")
        gr[(f + "_w_gate", i)] = _tn_matmul(da, h, f"{f}_dgate_{i}")
        gr[(f + "_w_up", i)] = _tn_matmul(du, h, f"{f}_dup_{i}")
        gr[(f + "_w_down", i)] = _tn_matmul(y, dob, f"{f}_ddown_{i}")
        small[f + "_norm"][i] = jnp.sum(dg, axis=0)
        return dx

    token = jnp.zeros((8, 128), F32)
    for i in reversed(range(DEPTH)):
        j = i // 2
        st = saved[i]
        dcur = ffn_backward("ffn2", i, st["x2"], st["a2"], st["u2"], tied(ffn2_norm[i:i + 1], token), dcur)
        if i % 2 == 0:
            dcur, dpw, dsc, dg = _pool_bwd(st["x1"], dcur, mix_norm[i:i + 1], w[("pool_w", j)], pool_scale[j:j + 1],
                                           f"pool_bwd_{i}")
            gr[("pool_w", j)] = dpw
            small["pool_scale"][j] = jnp.sum(dsc, axis=0)
            small["mix_norm"][i] = jnp.sum(dg, axis=0)
        else:
            dob, dpo, delta = _mla_out_bwd(dcur, st["o"], w[("mla_w_out", j)], f"mla_out_bwd_{i}")
            gr[("mla_w_out", j)] = _tn_matmul(st["o"], dob, f"mla_dout_{i}")
            dk, dv, dq = _flash_bwd(st["q"], st["k"], st["v"], dpo, st["lse"], delta, f"flash_bwd_{i}")
            (dcur, h, dlat, cqn, dqp, ckvn, dkv, dg, dgq, dgkv, dghq, dghk) = _mla_pre_bwd(
                st["x1"], dcur, dq, dk, dv, mix_norm[i:i + 1], *mla_weights(j), f"mla_pre_bwd_{i}")
            gr[("mla_w_in", j)] = _tn_matmul(dlat, h, f"mla_din_{i}")
            gr[("mla_w_q_up", j)] = _tn_matmul(dqp, cqn, f"mla_dqup_{i}")
            gr[("mla_w_kv_up", j)] = _tn_matmul(dkv, ckvn, f"mla_dkvup_{i}")
            small["mix_norm"][i] = jnp.sum(dg, axis=0)
            small["mla_q_norm"][j] = _pad_lanes(jnp.sum(dgq, axis=0)[None], D_MODEL)[0]
            small["mla_kv_norm"][j] = _pad_lanes(jnp.sum(dgkv, axis=0)[None], D_MODEL)[0]
            small["mla_q_head_norm"][j] = _pad_lanes(jnp.sum(dghq, axis=0)[None], D_MODEL)[0]
            small["mla_k_head_norm"][j] = _pad_lanes(jnp.sum(dghk, axis=0)[None], D_MODEL)[0]
        token = reduce_start(_ffn_segments("ffn2", i) + _mixer_segments(i), f"a{i}")
        dcur = ffn_backward("ffn1", i, st["x0"], st["a1"], st["u1"], tied(ffn1_norm[i:i + 1], token), dcur)
        token = reduce_start(_ffn_segments("ffn1", i), f"b{i}")
    grad_x = dcur[None]

    small_order = ("ffn1_norm", "mix_norm", "ffn2_norm", "pool_scale", "mla_q_norm", "mla_kv_norm",
                   "mla_q_head_norm", "mla_k_head_norm")
    rows = [r for k in small_order for r in small[k]]
    rows.append(jnp.zeros((D_MODEL,), F32).at[0].set(loss_part))
    rows.append(jnp.zeros((D_MODEL,), F32))
    small_sum = _sum_parts(_all_gather(jnp.stack(rows), "small_all_gather"), "small_sum")
    loss = small_sum[SM_ROWS - 2, 0]
    reduce_finish(small_sum)
    counts = {k: (DEPTH // 2 if k.startswith(("mla", "pool")) else DEPTH) for k in WEIGHTS}
    grads = {k: jnp.stack([grads[(k, idx)] for idx in range(counts[k])]) for k in WEIGHTS if (k, 0) in grads}
    off = 0
    for k in small_order:
        cnt = len(small[k])
        g = small_sum[off:off + cnt]
        off += cnt
        if k == "mla_q_norm":
            g = lax.dynamic_slice_in_dim(g[:, :Q_LORA], dev * (Q_LORA // N_DEV), Q_LORA // N_DEV, axis=1)
        elif k == "mla_kv_norm":
            g = lax.dynamic_slice_in_dim(g[:, :KV_LORA], dev * (KV_LORA // N_DEV), KV_LORA // N_DEV, axis=1)
        elif k in ("mla_q_head_norm", "mla_k_head_norm"):
            g = g[:, :QK_HEAD]
        grads[k] = g

    deltas, new_m, new_v = {}, {}, {}
    for k in WEIGHTS:
        shape = p[k].shape
        view = (-1, shape[-1])
        d, mn, vn = _adamw(p[k].reshape(view), grads[k].reshape(view), moments_m[k].reshape(view),
                           moments_v[k].reshape(view), "adamw_" + k)
        deltas[k], new_m[k], new_v[k] = d.reshape(shape), mn.reshape(shape), vn.reshape(shape)
        grads[k] = grads[k].reshape(shape)

    return (loss, grad_x, *[grads[k] for k in WEIGHTS], *[deltas[k] for k in WEIGHTS],
            *[new_m[k] for k in WEIGHTS], *[new_v[k] for k in WEIGHTS])
```

```python
import functools

import jax
import jax.numpy as jnp
from jax import lax
from jax.experimental import pallas as pl
from jax.experimental.pallas import tpu as pltpu

F32 = jnp.float32
BF16 = jnp.bfloat16

D_MODEL = 1024
DEPTH = 4
D_FF = 2816
POOL_WINDOWS = (2, 4, 8, 16)
POOL_GROUP = 256
POOL_HALO = 16
N_HEADS = 8
QK_NOPE = 128
QK_ROPE = 64
QK_HEAD = 192
V_HEAD = 128
Q_LORA = 768
KV_LORA = 256
ROPE_THETA = 10000.0
EPS = 1e-6
FFN_HALF = 0.5
ADAM_LR = 0.001
ADAM_B1 = 0.9
ADAM_B2 = 0.999
ADAM_EPS = 1e-08
ADAM_WD = 0.01
ADAM_STEP = 10

N_DEV = 8
HEAD_PAD = 256
LAT_PAD = 1152
VT_ROWS = 144
LOG2_E = 1.4426950408889634
ATTN_SCALE = QK_HEAD ** -0.5
LOGIT_SCALE = ATTN_SCALE * LOG2_E
V7X_VMEM_LIMIT = 56 * 1024 * 1024
FF_CHUNK = 256
SM_ROWS = 24

NAMES = ['x', 'positions', 'ffn1_norm', 'ffn1_w_gate', 'ffn1_w_up', 'ffn1_w_down', 'mix_norm', 'pool_w',
         'pool_scale', 'mla_w_in', 'mla_q_norm', 'mla_w_q_up', 'mla_kv_norm', 'mla_w_kv_up', 'mla_q_head_norm',
         'mla_k_head_norm', 'mla_w_out', 'ffn2_norm', 'ffn2_w_gate', 'ffn2_w_up', 'ffn2_w_down']
WEIGHTS = NAMES[2:]


def _tiles(n):
    return dict(ffn_fwd=min(512, n), ffn_bwd=min(256, n), fwd_k=min(512, n // 2), bwd_q=min(512, n // 2),
                mla_bwd=min(256, n), pool=min(512, n), tn=min(512, n), rows=min(1024, n))


def _params(*sem):
    return pltpu.CompilerParams(dimension_semantics=sem, vmem_limit_bytes=V7X_VMEM_LIMIT)


def _dot(a, b):
    return jnp.dot(a, b, preferred_element_type=F32)


def _dot_nt(a, b):
    return lax.dot_general(a, b, (((1,), (1,)), ((), ())), preferred_element_type=F32)


def _dot_tn(a, b):
    return lax.dot_general(a, b, (((0,), (0,)), ((), ())), preferred_element_type=F32)


def _rowsum8(v):
    rows, w = v.shape
    return jnp.sum(v.reshape(rows // 8, 8, w), axis=0)


def _sigmoid(a):
    return 1.0 / (1.0 + jnp.exp(-a))


def _rms_fwd(x, width=None):
    width = x.shape[-1] if width is None else width
    r = lax.rsqrt(jnp.sum(x * x, axis=-1, keepdims=True) * (1.0 / width) + EPS)
    return x * r, r


def _rms_bwd(dy, xhat, r, gain, width=None):
    width = xhat.shape[-1] if width is None else width
    t = dy * gain
    return r * (t - xhat * (jnp.sum(t * xhat, axis=-1, keepdims=True) * (1.0 / width)))


def _full(shape):
    return pl.BlockSpec(shape, lambda *_: (0,) * len(shape))


def _load_weights(srcs, dsts, sems):
    copies = [pltpu.make_async_copy(s, d, sems.at[i]) for i, (s, d) in enumerate(zip(srcs, dsts))]
    for cp in copies:
        cp.start()
    for cp in copies:
        cp.wait()


def _ffn_fwd(x, gain, wg_t, wu_t, wd, name):
    n = x.shape[0]
    tm = _tiles(n)["ffn_fwd"]

    def body(x_ref, g_ref, wg_hbm, wu_hbm, wd_hbm, out_ref, a_ref, u_ref, wg_v, wu_v, wd_v, sems):
        @pl.when(pl.program_id(0) == 0)
        def _():
            _load_weights((wg_hbm, wu_hbm, wd_hbm), (wg_v, wu_v, wd_v), sems)

        xt = x_ref[...]
        xhat, _ = _rms_fwd(xt)
        h = (xhat * g_ref[...]).astype(BF16)
        acc = jnp.zeros((tm, D_MODEL), F32)
        for c in range(D_FF // FF_CHUNK):
            sl = pl.ds(c * FF_CHUNK, FF_CHUNK)
            a = _dot_nt(h, wg_v[sl, :])
            u = _dot_nt(h, wu_v[sl, :])
            a_ref[:, sl] = a.astype(BF16)
            u_ref[:, sl] = u.astype(BF16)
            y = (a * _sigmoid(a) * u).astype(BF16)
            acc = acc + _dot(y, wd_v[sl, :])
        out_ref[...] = xt + FFN_HALF * acc

    any_spec = pl.BlockSpec(memory_space=pl.ANY)
    return pl.pallas_call(
        body, name=name, grid=(n // tm,),
        in_specs=[pl.BlockSpec((tm, D_MODEL), lambda i: (i, 0)), _full((1, D_MODEL)), any_spec, any_spec, any_spec],
        out_specs=[pl.BlockSpec((tm, D_MODEL), lambda i: (i, 0)), pl.BlockSpec((tm, D_FF), lambda i: (i, 0)),
                   pl.BlockSpec((tm, D_FF), lambda i: (i, 0))],
        out_shape=[jax.ShapeDtypeStruct((n, D_MODEL), F32), jax.ShapeDtypeStruct((n, D_FF), BF16),
                   jax.ShapeDtypeStruct((n, D_FF), BF16)],
        scratch_shapes=[pltpu.VMEM((D_FF, D_MODEL), BF16)] * 3 + [pltpu.SemaphoreType.DMA((3,))],
        compiler_params=_params("arbitrary"),
    )(x, gain, wg_t, wu_t, wd)


def _ffn_bwd(x, dout, a, u, gain, wg_t, wu_t, wd, name):
    n = x.shape[0]
    tm = _tiles(n)["ffn_bwd"]

    def body(x_ref, do_ref, a_ref, u_ref, g_ref, wg_hbm, wu_hbm, wd_hbm,
             dx_ref, h_ref, dob_ref, y_ref, da_ref, du_ref, dg_ref, wg_v, wu_v, wd_v, sems):
        @pl.when(pl.program_id(0) == 0)
        def _():
            _load_weights((wg_hbm, wu_hbm, wd_hbm), (wg_v, wu_v, wd_v), sems)
            dg_ref[...] = jnp.zeros_like(dg_ref)

        xt = x_ref[...]
        g = g_ref[...]
        xhat, r = _rms_fwd(xt)
        h_ref[...] = (xhat * g).astype(BF16)
        dout = do_ref[...]
        dob = (FFN_HALF * dout).astype(BF16)
        dob_ref[...] = dob
        dh = jnp.zeros((tm, D_MODEL), F32)
        for c in range(D_FF // FF_CHUNK):
            sl = pl.ds(c * FF_CHUNK, FF_CHUNK)
            dy = _dot_nt(dob, wd_v[sl, :])
            av = a_ref[:, sl].astype(F32)
            uv = u_ref[:, sl].astype(F32)
            s = _sigmoid(av)
            silu = av * s
            y_ref[:, sl] = (silu * uv).astype(BF16)
            du = (dy * silu).astype(BF16)
            da = (dy * uv * (s * (1.0 + av * (1.0 - s)))).astype(BF16)
            du_ref[:, sl] = du
            da_ref[:, sl] = da
            dh = dh + _dot(da, wg_v[sl, :]) + _dot(du, wu_v[sl, :])
        dg_ref[...] += _rowsum8(dh * xhat)
        dx_ref[...] = dout + _rms_bwd(dh, xhat, r, g)

    any_spec = pl.BlockSpec(memory_space=pl.ANY)
    row_d = pl.BlockSpec((tm, D_MODEL), lambda i: (i, 0))
    row_f = pl.BlockSpec((tm, D_FF), lambda i: (i, 0))
    return pl.pallas_call(
        body, name=name, grid=(n // tm,),
        in_specs=[row_d, row_d, row_f, row_f, _full((1, D_MODEL)), any_spec, any_spec, any_spec],
        out_specs=[row_d, row_d, row_d, row_f, row_f, row_f, _full((8, D_MODEL))],
        out_shape=[jax.ShapeDtypeStruct((n, D_MODEL), F32), jax.ShapeDtypeStruct((n, D_MODEL), BF16),
                   jax.ShapeDtypeStruct((n, D_MODEL), BF16), jax.ShapeDtypeStruct((n, D_FF), BF16),
                   jax.ShapeDtypeStruct((n, D_FF), BF16), jax.ShapeDtypeStruct((n, D_FF), BF16),
                   jax.ShapeDtypeStruct((8, D_MODEL), F32)],
        scratch_shapes=[pltpu.VMEM((D_FF, D_MODEL), BF16)] * 3 + [pltpu.SemaphoreType.DMA((3,))],
        compiler_params=_params("arbitrary"),
    )(x, dout, a, u, gain, wg_t, wu_t, wd)


def _tn_matmul(a, b, name, behind=None):
    n, fa = a.shape
    db = b.shape[1]
    tk = _tiles(n)["tn"]
    tf = fa // 2 if (fa // 2) % 128 == 0 and fa > 1024 else fa
    behind = jnp.zeros((8, 128), F32) if behind is None else behind

    def body(a_ref, b_ref, behind_ref, o_ref):
        @pl.when(pl.program_id(1) == 0)
        def _():
            o_ref[...] = jnp.zeros_like(o_ref)

        o_ref[...] += _dot_tn(a_ref[...], b_ref[...])

    return pl.pallas_call(
        body, name=name, grid=(fa // tf, n // tk),
        in_specs=[pl.BlockSpec((tk, tf), lambda i, k: (k, i)), pl.BlockSpec((tk, db), lambda i, k: (k, 0)),
                  pl.BlockSpec(memory_space=pl.ANY)],
        out_specs=pl.BlockSpec((tf, db), lambda i, k: (i, 0)),
        out_shape=jax.ShapeDtypeStruct((fa, db), F32),
        compiler_params=_params("arbitrary", "arbitrary"),
    )(a, b, behind)


def _loss_head(y, target, name):
    n = y.shape[0]
    tm = _tiles(n)["rows"]

    def body(y_ref, t_ref, d_ref, acc_ref):
        @pl.when(pl.program_id(0) == 0)
        def _():
            acc_ref[...] = jnp.zeros_like(acc_ref)

        d = y_ref[...] - t_ref[...]
        d_ref[...] = d * (1.0 / D_MODEL)
        acc_ref[...] += _rowsum8(d * d)

    row = pl.BlockSpec((tm, D_MODEL), lambda i: (i, 0))
    return pl.pallas_call(
        body, name=name, grid=(n // tm,), in_specs=[row, row], out_specs=[row, _full((8, D_MODEL))],
        out_shape=[jax.ShapeDtypeStruct((n, D_MODEL), F32), jax.ShapeDtypeStruct((8, D_MODEL), F32)],
        compiler_params=_params("arbitrary"),
    )(y, target)


def _window_sum(v, w, rows, forward):
    s = v
    sh = 1
    while sh < w:
        s = s + pltpu.roll(s, (rows - sh) if forward else sh, 0)
        sh *= 2
    return s


def _pool_fwd(x, gain, w, scale, name):
    n = x.shape[0]
    tm = _tiles(n)["pool"]
    hb = tm // POOL_HALO
    rows = tm + POOL_HALO

    def body(x_ref, xh_ref, g_ref, w_ref, sc_ref, out_ref):
        i = pl.program_id(0)
        xt = x_ref[...]
        e = jnp.concatenate([xh_ref[...], xt], axis=0)
        xhat, _ = _rms_fwd(e)
        row = lax.broadcasted_iota(jnp.int32, (rows, 1), 0)
        hn = jnp.where((row >= POOL_HALO) | (i > 0), xhat * g_ref[...], 0.0)
        t_glob = i * tm + row - POOL_HALO
        outs = []
        for gi, win in enumerate(POOL_WINDOWS):
            ug = hn[:, gi * POOL_GROUP:(gi + 1) * POOL_GROUP]
            cnt = jnp.maximum(jnp.minimum(t_glob + 1, win), 1).astype(F32)
            pooled = (_window_sum(ug, win, rows, False) / cnt - ug)[POOL_HALO:]
            outs.append(_dot(pooled.astype(BF16), w_ref[gi]))
        out_ref[...] = xt + jnp.concatenate(outs, axis=1) * sc_ref[...]

    return pl.pallas_call(
        body, name=name, grid=(n // tm,),
        in_specs=[pl.BlockSpec((tm, D_MODEL), lambda i: (i, 0)),
                  pl.BlockSpec((POOL_HALO, D_MODEL), lambda i: (jnp.maximum(i * hb - 1, 0), 0)),
                  _full((1, D_MODEL)), _full((4, POOL_GROUP, POOL_GROUP)), _full((1, D_MODEL))],
        out_specs=pl.BlockSpec((tm, D_MODEL), lambda i: (i, 0)),
        out_shape=jax.ShapeDtypeStruct((n, D_MODEL), F32),
        compiler_params=_params("arbitrary"),
    )(x, x, gain, w, scale)


def _pool_bwd(x, dout, gain, w, scale, name):
    n = x.shape[0]
    tm = _tiles(n)["pool"]
    hb = tm // POOL_HALO
    rows = tm + POOL_HALO
    nt = n // tm

    def body(x_ref, xh_ref, do_ref, doh_ref, g_ref, w_ref, sc_ref, dx_ref, dw_ref, dsc_ref, dg_ref):
        i = pl.program_id(0)

        @pl.when(i == 0)
        def _():
            dw_ref[...] = jnp.zeros_like(dw_ref)
            dsc_ref[...] = jnp.zeros_like(dsc_ref)
            dg_ref[...] = jnp.zeros_like(dg_ref)

        xt = x_ref[...]
        g = g_ref[...]
        e = jnp.concatenate([xh_ref[...], xt], axis=0)
        xhat_e, r_e = _rms_fwd(e)
        row = lax.broadcasted_iota(jnp.int32, (rows, 1), 0)
        hn = jnp.where((row >= POOL_HALO) | (i > 0), xhat_e * g, 0.0)
        t_prev = i * tm + row - POOL_HALO
        t_next = i * tm + row
        dout = do_ref[...]
        dt = jnp.concatenate([dout, doh_ref[...]], axis=0)
        dt = jnp.where((row < tm) | (i < nt - 1), dt, 0.0)
        dyr = dt * sc_ref[...]
        dus, dscs = [], []
        for gi, win in enumerate(POOL_WINDOWS):
            lanes = slice(gi * POOL_GROUP, (gi + 1) * POOL_GROUP)
            ug = hn[:, lanes]
            cnt = jnp.maximum(jnp.minimum(t_prev + 1, win), 1).astype(F32)
            pooled = (_window_sum(ug, win, rows, False) / cnt - ug)[POOL_HALO:].astype(BF16)
            yraw = _dot(pooled, w_ref[gi])
            dscs.append(_rowsum8(dout[:, lanes] * yraw))
            dyr_b = dyr[:, lanes].astype(BF16)
            dw_ref[gi] += _dot_tn(pooled, dyr_b[:tm])
            dpool = _dot_nt(dyr_b, w_ref[gi])
            cnt2 = jnp.minimum(t_next + 1, win).astype(F32)
            dus.append((_window_sum(dpool / cnt2, win, rows, True) - dpool)[:tm])
        dsc_ref[...] += jnp.concatenate(dscs, axis=1)
        dh = jnp.concatenate(dus, axis=1)
        xhat = xhat_e[POOL_HALO:]
        dg_ref[...] += _rowsum8(dh * xhat)
        dx_ref[...] = dout + _rms_bwd(dh, xhat, r_e[POOL_HALO:], g)

    row_d = pl.BlockSpec((tm, D_MODEL), lambda i: (i, 0))
    prev_h = pl.BlockSpec((POOL_HALO, D_MODEL), lambda i: (jnp.maximum(i * hb - 1, 0), 0))
    next_h = pl.BlockSpec((POOL_HALO, D_MODEL), lambda i: (jnp.minimum((i + 1) * hb, n // POOL_HALO - 1), 0))
    return pl.pallas_call(
        body, name=name, grid=(nt,),
        in_specs=[row_d, prev_h, row_d, next_h, _full((1, D_MODEL)), _full((4, POOL_GROUP, POOL_GROUP)),
                  _full((1, D_MODEL))],
        out_specs=[row_d, _full((4, POOL_GROUP, POOL_GROUP)), _full((8, D_MODEL)), _full((8, D_MODEL))],
        out_shape=[jax.ShapeDtypeStruct((n, D_MODEL), F32), jax.ShapeDtypeStruct((4, POOL_GROUP, POOL_GROUP), F32),
                   jax.ShapeDtypeStruct((8, D_MODEL), F32), jax.ShapeDtypeStruct((8, D_MODEL), F32)],
        compiler_params=_params("arbitrary"),
    )(x, x, dout, dout, gain, w, scale)


def _rope(v, cos, sin_signed):
    lo, hi = v[:, :128], v[:, 128:]
    lane = lax.broadcasted_iota(jnp.int32, hi.shape, 1)
    swapped = jnp.where(lane < 32, pltpu.roll(hi, 96, 1), pltpu.roll(hi, 32, 1))
    return jnp.concatenate([lo, hi * cos + swapped * sin_signed], axis=1)


def _rope_bwd(gr, cos, sin_signed):
    lo, hi = gr[:, :128], gr[:, 128:]
    t = hi * sin_signed
    lane = lax.broadcasted_iota(jnp.int32, hi.shape, 1)
    swapped = jnp.where(lane < 32, pltpu.roll(t, 96, 1), pltpu.roll(t, 32, 1))
    return jnp.concatenate([lo, hi * cos + swapped], axis=1)


def _mla_latents(h, win_ref):
    cq = _dot_nt(h, win_ref[0:Q_LORA, :])
    ckv = _dot_nt(h, win_ref[Q_LORA:Q_LORA + KV_LORA, :])
    kpe = _dot_nt(h, win_ref[Q_LORA + KV_LORA:LAT_PAD, :])
    return cq, ckv, kpe


def _mla_pre_fwd(x, gain, win, gq, gkv, wq, wkv, ghq, ghk, cos, sin_signed, name):
    n = x.shape[0]
    tm = _tiles(n)["fwd_k"]

    def body(x_ref, g_ref, win_ref, gq_ref, gkv_ref, wq_ref, wkv_ref, ghq_ref, ghk_ref, c_ref, s_ref,
             q_ref, k_ref, v_ref, vt_ref):
        xhat, _ = _rms_fwd(x_ref[...])
        h = (xhat * g_ref[...]).astype(BF16)
        cq, ckv, kpe = _mla_latents(h, win_ref)
        cqn = (_rms_fwd(cq)[0] * gq_ref[...]).astype(BF16)
        ckvn = (_rms_fwd(ckv)[0] * gkv_ref[...]).astype(BF16)
        cos, sn = c_ref[...], s_ref[...]
        for hd in range(N_HEADS):
            rws = pl.ds(hd * HEAD_PAD, HEAD_PAD)
            qh = _dot_nt(cqn, wq_ref[rws, :])
            qn = _rms_fwd(qh, QK_HEAD)[0] * ghq_ref[...]
            q_ref[hd] = (_rope(qn, cos, sn) * LOGIT_SCALE).astype(BF16)
            kvh = _dot_nt(ckvn, wkv_ref[rws, :])
            kpre = jnp.concatenate([kvh[:, :QK_NOPE], kpe], axis=1)
            kn = _rms_fwd(kpre, QK_HEAD)[0] * ghk_ref[...]
            k_ref[hd] = _rope(kn, cos, sn).astype(BF16)
            vh = kvh[:, QK_NOPE:]
            v_ref[hd] = vh.astype(BF16)
            vt_ref[hd, 0] = jnp.concatenate([vh.T, jnp.ones((VT_ROWS - V_HEAD, tm), F32)], axis=0).astype(BF16)

    row = lambda w: pl.BlockSpec((tm, w), lambda i: (i, 0))
    head = lambda w: pl.BlockSpec((N_HEADS, tm, w), lambda i: (0, i, 0))
    return pl.pallas_call(
        body, name=name, grid=(n // tm,),
        in_specs=[row(D_MODEL), _full((1, D_MODEL)), _full((LAT_PAD, D_MODEL)), _full((1, Q_LORA)),
                  _full((1, KV_LORA)), _full((N_HEADS * HEAD_PAD, Q_LORA)), _full((N_HEADS * HEAD_PAD, KV_LORA)),
                  _full((1, HEAD_PAD)), _full((1, HEAD_PAD)), row(128), row(128)],
        out_specs=[head(HEAD_PAD), head(HEAD_PAD), head(V_HEAD),
                   pl.BlockSpec((N_HEADS, 1, VT_ROWS, tm), lambda i: (0, i, 0, 0))],
        out_shape=[jax.ShapeDtypeStruct((N_HEADS, n, HEAD_PAD), BF16), jax.ShapeDtypeStruct((N_HEADS, n, HEAD_PAD), BF16),
                   jax.ShapeDtypeStruct((N_HEADS, n, V_HEAD), BF16),
                   jax.ShapeDtypeStruct((N_HEADS, n // tm, VT_ROWS, tm), BF16)],
        compiler_params=_params("arbitrary"),
    )(x, gain, win, gq, gkv, wq, wkv, ghq, ghk, cos, sin_signed)


def _flash_fwd(q, k, vt, name):
    n = q.shape[1]
    tk = _tiles(n)["fwd_k"]
    tq = 2 * tk
    nq = n // tq

    def body(q_ref, k_ref, vt_ref, o_ref, lse_ref, s_scr, m_scr, acc_scr):
        i = pl.program_id(1)
        qi = q_ref[0]

        def scores(j, slot):
            s_scr[slot] = _dot_nt(k_ref[0, pl.ds(pl.multiple_of(j * tk, tk), tk), :], qi)

        def update(j, slot, diagonal=None):
            s = s_scr[slot]
            if diagonal is not None:
                krow = lax.broadcasted_iota(jnp.int32, (tk, tq), 0) + diagonal * tk
                qcol = lax.broadcasted_iota(jnp.int32, (tk, tq), 1)
                s = jnp.where(krow <= qcol, s, -jnp.inf)
            m = m_scr[...]
            m_new = jnp.maximum(m, jnp.max(s, axis=0, keepdims=True))
            p = jnp.exp2(s - m_new).astype(BF16)
            acc_scr[...] = jnp.exp2(m - m_new) * acc_scr[...] + _dot(vt_ref[0, j], p)
            m_scr[...] = m_new

        m_scr[...] = jnp.full((1, tq), -jnp.inf, F32)
        acc_scr[...] = jnp.zeros((VT_ROWS, tq), F32)
        scores(0, 0)

        def pair(jj, carry):
            scores(2 * jj + 1, 1)
            update(2 * jj, 0)
            scores(2 * jj + 2, 0)
            update(2 * jj + 1, 1)
            return carry

        lax.fori_loop(0, i, pair, 0)
        scores(2 * i + 1, 1)
        update(2 * i, 0, diagonal=0)
        update(2 * i + 1, 1, diagonal=1)
        l = acc_scr[V_HEAD:V_HEAD + 1, :]
        o_ref[...] = (acc_scr[0:V_HEAD, :] / l).T.astype(BF16)
        lse_ref[0, 0] = m_scr[...] + jnp.log2(l)

    return pl.pallas_call(
        body, name=name, grid=(N_HEADS, nq),
        in_specs=[pl.BlockSpec((1, tq, HEAD_PAD), lambda h, i: (h, i, 0)),
                  pl.BlockSpec((1, n, HEAD_PAD), lambda h, i: (h, 0, 0)),
                  pl.BlockSpec((1, n // tk, VT_ROWS, tk), lambda h, i: (h, 0, 0, 0))],
        out_specs=[pl.BlockSpec((tq, V_HEAD), lambda h, i: (i, h)),
                   pl.BlockSpec((1, 1, 1, tq), lambda h, i: (h, i, 0, 0))],
        out_shape=[jax.ShapeDtypeStruct((n, N_HEADS * V_HEAD), BF16), jax.ShapeDtypeStruct((N_HEADS, nq, 1, tq), F32)],
        scratch_shapes=[pltpu.VMEM((2, tk, tq), F32), pltpu.VMEM((1, tq), F32), pltpu.VMEM((VT_ROWS, tq), F32)],
        compiler_params=_params("arbitrary", "arbitrary"),
    )(q, k, vt)


def _mla_post_fwd(o, x, wout, name):
    n = x.shape[0]
    tm = _tiles(n)["rows"]

    def body(o_ref, x_ref, w_ref, out_ref):
        out_ref[...] = x_ref[...] + _dot(o_ref[...], w_ref[...])

    row = pl.BlockSpec((tm, D_MODEL), lambda i: (i, 0))
    return pl.pallas_call(
        body, name=name, grid=(n // tm,), in_specs=[row, row, _full((D_MODEL, D_MODEL))], out_specs=row,
        out_shape=jax.ShapeDtypeStruct((n, D_MODEL), F32), compiler_params=_params("arbitrary"),
    )(o, x, wout)


def _mla_out_bwd(dout, o, wout, name):
    n = dout.shape[0]
    t = _tiles(n)["bwd_q"]
    nq = n // t

    def body(do_ref, o_ref, w_ref, dob_ref, dpo_ref, dl_ref):
        dob = do_ref[...].astype(BF16)
        dob_ref[...] = dob
        dpo = _dot_nt(dob, w_ref[...])
        dpo_ref[...] = dpo.astype(BF16)
        ov = o_ref[...].astype(F32)
        for hd in range(N_HEADS):
            lanes = slice(hd * V_HEAD, (hd + 1) * V_HEAD)
            prod = dpo[:, lanes] * ov[:, lanes]
            dl_ref[hd, 0] = jnp.sum(prod.T, axis=0, keepdims=True)

    row = pl.BlockSpec((t, D_MODEL), lambda i: (i, 0))
    return pl.pallas_call(
        body, name=name, grid=(nq,), in_specs=[row, row, _full((D_MODEL, D_MODEL))],
        out_specs=[row, row, pl.BlockSpec((N_HEADS, 1, 1, t), lambda i: (0, i, 0, 0))],
        out_shape=[jax.ShapeDtypeStruct((n, D_MODEL), BF16), jax.ShapeDtypeStruct((n, D_MODEL), BF16),
                   jax.ShapeDtypeStruct((N_HEADS, nq, 1, t), F32)],
        compiler_params=_params("arbitrary"),
    )(dout, o, wout)


def _flash_bwd(q, k, v, dpo, lse, delta, name):
    n = q.shape[1]
    tq = _tiles(n)["bwd_q"]
    tk = 2 * tq
    nk = n // tk
    nqb = n // tq

    def body(k_ref, v_ref, q_ref, do_ref, lse_ref, dl_ref, dk_ref, dv_ref, dq_hbm,
             dq_acc, s_scr, dp_scr, dk_acc, dv_acc, sem):
        h = pl.program_id(0)
        j = pl.program_id(1)

        @pl.when(j == 0)
        def _():
            dq_acc[...] = jnp.zeros_like(dq_acc)

        dk_acc[...] = jnp.zeros_like(dk_acc)
        dv_acc[...] = jnp.zeros_like(dv_acc)
        kj = k_ref[0]
        vj = v_ref[0]
        npairs = nk - 1 - j

        def block(t):
            return jnp.where(t < 2 * npairs, 2 * j + 2 + t, 2 * j + (t - 2 * npairs))

        def scores(i, slot):
            rws = pl.ds(pl.multiple_of(i * tq, tq), tq)
            s_scr[slot] = _dot_nt(kj, q_ref[0, rws, :])
            dp_scr[slot] = _dot_nt(vj, do_ref[rws, :])

        def update(i, slot, diagonal=None):
            rws = pl.ds(pl.multiple_of(i * tq, tq), tq)
            p = jnp.exp2(s_scr[slot] - lse_ref[0, i])
            if diagonal is not None:
                krow = lax.broadcasted_iota(jnp.int32, (tk, tq), 0)
                qcol = lax.broadcasted_iota(jnp.int32, (tk, tq), 1) + diagonal * tq
                p = jnp.where(krow <= qcol, p, 0.0)
            dv_acc[...] += _dot(p.astype(BF16), do_ref[rws, :])
            ds = (p * (dp_scr[slot] - dl_ref[0, i])).astype(BF16)
            dk_acc[...] += _dot(ds, q_ref[0, rws, :])
            dq_acc[rws, :] += _dot_tn(ds, kj)

        scores(block(0), 0)

        def pair(jj, carry):
            scores(block(2 * jj + 1), 1)
            update(block(2 * jj), 0)
            scores(block(2 * jj + 2), 0)
            update(block(2 * jj + 1), 1)
            return carry

        lax.fori_loop(0, npairs, pair, 0)
        scores(2 * j + 1, 1)
        update(2 * j, 0, diagonal=0)
        update(2 * j + 1, 1, diagonal=1)
        dk_ref[0] = dk_acc[...] * (ATTN_SCALE / LOGIT_SCALE)
        dv_ref[0] = dv_acc[...]

        @pl.when(j == nk - 1)
        def _():
            dq_acc[...] = dq_acc[...] * ATTN_SCALE
            cp = pltpu.make_async_copy(dq_acc, dq_hbm.at[h], sem)
            cp.start()
            cp.wait()

    resident = dict(pipeline_mode=pl.Buffered(1))
    return pl.pallas_call(
        body, name=name, grid=(N_HEADS, nk),
        in_specs=[pl.BlockSpec((1, tk, HEAD_PAD), lambda h, j: (h, j, 0)),
                  pl.BlockSpec((1, tk, V_HEAD), lambda h, j: (h, j, 0)),
                  pl.BlockSpec((1, n, HEAD_PAD), lambda h, j: (h, 0, 0), **resident),
                  pl.BlockSpec((n, V_HEAD), lambda h, j: (0, h), **resident),
                  pl.BlockSpec((1, nqb, 1, tq), lambda h, j: (h, 0, 0, 0)),
                  pl.BlockSpec((1, nqb, 1, tq), lambda h, j: (h, 0, 0, 0))],
        out_specs=[pl.BlockSpec((1, tk, HEAD_PAD), lambda h, j: (h, j, 0)),
                   pl.BlockSpec((1, tk, V_HEAD), lambda h, j: (h, j, 0)),
                   pl.BlockSpec(memory_space=pl.ANY)],
        out_shape=[jax.ShapeDtypeStruct((N_HEADS, n, HEAD_PAD), F32), jax.ShapeDtypeStruct((N_HEADS, n, V_HEAD), F32),
                   jax.ShapeDtypeStruct((N_HEADS, n, HEAD_PAD), F32)],
        scratch_shapes=[pltpu.VMEM((n, HEAD_PAD), F32), pltpu.VMEM((2, tk, tq), F32), pltpu.VMEM((2, tk, tq), F32),
                        pltpu.VMEM((tk, HEAD_PAD), F32), pltpu.VMEM((tk, V_HEAD), F32), pltpu.SemaphoreType.DMA(())],
        compiler_params=_params("arbitrary", "arbitrary"),
    )(k, v, q, dpo, lse, delta)


def _mla_pre_bwd(x, dout, dq, dk, dv, gain, win, gq, gkv, wq, wkv, ghq, ghk, cos, sin_signed, name):
    n = x.shape[0]
    tm = _tiles(n)["mla_bwd"]
    hw = N_HEADS * HEAD_PAD

    def body(x_ref, do_ref, dq_ref, dk_ref, dv_ref, g_ref, win_ref, gq_ref, gkv_ref, wq_ref, wkv_ref, ghq_ref, ghk_ref,
             c_ref, s_ref, dx_ref, h_ref, dlat_ref, cqn_ref, dqp_ref, ckvn_ref, dkv_ref,
             dg_ref, dgq_ref, dgkv_ref, dghq_ref, dghk_ref):
        @pl.when(pl.program_id(0) == 0)
        def _():
            for ref in (dg_ref, dgq_ref, dgkv_ref, dghq_ref, dghk_ref):
                ref[...] = jnp.zeros_like(ref)

        g = g_ref[...]
        xhat, r = _rms_fwd(x_ref[...])
        h = (xhat * g).astype(BF16)
        h_ref[...] = h
        cq, ckv, kpe = _mla_latents(h, win_ref)
        cqhat, rcq = _rms_fwd(cq)
        ckvhat, rckv = _rms_fwd(ckv)
        cqn = (cqhat * gq_ref[...]).astype(BF16)
        ckvn = (ckvhat * gkv_ref[...]).astype(BF16)
        cqn_ref[...] = cqn
        ckvn_ref[...] = ckvn
        cos, sn = c_ref[...], s_ref[...]
        ghq, ghk = ghq_ref[...], ghk_ref[...]
        dcqn = jnp.zeros((tm, Q_LORA), F32)
        dckvn = jnp.zeros((tm, KV_LORA), F32)
        dkpe = jnp.zeros((tm, 128), F32)
        dghq = jnp.zeros((8, HEAD_PAD), F32)
        dghk = jnp.zeros((8, HEAD_PAD), F32)
        for hd in range(N_HEADS):
            rws = pl.ds(hd * HEAD_PAD, HEAD_PAD)
            lanes = slice(hd * HEAD_PAD, (hd + 1) * HEAD_PAD)
            qhat, rq = _rms_fwd(_dot_nt(cqn, wq_ref[rws, :]), QK_HEAD)
            gqn = _rope_bwd(dq_ref[hd], cos, sn)
            dghq = dghq + _rowsum8(gqn * qhat)
            dqpre = _rms_bwd(gqn, qhat, rq, ghq, QK_HEAD).astype(BF16)
            dqp_ref[:, lanes] = dqpre
            dcqn = dcqn + _dot(dqpre, wq_ref[rws, :])
            kvh = _dot_nt(ckvn, wkv_ref[rws, :])
            khat, rk = _rms_fwd(jnp.concatenate([kvh[:, :QK_NOPE], kpe], axis=1), QK_HEAD)
            gkn = _rope_bwd(dk_ref[hd], cos, sn)
            dghk = dghk + _rowsum8(gkn * khat)
            dkpre = _rms_bwd(gkn, khat, rk, ghk, QK_HEAD)
            dkpe = dkpe + dkpre[:, QK_NOPE:]
            dkvh = jnp.concatenate([dkpre[:, :QK_NOPE], dv_ref[hd]], axis=1).astype(BF16)
            dkv_ref[:, lanes] = dkvh
            dckvn = dckvn + _dot(dkvh, wkv_ref[rws, :])
        dghq_ref[...] += dghq
        dghk_ref[...] += dghk
        dgq_ref[...] += _rowsum8(dcqn * cqhat)
        dgkv_ref[...] += _rowsum8(dckvn * ckvhat)
        dlat = jnp.concatenate([_rms_bwd(dcqn, cqhat, rcq, gq_ref[...]), _rms_bwd(dckvn, ckvhat, rckv, gkv_ref[...]),
                                dkpe], axis=1).astype(BF16)
        dlat_ref[...] = dlat
        dh = _dot(dlat, win_ref[...])
        dg_ref[...] += _rowsum8(dh * xhat)
        dx_ref[...] = do_ref[...] + _rms_bwd(dh, xhat, r, g)

    row = lambda w: pl.BlockSpec((tm, w), lambda i: (i, 0))
    head = lambda w: pl.BlockSpec((N_HEADS, tm, w), lambda i: (0, i, 0))
    sds = jax.ShapeDtypeStruct
    return pl.pallas_call(
        body, name=name, grid=(n // tm,),
        in_specs=[row(D_MODEL), row(D_MODEL), head(HEAD_PAD), head(HEAD_PAD), head(V_HEAD), _full((1, D_MODEL)),
                  _full((LAT_PAD, D_MODEL)), _full((1, Q_LORA)), _full((1, KV_LORA)), _full((hw, Q_LORA)),
                  _full((hw, KV_LORA)), _full((1, HEAD_PAD)), _full((1, HEAD_PAD)), row(128), row(128)],
        out_specs=[row(D_MODEL), row(D_MODEL), row(LAT_PAD), row(Q_LORA), row(hw), row(KV_LORA), row(hw),
                   _full((8, D_MODEL)), _full((8, Q_LORA)), _full((8, KV_LORA)), _full((8, HEAD_PAD)),
                   _full((8, HEAD_PAD))],
        out_shape=[sds((n, D_MODEL), F32), sds((n, D_MODEL), BF16), sds((n, LAT_PAD), BF16), sds((n, Q_LORA), BF16),
                   sds((n, hw), BF16), sds((n, KV_LORA), BF16), sds((n, hw), BF16), sds((8, D_MODEL), F32),
                   sds((8, Q_LORA), F32), sds((8, KV_LORA), F32), sds((8, HEAD_PAD), F32), sds((8, HEAD_PAD), F32)],
        compiler_params=_params("arbitrary"),
    )(x, dout, dq, dk, dv, gain, win, gq, gkv, wq, wkv, ghq, ghk, cos, sin_signed)


def _adamw(w, g, m, v, behind, name):
    rows, cols = w.shape
    tr = rows
    for cand in (512, 256, 128, 64, 32, 16, 8):
        if rows % cand == 0 and rows > cand:
            tr = cand
            break

    def body(w_ref, g_ref, m_ref, v_ref, behind_ref, d_ref, mo_ref, vo_ref):
        gv = g_ref[...]
        mn = ADAM_B1 * m_ref[...] + (1.0 - ADAM_B1) * gv
        vn = ADAM_B2 * v_ref[...] + (1.0 - ADAM_B2) * (gv * gv)
        m_hat = mn / (1.0 - ADAM_B1 ** ADAM_STEP)
        v_hat = vn / (1.0 - ADAM_B2 ** ADAM_STEP)
        d_ref[...] = -ADAM_LR * (m_hat / (jnp.sqrt(v_hat) + ADAM_EPS) + ADAM_WD * w_ref[...])
        mo_ref[...] = mn
        vo_ref[...] = vn

    blk = pl.BlockSpec((tr, cols), lambda i: (i, 0))
    return pl.pallas_call(
        body, name=name, grid=(rows // tr,), in_specs=[blk] * 4 + [pl.BlockSpec(memory_space=pl.ANY)],
        out_specs=[blk] * 3, out_shape=[jax.ShapeDtypeStruct((rows, cols), F32)] * 3,
        compiler_params=_params("arbitrary"),
    )(w, g, m, v, behind)


def _sum_parts(parts, name):
    k, r, c = parts.shape
    tr = min(r, 512)

    def body(p_ref, o_ref):
        acc = p_ref[0]
        for j in range(1, k):
            acc = acc + p_ref[j]
        o_ref[...] = acc

    return pl.pallas_call(
        body, name=name, grid=(r // tr,), in_specs=[pl.BlockSpec((k, tr, c), lambda i: (0, i, 0))],
        out_specs=pl.BlockSpec((tr, c), lambda i: (i, 0)), out_shape=jax.ShapeDtypeStruct((r, c), parts.dtype),
        compiler_params=_params("arbitrary"),
    )(parts)


def _row_tile(r, most=256):
    best = r
    for cand in range(8, most + 1, 8):
        if r % cand == 0:
            best = cand
    return best


def _sum_exchange(mine, landed, me, name):
    _, r, c = mine.shape
    tr = _row_tile(r)

    def body(me_ref, m_ref, l_ref, o_ref):
        acc = m_ref[0]
        for k in range(1, N_DEV):
            acc = acc + l_ref[k]
        o_ref[...] = acc

    return pl.pallas_call(
        body, name=name,
        grid_spec=pltpu.PrefetchScalarGridSpec(
            num_scalar_prefetch=1, grid=(r // tr,),
            in_specs=[pl.BlockSpec((1, tr, c), lambda i, me_ref: (me_ref[0], i, 0)),
                      pl.BlockSpec((N_DEV, tr, c), lambda i, me_ref: (0, i, 0))],
            out_specs=pl.BlockSpec((tr, c), lambda i, me_ref: (i, 0))),
        out_shape=jax.ShapeDtypeStruct((r, c), mine.dtype), compiler_params=_params("arbitrary"),
    )(me, mine, landed)


MESH = pl.DeviceIdType.MESH


def _all_gather(x, name):
    r, c = x.shape

    def body(x_ref, out_ref, send_sems, recv_sems, local_sem):
        mx, my, mc = lax.axis_index("x"), lax.axis_index("y"), lax.axis_index("c")
        me, sibling = (mx, my, mc), (mx, my, 1 - mc)
        chips = [(1 - mx, my), (mx, 1 - my), (1 - mx, 1 - my)]

        def slot(px, py, pc):
            return out_ref.at[4 * px + 2 * py + pc]

        def copy(k, block, to, src=None):
            return pltpu.make_async_remote_copy(
                src_ref=slot(*block) if src is None else src, dst_ref=slot(*block),
                send_sem=send_sems.at[k], recv_sem=recv_sems.at[k], device_id=to, device_id_type=MESH)

        mine = pltpu.make_async_copy(x_ref, slot(*me), local_sem)
        mine.start()
        first = [copy(0, me, sibling, src=x_ref)]
        first += [copy(1 + j, me, (*chip, mc), src=x_ref) for j, chip in enumerate(chips)]
        for cp in first:
            cp.start()
        passed = [copy(4 + j, (*chip, mc), sibling) for j, chip in enumerate(chips)]
        for j, chip in enumerate(chips):
            copy(1 + j, (*chip, mc), me).wait_recv()
            passed[j].start()
        copy(0, sibling, me).wait_recv()
        for j, chip in enumerate(chips):
            copy(4 + j, (*chip, 1 - mc), me).wait_recv()
        for cp in first + passed:
            cp.wait_send()
        mine.wait()

    any_spec = pl.BlockSpec(memory_space=pl.ANY)
    return pl.pallas_call(
        body, name=name, in_specs=[any_spec], out_specs=any_spec,
        out_shape=jax.ShapeDtypeStruct((N_DEV, r, c), x.dtype),
        scratch_shapes=[pltpu.SemaphoreType.DMA((7,)), pltpu.SemaphoreType.DMA((7,)), pltpu.SemaphoreType.DMA(())],
    )(x)


HBM_SPEC = pl.BlockSpec(memory_space=pltpu.HBM)
SEM_SPEC = pl.BlockSpec(memory_space=pltpu.SEMAPHORE)
SPLIT_EFFECT = pltpu.SideEffectType.DATAFLOW_SIDE_EFFECTING


def _exchange_copies(src_ref, land_ref, send_sems, recv_sems, gather):
    mx, my, mc = lax.axis_index("x"), lax.axis_index("y"), lax.axis_index("c")
    me = 4 * mx + 2 * my + mc
    copies = []
    for k in range(1, N_DEV):
        px = 1 - mx if k & 4 else mx
        py = 1 - my if k & 2 else my
        pc = 1 - mc if k & 1 else mc
        src = src_ref if gather else src_ref.at[4 * px + 2 * py + pc]
        dst = land_ref.at[me] if gather else land_ref.at[k]
        copies.append(pltpu.make_async_remote_copy(
            src_ref=src, dst_ref=dst, send_sem=send_sems.at[k - 1], recv_sem=recv_sems.at[k - 1],
            device_id=(px, py, pc), device_id_type=MESH))
    return copies


def _exchange_start(src, after, gather, name):
    land_shape = (N_DEV,) + src.shape[-2:]

    def body(src_ref, land_ref, after_ref, send_sems, recv_sems, src_thru, land_thru, token):
        for cp in _exchange_copies(src_ref, land_ref, send_sems, recv_sems, gather):
            cp.start()
        token[...] = jnp.zeros_like(token)

    return pl.pallas_call(
        body, name=name,
        out_shape=(pltpu.SemaphoreType.DMA((N_DEV - 1,)), pltpu.SemaphoreType.DMA((N_DEV - 1,)),
                   pltpu.HBM(src.shape, src.dtype), pltpu.HBM(land_shape, src.dtype),
                   jax.ShapeDtypeStruct((8, 128), F32)),
        in_specs=(HBM_SPEC, HBM_SPEC, pl.BlockSpec(memory_space=pl.ANY)),
        out_specs=(SEM_SPEC, SEM_SPEC, HBM_SPEC, HBM_SPEC, pl.BlockSpec(memory_space=pltpu.VMEM)),
        input_output_aliases={0: 2, 1: 3},
        compiler_params=pltpu.CompilerParams(has_side_effects=SPLIT_EFFECT),
    )(pltpu.with_memory_space_constraint(src, pltpu.HBM),
      pltpu.with_memory_space_constraint(lax.empty(land_shape, src.dtype), pltpu.HBM), after)


def _exchange_wait(started, after, gather, name):
    send_sems, recv_sems, src_thru, land_thru, _ = started

    def body(src_ref, land_ref, send_sems, recv_sems, after_ref, src_out, land_out):
        for cp in _exchange_copies(src_ref, land_ref, send_sems, recv_sems, gather):
            cp.wait_send()
            cp.wait_recv()

    return pl.pallas_call(
        body, name=name,
        out_shape=(pltpu.HBM(src_thru.shape, src_thru.dtype), pltpu.HBM(land_thru.shape, land_thru.dtype)),
        in_specs=(HBM_SPEC, HBM_SPEC, SEM_SPEC, SEM_SPEC, pl.BlockSpec(memory_space=pl.ANY)),
        out_specs=(HBM_SPEC, HBM_SPEC), input_output_aliases={0: 0, 1: 1},
        compiler_params=pltpu.CompilerParams(has_side_effects=SPLIT_EFFECT),
    )(src_thru, land_thru, send_sems, recv_sems, after)


FFN_ROWS = D_FF // N_DEV
WIN_ROWS = (Q_LORA + KV_LORA + QK_ROPE) // N_DEV
WIN_ROWS_PAD = 144
WQ_ROWS = QK_HEAD * Q_LORA // D_MODEL
WKV_ROWS = 256 * KV_LORA // D_MODEL
WOUT_ROWS = V_HEAD
POOL_ROWS = 4 * 32 * POOL_GROUP // D_MODEL


def _t(w):
    return jnp.swapaxes(w, -1, -2)


def _ffn_segments(f, i):
    return [(f + "_w_gate", i, FFN_ROWS), (f + "_w_up", i, FFN_ROWS), (f + "_w_down", i, FFN_ROWS)]


def _mixer_segments(i):
    j = i // 2
    if i % 2 == 0:
        return [("pool_w", j, POOL_ROWS)]
    return [("mla_w_in", j, WIN_ROWS_PAD), ("mla_w_q_up", j, WQ_ROWS), ("mla_w_kv_up", j, WKV_ROWS),
            ("mla_w_out", j, WOUT_ROWS)]


def _pack_shards(p, segs, dtype):
    parts = []
    for name, idx, _ in segs:
        w = p[name][idx]
        if name.endswith("w_gate") or name.endswith("w_up"):
            w = _t(w)
        elif name == "mla_w_in":
            w = jnp.pad(_t(w), ((0, WIN_ROWS_PAD - WIN_ROWS), (0, 0)))
        elif name == "mla_w_q_up":
            w = _t(w).reshape(WQ_ROWS, D_MODEL)
        elif name == "mla_w_kv_up":
            w = _t(w).reshape(WKV_ROWS, D_MODEL)
        elif name == "pool_w":
            w = w.reshape(POOL_ROWS, D_MODEL)
        parts.append(w.astype(dtype))
    return jnp.concatenate(parts, axis=0)


def _unpack_gathered(g, segs):
    out = {}
    off = 0
    for name, layer, rows in segs:
        seg = g[:, off:off + rows, :]
        off += rows
        if name == "mla_w_in":
            w = seg[:, :WIN_ROWS].reshape(N_DEV * WIN_ROWS, D_MODEL)
            w = jnp.pad(w, ((0, LAT_PAD - N_DEV * WIN_ROWS), (0, 0)))
        elif name == "mla_w_q_up":
            w = seg.reshape(N_HEADS, QK_HEAD, Q_LORA)
            w = jnp.pad(w, ((0, 0), (0, HEAD_PAD - QK_HEAD), (0, 0))).reshape(N_HEADS * HEAD_PAD, Q_LORA)
        elif name == "mla_w_kv_up":
            w = seg.reshape(N_HEADS * 256, KV_LORA)
        elif name == "pool_w":
            w = seg.reshape(N_DEV, 4, 32, POOL_GROUP).transpose(1, 0, 2, 3).reshape(4, POOL_GROUP, POOL_GROUP)
        else:
            w = seg.reshape(N_DEV * rows, D_MODEL)
        out[(name, layer)] = w
    return out


def _pack_grads(gr, segments):
    segs = []
    for name, layer, rows in segments:
        g = gr[(name, layer)]
        if name == "mla_w_in":
            g = g[:N_DEV * WIN_ROWS].reshape(N_DEV, WIN_ROWS, D_MODEL)
            g = jnp.pad(g, ((0, 0), (0, WIN_ROWS_PAD - WIN_ROWS), (0, 0)))
        elif name == "mla_w_q_up":
            g = g.reshape(N_HEADS, HEAD_PAD, Q_LORA)[:, :QK_HEAD].reshape(N_DEV, WQ_ROWS, D_MODEL)
        elif name == "mla_w_kv_up":
            g = g.reshape(N_DEV, WKV_ROWS, D_MODEL)
        elif name == "pool_w":
            g = g.reshape(4, N_DEV, 32, POOL_GROUP).transpose(1, 0, 2, 3).reshape(N_DEV, POOL_ROWS, D_MODEL)
        else:
            g = g.reshape(N_DEV, rows, D_MODEL)
        segs.append(g)
    return jnp.concatenate(segs, axis=1)


def _unpack_shard_grads(flat, segments):
    per = {}
    off = 0
    for name, layer, rows in segments:
        seg = flat[off:off + rows]
        off += rows
        if name.endswith("w_gate") or name.endswith("w_up"):
            g = _t(seg)
        elif name == "mla_w_in":
            g = _t(seg[:WIN_ROWS])
        elif name == "mla_w_q_up":
            g = _t(seg.reshape(QK_HEAD, Q_LORA))
        elif name == "mla_w_kv_up":
            g = _t(seg.reshape(256, KV_LORA))
        elif name == "pool_w":
            g = seg.reshape(4, 32, POOL_GROUP)
        else:
            g = seg
        per[(name, layer)] = g
    return per


def _pad_lanes(v, width):
    return jnp.pad(v, ((0, 0), (0, width - v.shape[-1])))


def kernel(x, positions, ffn1_norm, ffn1_w_gate, ffn1_w_up, ffn1_w_down, mix_norm, pool_w, pool_scale, mla_w_in, mla_q_norm, mla_w_q_up, mla_kv_norm, mla_w_kv_up, mla_q_head_norm, mla_k_head_norm, mla_w_out, ffn2_norm, ffn2_w_gate, ffn2_w_up, ffn2_w_down, loss_target, m_ffn1_norm, m_ffn1_w_gate, m_ffn1_w_up, m_ffn1_w_down, m_mix_norm, m_pool_w, m_pool_scale, m_mla_w_in, m_mla_q_norm, m_mla_w_q_up, m_mla_kv_norm, m_mla_w_kv_up, m_mla_q_head_norm, m_mla_k_head_norm, m_mla_w_out, m_ffn2_norm, m_ffn2_w_gate, m_ffn2_w_up, m_ffn2_w_down, v_ffn1_norm, v_ffn1_w_gate, v_ffn1_w_up, v_ffn1_w_down, v_mix_norm, v_pool_w, v_pool_scale, v_mla_w_in, v_mla_q_norm, v_mla_w_q_up, v_mla_kv_norm, v_mla_w_kv_up, v_mla_q_head_norm, v_mla_k_head_norm, v_mla_w_out, v_ffn2_norm, v_ffn2_w_gate, v_ffn2_w_up, v_ffn2_w_down):
    args = (x, positions, ffn1_norm, ffn1_w_gate, ffn1_w_up, ffn1_w_down, mix_norm, pool_w, pool_scale, mla_w_in,
            mla_q_norm, mla_w_q_up, mla_kv_norm, mla_w_kv_up, mla_q_head_norm, mla_k_head_norm, mla_w_out, ffn2_norm,
            ffn2_w_gate, ffn2_w_up, ffn2_w_down)
    p = dict(zip(NAMES, args))
    moments_m = dict(zip(WEIGHTS, (m_ffn1_norm, m_ffn1_w_gate, m_ffn1_w_up, m_ffn1_w_down, m_mix_norm, m_pool_w, m_pool_scale, m_mla_w_in, m_mla_q_norm, m_mla_w_q_up, m_mla_kv_norm, m_mla_w_kv_up, m_mla_q_head_norm, m_mla_k_head_norm, m_mla_w_out, m_ffn2_norm, m_ffn2_w_gate, m_ffn2_w_up, m_ffn2_w_down)))
    moments_v = dict(zip(WEIGHTS, (v_ffn1_norm, v_ffn1_w_gate, v_ffn1_w_up, v_ffn1_w_down, v_mix_norm, v_pool_w, v_pool_scale, v_mla_w_in, v_mla_q_norm, v_mla_w_q_up, v_mla_kv_norm, v_mla_w_kv_up, v_mla_q_head_norm, v_mla_k_head_norm, v_mla_w_out, v_ffn2_norm, v_ffn2_w_gate, v_ffn2_w_up, v_ffn2_w_down)))
    dev = 4 * lax.axis_index("x") + 2 * lax.axis_index("y") + lax.axis_index("c")

    xs = x[0]
    n = xs.shape[0]
    target = loss_target[0]

    inv_freq = 1.0 / (ROPE_THETA ** (jnp.arange(0, QK_ROPE, 2, dtype=F32) / QK_ROPE))
    ang = positions[0].astype(F32)[..., None] * inv_freq
    cos, sin = jnp.cos(ang), jnp.sin(ang)
    zero = jnp.zeros((n, 128 - QK_ROPE), F32)
    rope_cos = jnp.concatenate([cos, cos, zero], axis=1)
    rope_sin = jnp.concatenate([-sin, sin, zero], axis=1)

    ag_groups = [_ffn_segments("ffn1", 0), _mixer_segments(0) + _ffn_segments("ffn2", 0)]
    ag_groups += [_ffn_segments("ffn1", i) + _mixer_segments(i) + _ffn_segments("ffn2", i) for i in range(1, DEPTH)]
    shards = [_pack_shards(p, segs, BF16) for segs in ag_groups]
    w = {}
    no_token = jnp.zeros((8, 128), F32)

    def tied(gain, token):
        return gain + token[0, 0]

    gains_local = jnp.concatenate([_pad_lanes(mla_q_norm, 128), _pad_lanes(mla_kv_norm, 128)], axis=0)
    gains_all = _all_gather(jnp.pad(gains_local, ((0, 4), (0, 0))), "gains_all_gather")
    q_norm_full = gains_all[:, 0:2, :Q_LORA // N_DEV].transpose(1, 0, 2).reshape(2, Q_LORA)
    kv_norm_full = gains_all[:, 2:4, :KV_LORA // N_DEV].transpose(1, 0, 2).reshape(2, KV_LORA)
    ghq = _pad_lanes(mla_q_head_norm, HEAD_PAD)
    ghk = _pad_lanes(mla_k_head_norm, HEAD_PAD)

    def mla_weights(j):
        return (w[("mla_w_in", j)], q_norm_full[j:j + 1], kv_norm_full[j:j + 1], w[("mla_w_q_up", j)],
                w[("mla_w_kv_up", j)], ghq[j:j + 1], ghk[j:j + 1], rope_cos, rope_sin)

    saved = []
    cur = xs
    gathers = [_exchange_start(shards[0], gains_all, True, "ag_start_0")]

    def gather_step(after):
        g = len(gathers) - 1
        _, landed = _exchange_wait(gathers[g], after, True, f"ag_wait_{g}")
        w.update(_unpack_gathered(lax.dynamic_update_slice(landed, shards[g][None], (dev, 0, 0)), ag_groups[g]))
        if g + 1 == len(ag_groups):
            return no_token
        gathers.append(_exchange_start(shards[g + 1], landed, True, f"ag_start_{g + 1}"))
        return gathers[-1][4]

    for i in range(DEPTH):
        j = i // 2
        st = {"x0": cur}
        token = gather_step(cur)
        cur, st["a1"], st["u1"] = _ffn_fwd(cur, tied(ffn1_norm[i:i + 1], token), w[("ffn1_w_gate", i)],
                                           w[("ffn1_w_up", i)], w[("ffn1_w_down", i)], f"ffn1_fwd_{i}")
        st["x1"] = cur
        if i == 0:
            token = gather_step(cur)
        if i % 2 == 0:
            cur = _pool_fwd(cur, tied(mix_norm[i:i + 1], token), w[("pool_w", j)], pool_scale[j:j + 1],
                            f"pool_fwd_{i}")
        else:
            st["q"], st["k"], st["v"], vt = _mla_pre_fwd(cur, mix_norm[i:i + 1], *mla_weights(j), f"mla_pre_fwd_{i}")
            st["o"], lse = _flash_fwd(st["q"], st["k"], vt, f"flash_fwd_{i}")
            tq_bwd = _tiles(n)["bwd_q"]
            st["lse"] = lse.reshape(N_HEADS, n // tq_bwd, 1, tq_bwd)
            cur = _mla_post_fwd(st["o"], cur, w[("mla_w_out", j)], f"mla_post_fwd_{i}")
        st["x2"] = cur
        cur, st["a2"], st["u2"] = _ffn_fwd(cur, ffn2_norm[i:i + 1], w[("ffn2_w_gate", i)], w[("ffn2_w_up", i)],
                                           w[("ffn2_w_down", i)], f"ffn2_fwd_{i}")
        saved.append(st)

    dcur, sq_err = _loss_head(cur, target, "loss_head")
    loss_part = 0.5 * jnp.sum(sq_err) * (1.0 / D_MODEL)

    gr = {}
    small = {k: [None] * DEPTH for k in ("ffn1_norm", "mix_norm", "ffn2_norm")}
    small.update({k: [None] * (DEPTH // 2) for k in ("pool_scale", "mla_q_norm", "mla_kv_norm", "mla_q_head_norm",
                                                     "mla_k_head_norm")})

    me = jnp.reshape(dev, (1,)).astype(jnp.int32)
    grads = {}
    in_flight = []

    def reduce_start(segs, tag):
        started = _exchange_start(_pack_grads(gr, segs), dcur, False, f"rs_start_{tag}")
        reduce_finish(started[4])
        in_flight.append((started, segs, tag))
        return started[4]

    def reduce_finish(after):
        if in_flight:
            started, segs, tag = in_flight.pop()
            mine, landed = _exchange_wait(started, after, False, f"rs_wait_{tag}")
            grads.update(_unpack_shard_grads(_sum_exchange(mine, landed, me, f"rs_sum_{tag}"), segs))

    def ffn_backward(f, i, x_in, a, u, gain, dout):
        dx, h, dob, y, da, du, dg = _ffn_bwd(x_in, dout, a, u, gain, w[(f + "_w_gate", i)], w[(f + "_w_up", i)],
                                             w[(f + "_w_down", i)], f"{f}_bwd_{i}")
        gr[(f + "_w_gate", i)] = _tn_matmul(da, h, f"{f}_dgate_{i}")
        gr[(f + "_w_up", i)] = _tn_matmul(du, h, f"{f}_dup_{i}")
        gr[(f + "_w_down", i)] = _tn_matmul(y, dob, f"{f}_ddown_{i}")
        small[f + "_norm"][i] = jnp.sum(dg, axis=0)
        return dx

    token = jnp.zeros((8, 128), F32)
    for i in reversed(range(DEPTH)):
        j = i // 2
        st = saved[i]
        dcur = ffn_backward("ffn2", i, st["x2"], st["a2"], st["u2"], tied(ffn2_norm[i:i + 1], token), dcur)
        if i % 2 == 0:
            dcur, dpw, dsc, dg = _pool_bwd(st["x1"], dcur, mix_norm[i:i + 1], w[("pool_w", j)], pool_scale[j:j + 1],
                                           f"pool_bwd_{i}")
            gr[("pool_w", j)] = dpw
            small["pool_scale"][j] = jnp.sum(dsc, axis=0)
            small["mix_norm"][i] = jnp.sum(dg, axis=0)
        else:
            dob, dpo, delta = _mla_out_bwd(dcur, st["o"], w[("mla_w_out", j)], f"mla_out_bwd_{i}")
            gr[("mla_w_out", j)] = _tn_matmul(st["o"], dob, f"mla_dout_{i}")
            dk, dv, dq = _flash_bwd(st["q"], st["k"], st["v"], dpo, st["lse"], delta, f"flash_bwd_{i}")
            (dcur, h, dlat, cqn, dqp, ckvn, dkv, dg, dgq, dgkv, dghq, dghk) = _mla_pre_bwd(
                st["x1"], dcur, dq, dk, dv, mix_norm[i:i + 1], *mla_weights(j), f"mla_pre_bwd_{i}")
            gr[("mla_w_in", j)] = _tn_matmul(dlat, h, f"mla_din_{i}")
            gr[("mla_w_q_up", j)] = _tn_matmul(dqp, cqn, f"mla_dqup_{i}")
            gr[("mla_w_kv_up", j)] = _tn_matmul(dkv, ckvn, f"mla_dkvup_{i}")
            small["mix_norm"][i] = jnp.sum(dg, axis=0)
            small["mla_q_norm"][j] = _pad_lanes(jnp.sum(dgq, axis=0)[None], D_MODEL)[0]
            small["mla_kv_norm"][j] = _pad_lanes(jnp.sum(dgkv, axis=0)[None], D_MODEL)[0]
            small["mla_q_head_norm"][j] = _pad_lanes(jnp.sum(dghq, axis=0)[None], D_MODEL)[0]
            small["mla_k_head_norm"][j] = _pad_lanes(jnp.sum(dghk, axis=0)[None], D_MODEL)[0]
        token = reduce_start(_ffn_segments("ffn2", i) + _mixer_segments(i), f"a{i}")
        if i > 0:
            dcur = ffn_backward("ffn1", i, st["x0"], st["a1"], st["u1"], tied(ffn1_norm[i:i + 1], token), dcur)
            token = reduce_start(_ffn_segments("ffn1", i), f"b{i}")
    st = saved[0]
    dcur, h, dob, y, da, du, dg = _ffn_bwd(st["x0"], dcur, st["a1"], st["u1"], tied(ffn1_norm[0:1], token),
                                           w[("ffn1_w_gate", 0)], w[("ffn1_w_up", 0)], w[("ffn1_w_down", 0)],
                                           "ffn1_bwd_0")
    small["ffn1_norm"][0] = jnp.sum(dg, axis=0)
    grad_x = dcur[None]
    small_order = ("ffn1_norm", "mix_norm", "ffn2_norm", "pool_scale", "mla_q_norm", "mla_kv_norm",
                   "mla_q_head_norm", "mla_k_head_norm")
    rows = [r for k in small_order for r in small[k]]
    rows.append(jnp.zeros((D_MODEL,), F32).at[0].set(loss_part))
    rows.append(jnp.zeros((D_MODEL,), F32))
    small_sum = _sum_parts(_all_gather(jnp.stack(rows), "small_all_gather"), "small_sum")
    loss = small_sum[SM_ROWS - 2, 0]
    token = small_sum
    for seg, lhs, rhs in zip(_ffn_segments("ffn1", 0), (da, du, y), (h, h, dob)):
        gr[seg[:2]] = _tn_matmul(lhs, rhs, f"ffn1_d{seg[0][7:]}_0", behind=token)
        token = reduce_start([seg], f"b0_{seg[0][7:]}")
    last = "ffn1_w_down"

    counts = {k: (DEPTH // 2 if k.startswith(("mla", "pool")) else DEPTH) for k in WEIGHTS}

    def stacked(k):
        return jnp.stack([grads[(k, idx)] for idx in range(counts[k])])

    grads.update({k: stacked(k) for k in WEIGHTS if (k, 0) in grads and k != last})
    off = 0
    for k in small_order:
        cnt = len(small[k])
        g = small_sum[off:off + cnt]
        off += cnt
        if k == "mla_q_norm":
            g = lax.dynamic_slice_in_dim(g[:, :Q_LORA], dev * (Q_LORA // N_DEV), Q_LORA // N_DEV, axis=1)
        elif k == "mla_kv_norm":
            g = lax.dynamic_slice_in_dim(g[:, :KV_LORA], dev * (KV_LORA // N_DEV), KV_LORA // N_DEV, axis=1)
        elif k in ("mla_q_head_norm", "mla_k_head_norm"):
            g = g[:, :QK_HEAD]
        grads[k] = g

    deltas, new_m, new_v = {}, {}, {}

    def update(k, behind):
        shape = p[k].shape
        view = (-1, shape[-1])
        d, mn, vn = _adamw(p[k].reshape(view), grads[k].reshape(view), moments_m[k].reshape(view),
                           moments_v[k].reshape(view), behind, "adamw_" + k)
        deltas[k], new_m[k], new_v[k] = d.reshape(shape), mn.reshape(shape), vn.reshape(shape)
        grads[k] = grads[k].reshape(shape)
        return d

    done = token
    for k in WEIGHTS:
        if k != last:
            done = update(k, done)
    reduce_finish(done)
    grads[last] = stacked(last)
    update(last, done)

    return (loss, grad_x, *[grads[k] for k in WEIGHTS], *[deltas[k] for k in WEIGHTS],
            *[new_m[k] for k in WEIGHTS], *[new_v[k] for k in WEIGHTS])
```

```python
import functools

import jax
import jax.numpy as jnp
from jax import lax
from jax.experimental import pallas as pl
from jax.experimental.pallas import tpu as pltpu

F32 = jnp.float32
BF16 = jnp.bfloat16

D_MODEL = 1024
DEPTH = 4
D_FF = 2816
POOL_WINDOWS = (2, 4, 8, 16)
POOL_GROUP = 256
POOL_HALO = 16
N_HEADS = 8
QK_NOPE = 128
QK_ROPE = 64
QK_HEAD = 192
V_HEAD = 128
Q_LORA = 768
KV_LORA = 256
ROPE_THETA = 10000.0
EPS = 1e-6
FFN_HALF = 0.5
ADAM_LR = 0.001
ADAM_B1 = 0.9
ADAM_B2 = 0.999
ADAM_EPS = 1e-08
ADAM_WD = 0.01
ADAM_STEP = 10

N_DEV = 8
HEAD_PAD = 256
LAT_PAD = 1152
VT_ROWS = 144
LOG2_E = 1.4426950408889634
ATTN_SCALE = QK_HEAD ** -0.5
LOGIT_SCALE = ATTN_SCALE * LOG2_E
V7X_VMEM_LIMIT = 56 * 1024 * 1024
FF_CHUNK = 256
SM_ROWS = 24

NAMES = ['x', 'positions', 'ffn1_norm', 'ffn1_w_gate', 'ffn1_w_up', 'ffn1_w_down', 'mix_norm', 'pool_w',
         'pool_scale', 'mla_w_in', 'mla_q_norm', 'mla_w_q_up', 'mla_kv_norm', 'mla_w_kv_up', 'mla_q_head_norm',
         'mla_k_head_norm', 'mla_w_out', 'ffn2_norm', 'ffn2_w_gate', 'ffn2_w_up', 'ffn2_w_down']
WEIGHTS = NAMES[2:]


def _tiles(n):
    return dict(ffn_fwd=min(512, n), ffn_bwd=min(256, n), fwd_k=min(512, n // 2), bwd_q=min(512, n // 2),
                mla_bwd=min(256, n), pool=min(512, n), tn=min(2048, n), rows=min(1024, n))


def _params(*sem):
    return pltpu.CompilerParams(dimension_semantics=sem, vmem_limit_bytes=V7X_VMEM_LIMIT)


def _dot(a, b):
    return jnp.dot(a, b, preferred_element_type=F32)


def _dot_nt(a, b):
    return lax.dot_general(a, b, (((1,), (1,)), ((), ())), preferred_element_type=F32)


def _dot_tn(a, b):
    return lax.dot_general(a, b, (((0,), (0,)), ((), ())), preferred_element_type=F32)


def _rowsum8(v):
    rows, w = v.shape
    return jnp.sum(v.reshape(rows // 8, 8, w), axis=0)


def _sigmoid(a):
    return 1.0 / (1.0 + jnp.exp(-a))


def _rms_fwd(x, width=None):
    width = x.shape[-1] if width is None else width
    r = lax.rsqrt(jnp.sum(x * x, axis=-1, keepdims=True) * (1.0 / width) + EPS)
    return x * r, r


def _rms_bwd(dy, xhat, r, gain, width=None):
    width = xhat.shape[-1] if width is None else width
    t = dy * gain
    return r * (t - xhat * (jnp.sum(t * xhat, axis=-1, keepdims=True) * (1.0 / width)))


def _full(shape):
    return pl.BlockSpec(shape, lambda *_: (0,) * len(shape))


def _load_weights(srcs, dsts, sems):
    copies = [pltpu.make_async_copy(s, d, sems.at[i]) for i, (s, d) in enumerate(zip(srcs, dsts))]
    for cp in copies:
        cp.start()
    for cp in copies:
        cp.wait()


def _ffn_fwd(x, gain, wg_t, wu_t, wd, name):
    n = x.shape[0]
    tm = _tiles(n)["ffn_fwd"]

    def body(x_ref, g_ref, wg_hbm, wu_hbm, wd_hbm, out_ref, a_ref, u_ref, wg_v, wu_v, wd_v, sems):
        @pl.when(pl.program_id(0) == 0)
        def _():
            _load_weights((wg_hbm, wu_hbm, wd_hbm), (wg_v, wu_v, wd_v), sems)

        xt = x_ref[...]
        xhat, _ = _rms_fwd(xt)
        h = (xhat * g_ref[...]).astype(BF16)
        acc = jnp.zeros((tm, D_MODEL), F32)
        for c in range(D_FF // FF_CHUNK):
            sl = pl.ds(c * FF_CHUNK, FF_CHUNK)
            a = _dot_nt(h, wg_v[sl, :])
            u = _dot_nt(h, wu_v[sl, :])
            a_ref[:, sl] = a.astype(BF16)
            u_ref[:, sl] = u.astype(BF16)
            y = (a * _sigmoid(a) * u).astype(BF16)
            acc = acc + _dot(y, wd_v[sl, :])
        out_ref[...] = xt + FFN_HALF * acc

    any_spec = pl.BlockSpec(memory_space=pl.ANY)
    return pl.pallas_call(
        body, name=name, grid=(n // tm,),
        in_specs=[pl.BlockSpec((tm, D_MODEL), lambda i: (i, 0)), _full((1, D_MODEL)), any_spec, any_spec, any_spec],
        out_specs=[pl.BlockSpec((tm, D_MODEL), lambda i: (i, 0)), pl.BlockSpec((tm, D_FF), lambda i: (i, 0)),
                   pl.BlockSpec((tm, D_FF), lambda i: (i, 0))],
        out_shape=[jax.ShapeDtypeStruct((n, D_MODEL), F32), jax.ShapeDtypeStruct((n, D_FF), BF16),
                   jax.ShapeDtypeStruct((n, D_FF), BF16)],
        scratch_shapes=[pltpu.VMEM((D_FF, D_MODEL), BF16)] * 3 + [pltpu.SemaphoreType.DMA((3,))],
        compiler_params=_params("arbitrary"),
    )(x, gain, wg_t, wu_t, wd)


def _ffn_bwd(x, dout, a, u, gain, wg_t, wu_t, wd, name):
    n = x.shape[0]
    tm = _tiles(n)["ffn_bwd"]

    def body(x_ref, do_ref, a_ref, u_ref, g_ref, wg_hbm, wu_hbm, wd_hbm,
             dx_ref, h_ref, dob_ref, y_ref, da_ref, du_ref, dg_ref, wg_v, wu_v, wd_v, sems):
        @pl.when(pl.program_id(0) == 0)
        def _():
            _load_weights((wg_hbm, wu_hbm, wd_hbm), (wg_v, wu_v, wd_v), sems)
            dg_ref[...] = jnp.zeros_like(dg_ref)

        xt = x_ref[...]
        g = g_ref[...]
        xhat, r = _rms_fwd(xt)
        h_ref[...] = (xhat * g).astype(BF16)
        dout = do_ref[...]
        dob = (FFN_HALF * dout).astype(BF16)
        dob_ref[...] = dob
        for c in range(D_FF // FF_CHUNK):
            sl = pl.ds(c * FF_CHUNK, FF_CHUNK)
            dy = _dot_nt(dob, wd_v[sl, :])
            av = a_ref[:, sl].astype(F32)
            uv = u_ref[:, sl].astype(F32)
            s = _sigmoid(av)
            silu = av * s
            y_ref[:, sl] = (silu * uv).astype(BF16)
            du_ref[:, sl] = (dy * silu).astype(BF16)
            da_ref[:, sl] = (dy * uv * (s * (1.0 + av * (1.0 - s)))).astype(BF16)
        dh = _dot(da_ref[...], wg_v[...]) + _dot(du_ref[...], wu_v[...])
        dg_ref[...] += _rowsum8(dh * xhat)
        dx_ref[...] = dout + _rms_bwd(dh, xhat, r, g)

    any_spec = pl.BlockSpec(memory_space=pl.ANY)
    row_d = pl.BlockSpec((tm, D_MODEL), lambda i: (i, 0))
    row_f = pl.BlockSpec((tm, D_FF), lambda i: (i, 0))
    return pl.pallas_call(
        body, name=name, grid=(n // tm,),
        in_specs=[row_d, row_d, row_f, row_f, _full((1, D_MODEL)), any_spec, any_spec, any_spec],
        out_specs=[row_d, row_d, row_d, row_f, row_f, row_f, _full((8, D_MODEL))],
        out_shape=[jax.ShapeDtypeStruct((n, D_MODEL), F32), jax.ShapeDtypeStruct((n, D_MODEL), BF16),
                   jax.ShapeDtypeStruct((n, D_MODEL), BF16), jax.ShapeDtypeStruct((n, D_FF), BF16),
                   jax.ShapeDtypeStruct((n, D_FF), BF16), jax.ShapeDtypeStruct((n, D_FF), BF16),
                   jax.ShapeDtypeStruct((8, D_MODEL), F32)],
        scratch_shapes=[pltpu.VMEM((D_FF, D_MODEL), BF16)] * 3 + [pltpu.SemaphoreType.DMA((3,))],
        compiler_params=_params("arbitrary"),
    )(x, dout, a, u, gain, wg_t, wu_t, wd)


def _tn_matmul(a, b, name, behind=None):
    n, fa = a.shape
    db = b.shape[1]
    tk = _tiles(n)["tn"]
    tf = fa // 2 if (fa // 2) % 128 == 0 and fa > 1024 else fa
    behind = jnp.zeros((8, 128), F32) if behind is None else behind

    def body(a_ref, b_ref, behind_ref, o_ref):
        @pl.when(pl.program_id(1) == 0)
        def _():
            o_ref[...] = jnp.zeros_like(o_ref)

        o_ref[...] += _dot_tn(a_ref[...], b_ref[...])

    return pl.pallas_call(
        body, name=name, grid=(fa // tf, n // tk),
        in_specs=[pl.BlockSpec((tk, tf), lambda i, k: (k, i)), pl.BlockSpec((tk, db), lambda i, k: (k, 0)),
                  pl.BlockSpec(memory_space=pl.ANY)],
        out_specs=pl.BlockSpec((tf, db), lambda i, k: (i, 0)),
        out_shape=jax.ShapeDtypeStruct((fa, db), F32),
        compiler_params=_params("arbitrary", "arbitrary"),
    )(a, b, behind)


def _loss_head(y, target, name):
    n = y.shape[0]
    tm = _tiles(n)["rows"]

    def body(y_ref, t_ref, d_ref, acc_ref):
        @pl.when(pl.program_id(0) == 0)
        def _():
            acc_ref[...] = jnp.zeros_like(acc_ref)

        d = y_ref[...] - t_ref[...]
        d_ref[...] = d * (1.0 / D_MODEL)
        acc_ref[...] += _rowsum8(d * d)

    row = pl.BlockSpec((tm, D_MODEL), lambda i: (i, 0))
    return pl.pallas_call(
        body, name=name, grid=(n // tm,), in_specs=[row, row], out_specs=[row, _full((8, D_MODEL))],
        out_shape=[jax.ShapeDtypeStruct((n, D_MODEL), F32), jax.ShapeDtypeStruct((8, D_MODEL), F32)],
        compiler_params=_params("arbitrary"),
    )(y, target)


def _window_sum(v, w, rows, forward):
    s = v
    sh = 1
    while sh < w:
        s = s + pltpu.roll(s, (rows - sh) if forward else sh, 0)
        sh *= 2
    return s


def _pool_fwd(x, gain, w, scale, name):
    n = x.shape[0]
    tm = _tiles(n)["pool"]
    hb = tm // POOL_HALO
    rows = tm + POOL_HALO

    def body(x_ref, xh_ref, g_ref, w_ref, sc_ref, out_ref):
        i = pl.program_id(0)
        xt = x_ref[...]
        e = jnp.concatenate([xh_ref[...], xt], axis=0)
        xhat, _ = _rms_fwd(e)
        row = lax.broadcasted_iota(jnp.int32, (rows, 1), 0)
        hn = jnp.where((row >= POOL_HALO) | (i > 0), xhat * g_ref[...], 0.0)
        t_glob = i * tm + row - POOL_HALO
        outs = []
        for gi, win in enumerate(POOL_WINDOWS):
            ug = hn[:, gi * POOL_GROUP:(gi + 1) * POOL_GROUP]
            cnt = jnp.maximum(jnp.minimum(t_glob + 1, win), 1).astype(F32)
            pooled = (_window_sum(ug, win, rows, False) / cnt - ug)[POOL_HALO:]
            outs.append(_dot(pooled.astype(BF16), w_ref[gi]))
        out_ref[...] = xt + jnp.concatenate(outs, axis=1) * sc_ref[...]

    return pl.pallas_call(
        body, name=name, grid=(n // tm,),
        in_specs=[pl.BlockSpec((tm, D_MODEL), lambda i: (i, 0)),
                  pl.BlockSpec((POOL_HALO, D_MODEL), lambda i: (jnp.maximum(i * hb - 1, 0), 0)),
                  _full((1, D_MODEL)), _full((4, POOL_GROUP, POOL_GROUP)), _full((1, D_MODEL))],
        out_specs=pl.BlockSpec((tm, D_MODEL), lambda i: (i, 0)),
        out_shape=jax.ShapeDtypeStruct((n, D_MODEL), F32),
        compiler_params=_params("arbitrary"),
    )(x, x, gain, w, scale)


def _pool_bwd(x, dout, gain, w, scale, name):
    n = x.shape[0]
    tm = _tiles(n)["pool"]
    hb = tm // POOL_HALO
    rows = tm + POOL_HALO
    nt = n // tm

    def body(x_ref, xh_ref, do_ref, doh_ref, g_ref, w_ref, sc_ref, dx_ref, dw_ref, dsc_ref, dg_ref):
        i = pl.program_id(0)

        @pl.when(i == 0)
        def _():
            dw_ref[...] = jnp.zeros_like(dw_ref)
            dsc_ref[...] = jnp.zeros_like(dsc_ref)
            dg_ref[...] = jnp.zeros_like(dg_ref)

        xt = x_ref[...]
        g = g_ref[...]
        e = jnp.concatenate([xh_ref[...], xt], axis=0)
        xhat_e, r_e = _rms_fwd(e)
        row = lax.broadcasted_iota(jnp.int32, (rows, 1), 0)
        hn = jnp.where((row >= POOL_HALO) | (i > 0), xhat_e * g, 0.0)
        t_prev = i * tm + row - POOL_HALO
        t_next = i * tm + row
        dout = do_ref[...]
        dt = jnp.concatenate([dout, doh_ref[...]], axis=0)
        dt = jnp.where((row < tm) | (i < nt - 1), dt, 0.0)
        dyr = dt * sc_ref[...]
        dus, dscs = [], []
        for gi, win in enumerate(POOL_WINDOWS):
            lanes = slice(gi * POOL_GROUP, (gi + 1) * POOL_GROUP)
            ug = hn[:, lanes]
            cnt = jnp.maximum(jnp.minimum(t_prev + 1, win), 1).astype(F32)
            pooled = (_window_sum(ug, win, rows, False) / cnt - ug)[POOL_HALO:].astype(BF16)
            yraw = _dot(pooled, w_ref[gi])
            dscs.append(_rowsum8(dout[:, lanes] * yraw))
            dyr_b = dyr[:, lanes].astype(BF16)
            dw_ref[gi] += _dot_tn(pooled, dyr_b[:tm])
            dpool = _dot_nt(dyr_b, w_ref[gi])
            cnt2 = jnp.minimum(t_next + 1, win).astype(F32)
            dus.append((_window_sum(dpool / cnt2, win, rows, True) - dpool)[:tm])
        dsc_ref[...] += jnp.concatenate(dscs, axis=1)
        dh = jnp.concatenate(dus, axis=1)
        xhat = xhat_e[POOL_HALO:]
        dg_ref[...] += _rowsum8(dh * xhat)
        dx_ref[...] = dout + _rms_bwd(dh, xhat, r_e[POOL_HALO:], g)

    row_d = pl.BlockSpec((tm, D_MODEL), lambda i: (i, 0))
    prev_h = pl.BlockSpec((POOL_HALO, D_MODEL), lambda i: (jnp.maximum(i * hb - 1, 0), 0))
    next_h = pl.BlockSpec((POOL_HALO, D_MODEL), lambda i: (jnp.minimum((i + 1) * hb, n // POOL_HALO - 1), 0))
    return pl.pallas_call(
        body, name=name, grid=(nt,),
        in_specs=[row_d, prev_h, row_d, next_h, _full((1, D_MODEL)), _full((4, POOL_GROUP, POOL_GROUP)),
                  _full((1, D_MODEL))],
        out_specs=[row_d, _full((4, POOL_GROUP, POOL_GROUP)), _full((8, D_MODEL)), _full((8, D_MODEL))],
        out_shape=[jax.ShapeDtypeStruct((n, D_MODEL), F32), jax.ShapeDtypeStruct((4, POOL_GROUP, POOL_GROUP), F32),
                   jax.ShapeDtypeStruct((8, D_MODEL), F32), jax.ShapeDtypeStruct((8, D_MODEL), F32)],
        compiler_params=_params("arbitrary"),
    )(x, x, dout, dout, gain, w, scale)


def _rope(v, cos, sin_signed):
    lo, hi = v[:, :128], v[:, 128:]
    lane = lax.broadcasted_iota(jnp.int32, hi.shape, 1)
    swapped = jnp.where(lane < 32, pltpu.roll(hi, 96, 1), pltpu.roll(hi, 32, 1))
    return jnp.concatenate([lo, hi * cos + swapped * sin_signed], axis=1)


def _rope_bwd(gr, cos, sin_signed):
    lo, hi = gr[:, :128], gr[:, 128:]
    t = hi * sin_signed
    lane = lax.broadcasted_iota(jnp.int32, hi.shape, 1)
    swapped = jnp.where(lane < 32, pltpu.roll(t, 96, 1), pltpu.roll(t, 32, 1))
    return jnp.concatenate([lo, hi * cos + swapped], axis=1)


def _mla_latents(h, win_ref):
    cq = _dot_nt(h, win_ref[0:Q_LORA, :])
    ckv = _dot_nt(h, win_ref[Q_LORA:Q_LORA + KV_LORA, :])
    kpe = _dot_nt(h, win_ref[Q_LORA + KV_LORA:LAT_PAD, :])
    return cq, ckv, kpe


def _mla_pre_fwd(x, gain, win, gq, gkv, wq, wkv, ghq, ghk, cos, sin_signed, name):
    n = x.shape[0]
    tm = _tiles(n)["fwd_k"]

    def body(x_ref, g_ref, win_ref, gq_ref, gkv_ref, wq_ref, wkv_ref, ghq_ref, ghk_ref, c_ref, s_ref,
             q_ref, k_ref, v_ref, vt_ref):
        xhat, _ = _rms_fwd(x_ref[...])
        h = (xhat * g_ref[...]).astype(BF16)
        cq, ckv, kpe = _mla_latents(h, win_ref)
        cqn = (_rms_fwd(cq)[0] * gq_ref[...]).astype(BF16)
        ckvn = (_rms_fwd(ckv)[0] * gkv_ref[...]).astype(BF16)
        cos, sn = c_ref[...], s_ref[...]
        for hd in range(N_HEADS):
            rws = pl.ds(hd * HEAD_PAD, HEAD_PAD)
            qh = _dot_nt(cqn, wq_ref[rws, :])
            qn = _rms_fwd(qh, QK_HEAD)[0] * ghq_ref[...]
            q_ref[hd] = (_rope(qn, cos, sn) * LOGIT_SCALE).astype(BF16)
            kvh = _dot_nt(ckvn, wkv_ref[rws, :])
            kpre = jnp.concatenate([kvh[:, :QK_NOPE], kpe], axis=1)
            kn = _rms_fwd(kpre, QK_HEAD)[0] * ghk_ref[...]
            k_ref[hd] = _rope(kn, cos, sn).astype(BF16)
            vh = kvh[:, QK_NOPE:]
            v_ref[hd] = vh.astype(BF16)
            vt_ref[hd, 0] = jnp.concatenate([vh.T, jnp.ones((VT_ROWS - V_HEAD, tm), F32)], axis=0).astype(BF16)

    row = lambda w: pl.BlockSpec((tm, w), lambda i: (i, 0))
    head = lambda w: pl.BlockSpec((N_HEADS, tm, w), lambda i: (0, i, 0))
    return pl.pallas_call(
        body, name=name, grid=(n // tm,),
        in_specs=[row(D_MODEL), _full((1, D_MODEL)), _full((LAT_PAD, D_MODEL)), _full((1, Q_LORA)),
                  _full((1, KV_LORA)), _full((N_HEADS * HEAD_PAD, Q_LORA)), _full((N_HEADS * HEAD_PAD, KV_LORA)),
                  _full((1, HEAD_PAD)), _full((1, HEAD_PAD)), row(128), row(128)],
        out_specs=[head(HEAD_PAD), head(HEAD_PAD), head(V_HEAD),
                   pl.BlockSpec((N_HEADS, 1, VT_ROWS, tm), lambda i: (0, i, 0, 0))],
        out_shape=[jax.ShapeDtypeStruct((N_HEADS, n, HEAD_PAD), BF16), jax.ShapeDtypeStruct((N_HEADS, n, HEAD_PAD), BF16),
                   jax.ShapeDtypeStruct((N_HEADS, n, V_HEAD), BF16),
                   jax.ShapeDtypeStruct((N_HEADS, n // tm, VT_ROWS, tm), BF16)],
        compiler_params=_params("arbitrary"),
    )(x, gain, win, gq, gkv, wq, wkv, ghq, ghk, cos, sin_signed)


def _flash_fwd(q, k, vt, name):
    n = q.shape[1]
    tk = _tiles(n)["fwd_k"]
    tq = 2 * tk
    nq = n // tq

    def body(q_ref, k_ref, vt_ref, o_ref, lse_ref, s_scr, m_scr, acc_scr):
        i = pl.program_id(1)
        qi = q_ref[0]

        def scores(j, slot):
            s_scr[slot] = _dot_nt(k_ref[0, pl.ds(pl.multiple_of(j * tk, tk), tk), :], qi)

        def update(j, slot, diagonal=None):
            s = s_scr[slot]
            if diagonal is not None:
                krow = lax.broadcasted_iota(jnp.int32, (tk, tq), 0) + diagonal * tk
                qcol = lax.broadcasted_iota(jnp.int32, (tk, tq), 1)
                s = jnp.where(krow <= qcol, s, -jnp.inf)
            m = m_scr[...]
            m_new = jnp.maximum(m, jnp.max(s, axis=0, keepdims=True))
            p = jnp.exp2(s - m_new).astype(BF16)
            acc_scr[...] = jnp.exp2(m - m_new) * acc_scr[...] + _dot(vt_ref[0, j], p)
            m_scr[...] = m_new

        m_scr[...] = jnp.full((1, tq), -jnp.inf, F32)
        acc_scr[...] = jnp.zeros((VT_ROWS, tq), F32)
        scores(0, 0)

        def pair(jj, carry):
            scores(2 * jj + 1, 1)
            update(2 * jj, 0)
            scores(2 * jj + 2, 0)
            update(2 * jj + 1, 1)
            return carry

        lax.fori_loop(0, i, pair, 0)
        scores(2 * i + 1, 1)
        update(2 * i, 0, diagonal=0)
        update(2 * i + 1, 1, diagonal=1)
        l = acc_scr[V_HEAD:V_HEAD + 1, :]
        o_ref[...] = (acc_scr[0:V_HEAD, :] / l).T.astype(BF16)
        lse_ref[0, 0] = m_scr[...] + jnp.log2(l)

    return pl.pallas_call(
        body, name=name, grid=(N_HEADS, nq),
        in_specs=[pl.BlockSpec((1, tq, HEAD_PAD), lambda h, i: (h, i, 0)),
                  pl.BlockSpec((1, n, HEAD_PAD), lambda h, i: (h, 0, 0)),
                  pl.BlockSpec((1, n // tk, VT_ROWS, tk), lambda h, i: (h, 0, 0, 0))],
        out_specs=[pl.BlockSpec((tq, V_HEAD), lambda h, i: (i, h)),
                   pl.BlockSpec((1, 1, 1, tq), lambda h, i: (h, i, 0, 0))],
        out_shape=[jax.ShapeDtypeStruct((n, N_HEADS * V_HEAD), BF16), jax.ShapeDtypeStruct((N_HEADS, nq, 1, tq), F32)],
        scratch_shapes=[pltpu.VMEM((2, tk, tq), F32), pltpu.VMEM((1, tq), F32), pltpu.VMEM((VT_ROWS, tq), F32)],
        compiler_params=_params("arbitrary", "arbitrary"),
    )(q, k, vt)


def _mla_post_fwd(o, x, wout, name):
    n = x.shape[0]
    tm = _tiles(n)["rows"]

    def body(o_ref, x_ref, w_ref, out_ref):
        out_ref[...] = x_ref[...] + _dot(o_ref[...], w_ref[...])

    row = pl.BlockSpec((tm, D_MODEL), lambda i: (i, 0))
    return pl.pallas_call(
        body, name=name, grid=(n // tm,), in_specs=[row, row, _full((D_MODEL, D_MODEL))], out_specs=row,
        out_shape=jax.ShapeDtypeStruct((n, D_MODEL), F32), compiler_params=_params("arbitrary"),
    )(o, x, wout)


def _mla_out_bwd(dout, o, wout, name):
    n = dout.shape[0]
    t = _tiles(n)["bwd_q"]
    nq = n // t

    def body(do_ref, o_ref, w_ref, dob_ref, dpo_ref, dl_ref):
        dob = do_ref[...].astype(BF16)
        dob_ref[...] = dob
        dpo = _dot_nt(dob, w_ref[...])
        dpo_ref[...] = dpo.astype(BF16)
        ov = o_ref[...].astype(F32)
        for hd in range(N_HEADS):
            lanes = slice(hd * V_HEAD, (hd + 1) * V_HEAD)
            prod = dpo[:, lanes] * ov[:, lanes]
            dl_ref[hd, 0] = jnp.sum(prod.T, axis=0, keepdims=True)

    row = pl.BlockSpec((t, D_MODEL), lambda i: (i, 0))
    return pl.pallas_call(
        body, name=name, grid=(nq,), in_specs=[row, row, _full((D_MODEL, D_MODEL))],
        out_specs=[row, row, pl.BlockSpec((N_HEADS, 1, 1, t), lambda i: (0, i, 0, 0))],
        out_shape=[jax.ShapeDtypeStruct((n, D_MODEL), BF16), jax.ShapeDtypeStruct((n, D_MODEL), BF16),
                   jax.ShapeDtypeStruct((N_HEADS, nq, 1, t), F32)],
        compiler_params=_params("arbitrary"),
    )(dout, o, wout)


def _flash_bwd(q, k, v, dpo, lse, delta, name):
    n = q.shape[1]
    tq = _tiles(n)["bwd_q"]
    tk = 2 * tq
    nk = n // tk
    nqb = n // tq

    def body(k_ref, v_ref, q_ref, do_ref, lse_ref, dl_ref, dk_ref, dv_ref, dq_hbm,
             dq_acc, s_scr, dp_scr, dk_acc, dv_acc, sem):
        h = pl.program_id(0)
        j = pl.program_id(1)

        @pl.when(j == 0)
        def _():
            dq_acc[...] = jnp.zeros_like(dq_acc)

        dk_acc[...] = jnp.zeros_like(dk_acc)
        dv_acc[...] = jnp.zeros_like(dv_acc)
        kj = k_ref[0]
        vj = v_ref[0]
        npairs = nk - 1 - j

        def block(t):
            return jnp.where(t < 2 * npairs, 2 * j + 2 + t, 2 * j + (t - 2 * npairs))

        def scores(i, slot):
            rws = pl.ds(pl.multiple_of(i * tq, tq), tq)
            s_scr[slot] = _dot_nt(kj, q_ref[0, rws, :])
            dp_scr[slot] = _dot_nt(vj, do_ref[rws, :])

        def update(i, slot, diagonal=None):
            rws = pl.ds(pl.multiple_of(i * tq, tq), tq)
            p = jnp.exp2(s_scr[slot] - lse_ref[0, i])
            if diagonal is not None:
                krow = lax.broadcasted_iota(jnp.int32, (tk, tq), 0)
                qcol = lax.broadcasted_iota(jnp.int32, (tk, tq), 1) + diagonal * tq
                p = jnp.where(krow <= qcol, p, 0.0)
            dv_acc[...] += _dot(p.astype(BF16), do_ref[rws, :])
            ds = (p * (dp_scr[slot] - dl_ref[0, i])).astype(BF16)
            dk_acc[...] += _dot(ds, q_ref[0, rws, :])
            dq_acc[rws, :] += _dot_tn(ds, kj)

        scores(block(0), 0)

        def pair(jj, carry):
            scores(block(2 * jj + 1), 1)
            update(block(2 * jj), 0)
            scores(block(2 * jj + 2), 0)
            update(block(2 * jj + 1), 1)
            return carry

        lax.fori_loop(0, npairs, pair, 0)
        scores(2 * j + 1, 1)
        update(2 * j, 0, diagonal=0)
        update(2 * j + 1, 1, diagonal=1)
        dk_ref[0] = dk_acc[...] * (ATTN_SCALE / LOGIT_SCALE)
        dv_ref[0] = dv_acc[...]

        @pl.when(j == nk - 1)
        def _():
            dq_acc[...] = dq_acc[...] * ATTN_SCALE
            cp = pltpu.make_async_copy(dq_acc, dq_hbm.at[h], sem)
            cp.start()
            cp.wait()

    resident = dict(pipeline_mode=pl.Buffered(1))
    return pl.pallas_call(
        body, name=name, grid=(N_HEADS, nk),
        in_specs=[pl.BlockSpec((1, tk, HEAD_PAD), lambda h, j: (h, j, 0)),
                  pl.BlockSpec((1, tk, V_HEAD), lambda h, j: (h, j, 0)),
                  pl.BlockSpec((1, n, HEAD_PAD), lambda h, j: (h, 0, 0), **resident),
                  pl.BlockSpec((n, V_HEAD), lambda h, j: (0, h), **resident),
                  pl.BlockSpec((1, nqb, 1, tq), lambda h, j: (h, 0, 0, 0)),
                  pl.BlockSpec((1, nqb, 1, tq), lambda h, j: (h, 0, 0, 0))],
        out_specs=[pl.BlockSpec((1, tk, HEAD_PAD), lambda h, j: (h, j, 0)),
                   pl.BlockSpec((1, tk, V_HEAD), lambda h, j: (h, j, 0)),
                   pl.BlockSpec(memory_space=pl.ANY)],
        out_shape=[jax.ShapeDtypeStruct((N_HEADS, n, HEAD_PAD), F32), jax.ShapeDtypeStruct((N_HEADS, n, V_HEAD), F32),
                   jax.ShapeDtypeStruct((N_HEADS, n, HEAD_PAD), F32)],
        scratch_shapes=[pltpu.VMEM((n, HEAD_PAD), F32), pltpu.VMEM((2, tk, tq), F32), pltpu.VMEM((2, tk, tq), F32),
                        pltpu.VMEM((tk, HEAD_PAD), F32), pltpu.VMEM((tk, V_HEAD), F32), pltpu.SemaphoreType.DMA(())],
        compiler_params=_params("arbitrary", "arbitrary"),
    )(k, v, q, dpo, lse, delta)


def _mla_pre_bwd(x, dout, dq, dk, dv, gain, win, gq, gkv, wq, wkv, ghq, ghk, cos, sin_signed, name):
    n = x.shape[0]
    tm = _tiles(n)["mla_bwd"]
    hw = N_HEADS * HEAD_PAD

    def body(x_ref, do_ref, dq_ref, dk_ref, dv_ref, g_ref, win_ref, gq_ref, gkv_ref, wq_ref, wkv_ref, ghq_ref, ghk_ref,
             c_ref, s_ref, dx_ref, h_ref, dlat_ref, cqn_ref, dqp_ref, ckvn_ref, dkv_ref,
             dg_ref, dgq_ref, dgkv_ref, dghq_ref, dghk_ref):
        @pl.when(pl.program_id(0) == 0)
        def _():
            for ref in (dg_ref, dgq_ref, dgkv_ref, dghq_ref, dghk_ref):
                ref[...] = jnp.zeros_like(ref)

        g = g_ref[...]
        xhat, r = _rms_fwd(x_ref[...])
        h = (xhat * g).astype(BF16)
        h_ref[...] = h
        cq, ckv, kpe = _mla_latents(h, win_ref)
        cqhat, rcq = _rms_fwd(cq)
        ckvhat, rckv = _rms_fwd(ckv)
        cqn = (cqhat * gq_ref[...]).astype(BF16)
        ckvn = (ckvhat * gkv_ref[...]).astype(BF16)
        cqn_ref[...] = cqn
        ckvn_ref[...] = ckvn
        cos, sn = c_ref[...], s_ref[...]
        ghq, ghk = ghq_ref[...], ghk_ref[...]
        dkpe = jnp.zeros((tm, 128), F32)
        dghq = jnp.zeros((8, HEAD_PAD), F32)
        dghk = jnp.zeros((8, HEAD_PAD), F32)
        for hd in range(N_HEADS):
            rws = pl.ds(hd * HEAD_PAD, HEAD_PAD)
            lanes = slice(hd * HEAD_PAD, (hd + 1) * HEAD_PAD)
            qhat, rq = _rms_fwd(_dot_nt(cqn, wq_ref[rws, :]), QK_HEAD)
            gqn = _rope_bwd(dq_ref[hd], cos, sn)
            dghq = dghq + _rowsum8(gqn * qhat)
            dqpre = _rms_bwd(gqn, qhat, rq, ghq, QK_HEAD).astype(BF16)
            dqp_ref[:, lanes] = dqpre
            kvh = _dot_nt(ckvn, wkv_ref[rws, :])
            khat, rk = _rms_fwd(jnp.concatenate([kvh[:, :QK_NOPE], kpe], axis=1), QK_HEAD)
            gkn = _rope_bwd(dk_ref[hd], cos, sn)
            dghk = dghk + _rowsum8(gkn * khat)
            dkpre = _rms_bwd(gkn, khat, rk, ghk, QK_HEAD)
            dkpe = dkpe + dkpre[:, QK_NOPE:]
            dkvh = jnp.concatenate([dkpre[:, :QK_NOPE], dv_ref[hd]], axis=1).astype(BF16)
            dkv_ref[:, lanes] = dkvh
        dcqn = _dot(dqp_ref[...], wq_ref[...])
        dckvn = _dot(dkv_ref[...], wkv_ref[...])
        dghq_ref[...] += dghq
        dghk_ref[...] += dghk
        dgq_ref[...] += _rowsum8(dcqn * cqhat)
        dgkv_ref[...] += _rowsum8(dckvn * ckvhat)
        dlat = jnp.concatenate([_rms_bwd(dcqn, cqhat, rcq, gq_ref[...]), _rms_bwd(dckvn, ckvhat, rckv, gkv_ref[...]),
                                dkpe], axis=1).astype(BF16)
        dlat_ref[...] = dlat
        dh = _dot(dlat, win_ref[...])
        dg_ref[...] += _rowsum8(dh * xhat)
        dx_ref[...] = do_ref[...] + _rms_bwd(dh, xhat, r, g)

    row = lambda w: pl.BlockSpec((tm, w), lambda i: (i, 0))
    head = lambda w: pl.BlockSpec((N_HEADS, tm, w), lambda i: (0, i, 0))
    sds = jax.ShapeDtypeStruct
    return pl.pallas_call(
        body, name=name, grid=(n // tm,),
        in_specs=[row(D_MODEL), row(D_MODEL), head(HEAD_PAD), head(HEAD_PAD), head(V_HEAD), _full((1, D_MODEL)),
                  _full((LAT_PAD, D_MODEL)), _full((1, Q_LORA)), _full((1, KV_LORA)), _full((hw, Q_LORA)),
                  _full((hw, KV_LORA)), _full((1, HEAD_PAD)), _full((1, HEAD_PAD)), row(128), row(128)],
        out_specs=[row(D_MODEL), row(D_MODEL), row(LAT_PAD), row(Q_LORA), row(hw), row(KV_LORA), row(hw),
                   _full((8, D_MODEL)), _full((8, Q_LORA)), _full((8, KV_LORA)), _full((8, HEAD_PAD)),
                   _full((8, HEAD_PAD))],
        out_shape=[sds((n, D_MODEL), F32), sds((n, D_MODEL), BF16), sds((n, LAT_PAD), BF16), sds((n, Q_LORA), BF16),
                   sds((n, hw), BF16), sds((n, KV_LORA), BF16), sds((n, hw), BF16), sds((8, D_MODEL), F32),
                   sds((8, Q_LORA), F32), sds((8, KV_LORA), F32), sds((8, HEAD_PAD), F32), sds((8, HEAD_PAD), F32)],
        compiler_params=_params("arbitrary"),
    )(x, dout, dq, dk, dv, gain, win, gq, gkv, wq, wkv, ghq, ghk, cos, sin_signed)


def _adamw(w, g, m, v, behind, name):
    rows, cols = w.shape
    tr = rows
    for cand in (512, 256, 128, 64, 32, 16, 8):
        if rows % cand == 0 and rows > cand:
            tr = cand
            break

    def body(w_ref, g_ref, m_ref, v_ref, behind_ref, d_ref, mo_ref, vo_ref):
        gv = g_ref[...]
        mn = ADAM_B1 * m_ref[...] + (1.0 - ADAM_B1) * gv
        vn = ADAM_B2 * v_ref[...] + (1.0 - ADAM_B2) * (gv * gv)
        m_hat = mn / (1.0 - ADAM_B1 ** ADAM_STEP)
        v_hat = vn / (1.0 - ADAM_B2 ** ADAM_STEP)
        d_ref[...] = -ADAM_LR * (m_hat / (jnp.sqrt(v_hat) + ADAM_EPS) + ADAM_WD * w_ref[...])
        mo_ref[...] = mn
        vo_ref[...] = vn

    blk = pl.BlockSpec((tr, cols), lambda i: (i, 0))
    return pl.pallas_call(
        body, name=name, grid=(rows // tr,), in_specs=[blk] * 4 + [pl.BlockSpec(memory_space=pl.ANY)],
        out_specs=[blk] * 3, out_shape=[jax.ShapeDtypeStruct((rows, cols), F32)] * 3,
        compiler_params=_params("arbitrary"),
    )(w, g, m, v, behind)


def _sum_parts(parts, name):
    k, r, c = parts.shape
    tr = min(r, 512)

    def body(p_ref, o_ref):
        acc = p_ref[0]
        for j in range(1, k):
            acc = acc + p_ref[j]
        o_ref[...] = acc

    return pl.pallas_call(
        body, name=name, grid=(r // tr,), in_specs=[pl.BlockSpec((k, tr, c), lambda i: (0, i, 0))],
        out_specs=pl.BlockSpec((tr, c), lambda i: (i, 0)), out_shape=jax.ShapeDtypeStruct((r, c), parts.dtype),
        compiler_params=_params("arbitrary"),
    )(parts)


def _row_tile(r, most=256):
    best = r
    for cand in range(8, most + 1, 8):
        if r % cand == 0:
            best = cand
    return best


def _sum_exchange(mine, landed, me, name):
    _, r, c = mine.shape
    tr = _row_tile(r)

    def body(me_ref, m_ref, l_ref, o_ref):
        acc = m_ref[0]
        for k in range(1, N_DEV):
            acc = acc + l_ref[k]
        o_ref[...] = acc

    return pl.pallas_call(
        body, name=name,
        grid_spec=pltpu.PrefetchScalarGridSpec(
            num_scalar_prefetch=1, grid=(r // tr,),
            in_specs=[pl.BlockSpec((1, tr, c), lambda i, me_ref: (me_ref[0], i, 0)),
                      pl.BlockSpec((N_DEV, tr, c), lambda i, me_ref: (0, i, 0))],
            out_specs=pl.BlockSpec((tr, c), lambda i, me_ref: (i, 0))),
        out_shape=jax.ShapeDtypeStruct((r, c), mine.dtype), compiler_params=_params("arbitrary"),
    )(me, mine, landed)


MESH = pl.DeviceIdType.MESH


def _all_gather(x, name):
    r, c = x.shape

    def body(x_ref, out_ref, send_sems, recv_sems, local_sem):
        mx, my, mc = lax.axis_index("x"), lax.axis_index("y"), lax.axis_index("c")
        me, sibling = (mx, my, mc), (mx, my, 1 - mc)
        chips = [(1 - mx, my), (mx, 1 - my), (1 - mx, 1 - my)]

        def slot(px, py, pc):
            return out_ref.at[4 * px + 2 * py + pc]

        def copy(k, block, to, src=None):
            return pltpu.make_async_remote_copy(
                src_ref=slot(*block) if src is None else src, dst_ref=slot(*block),
                send_sem=send_sems.at[k], recv_sem=recv_sems.at[k], device_id=to, device_id_type=MESH)

        mine = pltpu.make_async_copy(x_ref, slot(*me), local_sem)
        mine.start()
        first = [copy(0, me, sibling, src=x_ref)]
        first += [copy(1 + j, me, (*chip, mc), src=x_ref) for j, chip in enumerate(chips)]
        for cp in first:
            cp.start()
        passed = [copy(4 + j, (*chip, mc), sibling) for j, chip in enumerate(chips)]
        for j, chip in enumerate(chips):
            copy(1 + j, (*chip, mc), me).wait_recv()
            passed[j].start()
        copy(0, sibling, me).wait_recv()
        for j, chip in enumerate(chips):
            copy(4 + j, (*chip, 1 - mc), me).wait_recv()
        for cp in first + passed:
            cp.wait_send()
        mine.wait()

    any_spec = pl.BlockSpec(memory_space=pl.ANY)
    return pl.pallas_call(
        body, name=name, in_specs=[any_spec], out_specs=any_spec,
        out_shape=jax.ShapeDtypeStruct((N_DEV, r, c), x.dtype),
        scratch_shapes=[pltpu.SemaphoreType.DMA((7,)), pltpu.SemaphoreType.DMA((7,)), pltpu.SemaphoreType.DMA(())],
    )(x)


HBM_SPEC = pl.BlockSpec(memory_space=pltpu.HBM)
SEM_SPEC = pl.BlockSpec(memory_space=pltpu.SEMAPHORE)
SPLIT_EFFECT = pltpu.SideEffectType.DATAFLOW_SIDE_EFFECTING


def _exchange_copies(src_ref, land_ref, send_sems, recv_sems, gather):
    mx, my, mc = lax.axis_index("x"), lax.axis_index("y"), lax.axis_index("c")
    me = 4 * mx + 2 * my + mc
    copies = []
    for k in range(1, N_DEV):
        px = 1 - mx if k & 4 else mx
        py = 1 - my if k & 2 else my
        pc = 1 - mc if k & 1 else mc
        src = src_ref if gather else src_ref.at[4 * px + 2 * py + pc]
        dst = land_ref.at[me] if gather else land_ref.at[k]
        copies.append(pltpu.make_async_remote_copy(
            src_ref=src, dst_ref=dst, send_sem=send_sems.at[k - 1], recv_sem=recv_sems.at[k - 1],
            device_id=(px, py, pc), device_id_type=MESH))
    return copies


def _exchange_start(src, after, gather, name):
    land_shape = (N_DEV,) + src.shape[-2:]

    def body(src_ref, land_ref, after_ref, send_sems, recv_sems, src_thru, land_thru, token):
        for cp in _exchange_copies(src_ref, land_ref, send_sems, recv_sems, gather):
            cp.start()
        token[...] = jnp.zeros_like(token)

    return pl.pallas_call(
        body, name=name,
        out_shape=(pltpu.SemaphoreType.DMA((N_DEV - 1,)), pltpu.SemaphoreType.DMA((N_DEV - 1,)),
                   pltpu.HBM(src.shape, src.dtype), pltpu.HBM(land_shape, src.dtype),
                   jax.ShapeDtypeStruct((8, 128), F32)),
        in_specs=(HBM_SPEC, HBM_SPEC, pl.BlockSpec(memory_space=pl.ANY)),
        out_specs=(SEM_SPEC, SEM_SPEC, HBM_SPEC, HBM_SPEC, pl.BlockSpec(memory_space=pltpu.VMEM)),
        input_output_aliases={0: 2, 1: 3},
        compiler_params=pltpu.CompilerParams(has_side_effects=SPLIT_EFFECT),
    )(pltpu.with_memory_space_constraint(src, pltpu.HBM),
      pltpu.with_memory_space_constraint(lax.empty(land_shape, src.dtype), pltpu.HBM), after)


def _exchange_wait(started, after, gather, name):
    send_sems, recv_sems, src_thru, land_thru, _ = started

    def body(src_ref, land_ref, send_sems, recv_sems, after_ref, src_out, land_out):
        for cp in _exchange_copies(src_ref, land_ref, send_sems, recv_sems, gather):
            cp.wait_send()
            cp.wait_recv()

    return pl.pallas_call(
        body, name=name,
        out_shape=(pltpu.HBM(src_thru.shape, src_thru.dtype), pltpu.HBM(land_thru.shape, land_thru.dtype)),
        in_specs=(HBM_SPEC, HBM_SPEC, SEM_SPEC, SEM_SPEC, pl.BlockSpec(memory_space=pl.ANY)),
        out_specs=(HBM_SPEC, HBM_SPEC), input_output_aliases={0: 0, 1: 1},
        compiler_params=pltpu.CompilerParams(has_side_effects=SPLIT_EFFECT),
    )(src_thru, land_thru, send_sems, recv_sems, after)


FFN_ROWS = D_FF // N_DEV
WIN_ROWS = (Q_LORA + KV_LORA + QK_ROPE) // N_DEV
WIN_ROWS_PAD = 144
WQ_ROWS = QK_HEAD * Q_LORA // D_MODEL
WKV_ROWS = 256 * KV_LORA // D_MODEL
WOUT_ROWS = V_HEAD
POOL_ROWS = 4 * 32 * POOL_GROUP // D_MODEL


def _t(w):
    return jnp.swapaxes(w, -1, -2)


def _ffn_segments(f, i):
    return [(f + "_w_gate", i, FFN_ROWS), (f + "_w_up", i, FFN_ROWS), (f + "_w_down", i, FFN_ROWS)]


def _mixer_segments(i):
    j = i // 2
    if i % 2 == 0:
        return [("pool_w", j, POOL_ROWS)]
    return [("mla_w_in", j, WIN_ROWS_PAD), ("mla_w_q_up", j, WQ_ROWS), ("mla_w_kv_up", j, WKV_ROWS),
            ("mla_w_out", j, WOUT_ROWS)]


def _pack_shards(p, segs, dtype):
    parts = []
    for name, idx, _ in segs:
        w = p[name][idx]
        if name.endswith("w_gate") or name.endswith("w_up"):
            w = _t(w)
        elif name == "mla_w_in":
            w = jnp.pad(_t(w), ((0, WIN_ROWS_PAD - WIN_ROWS), (0, 0)))
        elif name == "mla_w_q_up":
            w = _t(w).reshape(WQ_ROWS, D_MODEL)
        elif name == "mla_w_kv_up":
            w = _t(w).reshape(WKV_ROWS, D_MODEL)
        elif name == "pool_w":
            w = w.reshape(POOL_ROWS, D_MODEL)
        parts.append(w.astype(dtype))
    return jnp.concatenate(parts, axis=0)


def _unpack_gathered(g, segs):
    out = {}
    off = 0
    for name, layer, rows in segs:
        seg = g[:, off:off + rows, :]
        off += rows
        if name == "mla_w_in":
            w = seg[:, :WIN_ROWS].reshape(N_DEV * WIN_ROWS, D_MODEL)
            w = jnp.pad(w, ((0, LAT_PAD - N_DEV * WIN_ROWS), (0, 0)))
        elif name == "mla_w_q_up":
            w = seg.reshape(N_HEADS, QK_HEAD, Q_LORA)
            w = jnp.pad(w, ((0, 0), (0, HEAD_PAD - QK_HEAD), (0, 0))).reshape(N_HEADS * HEAD_PAD, Q_LORA)
        elif name == "mla_w_kv_up":
            w = seg.reshape(N_HEADS * 256, KV_LORA)
        elif name == "pool_w":
            w = seg.reshape(N_DEV, 4, 32, POOL_GROUP).transpose(1, 0, 2, 3).reshape(4, POOL_GROUP, POOL_GROUP)
        else:
            w = seg.reshape(N_DEV * rows, D_MODEL)
        out[(name, layer)] = w
    return out


def _pack_grads(gr, segments):
    segs = []
    for name, layer, rows in segments:
        g = gr[(name, layer)]
        if name == "mla_w_in":
            g = g[:N_DEV * WIN_ROWS].reshape(N_DEV, WIN_ROWS, D_MODEL)
            g = jnp.pad(g, ((0, 0), (0, WIN_ROWS_PAD - WIN_ROWS), (0, 0)))
        elif name == "mla_w_q_up":
            g = g.reshape(N_HEADS, HEAD_PAD, Q_LORA)[:, :QK_HEAD].reshape(N_DEV, WQ_ROWS, D_MODEL)
        elif name == "mla_w_kv_up":
            g = g.reshape(N_DEV, WKV_ROWS, D_MODEL)
        elif name == "pool_w":
            g = g.reshape(4, N_DEV, 32, POOL_GROUP).transpose(1, 0, 2, 3).reshape(N_DEV, POOL_ROWS, D_MODEL)
        else:
            g = g.reshape(N_DEV, rows, D_MODEL)
        segs.append(g)
    return jnp.concatenate(segs, axis=1)


def _unpack_shard_grads(flat, segments):
    per = {}
    off = 0
    for name, layer, rows in segments:
        seg = flat[off:off + rows]
        off += rows
        if name.endswith("w_gate") or name.endswith("w_up"):
            g = _t(seg)
        elif name == "mla_w_in":
            g = _t(seg[:WIN_ROWS])
        elif name == "mla_w_q_up":
            g = _t(seg.reshape(QK_HEAD, Q_LORA))
        elif name == "mla_w_kv_up":
            g = _t(seg.reshape(256, KV_LORA))
        elif name == "pool_w":
            g = seg.reshape(4, 32, POOL_GROUP)
        else:
            g = seg
        per[(name, layer)] = g
    return per


def _pad_lanes(v, width):
    return jnp.pad(v, ((0, 0), (0, width - v.shape[-1])))


def kernel(x, positions, ffn1_norm, ffn1_w_gate, ffn1_w_up, ffn1_w_down, mix_norm, pool_w, pool_scale, mla_w_in, mla_q_norm, mla_w_q_up, mla_kv_norm, mla_w_kv_up, mla_q_head_norm, mla_k_head_norm, mla_w_out, ffn2_norm, ffn2_w_gate, ffn2_w_up, ffn2_w_down, loss_target, m_ffn1_norm, m_ffn1_w_gate, m_ffn1_w_up, m_ffn1_w_down, m_mix_norm, m_pool_w, m_pool_scale, m_mla_w_in, m_mla_q_norm, m_mla_w_q_up, m_mla_kv_norm, m_mla_w_kv_up, m_mla_q_head_norm, m_mla_k_head_norm, m_mla_w_out, m_ffn2_norm, m_ffn2_w_gate, m_ffn2_w_up, m_ffn2_w_down, v_ffn1_norm, v_ffn1_w_gate, v_ffn1_w_up, v_ffn1_w_down, v_mix_norm, v_pool_w, v_pool_scale, v_mla_w_in, v_mla_q_norm, v_mla_w_q_up, v_mla_kv_norm, v_mla_w_kv_up, v_mla_q_head_norm, v_mla_k_head_norm, v_mla_w_out, v_ffn2_norm, v_ffn2_w_gate, v_ffn2_w_up, v_ffn2_w_down):
    args = (x, positions, ffn1_norm, ffn1_w_gate, ffn1_w_up, ffn1_w_down, mix_norm, pool_w, pool_scale, mla_w_in,
            mla_q_norm, mla_w_q_up, mla_kv_norm, mla_w_kv_up, mla_q_head_norm, mla_k_head_norm, mla_w_out, ffn2_norm,
            ffn2_w_gate, ffn2_w_up, ffn2_w_down)
    p = dict(zip(NAMES, args))
    moments_m = dict(zip(WEIGHTS, (m_ffn1_norm, m_ffn1_w_gate, m_ffn1_w_up, m_ffn1_w_down, m_mix_norm, m_pool_w, m_pool_scale, m_mla_w_in, m_mla_q_norm, m_mla_w_q_up, m_mla_kv_norm, m_mla_w_kv_up, m_mla_q_head_norm, m_mla_k_head_norm, m_mla_w_out, m_ffn2_norm, m_ffn2_w_gate, m_ffn2_w_up, m_ffn2_w_down)))
    moments_v = dict(zip(WEIGHTS, (v_ffn1_norm, v_ffn1_w_gate, v_ffn1_w_up, v_ffn1_w_down, v_mix_norm, v_pool_w, v_pool_scale, v_mla_w_in, v_mla_q_norm, v_mla_w_q_up, v_mla_kv_norm, v_mla_w_kv_up, v_mla_q_head_norm, v_mla_k_head_norm, v_mla_w_out, v_ffn2_norm, v_ffn2_w_gate, v_ffn2_w_up, v_ffn2_w_down)))
    dev = 4 * lax.axis_index("x") + 2 * lax.axis_index("y") + lax.axis_index("c")

    xs = x[0]
    n = xs.shape[0]
    target = loss_target[0]

    inv_freq = 1.0 / (ROPE_THETA ** (jnp.arange(0, QK_ROPE, 2, dtype=F32) / QK_ROPE))
    ang = positions[0].astype(F32)[..., None] * inv_freq
    cos, sin = jnp.cos(ang), jnp.sin(ang)
    zero = jnp.zeros((n, 128 - QK_ROPE), F32)
    rope_cos = jnp.concatenate([cos, cos, zero], axis=1)
    rope_sin = jnp.concatenate([-sin, sin, zero], axis=1)

    ag_groups = [_ffn_segments("ffn1", 0), _mixer_segments(0) + _ffn_segments("ffn2", 0)]
    ag_groups += [_ffn_segments("ffn1", i) + _mixer_segments(i) + _ffn_segments("ffn2", i) for i in range(1, DEPTH)]
    shards = [_pack_shards(p, segs, BF16) for segs in ag_groups]
    w = {}
    no_token = jnp.zeros((8, 128), F32)

    def tied(gain, token):
        return gain + token[0, 0]

    gains_local = jnp.concatenate([_pad_lanes(mla_q_norm, 128), _pad_lanes(mla_kv_norm, 128)], axis=0)
    gains_all = _all_gather(jnp.pad(gains_local, ((0, 4), (0, 0))), "gains_all_gather")
    q_norm_full = gains_all[:, 0:2, :Q_LORA // N_DEV].transpose(1, 0, 2).reshape(2, Q_LORA)
    kv_norm_full = gains_all[:, 2:4, :KV_LORA // N_DEV].transpose(1, 0, 2).reshape(2, KV_LORA)
    ghq = _pad_lanes(mla_q_head_norm, HEAD_PAD)
    ghk = _pad_lanes(mla_k_head_norm, HEAD_PAD)

    def mla_weights(j):
        return (w[("mla_w_in", j)], q_norm_full[j:j + 1], kv_norm_full[j:j + 1], w[("mla_w_q_up", j)],
                w[("mla_w_kv_up", j)], ghq[j:j + 1], ghk[j:j + 1], rope_cos, rope_sin)

    saved = []
    cur = xs
    gathers = [_exchange_start(shards[0], gains_all, True, "ag_start_0")]

    def gather_step(after):
        g = len(gathers) - 1
        _, landed = _exchange_wait(gathers[g], after, True, f"ag_wait_{g}")
        w.update(_unpack_gathered(lax.dynamic_update_slice(landed, shards[g][None], (dev, 0, 0)), ag_groups[g]))
        if g + 1 == len(ag_groups):
            return no_token
        gathers.append(_exchange_start(shards[g + 1], landed, True, f"ag_start_{g + 1}"))
        return gathers[-1][4]

    for i in range(DEPTH):
        j = i // 2
        st = {"x0": cur}
        token = gather_step(cur)
        cur, st["a1"], st["u1"] = _ffn_fwd(cur, tied(ffn1_norm[i:i + 1], token), w[("ffn1_w_gate", i)],
                                           w[("ffn1_w_up", i)], w[("ffn1_w_down", i)], f"ffn1_fwd_{i}")
        st["x1"] = cur
        if i == 0:
            token = gather_step(cur)
        if i % 2 == 0:
            cur = _pool_fwd(cur, tied(mix_norm[i:i + 1], token), w[("pool_w", j)], pool_scale[j:j + 1],
                            f"pool_fwd_{i}")
        else:
            st["q"], st["k"], st["v"], vt = _mla_pre_fwd(cur, mix_norm[i:i + 1], *mla_weights(j), f"mla_pre_fwd_{i}")
            st["o"], lse = _flash_fwd(st["q"], st["k"], vt, f"flash_fwd_{i}")
            tq_bwd = _tiles(n)["bwd_q"]
            st["lse"] = lse.reshape(N_HEADS, n // tq_bwd, 1, tq_bwd)
            cur = _mla_post_fwd(st["o"], cur, w[("mla_w_out", j)], f"mla_post_fwd_{i}")
        st["x2"] = cur
        cur, st["a2"], st["u2"] = _ffn_fwd(cur, ffn2_norm[i:i + 1], w[("ffn2_w_gate", i)], w[("ffn2_w_up", i)],
                                           w[("ffn2_w_down", i)], f"ffn2_fwd_{i}")
        saved.append(st)

    dcur, sq_err = _loss_head(cur, target, "loss_head")
    loss_part = 0.5 * jnp.sum(sq_err) * (1.0 / D_MODEL)

    gr = {}
    small = {k: [None] * DEPTH for k in ("ffn1_norm", "mix_norm", "ffn2_norm")}
    small.update({k: [None] * (DEPTH // 2) for k in ("pool_scale", "mla_q_norm", "mla_kv_norm", "mla_q_head_norm",
                                                     "mla_k_head_norm")})

    me = jnp.reshape(dev, (1,)).astype(jnp.int32)
    grads = {}
    in_flight = []

    def reduce_start(segs, tag):
        started = _exchange_start(_pack_grads(gr, segs), dcur, False, f"rs_start_{tag}")
        reduce_finish(started[4])
        in_flight.append((started, segs, tag))
        return started[4]

    def reduce_finish(after):
        if in_flight:
            started, segs, tag = in_flight.pop()
            mine, landed = _exchange_wait(started, after, False, f"rs_wait_{tag}")
            grads.update(_unpack_shard_grads(_sum_exchange(mine, landed, me, f"rs_sum_{tag}"), segs))

    def ffn_backward(f, i, x_in, a, u, gain, dout):
        dx, h, dob, y, da, du, dg = _ffn_bwd(x_in, dout, a, u, gain, w[(f + "_w_gate", i)], w[(f + "_w_up", i)],
                                             w[(f + "_w_down", i)], f"{f}_bwd_{i}")
        gr[(f + "_w_gate", i)] = _tn_matmul(da, h, f"{f}_dgate_{i}")
        gr[(f + "_w_up", i)] = _tn_matmul(du, h, f"{f}_dup_{i}")
        gr[(f + "_w_down", i)] = _tn_matmul(y, dob, f"{f}_ddown_{i}")
        small[f + "_norm"][i] = jnp.sum(dg, axis=0)
        return dx

    token = jnp.zeros((8, 128), F32)
    for i in reversed(range(DEPTH)):
        j = i // 2
        st = saved[i]
        dcur = ffn_backward("ffn2", i, st["x2"], st["a2"], st["u2"], tied(ffn2_norm[i:i + 1], token), dcur)
        if i % 2 == 0:
            dcur, dpw, dsc, dg = _pool_bwd(st["x1"], dcur, mix_norm[i:i + 1], w[("pool_w", j)], pool_scale[j:j + 1],
                                           f"pool_bwd_{i}")
            gr[("pool_w", j)] = dpw
            small["pool_scale"][j] = jnp.sum(dsc, axis=0)
            small["mix_norm"][i] = jnp.sum(dg, axis=0)
        else:
            dob, dpo, delta = _mla_out_bwd(dcur, st["o"], w[("mla_w_out", j)], f"mla_out_bwd_{i}")
            gr[("mla_w_out", j)] = _tn_matmul(st["o"], dob, f"mla_dout_{i}")
            dk, dv, dq = _flash_bwd(st["q"], st["k"], st["v"], dpo, st["lse"], delta, f"flash_bwd_{i}")
            (dcur, h, dlat, cqn, dqp, ckvn, dkv, dg, dgq, dgkv, dghq, dghk) = _mla_pre_bwd(
                st["x1"], dcur, dq, dk, dv, mix_norm[i:i + 1], *mla_weights(j), f"mla_pre_bwd_{i}")
            gr[("mla_w_in", j)] = _tn_matmul(dlat, h, f"mla_din_{i}")
            gr[("mla_w_q_up", j)] = _tn_matmul(dqp, cqn, f"mla_dqup_{i}")
            gr[("mla_w_kv_up", j)] = _tn_matmul(dkv, ckvn, f"mla_dkvup_{i}")
            small["mix_norm"][i] = jnp.sum(dg, axis=0)
            small["mla_q_norm"][j] = _pad_lanes(jnp.sum(dgq, axis=0)[None], D_MODEL)[0]
            small["mla_kv_norm"][j] = _pad_lanes(jnp.sum(dgkv, axis=0)[None], D_MODEL)[0]
            small["mla_q_head_norm"][j] = _pad_lanes(jnp.sum(dghq, axis=0)[None], D_MODEL)[0]
            small["mla_k_head_norm"][j] = _pad_lanes(jnp.sum(dghk, axis=0)[None], D_MODEL)[0]
        token = reduce_start(_ffn_segments("ffn2", i) + _mixer_segments(i), f"a{i}")
        if i > 0:
            dcur = ffn_backward("ffn1", i, st["x0"], st["a1"], st["u1"], tied(ffn1_norm[i:i + 1], token), dcur)
            token = reduce_start(_ffn_segments("ffn1", i), f"b{i}")
    st = saved[0]
    dcur, h, dob, y, da, du, dg = _ffn_bwd(st["x0"], dcur, st["a1"], st["u1"], tied(ffn1_norm[0:1], token),
                                           w[("ffn1_w_gate", 0)], w[("ffn1_w_up", 0)], w[("ffn1_w_down", 0)],
                                           "ffn1_bwd_0")
    small["ffn1_norm"][0] = jnp.sum(dg, axis=0)
    grad_x = dcur[None]
    small_order = ("ffn1_norm", "mix_norm", "ffn2_norm", "pool_scale", "mla_q_norm", "mla_kv_norm",
                   "mla_q_head_norm", "mla_k_head_norm")
    rows = [r for k in small_order for r in small[k]]
    rows.append(jnp.zeros((D_MODEL,), F32).at[0].set(loss_part))
    rows.append(jnp.zeros((D_MODEL,), F32))
    small_sum = _sum_parts(_all_gather(jnp.stack(rows), "small_all_gather"), "small_sum")
    loss = small_sum[SM_ROWS - 2, 0]
    token = small_sum
    for seg, lhs, rhs in zip(_ffn_segments("ffn1", 0), (da, du, y), (h, h, dob)):
        gr[seg[:2]] = _tn_matmul(lhs, rhs, f"ffn1_d{seg[0][7:]}_0", behind=token)
        token = reduce_start([seg], f"b0_{seg[0][7:]}")
    last = "ffn1_w_down"

    counts = {k: (DEPTH // 2 if k.startswith(("mla", "pool")) else DEPTH) for k in WEIGHTS}

    def stacked(k):
        return jnp.stack([grads[(k, idx)] for idx in range(counts[k])])

    grads.update({k: stacked(k) for k in WEIGHTS if (k, 0) in grads and k != last})
    off = 0
    for k in small_order:
        cnt = len(small[k])
        g = small_sum[off:off + cnt]
        off += cnt
        if k == "mla_q_norm":
            g = lax.dynamic_slice_in_dim(g[:, :Q_LORA], dev * (Q_LORA // N_DEV), Q_LORA // N_DEV, axis=1)
        elif k == "mla_kv_norm":
            g = lax.dynamic_slice_in_dim(g[:, :KV_LORA], dev * (KV_LORA // N_DEV), KV_LORA // N_DEV, axis=1)
        elif k in ("mla_q_head_norm", "mla_k_head_norm"):
            g = g[:, :QK_HEAD]
        grads[k] = g

    deltas, new_m, new_v = {}, {}, {}

    def update(k, behind):
        shape = p[k].shape
        view = (-1, shape[-1])
        d, mn, vn = _adamw(p[k].reshape(view), grads[k].reshape(view), moments_m[k].reshape(view),
                           moments_v[k].reshape(view), behind, "adamw_" + k)
        deltas[k], new_m[k], new_v[k] = d.reshape(shape), mn.reshape(shape), vn.reshape(shape)
        grads[k] = grads[k].reshape(shape)
        return d

    done = token
    for k in WEIGHTS:
        if k != last:
            done = update(k, done)
    reduce_finish(done)
    grads[last] = stacked(last)
    update(last, done)

    return (loss, grad_x, *[grads[k] for k in WEIGHTS], *[deltas[k] for k in WEIGHTS],
            *[new_m[k] for k in WEIGHTS], *[new_v[k] for k in WEIGHTS])
```

```python
import functools

import jax
import jax.numpy as jnp
from jax import lax
from jax.experimental import pallas as pl
from jax.experimental.pallas import tpu as pltpu

F32 = jnp.float32
BF16 = jnp.bfloat16

D_MODEL = 1024
DEPTH = 4
D_FF = 2816
POOL_WINDOWS = (2, 4, 8, 16)
POOL_GROUP = 256
POOL_HALO = 16
N_HEADS = 8
QK_NOPE = 128
QK_ROPE = 64
QK_HEAD = 192
V_HEAD = 128
Q_LORA = 768
KV_LORA = 256
ROPE_THETA = 10000.0
EPS = 1e-6
FFN_HALF = 0.5
ADAM_LR = 0.001
ADAM_B1 = 0.9
ADAM_B2 = 0.999
ADAM_EPS = 1e-08
ADAM_WD = 0.01
ADAM_STEP = 10

N_DEV = 8
HEAD_PAD = 256
LAT_PAD = 1152
VT_ROWS = 144
LOG2_E = 1.4426950408889634
ATTN_SCALE = QK_HEAD ** -0.5
LOGIT_SCALE = ATTN_SCALE * LOG2_E
V7X_VMEM_LIMIT = 56 * 1024 * 1024
FF_CHUNK = 256
SM_ROWS = 24

NAMES = ['x', 'positions', 'ffn1_norm', 'ffn1_w_gate', 'ffn1_w_up', 'ffn1_w_down', 'mix_norm', 'pool_w',
         'pool_scale', 'mla_w_in', 'mla_q_norm', 'mla_w_q_up', 'mla_kv_norm', 'mla_w_kv_up', 'mla_q_head_norm',
         'mla_k_head_norm', 'mla_w_out', 'ffn2_norm', 'ffn2_w_gate', 'ffn2_w_up', 'ffn2_w_down']
WEIGHTS = NAMES[2:]


def _tiles(n):
    return dict(ffn_fwd=min(512, n), ffn_bwd=min(256, n), fwd_k=min(512, n // 2), bwd_q=min(512, n // 2),
                mla_bwd=min(256, n), pool=min(512, n), tn=min(2048, n), rows=min(1024, n))


def _params(*sem):
    return pltpu.CompilerParams(dimension_semantics=sem, vmem_limit_bytes=V7X_VMEM_LIMIT)


def _dot(a, b):
    return jnp.dot(a, b, preferred_element_type=F32)


def _dot_nt(a, b):
    return lax.dot_general(a, b, (((1,), (1,)), ((), ())), preferred_element_type=F32)


def _dot_tn(a, b):
    return lax.dot_general(a, b, (((0,), (0,)), ((), ())), preferred_element_type=F32)


def _rowsum8(v):
    rows, w = v.shape
    return jnp.sum(v.reshape(rows // 8, 8, w), axis=0)


def _sigmoid(a):
    return 1.0 / (1.0 + jnp.exp(-a))


def _rms_fwd(x, width=None):
    width = x.shape[-1] if width is None else width
    r = lax.rsqrt(jnp.sum(x * x, axis=-1, keepdims=True) * (1.0 / width) + EPS)
    return x * r, r


def _rms_bwd(dy, xhat, r, gain, width=None):
    width = xhat.shape[-1] if width is None else width
    t = dy * gain
    return r * (t - xhat * (jnp.sum(t * xhat, axis=-1, keepdims=True) * (1.0 / width)))


def _full(shape):
    return pl.BlockSpec(shape, lambda *_: (0,) * len(shape))


def _load_weights(srcs, dsts, sems):
    copies = [pltpu.make_async_copy(s, d, sems.at[i]) for i, (s, d) in enumerate(zip(srcs, dsts))]
    for cp in copies:
        cp.start()
    for cp in copies:
        cp.wait()


def _ffn_fwd(x, gain, wg_t, wu_t, wd, name):
    n = x.shape[0]
    tm = _tiles(n)["ffn_fwd"]

    def body(x_ref, g_ref, wg_hbm, wu_hbm, wd_hbm, out_ref, a_ref, u_ref, wg_v, wu_v, wd_v, sems):
        @pl.when(pl.program_id(0) == 0)
        def _():
            _load_weights((wg_hbm, wu_hbm, wd_hbm), (wg_v, wu_v, wd_v), sems)

        xt = x_ref[...]
        xhat, _ = _rms_fwd(xt)
        h = (xhat * g_ref[...]).astype(BF16)
        acc = jnp.zeros((tm, D_MODEL), F32)
        for c in range(D_FF // FF_CHUNK):
            sl = pl.ds(c * FF_CHUNK, FF_CHUNK)
            a = _dot_nt(h, wg_v[sl, :])
            u = _dot_nt(h, wu_v[sl, :])
            a_ref[:, sl] = a.astype(BF16)
            u_ref[:, sl] = u.astype(BF16)
            y = (a * _sigmoid(a) * u).astype(BF16)
            acc = acc + _dot(y, wd_v[sl, :])
        out_ref[...] = xt + FFN_HALF * acc

    any_spec = pl.BlockSpec(memory_space=pl.ANY)
    return pl.pallas_call(
        body, name=name, grid=(n // tm,),
        in_specs=[pl.BlockSpec((tm, D_MODEL), lambda i: (i, 0)), _full((1, D_MODEL)), any_spec, any_spec, any_spec],
        out_specs=[pl.BlockSpec((tm, D_MODEL), lambda i: (i, 0)), pl.BlockSpec((tm, D_FF), lambda i: (i, 0)),
                   pl.BlockSpec((tm, D_FF), lambda i: (i, 0))],
        out_shape=[jax.ShapeDtypeStruct((n, D_MODEL), F32), jax.ShapeDtypeStruct((n, D_FF), BF16),
                   jax.ShapeDtypeStruct((n, D_FF), BF16)],
        scratch_shapes=[pltpu.VMEM((D_FF, D_MODEL), BF16)] * 3 + [pltpu.SemaphoreType.DMA((3,))],
        compiler_params=_params("arbitrary"),
    )(x, gain, wg_t, wu_t, wd)


def _ffn_bwd(x, dout, a, u, gain, wg_t, wu_t, wd, name):
    n = x.shape[0]
    tm = _tiles(n)["ffn_bwd"]

    def body(x_ref, do_ref, a_ref, u_ref, g_ref, wg_hbm, wu_hbm, wd_hbm,
             dx_ref, h_ref, dob_ref, y_ref, da_ref, du_ref, dg_ref, wg_v, wu_v, wd_v, sems):
        @pl.when(pl.program_id(0) == 0)
        def _():
            _load_weights((wg_hbm, wu_hbm, wd_hbm), (wg_v, wu_v, wd_v), sems)
            dg_ref[...] = jnp.zeros_like(dg_ref)

        xt = x_ref[...]
        g = g_ref[...]
        xhat, r = _rms_fwd(xt)
        h_ref[...] = (xhat * g).astype(BF16)
        dout = do_ref[...]
        dob = (FFN_HALF * dout).astype(BF16)
        dob_ref[...] = dob
        for c in range(D_FF // FF_CHUNK):
            sl = pl.ds(c * FF_CHUNK, FF_CHUNK)
            dy = _dot_nt(dob, wd_v[sl, :])
            av = a_ref[:, sl].astype(F32)
            uv = u_ref[:, sl].astype(F32)
            s = _sigmoid(av)
            silu = av * s
            y_ref[:, sl] = (silu * uv).astype(BF16)
            du_ref[:, sl] = (dy * silu).astype(BF16)
            da_ref[:, sl] = (dy * uv * (s * (1.0 + av * (1.0 - s)))).astype(BF16)
        dh = _dot(da_ref[...], wg_v[...]) + _dot(du_ref[...], wu_v[...])
        dg_ref[...] += _rowsum8(dh * xhat)
        dx_ref[...] = dout + _rms_bwd(dh, xhat, r, g)

    any_spec = pl.BlockSpec(memory_space=pl.ANY)
    row_d = pl.BlockSpec((tm, D_MODEL), lambda i: (i, 0))
    row_f = pl.BlockSpec((tm, D_FF), lambda i: (i, 0))
    return pl.pallas_call(
        body, name=name, grid=(n // tm,),
        in_specs=[row_d, row_d, row_f, row_f, _full((1, D_MODEL)), any_spec, any_spec, any_spec],
        out_specs=[row_d, row_d, row_d, row_f, row_f, row_f, _full((8, D_MODEL))],
        out_shape=[jax.ShapeDtypeStruct((n, D_MODEL), F32), jax.ShapeDtypeStruct((n, D_MODEL), BF16),
                   jax.ShapeDtypeStruct((n, D_MODEL), BF16), jax.ShapeDtypeStruct((n, D_FF), BF16),
                   jax.ShapeDtypeStruct((n, D_FF), BF16), jax.ShapeDtypeStruct((n, D_FF), BF16),
                   jax.ShapeDtypeStruct((8, D_MODEL), F32)],
        scratch_shapes=[pltpu.VMEM((D_FF, D_MODEL), BF16)] * 3 + [pltpu.SemaphoreType.DMA((3,))],
        compiler_params=_params("arbitrary"),
    )(x, dout, a, u, gain, wg_t, wu_t, wd)


def _tn_matmul(a, b, name, behind=None):
    n, fa = a.shape
    db = b.shape[1]
    tk = _tiles(n)["tn"]
    tf = fa // 2 if (fa // 2) % 128 == 0 and fa > 1024 else fa
    behind = jnp.zeros((8, 128), F32) if behind is None else behind

    def body(a_ref, b_ref, behind_ref, o_ref):
        @pl.when(pl.program_id(1) == 0)
        def _():
            o_ref[...] = jnp.zeros_like(o_ref)

        o_ref[...] += _dot_tn(a_ref[...], b_ref[...])

    return pl.pallas_call(
        body, name=name, grid=(fa // tf, n // tk),
        in_specs=[pl.BlockSpec((tk, tf), lambda i, k: (k, i)), pl.BlockSpec((tk, db), lambda i, k: (k, 0)),
                  pl.BlockSpec(memory_space=pl.ANY)],
        out_specs=pl.BlockSpec((tf, db), lambda i, k: (i, 0)),
        out_shape=jax.ShapeDtypeStruct((fa, db), F32),
        compiler_params=_params("arbitrary", "arbitrary"),
    )(a, b, behind)


def _loss_head(y, target, name):
    n = y.shape[0]
    tm = _tiles(n)["rows"]

    def body(y_ref, t_ref, d_ref, acc_ref):
        @pl.when(pl.program_id(0) == 0)
        def _():
            acc_ref[...] = jnp.zeros_like(acc_ref)

        d = y_ref[...] - t_ref[...]
        d_ref[...] = d * (1.0 / D_MODEL)
        acc_ref[...] += _rowsum8(d * d)

    row = pl.BlockSpec((tm, D_MODEL), lambda i: (i, 0))
    return pl.pallas_call(
        body, name=name, grid=(n // tm,), in_specs=[row, row], out_specs=[row, _full((8, D_MODEL))],
        out_shape=[jax.ShapeDtypeStruct((n, D_MODEL), F32), jax.ShapeDtypeStruct((8, D_MODEL), F32)],
        compiler_params=_params("arbitrary"),
    )(y, target)


def _window_sum(v, w, rows, forward):
    s = v
    sh = 1
    while sh < w:
        s = s + pltpu.roll(s, (rows - sh) if forward else sh, 0)
        sh *= 2
    return s


def _pool_fwd(x, gain, w, scale, name):
    n = x.shape[0]
    tm = _tiles(n)["pool"]
    hb = tm // POOL_HALO
    rows = tm + POOL_HALO

    def body(x_ref, xh_ref, g_ref, w_ref, sc_ref, out_ref):
        i = pl.program_id(0)
        xt = x_ref[...]
        e = jnp.concatenate([xh_ref[...], xt], axis=0)
        xhat, _ = _rms_fwd(e)
        row = lax.broadcasted_iota(jnp.int32, (rows, 1), 0)
        hn = jnp.where((row >= POOL_HALO) | (i > 0), xhat * g_ref[...], 0.0)
        t_glob = i * tm + row - POOL_HALO
        outs = []
        for gi, win in enumerate(POOL_WINDOWS):
            ug = hn[:, gi * POOL_GROUP:(gi + 1) * POOL_GROUP]
            cnt = jnp.maximum(jnp.minimum(t_glob + 1, win), 1).astype(F32)
            pooled = (_window_sum(ug, win, rows, False) / cnt - ug)[POOL_HALO:]
            outs.append(_dot(pooled.astype(BF16), w_ref[gi]))
        out_ref[...] = xt + jnp.concatenate(outs, axis=1) * sc_ref[...]

    return pl.pallas_call(
        body, name=name, grid=(n // tm,),
        in_specs=[pl.BlockSpec((tm, D_MODEL), lambda i: (i, 0)),
                  pl.BlockSpec((POOL_HALO, D_MODEL), lambda i: (jnp.maximum(i * hb - 1, 0), 0)),
                  _full((1, D_MODEL)), _full((4, POOL_GROUP, POOL_GROUP)), _full((1, D_MODEL))],
        out_specs=pl.BlockSpec((tm, D_MODEL), lambda i: (i, 0)),
        out_shape=jax.ShapeDtypeStruct((n, D_MODEL), F32),
        compiler_params=_params("arbitrary"),
    )(x, x, gain, w, scale)


def _pool_bwd(x, dout, gain, w, scale, name):
    n = x.shape[0]
    tm = _tiles(n)["pool"]
    hb = tm // POOL_HALO
    rows = tm + POOL_HALO
    nt = n // tm

    def body(x_ref, xh_ref, do_ref, doh_ref, g_ref, w_ref, sc_ref, dx_ref, dw_ref, dsc_ref, dg_ref):
        i = pl.program_id(0)

        @pl.when(i == 0)
        def _():
            dw_ref[...] = jnp.zeros_like(dw_ref)
            dsc_ref[...] = jnp.zeros_like(dsc_ref)
            dg_ref[...] = jnp.zeros_like(dg_ref)

        xt = x_ref[...]
        g = g_ref[...]
        e = jnp.concatenate([xh_ref[...], xt], axis=0)
        xhat_e, r_e = _rms_fwd(e)
        row = lax.broadcasted_iota(jnp.int32, (rows, 1), 0)
        hn = jnp.where((row >= POOL_HALO) | (i > 0), xhat_e * g, 0.0)
        t_prev = i * tm + row - POOL_HALO
        t_next = i * tm + row
        dout = do_ref[...]
        dt = jnp.concatenate([dout, doh_ref[...]], axis=0)
        dt = jnp.where((row < tm) | (i < nt - 1), dt, 0.0)
        dyr = dt * sc_ref[...]
        dus, dscs = [], []
        for gi, win in enumerate(POOL_WINDOWS):
            lanes = slice(gi * POOL_GROUP, (gi + 1) * POOL_GROUP)
            ug = hn[:, lanes]
            cnt = jnp.maximum(jnp.minimum(t_prev + 1, win), 1).astype(F32)
            pooled = (_window_sum(ug, win, rows, False) / cnt - ug)[POOL_HALO:].astype(BF16)
            yraw = _dot(pooled, w_ref[gi])
            dscs.append(_rowsum8(dout[:, lanes] * yraw))
            dyr_b = dyr[:, lanes].astype(BF16)
            dw_ref[gi] += _dot_tn(pooled, dyr_b[:tm])
            dpool = _dot_nt(dyr_b, w_ref[gi])
            cnt2 = jnp.minimum(t_next + 1, win).astype(F32)
            dus.append((_window_sum(dpool / cnt2, win, rows, True) - dpool)[:tm])
        dsc_ref[...] += jnp.concatenate(dscs, axis=1)
        dh = jnp.concatenate(dus, axis=1)
        xhat = xhat_e[POOL_HALO:]
        dg_ref[...] += _rowsum8(dh * xhat)
        dx_ref[...] = dout + _rms_bwd(dh, xhat, r_e[POOL_HALO:], g)

    row_d = pl.BlockSpec((tm, D_MODEL), lambda i: (i, 0))
    prev_h = pl.BlockSpec((POOL_HALO, D_MODEL), lambda i: (jnp.maximum(i * hb - 1, 0), 0))
    next_h = pl.BlockSpec((POOL_HALO, D_MODEL), lambda i: (jnp.minimum((i + 1) * hb, n // POOL_HALO - 1), 0))
    return pl.pallas_call(
        body, name=name, grid=(nt,),
        in_specs=[row_d, prev_h, row_d, next_h, _full((1, D_MODEL)), _full((4, POOL_GROUP, POOL_GROUP)),
                  _full((1, D_MODEL))],
        out_specs=[row_d, _full((4, POOL_GROUP, POOL_GROUP)), _full((8, D_MODEL)), _full((8, D_MODEL))],
        out_shape=[jax.ShapeDtypeStruct((n, D_MODEL), F32), jax.ShapeDtypeStruct((4, POOL_GROUP, POOL_GROUP), F32),
                   jax.ShapeDtypeStruct((8, D_MODEL), F32), jax.ShapeDtypeStruct((8, D_MODEL), F32)],
        compiler_params=_params("arbitrary"),
    )(x, x, dout, dout, gain, w, scale)


def _rope(v, cos, sin_signed):
    lo, hi = v[:, :128], v[:, 128:]
    lane = lax.broadcasted_iota(jnp.int32, hi.shape, 1)
    swapped = jnp.where(lane < 32, pltpu.roll(hi, 96, 1), pltpu.roll(hi, 32, 1))
    return jnp.concatenate([lo, hi * cos + swapped * sin_signed], axis=1)


def _rope_bwd(gr, cos, sin_signed):
    lo, hi = gr[:, :128], gr[:, 128:]
    t = hi * sin_signed
    lane = lax.broadcasted_iota(jnp.int32, hi.shape, 1)
    swapped = jnp.where(lane < 32, pltpu.roll(t, 96, 1), pltpu.roll(t, 32, 1))
    return jnp.concatenate([lo, hi * cos + swapped], axis=1)


def _mla_latents(h, win_ref):
    cq = _dot_nt(h, win_ref[0:Q_LORA, :])
    ckv = _dot_nt(h, win_ref[Q_LORA:Q_LORA + KV_LORA, :])
    kpe = _dot_nt(h, win_ref[Q_LORA + KV_LORA:LAT_PAD, :])
    return cq, ckv, kpe


def _mla_pre_fwd(x, gain, win, gq, gkv, wq, wkv, ghq, ghk, cos, sin_signed, name):
    n = x.shape[0]
    tm = _tiles(n)["fwd_k"]

    def body(x_ref, g_ref, win_ref, gq_ref, gkv_ref, wq_ref, wkv_ref, ghq_ref, ghk_ref, c_ref, s_ref,
             q_ref, k_ref, v_ref, vt_ref):
        xhat, _ = _rms_fwd(x_ref[...])
        h = (xhat * g_ref[...]).astype(BF16)
        cq, ckv, kpe = _mla_latents(h, win_ref)
        cqn = (_rms_fwd(cq)[0] * gq_ref[...]).astype(BF16)
        ckvn = (_rms_fwd(ckv)[0] * gkv_ref[...]).astype(BF16)
        cos, sn = c_ref[...], s_ref[...]
        for hd in range(N_HEADS):
            rws = pl.ds(hd * HEAD_PAD, HEAD_PAD)
            qh = _dot_nt(cqn, wq_ref[rws, :])
            qn = _rms_fwd(qh, QK_HEAD)[0] * ghq_ref[...]
            q_ref[hd] = (_rope(qn, cos, sn) * LOGIT_SCALE).astype(BF16)
            kvh = _dot_nt(ckvn, wkv_ref[rws, :])
            kpre = jnp.concatenate([kvh[:, :QK_NOPE], kpe], axis=1)
            kn = _rms_fwd(kpre, QK_HEAD)[0] * ghk_ref[...]
            k_ref[hd] = _rope(kn, cos, sn).astype(BF16)
            vh = kvh[:, QK_NOPE:]
            v_ref[hd] = vh.astype(BF16)
            vt_ref[hd, 0] = jnp.concatenate([vh.T, jnp.ones((VT_ROWS - V_HEAD, tm), F32)], axis=0).astype(BF16)

    row = lambda w: pl.BlockSpec((tm, w), lambda i: (i, 0))
    head = lambda w: pl.BlockSpec((N_HEADS, tm, w), lambda i: (0, i, 0))
    return pl.pallas_call(
        body, name=name, grid=(n // tm,),
        in_specs=[row(D_MODEL), _full((1, D_MODEL)), _full((LAT_PAD, D_MODEL)), _full((1, Q_LORA)),
                  _full((1, KV_LORA)), _full((N_HEADS * HEAD_PAD, Q_LORA)), _full((N_HEADS * HEAD_PAD, KV_LORA)),
                  _full((1, HEAD_PAD)), _full((1, HEAD_PAD)), row(128), row(128)],
        out_specs=[head(HEAD_PAD), head(HEAD_PAD), head(V_HEAD),
                   pl.BlockSpec((N_HEADS, 1, VT_ROWS, tm), lambda i: (0, i, 0, 0))],
        out_shape=[jax.ShapeDtypeStruct((N_HEADS, n, HEAD_PAD), BF16), jax.ShapeDtypeStruct((N_HEADS, n, HEAD_PAD), BF16),
                   jax.ShapeDtypeStruct((N_HEADS, n, V_HEAD), BF16),
                   jax.ShapeDtypeStruct((N_HEADS, n // tm, VT_ROWS, tm), BF16)],
        compiler_params=_params("arbitrary"),
    )(x, gain, win, gq, gkv, wq, wkv, ghq, ghk, cos, sin_signed)


def _flash_fwd(q, k, vt, name):
    n = q.shape[1]
    tk = _tiles(n)["fwd_k"]
    tq = 2 * tk
    nq = n // tq

    def body(q_ref, k_ref, vt_ref, o_ref, lse_ref, s_scr, m_scr, acc_scr):
        i = pl.program_id(1)
        qi = q_ref[0]

        def scores(j, slot):
            s_scr[slot] = _dot_nt(k_ref[0, pl.ds(pl.multiple_of(j * tk, tk), tk), :], qi)

        def update(j, slot, diagonal=None):
            s = s_scr[slot]
            if diagonal is not None:
                krow = lax.broadcasted_iota(jnp.int32, (tk, tq), 0) + diagonal * tk
                qcol = lax.broadcasted_iota(jnp.int32, (tk, tq), 1)
                s = jnp.where(krow <= qcol, s, -jnp.inf)
            m = m_scr[...]
            m_new = jnp.maximum(m, jnp.max(s, axis=0, keepdims=True))
            p = jnp.exp2(s - m_new).astype(BF16)
            acc_scr[...] = jnp.exp2(m - m_new) * acc_scr[...] + _dot(vt_ref[0, j], p)
            m_scr[...] = m_new

        m_scr[...] = jnp.full((1, tq), -jnp.inf, F32)
        acc_scr[...] = jnp.zeros((VT_ROWS, tq), F32)
        scores(0, 0)

        def pair(jj, carry):
            scores(2 * jj + 1, 1)
            update(2 * jj, 0)
            scores(2 * jj + 2, 0)
            update(2 * jj + 1, 1)
            return carry

        lax.fori_loop(0, i, pair, 0)
        scores(2 * i + 1, 1)
        update(2 * i, 0, diagonal=0)
        update(2 * i + 1, 1, diagonal=1)
        l = acc_scr[V_HEAD:V_HEAD + 1, :]
        o_ref[...] = (acc_scr[0:V_HEAD, :] / l).T.astype(BF16)
        lse_ref[0, 0] = m_scr[...] + jnp.log2(l)

    return pl.pallas_call(
        body, name=name, grid=(N_HEADS, nq),
        in_specs=[pl.BlockSpec((1, tq, HEAD_PAD), lambda h, i: (h, i, 0)),
                  pl.BlockSpec((1, n, HEAD_PAD), lambda h, i: (h, 0, 0)),
                  pl.BlockSpec((1, n // tk, VT_ROWS, tk), lambda h, i: (h, 0, 0, 0))],
        out_specs=[pl.BlockSpec((tq, V_HEAD), lambda h, i: (i, h)),
                   pl.BlockSpec((1, 1, 1, tq), lambda h, i: (h, i, 0, 0))],
        out_shape=[jax.ShapeDtypeStruct((n, N_HEADS * V_HEAD), BF16), jax.ShapeDtypeStruct((N_HEADS, nq, 1, tq), F32)],
        scratch_shapes=[pltpu.VMEM((2, tk, tq), F32), pltpu.VMEM((1, tq), F32), pltpu.VMEM((VT_ROWS, tq), F32)],
        compiler_params=_params("arbitrary", "arbitrary"),
    )(q, k, vt)


def _mla_post_fwd(o, x, wout, name):
    n = x.shape[0]
    tm = _tiles(n)["rows"]

    def body(o_ref, x_ref, w_ref, out_ref):
        out_ref[...] = x_ref[...] + _dot(o_ref[...], w_ref[...])

    row = pl.BlockSpec((tm, D_MODEL), lambda i: (i, 0))
    return pl.pallas_call(
        body, name=name, grid=(n // tm,), in_specs=[row, row, _full((D_MODEL, D_MODEL))], out_specs=row,
        out_shape=jax.ShapeDtypeStruct((n, D_MODEL), F32), compiler_params=_params("arbitrary"),
    )(o, x, wout)


def _mla_out_bwd(dout, o, wout, name):
    n = dout.shape[0]
    t = _tiles(n)["bwd_q"]
    nq = n // t

    def body(do_ref, o_ref, w_ref, dob_ref, dpo_ref, dl_ref):
        dob = do_ref[...].astype(BF16)
        dob_ref[...] = dob
        dpo = _dot_nt(dob, w_ref[...])
        dpo_ref[...] = dpo.astype(BF16)
        ov = o_ref[...].astype(F32)
        for hd in range(N_HEADS):
            lanes = slice(hd * V_HEAD, (hd + 1) * V_HEAD)
            prod = dpo[:, lanes] * ov[:, lanes]
            dl_ref[hd, 0] = jnp.sum(prod.T, axis=0, keepdims=True)

    row = pl.BlockSpec((t, D_MODEL), lambda i: (i, 0))
    return pl.pallas_call(
        body, name=name, grid=(nq,), in_specs=[row, row, _full((D_MODEL, D_MODEL))],
        out_specs=[row, row, pl.BlockSpec((N_HEADS, 1, 1, t), lambda i: (0, i, 0, 0))],
        out_shape=[jax.ShapeDtypeStruct((n, D_MODEL), BF16), jax.ShapeDtypeStruct((n, D_MODEL), BF16),
                   jax.ShapeDtypeStruct((N_HEADS, nq, 1, t), F32)],
        compiler_params=_params("arbitrary"),
    )(dout, o, wout)


def _flash_bwd(q, k, v, dpo, lse, delta, name):
    n = q.shape[1]
    tq = _tiles(n)["bwd_q"]
    tk = 2 * tq
    nk = n // tk
    nqb = n // tq

    def body(k_ref, v_ref, q_ref, do_ref, lse_ref, dl_ref, dk_ref, dv_ref, dq_hbm,
             dq_acc, s_scr, dp_scr, dk_acc, dv_acc, sem):
        h = pl.program_id(0)
        j = pl.program_id(1)

        @pl.when(j == 0)
        def _():
            dq_acc[...] = jnp.zeros_like(dq_acc)

        dk_acc[...] = jnp.zeros_like(dk_acc)
        dv_acc[...] = jnp.zeros_like(dv_acc)
        kj = k_ref[0]
        vj = v_ref[0]
        npairs = nk - 1 - j

        def block(t):
            return jnp.where(t < 2 * npairs, 2 * j + 2 + t, 2 * j + (t - 2 * npairs))

        def scores(i, slot):
            rws = pl.ds(pl.multiple_of(i * tq, tq), tq)
            s_scr[slot] = _dot_nt(kj, q_ref[0, rws, :])
            dp_scr[slot] = _dot_nt(vj, do_ref[rws, :])

        def update(i, slot, diagonal=None):
            rws = pl.ds(pl.multiple_of(i * tq, tq), tq)
            p = jnp.exp2(s_scr[slot] - lse_ref[0, i])
            if diagonal is not None:
                krow = lax.broadcasted_iota(jnp.int32, (tk, tq), 0)
                qcol = lax.broadcasted_iota(jnp.int32, (tk, tq), 1) + diagonal * tq
                p = jnp.where(krow <= qcol, p, 0.0)
            dv_acc[...] += _dot(p.astype(BF16), do_ref[rws, :])
            ds = (p * (dp_scr[slot] - dl_ref[0, i])).astype(BF16)
            dk_acc[...] += _dot(ds, q_ref[0, rws, :])
            dq_acc[rws, :] += _dot_tn(ds, kj)

        scores(block(0), 0)

        def pair(jj, carry):
            scores(block(2 * jj + 1), 1)
            update(block(2 * jj), 0)
            scores(block(2 * jj + 2), 0)
            update(block(2 * jj + 1), 1)
            return carry

        lax.fori_loop(0, npairs, pair, 0)
        scores(2 * j + 1, 1)
        update(2 * j, 0, diagonal=0)
        update(2 * j + 1, 1, diagonal=1)
        dk_ref[0] = dk_acc[...] * (ATTN_SCALE / LOGIT_SCALE)
        dv_ref[0] = dv_acc[...]

        @pl.when(j == nk - 1)
        def _():
            dq_acc[...] = dq_acc[...] * ATTN_SCALE
            cp = pltpu.make_async_copy(dq_acc, dq_hbm.at[h], sem)
            cp.start()
            cp.wait()

    resident = dict(pipeline_mode=pl.Buffered(1))
    return pl.pallas_call(
        body, name=name, grid=(N_HEADS, nk),
        in_specs=[pl.BlockSpec((1, tk, HEAD_PAD), lambda h, j: (h, j, 0)),
                  pl.BlockSpec((1, tk, V_HEAD), lambda h, j: (h, j, 0)),
                  pl.BlockSpec((1, n, HEAD_PAD), lambda h, j: (h, 0, 0), **resident),
                  pl.BlockSpec((n, V_HEAD), lambda h, j: (0, h), **resident),
                  pl.BlockSpec((1, nqb, 1, tq), lambda h, j: (h, 0, 0, 0)),
                  pl.BlockSpec((1, nqb, 1, tq), lambda h, j: (h, 0, 0, 0))],
        out_specs=[pl.BlockSpec((1, tk, HEAD_PAD), lambda h, j: (h, j, 0)),
                   pl.BlockSpec((1, tk, V_HEAD), lambda h, j: (h, j, 0)),
                   pl.BlockSpec(memory_space=pl.ANY)],
        out_shape=[jax.ShapeDtypeStruct((N_HEADS, n, HEAD_PAD), F32), jax.ShapeDtypeStruct((N_HEADS, n, V_HEAD), F32),
                   jax.ShapeDtypeStruct((N_HEADS, n, HEAD_PAD), F32)],
        scratch_shapes=[pltpu.VMEM((n, HEAD_PAD), F32), pltpu.VMEM((2, tk, tq), F32), pltpu.VMEM((2, tk, tq), F32),
                        pltpu.VMEM((tk, HEAD_PAD), F32), pltpu.VMEM((tk, V_HEAD), F32), pltpu.SemaphoreType.DMA(())],
        compiler_params=_params("arbitrary", "arbitrary"),
    )(k, v, q, dpo, lse, delta)


def _mla_pre_bwd(x, dout, dq, dk, dv, gain, win, gq, gkv, wq, wkv, ghq, ghk, cos, sin_signed, name):
    n = x.shape[0]
    tm = _tiles(n)["mla_bwd"]
    hw = N_HEADS * HEAD_PAD

    def body(x_ref, do_ref, dq_ref, dk_ref, dv_ref, g_ref, win_ref, gq_ref, gkv_ref, wq_ref, wkv_ref, ghq_ref, ghk_ref,
             c_ref, s_ref, dx_ref, h_ref, dlat_ref, cqn_ref, dqp_ref, ckvn_ref, dkv_ref,
             dg_ref, dgq_ref, dgkv_ref, dghq_ref, dghk_ref):
        @pl.when(pl.program_id(0) == 0)
        def _():
            for ref in (dg_ref, dgq_ref, dgkv_ref, dghq_ref, dghk_ref):
                ref[...] = jnp.zeros_like(ref)

        g = g_ref[...]
        xhat, r = _rms_fwd(x_ref[...])
        h = (xhat * g).astype(BF16)
        h_ref[...] = h
        cq, ckv, kpe = _mla_latents(h, win_ref)
        cqhat, rcq = _rms_fwd(cq)
        ckvhat, rckv = _rms_fwd(ckv)
        cqn = (cqhat * gq_ref[...]).astype(BF16)
        ckvn = (ckvhat * gkv_ref[...]).astype(BF16)
        cqn_ref[...] = cqn
        ckvn_ref[...] = ckvn
        cos, sn = c_ref[...], s_ref[...]
        ghq, ghk = ghq_ref[...], ghk_ref[...]
        dkpe = jnp.zeros((tm, 128), F32)
        dghq = jnp.zeros((8, HEAD_PAD), F32)
        dghk = jnp.zeros((8, HEAD_PAD), F32)
        for hd in range(N_HEADS):
            rws = pl.ds(hd * HEAD_PAD, HEAD_PAD)
            lanes = slice(hd * HEAD_PAD, (hd + 1) * HEAD_PAD)
            qhat, rq = _rms_fwd(_dot_nt(cqn, wq_ref[rws, :]), QK_HEAD)
            gqn = _rope_bwd(dq_ref[hd], cos, sn)
            dghq = dghq + _rowsum8(gqn * qhat)
            dqpre = _rms_bwd(gqn, qhat, rq, ghq, QK_HEAD).astype(BF16)
            dqp_ref[:, lanes] = dqpre
            kvh = _dot_nt(ckvn, wkv_ref[rws, :])
            khat, rk = _rms_fwd(jnp.concatenate([kvh[:, :QK_NOPE], kpe], axis=1), QK_HEAD)
            gkn = _rope_bwd(dk_ref[hd], cos, sn)
            dghk = dghk + _rowsum8(gkn * khat)
            dkpre = _rms_bwd(gkn, khat, rk, ghk, QK_HEAD)
            dkpe = dkpe + dkpre[:, QK_NOPE:]
            dkvh = jnp.concatenate([dkpre[:, :QK_NOPE], dv_ref[hd]], axis=1).astype(BF16)
            dkv_ref[:, lanes] = dkvh
        dcqn = _dot(dqp_ref[...], wq_ref[...])
        dckvn = _dot(dkv_ref[...], wkv_ref[...])
        dghq_ref[...] += dghq
        dghk_ref[...] += dghk
        dgq_ref[...] += _rowsum8(dcqn * cqhat)
        dgkv_ref[...] += _rowsum8(dckvn * ckvhat)
        dlat = jnp.concatenate([_rms_bwd(dcqn, cqhat, rcq, gq_ref[...]), _rms_bwd(dckvn, ckvhat, rckv, gkv_ref[...]),
                                dkpe], axis=1).astype(BF16)
        dlat_ref[...] = dlat
        dh = _dot(dlat, win_ref[...])
        dg_ref[...] += _rowsum8(dh * xhat)
        dx_ref[...] = do_ref[...] + _rms_bwd(dh, xhat, r, g)

    row = lambda w: pl.BlockSpec((tm, w), lambda i: (i, 0))
    head = lambda w: pl.BlockSpec((N_HEADS, tm, w), lambda i: (0, i, 0))
    sds = jax.ShapeDtypeStruct
    return pl.pallas_call(
        body, name=name, grid=(n // tm,),
        in_specs=[row(D_MODEL), row(D_MODEL), head(HEAD_PAD), head(HEAD_PAD), head(V_HEAD), _full((1, D_MODEL)),
                  _full((LAT_PAD, D_MODEL)), _full((1, Q_LORA)), _full((1, KV_LORA)), _full((hw, Q_LORA)),
                  _full((hw, KV_LORA)), _full((1, HEAD_PAD)), _full((1, HEAD_PAD)), row(128), row(128)],
        out_specs=[row(D_MODEL), row(D_MODEL), row(LAT_PAD), row(Q_LORA), row(hw), row(KV_LORA), row(hw),
                   _full((8, D_MODEL)), _full((8, Q_LORA)), _full((8, KV_LORA)), _full((8, HEAD_PAD)),
                   _full((8, HEAD_PAD))],
        out_shape=[sds((n, D_MODEL), F32), sds((n, D_MODEL), BF16), sds((n, LAT_PAD), BF16), sds((n, Q_LORA), BF16),
                   sds((n, hw), BF16), sds((n, KV_LORA), BF16), sds((n, hw), BF16), sds((8, D_MODEL), F32),
                   sds((8, Q_LORA), F32), sds((8, KV_LORA), F32), sds((8, HEAD_PAD), F32), sds((8, HEAD_PAD), F32)],
        compiler_params=_params("arbitrary"),
    )(x, dout, dq, dk, dv, gain, win, gq, gkv, wq, wkv, ghq, ghk, cos, sin_signed)


def _adamw(w, g, m, v, behind, name):
    rows, cols = w.shape
    tr = rows
    for cand in (512, 256, 128, 64, 32, 16, 8):
        if rows % cand == 0 and rows > cand:
            tr = cand
            break

    def body(w_ref, g_ref, m_ref, v_ref, behind_ref, d_ref, mo_ref, vo_ref):
        gv = g_ref[...]
        mn = ADAM_B1 * m_ref[...] + (1.0 - ADAM_B1) * gv
        vn = ADAM_B2 * v_ref[...] + (1.0 - ADAM_B2) * (gv * gv)
        m_hat = mn / (1.0 - ADAM_B1 ** ADAM_STEP)
        v_hat = vn / (1.0 - ADAM_B2 ** ADAM_STEP)
        d_ref[...] = -ADAM_LR * (m_hat / (jnp.sqrt(v_hat) + ADAM_EPS) + ADAM_WD * w_ref[...])
        mo_ref[...] = mn
        vo_ref[...] = vn

    blk = pl.BlockSpec((tr, cols), lambda i: (i, 0))
    return pl.pallas_call(
        body, name=name, grid=(rows // tr,), in_specs=[blk] * 4 + [pl.BlockSpec(memory_space=pl.ANY)],
        out_specs=[blk] * 3, out_shape=[jax.ShapeDtypeStruct((rows, cols), F32)] * 3,
        compiler_params=_params("arbitrary"),
    )(w, g, m, v, behind)


def _sum_parts(parts, name):
    k, r, c = parts.shape
    tr = min(r, 512)

    def body(p_ref, o_ref):
        acc = p_ref[0]
        for j in range(1, k):
            acc = acc + p_ref[j]
        o_ref[...] = acc

    return pl.pallas_call(
        body, name=name, grid=(r // tr,), in_specs=[pl.BlockSpec((k, tr, c), lambda i: (0, i, 0))],
        out_specs=pl.BlockSpec((tr, c), lambda i: (i, 0)), out_shape=jax.ShapeDtypeStruct((r, c), parts.dtype),
        compiler_params=_params("arbitrary"),
    )(parts)


def _row_tile(r, most=256):
    best = r
    for cand in range(8, most + 1, 8):
        if r % cand == 0:
            best = cand
    return best


def _sum_exchange(mine, landed, me, name):
    _, r, c = mine.shape
    tr = _row_tile(r)

    def body(me_ref, m_ref, l_ref, o_ref):
        acc = m_ref[0]
        for k in range(1, N_DEV):
            acc = acc + l_ref[k]
        o_ref[...] = acc

    return pl.pallas_call(
        body, name=name,
        grid_spec=pltpu.PrefetchScalarGridSpec(
            num_scalar_prefetch=1, grid=(r // tr,),
            in_specs=[pl.BlockSpec((1, tr, c), lambda i, me_ref: (me_ref[0], i, 0)),
                      pl.BlockSpec((N_DEV, tr, c), lambda i, me_ref: (0, i, 0))],
            out_specs=pl.BlockSpec((tr, c), lambda i, me_ref: (i, 0))),
        out_shape=jax.ShapeDtypeStruct((r, c), mine.dtype), compiler_params=_params("arbitrary"),
    )(me, mine, landed)


MESH = pl.DeviceIdType.MESH


def _all_gather(x, name):
    r, c = x.shape

    def body(x_ref, out_ref, send_sems, recv_sems, local_sem):
        mx, my, mc = lax.axis_index("x"), lax.axis_index("y"), lax.axis_index("c")
        me, sibling = (mx, my, mc), (mx, my, 1 - mc)
        chips = [(1 - mx, my), (mx, 1 - my), (1 - mx, 1 - my)]

        def slot(px, py, pc):
            return out_ref.at[4 * px + 2 * py + pc]

        def copy(k, block, to, src=None):
            return pltpu.make_async_remote_copy(
                src_ref=slot(*block) if src is None else src, dst_ref=slot(*block),
                send_sem=send_sems.at[k], recv_sem=recv_sems.at[k], device_id=to, device_id_type=MESH)

        mine = pltpu.make_async_copy(x_ref, slot(*me), local_sem)
        mine.start()
        first = [copy(0, me, sibling, src=x_ref)]
        first += [copy(1 + j, me, (*chip, mc), src=x_ref) for j, chip in enumerate(chips)]
        for cp in first:
            cp.start()
        passed = [copy(4 + j, (*chip, mc), sibling) for j, chip in enumerate(chips)]
        for j, chip in enumerate(chips):
            copy(1 + j, (*chip, mc), me).wait_recv()
            passed[j].start()
        copy(0, sibling, me).wait_recv()
        for j, chip in enumerate(chips):
            copy(4 + j, (*chip, 1 - mc), me).wait_recv()
        for cp in first + passed:
            cp.wait_send()
        mine.wait()

    any_spec = pl.BlockSpec(memory_space=pl.ANY)
    return pl.pallas_call(
        body, name=name, in_specs=[any_spec], out_specs=any_spec,
        out_shape=jax.ShapeDtypeStruct((N_DEV, r, c), x.dtype),
        scratch_shapes=[pltpu.SemaphoreType.DMA((7,)), pltpu.SemaphoreType.DMA((7,)), pltpu.SemaphoreType.DMA(())],
    )(x)


HBM_SPEC = pl.BlockSpec(memory_space=pltpu.HBM)
SEM_SPEC = pl.BlockSpec(memory_space=pltpu.SEMAPHORE)
SPLIT_EFFECT = pltpu.SideEffectType.DATAFLOW_SIDE_EFFECTING


def _exchange_copies(src_ref, land_ref, send_sems, recv_sems, gather):
    mx, my, mc = lax.axis_index("x"), lax.axis_index("y"), lax.axis_index("c")
    me = 4 * mx + 2 * my + mc
    copies = []
    for k in range(1, N_DEV):
        px = 1 - mx if k & 4 else mx
        py = 1 - my if k & 2 else my
        pc = 1 - mc if k & 1 else mc
        src = src_ref if gather else src_ref.at[4 * px + 2 * py + pc]
        dst = land_ref.at[me] if gather else land_ref.at[k]
        copies.append(pltpu.make_async_remote_copy(
            src_ref=src, dst_ref=dst, send_sem=send_sems.at[k - 1], recv_sem=recv_sems.at[k - 1],
            device_id=(px, py, pc), device_id_type=MESH))
    return copies


def _exchange_start(src, after, gather, name):
    land_shape = (N_DEV,) + src.shape[-2:]

    def body(src_ref, land_ref, after_ref, send_sems, recv_sems, src_thru, land_thru, token):
        for cp in _exchange_copies(src_ref, land_ref, send_sems, recv_sems, gather):
            cp.start()
        token[...] = jnp.zeros_like(token)

    return pl.pallas_call(
        body, name=name,
        out_shape=(pltpu.SemaphoreType.DMA((N_DEV - 1,)), pltpu.SemaphoreType.DMA((N_DEV - 1,)),
                   pltpu.HBM(src.shape, src.dtype), pltpu.HBM(land_shape, src.dtype),
                   jax.ShapeDtypeStruct((8, 128), F32)),
        in_specs=(HBM_SPEC, HBM_SPEC, pl.BlockSpec(memory_space=pl.ANY)),
        out_specs=(SEM_SPEC, SEM_SPEC, HBM_SPEC, HBM_SPEC, pl.BlockSpec(memory_space=pltpu.VMEM)),
        input_output_aliases={0: 2, 1: 3},
        compiler_params=pltpu.CompilerParams(has_side_effects=SPLIT_EFFECT),
    )(pltpu.with_memory_space_constraint(src, pltpu.HBM),
      pltpu.with_memory_space_constraint(lax.empty(land_shape, src.dtype), pltpu.HBM), after)


def _exchange_wait(started, after, gather, name):
    send_sems, recv_sems, src_thru, land_thru, _ = started

    def body(src_ref, land_ref, send_sems, recv_sems, after_ref, src_out, land_out):
        for cp in _exchange_copies(src_ref, land_ref, send_sems, recv_sems, gather):
            cp.wait_send()
            cp.wait_recv()

    return pl.pallas_call(
        body, name=name,
        out_shape=(pltpu.HBM(src_thru.shape, src_thru.dtype), pltpu.HBM(land_thru.shape, land_thru.dtype)),
        in_specs=(HBM_SPEC, HBM_SPEC, SEM_SPEC, SEM_SPEC, pl.BlockSpec(memory_space=pl.ANY)),
        out_specs=(HBM_SPEC, HBM_SPEC), input_output_aliases={0: 0, 1: 1},
        compiler_params=pltpu.CompilerParams(has_side_effects=SPLIT_EFFECT),
    )(src_thru, land_thru, send_sems, recv_sems, after)


FFN_ROWS = D_FF // N_DEV
WIN_ROWS = (Q_LORA + KV_LORA + QK_ROPE) // N_DEV
WIN_ROWS_PAD = 144
WQ_ROWS = QK_HEAD * Q_LORA // D_MODEL
WKV_ROWS = 256 * KV_LORA // D_MODEL
WOUT_ROWS = V_HEAD
POOL_ROWS = 4 * 32 * POOL_GROUP // D_MODEL


COLUMN_SHARDED = ("ffn1_w_gate", "ffn1_w_up", "ffn2_w_gate", "ffn2_w_up", "mla_w_in", "mla_w_q_up", "mla_w_kv_up")


def _t(w):
    return jnp.swapaxes(w, -1, -2)


def _ffn_segments(f, i):
    return [(f + "_w_gate", i, FFN_ROWS), (f + "_w_up", i, FFN_ROWS), (f + "_w_down", i, FFN_ROWS)]


def _mixer_segments(i):
    j = i // 2
    if i % 2 == 0:
        return [("pool_w", j, POOL_ROWS)]
    return [("mla_w_in", j, WIN_ROWS_PAD), ("mla_w_q_up", j, WQ_ROWS), ("mla_w_kv_up", j, WKV_ROWS),
            ("mla_w_out", j, WOUT_ROWS)]


def _pack_shards(p, segs, dtype):
    parts = []
    for name, idx, _ in segs:
        w = p[name][idx]
        if name.endswith("w_gate") or name.endswith("w_up"):
            w = _t(w)
        elif name == "mla_w_in":
            w = jnp.pad(_t(w), ((0, WIN_ROWS_PAD - WIN_ROWS), (0, 0)))
        elif name == "mla_w_q_up":
            w = _t(w).reshape(WQ_ROWS, D_MODEL)
        elif name == "mla_w_kv_up":
            w = _t(w).reshape(WKV_ROWS, D_MODEL)
        elif name == "pool_w":
            w = w.reshape(POOL_ROWS, D_MODEL)
        parts.append(w.astype(dtype))
    return jnp.concatenate(parts, axis=0)


def _unpack_gathered(g, segs):
    out = {}
    off = 0
    for name, layer, rows in segs:
        seg = g[:, off:off + rows, :]
        off += rows
        if name == "mla_w_in":
            w = seg[:, :WIN_ROWS].reshape(N_DEV * WIN_ROWS, D_MODEL)
            w = jnp.pad(w, ((0, LAT_PAD - N_DEV * WIN_ROWS), (0, 0)))
        elif name == "mla_w_q_up":
            w = seg.reshape(N_HEADS, QK_HEAD, Q_LORA)
            w = jnp.pad(w, ((0, 0), (0, HEAD_PAD - QK_HEAD), (0, 0))).reshape(N_HEADS * HEAD_PAD, Q_LORA)
        elif name == "mla_w_kv_up":
            w = seg.reshape(N_HEADS * 256, KV_LORA)
        elif name == "pool_w":
            w = seg.reshape(N_DEV, 4, 32, POOL_GROUP).transpose(1, 0, 2, 3).reshape(4, POOL_GROUP, POOL_GROUP)
        else:
            w = seg.reshape(N_DEV * rows, D_MODEL)
        out[(name, layer)] = w
    return out


def _pack_grads(gr, segments):
    segs = []
    for name, layer, rows in segments:
        g = gr[(name, layer)]
        if name == "mla_w_in":
            g = g[:N_DEV * WIN_ROWS].reshape(N_DEV, WIN_ROWS, D_MODEL)
            g = jnp.pad(g, ((0, 0), (0, WIN_ROWS_PAD - WIN_ROWS), (0, 0)))
        elif name == "mla_w_q_up":
            g = g.reshape(N_HEADS, HEAD_PAD, Q_LORA)[:, :QK_HEAD].reshape(N_DEV, WQ_ROWS, D_MODEL)
        elif name == "mla_w_kv_up":
            g = g.reshape(N_DEV, WKV_ROWS, D_MODEL)
        elif name == "pool_w":
            g = g.reshape(4, N_DEV, 32, POOL_GROUP).transpose(1, 0, 2, 3).reshape(N_DEV, POOL_ROWS, D_MODEL)
        else:
            g = g.reshape(N_DEV, rows, D_MODEL)
        segs.append(g)
    return jnp.concatenate(segs, axis=1)


def _unpack_shard_grads(flat, segments):
    per = {}
    off = 0
    for name, layer, rows in segments:
        seg = flat[off:off + rows]
        off += rows
        if name == "mla_w_in":
            g = seg[:WIN_ROWS]
        elif name == "mla_w_q_up":
            g = seg.reshape(QK_HEAD, Q_LORA)
        elif name == "mla_w_kv_up":
            g = seg.reshape(256, KV_LORA)
        elif name == "pool_w":
            g = seg.reshape(4, 32, POOL_GROUP)
        else:
            g = seg
        per[(name, layer)] = g
    return per


def _pad_lanes(v, width):
    return jnp.pad(v, ((0, 0), (0, width - v.shape[-1])))


def kernel(x, positions, ffn1_norm, ffn1_w_gate, ffn1_w_up, ffn1_w_down, mix_norm, pool_w, pool_scale, mla_w_in, mla_q_norm, mla_w_q_up, mla_kv_norm, mla_w_kv_up, mla_q_head_norm, mla_k_head_norm, mla_w_out, ffn2_norm, ffn2_w_gate, ffn2_w_up, ffn2_w_down, loss_target, m_ffn1_norm, m_ffn1_w_gate, m_ffn1_w_up, m_ffn1_w_down, m_mix_norm, m_pool_w, m_pool_scale, m_mla_w_in, m_mla_q_norm, m_mla_w_q_up, m_mla_kv_norm, m_mla_w_kv_up, m_mla_q_head_norm, m_mla_k_head_norm, m_mla_w_out, m_ffn2_norm, m_ffn2_w_gate, m_ffn2_w_up, m_ffn2_w_down, v_ffn1_norm, v_ffn1_w_gate, v_ffn1_w_up, v_ffn1_w_down, v_mix_norm, v_pool_w, v_pool_scale, v_mla_w_in, v_mla_q_norm, v_mla_w_q_up, v_mla_kv_norm, v_mla_w_kv_up, v_mla_q_head_norm, v_mla_k_head_norm, v_mla_w_out, v_ffn2_norm, v_ffn2_w_gate, v_ffn2_w_up, v_ffn2_w_down):
    args = (x, positions, ffn1_norm, ffn1_w_gate, ffn1_w_up, ffn1_w_down, mix_norm, pool_w, pool_scale, mla_w_in,
            mla_q_norm, mla_w_q_up, mla_kv_norm, mla_w_kv_up, mla_q_head_norm, mla_k_head_norm, mla_w_out, ffn2_norm,
            ffn2_w_gate, ffn2_w_up, ffn2_w_down)
    p = dict(zip(NAMES, args))
    moments_m = dict(zip(WEIGHTS, (m_ffn1_norm, m_ffn1_w_gate, m_ffn1_w_up, m_ffn1_w_down, m_mix_norm, m_pool_w, m_pool_scale, m_mla_w_in, m_mla_q_norm, m_mla_w_q_up, m_mla_kv_norm, m_mla_w_kv_up, m_mla_q_head_norm, m_mla_k_head_norm, m_mla_w_out, m_ffn2_norm, m_ffn2_w_gate, m_ffn2_w_up, m_ffn2_w_down)))
    moments_v = dict(zip(WEIGHTS, (v_ffn1_norm, v_ffn1_w_gate, v_ffn1_w_up, v_ffn1_w_down, v_mix_norm, v_pool_w, v_pool_scale, v_mla_w_in, v_mla_q_norm, v_mla_w_q_up, v_mla_kv_norm, v_mla_w_kv_up, v_mla_q_head_norm, v_mla_k_head_norm, v_mla_w_out, v_ffn2_norm, v_ffn2_w_gate, v_ffn2_w_up, v_ffn2_w_down)))
    dev = 4 * lax.axis_index("x") + 2 * lax.axis_index("y") + lax.axis_index("c")

    xs = x[0]
    n = xs.shape[0]
    target = loss_target[0]

    inv_freq = 1.0 / (ROPE_THETA ** (jnp.arange(0, QK_ROPE, 2, dtype=F32) / QK_ROPE))
    ang = positions[0].astype(F32)[..., None] * inv_freq
    cos, sin = jnp.cos(ang), jnp.sin(ang)
    zero = jnp.zeros((n, 128 - QK_ROPE), F32)
    rope_cos = jnp.concatenate([cos, cos, zero], axis=1)
    rope_sin = jnp.concatenate([-sin, sin, zero], axis=1)

    ag_groups = [_ffn_segments("ffn1", 0), _mixer_segments(0) + _ffn_segments("ffn2", 0)]
    ag_groups += [_ffn_segments("ffn1", i) + _mixer_segments(i) + _ffn_segments("ffn2", i) for i in range(1, DEPTH)]
    shards = [_pack_shards(p, segs, BF16) for segs in ag_groups]
    w = {}
    no_token = jnp.zeros((8, 128), F32)

    def tied(gain, token):
        return gain + token[0, 0]

    gains_local = jnp.concatenate([_pad_lanes(mla_q_norm, 128), _pad_lanes(mla_kv_norm, 128)], axis=0)
    gains_all = _all_gather(jnp.pad(gains_local, ((0, 4), (0, 0))), "gains_all_gather")
    q_norm_full = gains_all[:, 0:2, :Q_LORA // N_DEV].transpose(1, 0, 2).reshape(2, Q_LORA)
    kv_norm_full = gains_all[:, 2:4, :KV_LORA // N_DEV].transpose(1, 0, 2).reshape(2, KV_LORA)
    ghq = _pad_lanes(mla_q_head_norm, HEAD_PAD)
    ghk = _pad_lanes(mla_k_head_norm, HEAD_PAD)

    def mla_weights(j):
        return (w[("mla_w_in", j)], q_norm_full[j:j + 1], kv_norm_full[j:j + 1], w[("mla_w_q_up", j)],
                w[("mla_w_kv_up", j)], ghq[j:j + 1], ghk[j:j + 1], rope_cos, rope_sin)

    saved = []
    cur = xs
    gathers = [_exchange_start(shards[0], gains_all, True, "ag_start_0")]

    def gather_step(after):
        g = len(gathers) - 1
        _, landed = _exchange_wait(gathers[g], after, True, f"ag_wait_{g}")
        w.update(_unpack_gathered(lax.dynamic_update_slice(landed, shards[g][None], (dev, 0, 0)), ag_groups[g]))
        if g + 1 == len(ag_groups):
            return no_token
        gathers.append(_exchange_start(shards[g + 1], landed, True, f"ag_start_{g + 1}"))
        return gathers[-1][4]

    for i in range(DEPTH):
        j = i // 2
        st = {"x0": cur}
        token = gather_step(cur)
        cur, st["a1"], st["u1"] = _ffn_fwd(cur, tied(ffn1_norm[i:i + 1], token), w[("ffn1_w_gate", i)],
                                           w[("ffn1_w_up", i)], w[("ffn1_w_down", i)], f"ffn1_fwd_{i}")
        st["x1"] = cur
        if i == 0:
            token = gather_step(cur)
        if i % 2 == 0:
            cur = _pool_fwd(cur, tied(mix_norm[i:i + 1], token), w[("pool_w", j)], pool_scale[j:j + 1],
                            f"pool_fwd_{i}")
        else:
            st["q"], st["k"], st["v"], vt = _mla_pre_fwd(cur, mix_norm[i:i + 1], *mla_weights(j), f"mla_pre_fwd_{i}")
            st["o"], lse = _flash_fwd(st["q"], st["k"], vt, f"flash_fwd_{i}")
            tq_bwd = _tiles(n)["bwd_q"]
            st["lse"] = lse.reshape(N_HEADS, n // tq_bwd, 1, tq_bwd)
            cur = _mla_post_fwd(st["o"], cur, w[("mla_w_out", j)], f"mla_post_fwd_{i}")
        st["x2"] = cur
        cur, st["a2"], st["u2"] = _ffn_fwd(cur, ffn2_norm[i:i + 1], w[("ffn2_w_gate", i)], w[("ffn2_w_up", i)],
                                           w[("ffn2_w_down", i)], f"ffn2_fwd_{i}")
        saved.append(st)

    dcur, sq_err = _loss_head(cur, target, "loss_head")
    loss_part = 0.5 * jnp.sum(sq_err) * (1.0 / D_MODEL)

    gr = {}
    small = {k: [None] * DEPTH for k in ("ffn1_norm", "mix_norm", "ffn2_norm")}
    small.update({k: [None] * (DEPTH // 2) for k in ("pool_scale", "mla_q_norm", "mla_kv_norm", "mla_q_head_norm",
                                                     "mla_k_head_norm")})

    me = jnp.reshape(dev, (1,)).astype(jnp.int32)
    grads = {}
    in_flight = []

    def reduce_start(segs, tag):
        started = _exchange_start(_pack_grads(gr, segs), dcur, False, f"rs_start_{tag}")
        reduce_finish(started[4])
        in_flight.append((started, segs, tag))
        return started[4]

    def reduce_finish(after):
        if in_flight:
            started, segs, tag = in_flight.pop()
            mine, landed = _exchange_wait(started, after, False, f"rs_wait_{tag}")
            grads.update(_unpack_shard_grads(_sum_exchange(mine, landed, me, f"rs_sum_{tag}"), segs))

    def ffn_backward(f, i, x_in, a, u, gain, dout):
        dx, h, dob, y, da, du, dg = _ffn_bwd(x_in, dout, a, u, gain, w[(f + "_w_gate", i)], w[(f + "_w_up", i)],
                                             w[(f + "_w_down", i)], f"{f}_bwd_{i}")
        gr[(f + "_w_gate", i)] = _tn_matmul(da, h, f"{f}_dgate_{i}")
        gr[(f + "_w_up", i)] = _tn_matmul(du, h, f"{f}_dup_{i}")
        gr[(f + "_w_down", i)] = _tn_matmul(y, dob, f"{f}_ddown_{i}")
        small[f + "_norm"][i] = jnp.sum(dg, axis=0)
        return dx

    token = jnp.zeros((8, 128), F32)
    for i in reversed(range(DEPTH)):
        j = i // 2
        st = saved[i]
        dcur = ffn_backward("ffn2", i, st["x2"], st["a2"], st["u2"], tied(ffn2_norm[i:i + 1], token), dcur)
        if i % 2 == 0:
            dcur, dpw, dsc, dg = _pool_bwd(st["x1"], dcur, mix_norm[i:i + 1], w[("pool_w", j)], pool_scale[j:j + 1],
                                           f"pool_bwd_{i}")
            gr[("pool_w", j)] = dpw
            small["pool_scale"][j] = jnp.sum(dsc, axis=0)
            small["mix_norm"][i] = jnp.sum(dg, axis=0)
        else:
            dob, dpo, delta = _mla_out_bwd(dcur, st["o"], w[("mla_w_out", j)], f"mla_out_bwd_{i}")
            gr[("mla_w_out", j)] = _tn_matmul(st["o"], dob, f"mla_dout_{i}")
            dk, dv, dq = _flash_bwd(st["q"], st["k"], st["v"], dpo, st["lse"], delta, f"flash_bwd_{i}")
            (dcur, h, dlat, cqn, dqp, ckvn, dkv, dg, dgq, dgkv, dghq, dghk) = _mla_pre_bwd(
                st["x1"], dcur, dq, dk, dv, mix_norm[i:i + 1], *mla_weights(j), f"mla_pre_bwd_{i}")
            gr[("mla_w_in", j)] = _tn_matmul(dlat, h, f"mla_din_{i}")
            gr[("mla_w_q_up", j)] = _tn_matmul(dqp, cqn, f"mla_dqup_{i}")
            gr[("mla_w_kv_up", j)] = _tn_matmul(dkv, ckvn, f"mla_dkvup_{i}")
            small["mix_norm"][i] = jnp.sum(dg, axis=0)
            small["mla_q_norm"][j] = _pad_lanes(jnp.sum(dgq, axis=0)[None], D_MODEL)[0]
            small["mla_kv_norm"][j] = _pad_lanes(jnp.sum(dgkv, axis=0)[None], D_MODEL)[0]
            small["mla_q_head_norm"][j] = _pad_lanes(jnp.sum(dghq, axis=0)[None], D_MODEL)[0]
            small["mla_k_head_norm"][j] = _pad_lanes(jnp.sum(dghk, axis=0)[None], D_MODEL)[0]
        token = reduce_start(_ffn_segments("ffn2", i) + _mixer_segments(i), f"a{i}")
        if i > 0:
            dcur = ffn_backward("ffn1", i, st["x0"], st["a1"], st["u1"], tied(ffn1_norm[i:i + 1], token), dcur)
            token = reduce_start(_ffn_segments("ffn1", i), f"b{i}")
    st = saved[0]
    dcur, h, dob, y, da, du, dg = _ffn_bwd(st["x0"], dcur, st["a1"], st["u1"], tied(ffn1_norm[0:1], token),
                                           w[("ffn1_w_gate", 0)], w[("ffn1_w_up", 0)], w[("ffn1_w_down", 0)],
                                           "ffn1_bwd_0")
    small["ffn1_norm"][0] = jnp.sum(dg, axis=0)
    grad_x = dcur[None]
    small_order = ("ffn1_norm", "mix_norm", "ffn2_norm", "pool_scale", "mla_q_norm", "mla_kv_norm",
                   "mla_q_head_norm", "mla_k_head_norm")
    rows = [r for k in small_order for r in small[k]]
    rows.append(jnp.zeros((D_MODEL,), F32).at[0].set(loss_part))
    rows.append(jnp.zeros((D_MODEL,), F32))
    small_sum = _sum_parts(_all_gather(jnp.stack(rows), "small_all_gather"), "small_sum")
    loss = small_sum[SM_ROWS - 2, 0]
    token = small_sum
    for seg, lhs, rhs in zip(_ffn_segments("ffn1", 0), (da, du, y), (h, h, dob)):
        gr[seg[:2]] = _tn_matmul(lhs, rhs, f"ffn1_d{seg[0][7:]}_0", behind=token)
        token = reduce_start([seg], f"b0_{seg[0][7:]}")
    last = "ffn1_w_down"

    counts = {k: (DEPTH // 2 if k.startswith(("mla", "pool")) else DEPTH) for k in WEIGHTS}

    def stacked(k):
        return jnp.stack([grads[(k, idx)] for idx in range(counts[k])])

    grads.update({k: stacked(k) for k in WEIGHTS if (k, 0) in grads and k != last})
    off = 0
    for k in small_order:
        cnt = len(small[k])
        g = small_sum[off:off + cnt]
        off += cnt
        if k == "mla_q_norm":
            g = lax.dynamic_slice_in_dim(g[:, :Q_LORA], dev * (Q_LORA // N_DEV), Q_LORA // N_DEV, axis=1)
        elif k == "mla_kv_norm":
            g = lax.dynamic_slice_in_dim(g[:, :KV_LORA], dev * (KV_LORA // N_DEV), KV_LORA // N_DEV, axis=1)
        elif k in ("mla_q_head_norm", "mla_k_head_norm"):
            g = g[:, :QK_HEAD]
        grads[k] = g

    deltas, new_m, new_v = {}, {}, {}

    def update(k, behind):
        to_view = _t if k in COLUMN_SHARDED else (lambda a: a)
        shape = to_view(p[k]).shape
        view = (-1, shape[-1])
        d, mn, vn = _adamw(to_view(p[k]).reshape(view), grads[k].reshape(view), to_view(moments_m[k]).reshape(view),
                           to_view(moments_v[k]).reshape(view), behind, "adamw_" + k)
        deltas[k], new_m[k], new_v[k] = (to_view(a.reshape(shape)) for a in (d, mn, vn))
        grads[k] = to_view(grads[k].reshape(shape))
        return d

    done = token
    for k in WEIGHTS:
        if k != last:
            done = update(k, done)
    reduce_finish(done)
    grads[last] = stacked(last)
    update(last, done)

    return (loss, grad_x, *[grads[k] for k in WEIGHTS], *[deltas[k] for k in WEIGHTS],
            *[new_m[k] for k in WEIGHTS], *[new_v[k] for k in WEIGHTS])
```

```python
import functools

import jax
import jax.numpy as jnp
from jax import lax
from jax.experimental import pallas as pl
from jax.experimental.pallas import tpu as pltpu

F32 = jnp.float32
BF16 = jnp.bfloat16

D_MODEL = 1024
DEPTH = 4
D_FF = 2816
POOL_WINDOWS = (2, 4, 8, 16)
POOL_GROUP = 256
POOL_HALO = 16
N_HEADS = 8
QK_NOPE = 128
QK_ROPE = 64
QK_HEAD = 192
V_HEAD = 128
Q_LORA = 768
KV_LORA = 256
ROPE_THETA = 10000.0
EPS = 1e-6
FFN_HALF = 0.5
ADAM_LR = 0.001
ADAM_B1 = 0.9
ADAM_B2 = 0.999
ADAM_EPS = 1e-08
ADAM_WD = 0.01
ADAM_STEP = 10

N_DEV = 8
HEAD_PAD = 256
LAT_PAD = 1152
VT_ROWS = 144
LOG2_E = 1.4426950408889634
ATTN_SCALE = QK_HEAD ** -0.5
LOGIT_SCALE = ATTN_SCALE * LOG2_E
V7X_VMEM_LIMIT = 56 * 1024 * 1024
FF_CHUNK = 256
SM_ROWS = 24

NAMES = ['x', 'positions', 'ffn1_norm', 'ffn1_w_gate', 'ffn1_w_up', 'ffn1_w_down', 'mix_norm', 'pool_w',
         'pool_scale', 'mla_w_in', 'mla_q_norm', 'mla_w_q_up', 'mla_kv_norm', 'mla_w_kv_up', 'mla_q_head_norm',
         'mla_k_head_norm', 'mla_w_out', 'ffn2_norm', 'ffn2_w_gate', 'ffn2_w_up', 'ffn2_w_down']
WEIGHTS = NAMES[2:]


def _tiles(n):
    return dict(ffn_fwd=min(512, n), ffn_bwd=min(256, n), fwd_k=min(512, n // 2), bwd_q=min(512, n // 2),
                mla_bwd=min(256, n), pool=min(512, n), tn=min(2048, n), rows=min(1024, n))


def _params(*sem):
    return pltpu.CompilerParams(dimension_semantics=sem, vmem_limit_bytes=V7X_VMEM_LIMIT)


def _dot(a, b):
    return jnp.dot(a, b, preferred_element_type=F32)


def _dot_nt(a, b):
    return lax.dot_general(a, b, (((1,), (1,)), ((), ())), preferred_element_type=F32)


def _dot_tn(a, b):
    return lax.dot_general(a, b, (((0,), (0,)), ((), ())), preferred_element_type=F32)


def _rowsum8(v):
    rows, w = v.shape
    return jnp.sum(v.reshape(rows // 8, 8, w), axis=0)


def _sigmoid(a):
    return 1.0 / (1.0 + jnp.exp(-a))


def _lane_sum(v):
    f = v[:, :128]
    for t in range(1, v.shape[1] // 128):
        f = f + v[:, t * 128:(t + 1) * 128]
    hi = f.astype(BF16)
    lo = (f - hi.astype(F32)).astype(BF16)
    return _dot(jnp.concatenate([hi, lo], axis=1), jnp.ones((256, 128), BF16))


def _by_row(v, r):
    return jnp.concatenate([v[:, t * 128:(t + 1) * 128] * r for t in range(v.shape[1] // 128)], axis=1)


def _rms_fwd(x, width=None):
    width = x.shape[-1] if width is None else width
    r = lax.rsqrt(_lane_sum(x * x) * (1.0 / width) + EPS)
    return _by_row(x, r), r


def _rms_bwd(dy, xhat, r, gain, width=None):
    width = xhat.shape[-1] if width is None else width
    t = dy * gain
    return _by_row(t - _by_row(xhat, _lane_sum(t * xhat) * (1.0 / width)), r)


def _full(shape):
    return pl.BlockSpec(shape, lambda *_: (0,) * len(shape))


def _load_weights(srcs, dsts, sems):
    copies = [pltpu.make_async_copy(s, d, sems.at[i]) for i, (s, d) in enumerate(zip(srcs, dsts))]
    for cp in copies:
        cp.start()
    for cp in copies:
        cp.wait()


def _ffn_fwd(x, gain, wg_t, wu_t, wd, name):
    n = x.shape[0]
    tm = _tiles(n)["ffn_fwd"]

    def body(x_ref, g_ref, wg_hbm, wu_hbm, wd_hbm, out_ref, a_ref, u_ref, wg_v, wu_v, wd_v, sems):
        @pl.when(pl.program_id(0) == 0)
        def _():
            _load_weights((wg_hbm, wu_hbm, wd_hbm), (wg_v, wu_v, wd_v), sems)

        xt = x_ref[...]
        xhat, _ = _rms_fwd(xt)
        h = (xhat * g_ref[...]).astype(BF16)
        acc = jnp.zeros((tm, D_MODEL), F32)
        for c in range(D_FF // FF_CHUNK):
            sl = pl.ds(c * FF_CHUNK, FF_CHUNK)
            a = _dot_nt(h, wg_v[sl, :])
            u = _dot_nt(h, wu_v[sl, :])
            a_ref[:, sl] = a.astype(BF16)
            u_ref[:, sl] = u.astype(BF16)
            y = (a * _sigmoid(a) * u).astype(BF16)
            acc = acc + _dot(y, wd_v[sl, :])
        out_ref[...] = xt + FFN_HALF * acc

    any_spec = pl.BlockSpec(memory_space=pl.ANY)
    return pl.pallas_call(
        body, name=name, grid=(n // tm,),
        in_specs=[pl.BlockSpec((tm, D_MODEL), lambda i: (i, 0)), _full((1, D_MODEL)), any_spec, any_spec, any_spec],
        out_specs=[pl.BlockSpec((tm, D_MODEL), lambda i: (i, 0)), pl.BlockSpec((tm, D_FF), lambda i: (i, 0)),
                   pl.BlockSpec((tm, D_FF), lambda i: (i, 0))],
        out_shape=[jax.ShapeDtypeStruct((n, D_MODEL), F32), jax.ShapeDtypeStruct((n, D_FF), BF16),
                   jax.ShapeDtypeStruct((n, D_FF), BF16)],
        scratch_shapes=[pltpu.VMEM((D_FF, D_MODEL), BF16)] * 3 + [pltpu.SemaphoreType.DMA((3,))],
        compiler_params=_params("arbitrary"),
    )(x, gain, wg_t, wu_t, wd)


def _ffn_bwd(x, dout, a, u, gain, wg_t, wu_t, wd, name):
    n = x.shape[0]
    tm = _tiles(n)["ffn_bwd"]

    def body(x_ref, do_ref, a_ref, u_ref, g_ref, wg_hbm, wu_hbm, wd_hbm,
             dx_ref, h_ref, dob_ref, y_ref, da_ref, du_ref, dg_ref, wg_v, wu_v, wd_v, sems):
        @pl.when(pl.program_id(0) == 0)
        def _():
            _load_weights((wg_hbm, wu_hbm, wd_hbm), (wg_v, wu_v, wd_v), sems)
            dg_ref[...] = jnp.zeros_like(dg_ref)

        xt = x_ref[...]
        g = g_ref[...]
        xhat, r = _rms_fwd(xt)
        h_ref[...] = (xhat * g).astype(BF16)
        dout = do_ref[...]
        dob = (FFN_HALF * dout).astype(BF16)
        dob_ref[...] = dob
        for c in range(D_FF // FF_CHUNK):
            sl = pl.ds(c * FF_CHUNK, FF_CHUNK)
            dy = _dot_nt(dob, wd_v[sl, :])
            av = a_ref[:, sl].astype(F32)
            uv = u_ref[:, sl].astype(F32)
            s = _sigmoid(av)
            silu = av * s
            y_ref[:, sl] = (silu * uv).astype(BF16)
            du_ref[:, sl] = (dy * silu).astype(BF16)
            da_ref[:, sl] = (dy * uv * (s * (1.0 + av * (1.0 - s)))).astype(BF16)
        dh = _dot(da_ref[...], wg_v[...]) + _dot(du_ref[...], wu_v[...])
        dg_ref[...] += _rowsum8(dh * xhat)
        dx_ref[...] = dout + _rms_bwd(dh, xhat, r, g)

    any_spec = pl.BlockSpec(memory_space=pl.ANY)
    row_d = pl.BlockSpec((tm, D_MODEL), lambda i: (i, 0))
    row_f = pl.BlockSpec((tm, D_FF), lambda i: (i, 0))
    return pl.pallas_call(
        body, name=name, grid=(n // tm,),
        in_specs=[row_d, row_d, row_f, row_f, _full((1, D_MODEL)), any_spec, any_spec, any_spec],
        out_specs=[row_d, row_d, row_d, row_f, row_f, row_f, _full((8, D_MODEL))],
        out_shape=[jax.ShapeDtypeStruct((n, D_MODEL), F32), jax.ShapeDtypeStruct((n, D_MODEL), BF16),
                   jax.ShapeDtypeStruct((n, D_MODEL), BF16), jax.ShapeDtypeStruct((n, D_FF), BF16),
                   jax.ShapeDtypeStruct((n, D_FF), BF16), jax.ShapeDtypeStruct((n, D_FF), BF16),
                   jax.ShapeDtypeStruct((8, D_MODEL), F32)],
        scratch_shapes=[pltpu.VMEM((D_FF, D_MODEL), BF16)] * 3 + [pltpu.SemaphoreType.DMA((3,))],
        compiler_params=_params("arbitrary"),
    )(x, dout, a, u, gain, wg_t, wu_t, wd)


def _tn_matmul(a, b, name, behind=None):
    n, fa = a.shape
    db = b.shape[1]
    tk = _tiles(n)["tn"]
    tf = fa // 2 if (fa // 2) % 128 == 0 and fa > 1024 else fa
    behind = jnp.zeros((8, 128), F32) if behind is None else behind

    def body(a_ref, b_ref, behind_ref, o_ref):
        @pl.when(pl.program_id(1) == 0)
        def _():
            o_ref[...] = jnp.zeros_like(o_ref)

        o_ref[...] += _dot_tn(a_ref[...], b_ref[...])

    return pl.pallas_call(
        body, name=name, grid=(fa // tf, n // tk),
        in_specs=[pl.BlockSpec((tk, tf), lambda i, k: (k, i)), pl.BlockSpec((tk, db), lambda i, k: (k, 0)),
                  pl.BlockSpec(memory_space=pl.ANY)],
        out_specs=pl.BlockSpec((tf, db), lambda i, k: (i, 0)),
        out_shape=jax.ShapeDtypeStruct((fa, db), F32),
        compiler_params=_params("arbitrary", "arbitrary"),
    )(a, b, behind)


def _loss_head(y, target, name):
    n = y.shape[0]
    tm = _tiles(n)["rows"]

    def body(y_ref, t_ref, d_ref, acc_ref):
        @pl.when(pl.program_id(0) == 0)
        def _():
            acc_ref[...] = jnp.zeros_like(acc_ref)

        d = y_ref[...] - t_ref[...]
        d_ref[...] = d * (1.0 / D_MODEL)
        acc_ref[...] += _rowsum8(d * d)

    row = pl.BlockSpec((tm, D_MODEL), lambda i: (i, 0))
    return pl.pallas_call(
        body, name=name, grid=(n // tm,), in_specs=[row, row], out_specs=[row, _full((8, D_MODEL))],
        out_shape=[jax.ShapeDtypeStruct((n, D_MODEL), F32), jax.ShapeDtypeStruct((8, D_MODEL), F32)],
        compiler_params=_params("arbitrary"),
    )(y, target)


def _window_sum(v, w, rows, forward):
    s = v
    sh = 1
    while sh < w:
        s = s + pltpu.roll(s, (rows - sh) if forward else sh, 0)
        sh *= 2
    return s


def _pool_fwd(x, gain, w, scale, name):
    n = x.shape[0]
    tm = _tiles(n)["pool"]
    hb = tm // POOL_HALO
    rows = tm + POOL_HALO

    def body(x_ref, xh_ref, g_ref, w_ref, sc_ref, out_ref):
        i = pl.program_id(0)
        xt = x_ref[...]
        e = jnp.concatenate([xh_ref[...], xt], axis=0)
        xhat, _ = _rms_fwd(e)
        row = lax.broadcasted_iota(jnp.int32, (rows, 1), 0)
        hn = jnp.where((row >= POOL_HALO) | (i > 0), xhat * g_ref[...], 0.0)
        t_glob = i * tm + row - POOL_HALO
        outs = []
        for gi, win in enumerate(POOL_WINDOWS):
            ug = hn[:, gi * POOL_GROUP:(gi + 1) * POOL_GROUP]
            cnt = jnp.maximum(jnp.minimum(t_glob + 1, win), 1).astype(F32)
            pooled = (_window_sum(ug, win, rows, False) / cnt - ug)[POOL_HALO:]
            outs.append(_dot(pooled.astype(BF16), w_ref[gi]))
        out_ref[...] = xt + jnp.concatenate(outs, axis=1) * sc_ref[...]

    return pl.pallas_call(
        body, name=name, grid=(n // tm,),
        in_specs=[pl.BlockSpec((tm, D_MODEL), lambda i: (i, 0)),
                  pl.BlockSpec((POOL_HALO, D_MODEL), lambda i: (jnp.maximum(i * hb - 1, 0), 0)),
                  _full((1, D_MODEL)), _full((4, POOL_GROUP, POOL_GROUP)), _full((1, D_MODEL))],
        out_specs=pl.BlockSpec((tm, D_MODEL), lambda i: (i, 0)),
        out_shape=jax.ShapeDtypeStruct((n, D_MODEL), F32),
        compiler_params=_params("arbitrary"),
    )(x, x, gain, w, scale)


def _pool_bwd(x, dout, gain, w, scale, name):
    n = x.shape[0]
    tm = _tiles(n)["pool"]
    hb = tm // POOL_HALO
    rows = tm + POOL_HALO
    nt = n // tm

    def body(x_ref, xh_ref, do_ref, doh_ref, g_ref, w_ref, sc_ref, dx_ref, dw_ref, dsc_ref, dg_ref):
        i = pl.program_id(0)

        @pl.when(i == 0)
        def _():
            dw_ref[...] = jnp.zeros_like(dw_ref)
            dsc_ref[...] = jnp.zeros_like(dsc_ref)
            dg_ref[...] = jnp.zeros_like(dg_ref)

        xt = x_ref[...]
        g = g_ref[...]
        e = jnp.concatenate([xh_ref[...], xt], axis=0)
        xhat_e, r_e = _rms_fwd(e)
        row = lax.broadcasted_iota(jnp.int32, (rows, 1), 0)
        hn = jnp.where((row >= POOL_HALO) | (i > 0), xhat_e * g, 0.0)
        t_prev = i * tm + row - POOL_HALO
        t_next = i * tm + row
        dout = do_ref[...]
        dt = jnp.concatenate([dout, doh_ref[...]], axis=0)
        dt = jnp.where((row < tm) | (i < nt - 1), dt, 0.0)
        dyr = dt * sc_ref[...]
        dus, dscs = [], []
        for gi, win in enumerate(POOL_WINDOWS):
            lanes = slice(gi * POOL_GROUP, (gi + 1) * POOL_GROUP)
            ug = hn[:, lanes]
            cnt = jnp.maximum(jnp.minimum(t_prev + 1, win), 1).astype(F32)
            pooled = (_window_sum(ug, win, rows, False) / cnt - ug)[POOL_HALO:].astype(BF16)
            yraw = _dot(pooled, w_ref[gi])
            dscs.append(_rowsum8(dout[:, lanes] * yraw))
            dyr_b = dyr[:, lanes].astype(BF16)
            dw_ref[gi] += _dot_tn(pooled, dyr_b[:tm])
            dpool = _dot_nt(dyr_b, w_ref[gi])
            cnt2 = jnp.minimum(t_next + 1, win).astype(F32)
            dus.append((_window_sum(dpool / cnt2, win, rows, True) - dpool)[:tm])
        dsc_ref[...] += jnp.concatenate(dscs, axis=1)
        dh = jnp.concatenate(dus, axis=1)
        xhat = xhat_e[POOL_HALO:]
        dg_ref[...] += _rowsum8(dh * xhat)
        dx_ref[...] = dout + _rms_bwd(dh, xhat, r_e[POOL_HALO:], g)

    row_d = pl.BlockSpec((tm, D_MODEL), lambda i: (i, 0))
    prev_h = pl.BlockSpec((POOL_HALO, D_MODEL), lambda i: (jnp.maximum(i * hb - 1, 0), 0))
    next_h = pl.BlockSpec((POOL_HALO, D_MODEL), lambda i: (jnp.minimum((i + 1) * hb, n // POOL_HALO - 1), 0))
    return pl.pallas_call(
        body, name=name, grid=(nt,),
        in_specs=[row_d, prev_h, row_d, next_h, _full((1, D_MODEL)), _full((4, POOL_GROUP, POOL_GROUP)),
                  _full((1, D_MODEL))],
        out_specs=[row_d, _full((4, POOL_GROUP, POOL_GROUP)), _full((8, D_MODEL)), _full((8, D_MODEL))],
        out_shape=[jax.ShapeDtypeStruct((n, D_MODEL), F32), jax.ShapeDtypeStruct((4, POOL_GROUP, POOL_GROUP), F32),
                   jax.ShapeDtypeStruct((8, D_MODEL), F32), jax.ShapeDtypeStruct((8, D_MODEL), F32)],
        compiler_params=_params("arbitrary"),
    )(x, x, dout, dout, gain, w, scale)


def _rope(v, cos, sin_signed):
    lo, hi = v[:, :128], v[:, 128:]
    lane = lax.broadcasted_iota(jnp.int32, hi.shape, 1)
    swapped = jnp.where(lane < 32, pltpu.roll(hi, 96, 1), pltpu.roll(hi, 32, 1))
    return jnp.concatenate([lo, hi * cos + swapped * sin_signed], axis=1)


def _rope_bwd(gr, cos, sin_signed):
    lo, hi = gr[:, :128], gr[:, 128:]
    t = hi * sin_signed
    lane = lax.broadcasted_iota(jnp.int32, hi.shape, 1)
    swapped = jnp.where(lane < 32, pltpu.roll(t, 96, 1), pltpu.roll(t, 32, 1))
    return jnp.concatenate([lo, hi * cos + swapped], axis=1)


def _mla_latents(h, win_ref):
    cq = _dot_nt(h, win_ref[0:Q_LORA, :])
    ckv = _dot_nt(h, win_ref[Q_LORA:Q_LORA + KV_LORA, :])
    kpe = _dot_nt(h, win_ref[Q_LORA + KV_LORA:LAT_PAD, :])
    return cq, ckv, kpe


def _mla_pre_fwd(x, gain, win, gq, gkv, wq, wkv, ghq, ghk, cos, sin_signed, name):
    n = x.shape[0]
    tm = _tiles(n)["fwd_k"]

    def body(x_ref, g_ref, win_ref, gq_ref, gkv_ref, wq_ref, wkv_ref, ghq_ref, ghk_ref, c_ref, s_ref,
             q_ref, k_ref, v_ref, vt_ref):
        xhat, _ = _rms_fwd(x_ref[...])
        h = (xhat * g_ref[...]).astype(BF16)
        cq, ckv, kpe = _mla_latents(h, win_ref)
        cqn = (_rms_fwd(cq)[0] * gq_ref[...]).astype(BF16)
        ckvn = (_rms_fwd(ckv)[0] * gkv_ref[...]).astype(BF16)
        cos, sn = c_ref[...], s_ref[...]
        for hd in range(N_HEADS):
            rws = pl.ds(hd * HEAD_PAD, HEAD_PAD)
            qh = _dot_nt(cqn, wq_ref[rws, :])
            qn = _rms_fwd(qh, QK_HEAD)[0] * ghq_ref[...]
            q_ref[hd] = (_rope(qn, cos, sn) * LOGIT_SCALE).astype(BF16)
            kvh = _dot_nt(ckvn, wkv_ref[rws, :])
            kpre = jnp.concatenate([kvh[:, :QK_NOPE], kpe], axis=1)
            kn = _rms_fwd(kpre, QK_HEAD)[0] * ghk_ref[...]
            k_ref[hd] = _rope(kn, cos, sn).astype(BF16)
            vh = kvh[:, QK_NOPE:]
            v_ref[hd] = vh.astype(BF16)
            vt_ref[hd, 0] = jnp.concatenate([vh.T, jnp.ones((VT_ROWS - V_HEAD, tm), F32)], axis=0).astype(BF16)

    row = lambda w: pl.BlockSpec((tm, w), lambda i: (i, 0))
    head = lambda w: pl.BlockSpec((N_HEADS, tm, w), lambda i: (0, i, 0))
    return pl.pallas_call(
        body, name=name, grid=(n // tm,),
        in_specs=[row(D_MODEL), _full((1, D_MODEL)), _full((LAT_PAD, D_MODEL)), _full((1, Q_LORA)),
                  _full((1, KV_LORA)), _full((N_HEADS * HEAD_PAD, Q_LORA)), _full((N_HEADS * HEAD_PAD, KV_LORA)),
                  _full((1, HEAD_PAD)), _full((1, HEAD_PAD)), row(128), row(128)],
        out_specs=[head(HEAD_PAD), head(HEAD_PAD), head(V_HEAD),
                   pl.BlockSpec((N_HEADS, 1, VT_ROWS, tm), lambda i: (0, i, 0, 0))],
        out_shape=[jax.ShapeDtypeStruct((N_HEADS, n, HEAD_PAD), BF16), jax.ShapeDtypeStruct((N_HEADS, n, HEAD_PAD), BF16),
                   jax.ShapeDtypeStruct((N_HEADS, n, V_HEAD), BF16),
                   jax.ShapeDtypeStruct((N_HEADS, n // tm, VT_ROWS, tm), BF16)],
        compiler_params=_params("arbitrary"),
    )(x, gain, win, gq, gkv, wq, wkv, ghq, ghk, cos, sin_signed)


def _flash_fwd(q, k, vt, name):
    n = q.shape[1]
    tk = _tiles(n)["fwd_k"]
    tq = 2 * tk
    nq = n // tq

    def body(q_ref, k_ref, vt_ref, o_ref, lse_ref, s_scr, m_scr, acc_scr):
        i = pl.program_id(1)
        qi = q_ref[0]

        def scores(j, slot):
            s_scr[slot] = _dot_nt(k_ref[0, pl.ds(pl.multiple_of(j * tk, tk), tk), :], qi)

        def update(j, slot, diagonal=None):
            s = s_scr[slot]
            if diagonal is not None:
                krow = lax.broadcasted_iota(jnp.int32, (tk, tq), 0) + diagonal * tk
                qcol = lax.broadcasted_iota(jnp.int32, (tk, tq), 1)
                s = jnp.where(krow <= qcol, s, -jnp.inf)
            m = m_scr[...]
            m_new = jnp.maximum(m, jnp.max(s, axis=0, keepdims=True))
            p = jnp.exp2(s - m_new).astype(BF16)
            acc_scr[...] = jnp.exp2(m - m_new) * acc_scr[...] + _dot(vt_ref[0, j], p)
            m_scr[...] = m_new

        m_scr[...] = jnp.full((1, tq), -jnp.inf, F32)
        acc_scr[...] = jnp.zeros((VT_ROWS, tq), F32)
        scores(0, 0)

        def pair(jj, carry):
            scores(2 * jj + 1, 1)
            update(2 * jj, 0)
            scores(2 * jj + 2, 0)
            update(2 * jj + 1, 1)
            return carry

        lax.fori_loop(0, i, pair, 0)
        scores(2 * i + 1, 1)
        update(2 * i, 0, diagonal=0)
        update(2 * i + 1, 1, diagonal=1)
        l = acc_scr[V_HEAD:V_HEAD + 1, :]
        o_ref[...] = (acc_scr[0:V_HEAD, :] / l).T.astype(BF16)
        lse_ref[0, 0] = m_scr[...] + jnp.log2(l)

    return pl.pallas_call(
        body, name=name, grid=(N_HEADS, nq),
        in_specs=[pl.BlockSpec((1, tq, HEAD_PAD), lambda h, i: (h, i, 0)),
                  pl.BlockSpec((1, n, HEAD_PAD), lambda h, i: (h, 0, 0)),
                  pl.BlockSpec((1, n // tk, VT_ROWS, tk), lambda h, i: (h, 0, 0, 0))],
        out_specs=[pl.BlockSpec((tq, V_HEAD), lambda h, i: (i, h)),
                   pl.BlockSpec((1, 1, 1, tq), lambda h, i: (h, i, 0, 0))],
        out_shape=[jax.ShapeDtypeStruct((n, N_HEADS * V_HEAD), BF16), jax.ShapeDtypeStruct((N_HEADS, nq, 1, tq), F32)],
        scratch_shapes=[pltpu.VMEM((2, tk, tq), F32), pltpu.VMEM((1, tq), F32), pltpu.VMEM((VT_ROWS, tq), F32)],
        compiler_params=_params("arbitrary", "arbitrary"),
    )(q, k, vt)


def _mla_post_fwd(o, x, wout, name):
    n = x.shape[0]
    tm = _tiles(n)["rows"]

    def body(o_ref, x_ref, w_ref, out_ref):
        out_ref[...] = x_ref[...] + _dot(o_ref[...], w_ref[...])

    row = pl.BlockSpec((tm, D_MODEL), lambda i: (i, 0))
    return pl.pallas_call(
        body, name=name, grid=(n // tm,), in_specs=[row, row, _full((D_MODEL, D_MODEL))], out_specs=row,
        out_shape=jax.ShapeDtypeStruct((n, D_MODEL), F32), compiler_params=_params("arbitrary"),
    )(o, x, wout)


def _mla_out_bwd(dout, o, wout, name):
    n = dout.shape[0]
    t = _tiles(n)["bwd_q"]
    nq = n // t

    def body(do_ref, o_ref, w_ref, dob_ref, dpo_ref, dl_ref):
        dob = do_ref[...].astype(BF16)
        dob_ref[...] = dob
        dpo = _dot_nt(dob, w_ref[...])
        dpo_ref[...] = dpo.astype(BF16)
        ov = o_ref[...].astype(F32)
        for hd in range(N_HEADS):
            lanes = slice(hd * V_HEAD, (hd + 1) * V_HEAD)
            prod = dpo[:, lanes] * ov[:, lanes]
            dl_ref[hd, 0] = jnp.sum(prod.T, axis=0, keepdims=True)

    row = pl.BlockSpec((t, D_MODEL), lambda i: (i, 0))
    return pl.pallas_call(
        body, name=name, grid=(nq,), in_specs=[row, row, _full((D_MODEL, D_MODEL))],
        out_specs=[row, row, pl.BlockSpec((N_HEADS, 1, 1, t), lambda i: (0, i, 0, 0))],
        out_shape=[jax.ShapeDtypeStruct((n, D_MODEL), BF16), jax.ShapeDtypeStruct((n, D_MODEL), BF16),
                   jax.ShapeDtypeStruct((N_HEADS, nq, 1, t), F32)],
        compiler_params=_params("arbitrary"),
    )(dout, o, wout)


def _flash_bwd(q, k, v, dpo, lse, delta, name):
    n = q.shape[1]
    tq = _tiles(n)["bwd_q"]
    tk = 2 * tq
    nk = n // tk
    nqb = n // tq

    def body(k_ref, v_ref, q_ref, do_ref, lse_ref, dl_ref, dk_ref, dv_ref, dq_hbm,
             dq_acc, s_scr, dp_scr, dk_acc, dv_acc, sem):
        h = pl.program_id(0)
        j = pl.program_id(1)

        @pl.when(j == 0)
        def _():
            dq_acc[...] = jnp.zeros_like(dq_acc)

        dk_acc[...] = jnp.zeros_like(dk_acc)
        dv_acc[...] = jnp.zeros_like(dv_acc)
        kj = k_ref[0]
        vj = v_ref[0]
        npairs = nk - 1 - j

        def block(t):
            return jnp.where(t < 2 * npairs, 2 * j + 2 + t, 2 * j + (t - 2 * npairs))

        def scores(i, slot):
            rws = pl.ds(pl.multiple_of(i * tq, tq), tq)
            s_scr[slot] = _dot_nt(kj, q_ref[0, rws, :])
            dp_scr[slot] = _dot_nt(vj, do_ref[rws, :])

        def update(i, slot, diagonal=None):
            rws = pl.ds(pl.multiple_of(i * tq, tq), tq)
            p = jnp.exp2(s_scr[slot] - lse_ref[0, i])
            if diagonal is not None:
                krow = lax.broadcasted_iota(jnp.int32, (tk, tq), 0)
                qcol = lax.broadcasted_iota(jnp.int32, (tk, tq), 1) + diagonal * tq
                p = jnp.where(krow <= qcol, p, 0.0)
            dv_acc[...] += _dot(p.astype(BF16), do_ref[rws, :])
            ds = (p * (dp_scr[slot] - dl_ref[0, i])).astype(BF16)
            dk_acc[...] += _dot(ds, q_ref[0, rws, :])
            dq_acc[rws, :] += _dot_tn(ds, kj)

        scores(block(0), 0)

        def pair(jj, carry):
            scores(block(2 * jj + 1), 1)
            update(block(2 * jj), 0)
            scores(block(2 * jj + 2), 0)
            update(block(2 * jj + 1), 1)
            return carry

        lax.fori_loop(0, npairs, pair, 0)
        scores(2 * j + 1, 1)
        update(2 * j, 0, diagonal=0)
        update(2 * j + 1, 1, diagonal=1)
        dk_ref[0] = dk_acc[...] * (ATTN_SCALE / LOGIT_SCALE)
        dv_ref[0] = dv_acc[...]

        @pl.when(j == nk - 1)
        def _():
            dq_acc[...] = dq_acc[...] * ATTN_SCALE
            cp = pltpu.make_async_copy(dq_acc, dq_hbm.at[h], sem)
            cp.start()
            cp.wait()

    resident = dict(pipeline_mode=pl.Buffered(1))
    return pl.pallas_call(
        body, name=name, grid=(N_HEADS, nk),
        in_specs=[pl.BlockSpec((1, tk, HEAD_PAD), lambda h, j: (h, j, 0)),
                  pl.BlockSpec((1, tk, V_HEAD), lambda h, j: (h, j, 0)),
                  pl.BlockSpec((1, n, HEAD_PAD), lambda h, j: (h, 0, 0), **resident),
                  pl.BlockSpec((n, V_HEAD), lambda h, j: (0, h), **resident),
                  pl.BlockSpec((1, nqb, 1, tq), lambda h, j: (h, 0, 0, 0)),
                  pl.BlockSpec((1, nqb, 1, tq), lambda h, j: (h, 0, 0, 0))],
        out_specs=[pl.BlockSpec((1, tk, HEAD_PAD), lambda h, j: (h, j, 0)),
                   pl.BlockSpec((1, tk, V_HEAD), lambda h, j: (h, j, 0)),
                   pl.BlockSpec(memory_space=pl.ANY)],
        out_shape=[jax.ShapeDtypeStruct((N_HEADS, n, HEAD_PAD), F32), jax.ShapeDtypeStruct((N_HEADS, n, V_HEAD), F32),
                   jax.ShapeDtypeStruct((N_HEADS, n, HEAD_PAD), F32)],
        scratch_shapes=[pltpu.VMEM((n, HEAD_PAD), F32), pltpu.VMEM((2, tk, tq), F32), pltpu.VMEM((2, tk, tq), F32),
                        pltpu.VMEM((tk, HEAD_PAD), F32), pltpu.VMEM((tk, V_HEAD), F32), pltpu.SemaphoreType.DMA(())],
        compiler_params=_params("arbitrary", "arbitrary"),
    )(k, v, q, dpo, lse, delta)


def _mla_pre_bwd(x, dout, dq, dk, dv, gain, win, gq, gkv, wq, wkv, ghq, ghk, cos, sin_signed, name):
    n = x.shape[0]
    tm = _tiles(n)["mla_bwd"]
    hw = N_HEADS * HEAD_PAD

    def body(x_ref, do_ref, dq_ref, dk_ref, dv_ref, g_ref, win_ref, gq_ref, gkv_ref, wq_ref, wkv_ref, ghq_ref, ghk_ref,
             c_ref, s_ref, dx_ref, h_ref, dlat_ref, cqn_ref, dqp_ref, ckvn_ref, dkv_ref,
             dg_ref, dgq_ref, dgkv_ref, dghq_ref, dghk_ref):
        @pl.when(pl.program_id(0) == 0)
        def _():
            for ref in (dg_ref, dgq_ref, dgkv_ref, dghq_ref, dghk_ref):
                ref[...] = jnp.zeros_like(ref)

        g = g_ref[...]
        xhat, r = _rms_fwd(x_ref[...])
        h = (xhat * g).astype(BF16)
        h_ref[...] = h
        cq, ckv, kpe = _mla_latents(h, win_ref)
        cqhat, rcq = _rms_fwd(cq)
        ckvhat, rckv = _rms_fwd(ckv)
        cqn = (cqhat * gq_ref[...]).astype(BF16)
        ckvn = (ckvhat * gkv_ref[...]).astype(BF16)
        cqn_ref[...] = cqn
        ckvn_ref[...] = ckvn
        cos, sn = c_ref[...], s_ref[...]
        ghq, ghk = ghq_ref[...], ghk_ref[...]
        dkpe = jnp.zeros((tm, 128), F32)
        dghq = jnp.zeros((8, HEAD_PAD), F32)
        dghk = jnp.zeros((8, HEAD_PAD), F32)
        for hd in range(N_HEADS):
            rws = pl.ds(hd * HEAD_PAD, HEAD_PAD)
            lanes = slice(hd * HEAD_PAD, (hd + 1) * HEAD_PAD)
            qhat, rq = _rms_fwd(_dot_nt(cqn, wq_ref[rws, :]), QK_HEAD)
            gqn = _rope_bwd(dq_ref[hd], cos, sn)
            dghq = dghq + _rowsum8(gqn * qhat)
            dqpre = _rms_bwd(gqn, qhat, rq, ghq, QK_HEAD).astype(BF16)
            dqp_ref[:, lanes] = dqpre
            kvh = _dot_nt(ckvn, wkv_ref[rws, :])
            khat, rk = _rms_fwd(jnp.concatenate([kvh[:, :QK_NOPE], kpe], axis=1), QK_HEAD)
            gkn = _rope_bwd(dk_ref[hd], cos, sn)
            dghk = dghk + _rowsum8(gkn * khat)
            dkpre = _rms_bwd(gkn, khat, rk, ghk, QK_HEAD)
            dkpe = dkpe + dkpre[:, QK_NOPE:]
            dkvh = jnp.concatenate([dkpre[:, :QK_NOPE], dv_ref[hd]], axis=1).astype(BF16)
            dkv_ref[:, lanes] = dkvh
        dcqn = _dot(dqp_ref[...], wq_ref[...])
        dckvn = _dot(dkv_ref[...], wkv_ref[...])
        dghq_ref[...] += dghq
        dghk_ref[...] += dghk
        dgq_ref[...] += _rowsum8(dcqn * cqhat)
        dgkv_ref[...] += _rowsum8(dckvn * ckvhat)
        dlat = jnp.concatenate([_rms_bwd(dcqn, cqhat, rcq, gq_ref[...]), _rms_bwd(dckvn, ckvhat, rckv, gkv_ref[...]),
                                dkpe], axis=1).astype(BF16)
        dlat_ref[...] = dlat
        dh = _dot(dlat, win_ref[...])
        dg_ref[...] += _rowsum8(dh * xhat)
        dx_ref[...] = do_ref[...] + _rms_bwd(dh, xhat, r, g)

    row = lambda w: pl.BlockSpec((tm, w), lambda i: (i, 0))
    head = lambda w: pl.BlockSpec((N_HEADS, tm, w), lambda i: (0, i, 0))
    sds = jax.ShapeDtypeStruct
    return pl.pallas_call(
        body, name=name, grid=(n // tm,),
        in_specs=[row(D_MODEL), row(D_MODEL), head(HEAD_PAD), head(HEAD_PAD), head(V_HEAD), _full((1, D_MODEL)),
                  _full((LAT_PAD, D_MODEL)), _full((1, Q_LORA)), _full((1, KV_LORA)), _full((hw, Q_LORA)),
                  _full((hw, KV_LORA)), _full((1, HEAD_PAD)), _full((1, HEAD_PAD)), row(128), row(128)],
        out_specs=[row(D_MODEL), row(D_MODEL), row(LAT_PAD), row(Q_LORA), row(hw), row(KV_LORA), row(hw),
                   _full((8, D_MODEL)), _full((8, Q_LORA)), _full((8, KV_LORA)), _full((8, HEAD_PAD)),
                   _full((8, HEAD_PAD))],
        out_shape=[sds((n, D_MODEL), F32), sds((n, D_MODEL), BF16), sds((n, LAT_PAD), BF16), sds((n, Q_LORA), BF16),
                   sds((n, hw), BF16), sds((n, KV_LORA), BF16), sds((n, hw), BF16), sds((8, D_MODEL), F32),
                   sds((8, Q_LORA), F32), sds((8, KV_LORA), F32), sds((8, HEAD_PAD), F32), sds((8, HEAD_PAD), F32)],
        compiler_params=_params("arbitrary"),
    )(x, dout, dq, dk, dv, gain, win, gq, gkv, wq, wkv, ghq, ghk, cos, sin_signed)


def _adamw(w, g, m, v, behind, name):
    rows, cols = w.shape
    tr = rows
    for cand in (512, 256, 128, 64, 32, 16, 8):
        if rows % cand == 0 and rows > cand:
            tr = cand
            break

    def body(w_ref, g_ref, m_ref, v_ref, behind_ref, d_ref, mo_ref, vo_ref):
        gv = g_ref[...]
        mn = ADAM_B1 * m_ref[...] + (1.0 - ADAM_B1) * gv
        vn = ADAM_B2 * v_ref[...] + (1.0 - ADAM_B2) * (gv * gv)
        m_hat = mn / (1.0 - ADAM_B1 ** ADAM_STEP)
        v_hat = vn / (1.0 - ADAM_B2 ** ADAM_STEP)
        d_ref[...] = -ADAM_LR * (m_hat / (jnp.sqrt(v_hat) + ADAM_EPS) + ADAM_WD * w_ref[...])
        mo_ref[...] = mn
        vo_ref[...] = vn

    blk = pl.BlockSpec((tr, cols), lambda i: (i, 0))
    return pl.pallas_call(
        body, name=name, grid=(rows // tr,), in_specs=[blk] * 4 + [pl.BlockSpec(memory_space=pl.ANY)],
        out_specs=[blk] * 3, out_shape=[jax.ShapeDtypeStruct((rows, cols), F32)] * 3,
        compiler_params=_params("arbitrary"),
    )(w, g, m, v, behind)


def _sum_parts(parts, name):
    k, r, c = parts.shape
    tr = min(r, 512)

    def body(p_ref, o_ref):
        acc = p_ref[0]
        for j in range(1, k):
            acc = acc + p_ref[j]
        o_ref[...] = acc

    return pl.pallas_call(
        body, name=name, grid=(r // tr,), in_specs=[pl.BlockSpec((k, tr, c), lambda i: (0, i, 0))],
        out_specs=pl.BlockSpec((tr, c), lambda i: (i, 0)), out_shape=jax.ShapeDtypeStruct((r, c), parts.dtype),
        compiler_params=_params("arbitrary"),
    )(parts)


def _row_tile(r, most=256):
    best = r
    for cand in range(8, most + 1, 8):
        if r % cand == 0:
            best = cand
    return best


def _sum_exchange(mine, landed, me, name):
    _, r, c = mine.shape
    tr = _row_tile(r)

    def body(me_ref, m_ref, l_ref, o_ref):
        acc = m_ref[0]
        for k in range(1, N_DEV):
            acc = acc + l_ref[k]
        o_ref[...] = acc

    return pl.pallas_call(
        body, name=name,
        grid_spec=pltpu.PrefetchScalarGridSpec(
            num_scalar_prefetch=1, grid=(r // tr,),
            in_specs=[pl.BlockSpec((1, tr, c), lambda i, me_ref: (me_ref[0], i, 0)),
                      pl.BlockSpec((N_DEV, tr, c), lambda i, me_ref: (0, i, 0))],
            out_specs=pl.BlockSpec((tr, c), lambda i, me_ref: (i, 0))),
        out_shape=jax.ShapeDtypeStruct((r, c), mine.dtype), compiler_params=_params("arbitrary"),
    )(me, mine, landed)


MESH = pl.DeviceIdType.MESH


def _all_gather(x, name):
    r, c = x.shape

    def body(x_ref, out_ref, send_sems, recv_sems, local_sem):
        mx, my, mc = lax.axis_index("x"), lax.axis_index("y"), lax.axis_index("c")
        me, sibling = (mx, my, mc), (mx, my, 1 - mc)
        chips = [(1 - mx, my), (mx, 1 - my), (1 - mx, 1 - my)]

        def slot(px, py, pc):
            return out_ref.at[4 * px + 2 * py + pc]

        def copy(k, block, to, src=None):
            return pltpu.make_async_remote_copy(
                src_ref=slot(*block) if src is None else src, dst_ref=slot(*block),
                send_sem=send_sems.at[k], recv_sem=recv_sems.at[k], device_id=to, device_id_type=MESH)

        mine = pltpu.make_async_copy(x_ref, slot(*me), local_sem)
        mine.start()
        first = [copy(0, me, sibling, src=x_ref)]
        first += [copy(1 + j, me, (*chip, mc), src=x_ref) for j, chip in enumerate(chips)]
        for cp in first:
            cp.start()
        passed = [copy(4 + j, (*chip, mc), sibling) for j, chip in enumerate(chips)]
        for j, chip in enumerate(chips):
            copy(1 + j, (*chip, mc), me).wait_recv()
            passed[j].start()
        copy(0, sibling, me).wait_recv()
        for j, chip in enumerate(chips):
            copy(4 + j, (*chip, 1 - mc), me).wait_recv()
        for cp in first + passed:
            cp.wait_send()
        mine.wait()

    any_spec = pl.BlockSpec(memory_space=pl.ANY)
    return pl.pallas_call(
        body, name=name, in_specs=[any_spec], out_specs=any_spec,
        out_shape=jax.ShapeDtypeStruct((N_DEV, r, c), x.dtype),
        scratch_shapes=[pltpu.SemaphoreType.DMA((7,)), pltpu.SemaphoreType.DMA((7,)), pltpu.SemaphoreType.DMA(())],
    )(x)


HBM_SPEC = pl.BlockSpec(memory_space=pltpu.HBM)
SEM_SPEC = pl.BlockSpec(memory_space=pltpu.SEMAPHORE)
SPLIT_EFFECT = pltpu.SideEffectType.DATAFLOW_SIDE_EFFECTING


def _exchange_copies(src_ref, land_ref, send_sems, recv_sems, gather):
    mx, my, mc = lax.axis_index("x"), lax.axis_index("y"), lax.axis_index("c")
    me = 4 * mx + 2 * my + mc
    copies = []
    for k in range(1, N_DEV):
        px = 1 - mx if k & 4 else mx
        py = 1 - my if k & 2 else my
        pc = 1 - mc if k & 1 else mc
        src = src_ref if gather else src_ref.at[4 * px + 2 * py + pc]
        dst = land_ref.at[me] if gather else land_ref.at[k]
        copies.append(pltpu.make_async_remote_copy(
            src_ref=src, dst_ref=dst, send_sem=send_sems.at[k - 1], recv_sem=recv_sems.at[k - 1],
            device_id=(px, py, pc), device_id_type=MESH))
    return copies


def _exchange_start(src, after, gather, name):
    land_shape = (N_DEV,) + src.shape[-2:]

    def body(src_ref, land_ref, after_ref, send_sems, recv_sems, src_thru, land_thru, token):
        for cp in _exchange_copies(src_ref, land_ref, send_sems, recv_sems, gather):
            cp.start()
        token[...] = jnp.zeros_like(token)

    return pl.pallas_call(
        body, name=name,
        out_shape=(pltpu.SemaphoreType.DMA((N_DEV - 1,)), pltpu.SemaphoreType.DMA((N_DEV - 1,)),
                   pltpu.HBM(src.shape, src.dtype), pltpu.HBM(land_shape, src.dtype),
                   jax.ShapeDtypeStruct((8, 128), F32)),
        in_specs=(HBM_SPEC, HBM_SPEC, pl.BlockSpec(memory_space=pl.ANY)),
        out_specs=(SEM_SPEC, SEM_SPEC, HBM_SPEC, HBM_SPEC, pl.BlockSpec(memory_space=pltpu.VMEM)),
        input_output_aliases={0: 2, 1: 3},
        compiler_params=pltpu.CompilerParams(has_side_effects=SPLIT_EFFECT),
    )(pltpu.with_memory_space_constraint(src, pltpu.HBM),
      pltpu.with_memory_space_constraint(lax.empty(land_shape, src.dtype), pltpu.HBM), after)


def _exchange_wait(started, after, gather, name):
    send_sems, recv_sems, src_thru, land_thru, _ = started

    def body(src_ref, land_ref, send_sems, recv_sems, after_ref, src_out, land_out):
        for cp in _exchange_copies(src_ref, land_ref, send_sems, recv_sems, gather):
            cp.wait_send()
            cp.wait_recv()

    return pl.pallas_call(
        body, name=name,
        out_shape=(pltpu.HBM(src_thru.shape, src_thru.dtype), pltpu.HBM(land_thru.shape, land_thru.dtype)),
        in_specs=(HBM_SPEC, HBM_SPEC, SEM_SPEC, SEM_SPEC, pl.BlockSpec(memory_space=pl.ANY)),
        out_specs=(HBM_SPEC, HBM_SPEC), input_output_aliases={0: 0, 1: 1},
        compiler_params=pltpu.CompilerParams(has_side_effects=SPLIT_EFFECT),
    )(src_thru, land_thru, send_sems, recv_sems, after)


FFN_ROWS = D_FF // N_DEV
WIN_ROWS = (Q_LORA + KV_LORA + QK_ROPE) // N_DEV
WIN_ROWS_PAD = 144
WQ_ROWS = QK_HEAD * Q_LORA // D_MODEL
WKV_ROWS = 256 * KV_LORA // D_MODEL
WOUT_ROWS = V_HEAD
POOL_ROWS = 4 * 32 * POOL_GROUP // D_MODEL


COLUMN_SHARDED = ("ffn1_w_gate", "ffn1_w_up", "ffn2_w_gate", "ffn2_w_up", "mla_w_in", "mla_w_q_up", "mla_w_kv_up")


def _t(w):
    return jnp.swapaxes(w, -1, -2)


def _ffn_segments(f, i):
    return [(f + "_w_gate", i, FFN_ROWS), (f + "_w_up", i, FFN_ROWS), (f + "_w_down", i, FFN_ROWS)]


def _mixer_segments(i):
    j = i // 2
    if i % 2 == 0:
        return [("pool_w", j, POOL_ROWS)]
    return [("mla_w_in", j, WIN_ROWS_PAD), ("mla_w_q_up", j, WQ_ROWS), ("mla_w_kv_up", j, WKV_ROWS),
            ("mla_w_out", j, WOUT_ROWS)]


def _pack_shards(p, segs, dtype):
    parts = []
    for name, idx, _ in segs:
        w = p[name][idx]
        if name.endswith("w_gate") or name.endswith("w_up"):
            w = _t(w)
        elif name == "mla_w_in":
            w = jnp.pad(_t(w), ((0, WIN_ROWS_PAD - WIN_ROWS), (0, 0)))
        elif name == "mla_w_q_up":
            w = _t(w).reshape(WQ_ROWS, D_MODEL)
        elif name == "mla_w_kv_up":
            w = _t(w).reshape(WKV_ROWS, D_MODEL)
        elif name == "pool_w":
            w = w.reshape(POOL_ROWS, D_MODEL)
        parts.append(w.astype(dtype))
    return jnp.concatenate(parts, axis=0)


def _unpack_gathered(g, segs):
    out = {}
    off = 0
    for name, layer, rows in segs:
        seg = g[:, off:off + rows, :]
        off += rows
        if name == "mla_w_in":
            w = seg[:, :WIN_ROWS].reshape(N_DEV * WIN_ROWS, D_MODEL)
            w = jnp.pad(w, ((0, LAT_PAD - N_DEV * WIN_ROWS), (0, 0)))
        elif name == "mla_w_q_up":
            w = seg.reshape(N_HEADS, QK_HEAD, Q_LORA)
            w = jnp.pad(w, ((0, 0), (0, HEAD_PAD - QK_HEAD), (0, 0))).reshape(N_HEADS * HEAD_PAD, Q_LORA)
        elif name == "mla_w_kv_up":
            w = seg.reshape(N_HEADS * 256, KV_LORA)
        elif name == "pool_w":
            w = seg.reshape(N_DEV, 4, 32, POOL_GROUP).transpose(1, 0, 2, 3).reshape(4, POOL_GROUP, POOL_GROUP)
        else:
            w = seg.reshape(N_DEV * rows, D_MODEL)
        out[(name, layer)] = w
    return out


def _pack_grads(gr, segments):
    segs = []
    for name, layer, rows in segments:
        g = gr[(name, layer)]
        if name == "mla_w_in":
            g = g[:N_DEV * WIN_ROWS].reshape(N_DEV, WIN_ROWS, D_MODEL)
            g = jnp.pad(g, ((0, 0), (0, WIN_ROWS_PAD - WIN_ROWS), (0, 0)))
        elif name == "mla_w_q_up":
            g = g.reshape(N_HEADS, HEAD_PAD, Q_LORA)[:, :QK_HEAD].reshape(N_DEV, WQ_ROWS, D_MODEL)
        elif name == "mla_w_kv_up":
            g = g.reshape(N_DEV, WKV_ROWS, D_MODEL)
        elif name == "pool_w":
            g = g.reshape(4, N_DEV, 32, POOL_GROUP).transpose(1, 0, 2, 3).reshape(N_DEV, POOL_ROWS, D_MODEL)
        else:
            g = g.reshape(N_DEV, rows, D_MODEL)
        segs.append(g)
    return jnp.concatenate(segs, axis=1)


def _unpack_shard_grads(flat, segments):
    per = {}
    off = 0
    for name, layer, rows in segments:
        seg = flat[off:off + rows]
        off += rows
        if name == "mla_w_in":
            g = seg[:WIN_ROWS]
        elif name == "mla_w_q_up":
            g = seg.reshape(QK_HEAD, Q_LORA)
        elif name == "mla_w_kv_up":
            g = seg.reshape(256, KV_LORA)
        elif name == "pool_w":
            g = seg.reshape(4, 32, POOL_GROUP)
        else:
            g = seg
        per[(name, layer)] = g
    return per


def _pad_lanes(v, width):
    return jnp.pad(v, ((0, 0), (0, width - v.shape[-1])))


def kernel(x, positions, ffn1_norm, ffn1_w_gate, ffn1_w_up, ffn1_w_down, mix_norm, pool_w, pool_scale, mla_w_in, mla_q_norm, mla_w_q_up, mla_kv_norm, mla_w_kv_up, mla_q_head_norm, mla_k_head_norm, mla_w_out, ffn2_norm, ffn2_w_gate, ffn2_w_up, ffn2_w_down, loss_target, m_ffn1_norm, m_ffn1_w_gate, m_ffn1_w_up, m_ffn1_w_down, m_mix_norm, m_pool_w, m_pool_scale, m_mla_w_in, m_mla_q_norm, m_mla_w_q_up, m_mla_kv_norm, m_mla_w_kv_up, m_mla_q_head_norm, m_mla_k_head_norm, m_mla_w_out, m_ffn2_norm, m_ffn2_w_gate, m_ffn2_w_up, m_ffn2_w_down, v_ffn1_norm, v_ffn1_w_gate, v_ffn1_w_up, v_ffn1_w_down, v_mix_norm, v_pool_w, v_pool_scale, v_mla_w_in, v_mla_q_norm, v_mla_w_q_up, v_mla_kv_norm, v_mla_w_kv_up, v_mla_q_head_norm, v_mla_k_head_norm, v_mla_w_out, v_ffn2_norm, v_ffn2_w_gate, v_ffn2_w_up, v_ffn2_w_down):
    args = (x, positions, ffn1_norm, ffn1_w_gate, ffn1_w_up, ffn1_w_down, mix_norm, pool_w, pool_scale, mla_w_in,
            mla_q_norm, mla_w_q_up, mla_kv_norm, mla_w_kv_up, mla_q_head_norm, mla_k_head_norm, mla_w_out, ffn2_norm,
            ffn2_w_gate, ffn2_w_up, ffn2_w_down)
    p = dict(zip(NAMES, args))
    moments_m = dict(zip(WEIGHTS, (m_ffn1_norm, m_ffn1_w_gate, m_ffn1_w_up, m_ffn1_w_down, m_mix_norm, m_pool_w, m_pool_scale, m_mla_w_in, m_mla_q_norm, m_mla_w_q_up, m_mla_kv_norm, m_mla_w_kv_up, m_mla_q_head_norm, m_mla_k_head_norm, m_mla_w_out, m_ffn2_norm, m_ffn2_w_gate, m_ffn2_w_up, m_ffn2_w_down)))
    moments_v = dict(zip(WEIGHTS, (v_ffn1_norm, v_ffn1_w_gate, v_ffn1_w_up, v_ffn1_w_down, v_mix_norm, v_pool_w, v_pool_scale, v_mla_w_in, v_mla_q_norm, v_mla_w_q_up, v_mla_kv_norm, v_mla_w_kv_up, v_mla_q_head_norm, v_mla_k_head_norm, v_mla_w_out, v_ffn2_norm, v_ffn2_w_gate, v_ffn2_w_up, v_ffn2_w_down)))
    dev = 4 * lax.axis_index("x") + 2 * lax.axis_index("y") + lax.axis_index("c")

    xs = x[0]
    n = xs.shape[0]
    target = loss_target[0]

    inv_freq = 1.0 / (ROPE_THETA ** (jnp.arange(0, QK_ROPE, 2, dtype=F32) / QK_ROPE))
    ang = positions[0].astype(F32)[..., None] * inv_freq
    cos, sin = jnp.cos(ang), jnp.sin(ang)
    zero = jnp.zeros((n, 128 - QK_ROPE), F32)
    rope_cos = jnp.concatenate([cos, cos, zero], axis=1)
    rope_sin = jnp.concatenate([-sin, sin, zero], axis=1)

    ag_groups = [_ffn_segments("ffn1", 0), _mixer_segments(0) + _ffn_segments("ffn2", 0)]
    ag_groups += [_ffn_segments("ffn1", i) + _mixer_segments(i) + _ffn_segments("ffn2", i) for i in range(1, DEPTH)]
    shards = [_pack_shards(p, segs, BF16) for segs in ag_groups]
    w = {}
    no_token = jnp.zeros((8, 128), F32)

    def tied(gain, token):
        return gain + token[0, 0]

    gains_local = jnp.concatenate([_pad_lanes(mla_q_norm, 128), _pad_lanes(mla_kv_norm, 128)], axis=0)
    gains_all = _all_gather(jnp.pad(gains_local, ((0, 4), (0, 0))), "gains_all_gather")
    q_norm_full = gains_all[:, 0:2, :Q_LORA // N_DEV].transpose(1, 0, 2).reshape(2, Q_LORA)
    kv_norm_full = gains_all[:, 2:4, :KV_LORA // N_DEV].transpose(1, 0, 2).reshape(2, KV_LORA)
    ghq = _pad_lanes(mla_q_head_norm, HEAD_PAD)
    ghk = _pad_lanes(mla_k_head_norm, HEAD_PAD)

    def mla_weights(j):
        return (w[("mla_w_in", j)], q_norm_full[j:j + 1], kv_norm_full[j:j + 1], w[("mla_w_q_up", j)],
                w[("mla_w_kv_up", j)], ghq[j:j + 1], ghk[j:j + 1], rope_cos, rope_sin)

    saved = []
    cur = xs
    gathers = [_exchange_start(shards[0], gains_all, True, "ag_start_0")]

    def gather_step(after):
        g = len(gathers) - 1
        _, landed = _exchange_wait(gathers[g], after, True, f"ag_wait_{g}")
        w.update(_unpack_gathered(lax.dynamic_update_slice(landed, shards[g][None], (dev, 0, 0)), ag_groups[g]))
        if g + 1 == len(ag_groups):
            return no_token
        gathers.append(_exchange_start(shards[g + 1], landed, True, f"ag_start_{g + 1}"))
        return gathers[-1][4]

    for i in range(DEPTH):
        j = i // 2
        st = {"x0": cur}
        token = gather_step(cur)
        cur, st["a1"], st["u1"] = _ffn_fwd(cur, tied(ffn1_norm[i:i + 1], token), w[("ffn1_w_gate", i)],
                                           w[("ffn1_w_up", i)], w[("ffn1_w_down", i)], f"ffn1_fwd_{i}")
        st["x1"] = cur
        if i == 0:
            token = gather_step(cur)
        if i % 2 == 0:
            cur = _pool_fwd(cur, tied(mix_norm[i:i + 1], token), w[("pool_w", j)], pool_scale[j:j + 1],
                            f"pool_fwd_{i}")
        else:
            st["q"], st["k"], st["v"], vt = _mla_pre_fwd(cur, mix_norm[i:i + 1], *mla_weights(j), f"mla_pre_fwd_{i}")
            st["o"], lse = _flash_fwd(st["q"], st["k"], vt, f"flash_fwd_{i}")
            tq_bwd = _tiles(n)["bwd_q"]
            st["lse"] = lse.reshape(N_HEADS, n // tq_bwd, 1, tq_bwd)
            cur = _mla_post_fwd(st["o"], cur, w[("mla_w_out", j)], f"mla_post_fwd_{i}")
        st["x2"] = cur
        cur, st["a2"], st["u2"] = _ffn_fwd(cur, ffn2_norm[i:i + 1], w[("ffn2_w_gate", i)], w[("ffn2_w_up", i)],
                                           w[("ffn2_w_down", i)], f"ffn2_fwd_{i}")
        saved.append(st)

    dcur, sq_err = _loss_head(cur, target, "loss_head")
    loss_part = 0.5 * jnp.sum(sq_err) * (1.0 / D_MODEL)

    gr = {}
    small = {k: [None] * DEPTH for k in ("ffn1_norm", "mix_norm", "ffn2_norm")}
    small.update({k: [None] * (DEPTH // 2) for k in ("pool_scale", "mla_q_norm", "mla_kv_norm", "mla_q_head_norm",
                                                     "mla_k_head_norm")})

    me = jnp.reshape(dev, (1,)).astype(jnp.int32)
    grads = {}
    in_flight = []

    def reduce_start(segs, tag):
        started = _exchange_start(_pack_grads(gr, segs), dcur, False, f"rs_start_{tag}")
        reduce_finish(started[4])
        in_flight.append((started, segs, tag))
        return started[4]

    def reduce_finish(after):
        if in_flight:
            started, segs, tag = in_flight.pop()
            mine, landed = _exchange_wait(started, after, False, f"rs_wait_{tag}")
            grads.update(_unpack_shard_grads(_sum_exchange(mine, landed, me, f"rs_sum_{tag}"), segs))

    def ffn_backward(f, i, x_in, a, u, gain, dout):
        dx, h, dob, y, da, du, dg = _ffn_bwd(x_in, dout, a, u, gain, w[(f + "_w_gate", i)], w[(f + "_w_up", i)],
                                             w[(f + "_w_down", i)], f"{f}_bwd_{i}")
        gr[(f + "_w_gate", i)] = _tn_matmul(da, h, f"{f}_dgate_{i}")
        gr[(f + "_w_up", i)] = _tn_matmul(du, h, f"{f}_dup_{i}")
        gr[(f + "_w_down", i)] = _tn_matmul(y, dob, f"{f}_ddown_{i}")
        small[f + "_norm"][i] = jnp.sum(dg, axis=0)
        return dx

    token = jnp.zeros((8, 128), F32)
    for i in reversed(range(DEPTH)):
        j = i // 2
        st = saved[i]
        dcur = ffn_backward("ffn2", i, st["x2"], st["a2"], st["u2"], tied(ffn2_norm[i:i + 1], token), dcur)
        if i % 2 == 0:
            dcur, dpw, dsc, dg = _pool_bwd(st["x1"], dcur, mix_norm[i:i + 1], w[("pool_w", j)], pool_scale[j:j + 1],
                                           f"pool_bwd_{i}")
            gr[("pool_w", j)] = dpw
            small["pool_scale"][j] = jnp.sum(dsc, axis=0)
            small["mix_norm"][i] = jnp.sum(dg, axis=0)
        else:
            dob, dpo, delta = _mla_out_bwd(dcur, st["o"], w[("mla_w_out", j)], f"mla_out_bwd_{i}")
            gr[("mla_w_out", j)] = _tn_matmul(st["o"], dob, f"mla_dout_{i}")
            dk, dv, dq = _flash_bwd(st["q"], st["k"], st["v"], dpo, st["lse"], delta, f"flash_bwd_{i}")
            (dcur, h, dlat, cqn, dqp, ckvn, dkv, dg, dgq, dgkv, dghq, dghk) = _mla_pre_bwd(
                st["x1"], dcur, dq, dk, dv, mix_norm[i:i + 1], *mla_weights(j), f"mla_pre_bwd_{i}")
            gr[("mla_w_in", j)] = _tn_matmul(dlat, h, f"mla_din_{i}")
            gr[("mla_w_q_up", j)] = _tn_matmul(dqp, cqn, f"mla_dqup_{i}")
            gr[("mla_w_kv_up", j)] = _tn_matmul(dkv, ckvn, f"mla_dkvup_{i}")
            small["mix_norm"][i] = jnp.sum(dg, axis=0)
            small["mla_q_norm"][j] = _pad_lanes(jnp.sum(dgq, axis=0)[None], D_MODEL)[0]
            small["mla_kv_norm"][j] = _pad_lanes(jnp.sum(dgkv, axis=0)[None], D_MODEL)[0]
            small["mla_q_head_norm"][j] = _pad_lanes(jnp.sum(dghq, axis=0)[None], D_MODEL)[0]
            small["mla_k_head_norm"][j] = _pad_lanes(jnp.sum(dghk, axis=0)[None], D_MODEL)[0]
        token = reduce_start(_ffn_segments("ffn2", i) + _mixer_segments(i), f"a{i}")
        if i > 0:
            dcur = ffn_backward("ffn1", i, st["x0"], st["a1"], st["u1"], tied(ffn1_norm[i:i + 1], token), dcur)
            token = reduce_start(_ffn_segments("ffn1", i), f"b{i}")
    st = saved[0]
    dcur, h, dob, y, da, du, dg = _ffn_bwd(st["x0"], dcur, st["a1"], st["u1"], tied(ffn1_norm[0:1], token),
                                           w[("ffn1_w_gate", 0)], w[("ffn1_w_up", 0)], w[("ffn1_w_down", 0)],
                                           "ffn1_bwd_0")
    small["ffn1_norm"][0] = jnp.sum(dg, axis=0)
    grad_x = dcur[None]
    small_order = ("ffn1_norm", "mix_norm", "ffn2_norm", "pool_scale", "mla_q_norm", "mla_kv_norm",
                   "mla_q_head_norm", "mla_k_head_norm")
    rows = [r for k in small_order for r in small[k]]
    rows.append(jnp.zeros((D_MODEL,), F32).at[0].set(loss_part))
    rows.append(jnp.zeros((D_MODEL,), F32))
    small_sum = _sum_parts(_all_gather(jnp.stack(rows), "small_all_gather"), "small_sum")
    loss = small_sum[SM_ROWS - 2, 0]
    token = small_sum
    for seg, lhs, rhs in zip(_ffn_segments("ffn1", 0), (da, du, y), (h, h, dob)):
        gr[seg[:2]] = _tn_matmul(lhs, rhs, f"ffn1_d{seg[0][7:]}_0", behind=token)
        token = reduce_start([seg], f"b0_{seg[0][7:]}")
    last = "ffn1_w_down"

    counts = {k: (DEPTH // 2 if k.startswith(("mla", "pool")) else DEPTH) for k in WEIGHTS}

    def stacked(k):
        return jnp.stack([grads[(k, idx)] for idx in range(counts[k])])

    grads.update({k: stacked(k) for k in WEIGHTS if (k, 0) in grads and k != last})
    off = 0
    for k in small_order:
        cnt = len(small[k])
        g = small_sum[off:off + cnt]
        off += cnt
        if k == "mla_q_norm":
            g = lax.dynamic_slice_in_dim(g[:, :Q_LORA], dev * (Q_LORA // N_DEV), Q_LORA // N_DEV, axis=1)
        elif k == "mla_kv_norm":
            g = lax.dynamic_slice_in_dim(g[:, :KV_LORA], dev * (KV_LORA // N_DEV), KV_LORA // N_DEV, axis=1)
        elif k in ("mla_q_head_norm", "mla_k_head_norm"):
            g = g[:, :QK_HEAD]
        grads[k] = g

    deltas, new_m, new_v = {}, {}, {}

    def update(k, behind):
        to_view = _t if k in COLUMN_SHARDED else (lambda a: a)
        shape = to_view(p[k]).shape
        view = (-1, shape[-1])
        d, mn, vn = _adamw(to_view(p[k]).reshape(view), grads[k].reshape(view), to_view(moments_m[k]).reshape(view),
                           to_view(moments_v[k]).reshape(view), behind, "adamw_" + k)
        deltas[k], new_m[k], new_v[k] = (to_view(a.reshape(shape)) for a in (d, mn, vn))
        grads[k] = to_view(grads[k].reshape(shape))
        return d

    done = token
    for k in WEIGHTS:
        if k != last:
            done = update(k, done)
    reduce_finish(done)
    grads[last] = stacked(last)
    update(last, done)

    return (loss, grad_x, *[grads[k] for k in WEIGHTS], *[deltas[k] for k in WEIGHTS],
            *[new_m[k] for k in WEIGHTS], *[new_v[k] for k in WEIGHTS])
```

```python
import functools

import jax
import jax.numpy as jnp
from jax import lax
from jax.experimental import pallas as pl
from jax.experimental.pallas import tpu as pltpu

F32 = jnp.float32
BF16 = jnp.bfloat16

D_MODEL = 1024
DEPTH = 4
D_FF = 2816
POOL_WINDOWS = (2, 4, 8, 16)
POOL_GROUP = 256
POOL_HALO = 16
N_HEADS = 8
QK_NOPE = 128
QK_ROPE = 64
QK_HEAD = 192
V_HEAD = 128
Q_LORA = 768
KV_LORA = 256
ROPE_THETA = 10000.0
EPS = 1e-6
FFN_HALF = 0.5
ADAM_LR = 0.001
ADAM_B1 = 0.9
ADAM_B2 = 0.999
ADAM_EPS = 1e-08
ADAM_WD = 0.01
ADAM_STEP = 10

N_DEV = 8
HEAD_PAD = 256
LAT_PAD = 1152
VT_ROWS = 144
LOG2_E = 1.4426950408889634
ATTN_SCALE = QK_HEAD ** -0.5
LOGIT_SCALE = ATTN_SCALE * LOG2_E
V7X_VMEM_LIMIT = 56 * 1024 * 1024
FF_CHUNK = 256
SM_ROWS = 24

NAMES = ['x', 'positions', 'ffn1_norm', 'ffn1_w_gate', 'ffn1_w_up', 'ffn1_w_down', 'mix_norm', 'pool_w',
         'pool_scale', 'mla_w_in', 'mla_q_norm', 'mla_w_q_up', 'mla_kv_norm', 'mla_w_kv_up', 'mla_q_head_norm',
         'mla_k_head_norm', 'mla_w_out', 'ffn2_norm', 'ffn2_w_gate', 'ffn2_w_up', 'ffn2_w_down']
WEIGHTS = NAMES[2:]


def _tiles(n):
    return dict(ffn_fwd=min(512, n), ffn_bwd=min(256, n), fwd_k=min(512, n // 2), bwd_q=min(512, n // 2),
                mla_bwd=min(256, n), pool=min(512, n), tn=min(2048, n), rows=min(1024, n))


def _params(*sem):
    return pltpu.CompilerParams(dimension_semantics=sem, vmem_limit_bytes=V7X_VMEM_LIMIT)


def _dot(a, b):
    return jnp.dot(a, b, preferred_element_type=F32)


def _dot_nt(a, b):
    return lax.dot_general(a, b, (((1,), (1,)), ((), ())), preferred_element_type=F32)


def _dot_tn(a, b):
    return lax.dot_general(a, b, (((0,), (0,)), ((), ())), preferred_element_type=F32)


def _rowsum8(v):
    rows, w = v.shape
    return jnp.sum(v.reshape(rows // 8, 8, w), axis=0)


def _sigmoid(a):
    return 1.0 / (1.0 + jnp.exp(-a))


def _lane_sum(v):
    f = v[:, :128]
    for t in range(1, v.shape[1] // 128):
        f = f + v[:, t * 128:(t + 1) * 128]
    hi = f.astype(BF16)
    lo = (f - hi.astype(F32)).astype(BF16)
    return _dot(jnp.concatenate([hi, lo], axis=1), jnp.ones((256, 128), BF16))


def _by_row(v, r):
    return jnp.concatenate([v[:, t * 128:(t + 1) * 128] * r for t in range(v.shape[1] // 128)], axis=1)


def _rms_fwd(x, width=None):
    width = x.shape[-1] if width is None else width
    r = lax.rsqrt(_lane_sum(x * x) * (1.0 / width) + EPS)
    return _by_row(x, r), r


def _rms_bwd(dy, xhat, r, gain, width=None):
    width = xhat.shape[-1] if width is None else width
    t = dy * gain
    return _by_row(t - _by_row(xhat, _lane_sum(t * xhat) * (1.0 / width)), r)


def _full(shape):
    return pl.BlockSpec(shape, lambda *_: (0,) * len(shape))


def _load_weights(srcs, dsts, sems):
    copies = [pltpu.make_async_copy(s, d, sems.at[i]) for i, (s, d) in enumerate(zip(srcs, dsts))]
    for cp in copies:
        cp.start()
    for cp in copies:
        cp.wait()


def _ffn_fwd(x, gain, wg_t, wu_t, wd, name):
    n = x.shape[0]
    tm = _tiles(n)["ffn_fwd"]

    def body(x_ref, g_ref, wg_hbm, wu_hbm, wd_hbm, out_ref, a_ref, u_ref, wg_v, wu_v, wd_v, sems):
        @pl.when(pl.program_id(0) == 0)
        def _():
            _load_weights((wg_hbm, wu_hbm, wd_hbm), (wg_v, wu_v, wd_v), sems)

        xt = x_ref[...]
        xhat, _ = _rms_fwd(xt)
        h = (xhat * g_ref[...]).astype(BF16)
        acc = jnp.zeros((tm, D_MODEL), F32)
        for c in range(D_FF // FF_CHUNK):
            sl = pl.ds(c * FF_CHUNK, FF_CHUNK)
            a = _dot_nt(h, wg_v[sl, :])
            u = _dot_nt(h, wu_v[sl, :])
            a_ref[:, sl] = a.astype(BF16)
            u_ref[:, sl] = u.astype(BF16)
            y = (a * _sigmoid(a) * u).astype(BF16)
            acc = acc + _dot(y, wd_v[sl, :])
        out_ref[...] = xt + FFN_HALF * acc

    any_spec = pl.BlockSpec(memory_space=pl.ANY)
    return pl.pallas_call(
        body, name=name, grid=(n // tm,),
        in_specs=[pl.BlockSpec((tm, D_MODEL), lambda i: (i, 0)), _full((1, D_MODEL)), any_spec, any_spec, any_spec],
        out_specs=[pl.BlockSpec((tm, D_MODEL), lambda i: (i, 0)), pl.BlockSpec((tm, D_FF), lambda i: (i, 0)),
                   pl.BlockSpec((tm, D_FF), lambda i: (i, 0))],
        out_shape=[jax.ShapeDtypeStruct((n, D_MODEL), F32), jax.ShapeDtypeStruct((n, D_FF), BF16),
                   jax.ShapeDtypeStruct((n, D_FF), BF16)],
        scratch_shapes=[pltpu.VMEM((D_FF, D_MODEL), BF16)] * 3 + [pltpu.SemaphoreType.DMA((3,))],
        compiler_params=_params("arbitrary"),
    )(x, gain, wg_t, wu_t, wd)


def _ffn_bwd(x, dout, a, u, gain, wg_t, wu_t, wd, name):
    n = x.shape[0]
    tm = _tiles(n)["ffn_bwd"]

    def body(x_ref, do_ref, a_ref, u_ref, g_ref, wg_hbm, wu_hbm, wd_hbm,
             dx_ref, h_ref, dob_ref, y_ref, da_ref, du_ref, dg_ref, wg_v, wu_v, wd_v, sems):
        @pl.when(pl.program_id(0) == 0)
        def _():
            _load_weights((wg_hbm, wu_hbm, wd_hbm), (wg_v, wu_v, wd_v), sems)
            dg_ref[...] = jnp.zeros_like(dg_ref)

        xt = x_ref[...]
        g = g_ref[...]
        xhat, r = _rms_fwd(xt)
        h_ref[...] = (xhat * g).astype(BF16)
        dout = do_ref[...]
        dob = (FFN_HALF * dout).astype(BF16)
        dob_ref[...] = dob
        for c in range(D_FF // FF_CHUNK):
            sl = pl.ds(c * FF_CHUNK, FF_CHUNK)
            dy = _dot_nt(dob, wd_v[sl, :])
            av = a_ref[:, sl].astype(F32)
            uv = u_ref[:, sl].astype(F32)
            s = _sigmoid(av)
            silu = av * s
            y_ref[:, sl] = (silu * uv).astype(BF16)
            du_ref[:, sl] = (dy * silu).astype(BF16)
            da_ref[:, sl] = (dy * uv * (s * (1.0 + av * (1.0 - s)))).astype(BF16)
        dh = _dot(da_ref[...], wg_v[...]) + _dot(du_ref[...], wu_v[...])
        dg_ref[...] += _rowsum8(dh * xhat)
        dx_ref[...] = dout + _rms_bwd(dh, xhat, r, g)

    any_spec = pl.BlockSpec(memory_space=pl.ANY)
    row_d = pl.BlockSpec((tm, D_MODEL), lambda i: (i, 0))
    row_f = pl.BlockSpec((tm, D_FF), lambda i: (i, 0))
    return pl.pallas_call(
        body, name=name, grid=(n // tm,),
        in_specs=[row_d, row_d, row_f, row_f, _full((1, D_MODEL)), any_spec, any_spec, any_spec],
        out_specs=[row_d, row_d, row_d, row_f, row_f, row_f, _full((8, D_MODEL))],
        out_shape=[jax.ShapeDtypeStruct((n, D_MODEL), F32), jax.ShapeDtypeStruct((n, D_MODEL), BF16),
                   jax.ShapeDtypeStruct((n, D_MODEL), BF16), jax.ShapeDtypeStruct((n, D_FF), BF16),
                   jax.ShapeDtypeStruct((n, D_FF), BF16), jax.ShapeDtypeStruct((n, D_FF), BF16),
                   jax.ShapeDtypeStruct((8, D_MODEL), F32)],
        scratch_shapes=[pltpu.VMEM((D_FF, D_MODEL), BF16)] * 3 + [pltpu.SemaphoreType.DMA((3,))],
        compiler_params=_params("arbitrary"),
    )(x, dout, a, u, gain, wg_t, wu_t, wd)


def _tn_matmul(a, b, name, behind=None):
    n, fa = a.shape
    db = b.shape[1]
    tk = _tiles(n)["tn"]
    tf = fa // 2 if (fa // 2) % 128 == 0 and fa > 1024 else fa
    behind = jnp.zeros((8, 128), F32) if behind is None else behind

    def body(a_ref, b_ref, behind_ref, o_ref):
        @pl.when(pl.program_id(1) == 0)
        def _():
            o_ref[...] = jnp.zeros_like(o_ref)

        o_ref[...] += _dot_tn(a_ref[...], b_ref[...])

    return pl.pallas_call(
        body, name=name, grid=(fa // tf, n // tk),
        in_specs=[pl.BlockSpec((tk, tf), lambda i, k: (k, i)), pl.BlockSpec((tk, db), lambda i, k: (k, 0)),
                  pl.BlockSpec(memory_space=pl.ANY)],
        out_specs=pl.BlockSpec((tf, db), lambda i, k: (i, 0)),
        out_shape=jax.ShapeDtypeStruct((fa, db), F32),
        compiler_params=_params("arbitrary", "arbitrary"),
    )(a, b, behind)


def _loss_head(y, target, name):
    n = y.shape[0]
    tm = _tiles(n)["rows"]

    def body(y_ref, t_ref, d_ref, acc_ref):
        @pl.when(pl.program_id(0) == 0)
        def _():
            acc_ref[...] = jnp.zeros_like(acc_ref)

        d = y_ref[...] - t_ref[...]
        d_ref[...] = d * (1.0 / D_MODEL)
        acc_ref[...] += _rowsum8(d * d)

    row = pl.BlockSpec((tm, D_MODEL), lambda i: (i, 0))
    return pl.pallas_call(
        body, name=name, grid=(n // tm,), in_specs=[row, row], out_specs=[row, _full((8, D_MODEL))],
        out_shape=[jax.ShapeDtypeStruct((n, D_MODEL), F32), jax.ShapeDtypeStruct((8, D_MODEL), F32)],
        compiler_params=_params("arbitrary"),
    )(y, target)


def _window_sum(v, w, rows, forward):
    s = v
    sh = 1
    while sh < w:
        s = s + pltpu.roll(s, (rows - sh) if forward else sh, 0)
        sh *= 2
    return s


def _pool_fwd(x, gain, w, scale, name):
    n = x.shape[0]
    tm = _tiles(n)["pool"]
    hb = tm // POOL_HALO
    rows = tm + POOL_HALO

    def body(x_ref, xh_ref, g_ref, w_ref, sc_ref, out_ref):
        i = pl.program_id(0)
        xt = x_ref[...]
        e = jnp.concatenate([xh_ref[...], xt], axis=0)
        xhat, _ = _rms_fwd(e)
        row = lax.broadcasted_iota(jnp.int32, (rows, 1), 0)
        hn = jnp.where((row >= POOL_HALO) | (i > 0), xhat * g_ref[...], 0.0)
        t_glob = i * tm + row - POOL_HALO
        outs = []
        for gi, win in enumerate(POOL_WINDOWS):
            ug = hn[:, gi * POOL_GROUP:(gi + 1) * POOL_GROUP]
            cnt = jnp.maximum(jnp.minimum(t_glob + 1, win), 1).astype(F32)
            pooled = (_window_sum(ug, win, rows, False) / cnt - ug)[POOL_HALO:]
            outs.append(_dot(pooled.astype(BF16), w_ref[gi]))
        out_ref[...] = xt + jnp.concatenate(outs, axis=1) * sc_ref[...]

    return pl.pallas_call(
        body, name=name, grid=(n // tm,),
        in_specs=[pl.BlockSpec((tm, D_MODEL), lambda i: (i, 0)),
                  pl.BlockSpec((POOL_HALO, D_MODEL), lambda i: (jnp.maximum(i * hb - 1, 0), 0)),
                  _full((1, D_MODEL)), _full((4, POOL_GROUP, POOL_GROUP)), _full((1, D_MODEL))],
        out_specs=pl.BlockSpec((tm, D_MODEL), lambda i: (i, 0)),
        out_shape=jax.ShapeDtypeStruct((n, D_MODEL), F32),
        compiler_params=_params("arbitrary"),
    )(x, x, gain, w, scale)


def _pool_bwd(x, dout, gain, w, scale, name):
    n = x.shape[0]
    tm = _tiles(n)["pool"]
    hb = tm // POOL_HALO
    rows = tm + POOL_HALO
    nt = n // tm

    def body(x_ref, xh_ref, do_ref, doh_ref, g_ref, w_ref, sc_ref, dx_ref, dw_ref, dsc_ref, dg_ref):
        i = pl.program_id(0)

        @pl.when(i == 0)
        def _():
            dw_ref[...] = jnp.zeros_like(dw_ref)
            dsc_ref[...] = jnp.zeros_like(dsc_ref)
            dg_ref[...] = jnp.zeros_like(dg_ref)

        xt = x_ref[...]
        g = g_ref[...]
        e = jnp.concatenate([xh_ref[...], xt], axis=0)
        xhat_e, r_e = _rms_fwd(e)
        row = lax.broadcasted_iota(jnp.int32, (rows, 1), 0)
        hn = jnp.where((row >= POOL_HALO) | (i > 0), xhat_e * g, 0.0)
        t_prev = i * tm + row - POOL_HALO
        t_next = i * tm + row
        dout = do_ref[...]
        dt = jnp.concatenate([dout, doh_ref[...]], axis=0)
        dt = jnp.where((row < tm) | (i < nt - 1), dt, 0.0)
        dyr = dt * sc_ref[...]
        dus, dscs = [], []
        for gi, win in enumerate(POOL_WINDOWS):
            lanes = slice(gi * POOL_GROUP, (gi + 1) * POOL_GROUP)
            ug = hn[:, lanes]
            cnt = jnp.maximum(jnp.minimum(t_prev + 1, win), 1).astype(F32)
            pooled = (_window_sum(ug, win, rows, False) / cnt - ug)[POOL_HALO:].astype(BF16)
            yraw = _dot(pooled, w_ref[gi])
            dscs.append(_rowsum8(dout[:, lanes] * yraw))
            dyr_b = dyr[:, lanes].astype(BF16)
            dw_ref[gi] += _dot_tn(pooled, dyr_b[:tm])
            dpool = _dot_nt(dyr_b, w_ref[gi])
            cnt2 = jnp.minimum(t_next + 1, win).astype(F32)
            dus.append((_window_sum(dpool / cnt2, win, rows, True) - dpool)[:tm])
        dsc_ref[...] += jnp.concatenate(dscs, axis=1)
        dh = jnp.concatenate(dus, axis=1)
        xhat = xhat_e[POOL_HALO:]
        dg_ref[...] += _rowsum8(dh * xhat)
        dx_ref[...] = dout + _rms_bwd(dh, xhat, r_e[POOL_HALO:], g)

    row_d = pl.BlockSpec((tm, D_MODEL), lambda i: (i, 0))
    prev_h = pl.BlockSpec((POOL_HALO, D_MODEL), lambda i: (jnp.maximum(i * hb - 1, 0), 0))
    next_h = pl.BlockSpec((POOL_HALO, D_MODEL), lambda i: (jnp.minimum((i + 1) * hb, n // POOL_HALO - 1), 0))
    return pl.pallas_call(
        body, name=name, grid=(nt,),
        in_specs=[row_d, prev_h, row_d, next_h, _full((1, D_MODEL)), _full((4, POOL_GROUP, POOL_GROUP)),
                  _full((1, D_MODEL))],
        out_specs=[row_d, _full((4, POOL_GROUP, POOL_GROUP)), _full((8, D_MODEL)), _full((8, D_MODEL))],
        out_shape=[jax.ShapeDtypeStruct((n, D_MODEL), F32), jax.ShapeDtypeStruct((4, POOL_GROUP, POOL_GROUP), F32),
                   jax.ShapeDtypeStruct((8, D_MODEL), F32), jax.ShapeDtypeStruct((8, D_MODEL), F32)],
        compiler_params=_params("arbitrary"),
    )(x, x, dout, dout, gain, w, scale)


def _rope(v, cos, sin_signed):
    lo, hi = v[:, :128], v[:, 128:]
    lane = lax.broadcasted_iota(jnp.int32, hi.shape, 1)
    swapped = jnp.where(lane < 32, pltpu.roll(hi, 96, 1), pltpu.roll(hi, 32, 1))
    return jnp.concatenate([lo, hi * cos + swapped * sin_signed], axis=1)


def _rope_bwd(gr, cos, sin_signed):
    lo, hi = gr[:, :128], gr[:, 128:]
    t = hi * sin_signed
    lane = lax.broadcasted_iota(jnp.int32, hi.shape, 1)
    swapped = jnp.where(lane < 32, pltpu.roll(t, 96, 1), pltpu.roll(t, 32, 1))
    return jnp.concatenate([lo, hi * cos + swapped], axis=1)


def _mla_latents(h, win_ref):
    cq = _dot_nt(h, win_ref[0:Q_LORA, :])
    ckv = _dot_nt(h, win_ref[Q_LORA:Q_LORA + KV_LORA, :])
    kpe = _dot_nt(h, win_ref[Q_LORA + KV_LORA:LAT_PAD, :])
    return cq, ckv, kpe


def _mla_pre_fwd(x, gain, win, gq, gkv, wq, wkv, ghq, ghk, cos, sin_signed, name):
    n = x.shape[0]
    tm = _tiles(n)["fwd_k"]

    def body(x_ref, g_ref, win_ref, gq_ref, gkv_ref, wq_ref, wkv_ref, ghq_ref, ghk_ref, c_ref, s_ref,
             q_ref, k_ref, v_ref, vt_ref):
        xhat, _ = _rms_fwd(x_ref[...])
        h = (xhat * g_ref[...]).astype(BF16)
        cq, ckv, kpe = _mla_latents(h, win_ref)
        cqn = (_rms_fwd(cq)[0] * gq_ref[...]).astype(BF16)
        ckvn = (_rms_fwd(ckv)[0] * gkv_ref[...]).astype(BF16)
        cos, sn = c_ref[...], s_ref[...]
        for hd in range(N_HEADS):
            rws = pl.ds(hd * HEAD_PAD, HEAD_PAD)
            qh = _dot_nt(cqn, wq_ref[rws, :])
            qn = _rms_fwd(qh, QK_HEAD)[0] * ghq_ref[...]
            q_ref[hd] = (_rope(qn, cos, sn) * LOGIT_SCALE).astype(BF16)
            kvh = _dot_nt(ckvn, wkv_ref[rws, :])
            kpre = jnp.concatenate([kvh[:, :QK_NOPE], kpe], axis=1)
            kn = _rms_fwd(kpre, QK_HEAD)[0] * ghk_ref[...]
            k_ref[hd] = _rope(kn, cos, sn).astype(BF16)
            vh = kvh[:, QK_NOPE:]
            v_ref[hd] = vh.astype(BF16)
            vt_ref[hd, 0] = jnp.concatenate([vh.T, jnp.ones((VT_ROWS - V_HEAD, tm), F32)], axis=0).astype(BF16)

    row = lambda w: pl.BlockSpec((tm, w), lambda i: (i, 0))
    head = lambda w: pl.BlockSpec((N_HEADS, tm, w), lambda i: (0, i, 0))
    return pl.pallas_call(
        body, name=name, grid=(n // tm,),
        in_specs=[row(D_MODEL), _full((1, D_MODEL)), _full((LAT_PAD, D_MODEL)), _full((1, Q_LORA)),
                  _full((1, KV_LORA)), _full((N_HEADS * HEAD_PAD, Q_LORA)), _full((N_HEADS * HEAD_PAD, KV_LORA)),
                  _full((1, HEAD_PAD)), _full((1, HEAD_PAD)), row(128), row(128)],
        out_specs=[head(HEAD_PAD), head(HEAD_PAD), head(V_HEAD),
                   pl.BlockSpec((N_HEADS, 1, VT_ROWS, tm), lambda i: (0, i, 0, 0))],
        out_shape=[jax.ShapeDtypeStruct((N_HEADS, n, HEAD_PAD), BF16), jax.ShapeDtypeStruct((N_HEADS, n, HEAD_PAD), BF16),
                   jax.ShapeDtypeStruct((N_HEADS, n, V_HEAD), BF16),
                   jax.ShapeDtypeStruct((N_HEADS, n // tm, VT_ROWS, tm), BF16)],
        compiler_params=_params("arbitrary"),
    )(x, gain, win, gq, gkv, wq, wkv, ghq, ghk, cos, sin_signed)


def _flash_fwd(q, k, vt, name):
    n = q.shape[1]
    tk = _tiles(n)["fwd_k"]
    tq = 2 * tk
    nq = n // tq

    def body(q_ref, k_ref, vt_ref, o_ref, lse_ref, m_ref, p_hbm, s_scr, m_scr, acc_scr, p_scr, p_sems):
        h = pl.program_id(0)
        i = pl.program_id(1)
        qi = q_ref[0]

        def scores(j, slot):
            s_scr[slot] = _dot_nt(k_ref[0, pl.ds(pl.multiple_of(j * tk, tk), tk), :], qi)

        def p_copy(block, pslot):
            return pltpu.make_async_copy(p_scr.at[pslot], p_hbm.at[h, block], p_sems.at[pslot])

        def update(j, slot, pslot, diagonal=None):
            s = s_scr[slot]
            if diagonal is not None:
                krow = lax.broadcasted_iota(jnp.int32, (tk, tq), 0) + diagonal * tk
                qcol = lax.broadcasted_iota(jnp.int32, (tk, tq), 1)
                s = jnp.where(krow <= qcol, s, -jnp.inf)
            m = m_scr[...]
            m_new = jnp.maximum(m, jnp.max(s, axis=0, keepdims=True))
            p = jnp.exp2(s - m_new).astype(BF16)
            p_scr[pslot] = p
            acc_scr[...] = jnp.exp2(m - m_new) * acc_scr[...] + _dot(vt_ref[0, j], p)
            m_scr[...] = m_new
            m_ref[0, 0, j] = m_new

        def kv_pair(jj, diagonal):
            first = 2 * (jj % 2)
            p_copy(0, first).wait()
            p_copy(0, first + 1).wait()
            scores(2 * jj + 1, 1)
            update(2 * jj, 0, first, 0 if diagonal else None)
            if not diagonal:
                scores(2 * jj + 2, 0)
            update(2 * jj + 1, 1, first + 1, 1 if diagonal else None)
            p_copy(i * (i + 1) + 2 * jj, first).start()
            p_copy(i * (i + 1) + 2 * jj + 1, first + 1).start()

        m_scr[...] = jnp.full((1, tq), -jnp.inf, F32)
        acc_scr[...] = jnp.zeros((VT_ROWS, tq), F32)
        p_scr[...] = jnp.zeros_like(p_scr)
        for pslot in range(4):
            p_copy(nq * (nq + 1) + pslot, pslot).start()
        scores(0, 0)

        def pair(jj, carry):
            kv_pair(jj, False)
            return carry

        lax.fori_loop(0, i, pair, 0)
        kv_pair(i, True)
        l = acc_scr[V_HEAD:V_HEAD + 1, :]
        o_ref[...] = (acc_scr[0:V_HEAD, :] / l).T.astype(BF16)
        lse_ref[0, 0] = m_scr[...] + jnp.log2(l)
        for pslot in range(4):
            p_copy(0, pslot).wait()

    return pl.pallas_call(
        body, name=name, grid=(N_HEADS, nq),
        in_specs=[pl.BlockSpec((1, tq, HEAD_PAD), lambda h, i: (h, i, 0)),
                  pl.BlockSpec((1, n, HEAD_PAD), lambda h, i: (h, 0, 0)),
                  pl.BlockSpec((1, n // tk, VT_ROWS, tk), lambda h, i: (h, 0, 0, 0))],
        out_specs=[pl.BlockSpec((tq, V_HEAD), lambda h, i: (i, h)),
                   pl.BlockSpec((1, 1, 1, tq), lambda h, i: (h, i, 0, 0)),
                   pl.BlockSpec((1, 1, 2 * nq, 1, tq), lambda h, i: (h, i, 0, 0, 0)),
                   pl.BlockSpec(memory_space=pl.ANY)],
        out_shape=[jax.ShapeDtypeStruct((n, N_HEADS * V_HEAD), BF16), jax.ShapeDtypeStruct((N_HEADS, nq, 1, tq), F32),
                   jax.ShapeDtypeStruct((N_HEADS, nq, 2 * nq, 1, tq), F32),
                   jax.ShapeDtypeStruct((N_HEADS, nq * (nq + 1) + 4, tk, tq), BF16)],
        scratch_shapes=[pltpu.VMEM((2, tk, tq), F32), pltpu.VMEM((1, tq), F32), pltpu.VMEM((VT_ROWS, tq), F32),
                        pltpu.VMEM((4, tk, tq), BF16), pltpu.SemaphoreType.DMA((4,))],
        compiler_params=_params("arbitrary", "arbitrary"),
    )(q, k, vt)


def _mla_post_fwd(o, x, wout, name):
    n = x.shape[0]
    tm = _tiles(n)["rows"]

    def body(o_ref, x_ref, w_ref, out_ref):
        out_ref[...] = x_ref[...] + _dot(o_ref[...], w_ref[...])

    row = pl.BlockSpec((tm, D_MODEL), lambda i: (i, 0))
    return pl.pallas_call(
        body, name=name, grid=(n // tm,), in_specs=[row, row, _full((D_MODEL, D_MODEL))], out_specs=row,
        out_shape=jax.ShapeDtypeStruct((n, D_MODEL), F32), compiler_params=_params("arbitrary"),
    )(o, x, wout)


def _mla_out_bwd(dout, o, wout, name):
    n = dout.shape[0]
    t = _tiles(n)["bwd_q"]
    nq = n // t

    def body(do_ref, o_ref, w_ref, dob_ref, dpo_ref, dl_ref):
        dob = do_ref[...].astype(BF16)
        dob_ref[...] = dob
        dpo = _dot_nt(dob, w_ref[...])
        dpo_ref[...] = dpo.astype(BF16)
        ov = o_ref[...].astype(F32)
        for hd in range(N_HEADS):
            lanes = slice(hd * V_HEAD, (hd + 1) * V_HEAD)
            prod = dpo[:, lanes] * ov[:, lanes]
            dl_ref[hd, 0] = jnp.sum(prod.T, axis=0, keepdims=True)

    row = pl.BlockSpec((t, D_MODEL), lambda i: (i, 0))
    return pl.pallas_call(
        body, name=name, grid=(nq,), in_specs=[row, row, _full((D_MODEL, D_MODEL))],
        out_specs=[row, row, pl.BlockSpec((N_HEADS, 1, 1, t), lambda i: (0, i, 0, 0))],
        out_shape=[jax.ShapeDtypeStruct((n, D_MODEL), BF16), jax.ShapeDtypeStruct((n, D_MODEL), BF16),
                   jax.ShapeDtypeStruct((N_HEADS, nq, 1, t), F32)],
        compiler_params=_params("arbitrary"),
    )(dout, o, wout)


def _flash_bwd(q, k, v, dpo, lse, delta, p, m, name):
    n = q.shape[1]
    tq = _tiles(n)["bwd_q"]
    tk = 2 * tq
    nk = n // tk
    nqb = n // tq
    assert p.shape[2:] == (tq, tk) and m.shape[1:] == (nk, 2 * nk, 1, tk), "forward blocks are (tq keys, 2 tq queries)"

    def body(k_ref, v_ref, q_ref, do_ref, lse_ref, dl_ref, m_ref, p_hbm, dk_ref, dv_ref, dq_hbm,
             dq_acc, p_scr, dp_scr, sem, p_sems):
        h = pl.program_id(0)
        j = pl.program_id(1)

        @pl.when(j == 0)
        def _():
            dq_acc[...] = jnp.zeros_like(dq_acc)

        dk_ref[...] = jnp.zeros_like(dk_ref)
        dv_ref[...] = jnp.zeros_like(dv_ref)
        kj = k_ref[0]
        vj = v_ref[0]
        npairs = nk - 1 - j

        def block(t):
            return jnp.where(t < 2 * npairs, 2 * j + 2 + t, 2 * j + (t - 2 * npairs))

        def p_copy(i, first, half):
            tile = i // 2
            return pltpu.make_async_copy(p_hbm.at[h, pl.ds(tile * (tile + 1) + 2 * j, 2), :, pl.ds(half * tq, tq)],
                                         p_scr.at[first + half], p_sems.at[first + half])

        def fetch_pair(t):
            for half in range(2):
                p_copy(block(2 * t + half), 2 * (t % 2), half).start()

        def dp_ahead(i, half):
            dp_scr[half] = _dot_nt(vj, do_ref[pl.ds(pl.multiple_of(i * tq, tq), tq), :])

        def update(i, first, half):
            rws = pl.ds(pl.multiple_of(i * tq, tq), tq)
            lse_i = lse_ref[0, i]
            lanes = slice(half * tq, (half + 1) * tq)
            pv = jnp.concatenate(
                [p_scr[first + half, b].astype(F32) * jnp.exp2(m_ref[0, i // 2, 2 * j + b][:, lanes] - lse_i)
                 for b in range(2)], axis=0)
            dv_ref[0] += _dot(pv.astype(BF16), do_ref[rws, :])
            ds = (pv * (dp_scr[half] - dl_ref[0, i])).astype(BF16)
            dk_ref[0] += _dot(ds, q_ref[0, rws, :])
            dq_acc[rws, :] += _dot_tn(ds, kj)

        def q_pair(t, last):
            first = 2 * (t % 2)
            for half in range(2):
                p_copy(0, first, half).wait()
            if not last:
                fetch_pair(t + 1)
            dp_ahead(block(2 * t + 1), 1)
            update(block(2 * t), first, 0)
            if not last:
                dp_ahead(block(2 * t + 2), 0)
            update(block(2 * t + 1), first, 1)

        fetch_pair(0)
        dp_ahead(block(0), 0)

        def pair(t, carry):
            q_pair(t, False)
            return carry

        lax.fori_loop(0, npairs, pair, 0)
        q_pair(npairs, True)
        dk_ref[...] = dk_ref[...] * (ATTN_SCALE / LOGIT_SCALE)

        @pl.when(j == nk - 1)
        def _():
            dq_acc[...] = dq_acc[...] * ATTN_SCALE
            cp = pltpu.make_async_copy(dq_acc, dq_hbm.at[h], sem)
            cp.start()
            cp.wait()

    resident = dict(pipeline_mode=pl.Buffered(1))
    return pl.pallas_call(
        body, name=name, grid=(N_HEADS, nk),
        in_specs=[pl.BlockSpec((1, tk, HEAD_PAD), lambda h, j: (h, j, 0)),
                  pl.BlockSpec((1, tk, V_HEAD), lambda h, j: (h, j, 0)),
                  pl.BlockSpec((1, n, HEAD_PAD), lambda h, j: (h, 0, 0), **resident),
                  pl.BlockSpec((n, V_HEAD), lambda h, j: (0, h), **resident),
                  pl.BlockSpec((1, nqb, 1, tq), lambda h, j: (h, 0, 0, 0)),
                  pl.BlockSpec((1, nqb, 1, tq), lambda h, j: (h, 0, 0, 0)),
                  pl.BlockSpec((1, nk, 2 * nk, 1, tk), lambda h, j: (h, 0, 0, 0, 0), **resident),
                  pl.BlockSpec(memory_space=pl.ANY)],
        out_specs=[pl.BlockSpec((1, tk, HEAD_PAD), lambda h, j: (h, j, 0)),
                   pl.BlockSpec((1, tk, V_HEAD), lambda h, j: (h, j, 0)),
                   pl.BlockSpec(memory_space=pl.ANY)],
        out_shape=[jax.ShapeDtypeStruct((N_HEADS, n, HEAD_PAD), F32), jax.ShapeDtypeStruct((N_HEADS, n, V_HEAD), F32),
                   jax.ShapeDtypeStruct((N_HEADS, n, HEAD_PAD), F32)],
        scratch_shapes=[pltpu.VMEM((n, HEAD_PAD), F32), pltpu.VMEM((4, 2, tq, tq), BF16), pltpu.VMEM((2, tk, tq), F32),
                        pltpu.SemaphoreType.DMA(()), pltpu.SemaphoreType.DMA((4,))],
        compiler_params=_params("arbitrary", "arbitrary"),
    )(k, v, q, dpo, lse, delta, m, p)


def _mla_pre_bwd(x, dout, dq, dk, dv, gain, win, gq, gkv, wq, wkv, ghq, ghk, cos, sin_signed, name):
    n = x.shape[0]
    tm = _tiles(n)["mla_bwd"]
    hw = N_HEADS * HEAD_PAD

    def body(x_ref, do_ref, dq_ref, dk_ref, dv_ref, g_ref, win_ref, gq_ref, gkv_ref, wq_ref, wkv_ref, ghq_ref, ghk_ref,
             c_ref, s_ref, dx_ref, h_ref, dlat_ref, cqn_ref, dqp_ref, ckvn_ref, dkv_ref,
             dg_ref, dgq_ref, dgkv_ref, dghq_ref, dghk_ref):
        @pl.when(pl.program_id(0) == 0)
        def _():
            for ref in (dg_ref, dgq_ref, dgkv_ref, dghq_ref, dghk_ref):
                ref[...] = jnp.zeros_like(ref)

        g = g_ref[...]
        xhat, r = _rms_fwd(x_ref[...])
        h = (xhat * g).astype(BF16)
        h_ref[...] = h
        cq, ckv, kpe = _mla_latents(h, win_ref)
        cqhat, rcq = _rms_fwd(cq)
        ckvhat, rckv = _rms_fwd(ckv)
        cqn = (cqhat * gq_ref[...]).astype(BF16)
        ckvn = (ckvhat * gkv_ref[...]).astype(BF16)
        cqn_ref[...] = cqn
        ckvn_ref[...] = ckvn
        cos, sn = c_ref[...], s_ref[...]
        ghq, ghk = ghq_ref[...], ghk_ref[...]
        dkpe = jnp.zeros((tm, 128), F32)
        dghq = jnp.zeros((8, HEAD_PAD), F32)
        dghk = jnp.zeros((8, HEAD_PAD), F32)
        for hd in range(N_HEADS):
            rws = pl.ds(hd * HEAD_PAD, HEAD_PAD)
            lanes = slice(hd * HEAD_PAD, (hd + 1) * HEAD_PAD)
            qhat, rq = _rms_fwd(_dot_nt(cqn, wq_ref[rws, :]), QK_HEAD)
            gqn = _rope_bwd(dq_ref[hd], cos, sn)
            dghq = dghq + _rowsum8(gqn * qhat)
            dqpre = _rms_bwd(gqn, qhat, rq, ghq, QK_HEAD).astype(BF16)
            dqp_ref[:, lanes] = dqpre
            kvh = _dot_nt(ckvn, wkv_ref[rws, :])
            khat, rk = _rms_fwd(jnp.concatenate([kvh[:, :QK_NOPE], kpe], axis=1), QK_HEAD)
            gkn = _rope_bwd(dk_ref[hd], cos, sn)
            dghk = dghk + _rowsum8(gkn * khat)
            dkpre = _rms_bwd(gkn, khat, rk, ghk, QK_HEAD)
            dkpe = dkpe + dkpre[:, QK_NOPE:]
            dkvh = jnp.concatenate([dkpre[:, :QK_NOPE], dv_ref[hd]], axis=1).astype(BF16)
            dkv_ref[:, lanes] = dkvh
        dcqn = _dot(dqp_ref[...], wq_ref[...])
        dckvn = _dot(dkv_ref[...], wkv_ref[...])
        dghq_ref[...] += dghq
        dghk_ref[...] += dghk
        dgq_ref[...] += _rowsum8(dcqn * cqhat)
        dgkv_ref[...] += _rowsum8(dckvn * ckvhat)
        dlat = jnp.concatenate([_rms_bwd(dcqn, cqhat, rcq, gq_ref[...]), _rms_bwd(dckvn, ckvhat, rckv, gkv_ref[...]),
                                dkpe], axis=1).astype(BF16)
        dlat_ref[...] = dlat
        dh = _dot(dlat, win_ref[...])
        dg_ref[...] += _rowsum8(dh * xhat)
        dx_ref[...] = do_ref[...] + _rms_bwd(dh, xhat, r, g)

    row = lambda w: pl.BlockSpec((tm, w), lambda i: (i, 0))
    head = lambda w: pl.BlockSpec((N_HEADS, tm, w), lambda i: (0, i, 0))
    sds = jax.ShapeDtypeStruct
    return pl.pallas_call(
        body, name=name, grid=(n // tm,),
        in_specs=[row(D_MODEL), row(D_MODEL), head(HEAD_PAD), head(HEAD_PAD), head(V_HEAD), _full((1, D_MODEL)),
                  _full((LAT_PAD, D_MODEL)), _full((1, Q_LORA)), _full((1, KV_LORA)), _full((hw, Q_LORA)),
                  _full((hw, KV_LORA)), _full((1, HEAD_PAD)), _full((1, HEAD_PAD)), row(128), row(128)],
        out_specs=[row(D_MODEL), row(D_MODEL), row(LAT_PAD), row(Q_LORA), row(hw), row(KV_LORA), row(hw),
                   _full((8, D_MODEL)), _full((8, Q_LORA)), _full((8, KV_LORA)), _full((8, HEAD_PAD)),
                   _full((8, HEAD_PAD))],
        out_shape=[sds((n, D_MODEL), F32), sds((n, D_MODEL), BF16), sds((n, LAT_PAD), BF16), sds((n, Q_LORA), BF16),
                   sds((n, hw), BF16), sds((n, KV_LORA), BF16), sds((n, hw), BF16), sds((8, D_MODEL), F32),
                   sds((8, Q_LORA), F32), sds((8, KV_LORA), F32), sds((8, HEAD_PAD), F32), sds((8, HEAD_PAD), F32)],
        compiler_params=_params("arbitrary"),
    )(x, dout, dq, dk, dv, gain, win, gq, gkv, wq, wkv, ghq, ghk, cos, sin_signed)


def _adamw(w, g, m, v, behind, name):
    rows, cols = w.shape
    tr = rows
    for cand in (512, 256, 128, 64, 32, 16, 8):
        if rows % cand == 0 and rows > cand:
            tr = cand
            break

    def body(w_ref, g_ref, m_ref, v_ref, behind_ref, d_ref, mo_ref, vo_ref):
        gv = g_ref[...]
        mn = ADAM_B1 * m_ref[...] + (1.0 - ADAM_B1) * gv
        vn = ADAM_B2 * v_ref[...] + (1.0 - ADAM_B2) * (gv * gv)
        m_hat = mn / (1.0 - ADAM_B1 ** ADAM_STEP)
        v_hat = vn / (1.0 - ADAM_B2 ** ADAM_STEP)
        d_ref[...] = -ADAM_LR * (m_hat / (jnp.sqrt(v_hat) + ADAM_EPS) + ADAM_WD * w_ref[...])
        mo_ref[...] = mn
        vo_ref[...] = vn

    blk = pl.BlockSpec((tr, cols), lambda i: (i, 0))
    return pl.pallas_call(
        body, name=name, grid=(rows // tr,), in_specs=[blk] * 4 + [pl.BlockSpec(memory_space=pl.ANY)],
        out_specs=[blk] * 3, out_shape=[jax.ShapeDtypeStruct((rows, cols), F32)] * 3,
        compiler_params=_params("arbitrary"),
    )(w, g, m, v, behind)


def _sum_parts(parts, name):
    k, r, c = parts.shape
    tr = min(r, 512)

    def body(p_ref, o_ref):
        acc = p_ref[0]
        for j in range(1, k):
            acc = acc + p_ref[j]
        o_ref[...] = acc

    return pl.pallas_call(
        body, name=name, grid=(r // tr,), in_specs=[pl.BlockSpec((k, tr, c), lambda i: (0, i, 0))],
        out_specs=pl.BlockSpec((tr, c), lambda i: (i, 0)), out_shape=jax.ShapeDtypeStruct((r, c), parts.dtype),
        compiler_params=_params("arbitrary"),
    )(parts)


def _row_tile(r, most=256):
    best = r
    for cand in range(8, most + 1, 8):
        if r % cand == 0:
            best = cand
    return best


def _sum_exchange(mine, landed, me, name):
    _, r, c = mine.shape
    tr = _row_tile(r)

    def body(me_ref, m_ref, l_ref, o_ref):
        acc = m_ref[0]
        for k in range(1, N_DEV):
            acc = acc + l_ref[k]
        o_ref[...] = acc

    return pl.pallas_call(
        body, name=name,
        grid_spec=pltpu.PrefetchScalarGridSpec(
            num_scalar_prefetch=1, grid=(r // tr,),
            in_specs=[pl.BlockSpec((1, tr, c), lambda i, me_ref: (me_ref[0], i, 0)),
                      pl.BlockSpec((N_DEV, tr, c), lambda i, me_ref: (0, i, 0))],
            out_specs=pl.BlockSpec((tr, c), lambda i, me_ref: (i, 0))),
        out_shape=jax.ShapeDtypeStruct((r, c), mine.dtype), compiler_params=_params("arbitrary"),
    )(me, mine, landed)


MESH = pl.DeviceIdType.MESH


def _all_gather(x, name):
    r, c = x.shape

    def body(x_ref, out_ref, send_sems, recv_sems, local_sem):
        mx, my, mc = lax.axis_index("x"), lax.axis_index("y"), lax.axis_index("c")
        me, sibling = (mx, my, mc), (mx, my, 1 - mc)
        chips = [(1 - mx, my), (mx, 1 - my), (1 - mx, 1 - my)]

        def slot(px, py, pc):
            return out_ref.at[4 * px + 2 * py + pc]

        def copy(k, block, to, src=None):
            return pltpu.make_async_remote_copy(
                src_ref=slot(*block) if src is None else src, dst_ref=slot(*block),
                send_sem=send_sems.at[k], recv_sem=recv_sems.at[k], device_id=to, device_id_type=MESH)

        mine = pltpu.make_async_copy(x_ref, slot(*me), local_sem)
        mine.start()
        first = [copy(0, me, sibling, src=x_ref)]
        first += [copy(1 + j, me, (*chip, mc), src=x_ref) for j, chip in enumerate(chips)]
        for cp in first:
            cp.start()
        passed = [copy(4 + j, (*chip, mc), sibling) for j, chip in enumerate(chips)]
        for j, chip in enumerate(chips):
            copy(1 + j, (*chip, mc), me).wait_recv()
            passed[j].start()
        copy(0, sibling, me).wait_recv()
        for j, chip in enumerate(chips):
            copy(4 + j, (*chip, 1 - mc), me).wait_recv()
        for cp in first + passed:
            cp.wait_send()
        mine.wait()

    any_spec = pl.BlockSpec(memory_space=pl.ANY)
    return pl.pallas_call(
        body, name=name, in_specs=[any_spec], out_specs=any_spec,
        out_shape=jax.ShapeDtypeStruct((N_DEV, r, c), x.dtype),
        scratch_shapes=[pltpu.SemaphoreType.DMA((7,)), pltpu.SemaphoreType.DMA((7,)), pltpu.SemaphoreType.DMA(())],
    )(x)


HBM_SPEC = pl.BlockSpec(memory_space=pltpu.HBM)
SEM_SPEC = pl.BlockSpec(memory_space=pltpu.SEMAPHORE)
SPLIT_EFFECT = pltpu.SideEffectType.DATAFLOW_SIDE_EFFECTING


def _exchange_copies(src_ref, land_ref, send_sems, recv_sems, gather):
    mx, my, mc = lax.axis_index("x"), lax.axis_index("y"), lax.axis_index("c")
    me = 4 * mx + 2 * my + mc
    copies = []
    for k in range(1, N_DEV):
        px = 1 - mx if k & 4 else mx
        py = 1 - my if k & 2 else my
        pc = 1 - mc if k & 1 else mc
        src = src_ref if gather else src_ref.at[4 * px + 2 * py + pc]
        dst = land_ref.at[me] if gather else land_ref.at[k]
        copies.append(pltpu.make_async_remote_copy(
            src_ref=src, dst_ref=dst, send_sem=send_sems.at[k - 1], recv_sem=recv_sems.at[k - 1],
            device_id=(px, py, pc), device_id_type=MESH))
    return copies


def _exchange_start(src, after, gather, name):
    land_shape = (N_DEV,) + src.shape[-2:]

    def body(src_ref, land_ref, after_ref, send_sems, recv_sems, src_thru, land_thru, token):
        for cp in _exchange_copies(src_ref, land_ref, send_sems, recv_sems, gather):
            cp.start()
        token[...] = jnp.zeros_like(token)

    return pl.pallas_call(
        body, name=name,
        out_shape=(pltpu.SemaphoreType.DMA((N_DEV - 1,)), pltpu.SemaphoreType.DMA((N_DEV - 1,)),
                   pltpu.HBM(src.shape, src.dtype), pltpu.HBM(land_shape, src.dtype),
                   jax.ShapeDtypeStruct((8, 128), F32)),
        in_specs=(HBM_SPEC, HBM_SPEC, pl.BlockSpec(memory_space=pl.ANY)),
        out_specs=(SEM_SPEC, SEM_SPEC, HBM_SPEC, HBM_SPEC, pl.BlockSpec(memory_space=pltpu.VMEM)),
        input_output_aliases={0: 2, 1: 3},
        compiler_params=pltpu.CompilerParams(has_side_effects=SPLIT_EFFECT),
    )(pltpu.with_memory_space_constraint(src, pltpu.HBM),
      pltpu.with_memory_space_constraint(lax.empty(land_shape, src.dtype), pltpu.HBM), after)


def _exchange_wait(started, after, gather, name):
    send_sems, recv_sems, src_thru, land_thru, _ = started

    def body(src_ref, land_ref, send_sems, recv_sems, after_ref, src_out, land_out):
        for cp in _exchange_copies(src_ref, land_ref, send_sems, recv_sems, gather):
            cp.wait_send()
            cp.wait_recv()

    return pl.pallas_call(
        body, name=name,
        out_shape=(pltpu.HBM(src_thru.shape, src_thru.dtype), pltpu.HBM(land_thru.shape, land_thru.dtype)),
        in_specs=(HBM_SPEC, HBM_SPEC, SEM_SPEC, SEM_SPEC, pl.BlockSpec(memory_space=pl.ANY)),
        out_specs=(HBM_SPEC, HBM_SPEC), input_output_aliases={0: 0, 1: 1},
        compiler_params=pltpu.CompilerParams(has_side_effects=SPLIT_EFFECT),
    )(src_thru, land_thru, send_sems, recv_sems, after)


FFN_ROWS = D_FF // N_DEV
WIN_ROWS = (Q_LORA + KV_LORA + QK_ROPE) // N_DEV
WIN_ROWS_PAD = 144
WQ_ROWS = QK_HEAD * Q_LORA // D_MODEL
WKV_ROWS = 256 * KV_LORA // D_MODEL
WOUT_ROWS = V_HEAD
POOL_ROWS = 4 * 32 * POOL_GROUP // D_MODEL


COLUMN_SHARDED = ("ffn1_w_gate", "ffn1_w_up", "ffn2_w_gate", "ffn2_w_up", "mla_w_in", "mla_w_q_up", "mla_w_kv_up")


def _t(w):
    return jnp.swapaxes(w, -1, -2)


def _ffn_segments(f, i):
    return [(f + "_w_gate", i, FFN_ROWS), (f + "_w_up", i, FFN_ROWS), (f + "_w_down", i, FFN_ROWS)]


def _mixer_segments(i):
    j = i // 2
    if i % 2 == 0:
        return [("pool_w", j, POOL_ROWS)]
    return [("mla_w_in", j, WIN_ROWS_PAD), ("mla_w_q_up", j, WQ_ROWS), ("mla_w_kv_up", j, WKV_ROWS),
            ("mla_w_out", j, WOUT_ROWS)]


def _pack_shards(p, segs, dtype):
    parts = []
    for name, idx, _ in segs:
        w = p[name][idx]
        if name.endswith("w_gate") or name.endswith("w_up"):
            w = _t(w)
        elif name == "mla_w_in":
            w = jnp.pad(_t(w), ((0, WIN_ROWS_PAD - WIN_ROWS), (0, 0)))
        elif name == "mla_w_q_up":
            w = _t(w).reshape(WQ_ROWS, D_MODEL)
        elif name == "mla_w_kv_up":
            w = _t(w).reshape(WKV_ROWS, D_MODEL)
        elif name == "pool_w":
            w = w.reshape(POOL_ROWS, D_MODEL)
        parts.append(w.astype(dtype))
    return jnp.concatenate(parts, axis=0)


def _unpack_gathered(g, segs):
    out = {}
    off = 0
    for name, layer, rows in segs:
        seg = g[:, off:off + rows, :]
        off += rows
        if name == "mla_w_in":
            w = seg[:, :WIN_ROWS].reshape(N_DEV * WIN_ROWS, D_MODEL)
            w = jnp.pad(w, ((0, LAT_PAD - N_DEV * WIN_ROWS), (0, 0)))
        elif name == "mla_w_q_up":
            w = seg.reshape(N_HEADS, QK_HEAD, Q_LORA)
            w = jnp.pad(w, ((0, 0), (0, HEAD_PAD - QK_HEAD), (0, 0))).reshape(N_HEADS * HEAD_PAD, Q_LORA)
        elif name == "mla_w_kv_up":
            w = seg.reshape(N_HEADS * 256, KV_LORA)
        elif name == "pool_w":
            w = seg.reshape(N_DEV, 4, 32, POOL_GROUP).transpose(1, 0, 2, 3).reshape(4, POOL_GROUP, POOL_GROUP)
        else:
            w = seg.reshape(N_DEV * rows, D_MODEL)
        out[(name, layer)] = w
    return out


def _pack_grads(gr, segments):
    segs = []
    for name, layer, rows in segments:
        g = gr[(name, layer)]
        if name == "mla_w_in":
            g = g[:N_DEV * WIN_ROWS].reshape(N_DEV, WIN_ROWS, D_MODEL)
            g = jnp.pad(g, ((0, 0), (0, WIN_ROWS_PAD - WIN_ROWS), (0, 0)))
        elif name == "mla_w_q_up":
            g = g.reshape(N_HEADS, HEAD_PAD, Q_LORA)[:, :QK_HEAD].reshape(N_DEV, WQ_ROWS, D_MODEL)
        elif name == "mla_w_kv_up":
            g = g.reshape(N_DEV, WKV_ROWS, D_MODEL)
        elif name == "pool_w":
            g = g.reshape(4, N_DEV, 32, POOL_GROUP).transpose(1, 0, 2, 3).reshape(N_DEV, POOL_ROWS, D_MODEL)
        else:
            g = g.reshape(N_DEV, rows, D_MODEL)
        segs.append(g)
    return jnp.concatenate(segs, axis=1)


def _unpack_shard_grads(flat, segments):
    per = {}
    off = 0
    for name, layer, rows in segments:
        seg = flat[off:off + rows]
        off += rows
        if name == "mla_w_in":
            g = seg[:WIN_ROWS]
        elif name == "mla_w_q_up":
            g = seg.reshape(QK_HEAD, Q_LORA)
        elif name == "mla_w_kv_up":
            g = seg.reshape(256, KV_LORA)
        elif name == "pool_w":
            g = seg.reshape(4, 32, POOL_GROUP)
        else:
            g = seg
        per[(name, layer)] = g
    return per


def _pad_lanes(v, width):
    return jnp.pad(v, ((0, 0), (0, width - v.shape[-1])))


def kernel(x, positions, ffn1_norm, ffn1_w_gate, ffn1_w_up, ffn1_w_down, mix_norm, pool_w, pool_scale, mla_w_in, mla_q_norm, mla_w_q_up, mla_kv_norm, mla_w_kv_up, mla_q_head_norm, mla_k_head_norm, mla_w_out, ffn2_norm, ffn2_w_gate, ffn2_w_up, ffn2_w_down, loss_target, m_ffn1_norm, m_ffn1_w_gate, m_ffn1_w_up, m_ffn1_w_down, m_mix_norm, m_pool_w, m_pool_scale, m_mla_w_in, m_mla_q_norm, m_mla_w_q_up, m_mla_kv_norm, m_mla_w_kv_up, m_mla_q_head_norm, m_mla_k_head_norm, m_mla_w_out, m_ffn2_norm, m_ffn2_w_gate, m_ffn2_w_up, m_ffn2_w_down, v_ffn1_norm, v_ffn1_w_gate, v_ffn1_w_up, v_ffn1_w_down, v_mix_norm, v_pool_w, v_pool_scale, v_mla_w_in, v_mla_q_norm, v_mla_w_q_up, v_mla_kv_norm, v_mla_w_kv_up, v_mla_q_head_norm, v_mla_k_head_norm, v_mla_w_out, v_ffn2_norm, v_ffn2_w_gate, v_ffn2_w_up, v_ffn2_w_down):
    args = (x, positions, ffn1_norm, ffn1_w_gate, ffn1_w_up, ffn1_w_down, mix_norm, pool_w, pool_scale, mla_w_in,
            mla_q_norm, mla_w_q_up, mla_kv_norm, mla_w_kv_up, mla_q_head_norm, mla_k_head_norm, mla_w_out, ffn2_norm,
            ffn2_w_gate, ffn2_w_up, ffn2_w_down)
    p = dict(zip(NAMES, args))
    moments_m = dict(zip(WEIGHTS, (m_ffn1_norm, m_ffn1_w_gate, m_ffn1_w_up, m_ffn1_w_down, m_mix_norm, m_pool_w, m_pool_scale, m_mla_w_in, m_mla_q_norm, m_mla_w_q_up, m_mla_kv_norm, m_mla_w_kv_up, m_mla_q_head_norm, m_mla_k_head_norm, m_mla_w_out, m_ffn2_norm, m_ffn2_w_gate, m_ffn2_w_up, m_ffn2_w_down)))
    moments_v = dict(zip(WEIGHTS, (v_ffn1_norm, v_ffn1_w_gate, v_ffn1_w_up, v_ffn1_w_down, v_mix_norm, v_pool_w, v_pool_scale, v_mla_w_in, v_mla_q_norm, v_mla_w_q_up, v_mla_kv_norm, v_mla_w_kv_up, v_mla_q_head_norm, v_mla_k_head_norm, v_mla_w_out, v_ffn2_norm, v_ffn2_w_gate, v_ffn2_w_up, v_ffn2_w_down)))
    dev = 4 * lax.axis_index("x") + 2 * lax.axis_index("y") + lax.axis_index("c")

    xs = x[0]
    n = xs.shape[0]
    target = loss_target[0]

    inv_freq = 1.0 / (ROPE_THETA ** (jnp.arange(0, QK_ROPE, 2, dtype=F32) / QK_ROPE))
    ang = positions[0].astype(F32)[..., None] * inv_freq
    cos, sin = jnp.cos(ang), jnp.sin(ang)
    zero = jnp.zeros((n, 128 - QK_ROPE), F32)
    rope_cos = jnp.concatenate([cos, cos, zero], axis=1)
    rope_sin = jnp.concatenate([-sin, sin, zero], axis=1)

    ag_groups = [_ffn_segments("ffn1", 0), _mixer_segments(0) + _ffn_segments("ffn2", 0)]
    ag_groups += [_ffn_segments("ffn1", i) + _mixer_segments(i) + _ffn_segments("ffn2", i) for i in range(1, DEPTH)]
    shards = [_pack_shards(p, segs, BF16) for segs in ag_groups]
    w = {}
    no_token = jnp.zeros((8, 128), F32)

    def tied(gain, token):
        return gain + token[0, 0]

    gains_local = jnp.concatenate([_pad_lanes(mla_q_norm, 128), _pad_lanes(mla_kv_norm, 128)], axis=0)
    gains_all = _all_gather(jnp.pad(gains_local, ((0, 4), (0, 0))), "gains_all_gather")
    q_norm_full = gains_all[:, 0:2, :Q_LORA // N_DEV].transpose(1, 0, 2).reshape(2, Q_LORA)
    kv_norm_full = gains_all[:, 2:4, :KV_LORA // N_DEV].transpose(1, 0, 2).reshape(2, KV_LORA)
    ghq = _pad_lanes(mla_q_head_norm, HEAD_PAD)
    ghk = _pad_lanes(mla_k_head_norm, HEAD_PAD)

    def mla_weights(j):
        return (w[("mla_w_in", j)], q_norm_full[j:j + 1], kv_norm_full[j:j + 1], w[("mla_w_q_up", j)],
                w[("mla_w_kv_up", j)], ghq[j:j + 1], ghk[j:j + 1], rope_cos, rope_sin)

    saved = []
    cur = xs
    gathers = [_exchange_start(shards[0], gains_all, True, "ag_start_0")]

    def gather_step(after):
        g = len(gathers) - 1
        _, landed = _exchange_wait(gathers[g], after, True, f"ag_wait_{g}")
        w.update(_unpack_gathered(lax.dynamic_update_slice(landed, shards[g][None], (dev, 0, 0)), ag_groups[g]))
        if g + 1 == len(ag_groups):
            return no_token
        gathers.append(_exchange_start(shards[g + 1], landed, True, f"ag_start_{g + 1}"))
        return gathers[-1][4]

    for i in range(DEPTH):
        j = i // 2
        st = {"x0": cur}
        token = gather_step(cur)
        cur, st["a1"], st["u1"] = _ffn_fwd(cur, tied(ffn1_norm[i:i + 1], token), w[("ffn1_w_gate", i)],
                                           w[("ffn1_w_up", i)], w[("ffn1_w_down", i)], f"ffn1_fwd_{i}")
        st["x1"] = cur
        if i == 0:
            token = gather_step(cur)
        if i % 2 == 0:
            cur = _pool_fwd(cur, tied(mix_norm[i:i + 1], token), w[("pool_w", j)], pool_scale[j:j + 1],
                            f"pool_fwd_{i}")
        else:
            st["q"], st["k"], st["v"], vt = _mla_pre_fwd(cur, mix_norm[i:i + 1], *mla_weights(j), f"mla_pre_fwd_{i}")
            st["o"], lse, st["m"], st["p"] = _flash_fwd(st["q"], st["k"], vt, f"flash_fwd_{i}")
            tq_bwd = _tiles(n)["bwd_q"]
            st["lse"] = lse.reshape(N_HEADS, n // tq_bwd, 1, tq_bwd)
            cur = _mla_post_fwd(st["o"], cur, w[("mla_w_out", j)], f"mla_post_fwd_{i}")
        st["x2"] = cur
        cur, st["a2"], st["u2"] = _ffn_fwd(cur, ffn2_norm[i:i + 1], w[("ffn2_w_gate", i)], w[("ffn2_w_up", i)],
                                           w[("ffn2_w_down", i)], f"ffn2_fwd_{i}")
        saved.append(st)

    dcur, sq_err = _loss_head(cur, target, "loss_head")
    loss_part = 0.5 * jnp.sum(sq_err) * (1.0 / D_MODEL)

    gr = {}
    small = {k: [None] * DEPTH for k in ("ffn1_norm", "mix_norm", "ffn2_norm")}
    small.update({k: [None] * (DEPTH // 2) for k in ("pool_scale", "mla_q_norm", "mla_kv_norm", "mla_q_head_norm",
                                                     "mla_k_head_norm")})

    me = jnp.reshape(dev, (1,)).astype(jnp.int32)
    grads = {}
    in_flight = []

    def reduce_start(segs, tag):
        started = _exchange_start(_pack_grads(gr, segs), dcur, False, f"rs_start_{tag}")
        reduce_finish(started[4])
        in_flight.append((started, segs, tag))
        return started[4]

    def reduce_finish(after):
        if in_flight:
            started, segs, tag = in_flight.pop()
            mine, landed = _exchange_wait(started, after, False, f"rs_wait_{tag}")
            grads.update(_unpack_shard_grads(_sum_exchange(mine, landed, me, f"rs_sum_{tag}"), segs))

    def ffn_backward(f, i, x_in, a, u, gain, dout):
        dx, h, dob, y, da, du, dg = _ffn_bwd(x_in, dout, a, u, gain, w[(f + "_w_gate", i)], w[(f + "_w_up", i)],
                                             w[(f + "_w_down", i)], f"{f}_bwd_{i}")
        gr[(f + "_w_gate", i)] = _tn_matmul(da, h, f"{f}_dgate_{i}")
        gr[(f + "_w_up", i)] = _tn_matmul(du, h, f"{f}_dup_{i}")
        gr[(f + "_w_down", i)] = _tn_matmul(y, dob, f"{f}_ddown_{i}")
        small[f + "_norm"][i] = jnp.sum(dg, axis=0)
        return dx

    token = jnp.zeros((8, 128), F32)
    for i in reversed(range(DEPTH)):
        j = i // 2
        st = saved[i]
        dcur = ffn_backward("ffn2", i, st["x2"], st["a2"], st["u2"], tied(ffn2_norm[i:i + 1], token), dcur)
        if i % 2 == 0:
            dcur, dpw, dsc, dg = _pool_bwd(st["x1"], dcur, mix_norm[i:i + 1], w[("pool_w", j)], pool_scale[j:j + 1],
                                           f"pool_bwd_{i}")
            gr[("pool_w", j)] = dpw
            small["pool_scale"][j] = jnp.sum(dsc, axis=0)
            small["mix_norm"][i] = jnp.sum(dg, axis=0)
        else:
            dob, dpo, delta = _mla_out_bwd(dcur, st["o"], w[("mla_w_out", j)], f"mla_out_bwd_{i}")
            gr[("mla_w_out", j)] = _tn_matmul(st["o"], dob, f"mla_dout_{i}")
            dk, dv, dq = _flash_bwd(st["q"], st["k"], st["v"], dpo, st["lse"], delta, st["p"], st["m"],
                                    f"flash_bwd_{i}")
            (dcur, h, dlat, cqn, dqp, ckvn, dkv, dg, dgq, dgkv, dghq, dghk) = _mla_pre_bwd(
                st["x1"], dcur, dq, dk, dv, mix_norm[i:i + 1], *mla_weights(j), f"mla_pre_bwd_{i}")
            gr[("mla_w_in", j)] = _tn_matmul(dlat, h, f"mla_din_{i}")
            gr[("mla_w_q_up", j)] = _tn_matmul(dqp, cqn, f"mla_dqup_{i}")
            gr[("mla_w_kv_up", j)] = _tn_matmul(dkv, ckvn, f"mla_dkvup_{i}")
            small["mix_norm"][i] = jnp.sum(dg, axis=0)
            small["mla_q_norm"][j] = _pad_lanes(jnp.sum(dgq, axis=0)[None], D_MODEL)[0]
            small["mla_kv_norm"][j] = _pad_lanes(jnp.sum(dgkv, axis=0)[None], D_MODEL)[0]
            small["mla_q_head_norm"][j] = _pad_lanes(jnp.sum(dghq, axis=0)[None], D_MODEL)[0]
            small["mla_k_head_norm"][j] = _pad_lanes(jnp.sum(dghk, axis=0)[None], D_MODEL)[0]
        token = reduce_start(_ffn_segments("ffn2", i) + _mixer_segments(i), f"a{i}")
        if i > 0:
            dcur = ffn_backward("ffn1", i, st["x0"], st["a1"], st["u1"], tied(ffn1_norm[i:i + 1], token), dcur)
            token = reduce_start(_ffn_segments("ffn1", i), f"b{i}")
    st = saved[0]
    dcur, h, dob, y, da, du, dg = _ffn_bwd(st["x0"], dcur, st["a1"], st["u1"], tied(ffn1_norm[0:1], token),
                                           w[("ffn1_w_gate", 0)], w[("ffn1_w_up", 0)], w[("ffn1_w_down", 0)],
                                           "ffn1_bwd_0")
    small["ffn1_norm"][0] = jnp.sum(dg, axis=0)
    grad_x = dcur[None]
    small_order = ("ffn1_norm", "mix_norm", "ffn2_norm", "pool_scale", "mla_q_norm", "mla_kv_norm",
                   "mla_q_head_norm", "mla_k_head_norm")
    rows = [r for k in small_order for r in small[k]]
    rows.append(jnp.zeros((D_MODEL,), F32).at[0].set(loss_part))
    rows.append(jnp.zeros((D_MODEL,), F32))
    small_sum = _sum_parts(_all_gather(jnp.stack(rows), "small_all_gather"), "small_sum")
    loss = small_sum[SM_ROWS - 2, 0]
    token = small_sum
    for seg, lhs, rhs in zip(_ffn_segments("ffn1", 0), (da, du, y), (h, h, dob)):
        gr[seg[:2]] = _tn_matmul(lhs, rhs, f"ffn1_d{seg[0][7:]}_0", behind=token)
        token = reduce_start([seg], f"b0_{seg[0][7:]}")
    last = "ffn1_w_down"

    counts = {k: (DEPTH // 2 if k.startswith(("mla", "pool")) else DEPTH) for k in WEIGHTS}

    def stacked(k):
        return jnp.stack([grads[(k, idx)] for idx in range(counts[k])])

    grads.update({k: stacked(k) for k in WEIGHTS if (k, 0) in grads and k != last})
    off = 0
    for k in small_order:
        cnt = len(small[k])
        g = small_sum[off:off + cnt]
        off += cnt
        if k == "mla_q_norm":
            g = lax.dynamic_slice_in_dim(g[:, :Q_LORA], dev * (Q_LORA // N_DEV), Q_LORA // N_DEV, axis=1)
        elif k == "mla_kv_norm":
            g = lax.dynamic_slice_in_dim(g[:, :KV_LORA], dev * (KV_LORA // N_DEV), KV_LORA // N_DEV, axis=1)
        elif k in ("mla_q_head_norm", "mla_k_head_norm"):
            g = g[:, :QK_HEAD]
        grads[k] = g

    deltas, new_m, new_v = {}, {}, {}

    def update(k, behind):
        to_view = _t if k in COLUMN_SHARDED else (lambda a: a)
        shape = to_view(p[k]).shape
        view = (-1, shape[-1])
        d, mn, vn = _adamw(to_view(p[k]).reshape(view), grads[k].reshape(view), to_view(moments_m[k]).reshape(view),
                           to_view(moments_v[k]).reshape(view), behind, "adamw_" + k)
        deltas[k], new_m[k], new_v[k] = (to_view(a.reshape(shape)) for a in (d, mn, vn))
        grads[k] = to_view(grads[k].reshape(shape))
        return d

    done = token
    for k in WEIGHTS:
        if k != last:
            done = update(k, done)
    reduce_finish(done)
    grads[last] = stacked(last)
    update(last, done)

    return (loss, grad_x, *[grads[k] for k in WEIGHTS], *[deltas[k] for k in WEIGHTS],
            *[new_m[k] for k in WEIGHTS], *[new_v[k] for k in WEIGHTS])
```

```python
import functools

import jax
import jax.numpy as jnp
from jax import lax
from jax.experimental import pallas as pl
from jax.experimental.pallas import tpu as pltpu

F32 = jnp.float32
BF16 = jnp.bfloat16

D_MODEL = 1024
DEPTH = 4
D_FF = 2816
POOL_WINDOWS = (2, 4, 8, 16)
POOL_GROUP = 256
POOL_HALO = 16
N_HEADS = 8
QK_NOPE = 128
QK_ROPE = 64
QK_HEAD = 192
V_HEAD = 128
Q_LORA = 768
KV_LORA = 256
ROPE_THETA = 10000.0
EPS = 1e-6
FFN_HALF = 0.5
ADAM_LR = 0.001
ADAM_B1 = 0.9
ADAM_B2 = 0.999
ADAM_EPS = 1e-08
ADAM_WD = 0.01
ADAM_STEP = 10

N_DEV = 8
HEAD_PAD = 256
LAT_PAD = 1152
VT_ROWS = 144
LOG2_E = 1.4426950408889634
ATTN_SCALE = QK_HEAD ** -0.5
LOGIT_SCALE = ATTN_SCALE * LOG2_E
V7X_VMEM_LIMIT = 56 * 1024 * 1024
FF_CHUNK = 256
SM_ROWS = 24

NAMES = ['x', 'positions', 'ffn1_norm', 'ffn1_w_gate', 'ffn1_w_up', 'ffn1_w_down', 'mix_norm', 'pool_w',
         'pool_scale', 'mla_w_in', 'mla_q_norm', 'mla_w_q_up', 'mla_kv_norm', 'mla_w_kv_up', 'mla_q_head_norm',
         'mla_k_head_norm', 'mla_w_out', 'ffn2_norm', 'ffn2_w_gate', 'ffn2_w_up', 'ffn2_w_down']
WEIGHTS = NAMES[2:]


def _tiles(n):
    return dict(ffn_fwd=min(512, n), ffn_bwd=min(256, n), fwd_k=min(512, n // 2), bwd_q=min(512, n // 2),
                mla_bwd=min(256, n), pool=min(512, n), tn=min(2048, n), rows=min(1024, n))


def _params(*sem):
    return pltpu.CompilerParams(dimension_semantics=sem, vmem_limit_bytes=V7X_VMEM_LIMIT)


def _dot(a, b):
    return jnp.dot(a, b, preferred_element_type=F32)


def _dot_nt(a, b):
    return lax.dot_general(a, b, (((1,), (1,)), ((), ())), preferred_element_type=F32)


def _dot_tn(a, b):
    return lax.dot_general(a, b, (((0,), (0,)), ((), ())), preferred_element_type=F32)


def _rowsum8(v):
    rows, w = v.shape
    return jnp.sum(v.reshape(rows // 8, 8, w), axis=0)


def _sigmoid(a):
    return 1.0 / (1.0 + jnp.exp(-a))


def _lane_sum(v):
    f = v[:, :128]
    for t in range(1, v.shape[1] // 128):
        f = f + v[:, t * 128:(t + 1) * 128]
    hi = f.astype(BF16)
    lo = (f - hi.astype(F32)).astype(BF16)
    return _dot(jnp.concatenate([hi, lo], axis=1), jnp.ones((256, 128), BF16))


def _by_row(v, r):
    return jnp.concatenate([v[:, t * 128:(t + 1) * 128] * r for t in range(v.shape[1] // 128)], axis=1)


def _rms_fwd(x, width=None):
    width = x.shape[-1] if width is None else width
    r = lax.rsqrt(_lane_sum(x * x) * (1.0 / width) + EPS)
    return _by_row(x, r), r


def _rms_bwd(dy, xhat, r, gain, width=None):
    width = xhat.shape[-1] if width is None else width
    t = dy * gain
    return _by_row(t - _by_row(xhat, _lane_sum(t * xhat) * (1.0 / width)), r)


def _full(shape):
    return pl.BlockSpec(shape, lambda *_: (0,) * len(shape))


def _load_weights(gathered, offsets, dsts, sems):
    rows = D_FF // N_DEV
    copies = [pltpu.make_async_copy(gathered.at[j, pl.ds(off, rows), :], dst.at[pl.ds(j * rows, rows), :],
                                    sems.at[N_DEV * k + j])
              for k, (off, dst) in enumerate(zip(offsets, dsts)) for j in range(N_DEV)]
    for cp in copies:
        cp.start()
    for cp in copies:
        cp.wait()


def _ffn_fwd(x, gain, gathered, offsets, name):
    n = x.shape[0]
    tm = _tiles(n)["ffn_fwd"]

    def body(x_ref, g_ref, gathered_ref, out_ref, a_ref, u_ref, wg_v, wu_v, wd_v, sems):
        @pl.when(pl.program_id(0) == 0)
        def _():
            _load_weights(gathered_ref, offsets, (wg_v, wu_v, wd_v), sems)

        xt = x_ref[...]
        xhat, _ = _rms_fwd(xt)
        h = (xhat * g_ref[...]).astype(BF16)
        acc = jnp.zeros((tm, D_MODEL), F32)
        for c in range(D_FF // FF_CHUNK):
            sl = pl.ds(c * FF_CHUNK, FF_CHUNK)
            a = _dot_nt(h, wg_v[sl, :])
            u = _dot_nt(h, wu_v[sl, :])
            a_ref[:, sl] = a.astype(BF16)
            u_ref[:, sl] = u.astype(BF16)
            y = (a * _sigmoid(a) * u).astype(BF16)
            acc = acc + _dot(y, wd_v[sl, :])
        out_ref[...] = xt + FFN_HALF * acc

    any_spec = pl.BlockSpec(memory_space=pl.ANY)
    return pl.pallas_call(
        body, name=name, grid=(n // tm,),
        in_specs=[pl.BlockSpec((tm, D_MODEL), lambda i: (i, 0)), _full((1, D_MODEL)), any_spec],
        out_specs=[pl.BlockSpec((tm, D_MODEL), lambda i: (i, 0)), pl.BlockSpec((tm, D_FF), lambda i: (i, 0)),
                   pl.BlockSpec((tm, D_FF), lambda i: (i, 0))],
        out_shape=[jax.ShapeDtypeStruct((n, D_MODEL), F32), jax.ShapeDtypeStruct((n, D_FF), BF16),
                   jax.ShapeDtypeStruct((n, D_FF), BF16)],
        scratch_shapes=[pltpu.VMEM((D_FF, D_MODEL), BF16)] * 3 + [pltpu.SemaphoreType.DMA((3 * N_DEV,))],
        compiler_params=_params("arbitrary"),
    )(x, gain, gathered)


def _ffn_bwd(x, dout, a, u, gain, gathered, offsets, name):
    n = x.shape[0]
    tm = _tiles(n)["ffn_bwd"]

    def body(x_ref, do_ref, a_ref, u_ref, g_ref, gathered_ref,
             dx_ref, h_ref, dob_ref, y_ref, da_ref, du_ref, dg_ref, wg_v, wu_v, wd_v, sems):
        @pl.when(pl.program_id(0) == 0)
        def _():
            _load_weights(gathered_ref, offsets, (wg_v, wu_v, wd_v), sems)
            dg_ref[...] = jnp.zeros_like(dg_ref)

        xt = x_ref[...]
        g = g_ref[...]
        xhat, r = _rms_fwd(xt)
        h_ref[...] = (xhat * g).astype(BF16)
        dout = do_ref[...]
        dob = (FFN_HALF * dout).astype(BF16)
        dob_ref[...] = dob
        for c in range(D_FF // FF_CHUNK):
            sl = pl.ds(c * FF_CHUNK, FF_CHUNK)
            dy = _dot_nt(dob, wd_v[sl, :])
            av = a_ref[:, sl].astype(F32)
            uv = u_ref[:, sl].astype(F32)
            s = _sigmoid(av)
            silu = av * s
            y_ref[:, sl] = (silu * uv).astype(BF16)
            du_ref[:, sl] = (dy * silu).astype(BF16)
            da_ref[:, sl] = (dy * uv * (s * (1.0 + av * (1.0 - s)))).astype(BF16)
        dh = _dot(da_ref[...], wg_v[...]) + _dot(du_ref[...], wu_v[...])
        dg_ref[...] += _rowsum8(dh * xhat)
        dx_ref[...] = dout + _rms_bwd(dh, xhat, r, g)

    any_spec = pl.BlockSpec(memory_space=pl.ANY)
    row_d = pl.BlockSpec((tm, D_MODEL), lambda i: (i, 0))
    row_f = pl.BlockSpec((tm, D_FF), lambda i: (i, 0))
    return pl.pallas_call(
        body, name=name, grid=(n // tm,),
        in_specs=[row_d, row_d, row_f, row_f, _full((1, D_MODEL)), any_spec],
        out_specs=[row_d, row_d, row_d, row_f, row_f, row_f, _full((8, D_MODEL))],
        out_shape=[jax.ShapeDtypeStruct((n, D_MODEL), F32), jax.ShapeDtypeStruct((n, D_MODEL), BF16),
                   jax.ShapeDtypeStruct((n, D_MODEL), BF16), jax.ShapeDtypeStruct((n, D_FF), BF16),
                   jax.ShapeDtypeStruct((n, D_FF), BF16), jax.ShapeDtypeStruct((n, D_FF), BF16),
                   jax.ShapeDtypeStruct((8, D_MODEL), F32)],
        scratch_shapes=[pltpu.VMEM((D_FF, D_MODEL), BF16)] * 3 + [pltpu.SemaphoreType.DMA((3 * N_DEV,))],
        compiler_params=_params("arbitrary"),
    )(x, dout, a, u, gain, gathered)


def _tn_matmul(a, b, name, behind=None):
    n, fa = a.shape
    db = b.shape[1]
    tk = _tiles(n)["tn"]
    tf = fa // 2 if (fa // 2) % 128 == 0 and fa > 1024 else fa
    behind = jnp.zeros((8, 128), F32) if behind is None else behind

    def body(a_ref, b_ref, behind_ref, o_ref):
        @pl.when(pl.program_id(1) == 0)
        def _():
            o_ref[...] = jnp.zeros_like(o_ref)

        o_ref[...] += _dot_tn(a_ref[...], b_ref[...])

    return pl.pallas_call(
        body, name=name, grid=(fa // tf, n // tk),
        in_specs=[pl.BlockSpec((tk, tf), lambda i, k: (k, i)), pl.BlockSpec((tk, db), lambda i, k: (k, 0)),
                  pl.BlockSpec(memory_space=pl.ANY)],
        out_specs=pl.BlockSpec((tf, db), lambda i, k: (i, 0)),
        out_shape=jax.ShapeDtypeStruct((fa, db), F32),
        compiler_params=_params("arbitrary", "arbitrary"),
    )(a, b, behind)


def _loss_head(y, target, name):
    n = y.shape[0]
    tm = _tiles(n)["rows"]

    def body(y_ref, t_ref, d_ref, acc_ref):
        @pl.when(pl.program_id(0) == 0)
        def _():
            acc_ref[...] = jnp.zeros_like(acc_ref)

        d = y_ref[...] - t_ref[...]
        d_ref[...] = d * (1.0 / D_MODEL)
        acc_ref[...] += _rowsum8(d * d)

    row = pl.BlockSpec((tm, D_MODEL), lambda i: (i, 0))
    return pl.pallas_call(
        body, name=name, grid=(n // tm,), in_specs=[row, row], out_specs=[row, _full((8, D_MODEL))],
        out_shape=[jax.ShapeDtypeStruct((n, D_MODEL), F32), jax.ShapeDtypeStruct((8, D_MODEL), F32)],
        compiler_params=_params("arbitrary"),
    )(y, target)


def _window_sum(v, w, rows, forward):
    s = v
    sh = 1
    while sh < w:
        s = s + pltpu.roll(s, (rows - sh) if forward else sh, 0)
        sh *= 2
    return s


def _pool_fwd(x, gain, w, scale, name):
    n = x.shape[0]
    tm = _tiles(n)["pool"]
    hb = tm // POOL_HALO
    rows = tm + POOL_HALO

    def body(x_ref, xh_ref, g_ref, w_ref, sc_ref, out_ref):
        i = pl.program_id(0)
        xt = x_ref[...]
        e = jnp.concatenate([xh_ref[...], xt], axis=0)
        xhat, _ = _rms_fwd(e)
        row = lax.broadcasted_iota(jnp.int32, (rows, 1), 0)
        hn = jnp.where((row >= POOL_HALO) | (i > 0), xhat * g_ref[...], 0.0)
        t_glob = i * tm + row - POOL_HALO
        outs = []
        for gi, win in enumerate(POOL_WINDOWS):
            ug = hn[:, gi * POOL_GROUP:(gi + 1) * POOL_GROUP]
            cnt = jnp.maximum(jnp.minimum(t_glob + 1, win), 1).astype(F32)
            pooled = (_window_sum(ug, win, rows, False) / cnt - ug)[POOL_HALO:]
            outs.append(_dot(pooled.astype(BF16), w_ref[gi]))
        out_ref[...] = xt + jnp.concatenate(outs, axis=1) * sc_ref[...]

    return pl.pallas_call(
        body, name=name, grid=(n // tm,),
        in_specs=[pl.BlockSpec((tm, D_MODEL), lambda i: (i, 0)),
                  pl.BlockSpec((POOL_HALO, D_MODEL), lambda i: (jnp.maximum(i * hb - 1, 0), 0)),
                  _full((1, D_MODEL)), _full((4, POOL_GROUP, POOL_GROUP)), _full((1, D_MODEL))],
        out_specs=pl.BlockSpec((tm, D_MODEL), lambda i: (i, 0)),
        out_shape=jax.ShapeDtypeStruct((n, D_MODEL), F32),
        compiler_params=_params("arbitrary"),
    )(x, x, gain, w, scale)


def _pool_bwd(x, dout, gain, w, scale, name):
    n = x.shape[0]
    tm = _tiles(n)["pool"]
    hb = tm // POOL_HALO
    rows = tm + POOL_HALO
    nt = n // tm

    def body(x_ref, xh_ref, do_ref, doh_ref, g_ref, w_ref, sc_ref, dx_ref, dw_ref, dsc_ref, dg_ref):
        i = pl.program_id(0)

        @pl.when(i == 0)
        def _():
            dw_ref[...] = jnp.zeros_like(dw_ref)
            dsc_ref[...] = jnp.zeros_like(dsc_ref)
            dg_ref[...] = jnp.zeros_like(dg_ref)

        xt = x_ref[...]
        g = g_ref[...]
        e = jnp.concatenate([xh_ref[...], xt], axis=0)
        xhat_e, r_e = _rms_fwd(e)
        row = lax.broadcasted_iota(jnp.int32, (rows, 1), 0)
        hn = jnp.where((row >= POOL_HALO) | (i > 0), xhat_e * g, 0.0)
        t_prev = i * tm + row - POOL_HALO
        t_next = i * tm + row
        dout = do_ref[...]
        dt = jnp.concatenate([dout, doh_ref[...]], axis=0)
        dt = jnp.where((row < tm) | (i < nt - 1), dt, 0.0)
        dyr = dt * sc_ref[...]
        dus, dscs = [], []
        for gi, win in enumerate(POOL_WINDOWS):
            lanes = slice(gi * POOL_GROUP, (gi + 1) * POOL_GROUP)
            ug = hn[:, lanes]
            cnt = jnp.maximum(jnp.minimum(t_prev + 1, win), 1).astype(F32)
            pooled = (_window_sum(ug, win, rows, False) / cnt - ug)[POOL_HALO:].astype(BF16)
            yraw = _dot(pooled, w_ref[gi])
            dscs.append(_rowsum8(dout[:, lanes] * yraw))
            dyr_b = dyr[:, lanes].astype(BF16)
            dw_ref[gi] += _dot_tn(pooled, dyr_b[:tm])
            dpool = _dot_nt(dyr_b, w_ref[gi])
            cnt2 = jnp.minimum(t_next + 1, win).astype(F32)
            dus.append((_window_sum(dpool / cnt2, win, rows, True) - dpool)[:tm])
        dsc_ref[...] += jnp.concatenate(dscs, axis=1)
        dh = jnp.concatenate(dus, axis=1)
        xhat = xhat_e[POOL_HALO:]
        dg_ref[...] += _rowsum8(dh * xhat)
        dx_ref[...] = dout + _rms_bwd(dh, xhat, r_e[POOL_HALO:], g)

    row_d = pl.BlockSpec((tm, D_MODEL), lambda i: (i, 0))
    prev_h = pl.BlockSpec((POOL_HALO, D_MODEL), lambda i: (jnp.maximum(i * hb - 1, 0), 0))
    next_h = pl.BlockSpec((POOL_HALO, D_MODEL), lambda i: (jnp.minimum((i + 1) * hb, n // POOL_HALO - 1), 0))
    return pl.pallas_call(
        body, name=name, grid=(nt,),
        in_specs=[row_d, prev_h, row_d, next_h, _full((1, D_MODEL)), _full((4, POOL_GROUP, POOL_GROUP)),
                  _full((1, D_MODEL))],
        out_specs=[row_d, _full((4, POOL_GROUP, POOL_GROUP)), _full((8, D_MODEL)), _full((8, D_MODEL))],
        out_shape=[jax.ShapeDtypeStruct((n, D_MODEL), F32), jax.ShapeDtypeStruct((4, POOL_GROUP, POOL_GROUP), F32),
                   jax.ShapeDtypeStruct((8, D_MODEL), F32), jax.ShapeDtypeStruct((8, D_MODEL), F32)],
        compiler_params=_params("arbitrary"),
    )(x, x, dout, dout, gain, w, scale)


def _rope(v, cos, sin_signed):
    lo, hi = v[:, :128], v[:, 128:]
    lane = lax.broadcasted_iota(jnp.int32, hi.shape, 1)
    swapped = jnp.where(lane < 32, pltpu.roll(hi, 96, 1), pltpu.roll(hi, 32, 1))
    return jnp.concatenate([lo, hi * cos + swapped * sin_signed], axis=1)


def _rope_bwd(gr, cos, sin_signed):
    lo, hi = gr[:, :128], gr[:, 128:]
    t = hi * sin_signed
    lane = lax.broadcasted_iota(jnp.int32, hi.shape, 1)
    swapped = jnp.where(lane < 32, pltpu.roll(t, 96, 1), pltpu.roll(t, 32, 1))
    return jnp.concatenate([lo, hi * cos + swapped], axis=1)


def _mla_latents(h, win_ref):
    cq = _dot_nt(h, win_ref[0:Q_LORA, :])
    ckv = _dot_nt(h, win_ref[Q_LORA:Q_LORA + KV_LORA, :])
    kpe = _dot_nt(h, win_ref[Q_LORA + KV_LORA:LAT_PAD, :])
    return cq, ckv, kpe


def _mla_pre_fwd(x, gain, win, gq, gkv, wq, wkv, ghq, ghk, cos, sin_signed, name):
    n = x.shape[0]
    tm = _tiles(n)["fwd_k"]

    def body(x_ref, g_ref, win_ref, gq_ref, gkv_ref, wq_ref, wkv_ref, ghq_ref, ghk_ref, c_ref, s_ref,
             q_ref, k_ref, v_ref, vt_ref):
        xhat, _ = _rms_fwd(x_ref[...])
        h = (xhat * g_ref[...]).astype(BF16)
        cq, ckv, kpe = _mla_latents(h, win_ref)
        cqn = (_rms_fwd(cq)[0] * gq_ref[...]).astype(BF16)
        ckvn = (_rms_fwd(ckv)[0] * gkv_ref[...]).astype(BF16)
        cos, sn = c_ref[...], s_ref[...]
        for hd in range(N_HEADS):
            rws = pl.ds(hd * HEAD_PAD, HEAD_PAD)
            qh = _dot_nt(cqn, wq_ref[rws, :])
            qn = _rms_fwd(qh, QK_HEAD)[0] * ghq_ref[...]
            q_ref[hd] = (_rope(qn, cos, sn) * LOGIT_SCALE).astype(BF16)
            kvh = _dot_nt(ckvn, wkv_ref[rws, :])
            kpre = jnp.concatenate([kvh[:, :QK_NOPE], kpe], axis=1)
            kn = _rms_fwd(kpre, QK_HEAD)[0] * ghk_ref[...]
            k_ref[hd] = _rope(kn, cos, sn).astype(BF16)
            vh = kvh[:, QK_NOPE:]
            v_ref[hd] = vh.astype(BF16)
            vt_ref[hd, 0] = jnp.concatenate([vh.T, jnp.ones((VT_ROWS - V_HEAD, tm), F32)], axis=0).astype(BF16)

    row = lambda w: pl.BlockSpec((tm, w), lambda i: (i, 0))
    head = lambda w: pl.BlockSpec((N_HEADS, tm, w), lambda i: (0, i, 0))
    return pl.pallas_call(
        body, name=name, grid=(n // tm,),
        in_specs=[row(D_MODEL), _full((1, D_MODEL)), _full((LAT_PAD, D_MODEL)), _full((1, Q_LORA)),
                  _full((1, KV_LORA)), _full((N_HEADS * HEAD_PAD, Q_LORA)), _full((N_HEADS * HEAD_PAD, KV_LORA)),
                  _full((1, HEAD_PAD)), _full((1, HEAD_PAD)), row(128), row(128)],
        out_specs=[head(HEAD_PAD), head(HEAD_PAD), head(V_HEAD),
                   pl.BlockSpec((N_HEADS, 1, VT_ROWS, tm), lambda i: (0, i, 0, 0))],
        out_shape=[jax.ShapeDtypeStruct((N_HEADS, n, HEAD_PAD), BF16), jax.ShapeDtypeStruct((N_HEADS, n, HEAD_PAD), BF16),
                   jax.ShapeDtypeStruct((N_HEADS, n, V_HEAD), BF16),
                   jax.ShapeDtypeStruct((N_HEADS, n // tm, VT_ROWS, tm), BF16)],
        compiler_params=_params("arbitrary"),
    )(x, gain, win, gq, gkv, wq, wkv, ghq, ghk, cos, sin_signed)


def _flash_fwd(q, k, vt, name):
    n = q.shape[1]
    tk = _tiles(n)["fwd_k"]
    tq = 2 * tk
    nq = n // tq

    def body(q_ref, k_ref, vt_ref, o_ref, lse_ref, m_ref, p_hbm, s_scr, m_scr, acc_scr, p_scr, p_sems):
        h = pl.program_id(0)
        i = pl.program_id(1)
        qi = q_ref[0]

        def scores(j, slot):
            s_scr[slot] = _dot_nt(k_ref[0, pl.ds(pl.multiple_of(j * tk, tk), tk), :], qi)

        def p_copy(block, pslot):
            return pltpu.make_async_copy(p_scr.at[pslot], p_hbm.at[h, block], p_sems.at[pslot])

        def update(j, slot, pslot, diagonal=None):
            s = s_scr[slot]
            if diagonal is not None:
                krow = lax.broadcasted_iota(jnp.int32, (tk, tq), 0) + diagonal * tk
                qcol = lax.broadcasted_iota(jnp.int32, (tk, tq), 1)
                s = jnp.where(krow <= qcol, s, -jnp.inf)
            m = m_scr[...]
            m_new = jnp.maximum(m, jnp.max(s, axis=0, keepdims=True))
            p = jnp.exp2(s - m_new).astype(BF16)
            p_scr[pslot] = p
            acc_scr[...] = jnp.exp2(m - m_new) * acc_scr[...] + _dot(vt_ref[0, j], p)
            m_scr[...] = m_new
            m_ref[0, 0, j] = m_new

        def kv_pair(jj, diagonal):
            first = 2 * (jj % 2)
            p_copy(0, first).wait()
            p_copy(0, first + 1).wait()
            scores(2 * jj + 1, 1)
            update(2 * jj, 0, first, 0 if diagonal else None)
            if not diagonal:
                scores(2 * jj + 2, 0)
            update(2 * jj + 1, 1, first + 1, 1 if diagonal else None)
            p_copy(i * (i + 1) + 2 * jj, first).start()
            p_copy(i * (i + 1) + 2 * jj + 1, first + 1).start()

        m_scr[...] = jnp.full((1, tq), -jnp.inf, F32)
        acc_scr[...] = jnp.zeros((VT_ROWS, tq), F32)
        p_scr[...] = jnp.zeros_like(p_scr)
        for pslot in range(4):
            p_copy(nq * (nq + 1) + pslot, pslot).start()
        scores(0, 0)

        def pair(jj, carry):
            kv_pair(jj, False)
            return carry

        lax.fori_loop(0, i, pair, 0)
        kv_pair(i, True)
        l = acc_scr[V_HEAD:V_HEAD + 1, :]
        o_ref[...] = (acc_scr[0:V_HEAD, :] / l).T.astype(BF16)
        lse_ref[0, 0] = m_scr[...] + jnp.log2(l)
        for pslot in range(4):
            p_copy(0, pslot).wait()

    return pl.pallas_call(
        body, name=name, grid=(N_HEADS, nq),
        in_specs=[pl.BlockSpec((1, tq, HEAD_PAD), lambda h, i: (h, i, 0)),
                  pl.BlockSpec((1, n, HEAD_PAD), lambda h, i: (h, 0, 0)),
                  pl.BlockSpec((1, n // tk, VT_ROWS, tk), lambda h, i: (h, 0, 0, 0))],
        out_specs=[pl.BlockSpec((tq, V_HEAD), lambda h, i: (i, h)),
                   pl.BlockSpec((1, 1, 1, tq), lambda h, i: (h, i, 0, 0)),
                   pl.BlockSpec((1, 1, 2 * nq, 1, tq), lambda h, i: (h, i, 0, 0, 0)),
                   pl.BlockSpec(memory_space=pl.ANY)],
        out_shape=[jax.ShapeDtypeStruct((n, N_HEADS * V_HEAD), BF16), jax.ShapeDtypeStruct((N_HEADS, nq, 1, tq), F32),
                   jax.ShapeDtypeStruct((N_HEADS, nq, 2 * nq, 1, tq), F32),
                   jax.ShapeDtypeStruct((N_HEADS, nq * (nq + 1) + 4, tk, tq), BF16)],
        scratch_shapes=[pltpu.VMEM((2, tk, tq), F32), pltpu.VMEM((1, tq), F32), pltpu.VMEM((VT_ROWS, tq), F32),
                        pltpu.VMEM((4, tk, tq), BF16), pltpu.SemaphoreType.DMA((4,))],
        compiler_params=_params("arbitrary", "arbitrary"),
    )(q, k, vt)


def _mla_post_fwd(o, x, wout, name):
    n = x.shape[0]
    tm = _tiles(n)["rows"]

    def body(o_ref, x_ref, w_ref, out_ref):
        out_ref[...] = x_ref[...] + _dot(o_ref[...], w_ref[...])

    row = pl.BlockSpec((tm, D_MODEL), lambda i: (i, 0))
    return pl.pallas_call(
        body, name=name, grid=(n // tm,), in_specs=[row, row, _full((D_MODEL, D_MODEL))], out_specs=row,
        out_shape=jax.ShapeDtypeStruct((n, D_MODEL), F32), compiler_params=_params("arbitrary"),
    )(o, x, wout)


def _mla_out_bwd(dout, o, wout, name):
    n = dout.shape[0]
    t = _tiles(n)["bwd_q"]
    nq = n // t

    def body(do_ref, o_ref, w_ref, dob_ref, dpo_ref, dl_ref):
        dob = do_ref[...].astype(BF16)
        dob_ref[...] = dob
        dpo = _dot_nt(dob, w_ref[...])
        dpo_ref[...] = dpo.astype(BF16)
        ov = o_ref[...].astype(F32)
        for hd in range(N_HEADS):
            lanes = slice(hd * V_HEAD, (hd + 1) * V_HEAD)
            prod = dpo[:, lanes] * ov[:, lanes]
            dl_ref[hd, 0] = jnp.sum(prod.T, axis=0, keepdims=True)

    row = pl.BlockSpec((t, D_MODEL), lambda i: (i, 0))
    return pl.pallas_call(
        body, name=name, grid=(nq,), in_specs=[row, row, _full((D_MODEL, D_MODEL))],
        out_specs=[row, row, pl.BlockSpec((N_HEADS, 1, 1, t), lambda i: (0, i, 0, 0))],
        out_shape=[jax.ShapeDtypeStruct((n, D_MODEL), BF16), jax.ShapeDtypeStruct((n, D_MODEL), BF16),
                   jax.ShapeDtypeStruct((N_HEADS, nq, 1, t), F32)],
        compiler_params=_params("arbitrary"),
    )(dout, o, wout)


def _flash_bwd(q, k, v, dpo, lse, delta, p, m, name):
    n = q.shape[1]
    tq = _tiles(n)["bwd_q"]
    tk = 2 * tq
    nk = n // tk
    nqb = n // tq
    assert p.shape[2:] == (tq, tk) and m.shape[1:] == (nk, 2 * nk, 1, tk), "forward blocks are (tq keys, 2 tq queries)"

    def body(k_ref, v_ref, q_ref, do_ref, lse_ref, dl_ref, m_ref, p_hbm, dk_ref, dv_ref, dq_hbm,
             dq_acc, p_scr, dp_scr, sem, p_sems):
        h = pl.program_id(0)
        j = pl.program_id(1)

        @pl.when(j == 0)
        def _():
            dq_acc[...] = jnp.zeros_like(dq_acc)

        dk_ref[...] = jnp.zeros_like(dk_ref)
        dv_ref[...] = jnp.zeros_like(dv_ref)
        kj = k_ref[0]
        vj = v_ref[0]
        npairs = nk - 1 - j

        def block(t):
            return jnp.where(t < 2 * npairs, 2 * j + 2 + t, 2 * j + (t - 2 * npairs))

        def p_copy(i, first, half):
            tile = i // 2
            return pltpu.make_async_copy(p_hbm.at[h, pl.ds(tile * (tile + 1) + 2 * j, 2), :, pl.ds(half * tq, tq)],
                                         p_scr.at[first + half], p_sems.at[first + half])

        def fetch_pair(t):
            for half in range(2):
                p_copy(block(2 * t + half), 2 * (t % 2), half).start()

        def dp_ahead(i, half):
            dp_scr[half] = _dot_nt(vj, do_ref[pl.ds(pl.multiple_of(i * tq, tq), tq), :])

        def update(i, first, half):
            rws = pl.ds(pl.multiple_of(i * tq, tq), tq)
            lse_i = lse_ref[0, i]
            lanes = slice(half * tq, (half + 1) * tq)
            pv = jnp.concatenate(
                [p_scr[first + half, b].astype(F32) * jnp.exp2(m_ref[0, i // 2, 2 * j + b][:, lanes] - lse_i)
                 for b in range(2)], axis=0)
            dv_ref[0] += _dot(pv.astype(BF16), do_ref[rws, :])
            ds = (pv * (dp_scr[half] - dl_ref[0, i])).astype(BF16)
            dk_ref[0] += _dot(ds, q_ref[0, rws, :])
            dq_acc[rws, :] += _dot_tn(ds, kj)

        def q_pair(t, last):
            first = 2 * (t % 2)
            for half in range(2):
                p_copy(0, first, half).wait()
            if not last:
                fetch_pair(t + 1)
            dp_ahead(block(2 * t + 1), 1)
            update(block(2 * t), first, 0)
            if not last:
                dp_ahead(block(2 * t + 2), 0)
            update(block(2 * t + 1), first, 1)

        fetch_pair(0)
        dp_ahead(block(0), 0)

        def pair(t, carry):
            q_pair(t, False)
            return carry

        lax.fori_loop(0, npairs, pair, 0)
        q_pair(npairs, True)
        dk_ref[...] = dk_ref[...] * (ATTN_SCALE / LOGIT_SCALE)

        @pl.when(j == nk - 1)
        def _():
            dq_acc[...] = dq_acc[...] * ATTN_SCALE
            cp = pltpu.make_async_copy(dq_acc, dq_hbm.at[h], sem)
            cp.start()
            cp.wait()

    resident = dict(pipeline_mode=pl.Buffered(1))
    return pl.pallas_call(
        body, name=name, grid=(N_HEADS, nk),
        in_specs=[pl.BlockSpec((1, tk, HEAD_PAD), lambda h, j: (h, j, 0)),
                  pl.BlockSpec((1, tk, V_HEAD), lambda h, j: (h, j, 0)),
                  pl.BlockSpec((1, n, HEAD_PAD), lambda h, j: (h, 0, 0), **resident),
                  pl.BlockSpec((n, V_HEAD), lambda h, j: (0, h), **resident),
                  pl.BlockSpec((1, nqb, 1, tq), lambda h, j: (h, 0, 0, 0)),
                  pl.BlockSpec((1, nqb, 1, tq), lambda h, j: (h, 0, 0, 0)),
                  pl.BlockSpec((1, nk, 2 * nk, 1, tk), lambda h, j: (h, 0, 0, 0, 0), **resident),
                  pl.BlockSpec(memory_space=pl.ANY)],
        out_specs=[pl.BlockSpec((1, tk, HEAD_PAD), lambda h, j: (h, j, 0)),
                   pl.BlockSpec((1, tk, V_HEAD), lambda h, j: (h, j, 0)),
                   pl.BlockSpec(memory_space=pl.ANY)],
        out_shape=[jax.ShapeDtypeStruct((N_HEADS, n, HEAD_PAD), F32), jax.ShapeDtypeStruct((N_HEADS, n, V_HEAD), F32),
                   jax.ShapeDtypeStruct((N_HEADS, n, HEAD_PAD), F32)],
        scratch_shapes=[pltpu.VMEM((n, HEAD_PAD), F32), pltpu.VMEM((4, 2, tq, tq), BF16), pltpu.VMEM((2, tk, tq), F32),
                        pltpu.SemaphoreType.DMA(()), pltpu.SemaphoreType.DMA((4,))],
        compiler_params=_params("arbitrary", "arbitrary"),
    )(k, v, q, dpo, lse, delta, m, p)


def _mla_pre_bwd(x, dout, dq, dk, dv, gain, win, gq, gkv, wq, wkv, ghq, ghk, cos, sin_signed, name):
    n = x.shape[0]
    tm = _tiles(n)["mla_bwd"]
    hw = N_HEADS * HEAD_PAD

    def body(x_ref, do_ref, dq_ref, dk_ref, dv_ref, g_ref, win_ref, gq_ref, gkv_ref, wq_ref, wkv_ref, ghq_ref, ghk_ref,
             c_ref, s_ref, dx_ref, h_ref, dlat_ref, cqn_ref, dqp_ref, ckvn_ref, dkv_ref,
             dg_ref, dgq_ref, dgkv_ref, dghq_ref, dghk_ref):
        @pl.when(pl.program_id(0) == 0)
        def _():
            for ref in (dg_ref, dgq_ref, dgkv_ref, dghq_ref, dghk_ref):
                ref[...] = jnp.zeros_like(ref)

        g = g_ref[...]
        xhat, r = _rms_fwd(x_ref[...])
        h = (xhat * g).astype(BF16)
        h_ref[...] = h
        cq, ckv, kpe = _mla_latents(h, win_ref)
        cqhat, rcq = _rms_fwd(cq)
        ckvhat, rckv = _rms_fwd(ckv)
        cqn = (cqhat * gq_ref[...]).astype(BF16)
        ckvn = (ckvhat * gkv_ref[...]).astype(BF16)
        cqn_ref[...] = cqn
        ckvn_ref[...] = ckvn
        cos, sn = c_ref[...], s_ref[...]
        ghq, ghk = ghq_ref[...], ghk_ref[...]
        dkpe = jnp.zeros((tm, 128), F32)
        dghq = jnp.zeros((8, HEAD_PAD), F32)
        dghk = jnp.zeros((8, HEAD_PAD), F32)
        for hd in range(N_HEADS):
            rws = pl.ds(hd * HEAD_PAD, HEAD_PAD)
            lanes = slice(hd * HEAD_PAD, (hd + 1) * HEAD_PAD)
            qhat, rq = _rms_fwd(_dot_nt(cqn, wq_ref[rws, :]), QK_HEAD)
            gqn = _rope_bwd(dq_ref[hd], cos, sn)
            dghq = dghq + _rowsum8(gqn * qhat)
            dqpre = _rms_bwd(gqn, qhat, rq, ghq, QK_HEAD).astype(BF16)
            dqp_ref[:, lanes] = dqpre
            kvh = _dot_nt(ckvn, wkv_ref[rws, :])
            khat, rk = _rms_fwd(jnp.concatenate([kvh[:, :QK_NOPE], kpe], axis=1), QK_HEAD)
            gkn = _rope_bwd(dk_ref[hd], cos, sn)
            dghk = dghk + _rowsum8(gkn * khat)
            dkpre = _rms_bwd(gkn, khat, rk, ghk, QK_HEAD)
            dkpe = dkpe + dkpre[:, QK_NOPE:]
            dkvh = jnp.concatenate([dkpre[:, :QK_NOPE], dv_ref[hd]], axis=1).astype(BF16)
            dkv_ref[:, lanes] = dkvh
        dcqn = _dot(dqp_ref[...], wq_ref[...])
        dckvn = _dot(dkv_ref[...], wkv_ref[...])
        dghq_ref[...] += dghq
        dghk_ref[...] += dghk
        dgq_ref[...] += _rowsum8(dcqn * cqhat)
        dgkv_ref[...] += _rowsum8(dckvn * ckvhat)
        dlat = jnp.concatenate([_rms_bwd(dcqn, cqhat, rcq, gq_ref[...]), _rms_bwd(dckvn, ckvhat, rckv, gkv_ref[...]),
                                dkpe], axis=1).astype(BF16)
        dlat_ref[...] = dlat
        dh = _dot(dlat, win_ref[...])
        dg_ref[...] += _rowsum8(dh * xhat)
        dx_ref[...] = do_ref[...] + _rms_bwd(dh, xhat, r, g)

    row = lambda w: pl.BlockSpec((tm, w), lambda i: (i, 0))
    head = lambda w: pl.BlockSpec((N_HEADS, tm, w), lambda i: (0, i, 0))
    sds = jax.ShapeDtypeStruct
    return pl.pallas_call(
        body, name=name, grid=(n // tm,),
        in_specs=[row(D_MODEL), row(D_MODEL), head(HEAD_PAD), head(HEAD_PAD), head(V_HEAD), _full((1, D_MODEL)),
                  _full((LAT_PAD, D_MODEL)), _full((1, Q_LORA)), _full((1, KV_LORA)), _full((hw, Q_LORA)),
                  _full((hw, KV_LORA)), _full((1, HEAD_PAD)), _full((1, HEAD_PAD)), row(128), row(128)],
        out_specs=[row(D_MODEL), row(D_MODEL), row(LAT_PAD), row(Q_LORA), row(hw), row(KV_LORA), row(hw),
                   _full((8, D_MODEL)), _full((8, Q_LORA)), _full((8, KV_LORA)), _full((8, HEAD_PAD)),
                   _full((8, HEAD_PAD))],
        out_shape=[sds((n, D_MODEL), F32), sds((n, D_MODEL), BF16), sds((n, LAT_PAD), BF16), sds((n, Q_LORA), BF16),
                   sds((n, hw), BF16), sds((n, KV_LORA), BF16), sds((n, hw), BF16), sds((8, D_MODEL), F32),
                   sds((8, Q_LORA), F32), sds((8, KV_LORA), F32), sds((8, HEAD_PAD), F32), sds((8, HEAD_PAD), F32)],
        compiler_params=_params("arbitrary"),
    )(x, dout, dq, dk, dv, gain, win, gq, gkv, wq, wkv, ghq, ghk, cos, sin_signed)


def _adamw(w, g, m, v, behind, name):
    rows, cols = w.shape
    tr = rows
    for cand in (512, 256, 128, 64, 32, 16, 8):
        if rows % cand == 0 and rows > cand:
            tr = cand
            break

    def body(w_ref, g_ref, m_ref, v_ref, behind_ref, d_ref, mo_ref, vo_ref):
        d_ref[...], mo_ref[...], vo_ref[...] = _adamw_step(w_ref[...], g_ref[...], m_ref[...], v_ref[...])

    blk = pl.BlockSpec((tr, cols), lambda i: (i, 0))
    return pl.pallas_call(
        body, name=name, grid=(rows // tr,), in_specs=[blk] * 4 + [pl.BlockSpec(memory_space=pl.ANY)],
        out_specs=[blk] * 3, out_shape=[jax.ShapeDtypeStruct((rows, cols), F32)] * 3,
        compiler_params=_params("arbitrary"),
    )(w, g, m, v, behind)


def _adamw_step(w, g, m, v):
    mn = ADAM_B1 * m + (1.0 - ADAM_B1) * g
    vn = ADAM_B2 * v + (1.0 - ADAM_B2) * (g * g)
    m_hat = mn / (1.0 - ADAM_B1 ** ADAM_STEP)
    v_hat = vn / (1.0 - ADAM_B2 ** ADAM_STEP)
    return -ADAM_LR * (m_hat / (jnp.sqrt(v_hat) + ADAM_EPS) + ADAM_WD * w), mn, vn


def _adamw_layers(w, grads, m, v, behind, name):
    layers, r, c = w.shape
    tr = _row_tile(r, 512)

    def body(w_ref, m_ref, v_ref, *rest):
        g_refs, (d_ref, mo_ref, vo_ref, go_ref) = rest[:layers], rest[layers + 1:]
        layer = pl.program_id(1)
        g = g_refs[0][...]
        for k in range(1, layers):
            g = jnp.where(layer == k, g_refs[k][...], g)
        d_ref[0], mo_ref[0], vo_ref[0] = _adamw_step(w_ref[0], g, m_ref[0], v_ref[0])
        go_ref[0] = g

    stacked = pl.BlockSpec((1, tr, c), lambda i, layer: (layer, i, 0))
    piece = pl.BlockSpec((tr, c), lambda i, layer: (i, 0))
    return pl.pallas_call(
        body, name=name, grid=(r // tr, layers),
        in_specs=[stacked] * 3 + [piece] * layers + [pl.BlockSpec(memory_space=pl.ANY)],
        out_specs=[stacked] * 4, out_shape=[jax.ShapeDtypeStruct((layers, r, c), F32)] * 4,
        compiler_params=_params("arbitrary", "arbitrary"),
    )(w, m, v, *grads, behind)


def _sum_parts(parts, name):
    k, r, c = parts.shape
    tr = min(r, 512)

    def body(p_ref, o_ref):
        acc = p_ref[0]
        for j in range(1, k):
            acc = acc + p_ref[j]
        o_ref[...] = acc

    return pl.pallas_call(
        body, name=name, grid=(r // tr,), in_specs=[pl.BlockSpec((k, tr, c), lambda i: (0, i, 0))],
        out_specs=pl.BlockSpec((tr, c), lambda i: (i, 0)), out_shape=jax.ShapeDtypeStruct((r, c), parts.dtype),
        compiler_params=_params("arbitrary"),
    )(parts)


def _row_tile(r, most=256):
    best = r
    for cand in range(8, most + 1, 8):
        if r % cand == 0:
            best = cand
    return best


def _sum_exchange(mine, landed, me, name):
    _, r, c = mine.shape
    tr = _row_tile(r)

    def body(me_ref, m_ref, l_ref, o_ref):
        acc = m_ref[0]
        for k in range(1, N_DEV):
            acc = acc + l_ref[k]
        o_ref[...] = acc

    return pl.pallas_call(
        body, name=name,
        grid_spec=pltpu.PrefetchScalarGridSpec(
            num_scalar_prefetch=1, grid=(r // tr,),
            in_specs=[pl.BlockSpec((1, tr, c), lambda i, me_ref: (me_ref[0], i, 0)),
                      pl.BlockSpec((N_DEV, tr, c), lambda i, me_ref: (0, i, 0))],
            out_specs=pl.BlockSpec((tr, c), lambda i, me_ref: (i, 0))),
        out_shape=jax.ShapeDtypeStruct((r, c), mine.dtype), compiler_params=_params("arbitrary"),
    )(me, mine, landed)


MESH = pl.DeviceIdType.MESH


def _all_gather(x, name):
    r, c = x.shape

    def body(x_ref, out_ref, send_sems, recv_sems, local_sem):
        mx, my, mc = lax.axis_index("x"), lax.axis_index("y"), lax.axis_index("c")
        me, sibling = (mx, my, mc), (mx, my, 1 - mc)
        chips = [(1 - mx, my), (mx, 1 - my), (1 - mx, 1 - my)]

        def slot(px, py, pc):
            return out_ref.at[4 * px + 2 * py + pc]

        def copy(k, block, to, src=None):
            return pltpu.make_async_remote_copy(
                src_ref=slot(*block) if src is None else src, dst_ref=slot(*block),
                send_sem=send_sems.at[k], recv_sem=recv_sems.at[k], device_id=to, device_id_type=MESH)

        mine = pltpu.make_async_copy(x_ref, slot(*me), local_sem)
        mine.start()
        first = [copy(0, me, sibling, src=x_ref)]
        first += [copy(1 + j, me, (*chip, mc), src=x_ref) for j, chip in enumerate(chips)]
        for cp in first:
            cp.start()
        passed = [copy(4 + j, (*chip, mc), sibling) for j, chip in enumerate(chips)]
        for j, chip in enumerate(chips):
            copy(1 + j, (*chip, mc), me).wait_recv()
            passed[j].start()
        copy(0, sibling, me).wait_recv()
        for j, chip in enumerate(chips):
            copy(4 + j, (*chip, 1 - mc), me).wait_recv()
        for cp in first + passed:
            cp.wait_send()
        mine.wait()

    any_spec = pl.BlockSpec(memory_space=pl.ANY)
    return pl.pallas_call(
        body, name=name, in_specs=[any_spec], out_specs=any_spec,
        out_shape=jax.ShapeDtypeStruct((N_DEV, r, c), x.dtype),
        scratch_shapes=[pltpu.SemaphoreType.DMA((7,)), pltpu.SemaphoreType.DMA((7,)), pltpu.SemaphoreType.DMA(())],
    )(x)


HBM_SPEC = pl.BlockSpec(memory_space=pltpu.HBM)
SEM_SPEC = pl.BlockSpec(memory_space=pltpu.SEMAPHORE)
SPLIT_EFFECT = pltpu.SideEffectType.DATAFLOW_SIDE_EFFECTING


def _exchange_copies(src_ref, land_ref, send_sems, recv_sems, gather):
    mx, my, mc = lax.axis_index("x"), lax.axis_index("y"), lax.axis_index("c")
    me = 4 * mx + 2 * my + mc
    copies = []
    for k in range(1, N_DEV):
        px = 1 - mx if k & 4 else mx
        py = 1 - my if k & 2 else my
        pc = 1 - mc if k & 1 else mc
        src = src_ref if gather else src_ref.at[4 * px + 2 * py + pc]
        dst = land_ref.at[me] if gather else land_ref.at[k]
        copies.append(pltpu.make_async_remote_copy(
            src_ref=src, dst_ref=dst, send_sem=send_sems.at[k - 1], recv_sem=recv_sems.at[k - 1],
            device_id=(px, py, pc), device_id_type=MESH))
    return copies


def _exchange_start(src, after, gather, name):
    land_shape = (N_DEV,) + src.shape[-2:]

    def body(src_ref, land_ref, after_ref, send_sems, recv_sems, src_thru, land_thru, token):
        for cp in _exchange_copies(src_ref, land_ref, send_sems, recv_sems, gather):
            cp.start()
        token[...] = jnp.zeros_like(token)

    return pl.pallas_call(
        body, name=name,
        out_shape=(pltpu.SemaphoreType.DMA((N_DEV - 1,)), pltpu.SemaphoreType.DMA((N_DEV - 1,)),
                   pltpu.HBM(src.shape, src.dtype), pltpu.HBM(land_shape, src.dtype),
                   jax.ShapeDtypeStruct((8, 128), F32)),
        in_specs=(HBM_SPEC, HBM_SPEC, pl.BlockSpec(memory_space=pl.ANY)),
        out_specs=(SEM_SPEC, SEM_SPEC, HBM_SPEC, HBM_SPEC, pl.BlockSpec(memory_space=pltpu.VMEM)),
        input_output_aliases={0: 2, 1: 3},
        compiler_params=pltpu.CompilerParams(has_side_effects=SPLIT_EFFECT),
    )(pltpu.with_memory_space_constraint(src, pltpu.HBM),
      pltpu.with_memory_space_constraint(lax.empty(land_shape, src.dtype), pltpu.HBM), after)


def _exchange_wait(started, after, gather, name):
    send_sems, recv_sems, src_thru, land_thru, _ = started

    def body(src_ref, land_ref, send_sems, recv_sems, after_ref, src_out, land_out):
        for cp in _exchange_copies(src_ref, land_ref, send_sems, recv_sems, gather):
            cp.wait_send()
            cp.wait_recv()

    return pl.pallas_call(
        body, name=name,
        out_shape=(pltpu.HBM(src_thru.shape, src_thru.dtype), pltpu.HBM(land_thru.shape, land_thru.dtype)),
        in_specs=(HBM_SPEC, HBM_SPEC, SEM_SPEC, SEM_SPEC, pl.BlockSpec(memory_space=pl.ANY)),
        out_specs=(HBM_SPEC, HBM_SPEC), input_output_aliases={0: 0, 1: 1},
        compiler_params=pltpu.CompilerParams(has_side_effects=SPLIT_EFFECT),
    )(src_thru, land_thru, send_sems, recv_sems, after)


FFN_ROWS = D_FF // N_DEV
WIN_ROWS = (Q_LORA + KV_LORA + QK_ROPE) // N_DEV
WIN_ROWS_PAD = 144
WQ_ROWS = QK_HEAD * Q_LORA // D_MODEL
WKV_ROWS = 256 * KV_LORA // D_MODEL
WOUT_ROWS = V_HEAD
POOL_ROWS = 4 * 32 * POOL_GROUP // D_MODEL


COLUMN_SHARDED = ("ffn1_w_gate", "ffn1_w_up", "ffn2_w_gate", "ffn2_w_up", "mla_w_in", "mla_w_q_up", "mla_w_kv_up")


def _t(w):
    return jnp.swapaxes(w, -1, -2)


def _ffn_segments(f, i):
    return [(f + "_w_gate", i, FFN_ROWS), (f + "_w_up", i, FFN_ROWS), (f + "_w_down", i, FFN_ROWS)]


def _mixer_segments(i):
    j = i // 2
    if i % 2 == 0:
        return [("pool_w", j, POOL_ROWS)]
    return [("mla_w_in", j, WIN_ROWS_PAD), ("mla_w_q_up", j, WQ_ROWS), ("mla_w_kv_up", j, WKV_ROWS),
            ("mla_w_out", j, WOUT_ROWS)]


def _pack_shards(p, segs, dtype):
    parts = []
    for name, idx, _ in segs:
        w = p[name][idx]
        if name.endswith("w_gate") or name.endswith("w_up"):
            w = _t(w)
        elif name == "mla_w_in":
            w = jnp.pad(_t(w), ((0, WIN_ROWS_PAD - WIN_ROWS), (0, 0)))
        elif name == "mla_w_q_up":
            w = _t(w).reshape(WQ_ROWS, D_MODEL)
        elif name == "mla_w_kv_up":
            w = _t(w).reshape(WKV_ROWS, D_MODEL)
        elif name == "pool_w":
            w = w.reshape(POOL_ROWS, D_MODEL)
        parts.append(w.astype(dtype))
    return jnp.concatenate(parts, axis=0)


def _unpack_gathered(g, segs):
    out = {}
    off = 0
    for name, layer, rows in segs:
        seg = g[:, off:off + rows, :]
        off += rows
        if name.startswith("ffn"):
            out[(name, layer)] = (g, off - rows)
            continue
        if name == "mla_w_in":
            w = seg[:, :WIN_ROWS].reshape(N_DEV * WIN_ROWS, D_MODEL)
            w = jnp.pad(w, ((0, LAT_PAD - N_DEV * WIN_ROWS), (0, 0)))
        elif name == "mla_w_q_up":
            w = seg.reshape(N_HEADS, QK_HEAD, Q_LORA)
            w = jnp.pad(w, ((0, 0), (0, HEAD_PAD - QK_HEAD), (0, 0))).reshape(N_HEADS * HEAD_PAD, Q_LORA)
        elif name == "mla_w_kv_up":
            w = seg.reshape(N_HEADS * 256, KV_LORA)
        elif name == "pool_w":
            w = seg.reshape(N_DEV, 4, 32, POOL_GROUP).transpose(1, 0, 2, 3).reshape(4, POOL_GROUP, POOL_GROUP)
        else:
            w = seg.reshape(N_DEV * rows, D_MODEL)
        out[(name, layer)] = w
    return out


def _pack_grads(gr, segments):
    segs = []
    for name, layer, rows in segments:
        g = gr[(name, layer)]
        if name == "mla_w_in":
            g = g[:N_DEV * WIN_ROWS].reshape(N_DEV, WIN_ROWS, D_MODEL)
            g = jnp.pad(g, ((0, 0), (0, WIN_ROWS_PAD - WIN_ROWS), (0, 0)))
        elif name == "mla_w_q_up":
            g = g.reshape(N_HEADS, HEAD_PAD, Q_LORA)[:, :QK_HEAD].reshape(N_DEV, WQ_ROWS, D_MODEL)
        elif name == "mla_w_kv_up":
            g = g.reshape(N_DEV, WKV_ROWS, D_MODEL)
        elif name == "pool_w":
            g = g.reshape(4, N_DEV, 32, POOL_GROUP).transpose(1, 0, 2, 3).reshape(N_DEV, POOL_ROWS, D_MODEL)
        else:
            g = g.reshape(N_DEV, rows, D_MODEL)
        segs.append(g)
    return jnp.concatenate(segs, axis=1)


def _unpack_shard_grads(flat, segments):
    per = {}
    off = 0
    for name, layer, rows in segments:
        seg = flat[off:off + rows]
        off += rows
        if name == "mla_w_in":
            g = seg[:WIN_ROWS]
        elif name == "mla_w_q_up":
            g = seg.reshape(QK_HEAD, Q_LORA)
        elif name == "mla_w_kv_up":
            g = seg.reshape(256, KV_LORA)
        elif name == "pool_w":
            g = seg.reshape(4, 32, POOL_GROUP)
        else:
            g = seg
        per[(name, layer)] = g
    return per


def _pad_lanes(v, width):
    return jnp.pad(v, ((0, 0), (0, width - v.shape[-1])))


def kernel(x, positions, ffn1_norm, ffn1_w_gate, ffn1_w_up, ffn1_w_down, mix_norm, pool_w, pool_scale, mla_w_in, mla_q_norm, mla_w_q_up, mla_kv_norm, mla_w_kv_up, mla_q_head_norm, mla_k_head_norm, mla_w_out, ffn2_norm, ffn2_w_gate, ffn2_w_up, ffn2_w_down, loss_target, m_ffn1_norm, m_ffn1_w_gate, m_ffn1_w_up, m_ffn1_w_down, m_mix_norm, m_pool_w, m_pool_scale, m_mla_w_in, m_mla_q_norm, m_mla_w_q_up, m_mla_kv_norm, m_mla_w_kv_up, m_mla_q_head_norm, m_mla_k_head_norm, m_mla_w_out, m_ffn2_norm, m_ffn2_w_gate, m_ffn2_w_up, m_ffn2_w_down, v_ffn1_norm, v_ffn1_w_gate, v_ffn1_w_up, v_ffn1_w_down, v_mix_norm, v_pool_w, v_pool_scale, v_mla_w_in, v_mla_q_norm, v_mla_w_q_up, v_mla_kv_norm, v_mla_w_kv_up, v_mla_q_head_norm, v_mla_k_head_norm, v_mla_w_out, v_ffn2_norm, v_ffn2_w_gate, v_ffn2_w_up, v_ffn2_w_down):
    args = (x, positions, ffn1_norm, ffn1_w_gate, ffn1_w_up, ffn1_w_down, mix_norm, pool_w, pool_scale, mla_w_in,
            mla_q_norm, mla_w_q_up, mla_kv_norm, mla_w_kv_up, mla_q_head_norm, mla_k_head_norm, mla_w_out, ffn2_norm,
            ffn2_w_gate, ffn2_w_up, ffn2_w_down)
    p = dict(zip(NAMES, args))
    moments_m = dict(zip(WEIGHTS, (m_ffn1_norm, m_ffn1_w_gate, m_ffn1_w_up, m_ffn1_w_down, m_mix_norm, m_pool_w, m_pool_scale, m_mla_w_in, m_mla_q_norm, m_mla_w_q_up, m_mla_kv_norm, m_mla_w_kv_up, m_mla_q_head_norm, m_mla_k_head_norm, m_mla_w_out, m_ffn2_norm, m_ffn2_w_gate, m_ffn2_w_up, m_ffn2_w_down)))
    moments_v = dict(zip(WEIGHTS, (v_ffn1_norm, v_ffn1_w_gate, v_ffn1_w_up, v_ffn1_w_down, v_mix_norm, v_pool_w, v_pool_scale, v_mla_w_in, v_mla_q_norm, v_mla_w_q_up, v_mla_kv_norm, v_mla_w_kv_up, v_mla_q_head_norm, v_mla_k_head_norm, v_mla_w_out, v_ffn2_norm, v_ffn2_w_gate, v_ffn2_w_up, v_ffn2_w_down)))
    dev = 4 * lax.axis_index("x") + 2 * lax.axis_index("y") + lax.axis_index("c")

    xs = x[0]
    n = xs.shape[0]
    target = loss_target[0]

    inv_freq = 1.0 / (ROPE_THETA ** (jnp.arange(0, QK_ROPE, 2, dtype=F32) / QK_ROPE))
    ang = positions[0].astype(F32)[..., None] * inv_freq
    cos, sin = jnp.cos(ang), jnp.sin(ang)
    zero = jnp.zeros((n, 128 - QK_ROPE), F32)
    rope_cos = jnp.concatenate([cos, cos, zero], axis=1)
    rope_sin = jnp.concatenate([-sin, sin, zero], axis=1)

    ag_groups = [_ffn_segments("ffn1", 0), _mixer_segments(0) + _ffn_segments("ffn2", 0)]
    ag_groups += [_ffn_segments("ffn1", i) + _mixer_segments(i) + _ffn_segments("ffn2", i) for i in range(1, DEPTH)]
    shards = [_pack_shards(p, segs, BF16) for segs in ag_groups]
    w = {}
    no_token = jnp.zeros((8, 128), F32)

    def tied(gain, token):
        return gain + token[0, 0]

    gains_local = jnp.concatenate([_pad_lanes(mla_q_norm, 128), _pad_lanes(mla_kv_norm, 128)], axis=0)
    gains_all = _all_gather(jnp.pad(gains_local, ((0, 4), (0, 0))), "gains_all_gather")
    q_norm_full = gains_all[:, 0:2, :Q_LORA // N_DEV].transpose(1, 0, 2).reshape(2, Q_LORA)
    kv_norm_full = gains_all[:, 2:4, :KV_LORA // N_DEV].transpose(1, 0, 2).reshape(2, KV_LORA)
    ghq = _pad_lanes(mla_q_head_norm, HEAD_PAD)
    ghk = _pad_lanes(mla_k_head_norm, HEAD_PAD)

    def ffn_weights(f, i):
        parts = [w[(f + part, i)] for part in ("_w_gate", "_w_up", "_w_down")]
        return parts[0][0], tuple(off for _, off in parts)

    def mla_weights(j):
        return (w[("mla_w_in", j)], q_norm_full[j:j + 1], kv_norm_full[j:j + 1], w[("mla_w_q_up", j)],
                w[("mla_w_kv_up", j)], ghq[j:j + 1], ghk[j:j + 1], rope_cos, rope_sin)

    saved = []
    cur = xs
    gathers = [_exchange_start(shards[0], gains_all, True, "ag_start_0")]

    def gather_step(after):
        g = len(gathers) - 1
        _, landed = _exchange_wait(gathers[g], after, True, f"ag_wait_{g}")
        w.update(_unpack_gathered(lax.dynamic_update_slice(landed, shards[g][None], (dev, 0, 0)), ag_groups[g]))
        if g + 1 == len(ag_groups):
            return no_token
        gathers.append(_exchange_start(shards[g + 1], landed, True, f"ag_start_{g + 1}"))
        return gathers[-1][4]

    for i in range(DEPTH):
        j = i // 2
        st = {"x0": cur}
        token = gather_step(cur)
        cur, st["a1"], st["u1"] = _ffn_fwd(cur, tied(ffn1_norm[i:i + 1], token), *ffn_weights("ffn1", i),
                                           f"ffn1_fwd_{i}")
        st["x1"] = cur
        if i == 0:
            token = gather_step(cur)
        if i % 2 == 0:
            cur = _pool_fwd(cur, tied(mix_norm[i:i + 1], token), w[("pool_w", j)], pool_scale[j:j + 1],
                            f"pool_fwd_{i}")
        else:
            st["q"], st["k"], st["v"], vt = _mla_pre_fwd(cur, mix_norm[i:i + 1], *mla_weights(j), f"mla_pre_fwd_{i}")
            st["o"], lse, st["m"], st["p"] = _flash_fwd(st["q"], st["k"], vt, f"flash_fwd_{i}")
            tq_bwd = _tiles(n)["bwd_q"]
            st["lse"] = lse.reshape(N_HEADS, n // tq_bwd, 1, tq_bwd)
            cur = _mla_post_fwd(st["o"], cur, w[("mla_w_out", j)], f"mla_post_fwd_{i}")
        st["x2"] = cur
        cur, st["a2"], st["u2"] = _ffn_fwd(cur, ffn2_norm[i:i + 1], *ffn_weights("ffn2", i), f"ffn2_fwd_{i}")
        saved.append(st)

    dcur, sq_err = _loss_head(cur, target, "loss_head")
    loss_part = 0.5 * jnp.sum(sq_err) * (1.0 / D_MODEL)

    gr = {}
    small = {k: [None] * DEPTH for k in ("ffn1_norm", "mix_norm", "ffn2_norm")}
    small.update({k: [None] * (DEPTH // 2) for k in ("pool_scale", "mla_q_norm", "mla_kv_norm", "mla_q_head_norm",
                                                     "mla_k_head_norm")})

    me = jnp.reshape(dev, (1,)).astype(jnp.int32)
    grads = {}
    in_flight = []

    def reduce_start(segs, tag):
        started = _exchange_start(_pack_grads(gr, segs), dcur, False, f"rs_start_{tag}")
        reduce_finish(started[4])
        in_flight.append((started, segs, tag))
        return started[4]

    def reduce_finish(after):
        if in_flight:
            started, segs, tag = in_flight.pop()
            mine, landed = _exchange_wait(started, after, False, f"rs_wait_{tag}")
            grads.update(_unpack_shard_grads(_sum_exchange(mine, landed, me, f"rs_sum_{tag}"), segs))

    def ffn_backward(f, i, x_in, a, u, gain, dout):
        dx, h, dob, y, da, du, dg = _ffn_bwd(x_in, dout, a, u, gain, *ffn_weights(f, i), f"{f}_bwd_{i}")
        gr[(f + "_w_gate", i)] = _tn_matmul(da, h, f"{f}_dgate_{i}")
        gr[(f + "_w_up", i)] = _tn_matmul(du, h, f"{f}_dup_{i}")
        gr[(f + "_w_down", i)] = _tn_matmul(y, dob, f"{f}_ddown_{i}")
        small[f + "_norm"][i] = jnp.sum(dg, axis=0)
        return dx

    token = jnp.zeros((8, 128), F32)
    for i in reversed(range(DEPTH)):
        j = i // 2
        st = saved[i]
        dcur = ffn_backward("ffn2", i, st["x2"], st["a2"], st["u2"], tied(ffn2_norm[i:i + 1], token), dcur)
        if i % 2 == 0:
            dcur, dpw, dsc, dg = _pool_bwd(st["x1"], dcur, mix_norm[i:i + 1], w[("pool_w", j)], pool_scale[j:j + 1],
                                           f"pool_bwd_{i}")
            gr[("pool_w", j)] = dpw
            small["pool_scale"][j] = jnp.sum(dsc, axis=0)
            small["mix_norm"][i] = jnp.sum(dg, axis=0)
        else:
            dob, dpo, delta = _mla_out_bwd(dcur, st["o"], w[("mla_w_out", j)], f"mla_out_bwd_{i}")
            gr[("mla_w_out", j)] = _tn_matmul(st["o"], dob, f"mla_dout_{i}")
            dk, dv, dq = _flash_bwd(st["q"], st["k"], st["v"], dpo, st["lse"], delta, st["p"], st["m"],
                                    f"flash_bwd_{i}")
            (dcur, h, dlat, cqn, dqp, ckvn, dkv, dg, dgq, dgkv, dghq, dghk) = _mla_pre_bwd(
                st["x1"], dcur, dq, dk, dv, mix_norm[i:i + 1], *mla_weights(j), f"mla_pre_bwd_{i}")
            gr[("mla_w_in", j)] = _tn_matmul(dlat, h, f"mla_din_{i}")
            gr[("mla_w_q_up", j)] = _tn_matmul(dqp, cqn, f"mla_dqup_{i}")
            gr[("mla_w_kv_up", j)] = _tn_matmul(dkv, ckvn, f"mla_dkvup_{i}")
            small["mix_norm"][i] = jnp.sum(dg, axis=0)
            small["mla_q_norm"][j] = _pad_lanes(jnp.sum(dgq, axis=0)[None], D_MODEL)[0]
            small["mla_kv_norm"][j] = _pad_lanes(jnp.sum(dgkv, axis=0)[None], D_MODEL)[0]
            small["mla_q_head_norm"][j] = _pad_lanes(jnp.sum(dghq, axis=0)[None], D_MODEL)[0]
            small["mla_k_head_norm"][j] = _pad_lanes(jnp.sum(dghk, axis=0)[None], D_MODEL)[0]
        token = reduce_start(_ffn_segments("ffn2", i) + _mixer_segments(i), f"a{i}")
        if i > 0:
            dcur = ffn_backward("ffn1", i, st["x0"], st["a1"], st["u1"], tied(ffn1_norm[i:i + 1], token), dcur)
            token = reduce_start(_ffn_segments("ffn1", i), f"b{i}")
    st = saved[0]
    dcur, h, dob, y, da, du, dg = _ffn_bwd(st["x0"], dcur, st["a1"], st["u1"], tied(ffn1_norm[0:1], token),
                                           *ffn_weights("ffn1", 0), "ffn1_bwd_0")
    small["ffn1_norm"][0] = jnp.sum(dg, axis=0)
    grad_x = dcur[None]
    small_order = ("ffn1_norm", "mix_norm", "ffn2_norm", "pool_scale", "mla_q_norm", "mla_kv_norm",
                   "mla_q_head_norm", "mla_k_head_norm")
    rows = [r for k in small_order for r in small[k]]
    rows.append(jnp.zeros((D_MODEL,), F32).at[0].set(loss_part))
    rows.append(jnp.zeros((D_MODEL,), F32))
    small_sum = _sum_parts(_all_gather(jnp.stack(rows), "small_all_gather"), "small_sum")
    loss = small_sum[SM_ROWS - 2, 0]
    token = small_sum
    for seg, lhs, rhs in zip(_ffn_segments("ffn1", 0), (da, du, y), (h, h, dob)):
        gr[seg[:2]] = _tn_matmul(lhs, rhs, f"ffn1_d{seg[0][7:]}_0", behind=token)
        token = reduce_start([seg], f"b0_{seg[0][7:]}")
    last = "ffn1_w_down"

    off = 0
    for k in small_order:
        cnt = len(small[k])
        g = small_sum[off:off + cnt]
        off += cnt
        if k == "mla_q_norm":
            g = lax.dynamic_slice_in_dim(g[:, :Q_LORA], dev * (Q_LORA // N_DEV), Q_LORA // N_DEV, axis=1)
        elif k == "mla_kv_norm":
            g = lax.dynamic_slice_in_dim(g[:, :KV_LORA], dev * (KV_LORA // N_DEV), KV_LORA // N_DEV, axis=1)
        elif k in ("mla_q_head_norm", "mla_k_head_norm"):
            g = g[:, :QK_HEAD]
        grads[k] = g

    deltas, new_m, new_v = {}, {}, {}

    def update(k, behind):
        to_view = _t if k in COLUMN_SHARDED else (lambda a: a)
        shape = to_view(p[k]).shape
        operands = [to_view(a) for a in (p[k], moments_m[k], moments_v[k])]
        if k in grads:
            view = (-1, shape[-1])
            w2, m2, v2 = (a.reshape(view) for a in operands)
            results = (*_adamw(w2, grads[k].reshape(view), m2, v2, behind, "adamw_" + k), grads[k])
        else:
            view = (shape[0], -1, shape[-1])
            pieces = [grads[(k, idx)].reshape(view[1:]) for idx in range(shape[0])]
            w3, m3, v3 = (a.reshape(view) for a in operands)
            results = _adamw_layers(w3, pieces, m3, v3, behind, "adamw_" + k)
        deltas[k], new_m[k], new_v[k], grads[k] = (to_view(a.reshape(shape)) for a in results)
        return results[0]

    done = token
    for k in WEIGHTS:
        if k != last:
            done = update(k, done)
    reduce_finish(done)
    update(last, done)

    return (loss, grad_x, *[grads[k] for k in WEIGHTS], *[deltas[k] for k in WEIGHTS],
            *[new_m[k] for k in WEIGHTS], *[new_v[k] for k in WEIGHTS])
```

```python
import functools

import jax
import jax.numpy as jnp
from jax import lax
from jax.experimental import pallas as pl
from jax.experimental.pallas import tpu as pltpu

F32 = jnp.float32
BF16 = jnp.bfloat16

D_MODEL = 1024
DEPTH = 4
D_FF = 2816
POOL_WINDOWS = (2, 4, 8, 16)
POOL_GROUP = 256
POOL_HALO = 16
N_HEADS = 8
QK_NOPE = 128
QK_ROPE = 64
QK_HEAD = 192
V_HEAD = 128
Q_LORA = 768
KV_LORA = 256
ROPE_THETA = 10000.0
EPS = 1e-6
FFN_HALF = 0.5
ADAM_LR = 0.001
ADAM_B1 = 0.9
ADAM_B2 = 0.999
ADAM_EPS = 1e-08
ADAM_WD = 0.01
ADAM_STEP = 10

N_DEV = 8
HEAD_PAD = 256
LAT_PAD = 1152
VT_ROWS = 144
LOG2_E = 1.4426950408889634
ATTN_SCALE = QK_HEAD ** -0.5
LOGIT_SCALE = ATTN_SCALE * LOG2_E
V7X_VMEM_LIMIT = 56 * 1024 * 1024
FF_CHUNK = 256
SM_ROWS = 24

NAMES = ['x', 'positions', 'ffn1_norm', 'ffn1_w_gate', 'ffn1_w_up', 'ffn1_w_down', 'mix_norm', 'pool_w',
         'pool_scale', 'mla_w_in', 'mla_q_norm', 'mla_w_q_up', 'mla_kv_norm', 'mla_w_kv_up', 'mla_q_head_norm',
         'mla_k_head_norm', 'mla_w_out', 'ffn2_norm', 'ffn2_w_gate', 'ffn2_w_up', 'ffn2_w_down']
WEIGHTS = NAMES[2:]


def _tiles(n):
    return dict(ffn_fwd=min(512, n), ffn_bwd=min(256, n), fwd_k=min(512, n // 2), bwd_q=min(512, n // 2),
                mla_bwd=min(256, n), pool=min(512, n), tn=min(2048, n), rows=min(1024, n))


def _params(*sem):
    return pltpu.CompilerParams(dimension_semantics=sem, vmem_limit_bytes=V7X_VMEM_LIMIT)


def _dot(a, b):
    return jnp.dot(a, b, preferred_element_type=F32)


def _dot_nt(a, b):
    return lax.dot_general(a, b, (((1,), (1,)), ((), ())), preferred_element_type=F32)


def _dot_tn(a, b):
    return lax.dot_general(a, b, (((0,), (0,)), ((), ())), preferred_element_type=F32)


def _rowsum8(v):
    rows, w = v.shape
    return jnp.sum(v.reshape(rows // 8, 8, w), axis=0)


def _sigmoid(a):
    return 1.0 / (1.0 + jnp.exp(-a))


def _lane_sum(v):
    f = v[:, :128]
    for t in range(1, v.shape[1] // 128):
        f = f + v[:, t * 128:(t + 1) * 128]
    hi = f.astype(BF16)
    lo = (f - hi.astype(F32)).astype(BF16)
    return _dot(jnp.concatenate([hi, lo], axis=1), jnp.ones((256, 128), BF16))


def _by_row(v, r):
    return jnp.concatenate([v[:, t * 128:(t + 1) * 128] * r for t in range(v.shape[1] // 128)], axis=1)


def _rms_fwd(x, width=None):
    width = x.shape[-1] if width is None else width
    r = lax.rsqrt(_lane_sum(x * x) * (1.0 / width) + EPS)
    return _by_row(x, r), r


def _rms_bwd(dy, xhat, r, gain, width=None):
    width = xhat.shape[-1] if width is None else width
    t = dy * gain
    return _by_row(t - _by_row(xhat, _lane_sum(t * xhat) * (1.0 / width)), r)


def _full(shape):
    return pl.BlockSpec(shape, lambda *_: (0,) * len(shape))


def _load_weights(gathered, offsets, dsts, sems):
    rows = D_FF // N_DEV
    copies = [pltpu.make_async_copy(gathered.at[j, pl.ds(off, rows), :], dst.at[pl.ds(j * rows, rows), :],
                                    sems.at[N_DEV * k + j])
              for k, (off, dst) in enumerate(zip(offsets, dsts)) for j in range(N_DEV)]
    for cp in copies:
        cp.start()
    for cp in copies:
        cp.wait()


def _ffn_fwd(x, gain, gathered, offsets, name):
    n = x.shape[0]
    tm = _tiles(n)["ffn_fwd"]

    def body(x_ref, g_ref, gathered_ref, out_ref, a_ref, u_ref, wg_v, wu_v, wd_v, sems):
        @pl.when(pl.program_id(0) == 0)
        def _():
            _load_weights(gathered_ref, offsets, (wg_v, wu_v, wd_v), sems)

        xt = x_ref[...]
        xhat, _ = _rms_fwd(xt)
        h = (xhat * g_ref[...]).astype(BF16)
        acc = jnp.zeros((tm, D_MODEL), F32)
        for c in range(D_FF // FF_CHUNK):
            sl = pl.ds(c * FF_CHUNK, FF_CHUNK)
            a = _dot_nt(h, wg_v[sl, :])
            u = _dot_nt(h, wu_v[sl, :])
            a_ref[:, sl] = a.astype(BF16)
            u_ref[:, sl] = u.astype(BF16)
            y = (a * _sigmoid(a) * u).astype(BF16)
            acc = acc + _dot(y, wd_v[sl, :])
        out_ref[...] = xt + FFN_HALF * acc

    any_spec = pl.BlockSpec(memory_space=pl.ANY)
    return pl.pallas_call(
        body, name=name, grid=(n // tm,),
        in_specs=[pl.BlockSpec((tm, D_MODEL), lambda i: (i, 0)), _full((1, D_MODEL)), any_spec],
        out_specs=[pl.BlockSpec((tm, D_MODEL), lambda i: (i, 0)), pl.BlockSpec((tm, D_FF), lambda i: (i, 0)),
                   pl.BlockSpec((tm, D_FF), lambda i: (i, 0))],
        out_shape=[jax.ShapeDtypeStruct((n, D_MODEL), F32), jax.ShapeDtypeStruct((n, D_FF), BF16),
                   jax.ShapeDtypeStruct((n, D_FF), BF16)],
        scratch_shapes=[pltpu.VMEM((D_FF, D_MODEL), BF16)] * 3 + [pltpu.SemaphoreType.DMA((3 * N_DEV,))],
        compiler_params=_params("arbitrary"),
    )(x, gain, gathered)


def _ffn_bwd(x, dout, a, u, gain, gathered, offsets, name):
    n = x.shape[0]
    tm = _tiles(n)["ffn_bwd"]

    def body(x_ref, do_ref, a_ref, u_ref, g_ref, gathered_ref,
             dx_ref, h_ref, dob_ref, y_ref, da_ref, du_ref, dg_ref, wg_v, wu_v, wd_v, sems):
        @pl.when(pl.program_id(0) == 0)
        def _():
            _load_weights(gathered_ref, offsets, (wg_v, wu_v, wd_v), sems)
            dg_ref[...] = jnp.zeros_like(dg_ref)

        xt = x_ref[...]
        g = g_ref[...]
        xhat, r = _rms_fwd(xt)
        h_ref[...] = (xhat * g).astype(BF16)
        dout = do_ref[...]
        dob = (FFN_HALF * dout).astype(BF16)
        dob_ref[...] = dob
        for c in range(D_FF // FF_CHUNK):
            sl = pl.ds(c * FF_CHUNK, FF_CHUNK)
            dy = _dot_nt(dob, wd_v[sl, :])
            av = a_ref[:, sl].astype(F32)
            uv = u_ref[:, sl].astype(F32)
            s = _sigmoid(av)
            silu = av * s
            y_ref[:, sl] = (silu * uv).astype(BF16)
            du_ref[:, sl] = (dy * silu).astype(BF16)
            da_ref[:, sl] = (dy * uv * (s * (1.0 + av * (1.0 - s)))).astype(BF16)
        dh = _dot(da_ref[...], wg_v[...]) + _dot(du_ref[...], wu_v[...])
        dg_ref[...] += _rowsum8(dh * xhat)
        dx_ref[...] = dout + _rms_bwd(dh, xhat, r, g)

    any_spec = pl.BlockSpec(memory_space=pl.ANY)
    row_d = pl.BlockSpec((tm, D_MODEL), lambda i: (i, 0))
    row_f = pl.BlockSpec((tm, D_FF), lambda i: (i, 0))
    return pl.pallas_call(
        body, name=name, grid=(n // tm,),
        in_specs=[row_d, row_d, row_f, row_f, _full((1, D_MODEL)), any_spec],
        out_specs=[row_d, row_d, row_d, row_f, row_f, row_f, _full((8, D_MODEL))],
        out_shape=[jax.ShapeDtypeStruct((n, D_MODEL), F32), jax.ShapeDtypeStruct((n, D_MODEL), BF16),
                   jax.ShapeDtypeStruct((n, D_MODEL), BF16), jax.ShapeDtypeStruct((n, D_FF), BF16),
                   jax.ShapeDtypeStruct((n, D_FF), BF16), jax.ShapeDtypeStruct((n, D_FF), BF16),
                   jax.ShapeDtypeStruct((8, D_MODEL), F32)],
        scratch_shapes=[pltpu.VMEM((D_FF, D_MODEL), BF16)] * 3 + [pltpu.SemaphoreType.DMA((3 * N_DEV,))],
        compiler_params=_params("arbitrary"),
    )(x, dout, a, u, gain, gathered)


def _tn_matmul(a, b, name, behind=None):
    n, fa = a.shape
    db = b.shape[1]
    tk = _tiles(n)["tn"]
    tf = fa // 2 if (fa // 2) % 128 == 0 and fa > 1024 else fa
    behind = jnp.zeros((8, 128), F32) if behind is None else behind

    def body(a_ref, b_ref, behind_ref, o_ref):
        @pl.when(pl.program_id(1) == 0)
        def _():
            o_ref[...] = jnp.zeros_like(o_ref)

        o_ref[...] += _dot_tn(a_ref[...], b_ref[...])

    return pl.pallas_call(
        body, name=name, grid=(fa // tf, n // tk),
        in_specs=[pl.BlockSpec((tk, tf), lambda i, k: (k, i)), pl.BlockSpec((tk, db), lambda i, k: (k, 0)),
                  pl.BlockSpec(memory_space=pl.ANY)],
        out_specs=pl.BlockSpec((tf, db), lambda i, k: (i, 0)),
        out_shape=jax.ShapeDtypeStruct((fa, db), F32),
        compiler_params=_params("arbitrary", "arbitrary"),
    )(a, b, behind)


def _tn_matmul_packed(a, b, packed, slot, group_rows, name, behind=None):
    n, fa = a.shape
    db = b.shape[1]
    tk = _tiles(n)["tn"]
    per = fa // N_DEV
    slabs = N_DEV // 2
    behind = jnp.zeros((8, 128), F32) if behind is None else behind

    def body(a_ref, b_ref, behind_ref, *rest):
        o_ref = rest[-1]

        @pl.when(pl.program_id(1) == 0)
        def _():
            o_ref[...] = jnp.zeros_like(o_ref)

        o_ref[...] += _dot_tn(a_ref[...], b_ref[...]).reshape(slabs, per, db)

    any_spec = pl.BlockSpec(memory_space=pl.ANY)
    return pl.pallas_call(
        body, name=name, grid=(2, n // tk),
        in_specs=[pl.BlockSpec((tk, slabs * per), lambda i, k: (k, i)), pl.BlockSpec((tk, db), lambda i, k: (k, 0)),
                  any_spec] + ([] if packed is None else [any_spec]),
        out_specs=pl.BlockSpec((slabs, per, db), lambda i, k: (i, slot, 0)),
        out_shape=jax.ShapeDtypeStruct((N_DEV, group_rows, db), F32),
        input_output_aliases={} if packed is None else {3: 0},
        compiler_params=_params("arbitrary", "arbitrary"),
    )(a, b, behind, *([] if packed is None else [packed]))


def _loss_head(y, target, name):
    n = y.shape[0]
    tm = _tiles(n)["rows"]

    def body(y_ref, t_ref, d_ref, acc_ref):
        @pl.when(pl.program_id(0) == 0)
        def _():
            acc_ref[...] = jnp.zeros_like(acc_ref)

        d = y_ref[...] - t_ref[...]
        d_ref[...] = d * (1.0 / D_MODEL)
        acc_ref[...] += _rowsum8(d * d)

    row = pl.BlockSpec((tm, D_MODEL), lambda i: (i, 0))
    return pl.pallas_call(
        body, name=name, grid=(n // tm,), in_specs=[row, row], out_specs=[row, _full((8, D_MODEL))],
        out_shape=[jax.ShapeDtypeStruct((n, D_MODEL), F32), jax.ShapeDtypeStruct((8, D_MODEL), F32)],
        compiler_params=_params("arbitrary"),
    )(y, target)


def _window_sum(v, w, rows, forward):
    s = v
    sh = 1
    while sh < w:
        s = s + pltpu.roll(s, (rows - sh) if forward else sh, 0)
        sh *= 2
    return s


def _pool_fwd(x, gain, w, scale, name):
    n = x.shape[0]
    tm = _tiles(n)["pool"]
    hb = tm // POOL_HALO
    rows = tm + POOL_HALO

    def body(x_ref, xh_ref, g_ref, w_ref, sc_ref, out_ref):
        i = pl.program_id(0)
        xt = x_ref[...]
        e = jnp.concatenate([xh_ref[...], xt], axis=0)
        xhat, _ = _rms_fwd(e)
        row = lax.broadcasted_iota(jnp.int32, (rows, 1), 0)
        hn = jnp.where((row >= POOL_HALO) | (i > 0), xhat * g_ref[...], 0.0)
        t_glob = i * tm + row - POOL_HALO
        outs = []
        for gi, win in enumerate(POOL_WINDOWS):
            ug = hn[:, gi * POOL_GROUP:(gi + 1) * POOL_GROUP]
            cnt = jnp.maximum(jnp.minimum(t_glob + 1, win), 1).astype(F32)
            pooled = (_window_sum(ug, win, rows, False) / cnt - ug)[POOL_HALO:]
            outs.append(_dot(pooled.astype(BF16), w_ref[gi]))
        out_ref[...] = xt + jnp.concatenate(outs, axis=1) * sc_ref[...]

    return pl.pallas_call(
        body, name=name, grid=(n // tm,),
        in_specs=[pl.BlockSpec((tm, D_MODEL), lambda i: (i, 0)),
                  pl.BlockSpec((POOL_HALO, D_MODEL), lambda i: (jnp.maximum(i * hb - 1, 0), 0)),
                  _full((1, D_MODEL)), _full((4, POOL_GROUP, POOL_GROUP)), _full((1, D_MODEL))],
        out_specs=pl.BlockSpec((tm, D_MODEL), lambda i: (i, 0)),
        out_shape=jax.ShapeDtypeStruct((n, D_MODEL), F32),
        compiler_params=_params("arbitrary"),
    )(x, x, gain, w, scale)


def _pool_bwd(x, dout, gain, w, scale, name):
    n = x.shape[0]
    tm = _tiles(n)["pool"]
    hb = tm // POOL_HALO
    rows = tm + POOL_HALO
    nt = n // tm

    def body(x_ref, xh_ref, do_ref, doh_ref, g_ref, w_ref, sc_ref, dx_ref, dw_ref, dsc_ref, dg_ref):
        i = pl.program_id(0)

        @pl.when(i == 0)
        def _():
            dw_ref[...] = jnp.zeros_like(dw_ref)
            dsc_ref[...] = jnp.zeros_like(dsc_ref)
            dg_ref[...] = jnp.zeros_like(dg_ref)

        xt = x_ref[...]
        g = g_ref[...]
        e = jnp.concatenate([xh_ref[...], xt], axis=0)
        xhat_e, r_e = _rms_fwd(e)
        row = lax.broadcasted_iota(jnp.int32, (rows, 1), 0)
        hn = jnp.where((row >= POOL_HALO) | (i > 0), xhat_e * g, 0.0)
        t_prev = i * tm + row - POOL_HALO
        t_next = i * tm + row
        dout = do_ref[...]
        dt = jnp.concatenate([dout, doh_ref[...]], axis=0)
        dt = jnp.where((row < tm) | (i < nt - 1), dt, 0.0)
        dyr = dt * sc_ref[...]
        dus, dscs = [], []
        for gi, win in enumerate(POOL_WINDOWS):
            lanes = slice(gi * POOL_GROUP, (gi + 1) * POOL_GROUP)
            ug = hn[:, lanes]
            cnt = jnp.maximum(jnp.minimum(t_prev + 1, win), 1).astype(F32)
            pooled = (_window_sum(ug, win, rows, False) / cnt - ug)[POOL_HALO:].astype(BF16)
            yraw = _dot(pooled, w_ref[gi])
            dscs.append(_rowsum8(dout[:, lanes] * yraw))
            dyr_b = dyr[:, lanes].astype(BF16)
            dw_ref[gi] += _dot_tn(pooled, dyr_b[:tm])
            dpool = _dot_nt(dyr_b, w_ref[gi])
            cnt2 = jnp.minimum(t_next + 1, win).astype(F32)
            dus.append((_window_sum(dpool / cnt2, win, rows, True) - dpool)[:tm])
        dsc_ref[...] += jnp.concatenate(dscs, axis=1)
        dh = jnp.concatenate(dus, axis=1)
        xhat = xhat_e[POOL_HALO:]
        dg_ref[...] += _rowsum8(dh * xhat)
        dx_ref[...] = dout + _rms_bwd(dh, xhat, r_e[POOL_HALO:], g)

    row_d = pl.BlockSpec((tm, D_MODEL), lambda i: (i, 0))
    prev_h = pl.BlockSpec((POOL_HALO, D_MODEL), lambda i: (jnp.maximum(i * hb - 1, 0), 0))
    next_h = pl.BlockSpec((POOL_HALO, D_MODEL), lambda i: (jnp.minimum((i + 1) * hb, n // POOL_HALO - 1), 0))
    return pl.pallas_call(
        body, name=name, grid=(nt,),
        in_specs=[row_d, prev_h, row_d, next_h, _full((1, D_MODEL)), _full((4, POOL_GROUP, POOL_GROUP)),
                  _full((1, D_MODEL))],
        out_specs=[row_d, _full((4, POOL_GROUP, POOL_GROUP)), _full((8, D_MODEL)), _full((8, D_MODEL))],
        out_shape=[jax.ShapeDtypeStruct((n, D_MODEL), F32), jax.ShapeDtypeStruct((4, POOL_GROUP, POOL_GROUP), F32),
                   jax.ShapeDtypeStruct((8, D_MODEL), F32), jax.ShapeDtypeStruct((8, D_MODEL), F32)],
        compiler_params=_params("arbitrary"),
    )(x, x, dout, dout, gain, w, scale)


def _rope(v, cos, sin_signed):
    lo, hi = v[:, :128], v[:, 128:]
    lane = lax.broadcasted_iota(jnp.int32, hi.shape, 1)
    swapped = jnp.where(lane < 32, pltpu.roll(hi, 96, 1), pltpu.roll(hi, 32, 1))
    return jnp.concatenate([lo, hi * cos + swapped * sin_signed], axis=1)


def _rope_bwd(gr, cos, sin_signed):
    lo, hi = gr[:, :128], gr[:, 128:]
    t = hi * sin_signed
    lane = lax.broadcasted_iota(jnp.int32, hi.shape, 1)
    swapped = jnp.where(lane < 32, pltpu.roll(t, 96, 1), pltpu.roll(t, 32, 1))
    return jnp.concatenate([lo, hi * cos + swapped], axis=1)


def _mla_latents(h, win_ref):
    cq = _dot_nt(h, win_ref[0:Q_LORA, :])
    ckv = _dot_nt(h, win_ref[Q_LORA:Q_LORA + KV_LORA, :])
    kpe = _dot_nt(h, win_ref[Q_LORA + KV_LORA:LAT_PAD, :])
    return cq, ckv, kpe


def _mla_pre_fwd(x, gain, win, gq, gkv, wq, wkv, ghq, ghk, cos, sin_signed, name):
    n = x.shape[0]
    tm = _tiles(n)["fwd_k"]

    def body(x_ref, g_ref, win_ref, gq_ref, gkv_ref, wq_ref, wkv_ref, ghq_ref, ghk_ref, c_ref, s_ref,
             q_ref, k_ref, v_ref, vt_ref):
        xhat, _ = _rms_fwd(x_ref[...])
        h = (xhat * g_ref[...]).astype(BF16)
        cq, ckv, kpe = _mla_latents(h, win_ref)
        cqn = (_rms_fwd(cq)[0] * gq_ref[...]).astype(BF16)
        ckvn = (_rms_fwd(ckv)[0] * gkv_ref[...]).astype(BF16)
        cos, sn = c_ref[...], s_ref[...]
        for hd in range(N_HEADS):
            rws = pl.ds(hd * HEAD_PAD, HEAD_PAD)
            qh = _dot_nt(cqn, wq_ref[rws, :])
            qn = _rms_fwd(qh, QK_HEAD)[0] * ghq_ref[...]
            q_ref[hd] = (_rope(qn, cos, sn) * LOGIT_SCALE).astype(BF16)
            kvh = _dot_nt(ckvn, wkv_ref[rws, :])
            kpre = jnp.concatenate([kvh[:, :QK_NOPE], kpe], axis=1)
            kn = _rms_fwd(kpre, QK_HEAD)[0] * ghk_ref[...]
            k_ref[hd] = _rope(kn, cos, sn).astype(BF16)
            vh = kvh[:, QK_NOPE:]
            v_ref[hd] = vh.astype(BF16)
            vt_ref[hd, 0] = jnp.concatenate([vh.T, jnp.ones((VT_ROWS - V_HEAD, tm), F32)], axis=0).astype(BF16)

    row = lambda w: pl.BlockSpec((tm, w), lambda i: (i, 0))
    head = lambda w: pl.BlockSpec((N_HEADS, tm, w), lambda i: (0, i, 0))
    return pl.pallas_call(
        body, name=name, grid=(n // tm,),
        in_specs=[row(D_MODEL), _full((1, D_MODEL)), _full((LAT_PAD, D_MODEL)), _full((1, Q_LORA)),
                  _full((1, KV_LORA)), _full((N_HEADS * HEAD_PAD, Q_LORA)), _full((N_HEADS * HEAD_PAD, KV_LORA)),
                  _full((1, HEAD_PAD)), _full((1, HEAD_PAD)), row(128), row(128)],
        out_specs=[head(HEAD_PAD), head(HEAD_PAD), head(V_HEAD),
                   pl.BlockSpec((N_HEADS, 1, VT_ROWS, tm), lambda i: (0, i, 0, 0))],
        out_shape=[jax.ShapeDtypeStruct((N_HEADS, n, HEAD_PAD), BF16), jax.ShapeDtypeStruct((N_HEADS, n, HEAD_PAD), BF16),
                   jax.ShapeDtypeStruct((N_HEADS, n, V_HEAD), BF16),
                   jax.ShapeDtypeStruct((N_HEADS, n // tm, VT_ROWS, tm), BF16)],
        compiler_params=_params("arbitrary"),
    )(x, gain, win, gq, gkv, wq, wkv, ghq, ghk, cos, sin_signed)


def _flash_fwd(q, k, vt, name):
    n = q.shape[1]
    tk = _tiles(n)["fwd_k"]
    tq = 2 * tk
    nq = n // tq

    def body(q_ref, k_ref, vt_ref, o_ref, lse_ref, m_ref, p_hbm, s_scr, m_scr, acc_scr, p_scr, p_sems):
        h = pl.program_id(0)
        i = pl.program_id(1)
        qi = q_ref[0]

        def scores(j, slot):
            s_scr[slot] = _dot_nt(k_ref[0, pl.ds(pl.multiple_of(j * tk, tk), tk), :], qi)

        def p_copy(block, pslot):
            return pltpu.make_async_copy(p_scr.at[pslot], p_hbm.at[h, block], p_sems.at[pslot])

        def update(j, slot, pslot, diagonal=None):
            s = s_scr[slot]
            if diagonal is not None:
                krow = lax.broadcasted_iota(jnp.int32, (tk, tq), 0) + diagonal * tk
                qcol = lax.broadcasted_iota(jnp.int32, (tk, tq), 1)
                s = jnp.where(krow <= qcol, s, -jnp.inf)
            m = m_scr[...]
            m_new = jnp.maximum(m, jnp.max(s, axis=0, keepdims=True))
            p = jnp.exp2(s - m_new).astype(BF16)
            p_scr[pslot] = p
            acc_scr[...] = jnp.exp2(m - m_new) * acc_scr[...] + _dot(vt_ref[0, j], p)
            m_scr[...] = m_new
            m_ref[0, 0, j] = m_new

        def kv_pair(jj, diagonal):
            first = 2 * (jj % 2)
            p_copy(0, first).wait()
            p_copy(0, first + 1).wait()
            scores(2 * jj + 1, 1)
            update(2 * jj, 0, first, 0 if diagonal else None)
            if not diagonal:
                scores(2 * jj + 2, 0)
            update(2 * jj + 1, 1, first + 1, 1 if diagonal else None)
            p_copy(i * (i + 1) + 2 * jj, first).start()
            p_copy(i * (i + 1) + 2 * jj + 1, first + 1).start()

        m_scr[...] = jnp.full((1, tq), -jnp.inf, F32)
        acc_scr[...] = jnp.zeros((VT_ROWS, tq), F32)
        p_scr[...] = jnp.zeros_like(p_scr)
        for pslot in range(4):
            p_copy(nq * (nq + 1) + pslot, pslot).start()
        scores(0, 0)

        def pair(jj, carry):
            kv_pair(jj, False)
            return carry

        lax.fori_loop(0, i, pair, 0)
        kv_pair(i, True)
        l = acc_scr[V_HEAD:V_HEAD + 1, :]
        o_ref[...] = (acc_scr[0:V_HEAD, :] / l).T.astype(BF16)
        lse_ref[0, 0] = m_scr[...] + jnp.log2(l)
        for pslot in range(4):
            p_copy(0, pslot).wait()

    return pl.pallas_call(
        body, name=name, grid=(N_HEADS, nq),
        in_specs=[pl.BlockSpec((1, tq, HEAD_PAD), lambda h, i: (h, i, 0)),
                  pl.BlockSpec((1, n, HEAD_PAD), lambda h, i: (h, 0, 0)),
                  pl.BlockSpec((1, n // tk, VT_ROWS, tk), lambda h, i: (h, 0, 0, 0))],
        out_specs=[pl.BlockSpec((tq, V_HEAD), lambda h, i: (i, h)),
                   pl.BlockSpec((1, 1, 1, tq), lambda h, i: (h, i, 0, 0)),
                   pl.BlockSpec((1, 1, 2 * nq, 1, tq), lambda h, i: (h, i, 0, 0, 0)),
                   pl.BlockSpec(memory_space=pl.ANY)],
        out_shape=[jax.ShapeDtypeStruct((n, N_HEADS * V_HEAD), BF16), jax.ShapeDtypeStruct((N_HEADS, nq, 1, tq), F32),
                   jax.ShapeDtypeStruct((N_HEADS, nq, 2 * nq, 1, tq), F32),
                   jax.ShapeDtypeStruct((N_HEADS, nq * (nq + 1) + 4, tk, tq), BF16)],
        scratch_shapes=[pltpu.VMEM((2, tk, tq), F32), pltpu.VMEM((1, tq), F32), pltpu.VMEM((VT_ROWS, tq), F32),
                        pltpu.VMEM((4, tk, tq), BF16), pltpu.SemaphoreType.DMA((4,))],
        compiler_params=_params("arbitrary", "arbitrary"),
    )(q, k, vt)


def _mla_post_fwd(o, x, wout, name):
    n = x.shape[0]
    tm = _tiles(n)["rows"]

    def body(o_ref, x_ref, w_ref, out_ref):
        out_ref[...] = x_ref[...] + _dot(o_ref[...], w_ref[...])

    row = pl.BlockSpec((tm, D_MODEL), lambda i: (i, 0))
    return pl.pallas_call(
        body, name=name, grid=(n // tm,), in_specs=[row, row, _full((D_MODEL, D_MODEL))], out_specs=row,
        out_shape=jax.ShapeDtypeStruct((n, D_MODEL), F32), compiler_params=_params("arbitrary"),
    )(o, x, wout)


def _mla_out_bwd(dout, o, wout, name):
    n = dout.shape[0]
    t = _tiles(n)["bwd_q"]
    nq = n // t

    def body(do_ref, o_ref, w_ref, dob_ref, dpo_ref, dl_ref):
        dob = do_ref[...].astype(BF16)
        dob_ref[...] = dob
        dpo = _dot_nt(dob, w_ref[...])
        dpo_ref[...] = dpo.astype(BF16)
        ov = o_ref[...].astype(F32)
        for hd in range(N_HEADS):
            lanes = slice(hd * V_HEAD, (hd + 1) * V_HEAD)
            prod = dpo[:, lanes] * ov[:, lanes]
            dl_ref[hd, 0] = jnp.sum(prod.T, axis=0, keepdims=True)

    row = pl.BlockSpec((t, D_MODEL), lambda i: (i, 0))
    return pl.pallas_call(
        body, name=name, grid=(nq,), in_specs=[row, row, _full((D_MODEL, D_MODEL))],
        out_specs=[row, row, pl.BlockSpec((N_HEADS, 1, 1, t), lambda i: (0, i, 0, 0))],
        out_shape=[jax.ShapeDtypeStruct((n, D_MODEL), BF16), jax.ShapeDtypeStruct((n, D_MODEL), BF16),
                   jax.ShapeDtypeStruct((N_HEADS, nq, 1, t), F32)],
        compiler_params=_params("arbitrary"),
    )(dout, o, wout)


def _flash_bwd(q, k, v, dpo, lse, delta, p, m, name):
    n = q.shape[1]
    tq = _tiles(n)["bwd_q"]
    tk = 2 * tq
    nk = n // tk
    nqb = n // tq
    assert p.shape[2:] == (tq, tk) and m.shape[1:] == (nk, 2 * nk, 1, tk), "forward blocks are (tq keys, 2 tq queries)"

    def body(k_ref, v_ref, q_ref, do_ref, lse_ref, dl_ref, m_ref, p_hbm, dk_ref, dv_ref, dq_hbm,
             dq_acc, p_scr, dp_scr, sem, p_sems):
        h = pl.program_id(0)
        j = pl.program_id(1)

        @pl.when(j == 0)
        def _():
            dq_acc[...] = jnp.zeros_like(dq_acc)

        dk_ref[...] = jnp.zeros_like(dk_ref)
        dv_ref[...] = jnp.zeros_like(dv_ref)
        kj = k_ref[0]
        vj = v_ref[0]
        npairs = nk - 1 - j

        def block(t):
            return jnp.where(t < 2 * npairs, 2 * j + 2 + t, 2 * j + (t - 2 * npairs))

        def p_copy(i, first, half):
            tile = i // 2
            return pltpu.make_async_copy(p_hbm.at[h, pl.ds(tile * (tile + 1) + 2 * j, 2), :, pl.ds(half * tq, tq)],
                                         p_scr.at[first + half], p_sems.at[first + half])

        def fetch_pair(t):
            for half in range(2):
                p_copy(block(2 * t + half), 2 * (t % 2), half).start()

        def dp_ahead(i, half):
            dp_scr[half] = _dot_nt(vj, do_ref[pl.ds(pl.multiple_of(i * tq, tq), tq), :])

        def update(i, first, half):
            rws = pl.ds(pl.multiple_of(i * tq, tq), tq)
            lse_i = lse_ref[0, i]
            lanes = slice(half * tq, (half + 1) * tq)
            pv = jnp.concatenate(
                [p_scr[first + half, b].astype(F32) * jnp.exp2(m_ref[0, i // 2, 2 * j + b][:, lanes] - lse_i)
                 for b in range(2)], axis=0)
            dv_ref[0] += _dot(pv.astype(BF16), do_ref[rws, :])
            ds = (pv * (dp_scr[half] - dl_ref[0, i])).astype(BF16)
            dk_ref[0] += _dot(ds, q_ref[0, rws, :])
            dq_acc[rws, :] += _dot_tn(ds, kj)

        def q_pair(t, last):
            first = 2 * (t % 2)
            for half in range(2):
                p_copy(0, first, half).wait()
            if not last:
                fetch_pair(t + 1)
            dp_ahead(block(2 * t + 1), 1)
            update(block(2 * t), first, 0)
            if not last:
                dp_ahead(block(2 * t + 2), 0)
            update(block(2 * t + 1), first, 1)

        fetch_pair(0)
        dp_ahead(block(0), 0)

        def pair(t, carry):
            q_pair(t, False)
            return carry

        lax.fori_loop(0, npairs, pair, 0)
        q_pair(npairs, True)
        dk_ref[...] = dk_ref[...] * (ATTN_SCALE / LOGIT_SCALE)

        @pl.when(j == nk - 1)
        def _():
            dq_acc[...] = dq_acc[...] * ATTN_SCALE
            cp = pltpu.make_async_copy(dq_acc, dq_hbm.at[h], sem)
            cp.start()
            cp.wait()

    resident = dict(pipeline_mode=pl.Buffered(1))
    return pl.pallas_call(
        body, name=name, grid=(N_HEADS, nk),
        in_specs=[pl.BlockSpec((1, tk, HEAD_PAD), lambda h, j: (h, j, 0)),
                  pl.BlockSpec((1, tk, V_HEAD), lambda h, j: (h, j, 0)),
                  pl.BlockSpec((1, n, HEAD_PAD), lambda h, j: (h, 0, 0), **resident),
                  pl.BlockSpec((n, V_HEAD), lambda h, j: (0, h), **resident),
                  pl.BlockSpec((1, nqb, 1, tq), lambda h, j: (h, 0, 0, 0)),
                  pl.BlockSpec((1, nqb, 1, tq), lambda h, j: (h, 0, 0, 0)),
                  pl.BlockSpec((1, nk, 2 * nk, 1, tk), lambda h, j: (h, 0, 0, 0, 0), **resident),
                  pl.BlockSpec(memory_space=pl.ANY)],
        out_specs=[pl.BlockSpec((1, tk, HEAD_PAD), lambda h, j: (h, j, 0)),
                   pl.BlockSpec((1, tk, V_HEAD), lambda h, j: (h, j, 0)),
                   pl.BlockSpec(memory_space=pl.ANY)],
        out_shape=[jax.ShapeDtypeStruct((N_HEADS, n, HEAD_PAD), F32), jax.ShapeDtypeStruct((N_HEADS, n, V_HEAD), F32),
                   jax.ShapeDtypeStruct((N_HEADS, n, HEAD_PAD), F32)],
        scratch_shapes=[pltpu.VMEM((n, HEAD_PAD), F32), pltpu.VMEM((4, 2, tq, tq), BF16), pltpu.VMEM((2, tk, tq), F32),
                        pltpu.SemaphoreType.DMA(()), pltpu.SemaphoreType.DMA((4,))],
        compiler_params=_params("arbitrary", "arbitrary"),
    )(k, v, q, dpo, lse, delta, m, p)


def _mla_pre_bwd(x, dout, dq, dk, dv, gain, win, gq, gkv, wq, wkv, ghq, ghk, cos, sin_signed, name):
    n = x.shape[0]
    tm = _tiles(n)["mla_bwd"]
    hw = N_HEADS * HEAD_PAD

    def body(x_ref, do_ref, dq_ref, dk_ref, dv_ref, g_ref, win_ref, gq_ref, gkv_ref, wq_ref, wkv_ref, ghq_ref, ghk_ref,
             c_ref, s_ref, dx_ref, h_ref, dlat_ref, cqn_ref, dqp_ref, ckvn_ref, dkv_ref,
             dg_ref, dgq_ref, dgkv_ref, dghq_ref, dghk_ref):
        @pl.when(pl.program_id(0) == 0)
        def _():
            for ref in (dg_ref, dgq_ref, dgkv_ref, dghq_ref, dghk_ref):
                ref[...] = jnp.zeros_like(ref)

        g = g_ref[...]
        xhat, r = _rms_fwd(x_ref[...])
        h = (xhat * g).astype(BF16)
        h_ref[...] = h
        cq, ckv, kpe = _mla_latents(h, win_ref)
        cqhat, rcq = _rms_fwd(cq)
        ckvhat, rckv = _rms_fwd(ckv)
        cqn = (cqhat * gq_ref[...]).astype(BF16)
        ckvn = (ckvhat * gkv_ref[...]).astype(BF16)
        cqn_ref[...] = cqn
        ckvn_ref[...] = ckvn
        cos, sn = c_ref[...], s_ref[...]
        ghq, ghk = ghq_ref[...], ghk_ref[...]
        dkpe = jnp.zeros((tm, 128), F32)
        dghq = jnp.zeros((8, HEAD_PAD), F32)
        dghk = jnp.zeros((8, HEAD_PAD), F32)
        for hd in range(N_HEADS):
            rws = pl.ds(hd * HEAD_PAD, HEAD_PAD)
            lanes = slice(hd * HEAD_PAD, (hd + 1) * HEAD_PAD)
            qhat, rq = _rms_fwd(_dot_nt(cqn, wq_ref[rws, :]), QK_HEAD)
            gqn = _rope_bwd(dq_ref[hd], cos, sn)
            dghq = dghq + _rowsum8(gqn * qhat)
            dqpre = _rms_bwd(gqn, qhat, rq, ghq, QK_HEAD).astype(BF16)
            dqp_ref[:, lanes] = dqpre
            kvh = _dot_nt(ckvn, wkv_ref[rws, :])
            khat, rk = _rms_fwd(jnp.concatenate([kvh[:, :QK_NOPE], kpe], axis=1), QK_HEAD)
            gkn = _rope_bwd(dk_ref[hd], cos, sn)
            dghk = dghk + _rowsum8(gkn * khat)
            dkpre = _rms_bwd(gkn, khat, rk, ghk, QK_HEAD)
            dkpe = dkpe + dkpre[:, QK_NOPE:]
            dkvh = jnp.concatenate([dkpre[:, :QK_NOPE], dv_ref[hd]], axis=1).astype(BF16)
            dkv_ref[:, lanes] = dkvh
        dcqn = _dot(dqp_ref[...], wq_ref[...])
        dckvn = _dot(dkv_ref[...], wkv_ref[...])
        dghq_ref[...] += dghq
        dghk_ref[...] += dghk
        dgq_ref[...] += _rowsum8(dcqn * cqhat)
        dgkv_ref[...] += _rowsum8(dckvn * ckvhat)
        dlat = jnp.concatenate([_rms_bwd(dcqn, cqhat, rcq, gq_ref[...]), _rms_bwd(dckvn, ckvhat, rckv, gkv_ref[...]),
                                dkpe], axis=1).astype(BF16)
        dlat_ref[...] = dlat
        dh = _dot(dlat, win_ref[...])
        dg_ref[...] += _rowsum8(dh * xhat)
        dx_ref[...] = do_ref[...] + _rms_bwd(dh, xhat, r, g)

    row = lambda w: pl.BlockSpec((tm, w), lambda i: (i, 0))
    head = lambda w: pl.BlockSpec((N_HEADS, tm, w), lambda i: (0, i, 0))
    sds = jax.ShapeDtypeStruct
    return pl.pallas_call(
        body, name=name, grid=(n // tm,),
        in_specs=[row(D_MODEL), row(D_MODEL), head(HEAD_PAD), head(HEAD_PAD), head(V_HEAD), _full((1, D_MODEL)),
                  _full((LAT_PAD, D_MODEL)), _full((1, Q_LORA)), _full((1, KV_LORA)), _full((hw, Q_LORA)),
                  _full((hw, KV_LORA)), _full((1, HEAD_PAD)), _full((1, HEAD_PAD)), row(128), row(128)],
        out_specs=[row(D_MODEL), row(D_MODEL), row(LAT_PAD), row(Q_LORA), row(hw), row(KV_LORA), row(hw),
                   _full((8, D_MODEL)), _full((8, Q_LORA)), _full((8, KV_LORA)), _full((8, HEAD_PAD)),
                   _full((8, HEAD_PAD))],
        out_shape=[sds((n, D_MODEL), F32), sds((n, D_MODEL), BF16), sds((n, LAT_PAD), BF16), sds((n, Q_LORA), BF16),
                   sds((n, hw), BF16), sds((n, KV_LORA), BF16), sds((n, hw), BF16), sds((8, D_MODEL), F32),
                   sds((8, Q_LORA), F32), sds((8, KV_LORA), F32), sds((8, HEAD_PAD), F32), sds((8, HEAD_PAD), F32)],
        compiler_params=_params("arbitrary"),
    )(x, dout, dq, dk, dv, gain, win, gq, gkv, wq, wkv, ghq, ghk, cos, sin_signed)


def _adamw(w, g, m, v, behind, name):
    rows, cols = w.shape
    tr = rows
    for cand in (512, 256, 128, 64, 32, 16, 8):
        if rows % cand == 0 and rows > cand:
            tr = cand
            break

    def body(w_ref, g_ref, m_ref, v_ref, behind_ref, d_ref, mo_ref, vo_ref):
        d_ref[...], mo_ref[...], vo_ref[...] = _adamw_step(w_ref[...], g_ref[...], m_ref[...], v_ref[...])

    blk = pl.BlockSpec((tr, cols), lambda i: (i, 0))
    return pl.pallas_call(
        body, name=name, grid=(rows // tr,), in_specs=[blk] * 4 + [pl.BlockSpec(memory_space=pl.ANY)],
        out_specs=[blk] * 3, out_shape=[jax.ShapeDtypeStruct((rows, cols), F32)] * 3,
        compiler_params=_params("arbitrary"),
    )(w, g, m, v, behind)


def _adamw_step(w, g, m, v):
    mn = ADAM_B1 * m + (1.0 - ADAM_B1) * g
    vn = ADAM_B2 * v + (1.0 - ADAM_B2) * (g * g)
    m_hat = mn / (1.0 - ADAM_B1 ** ADAM_STEP)
    v_hat = vn / (1.0 - ADAM_B2 ** ADAM_STEP)
    return -ADAM_LR * (m_hat / (jnp.sqrt(v_hat) + ADAM_EPS) + ADAM_WD * w), mn, vn


def _adamw_layers(w, grads, m, v, behind, name):
    layers, r, c = w.shape
    tr = _row_tile(r, 512)

    def body(w_ref, m_ref, v_ref, *rest):
        g_refs, (d_ref, mo_ref, vo_ref, go_ref) = rest[:layers], rest[layers + 1:]
        layer = pl.program_id(1)
        g = g_refs[0][...]
        for k in range(1, layers):
            g = jnp.where(layer == k, g_refs[k][...], g)
        d_ref[0], mo_ref[0], vo_ref[0] = _adamw_step(w_ref[0], g, m_ref[0], v_ref[0])
        go_ref[0] = g

    stacked = pl.BlockSpec((1, tr, c), lambda i, layer: (layer, i, 0))
    piece = pl.BlockSpec((tr, c), lambda i, layer: (i, 0))
    return pl.pallas_call(
        body, name=name, grid=(r // tr, layers),
        in_specs=[stacked] * 3 + [piece] * layers + [pl.BlockSpec(memory_space=pl.ANY)],
        out_specs=[stacked] * 4, out_shape=[jax.ShapeDtypeStruct((layers, r, c), F32)] * 4,
        compiler_params=_params("arbitrary", "arbitrary"),
    )(w, m, v, *grads, behind)


def _sum_parts(parts, name):
    k, r, c = parts.shape
    tr = min(r, 512)

    def body(p_ref, o_ref):
        acc = p_ref[0]
        for j in range(1, k):
            acc = acc + p_ref[j]
        o_ref[...] = acc

    return pl.pallas_call(
        body, name=name, grid=(r // tr,), in_specs=[pl.BlockSpec((k, tr, c), lambda i: (0, i, 0))],
        out_specs=pl.BlockSpec((tr, c), lambda i: (i, 0)), out_shape=jax.ShapeDtypeStruct((r, c), parts.dtype),
        compiler_params=_params("arbitrary"),
    )(parts)


def _row_tile(r, most=256):
    best = r
    for cand in range(8, most + 1, 8):
        if r % cand == 0:
            best = cand
    return best


def _sum_exchange(mine, landed, me, name):
    _, r, c = mine.shape
    tr = _row_tile(r)

    def body(me_ref, m_ref, l_ref, o_ref):
        acc = m_ref[0]
        for k in range(1, N_DEV):
            acc = acc + l_ref[k]
        o_ref[...] = acc

    return pl.pallas_call(
        body, name=name,
        grid_spec=pltpu.PrefetchScalarGridSpec(
            num_scalar_prefetch=1, grid=(r // tr,),
            in_specs=[pl.BlockSpec((1, tr, c), lambda i, me_ref: (me_ref[0], i, 0)),
                      pl.BlockSpec((N_DEV, tr, c), lambda i, me_ref: (0, i, 0))],
            out_specs=pl.BlockSpec((tr, c), lambda i, me_ref: (i, 0))),
        out_shape=jax.ShapeDtypeStruct((r, c), mine.dtype), compiler_params=_params("arbitrary"),
    )(me, mine, landed)


MESH = pl.DeviceIdType.MESH


def _all_gather(x, name):
    r, c = x.shape

    def body(x_ref, out_ref, send_sems, recv_sems, local_sem):
        mx, my, mc = lax.axis_index("x"), lax.axis_index("y"), lax.axis_index("c")
        me, sibling = (mx, my, mc), (mx, my, 1 - mc)
        chips = [(1 - mx, my), (mx, 1 - my), (1 - mx, 1 - my)]

        def slot(px, py, pc):
            return out_ref.at[4 * px + 2 * py + pc]

        def copy(k, block, to, src=None):
            return pltpu.make_async_remote_copy(
                src_ref=slot(*block) if src is None else src, dst_ref=slot(*block),
                send_sem=send_sems.at[k], recv_sem=recv_sems.at[k], device_id=to, device_id_type=MESH)

        mine = pltpu.make_async_copy(x_ref, slot(*me), local_sem)
        mine.start()
        first = [copy(0, me, sibling, src=x_ref)]
        first += [copy(1 + j, me, (*chip, mc), src=x_ref) for j, chip in enumerate(chips)]
        for cp in first:
            cp.start()
        passed = [copy(4 + j, (*chip, mc), sibling) for j, chip in enumerate(chips)]
        for j, chip in enumerate(chips):
            copy(1 + j, (*chip, mc), me).wait_recv()
            passed[j].start()
        copy(0, sibling, me).wait_recv()
        for j, chip in enumerate(chips):
            copy(4 + j, (*chip, 1 - mc), me).wait_recv()
        for cp in first + passed:
            cp.wait_send()
        mine.wait()

    any_spec = pl.BlockSpec(memory_space=pl.ANY)
    return pl.pallas_call(
        body, name=name, in_specs=[any_spec], out_specs=any_spec,
        out_shape=jax.ShapeDtypeStruct((N_DEV, r, c), x.dtype),
        scratch_shapes=[pltpu.SemaphoreType.DMA((7,)), pltpu.SemaphoreType.DMA((7,)), pltpu.SemaphoreType.DMA(())],
    )(x)


HBM_SPEC = pl.BlockSpec(memory_space=pltpu.HBM)
SEM_SPEC = pl.BlockSpec(memory_space=pltpu.SEMAPHORE)
SPLIT_EFFECT = pltpu.SideEffectType.DATAFLOW_SIDE_EFFECTING


def _exchange_copies(src_ref, land_ref, send_sems, recv_sems, gather):
    mx, my, mc = lax.axis_index("x"), lax.axis_index("y"), lax.axis_index("c")
    me = 4 * mx + 2 * my + mc
    copies = []
    for k in range(1, N_DEV):
        px = 1 - mx if k & 4 else mx
        py = 1 - my if k & 2 else my
        pc = 1 - mc if k & 1 else mc
        src = src_ref if gather else src_ref.at[4 * px + 2 * py + pc]
        dst = land_ref.at[me] if gather else land_ref.at[k]
        copies.append(pltpu.make_async_remote_copy(
            src_ref=src, dst_ref=dst, send_sem=send_sems.at[k - 1], recv_sem=recv_sems.at[k - 1],
            device_id=(px, py, pc), device_id_type=MESH))
    return copies


def _exchange_start(src, after, gather, name):
    land_shape = (N_DEV,) + src.shape[-2:]

    def body(src_ref, land_ref, after_ref, send_sems, recv_sems, src_thru, land_thru, token):
        for cp in _exchange_copies(src_ref, land_ref, send_sems, recv_sems, gather):
            cp.start()
        token[...] = jnp.zeros_like(token)

    return pl.pallas_call(
        body, name=name,
        out_shape=(pltpu.SemaphoreType.DMA((N_DEV - 1,)), pltpu.SemaphoreType.DMA((N_DEV - 1,)),
                   pltpu.HBM(src.shape, src.dtype), pltpu.HBM(land_shape, src.dtype),
                   jax.ShapeDtypeStruct((8, 128), F32)),
        in_specs=(HBM_SPEC, HBM_SPEC, pl.BlockSpec(memory_space=pl.ANY)),
        out_specs=(SEM_SPEC, SEM_SPEC, HBM_SPEC, HBM_SPEC, pl.BlockSpec(memory_space=pltpu.VMEM)),
        input_output_aliases={0: 2, 1: 3},
        compiler_params=pltpu.CompilerParams(has_side_effects=SPLIT_EFFECT),
    )(pltpu.with_memory_space_constraint(src, pltpu.HBM),
      pltpu.with_memory_space_constraint(lax.empty(land_shape, src.dtype), pltpu.HBM), after)


def _exchange_wait(started, after, gather, name):
    send_sems, recv_sems, src_thru, land_thru, _ = started

    def body(src_ref, land_ref, send_sems, recv_sems, after_ref, src_out, land_out):
        for cp in _exchange_copies(src_ref, land_ref, send_sems, recv_sems, gather):
            cp.wait_send()
            cp.wait_recv()

    return pl.pallas_call(
        body, name=name,
        out_shape=(pltpu.HBM(src_thru.shape, src_thru.dtype), pltpu.HBM(land_thru.shape, land_thru.dtype)),
        in_specs=(HBM_SPEC, HBM_SPEC, SEM_SPEC, SEM_SPEC, pl.BlockSpec(memory_space=pl.ANY)),
        out_specs=(HBM_SPEC, HBM_SPEC), input_output_aliases={0: 0, 1: 1},
        compiler_params=pltpu.CompilerParams(has_side_effects=SPLIT_EFFECT),
    )(src_thru, land_thru, send_sems, recv_sems, after)


FFN_ROWS = D_FF // N_DEV
WIN_ROWS = (Q_LORA + KV_LORA + QK_ROPE) // N_DEV
WIN_ROWS_PAD = 144
WQ_ROWS = QK_HEAD * Q_LORA // D_MODEL
WKV_ROWS = 256 * KV_LORA // D_MODEL
WOUT_ROWS = V_HEAD
POOL_ROWS = 4 * 32 * POOL_GROUP // D_MODEL


COLUMN_SHARDED = ("ffn1_w_gate", "ffn1_w_up", "ffn2_w_gate", "ffn2_w_up", "mla_w_in", "mla_w_q_up", "mla_w_kv_up")


def _t(w):
    return jnp.swapaxes(w, -1, -2)


def _ffn_segments(f, i):
    return [(f + "_w_gate", i, FFN_ROWS), (f + "_w_up", i, FFN_ROWS), (f + "_w_down", i, FFN_ROWS)]


def _mixer_segments(i):
    j = i // 2
    if i % 2 == 0:
        return [("pool_w", j, POOL_ROWS)]
    return [("mla_w_in", j, WIN_ROWS_PAD), ("mla_w_q_up", j, WQ_ROWS), ("mla_w_kv_up", j, WKV_ROWS),
            ("mla_w_out", j, WOUT_ROWS)]


def _pack_shards(p, segs, dtype):
    parts = []
    for name, idx, _ in segs:
        w = p[name][idx]
        if name.endswith("w_gate") or name.endswith("w_up"):
            w = _t(w)
        elif name == "mla_w_in":
            w = jnp.pad(_t(w), ((0, WIN_ROWS_PAD - WIN_ROWS), (0, 0)))
        elif name == "mla_w_q_up":
            w = _t(w).reshape(WQ_ROWS, D_MODEL)
        elif name == "mla_w_kv_up":
            w = _t(w).reshape(WKV_ROWS, D_MODEL)
        elif name == "pool_w":
            w = w.reshape(POOL_ROWS, D_MODEL)
        parts.append(w.astype(dtype))
    return jnp.concatenate(parts, axis=0)


def _unpack_gathered(g, segs):
    out = {}
    off = 0
    for name, layer, rows in segs:
        seg = g[:, off:off + rows, :]
        off += rows
        if name.startswith("ffn"):
            out[(name, layer)] = (g, off - rows)
            continue
        if name == "mla_w_in":
            w = seg[:, :WIN_ROWS].reshape(N_DEV * WIN_ROWS, D_MODEL)
            w = jnp.pad(w, ((0, LAT_PAD - N_DEV * WIN_ROWS), (0, 0)))
        elif name == "mla_w_q_up":
            w = seg.reshape(N_HEADS, QK_HEAD, Q_LORA)
            w = jnp.pad(w, ((0, 0), (0, HEAD_PAD - QK_HEAD), (0, 0))).reshape(N_HEADS * HEAD_PAD, Q_LORA)
        elif name == "mla_w_kv_up":
            w = seg.reshape(N_HEADS * 256, KV_LORA)
        elif name == "pool_w":
            w = seg.reshape(N_DEV, 4, 32, POOL_GROUP).transpose(1, 0, 2, 3).reshape(4, POOL_GROUP, POOL_GROUP)
        else:
            w = seg.reshape(N_DEV * rows, D_MODEL)
        out[(name, layer)] = w
    return out


def _pack_grads(gr, segments):
    segs = []
    for name, layer, rows in segments:
        g = gr[(name, layer)]
        if name == "mla_w_in":
            g = g[:N_DEV * WIN_ROWS].reshape(N_DEV, WIN_ROWS, D_MODEL)
            g = jnp.pad(g, ((0, 0), (0, WIN_ROWS_PAD - WIN_ROWS), (0, 0)))
        elif name == "mla_w_q_up":
            g = g.reshape(N_HEADS, HEAD_PAD, Q_LORA)[:, :QK_HEAD].reshape(N_DEV, WQ_ROWS, D_MODEL)
        elif name == "mla_w_kv_up":
            g = g.reshape(N_DEV, WKV_ROWS, D_MODEL)
        elif name == "pool_w":
            g = g.reshape(4, N_DEV, 32, POOL_GROUP).transpose(1, 0, 2, 3).reshape(N_DEV, POOL_ROWS, D_MODEL)
        else:
            g = g.reshape(N_DEV, rows, D_MODEL)
        segs.append(g)
    return jnp.concatenate(segs, axis=1)


def _unpack_shard_grads(flat, segments):
    per = {}
    off = 0
    for name, layer, rows in segments:
        seg = flat[off:off + rows]
        off += rows
        if name == "mla_w_in":
            g = seg[:WIN_ROWS]
        elif name == "mla_w_q_up":
            g = seg.reshape(QK_HEAD, Q_LORA)
        elif name == "mla_w_kv_up":
            g = seg.reshape(256, KV_LORA)
        elif name == "pool_w":
            g = seg.reshape(4, 32, POOL_GROUP)
        else:
            g = seg
        per[(name, layer)] = g
    return per


def _pad_lanes(v, width):
    return jnp.pad(v, ((0, 0), (0, width - v.shape[-1])))


def kernel(x, positions, ffn1_norm, ffn1_w_gate, ffn1_w_up, ffn1_w_down, mix_norm, pool_w, pool_scale, mla_w_in, mla_q_norm, mla_w_q_up, mla_kv_norm, mla_w_kv_up, mla_q_head_norm, mla_k_head_norm, mla_w_out, ffn2_norm, ffn2_w_gate, ffn2_w_up, ffn2_w_down, loss_target, m_ffn1_norm, m_ffn1_w_gate, m_ffn1_w_up, m_ffn1_w_down, m_mix_norm, m_pool_w, m_pool_scale, m_mla_w_in, m_mla_q_norm, m_mla_w_q_up, m_mla_kv_norm, m_mla_w_kv_up, m_mla_q_head_norm, m_mla_k_head_norm, m_mla_w_out, m_ffn2_norm, m_ffn2_w_gate, m_ffn2_w_up, m_ffn2_w_down, v_ffn1_norm, v_ffn1_w_gate, v_ffn1_w_up, v_ffn1_w_down, v_mix_norm, v_pool_w, v_pool_scale, v_mla_w_in, v_mla_q_norm, v_mla_w_q_up, v_mla_kv_norm, v_mla_w_kv_up, v_mla_q_head_norm, v_mla_k_head_norm, v_mla_w_out, v_ffn2_norm, v_ffn2_w_gate, v_ffn2_w_up, v_ffn2_w_down):
    args = (x, positions, ffn1_norm, ffn1_w_gate, ffn1_w_up, ffn1_w_down, mix_norm, pool_w, pool_scale, mla_w_in,
            mla_q_norm, mla_w_q_up, mla_kv_norm, mla_w_kv_up, mla_q_head_norm, mla_k_head_norm, mla_w_out, ffn2_norm,
            ffn2_w_gate, ffn2_w_up, ffn2_w_down)
    p = dict(zip(NAMES, args))
    moments_m = dict(zip(WEIGHTS, (m_ffn1_norm, m_ffn1_w_gate, m_ffn1_w_up, m_ffn1_w_down, m_mix_norm, m_pool_w, m_pool_scale, m_mla_w_in, m_mla_q_norm, m_mla_w_q_up, m_mla_kv_norm, m_mla_w_kv_up, m_mla_q_head_norm, m_mla_k_head_norm, m_mla_w_out, m_ffn2_norm, m_ffn2_w_gate, m_ffn2_w_up, m_ffn2_w_down)))
    moments_v = dict(zip(WEIGHTS, (v_ffn1_norm, v_ffn1_w_gate, v_ffn1_w_up, v_ffn1_w_down, v_mix_norm, v_pool_w, v_pool_scale, v_mla_w_in, v_mla_q_norm, v_mla_w_q_up, v_mla_kv_norm, v_mla_w_kv_up, v_mla_q_head_norm, v_mla_k_head_norm, v_mla_w_out, v_ffn2_norm, v_ffn2_w_gate, v_ffn2_w_up, v_ffn2_w_down)))
    dev = 4 * lax.axis_index("x") + 2 * lax.axis_index("y") + lax.axis_index("c")

    xs = x[0]
    n = xs.shape[0]
    target = loss_target[0]

    inv_freq = 1.0 / (ROPE_THETA ** (jnp.arange(0, QK_ROPE, 2, dtype=F32) / QK_ROPE))
    ang = positions[0].astype(F32)[..., None] * inv_freq
    cos, sin = jnp.cos(ang), jnp.sin(ang)
    zero = jnp.zeros((n, 128 - QK_ROPE), F32)
    rope_cos = jnp.concatenate([cos, cos, zero], axis=1)
    rope_sin = jnp.concatenate([-sin, sin, zero], axis=1)

    ag_groups = [_ffn_segments("ffn1", 0), _mixer_segments(0) + _ffn_segments("ffn2", 0)]
    ag_groups += [_ffn_segments("ffn1", i) + _mixer_segments(i) + _ffn_segments("ffn2", i) for i in range(1, DEPTH)]
    shards = [_pack_shards(p, segs, BF16) for segs in ag_groups]
    w = {}
    no_token = jnp.zeros((8, 128), F32)

    def tied(gain, token):
        return gain + token[0, 0]

    gains_local = jnp.concatenate([_pad_lanes(mla_q_norm, 128), _pad_lanes(mla_kv_norm, 128)], axis=0)
    gains_all = _all_gather(jnp.pad(gains_local, ((0, 4), (0, 0))), "gains_all_gather")
    q_norm_full = gains_all[:, 0:2, :Q_LORA // N_DEV].transpose(1, 0, 2).reshape(2, Q_LORA)
    kv_norm_full = gains_all[:, 2:4, :KV_LORA // N_DEV].transpose(1, 0, 2).reshape(2, KV_LORA)
    ghq = _pad_lanes(mla_q_head_norm, HEAD_PAD)
    ghk = _pad_lanes(mla_k_head_norm, HEAD_PAD)

    def ffn_weights(f, i):
        parts = [w[(f + part, i)] for part in ("_w_gate", "_w_up", "_w_down")]
        return parts[0][0], tuple(off for _, off in parts)

    def mla_weights(j):
        return (w[("mla_w_in", j)], q_norm_full[j:j + 1], kv_norm_full[j:j + 1], w[("mla_w_q_up", j)],
                w[("mla_w_kv_up", j)], ghq[j:j + 1], ghk[j:j + 1], rope_cos, rope_sin)

    saved = []
    cur = xs
    gathers = [_exchange_start(shards[0], gains_all, True, "ag_start_0")]

    def gather_step(after):
        g = len(gathers) - 1
        _, landed = _exchange_wait(gathers[g], after, True, f"ag_wait_{g}")
        w.update(_unpack_gathered(lax.dynamic_update_slice(landed, shards[g][None], (dev, 0, 0)), ag_groups[g]))
        if g + 1 == len(ag_groups):
            return no_token
        gathers.append(_exchange_start(shards[g + 1], landed, True, f"ag_start_{g + 1}"))
        return gathers[-1][4]

    for i in range(DEPTH):
        j = i // 2
        st = {"x0": cur}
        token = gather_step(cur)
        cur, st["a1"], st["u1"] = _ffn_fwd(cur, tied(ffn1_norm[i:i + 1], token), *ffn_weights("ffn1", i),
                                           f"ffn1_fwd_{i}")
        st["x1"] = cur
        if i == 0:
            token = gather_step(cur)
        if i % 2 == 0:
            cur = _pool_fwd(cur, tied(mix_norm[i:i + 1], token), w[("pool_w", j)], pool_scale[j:j + 1],
                            f"pool_fwd_{i}")
        else:
            st["q"], st["k"], st["v"], vt = _mla_pre_fwd(cur, mix_norm[i:i + 1], *mla_weights(j), f"mla_pre_fwd_{i}")
            st["o"], lse, st["m"], st["p"] = _flash_fwd(st["q"], st["k"], vt, f"flash_fwd_{i}")
            tq_bwd = _tiles(n)["bwd_q"]
            st["lse"] = lse.reshape(N_HEADS, n // tq_bwd, 1, tq_bwd)
            cur = _mla_post_fwd(st["o"], cur, w[("mla_w_out", j)], f"mla_post_fwd_{i}")
        st["x2"] = cur
        cur, st["a2"], st["u2"] = _ffn_fwd(cur, ffn2_norm[i:i + 1], *ffn_weights("ffn2", i), f"ffn2_fwd_{i}")
        saved.append(st)

    dcur, sq_err = _loss_head(cur, target, "loss_head")
    loss_part = 0.5 * jnp.sum(sq_err) * (1.0 / D_MODEL)

    gr = {}
    small = {k: [None] * DEPTH for k in ("ffn1_norm", "mix_norm", "ffn2_norm")}
    small.update({k: [None] * (DEPTH // 2) for k in ("pool_scale", "mla_q_norm", "mla_kv_norm", "mla_q_head_norm",
                                                     "mla_k_head_norm")})

    me = jnp.reshape(dev, (1,)).astype(jnp.int32)
    grads = {}
    in_flight = []

    def reduce_start(packed, segs, tag):
        started = _exchange_start(packed, dcur, False, f"rs_start_{tag}")
        reduce_finish(started[4])
        in_flight.append((started, segs, tag))
        return started[4]

    def reduce_finish(after):
        if in_flight:
            started, segs, tag = in_flight.pop()
            mine, landed = _exchange_wait(started, after, False, f"rs_wait_{tag}")
            grads.update(_unpack_shard_grads(_sum_exchange(mine, landed, me, f"rs_sum_{tag}"), segs))

    def ffn_backward(f, i, x_in, a, u, gain, dout, group_rows):
        dx, h, dob, y, da, du, dg = _ffn_bwd(x_in, dout, a, u, gain, *ffn_weights(f, i), f"{f}_bwd_{i}")
        packed = _tn_matmul_packed(da, h, None, 0, group_rows, f"{f}_dgate_{i}")
        packed = _tn_matmul_packed(du, h, packed, 1, group_rows, f"{f}_dup_{i}")
        packed = _tn_matmul_packed(y, dob, packed, 2, group_rows, f"{f}_ddown_{i}")
        small[f + "_norm"][i] = jnp.sum(dg, axis=0)
        return dx, packed

    token = jnp.zeros((8, 128), F32)
    for i in reversed(range(DEPTH)):
        j = i // 2
        st = saved[i]
        mixer_rows = sum(rows for _, _, rows in _mixer_segments(i))
        dcur, packed = ffn_backward("ffn2", i, st["x2"], st["a2"], st["u2"], tied(ffn2_norm[i:i + 1], token), dcur,
                                    3 * FFN_ROWS + mixer_rows)
        if i % 2 == 0:
            dcur, dpw, dsc, dg = _pool_bwd(st["x1"], dcur, mix_norm[i:i + 1], w[("pool_w", j)], pool_scale[j:j + 1],
                                           f"pool_bwd_{i}")
            gr[("pool_w", j)] = dpw
            small["pool_scale"][j] = jnp.sum(dsc, axis=0)
            small["mix_norm"][i] = jnp.sum(dg, axis=0)
        else:
            dob, dpo, delta = _mla_out_bwd(dcur, st["o"], w[("mla_w_out", j)], f"mla_out_bwd_{i}")
            gr[("mla_w_out", j)] = _tn_matmul(st["o"], dob, f"mla_dout_{i}")
            dk, dv, dq = _flash_bwd(st["q"], st["k"], st["v"], dpo, st["lse"], delta, st["p"], st["m"],
                                    f"flash_bwd_{i}")
            (dcur, h, dlat, cqn, dqp, ckvn, dkv, dg, dgq, dgkv, dghq, dghk) = _mla_pre_bwd(
                st["x1"], dcur, dq, dk, dv, mix_norm[i:i + 1], *mla_weights(j), f"mla_pre_bwd_{i}")
            gr[("mla_w_in", j)] = _tn_matmul(dlat, h, f"mla_din_{i}")
            gr[("mla_w_q_up", j)] = _tn_matmul(dqp, cqn, f"mla_dqup_{i}")
            gr[("mla_w_kv_up", j)] = _tn_matmul(dkv, ckvn, f"mla_dkvup_{i}")
            small["mix_norm"][i] = jnp.sum(dg, axis=0)
            small["mla_q_norm"][j] = _pad_lanes(jnp.sum(dgq, axis=0)[None], D_MODEL)[0]
            small["mla_kv_norm"][j] = _pad_lanes(jnp.sum(dgkv, axis=0)[None], D_MODEL)[0]
            small["mla_q_head_norm"][j] = _pad_lanes(jnp.sum(dghq, axis=0)[None], D_MODEL)[0]
            small["mla_k_head_norm"][j] = _pad_lanes(jnp.sum(dghk, axis=0)[None], D_MODEL)[0]
        packed = lax.dynamic_update_slice(packed, _pack_grads(gr, _mixer_segments(i)), (0, 3 * FFN_ROWS, 0))
        token = reduce_start(packed, _ffn_segments("ffn2", i) + _mixer_segments(i), f"a{i}")
        if i > 0:
            dcur, packed = ffn_backward("ffn1", i, st["x0"], st["a1"], st["u1"], tied(ffn1_norm[i:i + 1], token),
                                        dcur, 3 * FFN_ROWS)
            token = reduce_start(packed, _ffn_segments("ffn1", i), f"b{i}")
    st = saved[0]
    dcur, h, dob, y, da, du, dg = _ffn_bwd(st["x0"], dcur, st["a1"], st["u1"], tied(ffn1_norm[0:1], token),
                                           *ffn_weights("ffn1", 0), "ffn1_bwd_0")
    small["ffn1_norm"][0] = jnp.sum(dg, axis=0)
    grad_x = dcur[None]
    small_order = ("ffn1_norm", "mix_norm", "ffn2_norm", "pool_scale", "mla_q_norm", "mla_kv_norm",
                   "mla_q_head_norm", "mla_k_head_norm")
    rows = [r for k in small_order for r in small[k]]
    rows.append(jnp.zeros((D_MODEL,), F32).at[0].set(loss_part))
    rows.append(jnp.zeros((D_MODEL,), F32))
    small_sum = _sum_parts(_all_gather(jnp.stack(rows), "small_all_gather"), "small_sum")
    loss = small_sum[SM_ROWS - 2, 0]
    token = small_sum
    for seg, lhs, rhs in zip(_ffn_segments("ffn1", 0), (da, du, y), (h, h, dob)):
        packed = _tn_matmul_packed(lhs, rhs, None, 0, FFN_ROWS, f"ffn1_d{seg[0][7:]}_0", behind=token)
        token = reduce_start(packed, [seg], f"b0_{seg[0][7:]}")
    last = "ffn1_w_down"

    off = 0
    for k in small_order:
        cnt = len(small[k])
        g = small_sum[off:off + cnt]
        off += cnt
        if k == "mla_q_norm":
            g = lax.dynamic_slice_in_dim(g[:, :Q_LORA], dev * (Q_LORA // N_DEV), Q_LORA // N_DEV, axis=1)
        elif k == "mla_kv_norm":
            g = lax.dynamic_slice_in_dim(g[:, :KV_LORA], dev * (KV_LORA // N_DEV), KV_LORA // N_DEV, axis=1)
        elif k in ("mla_q_head_norm", "mla_k_head_norm"):
            g = g[:, :QK_HEAD]
        grads[k] = g

    deltas, new_m, new_v = {}, {}, {}

    def update(k, behind):
        to_view = _t if k in COLUMN_SHARDED else (lambda a: a)
        shape = to_view(p[k]).shape
        operands = [to_view(a) for a in (p[k], moments_m[k], moments_v[k])]
        if k in grads:
            view = (-1, shape[-1])
            w2, m2, v2 = (a.reshape(view) for a in operands)
            results = (*_adamw(w2, grads[k].reshape(view), m2, v2, behind, "adamw_" + k), grads[k])
        else:
            view = (shape[0], -1, shape[-1])
            pieces = [grads[(k, idx)].reshape(view[1:]) for idx in range(shape[0])]
            w3, m3, v3 = (a.reshape(view) for a in operands)
            results = _adamw_layers(w3, pieces, m3, v3, behind, "adamw_" + k)
        deltas[k], new_m[k], new_v[k], grads[k] = (to_view(a.reshape(shape)) for a in results)
        return results[0]

    done = token
    for k in WEIGHTS:
        if k != last:
            done = update(k, done)
    reduce_finish(done)
    update(last, done)

    return (loss, grad_x, *[grads[k] for k in WEIGHTS], *[deltas[k] for k in WEIGHTS],
            *[new_m[k] for k in WEIGHTS], *[new_v[k] for k in WEIGHTS])
```

```python
import functools

import jax
import jax.numpy as jnp
from jax import lax
from jax.experimental import pallas as pl
from jax.experimental.pallas import tpu as pltpu

F32 = jnp.float32
BF16 = jnp.bfloat16

D_MODEL = 1024
DEPTH = 4
D_FF = 2816
POOL_WINDOWS = (2, 4, 8, 16)
POOL_GROUP = 256
POOL_HALO = 16
N_HEADS = 8
QK_NOPE = 128
QK_ROPE = 64
QK_HEAD = 192
V_HEAD = 128
Q_LORA = 768
KV_LORA = 256
ROPE_THETA = 10000.0
EPS = 1e-6
FFN_HALF = 0.5
ADAM_LR = 0.001
ADAM_B1 = 0.9
ADAM_B2 = 0.999
ADAM_EPS = 1e-08
ADAM_WD = 0.01
ADAM_STEP = 10

N_DEV = 8
HEAD_PAD = 256
LAT_PAD = 1152
VT_ROWS = 144
LOG2_E = 1.4426950408889634
ATTN_SCALE = QK_HEAD ** -0.5
LOGIT_SCALE = ATTN_SCALE * LOG2_E
V7X_VMEM_LIMIT = 56 * 1024 * 1024
FF_CHUNK = 256
SM_ROWS = 24

NAMES = ['x', 'positions', 'ffn1_norm', 'ffn1_w_gate', 'ffn1_w_up', 'ffn1_w_down', 'mix_norm', 'pool_w',
         'pool_scale', 'mla_w_in', 'mla_q_norm', 'mla_w_q_up', 'mla_kv_norm', 'mla_w_kv_up', 'mla_q_head_norm',
         'mla_k_head_norm', 'mla_w_out', 'ffn2_norm', 'ffn2_w_gate', 'ffn2_w_up', 'ffn2_w_down']
WEIGHTS = NAMES[2:]


def _tiles(n):
    return dict(ffn_fwd=min(512, n), ffn_bwd=min(256, n), fwd_k=min(512, n // 2), bwd_q=min(512, n // 2),
                mla_bwd=min(256, n), pool=min(512, n), tn=min(2048, n), rows=min(1024, n))


def _params(*sem):
    return pltpu.CompilerParams(dimension_semantics=sem, vmem_limit_bytes=V7X_VMEM_LIMIT)


def _dot(a, b):
    return jnp.dot(a, b, preferred_element_type=F32)


def _dot_nt(a, b):
    return lax.dot_general(a, b, (((1,), (1,)), ((), ())), preferred_element_type=F32)


def _dot_tn(a, b):
    return lax.dot_general(a, b, (((0,), (0,)), ((), ())), preferred_element_type=F32)


def _rowsum8(v):
    rows, w = v.shape
    return jnp.sum(v.reshape(rows // 8, 8, w), axis=0)


def _sigmoid(a):
    return 1.0 / (1.0 + jnp.exp(-a))


def _lane_sum(v):
    f = v[:, :128]
    for t in range(1, v.shape[1] // 128):
        f = f + v[:, t * 128:(t + 1) * 128]
    hi = f.astype(BF16)
    lo = (f - hi.astype(F32)).astype(BF16)
    return _dot(jnp.concatenate([hi, lo], axis=1), jnp.ones((256, 128), BF16))


def _by_row(v, r):
    return jnp.concatenate([v[:, t * 128:(t + 1) * 128] * r for t in range(v.shape[1] // 128)], axis=1)


def _rms_fwd(x, width=None):
    width = x.shape[-1] if width is None else width
    r = lax.rsqrt(_lane_sum(x * x) * (1.0 / width) + EPS)
    return _by_row(x, r), r


def _rms_bwd(dy, xhat, r, gain, width=None):
    width = xhat.shape[-1] if width is None else width
    t = dy * gain
    return _by_row(t - _by_row(xhat, _lane_sum(t * xhat) * (1.0 / width)), r)


def _full(shape):
    return pl.BlockSpec(shape, lambda *_: (0,) * len(shape))


def _load_weights(gathered, offsets, dsts, sems):
    rows = D_FF // N_DEV
    copies = [pltpu.make_async_copy(gathered.at[j, pl.ds(off, rows), :], dst.at[pl.ds(j * rows, rows), :],
                                    sems.at[N_DEV * k + j])
              for k, (off, dst) in enumerate(zip(offsets, dsts)) for j in range(N_DEV)]
    for cp in copies:
        cp.start()
    for cp in copies:
        cp.wait()


def _ffn_fwd(x, gain, gathered, offsets, name):
    n = x.shape[0]
    tm = _tiles(n)["ffn_fwd"]

    def body(x_ref, g_ref, gathered_ref, out_ref, a_ref, u_ref, wg_v, wu_v, wd_v, sems):
        @pl.when(pl.program_id(0) == 0)
        def _():
            _load_weights(gathered_ref, offsets, (wg_v, wu_v, wd_v), sems)

        xt = x_ref[...]
        xhat, _ = _rms_fwd(xt)
        h = (xhat * g_ref[...]).astype(BF16)
        acc = jnp.zeros((tm, D_MODEL), F32)
        for c in range(D_FF // FF_CHUNK):
            sl = pl.ds(c * FF_CHUNK, FF_CHUNK)
            a = _dot_nt(h, wg_v[sl, :])
            u = _dot_nt(h, wu_v[sl, :])
            a_ref[:, sl] = a.astype(BF16)
            u_ref[:, sl] = u.astype(BF16)
            y = (a * _sigmoid(a) * u).astype(BF16)
            acc = acc + _dot(y, wd_v[sl, :])
        out_ref[...] = xt + FFN_HALF * acc

    any_spec = pl.BlockSpec(memory_space=pl.ANY)
    return pl.pallas_call(
        body, name=name, grid=(n // tm,),
        in_specs=[pl.BlockSpec((tm, D_MODEL), lambda i: (i, 0)), _full((1, D_MODEL)), any_spec],
        out_specs=[pl.BlockSpec((tm, D_MODEL), lambda i: (i, 0)), pl.BlockSpec((tm, D_FF), lambda i: (i, 0)),
                   pl.BlockSpec((tm, D_FF), lambda i: (i, 0))],
        out_shape=[jax.ShapeDtypeStruct((n, D_MODEL), F32), jax.ShapeDtypeStruct((n, D_FF), BF16),
                   jax.ShapeDtypeStruct((n, D_FF), BF16)],
        scratch_shapes=[pltpu.VMEM((D_FF, D_MODEL), BF16)] * 3 + [pltpu.SemaphoreType.DMA((3 * N_DEV,))],
        compiler_params=_params("arbitrary"),
    )(x, gain, gathered)


def _ffn_bwd(x, dout, a, u, gain, gathered, offsets, name):
    n = x.shape[0]
    tm = _tiles(n)["ffn_bwd"]

    def body(x_ref, do_ref, a_ref, u_ref, g_ref, gathered_ref,
             dx_ref, h_ref, dob_ref, y_ref, da_ref, du_ref, dg_ref, wg_v, wu_v, wd_v, sems):
        @pl.when(pl.program_id(0) == 0)
        def _():
            _load_weights(gathered_ref, offsets, (wg_v, wu_v, wd_v), sems)
            dg_ref[...] = jnp.zeros_like(dg_ref)

        xt = x_ref[...]
        g = g_ref[...]
        xhat, r = _rms_fwd(xt)
        h_ref[...] = (xhat * g).astype(BF16)
        dout = do_ref[...]
        dob = (FFN_HALF * dout).astype(BF16)
        dob_ref[...] = dob
        for c in range(D_FF // FF_CHUNK):
            sl = pl.ds(c * FF_CHUNK, FF_CHUNK)
            dy = _dot_nt(dob, wd_v[sl, :])
            av = a_ref[:, sl].astype(F32)
            uv = u_ref[:, sl].astype(F32)
            s = _sigmoid(av)
            silu = av * s
            y_ref[:, sl] = (silu * uv).astype(BF16)
            du_ref[:, sl] = (dy * silu).astype(BF16)
            da_ref[:, sl] = (dy * uv * (s * (1.0 + av * (1.0 - s)))).astype(BF16)
        dh = _dot(da_ref[...], wg_v[...]) + _dot(du_ref[...], wu_v[...])
        dg_ref[...] += _rowsum8(dh * xhat)
        dx_ref[...] = dout + _rms_bwd(dh, xhat, r, g)

    any_spec = pl.BlockSpec(memory_space=pl.ANY)
    row_d = pl.BlockSpec((tm, D_MODEL), lambda i: (i, 0))
    row_f = pl.BlockSpec((tm, D_FF), lambda i: (i, 0))
    return pl.pallas_call(
        body, name=name, grid=(n // tm,),
        in_specs=[row_d, row_d, row_f, row_f, _full((1, D_MODEL)), any_spec],
        out_specs=[row_d, row_d, row_d, row_f, row_f, row_f, _full((8, D_MODEL))],
        out_shape=[jax.ShapeDtypeStruct((n, D_MODEL), F32), jax.ShapeDtypeStruct((n, D_MODEL), BF16),
                   jax.ShapeDtypeStruct((n, D_MODEL), BF16), jax.ShapeDtypeStruct((n, D_FF), BF16),
                   jax.ShapeDtypeStruct((n, D_FF), BF16), jax.ShapeDtypeStruct((n, D_FF), BF16),
                   jax.ShapeDtypeStruct((8, D_MODEL), F32)],
        scratch_shapes=[pltpu.VMEM((D_FF, D_MODEL), BF16)] * 3 + [pltpu.SemaphoreType.DMA((3 * N_DEV,))],
        compiler_params=_params("arbitrary"),
    )(x, dout, a, u, gain, gathered)


def _tn_matmul(a, b, name, behind=None):
    n, fa = a.shape
    db = b.shape[1]
    tk = _tiles(n)["tn"]
    tf = fa // 2 if (fa // 2) % 128 == 0 and fa > 1024 else fa
    behind = jnp.zeros((8, 128), F32) if behind is None else behind

    def body(a_ref, b_ref, behind_ref, o_ref):
        @pl.when(pl.program_id(1) == 0)
        def _():
            o_ref[...] = jnp.zeros_like(o_ref)

        o_ref[...] += _dot_tn(a_ref[...], b_ref[...])

    return pl.pallas_call(
        body, name=name, grid=(fa // tf, n // tk),
        in_specs=[pl.BlockSpec((tk, tf), lambda i, k: (k, i)), pl.BlockSpec((tk, db), lambda i, k: (k, 0)),
                  pl.BlockSpec(memory_space=pl.ANY)],
        out_specs=pl.BlockSpec((tf, db), lambda i, k: (i, 0)),
        out_shape=jax.ShapeDtypeStruct((fa, db), F32),
        compiler_params=_params("arbitrary", "arbitrary"),
    )(a, b, behind)


def _tn_matmul_packed(a, b, packed, slot, group_rows, name, behind=None):
    n, fa = a.shape
    db = b.shape[1]
    tk = _tiles(n)["tn"]
    per = fa // N_DEV
    slabs = N_DEV // 2
    behind = jnp.zeros((8, 128), F32) if behind is None else behind

    def body(a_ref, b_ref, behind_ref, *rest):
        o_ref = rest[-1]

        @pl.when(pl.program_id(1) == 0)
        def _():
            o_ref[...] = jnp.zeros_like(o_ref)

        o_ref[...] += _dot_tn(a_ref[...], b_ref[...]).reshape(slabs, per, db)

    any_spec = pl.BlockSpec(memory_space=pl.ANY)
    return pl.pallas_call(
        body, name=name, grid=(2, n // tk),
        in_specs=[pl.BlockSpec((tk, slabs * per), lambda i, k: (k, i)), pl.BlockSpec((tk, db), lambda i, k: (k, 0)),
                  any_spec] + ([] if packed is None else [any_spec]),
        out_specs=pl.BlockSpec((slabs, per, db), lambda i, k: (i, slot, 0)),
        out_shape=jax.ShapeDtypeStruct((N_DEV, group_rows, db), F32),
        input_output_aliases={} if packed is None else {3: 0},
        compiler_params=_params("arbitrary", "arbitrary"),
    )(a, b, behind, *([] if packed is None else [packed]))


def _loss_head(y, target, name):
    n = y.shape[0]
    tm = _tiles(n)["rows"]

    def body(y_ref, t_ref, d_ref, acc_ref):
        @pl.when(pl.program_id(0) == 0)
        def _():
            acc_ref[...] = jnp.zeros_like(acc_ref)

        d = y_ref[...] - t_ref[...]
        d_ref[...] = d * (1.0 / D_MODEL)
        acc_ref[...] += _rowsum8(d * d)

    row = pl.BlockSpec((tm, D_MODEL), lambda i: (i, 0))
    return pl.pallas_call(
        body, name=name, grid=(n // tm,), in_specs=[row, row], out_specs=[row, _full((8, D_MODEL))],
        out_shape=[jax.ShapeDtypeStruct((n, D_MODEL), F32), jax.ShapeDtypeStruct((8, D_MODEL), F32)],
        compiler_params=_params("arbitrary"),
    )(y, target)


def _window_sum(v, w, rows, forward):
    s = v
    sh = 1
    while sh < w:
        s = s + pltpu.roll(s, (rows - sh) if forward else sh, 0)
        sh *= 2
    return s


def _pool_fwd(x, gain, w, scale, name):
    n = x.shape[0]
    tm = _tiles(n)["pool"]
    hb = tm // POOL_HALO
    rows = tm + POOL_HALO

    def body(x_ref, xh_ref, g_ref, w_ref, sc_ref, out_ref):
        i = pl.program_id(0)
        xt = x_ref[...]
        e = jnp.concatenate([xh_ref[...], xt], axis=0)
        xhat, _ = _rms_fwd(e)
        row = lax.broadcasted_iota(jnp.int32, (rows, 1), 0)
        hn = jnp.where((row >= POOL_HALO) | (i > 0), xhat * g_ref[...], 0.0)
        t_glob = i * tm + row - POOL_HALO
        outs = []
        for gi, win in enumerate(POOL_WINDOWS):
            ug = hn[:, gi * POOL_GROUP:(gi + 1) * POOL_GROUP]
            cnt = jnp.maximum(jnp.minimum(t_glob + 1, win), 1).astype(F32)
            pooled = (_window_sum(ug, win, rows, False) / cnt - ug)[POOL_HALO:]
            outs.append(_dot(pooled.astype(BF16), w_ref[gi]))
        out_ref[...] = xt + jnp.concatenate(outs, axis=1) * sc_ref[...]

    return pl.pallas_call(
        body, name=name, grid=(n // tm,),
        in_specs=[pl.BlockSpec((tm, D_MODEL), lambda i: (i, 0)),
                  pl.BlockSpec((POOL_HALO, D_MODEL), lambda i: (jnp.maximum(i * hb - 1, 0), 0)),
                  _full((1, D_MODEL)), _full((4, POOL_GROUP, POOL_GROUP)), _full((1, D_MODEL))],
        out_specs=pl.BlockSpec((tm, D_MODEL), lambda i: (i, 0)),
        out_shape=jax.ShapeDtypeStruct((n, D_MODEL), F32),
        compiler_params=_params("arbitrary"),
    )(x, x, gain, w, scale)


def _pool_bwd(x, dout, gain, w, scale, name):
    n = x.shape[0]
    tm = _tiles(n)["pool"]
    hb = tm // POOL_HALO
    rows = tm + POOL_HALO
    nt = n // tm

    def body(x_ref, xh_ref, do_ref, doh_ref, g_ref, w_ref, sc_ref, dx_ref, dw_ref, dsc_ref, dg_ref):
        i = pl.program_id(0)

        @pl.when(i == 0)
        def _():
            dw_ref[...] = jnp.zeros_like(dw_ref)
            dsc_ref[...] = jnp.zeros_like(dsc_ref)
            dg_ref[...] = jnp.zeros_like(dg_ref)

        xt = x_ref[...]
        g = g_ref[...]
        e = jnp.concatenate([xh_ref[...], xt], axis=0)
        xhat_e, r_e = _rms_fwd(e)
        row = lax.broadcasted_iota(jnp.int32, (rows, 1), 0)
        hn = jnp.where((row >= POOL_HALO) | (i > 0), xhat_e * g, 0.0)
        t_prev = i * tm + row - POOL_HALO
        t_next = i * tm + row
        dout = do_ref[...]
        dt = jnp.concatenate([dout, doh_ref[...]], axis=0)
        dt = jnp.where((row < tm) | (i < nt - 1), dt, 0.0)
        dyr = dt * sc_ref[...]
        dus, dscs = [], []
        for gi, win in enumerate(POOL_WINDOWS):
            lanes = slice(gi * POOL_GROUP, (gi + 1) * POOL_GROUP)
            ug = hn[:, lanes]
            cnt = jnp.maximum(jnp.minimum(t_prev + 1, win), 1).astype(F32)
            pooled = (_window_sum(ug, win, rows, False) / cnt - ug)[POOL_HALO:].astype(BF16)
            yraw = _dot(pooled, w_ref[gi])
            dscs.append(_rowsum8(dout[:, lanes] * yraw))
            dyr_b = dyr[:, lanes].astype(BF16)
            dw_ref[gi] += _dot_tn(pooled, dyr_b[:tm])
            dpool = _dot_nt(dyr_b, w_ref[gi])
            cnt2 = jnp.minimum(t_next + 1, win).astype(F32)
            dus.append((_window_sum(dpool / cnt2, win, rows, True) - dpool)[:tm])
        dsc_ref[...] += jnp.concatenate(dscs, axis=1)
        dh = jnp.concatenate(dus, axis=1)
        xhat = xhat_e[POOL_HALO:]
        dg_ref[...] += _rowsum8(dh * xhat)
        dx_ref[...] = dout + _rms_bwd(dh, xhat, r_e[POOL_HALO:], g)

    row_d = pl.BlockSpec((tm, D_MODEL), lambda i: (i, 0))
    prev_h = pl.BlockSpec((POOL_HALO, D_MODEL), lambda i: (jnp.maximum(i * hb - 1, 0), 0))
    next_h = pl.BlockSpec((POOL_HALO, D_MODEL), lambda i: (jnp.minimum((i + 1) * hb, n // POOL_HALO - 1), 0))
    return pl.pallas_call(
        body, name=name, grid=(nt,),
        in_specs=[row_d, prev_h, row_d, next_h, _full((1, D_MODEL)), _full((4, POOL_GROUP, POOL_GROUP)),
                  _full((1, D_MODEL))],
        out_specs=[row_d, _full((4, POOL_GROUP, POOL_GROUP)), _full((8, D_MODEL)), _full((8, D_MODEL))],
        out_shape=[jax.ShapeDtypeStruct((n, D_MODEL), F32), jax.ShapeDtypeStruct((4, POOL_GROUP, POOL_GROUP), F32),
                   jax.ShapeDtypeStruct((8, D_MODEL), F32), jax.ShapeDtypeStruct((8, D_MODEL), F32)],
        compiler_params=_params("arbitrary"),
    )(x, x, dout, dout, gain, w, scale)


def _rope(v, cos, sin_signed):
    lo, hi = v[:, :128], v[:, 128:]
    lane = lax.broadcasted_iota(jnp.int32, hi.shape, 1)
    swapped = jnp.where(lane < 32, pltpu.roll(hi, 96, 1), pltpu.roll(hi, 32, 1))
    return jnp.concatenate([lo, hi * cos + swapped * sin_signed], axis=1)


def _rope_bwd(gr, cos, sin_signed):
    lo, hi = gr[:, :128], gr[:, 128:]
    t = hi * sin_signed
    lane = lax.broadcasted_iota(jnp.int32, hi.shape, 1)
    swapped = jnp.where(lane < 32, pltpu.roll(t, 96, 1), pltpu.roll(t, 32, 1))
    return jnp.concatenate([lo, hi * cos + swapped], axis=1)


def _mla_latents(h, win_ref):
    cq = _dot_nt(h, win_ref[0:Q_LORA, :])
    ckv = _dot_nt(h, win_ref[Q_LORA:Q_LORA + KV_LORA, :])
    kpe = _dot_nt(h, win_ref[Q_LORA + KV_LORA:LAT_PAD, :])
    return cq, ckv, kpe


def _mla_pre_fwd(x, gain, win, gq, gkv, wq, wkv, ghq, ghk, cos, sin_signed, name):
    n = x.shape[0]
    tm = _tiles(n)["fwd_k"]

    def body(x_ref, g_ref, win_ref, gq_ref, gkv_ref, wq_ref, wkv_ref, ghq_ref, ghk_ref, c_ref, s_ref,
             q_ref, k_ref, v_ref, vt_ref):
        xhat, _ = _rms_fwd(x_ref[...])
        h = (xhat * g_ref[...]).astype(BF16)
        cq, ckv, kpe = _mla_latents(h, win_ref)
        cqn = (_rms_fwd(cq)[0] * gq_ref[...]).astype(BF16)
        ckvn = (_rms_fwd(ckv)[0] * gkv_ref[...]).astype(BF16)
        cos, sn = c_ref[...], s_ref[...]
        for hd in range(N_HEADS):
            rws = pl.ds(hd * HEAD_PAD, HEAD_PAD)
            qh = _dot_nt(cqn, wq_ref[rws, :])
            qn = _rms_fwd(qh, QK_HEAD)[0] * ghq_ref[...]
            q_ref[hd] = (_rope(qn, cos, sn) * LOGIT_SCALE).astype(BF16)
            kvh = _dot_nt(ckvn, wkv_ref[rws, :])
            kpre = jnp.concatenate([kvh[:, :QK_NOPE], kpe], axis=1)
            kn = _rms_fwd(kpre, QK_HEAD)[0] * ghk_ref[...]
            k_ref[hd] = _rope(kn, cos, sn).astype(BF16)
            vh = kvh[:, QK_NOPE:]
            v_ref[hd] = vh.astype(BF16)
            vt_ref[hd, 0] = jnp.concatenate([vh.T, jnp.ones((VT_ROWS - V_HEAD, tm), F32)], axis=0).astype(BF16)

    row = lambda w: pl.BlockSpec((tm, w), lambda i: (i, 0))
    head = lambda w: pl.BlockSpec((N_HEADS, tm, w), lambda i: (0, i, 0))
    return pl.pallas_call(
        body, name=name, grid=(n // tm,),
        in_specs=[row(D_MODEL), _full((1, D_MODEL)), _full((LAT_PAD, D_MODEL)), _full((1, Q_LORA)),
                  _full((1, KV_LORA)), _full((N_HEADS * HEAD_PAD, Q_LORA)), _full((N_HEADS * HEAD_PAD, KV_LORA)),
                  _full((1, HEAD_PAD)), _full((1, HEAD_PAD)), row(128), row(128)],
        out_specs=[head(HEAD_PAD), head(HEAD_PAD), head(V_HEAD),
                   pl.BlockSpec((N_HEADS, 1, VT_ROWS, tm), lambda i: (0, i, 0, 0))],
        out_shape=[jax.ShapeDtypeStruct((N_HEADS, n, HEAD_PAD), BF16), jax.ShapeDtypeStruct((N_HEADS, n, HEAD_PAD), BF16),
                   jax.ShapeDtypeStruct((N_HEADS, n, V_HEAD), BF16),
                   jax.ShapeDtypeStruct((N_HEADS, n // tm, VT_ROWS, tm), BF16)],
        compiler_params=_params("arbitrary"),
    )(x, gain, win, gq, gkv, wq, wkv, ghq, ghk, cos, sin_signed)


def _flash_fwd(q, k, vt, name):
    n = q.shape[1]
    tk = _tiles(n)["fwd_k"]
    tq = 2 * tk
    nq = n // tq

    def body(q_ref, k_ref, vt_ref, o_ref, lse_ref, m_ref, p_hbm, s_scr, m_scr, acc_scr, p_scr, p_sems):
        h = pl.program_id(0)
        i = pl.program_id(1)
        qi = q_ref[0]

        def scores(j, slot):
            s_scr[slot] = _dot_nt(k_ref[0, pl.ds(pl.multiple_of(j * tk, tk), tk), :], qi)

        def p_copy(block, pslot):
            return pltpu.make_async_copy(p_scr.at[pslot], p_hbm.at[h, block], p_sems.at[pslot])

        def update(j, slot, pslot, diagonal=None):
            s = s_scr[slot]
            if diagonal is not None:
                krow = lax.broadcasted_iota(jnp.int32, (tk, tq), 0) + diagonal * tk
                qcol = lax.broadcasted_iota(jnp.int32, (tk, tq), 1)
                s = jnp.where(krow <= qcol, s, -jnp.inf)
            m = m_scr[...]
            m_new = jnp.maximum(m, jnp.max(s, axis=0, keepdims=True))
            p = jnp.exp2(s - m_new).astype(BF16)
            p_scr[pslot] = p
            acc_scr[...] = jnp.exp2(m - m_new) * acc_scr[...] + _dot(vt_ref[0, j], p)
            m_scr[...] = m_new
            m_ref[0, 0, j] = m_new

        def kv_pair(jj, diagonal):
            first = 2 * (jj % 2)
            p_copy(0, first).wait()
            p_copy(0, first + 1).wait()
            scores(2 * jj + 1, 1)
            update(2 * jj, 0, first, 0 if diagonal else None)
            if not diagonal:
                scores(2 * jj + 2, 0)
            update(2 * jj + 1, 1, first + 1, 1 if diagonal else None)
            p_copy(i * (i + 1) + 2 * jj, first).start()
            p_copy(i * (i + 1) + 2 * jj + 1, first + 1).start()

        m_scr[...] = jnp.full((1, tq), -jnp.inf, F32)
        acc_scr[...] = jnp.zeros((VT_ROWS, tq), F32)
        @pl.when((h == 0) & (i == 0))
        def _():
            p_scr[...] = jnp.zeros_like(p_scr)
            for pslot in range(4):
                p_copy(nq * (nq + 1) + pslot, pslot).start()

        scores(0, 0)

        def pair(jj, carry):
            kv_pair(jj, False)
            return carry

        lax.fori_loop(0, i, pair, 0)
        kv_pair(i, True)
        l = acc_scr[V_HEAD:V_HEAD + 1, :]
        o_ref[...] = (acc_scr[0:V_HEAD, :] / l).T.astype(BF16)
        lse_ref[0, 0] = m_scr[...] + jnp.log2(l)

        @pl.when((h == N_HEADS - 1) & (i == nq - 1))
        def _():
            for pslot in range(4):
                p_copy(0, pslot).wait()

    return pl.pallas_call(
        body, name=name, grid=(N_HEADS, nq),
        in_specs=[pl.BlockSpec((1, tq, HEAD_PAD), lambda h, i: (h, i, 0)),
                  pl.BlockSpec((1, n, HEAD_PAD), lambda h, i: (h, 0, 0)),
                  pl.BlockSpec((1, n // tk, VT_ROWS, tk), lambda h, i: (h, 0, 0, 0))],
        out_specs=[pl.BlockSpec((tq, V_HEAD), lambda h, i: (i, h)),
                   pl.BlockSpec((1, 1, 1, tq), lambda h, i: (h, i, 0, 0)),
                   pl.BlockSpec((1, 1, 2 * nq, 1, tq), lambda h, i: (h, i, 0, 0, 0)),
                   pl.BlockSpec(memory_space=pl.ANY)],
        out_shape=[jax.ShapeDtypeStruct((n, N_HEADS * V_HEAD), BF16), jax.ShapeDtypeStruct((N_HEADS, nq, 1, tq), F32),
                   jax.ShapeDtypeStruct((N_HEADS, nq, 2 * nq, 1, tq), F32),
                   jax.ShapeDtypeStruct((N_HEADS, nq * (nq + 1) + 4, tk, tq), BF16)],
        scratch_shapes=[pltpu.VMEM((2, tk, tq), F32), pltpu.VMEM((1, tq), F32), pltpu.VMEM((VT_ROWS, tq), F32),
                        pltpu.VMEM((4, tk, tq), BF16), pltpu.SemaphoreType.DMA((4,))],
        compiler_params=_params("arbitrary", "arbitrary"),
    )(q, k, vt)


def _mla_post_fwd(o, x, wout, name):
    n = x.shape[0]
    tm = _tiles(n)["rows"]

    def body(o_ref, x_ref, w_ref, out_ref):
        out_ref[...] = x_ref[...] + _dot(o_ref[...], w_ref[...])

    row = pl.BlockSpec((tm, D_MODEL), lambda i: (i, 0))
    return pl.pallas_call(
        body, name=name, grid=(n // tm,), in_specs=[row, row, _full((D_MODEL, D_MODEL))], out_specs=row,
        out_shape=jax.ShapeDtypeStruct((n, D_MODEL), F32), compiler_params=_params("arbitrary"),
    )(o, x, wout)


def _mla_out_bwd(dout, o, wout, name):
    n = dout.shape[0]
    t = _tiles(n)["bwd_q"]
    nq = n // t

    def body(do_ref, o_ref, w_ref, dob_ref, dpo_ref, dl_ref):
        dob = do_ref[...].astype(BF16)
        dob_ref[...] = dob
        dpo = _dot_nt(dob, w_ref[...])
        dpo_ref[...] = dpo.astype(BF16)
        ov = o_ref[...].astype(F32)
        for hd in range(N_HEADS):
            lanes = slice(hd * V_HEAD, (hd + 1) * V_HEAD)
            prod = dpo[:, lanes] * ov[:, lanes]
            dl_ref[hd, 0] = jnp.sum(prod.T, axis=0, keepdims=True)

    row = pl.BlockSpec((t, D_MODEL), lambda i: (i, 0))
    return pl.pallas_call(
        body, name=name, grid=(nq,), in_specs=[row, row, _full((D_MODEL, D_MODEL))],
        out_specs=[row, row, pl.BlockSpec((N_HEADS, 1, 1, t), lambda i: (0, i, 0, 0))],
        out_shape=[jax.ShapeDtypeStruct((n, D_MODEL), BF16), jax.ShapeDtypeStruct((n, D_MODEL), BF16),
                   jax.ShapeDtypeStruct((N_HEADS, nq, 1, t), F32)],
        compiler_params=_params("arbitrary"),
    )(dout, o, wout)


def _flash_bwd(q, k, v, dpo, lse, delta, p, m, name):
    n = q.shape[1]
    tq = _tiles(n)["bwd_q"]
    tk = 2 * tq
    nk = n // tk
    nqb = n // tq
    assert p.shape[2:] == (tq, tk) and m.shape[1:] == (nk, 2 * nk, 1, tk), "forward blocks are (tq keys, 2 tq queries)"

    def body(k_ref, v_ref, q_ref, do_ref, lse_ref, dl_ref, m_ref, p_hbm, dk_ref, dv_ref, dq_hbm,
             dq_acc, p_scr, dp_scr, sem, p_sems):
        h = pl.program_id(0)
        j = pl.program_id(1)

        @pl.when(j == 0)
        def _():
            dq_acc[...] = jnp.zeros_like(dq_acc)

        dk_ref[...] = jnp.zeros_like(dk_ref)
        dv_ref[...] = jnp.zeros_like(dv_ref)
        kj = k_ref[0]
        vj = v_ref[0]
        npairs = nk - 1 - j

        def block(t):
            return jnp.where(t < 2 * npairs, 2 * j + 2 + t, 2 * j + (t - 2 * npairs))

        def p_copy(i, first, half):
            tile = i // 2
            return pltpu.make_async_copy(p_hbm.at[h, pl.ds(tile * (tile + 1) + 2 * j, 2), :, pl.ds(half * tq, tq)],
                                         p_scr.at[first + half], p_sems.at[first + half])

        def fetch_pair(t):
            for half in range(2):
                p_copy(block(2 * t + half), 2 * (t % 2), half).start()

        def dp_ahead(i, half):
            dp_scr[half] = _dot_nt(vj, do_ref[pl.ds(pl.multiple_of(i * tq, tq), tq), :])

        def update(i, first, half):
            rws = pl.ds(pl.multiple_of(i * tq, tq), tq)
            lse_i = lse_ref[0, i]
            lanes = slice(half * tq, (half + 1) * tq)
            pv = jnp.concatenate(
                [p_scr[first + half, b].astype(F32) * jnp.exp2(m_ref[0, i // 2, 2 * j + b][:, lanes] - lse_i)
                 for b in range(2)], axis=0)
            dv_ref[0] += _dot(pv.astype(BF16), do_ref[rws, :])
            ds = (pv * (dp_scr[half] - dl_ref[0, i])).astype(BF16)
            dk_ref[0] += _dot(ds, q_ref[0, rws, :])
            dq_acc[rws, :] += _dot_tn(ds, kj)

        def q_pair(t, last):
            first = 2 * (t % 2)
            for half in range(2):
                p_copy(0, first, half).wait()
            if not last:
                fetch_pair(t + 1)
            dp_ahead(block(2 * t + 1), 1)
            update(block(2 * t), first, 0)
            if not last:
                dp_ahead(block(2 * t + 2), 0)
            update(block(2 * t + 1), first, 1)

        fetch_pair(0)
        dp_ahead(block(0), 0)

        def pair(t, carry):
            q_pair(t, False)
            return carry

        lax.fori_loop(0, npairs, pair, 0)
        q_pair(npairs, True)
        dk_ref[...] = dk_ref[...] * (ATTN_SCALE / LOGIT_SCALE)

        @pl.when(j == nk - 1)
        def _():
            dq_acc[...] = dq_acc[...] * ATTN_SCALE
            cp = pltpu.make_async_copy(dq_acc, dq_hbm.at[h], sem)
            cp.start()
            cp.wait()

    resident = dict(pipeline_mode=pl.Buffered(1))
    return pl.pallas_call(
        body, name=name, grid=(N_HEADS, nk),
        in_specs=[pl.BlockSpec((1, tk, HEAD_PAD), lambda h, j: (h, j, 0)),
                  pl.BlockSpec((1, tk, V_HEAD), lambda h, j: (h, j, 0)),
                  pl.BlockSpec((1, n, HEAD_PAD), lambda h, j: (h, 0, 0), **resident),
                  pl.BlockSpec((n, V_HEAD), lambda h, j: (0, h), **resident),
                  pl.BlockSpec((1, nqb, 1, tq), lambda h, j: (h, 0, 0, 0)),
                  pl.BlockSpec((1, nqb, 1, tq), lambda h, j: (h, 0, 0, 0)),
                  pl.BlockSpec((1, nk, 2 * nk, 1, tk), lambda h, j: (h, 0, 0, 0, 0), **resident),
                  pl.BlockSpec(memory_space=pl.ANY)],
        out_specs=[pl.BlockSpec((1, tk, HEAD_PAD), lambda h, j: (h, j, 0)),
                   pl.BlockSpec((1, tk, V_HEAD), lambda h, j: (h, j, 0)),
                   pl.BlockSpec(memory_space=pl.ANY)],
        out_shape=[jax.ShapeDtypeStruct((N_HEADS, n, HEAD_PAD), F32), jax.ShapeDtypeStruct((N_HEADS, n, V_HEAD), F32),
                   jax.ShapeDtypeStruct((N_HEADS, n, HEAD_PAD), F32)],
        scratch_shapes=[pltpu.VMEM((n, HEAD_PAD), F32), pltpu.VMEM((4, 2, tq, tq), BF16), pltpu.VMEM((2, tk, tq), F32),
                        pltpu.SemaphoreType.DMA(()), pltpu.SemaphoreType.DMA((4,))],
        compiler_params=_params("arbitrary", "arbitrary"),
    )(k, v, q, dpo, lse, delta, m, p)


def _mla_pre_bwd(x, dout, dq, dk, dv, gain, win, gq, gkv, wq, wkv, ghq, ghk, cos, sin_signed, name):
    n = x.shape[0]
    tm = _tiles(n)["mla_bwd"]
    hw = N_HEADS * HEAD_PAD

    def body(x_ref, do_ref, dq_ref, dk_ref, dv_ref, g_ref, win_ref, gq_ref, gkv_ref, wq_ref, wkv_ref, ghq_ref, ghk_ref,
             c_ref, s_ref, dx_ref, h_ref, dlat_ref, cqn_ref, dqp_ref, ckvn_ref, dkv_ref,
             dg_ref, dgq_ref, dgkv_ref, dghq_ref, dghk_ref):
        @pl.when(pl.program_id(0) == 0)
        def _():
            for ref in (dg_ref, dgq_ref, dgkv_ref, dghq_ref, dghk_ref):
                ref[...] = jnp.zeros_like(ref)

        g = g_ref[...]
        xhat, r = _rms_fwd(x_ref[...])
        h = (xhat * g).astype(BF16)
        h_ref[...] = h
        cq, ckv, kpe = _mla_latents(h, win_ref)
        cqhat, rcq = _rms_fwd(cq)
        ckvhat, rckv = _rms_fwd(ckv)
        cqn = (cqhat * gq_ref[...]).astype(BF16)
        ckvn = (ckvhat * gkv_ref[...]).astype(BF16)
        cqn_ref[...] = cqn
        ckvn_ref[...] = ckvn
        cos, sn = c_ref[...], s_ref[...]
        ghq, ghk = ghq_ref[...], ghk_ref[...]
        dkpe = jnp.zeros((tm, 128), F32)
        dghq = jnp.zeros((8, HEAD_PAD), F32)
        dghk = jnp.zeros((8, HEAD_PAD), F32)
        for hd in range(N_HEADS):
            rws = pl.ds(hd * HEAD_PAD, HEAD_PAD)
            lanes = slice(hd * HEAD_PAD, (hd + 1) * HEAD_PAD)
            qhat, rq = _rms_fwd(_dot_nt(cqn, wq_ref[rws, :]), QK_HEAD)
            gqn = _rope_bwd(dq_ref[hd], cos, sn)
            dghq = dghq + _rowsum8(gqn * qhat)
            dqpre = _rms_bwd(gqn, qhat, rq, ghq, QK_HEAD).astype(BF16)
            dqp_ref[:, lanes] = dqpre
            kvh = _dot_nt(ckvn, wkv_ref[rws, :])
            khat, rk = _rms_fwd(jnp.concatenate([kvh[:, :QK_NOPE], kpe], axis=1), QK_HEAD)
            gkn = _rope_bwd(dk_ref[hd], cos, sn)
            dghk = dghk + _rowsum8(gkn * khat)
            dkpre = _rms_bwd(gkn, khat, rk, ghk, QK_HEAD)
            dkpe = dkpe + dkpre[:, QK_NOPE:]
            dkvh = jnp.concatenate([dkpre[:, :QK_NOPE], dv_ref[hd]], axis=1).astype(BF16)
            dkv_ref[:, lanes] = dkvh
        dcqn = _dot(dqp_ref[...], wq_ref[...])
        dckvn = _dot(dkv_ref[...], wkv_ref[...])
        dghq_ref[...] += dghq
        dghk_ref[...] += dghk
        dgq_ref[...] += _rowsum8(dcqn * cqhat)
        dgkv_ref[...] += _rowsum8(dckvn * ckvhat)
        dlat = jnp.concatenate([_rms_bwd(dcqn, cqhat, rcq, gq_ref[...]), _rms_bwd(dckvn, ckvhat, rckv, gkv_ref[...]),
                                dkpe], axis=1).astype(BF16)
        dlat_ref[...] = dlat
        dh = _dot(dlat, win_ref[...])
        dg_ref[...] += _rowsum8(dh * xhat)
        dx_ref[...] = do_ref[...] + _rms_bwd(dh, xhat, r, g)

    row = lambda w: pl.BlockSpec((tm, w), lambda i: (i, 0))
    head = lambda w: pl.BlockSpec((N_HEADS, tm, w), lambda i: (0, i, 0))
    sds = jax.ShapeDtypeStruct
    return pl.pallas_call(
        body, name=name, grid=(n // tm,),
        in_specs=[row(D_MODEL), row(D_MODEL), head(HEAD_PAD), head(HEAD_PAD), head(V_HEAD), _full((1, D_MODEL)),
                  _full((LAT_PAD, D_MODEL)), _full((1, Q_LORA)), _full((1, KV_LORA)), _full((hw, Q_LORA)),
                  _full((hw, KV_LORA)), _full((1, HEAD_PAD)), _full((1, HEAD_PAD)), row(128), row(128)],
        out_specs=[row(D_MODEL), row(D_MODEL), row(LAT_PAD), row(Q_LORA), row(hw), row(KV_LORA), row(hw),
                   _full((8, D_MODEL)), _full((8, Q_LORA)), _full((8, KV_LORA)), _full((8, HEAD_PAD)),
                   _full((8, HEAD_PAD))],
        out_shape=[sds((n, D_MODEL), F32), sds((n, D_MODEL), BF16), sds((n, LAT_PAD), BF16), sds((n, Q_LORA), BF16),
                   sds((n, hw), BF16), sds((n, KV_LORA), BF16), sds((n, hw), BF16), sds((8, D_MODEL), F32),
                   sds((8, Q_LORA), F32), sds((8, KV_LORA), F32), sds((8, HEAD_PAD), F32), sds((8, HEAD_PAD), F32)],
        compiler_params=_params("arbitrary"),
    )(x, dout, dq, dk, dv, gain, win, gq, gkv, wq, wkv, ghq, ghk, cos, sin_signed)


def _adamw(w, g, m, v, behind, name):
    rows, cols = w.shape
    tr = rows
    for cand in (512, 256, 128, 64, 32, 16, 8):
        if rows % cand == 0 and rows > cand:
            tr = cand
            break

    def body(w_ref, g_ref, m_ref, v_ref, behind_ref, d_ref, mo_ref, vo_ref):
        d_ref[...], mo_ref[...], vo_ref[...] = _adamw_step(w_ref[...], g_ref[...], m_ref[...], v_ref[...])

    blk = pl.BlockSpec((tr, cols), lambda i: (i, 0))
    return pl.pallas_call(
        body, name=name, grid=(rows // tr,), in_specs=[blk] * 4 + [pl.BlockSpec(memory_space=pl.ANY)],
        out_specs=[blk] * 3, out_shape=[jax.ShapeDtypeStruct((rows, cols), F32)] * 3,
        compiler_params=_params("arbitrary"),
    )(w, g, m, v, behind)


def _adamw_step(w, g, m, v):
    mn = ADAM_B1 * m + (1.0 - ADAM_B1) * g
    vn = ADAM_B2 * v + (1.0 - ADAM_B2) * (g * g)
    m_hat = mn / (1.0 - ADAM_B1 ** ADAM_STEP)
    v_hat = vn / (1.0 - ADAM_B2 ** ADAM_STEP)
    return -ADAM_LR * (m_hat / (jnp.sqrt(v_hat) + ADAM_EPS) + ADAM_WD * w), mn, vn


def _adamw_layers(w, grads, m, v, behind, name):
    layers, r, c = w.shape
    tr = _row_tile(r, 512)

    def body(w_ref, m_ref, v_ref, *rest):
        g_refs, (d_ref, mo_ref, vo_ref, go_ref) = rest[:layers], rest[layers + 1:]
        layer = pl.program_id(1)
        g = g_refs[0][...]
        for k in range(1, layers):
            g = jnp.where(layer == k, g_refs[k][...], g)
        d_ref[0], mo_ref[0], vo_ref[0] = _adamw_step(w_ref[0], g, m_ref[0], v_ref[0])
        go_ref[0] = g

    stacked = pl.BlockSpec((1, tr, c), lambda i, layer: (layer, i, 0))
    piece = pl.BlockSpec((tr, c), lambda i, layer: (i, 0))
    return pl.pallas_call(
        body, name=name, grid=(r // tr, layers),
        in_specs=[stacked] * 3 + [piece] * layers + [pl.BlockSpec(memory_space=pl.ANY)],
        out_specs=[stacked] * 4, out_shape=[jax.ShapeDtypeStruct((layers, r, c), F32)] * 4,
        compiler_params=_params("arbitrary", "arbitrary"),
    )(w, m, v, *grads, behind)


def _sum_parts(parts, name):
    k, r, c = parts.shape
    tr = min(r, 512)

    def body(p_ref, o_ref):
        acc = p_ref[0]
        for j in range(1, k):
            acc = acc + p_ref[j]
        o_ref[...] = acc

    return pl.pallas_call(
        body, name=name, grid=(r // tr,), in_specs=[pl.BlockSpec((k, tr, c), lambda i: (0, i, 0))],
        out_specs=pl.BlockSpec((tr, c), lambda i: (i, 0)), out_shape=jax.ShapeDtypeStruct((r, c), parts.dtype),
        compiler_params=_params("arbitrary"),
    )(parts)


def _row_tile(r, most=256):
    best = r
    for cand in range(8, most + 1, 8):
        if r % cand == 0:
            best = cand
    return best


def _sum_exchange(mine, landed, me, name):
    _, r, c = mine.shape
    tr = _row_tile(r)

    def body(me_ref, m_ref, l_ref, o_ref):
        acc = m_ref[0]
        for k in range(1, N_DEV):
            acc = acc + l_ref[k]
        o_ref[...] = acc

    return pl.pallas_call(
        body, name=name,
        grid_spec=pltpu.PrefetchScalarGridSpec(
            num_scalar_prefetch=1, grid=(r // tr,),
            in_specs=[pl.BlockSpec((1, tr, c), lambda i, me_ref: (me_ref[0], i, 0)),
                      pl.BlockSpec((N_DEV, tr, c), lambda i, me_ref: (0, i, 0))],
            out_specs=pl.BlockSpec((tr, c), lambda i, me_ref: (i, 0))),
        out_shape=jax.ShapeDtypeStruct((r, c), mine.dtype), compiler_params=_params("arbitrary"),
    )(me, mine, landed)


MESH = pl.DeviceIdType.MESH


def _all_gather(x, name):
    r, c = x.shape

    def body(x_ref, out_ref, send_sems, recv_sems, local_sem):
        mx, my, mc = lax.axis_index("x"), lax.axis_index("y"), lax.axis_index("c")
        me, sibling = (mx, my, mc), (mx, my, 1 - mc)
        chips = [(1 - mx, my), (mx, 1 - my), (1 - mx, 1 - my)]

        def slot(px, py, pc):
            return out_ref.at[4 * px + 2 * py + pc]

        def copy(k, block, to, src=None):
            return pltpu.make_async_remote_copy(
                src_ref=slot(*block) if src is None else src, dst_ref=slot(*block),
                send_sem=send_sems.at[k], recv_sem=recv_sems.at[k], device_id=to, device_id_type=MESH)

        mine = pltpu.make_async_copy(x_ref, slot(*me), local_sem)
        mine.start()
        first = [copy(0, me, sibling, src=x_ref)]
        first += [copy(1 + j, me, (*chip, mc), src=x_ref) for j, chip in enumerate(chips)]
        for cp in first:
            cp.start()
        passed = [copy(4 + j, (*chip, mc), sibling) for j, chip in enumerate(chips)]
        for j, chip in enumerate(chips):
            copy(1 + j, (*chip, mc), me).wait_recv()
            passed[j].start()
        copy(0, sibling, me).wait_recv()
        for j, chip in enumerate(chips):
            copy(4 + j, (*chip, 1 - mc), me).wait_recv()
        for cp in first + passed:
            cp.wait_send()
        mine.wait()

    any_spec = pl.BlockSpec(memory_space=pl.ANY)
    return pl.pallas_call(
        body, name=name, in_specs=[any_spec], out_specs=any_spec,
        out_shape=jax.ShapeDtypeStruct((N_DEV, r, c), x.dtype),
        scratch_shapes=[pltpu.SemaphoreType.DMA((7,)), pltpu.SemaphoreType.DMA((7,)), pltpu.SemaphoreType.DMA(())],
    )(x)


HBM_SPEC = pl.BlockSpec(memory_space=pltpu.HBM)
SEM_SPEC = pl.BlockSpec(memory_space=pltpu.SEMAPHORE)
SPLIT_EFFECT = pltpu.SideEffectType.DATAFLOW_SIDE_EFFECTING


def _exchange_copies(src_ref, land_ref, send_sems, recv_sems, gather):
    mx, my, mc = lax.axis_index("x"), lax.axis_index("y"), lax.axis_index("c")
    me = 4 * mx + 2 * my + mc
    copies = []
    for k in range(1, N_DEV):
        px = 1 - mx if k & 4 else mx
        py = 1 - my if k & 2 else my
        pc = 1 - mc if k & 1 else mc
        src = src_ref if gather else src_ref.at[4 * px + 2 * py + pc]
        dst = land_ref.at[me] if gather else land_ref.at[k]
        copies.append(pltpu.make_async_remote_copy(
            src_ref=src, dst_ref=dst, send_sem=send_sems.at[k - 1], recv_sem=recv_sems.at[k - 1],
            device_id=(px, py, pc), device_id_type=MESH))
    return copies


def _exchange_start(src, after, gather, name):
    land_shape = (N_DEV,) + src.shape[-2:]

    def body(src_ref, land_ref, after_ref, send_sems, recv_sems, src_thru, land_thru, token):
        for cp in _exchange_copies(src_ref, land_ref, send_sems, recv_sems, gather):
            cp.start()
        token[...] = jnp.zeros_like(token)

    return pl.pallas_call(
        body, name=name,
        out_shape=(pltpu.SemaphoreType.DMA((N_DEV - 1,)), pltpu.SemaphoreType.DMA((N_DEV - 1,)),
                   pltpu.HBM(src.shape, src.dtype), pltpu.HBM(land_shape, src.dtype),
                   jax.ShapeDtypeStruct((8, 128), F32)),
        in_specs=(HBM_SPEC, HBM_SPEC, pl.BlockSpec(memory_space=pl.ANY)),
        out_specs=(SEM_SPEC, SEM_SPEC, HBM_SPEC, HBM_SPEC, pl.BlockSpec(memory_space=pltpu.VMEM)),
        input_output_aliases={0: 2, 1: 3},
        compiler_params=pltpu.CompilerParams(has_side_effects=SPLIT_EFFECT),
    )(pltpu.with_memory_space_constraint(src, pltpu.HBM),
      pltpu.with_memory_space_constraint(lax.empty(land_shape, src.dtype), pltpu.HBM), after)


def _exchange_wait(started, after, gather, name):
    send_sems, recv_sems, src_thru, land_thru, _ = started

    def body(src_ref, land_ref, send_sems, recv_sems, after_ref, src_out, land_out):
        for cp in _exchange_copies(src_ref, land_ref, send_sems, recv_sems, gather):
            cp.wait_send()
            cp.wait_recv()

    return pl.pallas_call(
        body, name=name,
        out_shape=(pltpu.HBM(src_thru.shape, src_thru.dtype), pltpu.HBM(land_thru.shape, land_thru.dtype)),
        in_specs=(HBM_SPEC, HBM_SPEC, SEM_SPEC, SEM_SPEC, pl.BlockSpec(memory_space=pl.ANY)),
        out_specs=(HBM_SPEC, HBM_SPEC), input_output_aliases={0: 0, 1: 1},
        compiler_params=pltpu.CompilerParams(has_side_effects=SPLIT_EFFECT),
    )(src_thru, land_thru, send_sems, recv_sems, after)


FFN_ROWS = D_FF // N_DEV
WIN_ROWS = (Q_LORA + KV_LORA + QK_ROPE) // N_DEV
WIN_ROWS_PAD = 144
WQ_ROWS = QK_HEAD * Q_LORA // D_MODEL
WKV_ROWS = 256 * KV_LORA // D_MODEL
WOUT_ROWS = V_HEAD
POOL_ROWS = 4 * 32 * POOL_GROUP // D_MODEL


COLUMN_SHARDED = ("ffn1_w_gate", "ffn1_w_up", "ffn2_w_gate", "ffn2_w_up", "mla_w_in", "mla_w_q_up", "mla_w_kv_up")


def _t(w):
    return jnp.swapaxes(w, -1, -2)


def _ffn_segments(f, i):
    return [(f + "_w_gate", i, FFN_ROWS), (f + "_w_up", i, FFN_ROWS), (f + "_w_down", i, FFN_ROWS)]


def _mixer_segments(i):
    j = i // 2
    if i % 2 == 0:
        return [("pool_w", j, POOL_ROWS)]
    return [("mla_w_in", j, WIN_ROWS_PAD), ("mla_w_q_up", j, WQ_ROWS), ("mla_w_kv_up", j, WKV_ROWS),
            ("mla_w_out", j, WOUT_ROWS)]


def _pack_shards(p, segs, dtype):
    parts = []
    for name, idx, _ in segs:
        w = p[name][idx]
        if name.endswith("w_gate") or name.endswith("w_up"):
            w = _t(w)
        elif name == "mla_w_in":
            w = jnp.pad(_t(w), ((0, WIN_ROWS_PAD - WIN_ROWS), (0, 0)))
        elif name == "mla_w_q_up":
            w = _t(w).reshape(WQ_ROWS, D_MODEL)
        elif name == "mla_w_kv_up":
            w = _t(w).reshape(WKV_ROWS, D_MODEL)
        elif name == "pool_w":
            w = w.reshape(POOL_ROWS, D_MODEL)
        parts.append(w.astype(dtype))
    return jnp.concatenate(parts, axis=0)


def _unpack_gathered(g, segs):
    out = {}
    off = 0
    for name, layer, rows in segs:
        seg = g[:, off:off + rows, :]
        off += rows
        if name.startswith("ffn"):
            out[(name, layer)] = (g, off - rows)
            continue
        if name == "mla_w_in":
            w = seg[:, :WIN_ROWS].reshape(N_DEV * WIN_ROWS, D_MODEL)
            w = jnp.pad(w, ((0, LAT_PAD - N_DEV * WIN_ROWS), (0, 0)))
        elif name == "mla_w_q_up":
            w = seg.reshape(N_HEADS, QK_HEAD, Q_LORA)
            w = jnp.pad(w, ((0, 0), (0, HEAD_PAD - QK_HEAD), (0, 0))).reshape(N_HEADS * HEAD_PAD, Q_LORA)
        elif name == "mla_w_kv_up":
            w = seg.reshape(N_HEADS * 256, KV_LORA)
        elif name == "pool_w":
            w = seg.reshape(N_DEV, 4, 32, POOL_GROUP).transpose(1, 0, 2, 3).reshape(4, POOL_GROUP, POOL_GROUP)
        else:
            w = seg.reshape(N_DEV * rows, D_MODEL)
        out[(name, layer)] = w
    return out


def _pack_grads(gr, segments):
    segs = []
    for name, layer, rows in segments:
        g = gr[(name, layer)]
        if name == "mla_w_in":
            g = g[:N_DEV * WIN_ROWS].reshape(N_DEV, WIN_ROWS, D_MODEL)
            g = jnp.pad(g, ((0, 0), (0, WIN_ROWS_PAD - WIN_ROWS), (0, 0)))
        elif name == "mla_w_q_up":
            g = g.reshape(N_HEADS, HEAD_PAD, Q_LORA)[:, :QK_HEAD].reshape(N_DEV, WQ_ROWS, D_MODEL)
        elif name == "mla_w_kv_up":
            g = g.reshape(N_DEV, WKV_ROWS, D_MODEL)
        elif name == "pool_w":
            g = g.reshape(4, N_DEV, 32, POOL_GROUP).transpose(1, 0, 2, 3).reshape(N_DEV, POOL_ROWS, D_MODEL)
        else:
            g = g.reshape(N_DEV, rows, D_MODEL)
        segs.append(g)
    return jnp.concatenate(segs, axis=1)


def _unpack_shard_grads(flat, segments):
    per = {}
    off = 0
    for name, layer, rows in segments:
        seg = flat[off:off + rows]
        off += rows
        if name == "mla_w_in":
            g = seg[:WIN_ROWS]
        elif name == "mla_w_q_up":
            g = seg.reshape(QK_HEAD, Q_LORA)
        elif name == "mla_w_kv_up":
            g = seg.reshape(256, KV_LORA)
        elif name == "pool_w":
            g = seg.reshape(4, 32, POOL_GROUP)
        else:
            g = seg
        per[(name, layer)] = g
    return per


def _pad_lanes(v, width):
    return jnp.pad(v, ((0, 0), (0, width - v.shape[-1])))


def kernel(x, positions, ffn1_norm, ffn1_w_gate, ffn1_w_up, ffn1_w_down, mix_norm, pool_w, pool_scale, mla_w_in, mla_q_norm, mla_w_q_up, mla_kv_norm, mla_w_kv_up, mla_q_head_norm, mla_k_head_norm, mla_w_out, ffn2_norm, ffn2_w_gate, ffn2_w_up, ffn2_w_down, loss_target, m_ffn1_norm, m_ffn1_w_gate, m_ffn1_w_up, m_ffn1_w_down, m_mix_norm, m_pool_w, m_pool_scale, m_mla_w_in, m_mla_q_norm, m_mla_w_q_up, m_mla_kv_norm, m_mla_w_kv_up, m_mla_q_head_norm, m_mla_k_head_norm, m_mla_w_out, m_ffn2_norm, m_ffn2_w_gate, m_ffn2_w_up, m_ffn2_w_down, v_ffn1_norm, v_ffn1_w_gate, v_ffn1_w_up, v_ffn1_w_down, v_mix_norm, v_pool_w, v_pool_scale, v_mla_w_in, v_mla_q_norm, v_mla_w_q_up, v_mla_kv_norm, v_mla_w_kv_up, v_mla_q_head_norm, v_mla_k_head_norm, v_mla_w_out, v_ffn2_norm, v_ffn2_w_gate, v_ffn2_w_up, v_ffn2_w_down):
    args = (x, positions, ffn1_norm, ffn1_w_gate, ffn1_w_up, ffn1_w_down, mix_norm, pool_w, pool_scale, mla_w_in,
            mla_q_norm, mla_w_q_up, mla_kv_norm, mla_w_kv_up, mla_q_head_norm, mla_k_head_norm, mla_w_out, ffn2_norm,
            ffn2_w_gate, ffn2_w_up, ffn2_w_down)
    p = dict(zip(NAMES, args))
    moments_m = dict(zip(WEIGHTS, (m_ffn1_norm, m_ffn1_w_gate, m_ffn1_w_up, m_ffn1_w_down, m_mix_norm, m_pool_w, m_pool_scale, m_mla_w_in, m_mla_q_norm, m_mla_w_q_up, m_mla_kv_norm, m_mla_w_kv_up, m_mla_q_head_norm, m_mla_k_head_norm, m_mla_w_out, m_ffn2_norm, m_ffn2_w_gate, m_ffn2_w_up, m_ffn2_w_down)))
    moments_v = dict(zip(WEIGHTS, (v_ffn1_norm, v_ffn1_w_gate, v_ffn1_w_up, v_ffn1_w_down, v_mix_norm, v_pool_w, v_pool_scale, v_mla_w_in, v_mla_q_norm, v_mla_w_q_up, v_mla_kv_norm, v_mla_w_kv_up, v_mla_q_head_norm, v_mla_k_head_norm, v_mla_w_out, v_ffn2_norm, v_ffn2_w_gate, v_ffn2_w_up, v_ffn2_w_down)))
    dev = 4 * lax.axis_index("x") + 2 * lax.axis_index("y") + lax.axis_index("c")

    xs = x[0]
    n = xs.shape[0]
    target = loss_target[0]

    inv_freq = 1.0 / (ROPE_THETA ** (jnp.arange(0, QK_ROPE, 2, dtype=F32) / QK_ROPE))
    ang = positions[0].astype(F32)[..., None] * inv_freq
    cos, sin = jnp.cos(ang), jnp.sin(ang)
    zero = jnp.zeros((n, 128 - QK_ROPE), F32)
    rope_cos = jnp.concatenate([cos, cos, zero], axis=1)
    rope_sin = jnp.concatenate([-sin, sin, zero], axis=1)

    ag_groups = [_ffn_segments("ffn1", 0), _mixer_segments(0) + _ffn_segments("ffn2", 0)]
    ag_groups += [_ffn_segments("ffn1", i) + _mixer_segments(i) + _ffn_segments("ffn2", i) for i in range(1, DEPTH)]
    shards = [_pack_shards(p, segs, BF16) for segs in ag_groups]
    w = {}
    no_token = jnp.zeros((8, 128), F32)

    def tied(gain, token):
        return gain + token[0, 0]

    gains_local = jnp.concatenate([_pad_lanes(mla_q_norm, 128), _pad_lanes(mla_kv_norm, 128)], axis=0)
    gains_all = _all_gather(jnp.pad(gains_local, ((0, 4), (0, 0))), "gains_all_gather")
    q_norm_full = gains_all[:, 0:2, :Q_LORA // N_DEV].transpose(1, 0, 2).reshape(2, Q_LORA)
    kv_norm_full = gains_all[:, 2:4, :KV_LORA // N_DEV].transpose(1, 0, 2).reshape(2, KV_LORA)
    ghq = _pad_lanes(mla_q_head_norm, HEAD_PAD)
    ghk = _pad_lanes(mla_k_head_norm, HEAD_PAD)

    def ffn_weights(f, i):
        parts = [w[(f + part, i)] for part in ("_w_gate", "_w_up", "_w_down")]
        return parts[0][0], tuple(off for _, off in parts)

    def mla_weights(j):
        return (w[("mla_w_in", j)], q_norm_full[j:j + 1], kv_norm_full[j:j + 1], w[("mla_w_q_up", j)],
                w[("mla_w_kv_up", j)], ghq[j:j + 1], ghk[j:j + 1], rope_cos, rope_sin)

    saved = []
    cur = xs
    gathers = [_exchange_start(shards[0], gains_all, True, "ag_start_0")]

    def gather_step(after):
        g = len(gathers) - 1
        _, landed = _exchange_wait(gathers[g], after, True, f"ag_wait_{g}")
        w.update(_unpack_gathered(lax.dynamic_update_slice(landed, shards[g][None], (dev, 0, 0)), ag_groups[g]))
        if g + 1 == len(ag_groups):
            return no_token
        gathers.append(_exchange_start(shards[g + 1], landed, True, f"ag_start_{g + 1}"))
        return gathers[-1][4]

    for i in range(DEPTH):
        j = i // 2
        st = {"x0": cur}
        token = gather_step(cur)
        cur, st["a1"], st["u1"] = _ffn_fwd(cur, tied(ffn1_norm[i:i + 1], token), *ffn_weights("ffn1", i),
                                           f"ffn1_fwd_{i}")
        st["x1"] = cur
        if i == 0:
            token = gather_step(cur)
        if i % 2 == 0:
            cur = _pool_fwd(cur, tied(mix_norm[i:i + 1], token), w[("pool_w", j)], pool_scale[j:j + 1],
                            f"pool_fwd_{i}")
        else:
            st["q"], st["k"], st["v"], vt = _mla_pre_fwd(cur, mix_norm[i:i + 1], *mla_weights(j), f"mla_pre_fwd_{i}")
            st["o"], lse, st["m"], st["p"] = _flash_fwd(st["q"], st["k"], vt, f"flash_fwd_{i}")
            tq_bwd = _tiles(n)["bwd_q"]
            st["lse"] = lse.reshape(N_HEADS, n // tq_bwd, 1, tq_bwd)
            cur = _mla_post_fwd(st["o"], cur, w[("mla_w_out", j)], f"mla_post_fwd_{i}")
        st["x2"] = cur
        cur, st["a2"], st["u2"] = _ffn_fwd(cur, ffn2_norm[i:i + 1], *ffn_weights("ffn2", i), f"ffn2_fwd_{i}")
        saved.append(st)

    dcur, sq_err = _loss_head(cur, target, "loss_head")
    loss_part = 0.5 * jnp.sum(sq_err) * (1.0 / D_MODEL)

    gr = {}
    small = {k: [None] * DEPTH for k in ("ffn1_norm", "mix_norm", "ffn2_norm")}
    small.update({k: [None] * (DEPTH // 2) for k in ("pool_scale", "mla_q_norm", "mla_kv_norm", "mla_q_head_norm",
                                                     "mla_k_head_norm")})

    me = jnp.reshape(dev, (1,)).astype(jnp.int32)
    grads = {}
    in_flight = []

    def reduce_start(packed, segs, tag):
        started = _exchange_start(packed, dcur, False, f"rs_start_{tag}")
        reduce_finish(started[4])
        in_flight.append((started, segs, tag))
        return started[4]

    def reduce_finish(after):
        if in_flight:
            started, segs, tag = in_flight.pop()
            mine, landed = _exchange_wait(started, after, False, f"rs_wait_{tag}")
            grads.update(_unpack_shard_grads(_sum_exchange(mine, landed, me, f"rs_sum_{tag}"), segs))

    def ffn_backward(f, i, x_in, a, u, gain, dout, group_rows):
        dx, h, dob, y, da, du, dg = _ffn_bwd(x_in, dout, a, u, gain, *ffn_weights(f, i), f"{f}_bwd_{i}")
        packed = _tn_matmul_packed(da, h, None, 0, group_rows, f"{f}_dgate_{i}")
        packed = _tn_matmul_packed(du, h, packed, 1, group_rows, f"{f}_dup_{i}")
        packed = _tn_matmul_packed(y, dob, packed, 2, group_rows, f"{f}_ddown_{i}")
        small[f + "_norm"][i] = jnp.sum(dg, axis=0)
        return dx, packed

    token = jnp.zeros((8, 128), F32)
    for i in reversed(range(DEPTH)):
        j = i // 2
        st = saved[i]
        mixer_rows = sum(rows for _, _, rows in _mixer_segments(i))
        dcur, packed = ffn_backward("ffn2", i, st["x2"], st["a2"], st["u2"], tied(ffn2_norm[i:i + 1], token), dcur,
                                    3 * FFN_ROWS + mixer_rows)
        if i % 2 == 0:
            dcur, dpw, dsc, dg = _pool_bwd(st["x1"], dcur, mix_norm[i:i + 1], w[("pool_w", j)], pool_scale[j:j + 1],
                                           f"pool_bwd_{i}")
            gr[("pool_w", j)] = dpw
            small["pool_scale"][j] = jnp.sum(dsc, axis=0)
            small["mix_norm"][i] = jnp.sum(dg, axis=0)
        else:
            dob, dpo, delta = _mla_out_bwd(dcur, st["o"], w[("mla_w_out", j)], f"mla_out_bwd_{i}")
            gr[("mla_w_out", j)] = _tn_matmul(st["o"], dob, f"mla_dout_{i}")
            dk, dv, dq = _flash_bwd(st["q"], st["k"], st["v"], dpo, st["lse"], delta, st["p"], st["m"],
                                    f"flash_bwd_{i}")
            (dcur, h, dlat, cqn, dqp, ckvn, dkv, dg, dgq, dgkv, dghq, dghk) = _mla_pre_bwd(
                st["x1"], dcur, dq, dk, dv, mix_norm[i:i + 1], *mla_weights(j), f"mla_pre_bwd_{i}")
            gr[("mla_w_in", j)] = _tn_matmul(dlat, h, f"mla_din_{i}")
            gr[("mla_w_q_up", j)] = _tn_matmul(dqp, cqn, f"mla_dqup_{i}")
            gr[("mla_w_kv_up", j)] = _tn_matmul(dkv, ckvn, f"mla_dkvup_{i}")
            small["mix_norm"][i] = jnp.sum(dg, axis=0)
            small["mla_q_norm"][j] = _pad_lanes(jnp.sum(dgq, axis=0)[None], D_MODEL)[0]
            small["mla_kv_norm"][j] = _pad_lanes(jnp.sum(dgkv, axis=0)[None], D_MODEL)[0]
            small["mla_q_head_norm"][j] = _pad_lanes(jnp.sum(dghq, axis=0)[None], D_MODEL)[0]
            small["mla_k_head_norm"][j] = _pad_lanes(jnp.sum(dghk, axis=0)[None], D_MODEL)[0]
        packed = lax.dynamic_update_slice(packed, _pack_grads(gr, _mixer_segments(i)), (0, 3 * FFN_ROWS, 0))
        token = reduce_start(packed, _ffn_segments("ffn2", i) + _mixer_segments(i), f"a{i}")
        if i > 0:
            dcur, packed = ffn_backward("ffn1", i, st["x0"], st["a1"], st["u1"], tied(ffn1_norm[i:i + 1], token),
                                        dcur, 3 * FFN_ROWS)
            token = reduce_start(packed, _ffn_segments("ffn1", i), f"b{i}")
    st = saved[0]
    dcur, h, dob, y, da, du, dg = _ffn_bwd(st["x0"], dcur, st["a1"], st["u1"], tied(ffn1_norm[0:1], token),
                                           *ffn_weights("ffn1", 0), "ffn1_bwd_0")
    small["ffn1_norm"][0] = jnp.sum(dg, axis=0)
    grad_x = dcur[None]
    small_order = ("ffn1_norm", "mix_norm", "ffn2_norm", "pool_scale", "mla_q_norm", "mla_kv_norm",
                   "mla_q_head_norm", "mla_k_head_norm")
    rows = [r for k in small_order for r in small[k]]
    rows.append(jnp.zeros((D_MODEL,), F32).at[0].set(loss_part))
    rows.append(jnp.zeros((D_MODEL,), F32))
    small_sum = _sum_parts(_all_gather(jnp.stack(rows), "small_all_gather"), "small_sum")
    loss = small_sum[SM_ROWS - 2, 0]
    token = small_sum
    for seg, lhs, rhs in zip(_ffn_segments("ffn1", 0), (da, du, y), (h, h, dob)):
        packed = _tn_matmul_packed(lhs, rhs, None, 0, FFN_ROWS, f"ffn1_d{seg[0][7:]}_0", behind=token)
        token = reduce_start(packed, [seg], f"b0_{seg[0][7:]}")
    last = "ffn1_w_down"

    off = 0
    for k in small_order:
        cnt = len(small[k])
        g = small_sum[off:off + cnt]
        off += cnt
        if k == "mla_q_norm":
            g = lax.dynamic_slice_in_dim(g[:, :Q_LORA], dev * (Q_LORA // N_DEV), Q_LORA // N_DEV, axis=1)
        elif k == "mla_kv_norm":
            g = lax.dynamic_slice_in_dim(g[:, :KV_LORA], dev * (KV_LORA // N_DEV), KV_LORA // N_DEV, axis=1)
        elif k in ("mla_q_head_norm", "mla_k_head_norm"):
            g = g[:, :QK_HEAD]
        grads[k] = g

    deltas, new_m, new_v = {}, {}, {}

    def update(k, behind):
        to_view = _t if k in COLUMN_SHARDED else (lambda a: a)
        shape = to_view(p[k]).shape
        operands = [to_view(a) for a in (p[k], moments_m[k], moments_v[k])]
        if k in grads:
            view = (-1, shape[-1])
            w2, m2, v2 = (a.reshape(view) for a in operands)
            results = (*_adamw(w2, grads[k].reshape(view), m2, v2, behind, "adamw_" + k), grads[k])
        else:
            view = (shape[0], -1, shape[-1])
            pieces = [grads[(k, idx)].reshape(view[1:]) for idx in range(shape[0])]
            w3, m3, v3 = (a.reshape(view) for a in operands)
            results = _adamw_layers(w3, pieces, m3, v3, behind, "adamw_" + k)
        deltas[k], new_m[k], new_v[k], grads[k] = (to_view(a.reshape(shape)) for a in results)
        return results[0]

    done = token
    for k in WEIGHTS:
        if k != last:
            done = update(k, done)
    reduce_finish(done)
    update(last, done)

    return (loss, grad_x, *[grads[k] for k in WEIGHTS], *[deltas[k] for k in WEIGHTS],
            *[new_m[k] for k in WEIGHTS], *[new_v[k] for k in WEIGHTS])
```

```python
import functools

import jax
import jax.numpy as jnp
from jax import lax
from jax.experimental import pallas as pl
from jax.experimental.pallas import tpu as pltpu

F32 = jnp.float32
BF16 = jnp.bfloat16

D_MODEL = 1024
DEPTH = 4
D_FF = 2816
POOL_WINDOWS = (2, 4, 8, 16)
POOL_GROUP = 256
POOL_HALO = 16
N_HEADS = 8
QK_NOPE = 128
QK_ROPE = 64
QK_HEAD = 192
V_HEAD = 128
Q_LORA = 768
KV_LORA = 256
ROPE_THETA = 10000.0
EPS = 1e-6
FFN_HALF = 0.5
ADAM_LR = 0.001
ADAM_B1 = 0.9
ADAM_B2 = 0.999
ADAM_EPS = 1e-08
ADAM_WD = 0.01
ADAM_STEP = 10

N_DEV = 8
HEAD_PAD = 256
LAT_PAD = 1152
VT_ROWS = 144
LOG2_E = 1.4426950408889634
ATTN_SCALE = QK_HEAD ** -0.5
LOGIT_SCALE = ATTN_SCALE * LOG2_E
V7X_VMEM_LIMIT = 56 * 1024 * 1024
FF_CHUNK = 256
SM_ROWS = 24

NAMES = ['x', 'positions', 'ffn1_norm', 'ffn1_w_gate', 'ffn1_w_up', 'ffn1_w_down', 'mix_norm', 'pool_w',
         'pool_scale', 'mla_w_in', 'mla_q_norm', 'mla_w_q_up', 'mla_kv_norm', 'mla_w_kv_up', 'mla_q_head_norm',
         'mla_k_head_norm', 'mla_w_out', 'ffn2_norm', 'ffn2_w_gate', 'ffn2_w_up', 'ffn2_w_down']
WEIGHTS = NAMES[2:]


def _tiles(n):
    return dict(ffn_fwd=min(512, n), ffn_bwd=min(256, n), fwd_k=min(512, n // 2), bwd_q=min(512, n // 2),
                mla_bwd=min(256, n), pool=min(512, n), tn=min(2048, n), rows=min(1024, n))


def _params(*sem):
    return pltpu.CompilerParams(dimension_semantics=sem, vmem_limit_bytes=V7X_VMEM_LIMIT)


def _dot(a, b):
    return jnp.dot(a, b, preferred_element_type=F32)


def _dot_nt(a, b):
    return lax.dot_general(a, b, (((1,), (1,)), ((), ())), preferred_element_type=F32)


def _dot_tn(a, b):
    return lax.dot_general(a, b, (((0,), (0,)), ((), ())), preferred_element_type=F32)


def _rowsum8(v):
    rows, w = v.shape
    return jnp.sum(v.reshape(rows // 8, 8, w), axis=0)


def _sigmoid(a):
    return 1.0 / (1.0 + jnp.exp(-a))


def _lane_sum(v):
    f = v[:, :128]
    for t in range(1, v.shape[1] // 128):
        f = f + v[:, t * 128:(t + 1) * 128]
    hi = f.astype(BF16)
    lo = (f - hi.astype(F32)).astype(BF16)
    return _dot(jnp.concatenate([hi, lo], axis=1), jnp.ones((256, 128), BF16))


def _by_row(v, r):
    return jnp.concatenate([v[:, t * 128:(t + 1) * 128] * r for t in range(v.shape[1] // 128)], axis=1)


def _rms_fwd(x, width=None):
    width = x.shape[-1] if width is None else width
    r = lax.rsqrt(_lane_sum(x * x) * (1.0 / width) + EPS)
    return _by_row(x, r), r


def _rms_bwd(dy, xhat, r, gain, width=None):
    width = xhat.shape[-1] if width is None else width
    t = dy * gain
    return _by_row(t - _by_row(xhat, _lane_sum(t * xhat) * (1.0 / width)), r)


def _full(shape):
    return pl.BlockSpec(shape, lambda *_: (0,) * len(shape))


def _load_weights(gathered, offsets, dsts, sems):
    rows = D_FF // N_DEV
    copies = [pltpu.make_async_copy(gathered.at[j, pl.ds(off, rows), :], dst.at[pl.ds(j * rows, rows), :],
                                    sems.at[N_DEV * k + j])
              for k, (off, dst) in enumerate(zip(offsets, dsts)) for j in range(N_DEV)]
    for cp in copies:
        cp.start()
    for cp in copies:
        cp.wait()


def _ffn_fwd(x, gain, gathered, offsets, name):
    n = x.shape[0]
    tm = _tiles(n)["ffn_fwd"]

    def body(x_ref, g_ref, gathered_ref, out_ref, a_ref, u_ref, wg_v, wu_v, wd_v, sems):
        @pl.when(pl.program_id(0) == 0)
        def _():
            _load_weights(gathered_ref, offsets, (wg_v, wu_v, wd_v), sems)

        xt = x_ref[...]
        xhat, _ = _rms_fwd(xt)
        h = (xhat * g_ref[...]).astype(BF16)
        acc = jnp.zeros((tm, D_MODEL), F32)
        for c in range(D_FF // FF_CHUNK):
            sl = pl.ds(c * FF_CHUNK, FF_CHUNK)
            a = _dot_nt(h, wg_v[sl, :])
            u = _dot_nt(h, wu_v[sl, :])
            a_ref[:, sl] = a.astype(BF16)
            u_ref[:, sl] = u.astype(BF16)
            y = (a * _sigmoid(a) * u).astype(BF16)
            acc = acc + _dot(y, wd_v[sl, :])
        out_ref[...] = xt + FFN_HALF * acc

    any_spec = pl.BlockSpec(memory_space=pl.ANY)
    return pl.pallas_call(
        body, name=name, grid=(n // tm,),
        in_specs=[pl.BlockSpec((tm, D_MODEL), lambda i: (i, 0)), _full((1, D_MODEL)), any_spec],
        out_specs=[pl.BlockSpec((tm, D_MODEL), lambda i: (i, 0)), pl.BlockSpec((tm, D_FF), lambda i: (i, 0)),
                   pl.BlockSpec((tm, D_FF), lambda i: (i, 0))],
        out_shape=[jax.ShapeDtypeStruct((n, D_MODEL), F32), jax.ShapeDtypeStruct((n, D_FF), BF16),
                   jax.ShapeDtypeStruct((n, D_FF), BF16)],
        scratch_shapes=[pltpu.VMEM((D_FF, D_MODEL), BF16)] * 3 + [pltpu.SemaphoreType.DMA((3 * N_DEV,))],
        compiler_params=_params("arbitrary"),
    )(x, gain, gathered)


def _ffn_bwd(x, dout, a, u, gain, gathered, offsets, name):
    n = x.shape[0]
    tm = _tiles(n)["ffn_bwd"]

    def body(x_ref, do_ref, a_ref, u_ref, g_ref, gathered_ref,
             dx_ref, h_ref, dob_ref, y_ref, da_ref, du_ref, dg_ref, wg_v, wu_v, wd_v, sems):
        @pl.when(pl.program_id(0) == 0)
        def _():
            _load_weights(gathered_ref, offsets, (wg_v, wu_v, wd_v), sems)
            dg_ref[...] = jnp.zeros_like(dg_ref)

        xt = x_ref[...]
        g = g_ref[...]
        xhat, r = _rms_fwd(xt)
        h_ref[...] = (xhat * g).astype(BF16)
        dout = do_ref[...]
        dob = (FFN_HALF * dout).astype(BF16)
        dob_ref[...] = dob
        for c in range(D_FF // FF_CHUNK):
            sl = pl.ds(c * FF_CHUNK, FF_CHUNK)
            dy = _dot_nt(dob, wd_v[sl, :])
            av = a_ref[:, sl].astype(F32)
            uv = u_ref[:, sl].astype(F32)
            s = _sigmoid(av)
            silu = av * s
            y_ref[:, sl] = (silu * uv).astype(BF16)
            du_ref[:, sl] = (dy * silu).astype(BF16)
            da_ref[:, sl] = (dy * uv * (s * (1.0 + av * (1.0 - s)))).astype(BF16)
        dh = _dot(da_ref[...], wg_v[...]) + _dot(du_ref[...], wu_v[...])
        dg_ref[...] += _rowsum8(dh * xhat)
        dx_ref[...] = dout + _rms_bwd(dh, xhat, r, g)

    any_spec = pl.BlockSpec(memory_space=pl.ANY)
    row_d = pl.BlockSpec((tm, D_MODEL), lambda i: (i, 0))
    row_f = pl.BlockSpec((tm, D_FF), lambda i: (i, 0))
    return pl.pallas_call(
        body, name=name, grid=(n // tm,),
        in_specs=[row_d, row_d, row_f, row_f, _full((1, D_MODEL)), any_spec],
        out_specs=[row_d, row_d, row_d, row_f, row_f, row_f, _full((8, D_MODEL))],
        out_shape=[jax.ShapeDtypeStruct((n, D_MODEL), F32), jax.ShapeDtypeStruct((n, D_MODEL), BF16),
                   jax.ShapeDtypeStruct((n, D_MODEL), BF16), jax.ShapeDtypeStruct((n, D_FF), BF16),
                   jax.ShapeDtypeStruct((n, D_FF), BF16), jax.ShapeDtypeStruct((n, D_FF), BF16),
                   jax.ShapeDtypeStruct((8, D_MODEL), F32)],
        scratch_shapes=[pltpu.VMEM((D_FF, D_MODEL), BF16)] * 3 + [pltpu.SemaphoreType.DMA((3 * N_DEV,))],
        compiler_params=_params("arbitrary"),
    )(x, dout, a, u, gain, gathered)


def _tn_matmul(a, b, name, behind=None):
    n, fa = a.shape
    db = b.shape[1]
    tk = _tiles(n)["tn"]
    tf = fa // 2 if (fa // 2) % 128 == 0 and fa > 1024 else fa
    behind = jnp.zeros((8, 128), F32) if behind is None else behind

    def body(a_ref, b_ref, behind_ref, o_ref):
        @pl.when(pl.program_id(1) == 0)
        def _():
            o_ref[...] = jnp.zeros_like(o_ref)

        o_ref[...] += _dot_tn(a_ref[...], b_ref[...])

    return pl.pallas_call(
        body, name=name, grid=(fa // tf, n // tk),
        in_specs=[pl.BlockSpec((tk, tf), lambda i, k: (k, i)), pl.BlockSpec((tk, db), lambda i, k: (k, 0)),
                  pl.BlockSpec(memory_space=pl.ANY)],
        out_specs=pl.BlockSpec((tf, db), lambda i, k: (i, 0)),
        out_shape=jax.ShapeDtypeStruct((fa, db), F32),
        compiler_params=_params("arbitrary", "arbitrary"),
    )(a, b, behind)


def _tn_matmul_packed(a, b, packed, slot, group_rows, name, behind=None):
    n, fa = a.shape
    db = b.shape[1]
    tk = _tiles(n)["tn"]
    per = fa // N_DEV
    slabs = N_DEV // 2
    behind = jnp.zeros((8, 128), F32) if behind is None else behind

    def body(a_ref, b_ref, behind_ref, *rest):
        o_ref = rest[-1]

        @pl.when(pl.program_id(1) == 0)
        def _():
            o_ref[...] = jnp.zeros_like(o_ref)

        o_ref[...] += _dot_tn(a_ref[...], b_ref[...]).reshape(slabs, per, db)

    any_spec = pl.BlockSpec(memory_space=pl.ANY)
    return pl.pallas_call(
        body, name=name, grid=(2, n // tk),
        in_specs=[pl.BlockSpec((tk, slabs * per), lambda i, k: (k, i)), pl.BlockSpec((tk, db), lambda i, k: (k, 0)),
                  any_spec] + ([] if packed is None else [any_spec]),
        out_specs=pl.BlockSpec((slabs, per, db), lambda i, k: (i, slot, 0)),
        out_shape=jax.ShapeDtypeStruct((N_DEV, group_rows, db), F32),
        input_output_aliases={} if packed is None else {3: 0},
        compiler_params=_params("arbitrary", "arbitrary"),
    )(a, b, behind, *([] if packed is None else [packed]))


def _loss_head(y, target, name):
    n = y.shape[0]
    tm = _tiles(n)["rows"]

    def body(y_ref, t_ref, d_ref, acc_ref):
        @pl.when(pl.program_id(0) == 0)
        def _():
            acc_ref[...] = jnp.zeros_like(acc_ref)

        d = y_ref[...] - t_ref[...]
        d_ref[...] = d * (1.0 / D_MODEL)
        acc_ref[...] += _rowsum8(d * d)

    row = pl.BlockSpec((tm, D_MODEL), lambda i: (i, 0))
    return pl.pallas_call(
        body, name=name, grid=(n // tm,), in_specs=[row, row], out_specs=[row, _full((8, D_MODEL))],
        out_shape=[jax.ShapeDtypeStruct((n, D_MODEL), F32), jax.ShapeDtypeStruct((8, D_MODEL), F32)],
        compiler_params=_params("arbitrary"),
    )(y, target)


def _window_sum(v, w, rows, forward):
    s = v
    sh = 1
    while sh < w:
        s = s + pltpu.roll(s, (rows - sh) if forward else sh, 0)
        sh *= 2
    return s


def _pool_fwd(x, gain, w, scale, name):
    n = x.shape[0]
    tm = _tiles(n)["pool"]
    hb = tm // POOL_HALO
    rows = tm + POOL_HALO

    def body(x_ref, xh_ref, g_ref, w_ref, sc_ref, out_ref):
        i = pl.program_id(0)
        xt = x_ref[...]
        e = jnp.concatenate([xh_ref[...], xt], axis=0)
        xhat, _ = _rms_fwd(e)
        row = lax.broadcasted_iota(jnp.int32, (rows, 1), 0)
        hn = jnp.where((row >= POOL_HALO) | (i > 0), xhat * g_ref[...], 0.0)
        t_glob = i * tm + row - POOL_HALO
        outs = []
        for gi, win in enumerate(POOL_WINDOWS):
            ug = hn[:, gi * POOL_GROUP:(gi + 1) * POOL_GROUP]
            cnt = jnp.maximum(jnp.minimum(t_glob + 1, win), 1).astype(F32)
            pooled = (_window_sum(ug, win, rows, False) / cnt - ug)[POOL_HALO:]
            outs.append(_dot(pooled.astype(BF16), w_ref[gi]))
        out_ref[...] = xt + jnp.concatenate(outs, axis=1) * sc_ref[...]

    return pl.pallas_call(
        body, name=name, grid=(n // tm,),
        in_specs=[pl.BlockSpec((tm, D_MODEL), lambda i: (i, 0)),
                  pl.BlockSpec((POOL_HALO, D_MODEL), lambda i: (jnp.maximum(i * hb - 1, 0), 0)),
                  _full((1, D_MODEL)), _full((4, POOL_GROUP, POOL_GROUP)), _full((1, D_MODEL))],
        out_specs=pl.BlockSpec((tm, D_MODEL), lambda i: (i, 0)),
        out_shape=jax.ShapeDtypeStruct((n, D_MODEL), F32),
        compiler_params=_params("arbitrary"),
    )(x, x, gain, w, scale)


def _pool_bwd(x, dout, gain, w, scale, name):
    n = x.shape[0]
    tm = _tiles(n)["pool"]
    hb = tm // POOL_HALO
    rows = tm + POOL_HALO
    nt = n // tm

    def body(x_ref, xh_ref, do_ref, doh_ref, g_ref, w_ref, sc_ref, dx_ref, dw_ref, dsc_ref, dg_ref):
        i = pl.program_id(0)

        @pl.when(i == 0)
        def _():
            dw_ref[...] = jnp.zeros_like(dw_ref)
            dsc_ref[...] = jnp.zeros_like(dsc_ref)
            dg_ref[...] = jnp.zeros_like(dg_ref)

        xt = x_ref[...]
        g = g_ref[...]
        e = jnp.concatenate([xh_ref[...], xt], axis=0)
        xhat_e, r_e = _rms_fwd(e)
        row = lax.broadcasted_iota(jnp.int32, (rows, 1), 0)
        hn = jnp.where((row >= POOL_HALO) | (i > 0), xhat_e * g, 0.0)
        t_prev = i * tm + row - POOL_HALO
        t_next = i * tm + row
        dout = do_ref[...]
        dt = jnp.concatenate([dout, doh_ref[...]], axis=0)
        dt = jnp.where((row < tm) | (i < nt - 1), dt, 0.0)
        dyr = dt * sc_ref[...]
        dus, dscs = [], []
        for gi, win in enumerate(POOL_WINDOWS):
            lanes = slice(gi * POOL_GROUP, (gi + 1) * POOL_GROUP)
            ug = hn[:, lanes]
            cnt = jnp.maximum(jnp.minimum(t_prev + 1, win), 1).astype(F32)
            pooled = (_window_sum(ug, win, rows, False) / cnt - ug)[POOL_HALO:].astype(BF16)
            yraw = _dot(pooled, w_ref[gi])
            dscs.append(_rowsum8(dout[:, lanes] * yraw))
            dyr_b = dyr[:, lanes].astype(BF16)
            dw_ref[gi] += _dot_tn(pooled, dyr_b[:tm])
            dpool = _dot_nt(dyr_b, w_ref[gi])
            cnt2 = jnp.minimum(t_next + 1, win).astype(F32)
            dus.append((_window_sum(dpool / cnt2, win, rows, True) - dpool)[:tm])
        dsc_ref[...] += jnp.concatenate(dscs, axis=1)
        dh = jnp.concatenate(dus, axis=1)
        xhat = xhat_e[POOL_HALO:]
        dg_ref[...] += _rowsum8(dh * xhat)
        dx_ref[...] = dout + _rms_bwd(dh, xhat, r_e[POOL_HALO:], g)

    row_d = pl.BlockSpec((tm, D_MODEL), lambda i: (i, 0))
    prev_h = pl.BlockSpec((POOL_HALO, D_MODEL), lambda i: (jnp.maximum(i * hb - 1, 0), 0))
    next_h = pl.BlockSpec((POOL_HALO, D_MODEL), lambda i: (jnp.minimum((i + 1) * hb, n // POOL_HALO - 1), 0))
    return pl.pallas_call(
        body, name=name, grid=(nt,),
        in_specs=[row_d, prev_h, row_d, next_h, _full((1, D_MODEL)), _full((4, POOL_GROUP, POOL_GROUP)),
                  _full((1, D_MODEL))],
        out_specs=[row_d, _full((4, POOL_GROUP, POOL_GROUP)), _full((8, D_MODEL)), _full((8, D_MODEL))],
        out_shape=[jax.ShapeDtypeStruct((n, D_MODEL), F32), jax.ShapeDtypeStruct((4, POOL_GROUP, POOL_GROUP), F32),
                   jax.ShapeDtypeStruct((8, D_MODEL), F32), jax.ShapeDtypeStruct((8, D_MODEL), F32)],
        compiler_params=_params("arbitrary"),
    )(x, x, dout, dout, gain, w, scale)


def _rope(v, cos, sin_signed):
    lo, hi = v[:, :128], v[:, 128:]
    lane = lax.broadcasted_iota(jnp.int32, hi.shape, 1)
    swapped = jnp.where(lane < 32, pltpu.roll(hi, 96, 1), pltpu.roll(hi, 32, 1))
    return jnp.concatenate([lo, hi * cos + swapped * sin_signed], axis=1)


def _rope_bwd(gr, cos, sin_signed):
    lo, hi = gr[:, :128], gr[:, 128:]
    t = hi * sin_signed
    lane = lax.broadcasted_iota(jnp.int32, hi.shape, 1)
    swapped = jnp.where(lane < 32, pltpu.roll(t, 96, 1), pltpu.roll(t, 32, 1))
    return jnp.concatenate([lo, hi * cos + swapped], axis=1)


def _mla_latents(h, win_ref):
    cq = _dot_nt(h, win_ref[0:Q_LORA, :])
    ckv = _dot_nt(h, win_ref[Q_LORA:Q_LORA + KV_LORA, :])
    kpe = _dot_nt(h, win_ref[Q_LORA + KV_LORA:LAT_PAD, :])
    return cq, ckv, kpe


def _mla_pre_fwd(x, gain, win, gq, gkv, wq, wkv, ghq, ghk, cos, sin_signed, name):
    n = x.shape[0]
    tm = _tiles(n)["fwd_k"]

    def body(x_ref, g_ref, win_ref, gq_ref, gkv_ref, wq_ref, wkv_ref, ghq_ref, ghk_ref, c_ref, s_ref,
             q_ref, k_ref, v_ref, vt_ref):
        xhat, _ = _rms_fwd(x_ref[...])
        h = (xhat * g_ref[...]).astype(BF16)
        cq, ckv, kpe = _mla_latents(h, win_ref)
        cqn = (_rms_fwd(cq)[0] * gq_ref[...]).astype(BF16)
        ckvn = (_rms_fwd(ckv)[0] * gkv_ref[...]).astype(BF16)
        cos, sn = c_ref[...], s_ref[...]
        for hd in range(N_HEADS):
            rws = pl.ds(hd * HEAD_PAD, HEAD_PAD)
            qh = _dot_nt(cqn, wq_ref[rws, :])
            qn = _rms_fwd(qh, QK_HEAD)[0] * ghq_ref[...]
            q_ref[hd] = (_rope(qn, cos, sn) * LOGIT_SCALE).astype(BF16)
            kvh = _dot_nt(ckvn, wkv_ref[rws, :])
            kpre = jnp.concatenate([kvh[:, :QK_NOPE], kpe], axis=1)
            kn = _rms_fwd(kpre, QK_HEAD)[0] * ghk_ref[...]
            k_ref[hd] = _rope(kn, cos, sn).astype(BF16)
            vh = kvh[:, QK_NOPE:]
            v_ref[hd] = vh.astype(BF16)
            vt_ref[hd, 0] = jnp.concatenate([vh.T, jnp.ones((VT_ROWS - V_HEAD, tm), F32)], axis=0).astype(BF16)

    row = lambda w: pl.BlockSpec((tm, w), lambda i: (i, 0))
    head = lambda w: pl.BlockSpec((N_HEADS, tm, w), lambda i: (0, i, 0))
    return pl.pallas_call(
        body, name=name, grid=(n // tm,),
        in_specs=[row(D_MODEL), _full((1, D_MODEL)), _full((LAT_PAD, D_MODEL)), _full((1, Q_LORA)),
                  _full((1, KV_LORA)), _full((N_HEADS * HEAD_PAD, Q_LORA)), _full((N_HEADS * HEAD_PAD, KV_LORA)),
                  _full((1, HEAD_PAD)), _full((1, HEAD_PAD)), row(128), row(128)],
        out_specs=[head(HEAD_PAD), head(HEAD_PAD), head(V_HEAD),
                   pl.BlockSpec((N_HEADS, 1, VT_ROWS, tm), lambda i: (0, i, 0, 0))],
        out_shape=[jax.ShapeDtypeStruct((N_HEADS, n, HEAD_PAD), BF16), jax.ShapeDtypeStruct((N_HEADS, n, HEAD_PAD), BF16),
                   jax.ShapeDtypeStruct((N_HEADS, n, V_HEAD), BF16),
                   jax.ShapeDtypeStruct((N_HEADS, n // tm, VT_ROWS, tm), BF16)],
        compiler_params=_params("arbitrary"),
    )(x, gain, win, gq, gkv, wq, wkv, ghq, ghk, cos, sin_signed)


def _flash_fwd(q, k, vt, name):
    n = q.shape[1]
    tk = _tiles(n)["fwd_k"]
    tq = 2 * tk
    nq = n // tq

    def body(q_ref, k_ref, vt_ref, o_ref, lse_ref, m_ref, p_hbm, s_scr, m_scr, acc_scr, p_scr, p_sems):
        h = pl.program_id(0)
        i = pl.program_id(1)
        qi = q_ref[0]

        def scores(j, slot):
            s_scr[slot] = _dot_nt(k_ref[0, pl.ds(pl.multiple_of(j * tk, tk), tk), :], qi)

        def p_copy(block, pslot):
            return pltpu.make_async_copy(p_scr.at[pslot], p_hbm.at[h, block], p_sems.at[pslot])

        def update(j, slot, pslot, diagonal=None):
            s = s_scr[slot]
            if diagonal is not None:
                krow = lax.broadcasted_iota(jnp.int32, (tk, tq), 0) + diagonal * tk
                qcol = lax.broadcasted_iota(jnp.int32, (tk, tq), 1)
                s = jnp.where(krow <= qcol, s, -jnp.inf)
            m = m_scr[...]
            m_new = jnp.maximum(m, jnp.max(s, axis=0, keepdims=True))
            p = jnp.exp2(s - m_new).astype(BF16)
            p_scr[pslot] = p
            acc_scr[...] = jnp.exp2(m - m_new) * acc_scr[...] + _dot(vt_ref[0, j], p)
            m_scr[...] = m_new
            m_ref[0, 0, j] = m_new

        def kv_pair(jj, diagonal):
            first = 2 * (jj % 2)
            p_copy(0, first).wait()
            p_copy(0, first + 1).wait()
            scores(2 * jj + 1, 1)
            update(2 * jj, 0, first, 0 if diagonal else None)
            if not diagonal:
                scores(2 * jj + 2, 0)
            update(2 * jj + 1, 1, first + 1, 1 if diagonal else None)
            p_copy(i * (i + 1) + 2 * jj, first).start()
            p_copy(i * (i + 1) + 2 * jj + 1, first + 1).start()

        m_scr[...] = jnp.full((1, tq), -jnp.inf, F32)
        acc_scr[...] = jnp.zeros((VT_ROWS, tq), F32)
        @pl.when((h == 0) & (i == 0))
        def _():
            p_scr[...] = jnp.zeros_like(p_scr)
            for pslot in range(4):
                p_copy(nq * (nq + 1) + pslot, pslot).start()

        scores(0, 0)

        def pair(jj, carry):
            kv_pair(jj, False)
            return carry

        lax.fori_loop(0, i, pair, 0)
        kv_pair(i, True)
        l = acc_scr[V_HEAD:V_HEAD + 1, :]
        o_ref[...] = (acc_scr[0:V_HEAD, :] / l).T.astype(BF16)
        lse_ref[0, 0] = m_scr[...] + jnp.log2(l)

        @pl.when((h == N_HEADS - 1) & (i == nq - 1))
        def _():
            for pslot in range(4):
                p_copy(0, pslot).wait()

    return pl.pallas_call(
        body, name=name, grid=(N_HEADS, nq),
        in_specs=[pl.BlockSpec((1, tq, HEAD_PAD), lambda h, i: (h, i, 0)),
                  pl.BlockSpec((1, n, HEAD_PAD), lambda h, i: (h, 0, 0)),
                  pl.BlockSpec((1, n // tk, VT_ROWS, tk), lambda h, i: (h, 0, 0, 0))],
        out_specs=[pl.BlockSpec((tq, V_HEAD), lambda h, i: (i, h)),
                   pl.BlockSpec((1, 1, 1, tq), lambda h, i: (h, i, 0, 0)),
                   pl.BlockSpec((1, 1, 2 * nq, 1, tq), lambda h, i: (h, i, 0, 0, 0)),
                   pl.BlockSpec(memory_space=pl.ANY)],
        out_shape=[jax.ShapeDtypeStruct((n, N_HEADS * V_HEAD), BF16), jax.ShapeDtypeStruct((N_HEADS, nq, 1, tq), F32),
                   jax.ShapeDtypeStruct((N_HEADS, nq, 2 * nq, 1, tq), F32),
                   jax.ShapeDtypeStruct((N_HEADS, nq * (nq + 1) + 4, tk, tq), BF16)],
        scratch_shapes=[pltpu.VMEM((2, tk, tq), F32), pltpu.VMEM((1, tq), F32), pltpu.VMEM((VT_ROWS, tq), F32),
                        pltpu.VMEM((4, tk, tq), BF16), pltpu.SemaphoreType.DMA((4,))],
        compiler_params=_params("arbitrary", "arbitrary"),
    )(q, k, vt)


def _mla_post_fwd(o, x, wout, name):
    n = x.shape[0]
    tm = _tiles(n)["rows"]

    def body(o_ref, x_ref, w_ref, out_ref):
        out_ref[...] = x_ref[...] + _dot(o_ref[...], w_ref[...])

    row = pl.BlockSpec((tm, D_MODEL), lambda i: (i, 0))
    return pl.pallas_call(
        body, name=name, grid=(n // tm,), in_specs=[row, row, _full((D_MODEL, D_MODEL))], out_specs=row,
        out_shape=jax.ShapeDtypeStruct((n, D_MODEL), F32), compiler_params=_params("arbitrary"),
    )(o, x, wout)


def _mla_out_bwd(dout, o, wout, name):
    n = dout.shape[0]
    t = _tiles(n)["bwd_q"]
    nq = n // t

    def body(do_ref, o_ref, w_ref, dob_ref, dpo_ref, dl_ref):
        dob = do_ref[...].astype(BF16)
        dob_ref[...] = dob
        dpo = _dot_nt(dob, w_ref[...])
        dpo_ref[...] = dpo.astype(BF16)
        ov = o_ref[...].astype(F32)
        for hd in range(N_HEADS):
            lanes = slice(hd * V_HEAD, (hd + 1) * V_HEAD)
            prod = dpo[:, lanes] * ov[:, lanes]
            dl_ref[hd, 0] = jnp.sum(prod.T, axis=0, keepdims=True)

    row = pl.BlockSpec((t, D_MODEL), lambda i: (i, 0))
    return pl.pallas_call(
        body, name=name, grid=(nq,), in_specs=[row, row, _full((D_MODEL, D_MODEL))],
        out_specs=[row, row, pl.BlockSpec((N_HEADS, 1, 1, t), lambda i: (0, i, 0, 0))],
        out_shape=[jax.ShapeDtypeStruct((n, D_MODEL), BF16), jax.ShapeDtypeStruct((n, D_MODEL), BF16),
                   jax.ShapeDtypeStruct((N_HEADS, nq, 1, t), F32)],
        compiler_params=_params("arbitrary"),
    )(dout, o, wout)


def _flash_bwd(q, k, v, dpo, lse, delta, p, m, name):
    n = q.shape[1]
    tq = _tiles(n)["bwd_q"]
    tk = 2 * tq
    nk = n // tk
    nqb = n // tq
    assert p.shape[2:] == (tq, tk) and m.shape[1:] == (nk, 2 * nk, 1, tk), "forward blocks are (tq keys, 2 tq queries)"

    def body(k_ref, v_ref, q_ref, do_ref, lse_ref, dl_ref, m_ref, p_hbm, dk_ref, dv_ref, dq_hbm,
             dq_acc, p_scr, dp_scr, sem, p_sems):
        h = pl.program_id(0)
        j = pl.program_id(1)

        @pl.when(j == 0)
        def _():
            dq_acc[...] = jnp.zeros_like(dq_acc)

        dk_ref[...] = jnp.zeros_like(dk_ref)
        dv_ref[...] = jnp.zeros_like(dv_ref)
        kj = k_ref[0]
        vj = v_ref[0]
        npairs = nk - 1 - j

        def block(t):
            return jnp.where(t < 2 * npairs, 2 * j + 2 + t, 2 * j + (t - 2 * npairs))

        def p_copy(i, first, half, head=h, kv_tile=j):
            tile = i // 2
            return pltpu.make_async_copy(
                p_hbm.at[head, pl.ds(tile * (tile + 1) + 2 * kv_tile, 2), :, pl.ds(half * tq, tq)],
                p_scr.at[first + half], p_sems.at[first + half])

        def fetch_pair(t):
            for half in range(2):
                p_copy(block(2 * t + half), 2 * (t % 2), half).start()

        def dp_ahead(i, half):
            dp_scr[half] = _dot_nt(vj, do_ref[pl.ds(pl.multiple_of(i * tq, tq), tq), :])

        def update(i, first, half):
            rws = pl.ds(pl.multiple_of(i * tq, tq), tq)
            lse_i = lse_ref[0, i]
            lanes = slice(half * tq, (half + 1) * tq)
            pv = jnp.concatenate(
                [p_scr[first + half, b].astype(F32) * jnp.exp2(m_ref[0, i // 2, 2 * j + b][:, lanes] - lse_i)
                 for b in range(2)], axis=0)
            dv_ref[0] += _dot(pv.astype(BF16), do_ref[rws, :])
            ds = (pv * (dp_scr[half] - dl_ref[0, i])).astype(BF16)
            dk_ref[0] += _dot(ds, q_ref[0, rws, :])
            dq_acc[rws, :] += _dot_tn(ds, kj)

        def q_pair(t, last):
            first = 2 * (t % 2)
            for half in range(2):
                p_copy(0, first, half).wait()
            if not last:
                fetch_pair(t + 1)
            dp_ahead(block(2 * t + 1), 1)
            update(block(2 * t), first, 0)
            if not last:
                dp_ahead(block(2 * t + 2), 0)
            update(block(2 * t + 1), first, 1)

        @pl.when((h == 0) & (j == 0))
        def _():
            fetch_pair(0)

        dp_ahead(block(0), 0)

        def pair(t, carry):
            q_pair(t, False)
            return carry

        lax.fori_loop(0, npairs, pair, 0)
        q_pair(npairs, True)

        @pl.when((h < N_HEADS - 1) | (j < nk - 1))
        def _():
            next_h = jnp.where(j == nk - 1, h + 1, h)
            next_j = jnp.where(j == nk - 1, 0, j + 1)
            first_block = jnp.where(next_j < nk - 1, 2 * next_j + 2, 2 * next_j)
            for half in range(2):
                p_copy(first_block + half, 0, half, next_h, next_j).start()
        dk_ref[...] = dk_ref[...] * (ATTN_SCALE / LOGIT_SCALE)

        @pl.when(j == nk - 1)
        def _():
            dq_acc[...] = dq_acc[...] * ATTN_SCALE
            cp = pltpu.make_async_copy(dq_acc, dq_hbm.at[h], sem)
            cp.start()
            cp.wait()

    resident = dict(pipeline_mode=pl.Buffered(1))
    return pl.pallas_call(
        body, name=name, grid=(N_HEADS, nk),
        in_specs=[pl.BlockSpec((1, tk, HEAD_PAD), lambda h, j: (h, j, 0)),
                  pl.BlockSpec((1, tk, V_HEAD), lambda h, j: (h, j, 0)),
                  pl.BlockSpec((1, n, HEAD_PAD), lambda h, j: (h, 0, 0), **resident),
                  pl.BlockSpec((n, V_HEAD), lambda h, j: (0, h), **resident),
                  pl.BlockSpec((1, nqb, 1, tq), lambda h, j: (h, 0, 0, 0)),
                  pl.BlockSpec((1, nqb, 1, tq), lambda h, j: (h, 0, 0, 0)),
                  pl.BlockSpec((1, nk, 2 * nk, 1, tk), lambda h, j: (h, 0, 0, 0, 0), **resident),
                  pl.BlockSpec(memory_space=pl.ANY)],
        out_specs=[pl.BlockSpec((1, tk, HEAD_PAD), lambda h, j: (h, j, 0)),
                   pl.BlockSpec((1, tk, V_HEAD), lambda h, j: (h, j, 0)),
                   pl.BlockSpec(memory_space=pl.ANY)],
        out_shape=[jax.ShapeDtypeStruct((N_HEADS, n, HEAD_PAD), F32), jax.ShapeDtypeStruct((N_HEADS, n, V_HEAD), F32),
                   jax.ShapeDtypeStruct((N_HEADS, n, HEAD_PAD), F32)],
        scratch_shapes=[pltpu.VMEM((n, HEAD_PAD), F32), pltpu.VMEM((4, 2, tq, tq), BF16), pltpu.VMEM((2, tk, tq), F32),
                        pltpu.SemaphoreType.DMA(()), pltpu.SemaphoreType.DMA((4,))],
        compiler_params=_params("arbitrary", "arbitrary"),
    )(k, v, q, dpo, lse, delta, m, p)


def _mla_pre_bwd(x, dout, dq, dk, dv, gain, win, gq, gkv, wq, wkv, ghq, ghk, cos, sin_signed, name):
    n = x.shape[0]
    tm = _tiles(n)["mla_bwd"]
    hw = N_HEADS * HEAD_PAD

    def body(x_ref, do_ref, dq_ref, dk_ref, dv_ref, g_ref, win_ref, gq_ref, gkv_ref, wq_ref, wkv_ref, ghq_ref, ghk_ref,
             c_ref, s_ref, dx_ref, h_ref, dlat_ref, cqn_ref, dqp_ref, ckvn_ref, dkv_ref,
             dg_ref, dgq_ref, dgkv_ref, dghq_ref, dghk_ref):
        @pl.when(pl.program_id(0) == 0)
        def _():
            for ref in (dg_ref, dgq_ref, dgkv_ref, dghq_ref, dghk_ref):
                ref[...] = jnp.zeros_like(ref)

        g = g_ref[...]
        xhat, r = _rms_fwd(x_ref[...])
        h = (xhat * g).astype(BF16)
        h_ref[...] = h
        cq, ckv, kpe = _mla_latents(h, win_ref)
        cqhat, rcq = _rms_fwd(cq)
        ckvhat, rckv = _rms_fwd(ckv)
        cqn = (cqhat * gq_ref[...]).astype(BF16)
        ckvn = (ckvhat * gkv_ref[...]).astype(BF16)
        cqn_ref[...] = cqn
        ckvn_ref[...] = ckvn
        cos, sn = c_ref[...], s_ref[...]
        ghq, ghk = ghq_ref[...], ghk_ref[...]
        dkpe = jnp.zeros((tm, 128), F32)
        dghq = jnp.zeros((8, HEAD_PAD), F32)
        dghk = jnp.zeros((8, HEAD_PAD), F32)
        for hd in range(N_HEADS):
            rws = pl.ds(hd * HEAD_PAD, HEAD_PAD)
            lanes = slice(hd * HEAD_PAD, (hd + 1) * HEAD_PAD)
            qhat, rq = _rms_fwd(_dot_nt(cqn, wq_ref[rws, :]), QK_HEAD)
            gqn = _rope_bwd(dq_ref[hd], cos, sn)
            dghq = dghq + _rowsum8(gqn * qhat)
            dqpre = _rms_bwd(gqn, qhat, rq, ghq, QK_HEAD).astype(BF16)
            dqp_ref[:, lanes] = dqpre
            kvh = _dot_nt(ckvn, wkv_ref[rws, :])
            khat, rk = _rms_fwd(jnp.concatenate([kvh[:, :QK_NOPE], kpe], axis=1), QK_HEAD)
            gkn = _rope_bwd(dk_ref[hd], cos, sn)
            dghk = dghk + _rowsum8(gkn * khat)
            dkpre = _rms_bwd(gkn, khat, rk, ghk, QK_HEAD)
            dkpe = dkpe + dkpre[:, QK_NOPE:]
            dkvh = jnp.concatenate([dkpre[:, :QK_NOPE], dv_ref[hd]], axis=1).astype(BF16)
            dkv_ref[:, lanes] = dkvh
        dcqn = _dot(dqp_ref[...], wq_ref[...])
        dckvn = _dot(dkv_ref[...], wkv_ref[...])
        dghq_ref[...] += dghq
        dghk_ref[...] += dghk
        dgq_ref[...] += _rowsum8(dcqn * cqhat)
        dgkv_ref[...] += _rowsum8(dckvn * ckvhat)
        dlat = jnp.concatenate([_rms_bwd(dcqn, cqhat, rcq, gq_ref[...]), _rms_bwd(dckvn, ckvhat, rckv, gkv_ref[...]),
                                dkpe], axis=1).astype(BF16)
        dlat_ref[...] = dlat
        dh = _dot(dlat, win_ref[...])
        dg_ref[...] += _rowsum8(dh * xhat)
        dx_ref[...] = do_ref[...] + _rms_bwd(dh, xhat, r, g)

    row = lambda w: pl.BlockSpec((tm, w), lambda i: (i, 0))
    head = lambda w: pl.BlockSpec((N_HEADS, tm, w), lambda i: (0, i, 0))
    sds = jax.ShapeDtypeStruct
    return pl.pallas_call(
        body, name=name, grid=(n // tm,),
        in_specs=[row(D_MODEL), row(D_MODEL), head(HEAD_PAD), head(HEAD_PAD), head(V_HEAD), _full((1, D_MODEL)),
                  _full((LAT_PAD, D_MODEL)), _full((1, Q_LORA)), _full((1, KV_LORA)), _full((hw, Q_LORA)),
                  _full((hw, KV_LORA)), _full((1, HEAD_PAD)), _full((1, HEAD_PAD)), row(128), row(128)],
        out_specs=[row(D_MODEL), row(D_MODEL), row(LAT_PAD), row(Q_LORA), row(hw), row(KV_LORA), row(hw),
                   _full((8, D_MODEL)), _full((8, Q_LORA)), _full((8, KV_LORA)), _full((8, HEAD_PAD)),
                   _full((8, HEAD_PAD))],
        out_shape=[sds((n, D_MODEL), F32), sds((n, D_MODEL), BF16), sds((n, LAT_PAD), BF16), sds((n, Q_LORA), BF16),
                   sds((n, hw), BF16), sds((n, KV_LORA), BF16), sds((n, hw), BF16), sds((8, D_MODEL), F32),
                   sds((8, Q_LORA), F32), sds((8, KV_LORA), F32), sds((8, HEAD_PAD), F32), sds((8, HEAD_PAD), F32)],
        compiler_params=_params("arbitrary"),
    )(x, dout, dq, dk, dv, gain, win, gq, gkv, wq, wkv, ghq, ghk, cos, sin_signed)


def _adamw(w, g, m, v, behind, name):
    rows, cols = w.shape
    tr = rows
    for cand in (512, 256, 128, 64, 32, 16, 8):
        if rows % cand == 0 and rows > cand:
            tr = cand
            break

    def body(w_ref, g_ref, m_ref, v_ref, behind_ref, d_ref, mo_ref, vo_ref):
        d_ref[...], mo_ref[...], vo_ref[...] = _adamw_step(w_ref[...], g_ref[...], m_ref[...], v_ref[...])

    blk = pl.BlockSpec((tr, cols), lambda i: (i, 0))
    return pl.pallas_call(
        body, name=name, grid=(rows // tr,), in_specs=[blk] * 4 + [pl.BlockSpec(memory_space=pl.ANY)],
        out_specs=[blk] * 3, out_shape=[jax.ShapeDtypeStruct((rows, cols), F32)] * 3,
        compiler_params=_params("arbitrary"),
    )(w, g, m, v, behind)


def _adamw_step(w, g, m, v):
    mn = ADAM_B1 * m + (1.0 - ADAM_B1) * g
    vn = ADAM_B2 * v + (1.0 - ADAM_B2) * (g * g)
    m_hat = mn / (1.0 - ADAM_B1 ** ADAM_STEP)
    v_hat = vn / (1.0 - ADAM_B2 ** ADAM_STEP)
    return -ADAM_LR * (m_hat / (jnp.sqrt(v_hat) + ADAM_EPS) + ADAM_WD * w), mn, vn


def _adamw_layers(w, grads, m, v, behind, name):
    layers, r, c = w.shape
    tr = _row_tile(r, 512)

    def body(w_ref, m_ref, v_ref, *rest):
        g_refs, (d_ref, mo_ref, vo_ref, go_ref) = rest[:layers], rest[layers + 1:]
        layer = pl.program_id(1)
        g = g_refs[0][...]
        for k in range(1, layers):
            g = jnp.where(layer == k, g_refs[k][...], g)
        d_ref[0], mo_ref[0], vo_ref[0] = _adamw_step(w_ref[0], g, m_ref[0], v_ref[0])
        go_ref[0] = g

    stacked = pl.BlockSpec((1, tr, c), lambda i, layer: (layer, i, 0))
    piece = pl.BlockSpec((tr, c), lambda i, layer: (i, 0))
    return pl.pallas_call(
        body, name=name, grid=(r // tr, layers),
        in_specs=[stacked] * 3 + [piece] * layers + [pl.BlockSpec(memory_space=pl.ANY)],
        out_specs=[stacked] * 4, out_shape=[jax.ShapeDtypeStruct((layers, r, c), F32)] * 4,
        compiler_params=_params("arbitrary", "arbitrary"),
    )(w, m, v, *grads, behind)


def _sum_parts(parts, name):
    k, r, c = parts.shape
    tr = min(r, 512)

    def body(p_ref, o_ref):
        acc = p_ref[0]
        for j in range(1, k):
            acc = acc + p_ref[j]
        o_ref[...] = acc

    return pl.pallas_call(
        body, name=name, grid=(r // tr,), in_specs=[pl.BlockSpec((k, tr, c), lambda i: (0, i, 0))],
        out_specs=pl.BlockSpec((tr, c), lambda i: (i, 0)), out_shape=jax.ShapeDtypeStruct((r, c), parts.dtype),
        compiler_params=_params("arbitrary"),
    )(parts)


def _row_tile(r, most=256):
    best = r
    for cand in range(8, most + 1, 8):
        if r % cand == 0:
            best = cand
    return best


def _sum_exchange(mine, landed, me, name):
    _, r, c = mine.shape
    tr = _row_tile(r)

    def body(me_ref, m_ref, l_ref, o_ref):
        acc = m_ref[0]
        for k in range(1, N_DEV):
            acc = acc + l_ref[k]
        o_ref[...] = acc

    return pl.pallas_call(
        body, name=name,
        grid_spec=pltpu.PrefetchScalarGridSpec(
            num_scalar_prefetch=1, grid=(r // tr,),
            in_specs=[pl.BlockSpec((1, tr, c), lambda i, me_ref: (me_ref[0], i, 0)),
                      pl.BlockSpec((N_DEV, tr, c), lambda i, me_ref: (0, i, 0))],
            out_specs=pl.BlockSpec((tr, c), lambda i, me_ref: (i, 0))),
        out_shape=jax.ShapeDtypeStruct((r, c), mine.dtype), compiler_params=_params("arbitrary"),
    )(me, mine, landed)


MESH = pl.DeviceIdType.MESH


def _all_gather(x, name):
    r, c = x.shape

    def body(x_ref, out_ref, send_sems, recv_sems, local_sem):
        mx, my, mc = lax.axis_index("x"), lax.axis_index("y"), lax.axis_index("c")
        me, sibling = (mx, my, mc), (mx, my, 1 - mc)
        chips = [(1 - mx, my), (mx, 1 - my), (1 - mx, 1 - my)]

        def slot(px, py, pc):
            return out_ref.at[4 * px + 2 * py + pc]

        def copy(k, block, to, src=None):
            return pltpu.make_async_remote_copy(
                src_ref=slot(*block) if src is None else src, dst_ref=slot(*block),
                send_sem=send_sems.at[k], recv_sem=recv_sems.at[k], device_id=to, device_id_type=MESH)

        mine = pltpu.make_async_copy(x_ref, slot(*me), local_sem)
        mine.start()
        first = [copy(0, me, sibling, src=x_ref)]
        first += [copy(1 + j, me, (*chip, mc), src=x_ref) for j, chip in enumerate(chips)]
        for cp in first:
            cp.start()
        passed = [copy(4 + j, (*chip, mc), sibling) for j, chip in enumerate(chips)]
        for j, chip in enumerate(chips):
            copy(1 + j, (*chip, mc), me).wait_recv()
            passed[j].start()
        copy(0, sibling, me).wait_recv()
        for j, chip in enumerate(chips):
            copy(4 + j, (*chip, 1 - mc), me).wait_recv()
        for cp in first + passed:
            cp.wait_send()
        mine.wait()

    any_spec = pl.BlockSpec(memory_space=pl.ANY)
    return pl.pallas_call(
        body, name=name, in_specs=[any_spec], out_specs=any_spec,
        out_shape=jax.ShapeDtypeStruct((N_DEV, r, c), x.dtype),
        scratch_shapes=[pltpu.SemaphoreType.DMA((7,)), pltpu.SemaphoreType.DMA((7,)), pltpu.SemaphoreType.DMA(())],
    )(x)


HBM_SPEC = pl.BlockSpec(memory_space=pltpu.HBM)
SEM_SPEC = pl.BlockSpec(memory_space=pltpu.SEMAPHORE)
SPLIT_EFFECT = pltpu.SideEffectType.DATAFLOW_SIDE_EFFECTING


def _exchange_copies(src_ref, land_ref, send_sems, recv_sems, gather):
    mx, my, mc = lax.axis_index("x"), lax.axis_index("y"), lax.axis_index("c")
    me = 4 * mx + 2 * my + mc
    copies = []
    for k in range(1, N_DEV):
        px = 1 - mx if k & 4 else mx
        py = 1 - my if k & 2 else my
        pc = 1 - mc if k & 1 else mc
        src = src_ref if gather else src_ref.at[4 * px + 2 * py + pc]
        dst = land_ref.at[me] if gather else land_ref.at[k]
        copies.append(pltpu.make_async_remote_copy(
            src_ref=src, dst_ref=dst, send_sem=send_sems.at[k - 1], recv_sem=recv_sems.at[k - 1],
            device_id=(px, py, pc), device_id_type=MESH))
    return copies


def _exchange_start(src, after, gather, name):
    land_shape = (N_DEV,) + src.shape[-2:]

    def body(src_ref, land_ref, after_ref, send_sems, recv_sems, src_thru, land_thru, token):
        for cp in _exchange_copies(src_ref, land_ref, send_sems, recv_sems, gather):
            cp.start()
        token[...] = jnp.zeros_like(token)

    return pl.pallas_call(
        body, name=name,
        out_shape=(pltpu.SemaphoreType.DMA((N_DEV - 1,)), pltpu.SemaphoreType.DMA((N_DEV - 1,)),
                   pltpu.HBM(src.shape, src.dtype), pltpu.HBM(land_shape, src.dtype),
                   jax.ShapeDtypeStruct((8, 128), F32)),
        in_specs=(HBM_SPEC, HBM_SPEC, pl.BlockSpec(memory_space=pl.ANY)),
        out_specs=(SEM_SPEC, SEM_SPEC, HBM_SPEC, HBM_SPEC, pl.BlockSpec(memory_space=pltpu.VMEM)),
        input_output_aliases={0: 2, 1: 3},
        compiler_params=pltpu.CompilerParams(has_side_effects=SPLIT_EFFECT),
    )(pltpu.with_memory_space_constraint(src, pltpu.HBM),
      pltpu.with_memory_space_constraint(lax.empty(land_shape, src.dtype), pltpu.HBM), after)


def _exchange_wait(started, after, gather, name):
    send_sems, recv_sems, src_thru, land_thru, _ = started

    def body(src_ref, land_ref, send_sems, recv_sems, after_ref, src_out, land_out):
        for cp in _exchange_copies(src_ref, land_ref, send_sems, recv_sems, gather):
            cp.wait_send()
            cp.wait_recv()

    return pl.pallas_call(
        body, name=name,
        out_shape=(pltpu.HBM(src_thru.shape, src_thru.dtype), pltpu.HBM(land_thru.shape, land_thru.dtype)),
        in_specs=(HBM_SPEC, HBM_SPEC, SEM_SPEC, SEM_SPEC, pl.BlockSpec(memory_space=pl.ANY)),
        out_specs=(HBM_SPEC, HBM_SPEC), input_output_aliases={0: 0, 1: 1},
        compiler_params=pltpu.CompilerParams(has_side_effects=SPLIT_EFFECT),
    )(src_thru, land_thru, send_sems, recv_sems, after)


FFN_ROWS = D_FF // N_DEV
WIN_ROWS = (Q_LORA + KV_LORA + QK_ROPE) // N_DEV
WIN_ROWS_PAD = 144
WQ_ROWS = QK_HEAD * Q_LORA // D_MODEL
WKV_ROWS = 256 * KV_LORA // D_MODEL
WOUT_ROWS = V_HEAD
POOL_ROWS = 4 * 32 * POOL_GROUP // D_MODEL


COLUMN_SHARDED = ("ffn1_w_gate", "ffn1_w_up", "ffn2_w_gate", "ffn2_w_up", "mla_w_in", "mla_w_q_up", "mla_w_kv_up")


def _t(w):
    return jnp.swapaxes(w, -1, -2)


def _ffn_segments(f, i):
    return [(f + "_w_gate", i, FFN_ROWS), (f + "_w_up", i, FFN_ROWS), (f + "_w_down", i, FFN_ROWS)]


def _mixer_segments(i):
    j = i // 2
    if i % 2 == 0:
        return [("pool_w", j, POOL_ROWS)]
    return [("mla_w_in", j, WIN_ROWS_PAD), ("mla_w_q_up", j, WQ_ROWS), ("mla_w_kv_up", j, WKV_ROWS),
            ("mla_w_out", j, WOUT_ROWS)]


def _pack_shards(p, segs, dtype):
    parts = []
    for name, idx, _ in segs:
        w = p[name][idx]
        if name.endswith("w_gate") or name.endswith("w_up"):
            w = _t(w)
        elif name == "mla_w_in":
            w = jnp.pad(_t(w), ((0, WIN_ROWS_PAD - WIN_ROWS), (0, 0)))
        elif name == "mla_w_q_up":
            w = _t(w).reshape(WQ_ROWS, D_MODEL)
        elif name == "mla_w_kv_up":
            w = _t(w).reshape(WKV_ROWS, D_MODEL)
        elif name == "pool_w":
            w = w.reshape(POOL_ROWS, D_MODEL)
        parts.append(w.astype(dtype))
    return jnp.concatenate(parts, axis=0)


def _unpack_gathered(g, segs):
    out = {}
    off = 0
    for name, layer, rows in segs:
        seg = g[:, off:off + rows, :]
        off += rows
        if name.startswith("ffn"):
            out[(name, layer)] = (g, off - rows)
            continue
        if name == "mla_w_in":
            w = seg[:, :WIN_ROWS].reshape(N_DEV * WIN_ROWS, D_MODEL)
            w = jnp.pad(w, ((0, LAT_PAD - N_DEV * WIN_ROWS), (0, 0)))
        elif name == "mla_w_q_up":
            w = seg.reshape(N_HEADS, QK_HEAD, Q_LORA)
            w = jnp.pad(w, ((0, 0), (0, HEAD_PAD - QK_HEAD), (0, 0))).reshape(N_HEADS * HEAD_PAD, Q_LORA)
        elif name == "mla_w_kv_up":
            w = seg.reshape(N_HEADS * 256, KV_LORA)
        elif name == "pool_w":
            w = seg.reshape(N_DEV, 4, 32, POOL_GROUP).transpose(1, 0, 2, 3).reshape(4, POOL_GROUP, POOL_GROUP)
        else:
            w = seg.reshape(N_DEV * rows, D_MODEL)
        out[(name, layer)] = w
    return out


def _pack_grads(gr, segments):
    segs = []
    for name, layer, rows in segments:
        g = gr[(name, layer)]
        if name == "mla_w_in":
            g = g[:N_DEV * WIN_ROWS].reshape(N_DEV, WIN_ROWS, D_MODEL)
            g = jnp.pad(g, ((0, 0), (0, WIN_ROWS_PAD - WIN_ROWS), (0, 0)))
        elif name == "mla_w_q_up":
            g = g.reshape(N_HEADS, HEAD_PAD, Q_LORA)[:, :QK_HEAD].reshape(N_DEV, WQ_ROWS, D_MODEL)
        elif name == "mla_w_kv_up":
            g = g.reshape(N_DEV, WKV_ROWS, D_MODEL)
        elif name == "pool_w":
            g = g.reshape(4, N_DEV, 32, POOL_GROUP).transpose(1, 0, 2, 3).reshape(N_DEV, POOL_ROWS, D_MODEL)
        else:
            g = g.reshape(N_DEV, rows, D_MODEL)
        segs.append(g)
    return jnp.concatenate(segs, axis=1)


def _unpack_shard_grads(flat, segments):
    per = {}
    off = 0
    for name, layer, rows in segments:
        seg = flat[off:off + rows]
        off += rows
        if name == "mla_w_in":
            g = seg[:WIN_ROWS]
        elif name == "mla_w_q_up":
            g = seg.reshape(QK_HEAD, Q_LORA)
        elif name == "mla_w_kv_up":
            g = seg.reshape(256, KV_LORA)
        elif name == "pool_w":
            g = seg.reshape(4, 32, POOL_GROUP)
        else:
            g = seg
        per[(name, layer)] = g
    return per


def _pad_lanes(v, width):
    return jnp.pad(v, ((0, 0), (0, width - v.shape[-1])))


def kernel(x, positions, ffn1_norm, ffn1_w_gate, ffn1_w_up, ffn1_w_down, mix_norm, pool_w, pool_scale, mla_w_in, mla_q_norm, mla_w_q_up, mla_kv_norm, mla_w_kv_up, mla_q_head_norm, mla_k_head_norm, mla_w_out, ffn2_norm, ffn2_w_gate, ffn2_w_up, ffn2_w_down, loss_target, m_ffn1_norm, m_ffn1_w_gate, m_ffn1_w_up, m_ffn1_w_down, m_mix_norm, m_pool_w, m_pool_scale, m_mla_w_in, m_mla_q_norm, m_mla_w_q_up, m_mla_kv_norm, m_mla_w_kv_up, m_mla_q_head_norm, m_mla_k_head_norm, m_mla_w_out, m_ffn2_norm, m_ffn2_w_gate, m_ffn2_w_up, m_ffn2_w_down, v_ffn1_norm, v_ffn1_w_gate, v_ffn1_w_up, v_ffn1_w_down, v_mix_norm, v_pool_w, v_pool_scale, v_mla_w_in, v_mla_q_norm, v_mla_w_q_up, v_mla_kv_norm, v_mla_w_kv_up, v_mla_q_head_norm, v_mla_k_head_norm, v_mla_w_out, v_ffn2_norm, v_ffn2_w_gate, v_ffn2_w_up, v_ffn2_w_down):
    args = (x, positions, ffn1_norm, ffn1_w_gate, ffn1_w_up, ffn1_w_down, mix_norm, pool_w, pool_scale, mla_w_in,
            mla_q_norm, mla_w_q_up, mla_kv_norm, mla_w_kv_up, mla_q_head_norm, mla_k_head_norm, mla_w_out, ffn2_norm,
            ffn2_w_gate, ffn2_w_up, ffn2_w_down)
    p = dict(zip(NAMES, args))
    moments_m = dict(zip(WEIGHTS, (m_ffn1_norm, m_ffn1_w_gate, m_ffn1_w_up, m_ffn1_w_down, m_mix_norm, m_pool_w, m_pool_scale, m_mla_w_in, m_mla_q_norm, m_mla_w_q_up, m_mla_kv_norm, m_mla_w_kv_up, m_mla_q_head_norm, m_mla_k_head_norm, m_mla_w_out, m_ffn2_norm, m_ffn2_w_gate, m_ffn2_w_up, m_ffn2_w_down)))
    moments_v = dict(zip(WEIGHTS, (v_ffn1_norm, v_ffn1_w_gate, v_ffn1_w_up, v_ffn1_w_down, v_mix_norm, v_pool_w, v_pool_scale, v_mla_w_in, v_mla_q_norm, v_mla_w_q_up, v_mla_kv_norm, v_mla_w_kv_up, v_mla_q_head_norm, v_mla_k_head_norm, v_mla_w_out, v_ffn2_norm, v_ffn2_w_gate, v_ffn2_w_up, v_ffn2_w_down)))
    dev = 4 * lax.axis_index("x") + 2 * lax.axis_index("y") + lax.axis_index("c")

    xs = x[0]
    n = xs.shape[0]
    target = loss_target[0]

    inv_freq = 1.0 / (ROPE_THETA ** (jnp.arange(0, QK_ROPE, 2, dtype=F32) / QK_ROPE))
    ang = positions[0].astype(F32)[..., None] * inv_freq
    cos, sin = jnp.cos(ang), jnp.sin(ang)
    zero = jnp.zeros((n, 128 - QK_ROPE), F32)
    rope_cos = jnp.concatenate([cos, cos, zero], axis=1)
    rope_sin = jnp.concatenate([-sin, sin, zero], axis=1)

    ag_groups = [_ffn_segments("ffn1", 0), _mixer_segments(0) + _ffn_segments("ffn2", 0)]
    ag_groups += [_ffn_segments("ffn1", i) + _mixer_segments(i) + _ffn_segments("ffn2", i) for i in range(1, DEPTH)]
    shards = [_pack_shards(p, segs, BF16) for segs in ag_groups]
    w = {}
    no_token = jnp.zeros((8, 128), F32)

    def tied(gain, token):
        return gain + token[0, 0]

    gains_local = jnp.concatenate([_pad_lanes(mla_q_norm, 128), _pad_lanes(mla_kv_norm, 128)], axis=0)
    gains_all = _all_gather(jnp.pad(gains_local, ((0, 4), (0, 0))), "gains_all_gather")
    q_norm_full = gains_all[:, 0:2, :Q_LORA // N_DEV].transpose(1, 0, 2).reshape(2, Q_LORA)
    kv_norm_full = gains_all[:, 2:4, :KV_LORA // N_DEV].transpose(1, 0, 2).reshape(2, KV_LORA)
    ghq = _pad_lanes(mla_q_head_norm, HEAD_PAD)
    ghk = _pad_lanes(mla_k_head_norm, HEAD_PAD)

    def ffn_weights(f, i):
        parts = [w[(f + part, i)] for part in ("_w_gate", "_w_up", "_w_down")]
        return parts[0][0], tuple(off for _, off in parts)

    def mla_weights(j):
        return (w[("mla_w_in", j)], q_norm_full[j:j + 1], kv_norm_full[j:j + 1], w[("mla_w_q_up", j)],
                w[("mla_w_kv_up", j)], ghq[j:j + 1], ghk[j:j + 1], rope_cos, rope_sin)

    saved = []
    cur = xs
    gathers = [_exchange_start(shards[0], gains_all, True, "ag_start_0")]

    def gather_step(after):
        g = len(gathers) - 1
        _, landed = _exchange_wait(gathers[g], after, True, f"ag_wait_{g}")
        w.update(_unpack_gathered(lax.dynamic_update_slice(landed, shards[g][None], (dev, 0, 0)), ag_groups[g]))
        if g + 1 == len(ag_groups):
            return no_token
        gathers.append(_exchange_start(shards[g + 1], landed, True, f"ag_start_{g + 1}"))
        return gathers[-1][4]

    for i in range(DEPTH):
        j = i // 2
        st = {"x0": cur}
        token = gather_step(cur)
        cur, st["a1"], st["u1"] = _ffn_fwd(cur, tied(ffn1_norm[i:i + 1], token), *ffn_weights("ffn1", i),
                                           f"ffn1_fwd_{i}")
        st["x1"] = cur
        if i == 0:
            token = gather_step(cur)
        if i % 2 == 0:
            cur = _pool_fwd(cur, tied(mix_norm[i:i + 1], token), w[("pool_w", j)], pool_scale[j:j + 1],
                            f"pool_fwd_{i}")
        else:
            st["q"], st["k"], st["v"], vt = _mla_pre_fwd(cur, mix_norm[i:i + 1], *mla_weights(j), f"mla_pre_fwd_{i}")
            st["o"], lse, st["m"], st["p"] = _flash_fwd(st["q"], st["k"], vt, f"flash_fwd_{i}")
            tq_bwd = _tiles(n)["bwd_q"]
            st["lse"] = lse.reshape(N_HEADS, n // tq_bwd, 1, tq_bwd)
            cur = _mla_post_fwd(st["o"], cur, w[("mla_w_out", j)], f"mla_post_fwd_{i}")
        st["x2"] = cur
        cur, st["a2"], st["u2"] = _ffn_fwd(cur, ffn2_norm[i:i + 1], *ffn_weights("ffn2", i), f"ffn2_fwd_{i}")
        saved.append(st)

    dcur, sq_err = _loss_head(cur, target, "loss_head")
    loss_part = 0.5 * jnp.sum(sq_err) * (1.0 / D_MODEL)

    gr = {}
    small = {k: [None] * DEPTH for k in ("ffn1_norm", "mix_norm", "ffn2_norm")}
    small.update({k: [None] * (DEPTH // 2) for k in ("pool_scale", "mla_q_norm", "mla_kv_norm", "mla_q_head_norm",
                                                     "mla_k_head_norm")})

    me = jnp.reshape(dev, (1,)).astype(jnp.int32)
    grads = {}
    in_flight = []

    def reduce_start(packed, segs, tag):
        started = _exchange_start(packed, dcur, False, f"rs_start_{tag}")
        reduce_finish(started[4])
        in_flight.append((started, segs, tag))
        return started[4]

    def reduce_finish(after):
        if in_flight:
            started, segs, tag = in_flight.pop()
            mine, landed = _exchange_wait(started, after, False, f"rs_wait_{tag}")
            grads.update(_unpack_shard_grads(_sum_exchange(mine, landed, me, f"rs_sum_{tag}"), segs))

    def ffn_backward(f, i, x_in, a, u, gain, dout, group_rows):
        dx, h, dob, y, da, du, dg = _ffn_bwd(x_in, dout, a, u, gain, *ffn_weights(f, i), f"{f}_bwd_{i}")
        packed = _tn_matmul_packed(da, h, None, 0, group_rows, f"{f}_dgate_{i}")
        packed = _tn_matmul_packed(du, h, packed, 1, group_rows, f"{f}_dup_{i}")
        packed = _tn_matmul_packed(y, dob, packed, 2, group_rows, f"{f}_ddown_{i}")
        small[f + "_norm"][i] = jnp.sum(dg, axis=0)
        return dx, packed

    token = jnp.zeros((8, 128), F32)
    for i in reversed(range(DEPTH)):
        j = i // 2
        st = saved[i]
        mixer_rows = sum(rows for _, _, rows in _mixer_segments(i))
        dcur, packed = ffn_backward("ffn2", i, st["x2"], st["a2"], st["u2"], tied(ffn2_norm[i:i + 1], token), dcur,
                                    3 * FFN_ROWS + mixer_rows)
        if i % 2 == 0:
            dcur, dpw, dsc, dg = _pool_bwd(st["x1"], dcur, mix_norm[i:i + 1], w[("pool_w", j)], pool_scale[j:j + 1],
                                           f"pool_bwd_{i}")
            gr[("pool_w", j)] = dpw
            small["pool_scale"][j] = jnp.sum(dsc, axis=0)
            small["mix_norm"][i] = jnp.sum(dg, axis=0)
        else:
            dob, dpo, delta = _mla_out_bwd(dcur, st["o"], w[("mla_w_out", j)], f"mla_out_bwd_{i}")
            gr[("mla_w_out", j)] = _tn_matmul(st["o"], dob, f"mla_dout_{i}")
            dk, dv, dq = _flash_bwd(st["q"], st["k"], st["v"], dpo, st["lse"], delta, st["p"], st["m"],
                                    f"flash_bwd_{i}")
            (dcur, h, dlat, cqn, dqp, ckvn, dkv, dg, dgq, dgkv, dghq, dghk) = _mla_pre_bwd(
                st["x1"], dcur, dq, dk, dv, mix_norm[i:i + 1], *mla_weights(j), f"mla_pre_bwd_{i}")
            gr[("mla_w_in", j)] = _tn_matmul(dlat, h, f"mla_din_{i}")
            gr[("mla_w_q_up", j)] = _tn_matmul(dqp, cqn, f"mla_dqup_{i}")
            gr[("mla_w_kv_up", j)] = _tn_matmul(dkv, ckvn, f"mla_dkvup_{i}")
            small["mix_norm"][i] = jnp.sum(dg, axis=0)
            small["mla_q_norm"][j] = _pad_lanes(jnp.sum(dgq, axis=0)[None], D_MODEL)[0]
            small["mla_kv_norm"][j] = _pad_lanes(jnp.sum(dgkv, axis=0)[None], D_MODEL)[0]
            small["mla_q_head_norm"][j] = _pad_lanes(jnp.sum(dghq, axis=0)[None], D_MODEL)[0]
            small["mla_k_head_norm"][j] = _pad_lanes(jnp.sum(dghk, axis=0)[None], D_MODEL)[0]
        packed = lax.dynamic_update_slice(packed, _pack_grads(gr, _mixer_segments(i)), (0, 3 * FFN_ROWS, 0))
        token = reduce_start(packed, _ffn_segments("ffn2", i) + _mixer_segments(i), f"a{i}")
        if i > 0:
            dcur, packed = ffn_backward("ffn1", i, st["x0"], st["a1"], st["u1"], tied(ffn1_norm[i:i + 1], token),
                                        dcur, 3 * FFN_ROWS)
            token = reduce_start(packed, _ffn_segments("ffn1", i), f"b{i}")
    st = saved[0]
    dcur, h, dob, y, da, du, dg = _ffn_bwd(st["x0"], dcur, st["a1"], st["u1"], tied(ffn1_norm[0:1], token),
                                           *ffn_weights("ffn1", 0), "ffn1_bwd_0")
    small["ffn1_norm"][0] = jnp.sum(dg, axis=0)
    grad_x = dcur[None]
    small_order = ("ffn1_norm", "mix_norm", "ffn2_norm", "pool_scale", "mla_q_norm", "mla_kv_norm",
                   "mla_q_head_norm", "mla_k_head_norm")
    rows = [r for k in small_order for r in small[k]]
    rows.append(jnp.zeros((D_MODEL,), F32).at[0].set(loss_part))
    rows.append(jnp.zeros((D_MODEL,), F32))
    small_sum = _sum_parts(_all_gather(jnp.stack(rows), "small_all_gather"), "small_sum")
    loss = small_sum[SM_ROWS - 2, 0]
    token = small_sum
    for seg, lhs, rhs in zip(_ffn_segments("ffn1", 0), (da, du, y), (h, h, dob)):
        packed = _tn_matmul_packed(lhs, rhs, None, 0, FFN_ROWS, f"ffn1_d{seg[0][7:]}_0", behind=token)
        token = reduce_start(packed, [seg], f"b0_{seg[0][7:]}")
    last = "ffn1_w_down"

    off = 0
    for k in small_order:
        cnt = len(small[k])
        g = small_sum[off:off + cnt]
        off += cnt
        if k == "mla_q_norm":
            g = lax.dynamic_slice_in_dim(g[:, :Q_LORA], dev * (Q_LORA // N_DEV), Q_LORA // N_DEV, axis=1)
        elif k == "mla_kv_norm":
            g = lax.dynamic_slice_in_dim(g[:, :KV_LORA], dev * (KV_LORA // N_DEV), KV_LORA // N_DEV, axis=1)
        elif k in ("mla_q_head_norm", "mla_k_head_norm"):
            g = g[:, :QK_HEAD]
        grads[k] = g

    deltas, new_m, new_v = {}, {}, {}

    def update(k, behind):
        to_view = _t if k in COLUMN_SHARDED else (lambda a: a)
        shape = to_view(p[k]).shape
        operands = [to_view(a) for a in (p[k], moments_m[k], moments_v[k])]
        if k in grads:
            view = (-1, shape[-1])
            w2, m2, v2 = (a.reshape(view) for a in operands)
            results = (*_adamw(w2, grads[k].reshape(view), m2, v2, behind, "adamw_" + k), grads[k])
        else:
            view = (shape[0], -1, shape[-1])
            pieces = [grads[(k, idx)].reshape(view[1:]) for idx in range(shape[0])]
            w3, m3, v3 = (a.reshape(view) for a in operands)
            results = _adamw_layers(w3, pieces, m3, v3, behind, "adamw_" + k)
        deltas[k], new_m[k], new_v[k], grads[k] = (to_view(a.reshape(shape)) for a in results)
        return results[0]

    done = token
    for k in WEIGHTS:
        if k != last:
            done = update(k, done)
    reduce_finish(done)
    update(last, done)

    return (loss, grad_x, *[grads[k] for k in WEIGHTS], *[deltas[k] for k in WEIGHTS],
            *[new_m[k] for k in WEIGHTS], *[new_v[k] for k in WEIGHTS])
```

```python
import functools

import jax
import jax.numpy as jnp
from jax import lax
from jax.experimental import pallas as pl
from jax.experimental.pallas import tpu as pltpu

F32 = jnp.float32
BF16 = jnp.bfloat16

D_MODEL = 1024
DEPTH = 4
D_FF = 2816
POOL_WINDOWS = (2, 4, 8, 16)
POOL_GROUP = 256
POOL_HALO = 16
N_HEADS = 8
QK_NOPE = 128
QK_ROPE = 64
QK_HEAD = 192
V_HEAD = 128
Q_LORA = 768
KV_LORA = 256
ROPE_THETA = 10000.0
EPS = 1e-6
FFN_HALF = 0.5
ADAM_LR = 0.001
ADAM_B1 = 0.9
ADAM_B2 = 0.999
ADAM_EPS = 1e-08
ADAM_WD = 0.01
ADAM_STEP = 10

N_DEV = 8
HEAD_PAD = 256
LAT_PAD = 1152
VT_ROWS = 144
LOG2_E = 1.4426950408889634
ATTN_SCALE = QK_HEAD ** -0.5
LOGIT_SCALE = ATTN_SCALE * LOG2_E
V7X_VMEM_LIMIT = 56 * 1024 * 1024
FF_CHUNK = 256
SM_ROWS = 24

NAMES = ['x', 'positions', 'ffn1_norm', 'ffn1_w_gate', 'ffn1_w_up', 'ffn1_w_down', 'mix_norm', 'pool_w',
         'pool_scale', 'mla_w_in', 'mla_q_norm', 'mla_w_q_up', 'mla_kv_norm', 'mla_w_kv_up', 'mla_q_head_norm',
         'mla_k_head_norm', 'mla_w_out', 'ffn2_norm', 'ffn2_w_gate', 'ffn2_w_up', 'ffn2_w_down']
WEIGHTS = NAMES[2:]


def _tiles(n):
    return dict(ffn_fwd=min(512, n), ffn_bwd=min(256, n), fwd_k=min(512, n // 2), bwd_q=min(512, n // 2),
                mla_bwd=min(256, n), pool=min(512, n), tn=min(2048, n), rows=min(1024, n))


def _params(*sem):
    return pltpu.CompilerParams(dimension_semantics=sem, vmem_limit_bytes=V7X_VMEM_LIMIT)


def _dot(a, b):
    return jnp.dot(a, b, preferred_element_type=F32)


def _dot_nt(a, b):
    return lax.dot_general(a, b, (((1,), (1,)), ((), ())), preferred_element_type=F32)


def _dot_tn(a, b):
    return lax.dot_general(a, b, (((0,), (0,)), ((), ())), preferred_element_type=F32)


def _rowsum8(v):
    rows, w = v.shape
    return jnp.sum(v.reshape(rows // 8, 8, w), axis=0)


def _sigmoid(a):
    return 1.0 / (1.0 + jnp.exp(-a))


def _lane_sum(v):
    f = v[:, :128]
    for t in range(1, v.shape[1] // 128):
        f = f + v[:, t * 128:(t + 1) * 128]
    hi = f.astype(BF16)
    lo = (f - hi.astype(F32)).astype(BF16)
    return _dot(jnp.concatenate([hi, lo], axis=1), jnp.ones((256, 128), BF16))


def _by_row(v, r):
    return jnp.concatenate([v[:, t * 128:(t + 1) * 128] * r for t in range(v.shape[1] // 128)], axis=1)


def _rms_fwd(x, width=None):
    width = x.shape[-1] if width is None else width
    r = lax.rsqrt(_lane_sum(x * x) * (1.0 / width) + EPS)
    return _by_row(x, r), r


def _rms_bwd(dy, xhat, r, gain, width=None):
    width = xhat.shape[-1] if width is None else width
    t = dy * gain
    return _by_row(t - _by_row(xhat, _lane_sum(t * xhat) * (1.0 / width)), r)


def _full(shape):
    return pl.BlockSpec(shape, lambda *_: (0,) * len(shape))


def _load_weights(gathered, offsets, dsts, sems):
    rows = D_FF // N_DEV
    copies = [pltpu.make_async_copy(gathered.at[j, pl.ds(off, rows), :], dst.at[pl.ds(j * rows, rows), :],
                                    sems.at[N_DEV * k + j])
              for k, (off, dst) in enumerate(zip(offsets, dsts)) for j in range(N_DEV)]
    for cp in copies:
        cp.start()
    for cp in copies:
        cp.wait()


def _ffn_fwd(x, gain, gathered, offsets, name):
    n = x.shape[0]
    tm = _tiles(n)["ffn_fwd"]

    def body(x_ref, g_ref, gathered_ref, out_ref, a_ref, u_ref, wg_v, wu_v, wd_v, sems):
        @pl.when(pl.program_id(0) == 0)
        def _():
            _load_weights(gathered_ref, offsets, (wg_v, wu_v, wd_v), sems)

        xt = x_ref[...]
        xhat, _ = _rms_fwd(xt)
        h = (xhat * g_ref[...]).astype(BF16)
        acc = jnp.zeros((tm, D_MODEL), F32)
        for c in range(D_FF // FF_CHUNK):
            sl = pl.ds(c * FF_CHUNK, FF_CHUNK)
            a = _dot_nt(h, wg_v[sl, :])
            u = _dot_nt(h, wu_v[sl, :])
            a_ref[:, sl] = a.astype(BF16)
            u_ref[:, sl] = u.astype(BF16)
            y = (a * _sigmoid(a) * u).astype(BF16)
            acc = acc + _dot(y, wd_v[sl, :])
        out_ref[...] = xt + FFN_HALF * acc

    any_spec = pl.BlockSpec(memory_space=pl.ANY)
    return pl.pallas_call(
        body, name=name, grid=(n // tm,),
        in_specs=[pl.BlockSpec((tm, D_MODEL), lambda i: (i, 0)), _full((1, D_MODEL)), any_spec],
        out_specs=[pl.BlockSpec((tm, D_MODEL), lambda i: (i, 0)), pl.BlockSpec((tm, D_FF), lambda i: (i, 0)),
                   pl.BlockSpec((tm, D_FF), lambda i: (i, 0))],
        out_shape=[jax.ShapeDtypeStruct((n, D_MODEL), F32), jax.ShapeDtypeStruct((n, D_FF), BF16),
                   jax.ShapeDtypeStruct((n, D_FF), BF16)],
        scratch_shapes=[pltpu.VMEM((D_FF, D_MODEL), BF16)] * 3 + [pltpu.SemaphoreType.DMA((3 * N_DEV,))],
        compiler_params=_params("arbitrary"),
    )(x, gain, gathered)


def _ffn_bwd(x, dout, a, u, gain, gathered, offsets, name):
    n = x.shape[0]
    tm = _tiles(n)["ffn_bwd"]

    def body(x_ref, do_ref, a_ref, u_ref, g_ref, gathered_ref,
             dx_ref, h_ref, dob_ref, y_ref, da_ref, du_ref, dg_ref, wg_v, wu_v, wd_v, sems):
        @pl.when(pl.program_id(0) == 0)
        def _():
            _load_weights(gathered_ref, offsets, (wg_v, wu_v, wd_v), sems)
            dg_ref[...] = jnp.zeros_like(dg_ref)

        xt = x_ref[...]
        g = g_ref[...]
        xhat, r = _rms_fwd(xt)
        h_ref[...] = (xhat * g).astype(BF16)
        dout = do_ref[...]
        dob = (FFN_HALF * dout).astype(BF16)
        dob_ref[...] = dob
        for c in range(D_FF // FF_CHUNK):
            sl = pl.ds(c * FF_CHUNK, FF_CHUNK)
            dy = _dot_nt(dob, wd_v[sl, :])
            av = a_ref[:, sl].astype(F32)
            uv = u_ref[:, sl].astype(F32)
            s = _sigmoid(av)
            silu = av * s
            y_ref[:, sl] = (silu * uv).astype(BF16)
            du_ref[:, sl] = (dy * silu).astype(BF16)
            da_ref[:, sl] = (dy * uv * (s * (1.0 + av * (1.0 - s)))).astype(BF16)
        dh = _dot(da_ref[...], wg_v[...]) + _dot(du_ref[...], wu_v[...])
        dg_ref[...] += _rowsum8(dh * xhat)
        dx_ref[...] = dout + _rms_bwd(dh, xhat, r, g)

    any_spec = pl.BlockSpec(memory_space=pl.ANY)
    row_d = pl.BlockSpec((tm, D_MODEL), lambda i: (i, 0))
    row_f = pl.BlockSpec((tm, D_FF), lambda i: (i, 0))
    return pl.pallas_call(
        body, name=name, grid=(n // tm,),
        in_specs=[row_d, row_d, row_f, row_f, _full((1, D_MODEL)), any_spec],
        out_specs=[row_d, row_d, row_d, row_f, row_f, row_f, _full((8, D_MODEL))],
        out_shape=[jax.ShapeDtypeStruct((n, D_MODEL), F32), jax.ShapeDtypeStruct((n, D_MODEL), BF16),
                   jax.ShapeDtypeStruct((n, D_MODEL), BF16), jax.ShapeDtypeStruct((n, D_FF), BF16),
                   jax.ShapeDtypeStruct((n, D_FF), BF16), jax.ShapeDtypeStruct((n, D_FF), BF16),
                   jax.ShapeDtypeStruct((8, D_MODEL), F32)],
        scratch_shapes=[pltpu.VMEM((D_FF, D_MODEL), BF16)] * 3 + [pltpu.SemaphoreType.DMA((3 * N_DEV,))],
        compiler_params=_params("arbitrary"),
    )(x, dout, a, u, gain, gathered)


def _tn_matmul(a, b, name, behind=None):
    n, fa = a.shape
    db = b.shape[1]
    tk = _tiles(n)["tn"]
    tf = fa // 2 if (fa // 2) % 128 == 0 and fa > 1024 else fa
    behind = jnp.zeros((8, 128), F32) if behind is None else behind

    def body(a_ref, b_ref, behind_ref, o_ref):
        @pl.when(pl.program_id(1) == 0)
        def _():
            o_ref[...] = jnp.zeros_like(o_ref)

        o_ref[...] += _dot_tn(a_ref[...], b_ref[...])

    return pl.pallas_call(
        body, name=name, grid=(fa // tf, n // tk),
        in_specs=[pl.BlockSpec((tk, tf), lambda i, k: (k, i)), pl.BlockSpec((tk, db), lambda i, k: (k, 0)),
                  pl.BlockSpec(memory_space=pl.ANY)],
        out_specs=pl.BlockSpec((tf, db), lambda i, k: (i, 0)),
        out_shape=jax.ShapeDtypeStruct((fa, db), F32),
        compiler_params=_params("arbitrary", "arbitrary"),
    )(a, b, behind)


def _tn_matmul_packed(a, b, packed, slot, group_rows, name, behind=None):
    n, fa = a.shape
    db = b.shape[1]
    tk = _tiles(n)["tn"]
    per = fa // N_DEV
    slabs = N_DEV // 2
    behind = jnp.zeros((8, 128), F32) if behind is None else behind

    def body(a_ref, b_ref, behind_ref, *rest):
        o_ref = rest[-1]

        @pl.when(pl.program_id(1) == 0)
        def _():
            o_ref[...] = jnp.zeros_like(o_ref)

        o_ref[...] += _dot_tn(a_ref[...], b_ref[...]).reshape(slabs, per, db)

    any_spec = pl.BlockSpec(memory_space=pl.ANY)
    return pl.pallas_call(
        body, name=name, grid=(2, n // tk),
        in_specs=[pl.BlockSpec((tk, slabs * per), lambda i, k: (k, i)), pl.BlockSpec((tk, db), lambda i, k: (k, 0)),
                  any_spec] + ([] if packed is None else [any_spec]),
        out_specs=pl.BlockSpec((slabs, per, db), lambda i, k: (i, slot, 0)),
        out_shape=jax.ShapeDtypeStruct((N_DEV, group_rows, db), F32),
        input_output_aliases={} if packed is None else {3: 0},
        compiler_params=_params("arbitrary", "arbitrary"),
    )(a, b, behind, *([] if packed is None else [packed]))


def _loss_head(y, target, name):
    n = y.shape[0]
    tm = _tiles(n)["rows"]

    def body(y_ref, t_ref, d_ref, acc_ref):
        @pl.when(pl.program_id(0) == 0)
        def _():
            acc_ref[...] = jnp.zeros_like(acc_ref)

        d = y_ref[...] - t_ref[...]
        d_ref[...] = d * (1.0 / D_MODEL)
        acc_ref[...] += _rowsum8(d * d)

    row = pl.BlockSpec((tm, D_MODEL), lambda i: (i, 0))
    return pl.pallas_call(
        body, name=name, grid=(n // tm,), in_specs=[row, row], out_specs=[row, _full((8, D_MODEL))],
        out_shape=[jax.ShapeDtypeStruct((n, D_MODEL), F32), jax.ShapeDtypeStruct((8, D_MODEL), F32)],
        compiler_params=_params("arbitrary"),
    )(y, target)


def _window_sum(v, w, rows, forward):
    s = v
    sh = 1
    while sh < w:
        s = s + pltpu.roll(s, (rows - sh) if forward else sh, 0)
        sh *= 2
    return s


def _pool_fwd(x, gain, w, scale, name):
    n = x.shape[0]
    tm = _tiles(n)["pool"]
    hb = tm // POOL_HALO
    rows = tm + POOL_HALO

    def body(x_ref, xh_ref, g_ref, w_ref, sc_ref, out_ref):
        i = pl.program_id(0)
        xt = x_ref[...]
        e = jnp.concatenate([xh_ref[...], xt], axis=0)
        xhat, _ = _rms_fwd(e)
        row = lax.broadcasted_iota(jnp.int32, (rows, 1), 0)
        hn = jnp.where((row >= POOL_HALO) | (i > 0), xhat * g_ref[...], 0.0)
        t_glob = i * tm + row - POOL_HALO
        outs = []
        for gi, win in enumerate(POOL_WINDOWS):
            ug = hn[:, gi * POOL_GROUP:(gi + 1) * POOL_GROUP]
            cnt = jnp.maximum(jnp.minimum(t_glob + 1, win), 1).astype(F32)
            pooled = (_window_sum(ug, win, rows, False) / cnt - ug)[POOL_HALO:]
            outs.append(_dot(pooled.astype(BF16), w_ref[gi]))
        out_ref[...] = xt + jnp.concatenate(outs, axis=1) * sc_ref[...]

    return pl.pallas_call(
        body, name=name, grid=(n // tm,),
        in_specs=[pl.BlockSpec((tm, D_MODEL), lambda i: (i, 0)),
                  pl.BlockSpec((POOL_HALO, D_MODEL), lambda i: (jnp.maximum(i * hb - 1, 0), 0)),
                  _full((1, D_MODEL)), _full((4, POOL_GROUP, POOL_GROUP)), _full((1, D_MODEL))],
        out_specs=pl.BlockSpec((tm, D_MODEL), lambda i: (i, 0)),
        out_shape=jax.ShapeDtypeStruct((n, D_MODEL), F32),
        compiler_params=_params("arbitrary"),
    )(x, x, gain, w, scale)


def _pool_bwd(x, dout, gain, w, scale, name):
    n = x.shape[0]
    tm = _tiles(n)["pool"]
    hb = tm // POOL_HALO
    rows = tm + POOL_HALO
    nt = n // tm

    def body(x_ref, xh_ref, do_ref, doh_ref, g_ref, w_ref, sc_ref, dx_ref, dw_ref, dsc_ref, dg_ref):
        i = pl.program_id(0)

        @pl.when(i == 0)
        def _():
            dw_ref[...] = jnp.zeros_like(dw_ref)
            dsc_ref[...] = jnp.zeros_like(dsc_ref)
            dg_ref[...] = jnp.zeros_like(dg_ref)

        xt = x_ref[...]
        g = g_ref[...]
        e = jnp.concatenate([xh_ref[...], xt], axis=0)
        xhat_e, r_e = _rms_fwd(e)
        row = lax.broadcasted_iota(jnp.int32, (rows, 1), 0)
        hn = jnp.where((row >= POOL_HALO) | (i > 0), xhat_e * g, 0.0)
        t_prev = i * tm + row - POOL_HALO
        t_next = i * tm + row
        dout = do_ref[...]
        dt = jnp.concatenate([dout, doh_ref[...]], axis=0)
        dt = jnp.where((row < tm) | (i < nt - 1), dt, 0.0)
        dyr = dt * sc_ref[...]
        dus, dscs = [], []
        for gi, win in enumerate(POOL_WINDOWS):
            lanes = slice(gi * POOL_GROUP, (gi + 1) * POOL_GROUP)
            ug = hn[:, lanes]
            cnt = jnp.maximum(jnp.minimum(t_prev + 1, win), 1).astype(F32)
            pooled = (_window_sum(ug, win, rows, False) / cnt - ug)[POOL_HALO:].astype(BF16)
            yraw = _dot(pooled, w_ref[gi])
            dscs.append(_rowsum8(dout[:, lanes] * yraw))
            dyr_b = dyr[:, lanes].astype(BF16)
            dw_ref[gi] += _dot_tn(pooled, dyr_b[:tm])
            dpool = _dot_nt(dyr_b, w_ref[gi])
            cnt2 = jnp.minimum(t_next + 1, win).astype(F32)
            dus.append((_window_sum(dpool / cnt2, win, rows, True) - dpool)[:tm])
        dsc_ref[...] += jnp.concatenate(dscs, axis=1)
        dh = jnp.concatenate(dus, axis=1)
        xhat = xhat_e[POOL_HALO:]
        dg_ref[...] += _rowsum8(dh * xhat)
        dx_ref[...] = dout + _rms_bwd(dh, xhat, r_e[POOL_HALO:], g)

    row_d = pl.BlockSpec((tm, D_MODEL), lambda i: (i, 0))
    prev_h = pl.BlockSpec((POOL_HALO, D_MODEL), lambda i: (jnp.maximum(i * hb - 1, 0), 0))
    next_h = pl.BlockSpec((POOL_HALO, D_MODEL), lambda i: (jnp.minimum((i + 1) * hb, n // POOL_HALO - 1), 0))
    return pl.pallas_call(
        body, name=name, grid=(nt,),
        in_specs=[row_d, prev_h, row_d, next_h, _full((1, D_MODEL)), _full((4, POOL_GROUP, POOL_GROUP)),
                  _full((1, D_MODEL))],
        out_specs=[row_d, _full((4, POOL_GROUP, POOL_GROUP)), _full((8, D_MODEL)), _full((8, D_MODEL))],
        out_shape=[jax.ShapeDtypeStruct((n, D_MODEL), F32), jax.ShapeDtypeStruct((4, POOL_GROUP, POOL_GROUP), F32),
                   jax.ShapeDtypeStruct((8, D_MODEL), F32), jax.ShapeDtypeStruct((8, D_MODEL), F32)],
        compiler_params=_params("arbitrary"),
    )(x, x, dout, dout, gain, w, scale)


def _rope(v, cos, sin_signed):
    lo, hi = v[:, :128], v[:, 128:]
    lane = lax.broadcasted_iota(jnp.int32, hi.shape, 1)
    swapped = jnp.where(lane < 32, pltpu.roll(hi, 96, 1), pltpu.roll(hi, 32, 1))
    return jnp.concatenate([lo, hi * cos + swapped * sin_signed], axis=1)


def _rope_bwd(gr, cos, sin_signed):
    lo, hi = gr[:, :128], gr[:, 128:]
    t = hi * sin_signed
    lane = lax.broadcasted_iota(jnp.int32, hi.shape, 1)
    swapped = jnp.where(lane < 32, pltpu.roll(t, 96, 1), pltpu.roll(t, 32, 1))
    return jnp.concatenate([lo, hi * cos + swapped], axis=1)


def _mla_latents(h, win_ref):
    cq = _dot_nt(h, win_ref[0:Q_LORA, :])
    ckv = _dot_nt(h, win_ref[Q_LORA:Q_LORA + KV_LORA, :])
    kpe = _dot_nt(h, win_ref[Q_LORA + KV_LORA:LAT_PAD, :])
    return cq, ckv, kpe


def _mla_pre_fwd(x, gain, win, gq, gkv, wq, wkv, ghq, ghk, cos, sin_signed, name):
    n = x.shape[0]
    tm = _tiles(n)["fwd_k"]

    def body(x_ref, g_ref, win_ref, gq_ref, gkv_ref, wq_ref, wkv_ref, ghq_ref, ghk_ref, c_ref, s_ref,
             q_ref, k_ref, v_ref, vt_ref):
        xhat, _ = _rms_fwd(x_ref[...])
        h = (xhat * g_ref[...]).astype(BF16)
        cq, ckv, kpe = _mla_latents(h, win_ref)
        cqn = (_rms_fwd(cq)[0] * gq_ref[...]).astype(BF16)
        ckvn = (_rms_fwd(ckv)[0] * gkv_ref[...]).astype(BF16)
        cos, sn = c_ref[...], s_ref[...]
        for hd in range(N_HEADS):
            rws = pl.ds(hd * HEAD_PAD, HEAD_PAD)
            qh = _dot_nt(cqn, wq_ref[rws, :])
            qn = _rms_fwd(qh, QK_HEAD)[0] * ghq_ref[...]
            q_ref[hd] = (_rope(qn, cos, sn) * LOGIT_SCALE).astype(BF16)
            kvh = _dot_nt(ckvn, wkv_ref[rws, :])
            kpre = jnp.concatenate([kvh[:, :QK_NOPE], kpe], axis=1)
            kn = _rms_fwd(kpre, QK_HEAD)[0] * ghk_ref[...]
            k_ref[hd] = _rope(kn, cos, sn).astype(BF16)
            vh = kvh[:, QK_NOPE:]
            v_ref[hd] = vh.astype(BF16)
            vt_ref[hd, 0] = jnp.concatenate([vh.T, jnp.ones((VT_ROWS - V_HEAD, tm), F32)], axis=0).astype(BF16)

    row = lambda w: pl.BlockSpec((tm, w), lambda i: (i, 0))
    head = lambda w: pl.BlockSpec((N_HEADS, tm, w), lambda i: (0, i, 0))
    return pl.pallas_call(
        body, name=name, grid=(n // tm,),
        in_specs=[row(D_MODEL), _full((1, D_MODEL)), _full((LAT_PAD, D_MODEL)), _full((1, Q_LORA)),
                  _full((1, KV_LORA)), _full((N_HEADS * HEAD_PAD, Q_LORA)), _full((N_HEADS * HEAD_PAD, KV_LORA)),
                  _full((1, HEAD_PAD)), _full((1, HEAD_PAD)), row(128), row(128)],
        out_specs=[head(HEAD_PAD), head(HEAD_PAD), head(V_HEAD),
                   pl.BlockSpec((N_HEADS, 1, VT_ROWS, tm), lambda i: (0, i, 0, 0))],
        out_shape=[jax.ShapeDtypeStruct((N_HEADS, n, HEAD_PAD), BF16), jax.ShapeDtypeStruct((N_HEADS, n, HEAD_PAD), BF16),
                   jax.ShapeDtypeStruct((N_HEADS, n, V_HEAD), BF16),
                   jax.ShapeDtypeStruct((N_HEADS, n // tm, VT_ROWS, tm), BF16)],
        compiler_params=_params("arbitrary"),
    )(x, gain, win, gq, gkv, wq, wkv, ghq, ghk, cos, sin_signed)


def _flash_fwd(q, k, vt, name):
    n = q.shape[1]
    tk = _tiles(n)["fwd_k"]
    tq = 2 * tk
    nq = n // tq

    def body(q_ref, k_ref, vt_ref, o_ref, lse_ref, m_ref, p_hbm, s_scr, m_scr, acc_scr, p_scr, p_sems):
        h = pl.program_id(0)
        i = pl.program_id(1)

        def scores(j, slot):
            s_scr[slot] = _dot_nt(k_ref[0, pl.ds(pl.multiple_of(j * tk, tk), tk), :], q_ref[0])

        def p_copy(block, pslot):
            return pltpu.make_async_copy(p_scr.at[pslot], p_hbm.at[h, block], p_sems.at[pslot])

        def update(j, slot, pslot, diagonal=None):
            s = s_scr[slot]
            if diagonal is not None:
                krow = lax.broadcasted_iota(jnp.int32, (tk, tq), 0) + diagonal * tk
                qcol = lax.broadcasted_iota(jnp.int32, (tk, tq), 1)
                s = jnp.where(krow <= qcol, s, -jnp.inf)
            m = m_scr[...]
            m_new = jnp.maximum(m, jnp.max(s, axis=0, keepdims=True))
            p = jnp.exp2(s - m_new).astype(BF16)
            p_scr[pslot] = p
            acc_scr[...] = jnp.exp2(m - m_new) * acc_scr[...] + _dot(vt_ref[0, j], p)
            m_scr[...] = m_new
            m_ref[0, 0, j] = m_new

        def kv_pair(jj, diagonal):
            first = 2 * (jj % 2)
            p_copy(0, first).wait()
            p_copy(0, first + 1).wait()
            scores(2 * jj + 1, 1)
            update(2 * jj, 0, first, 0 if diagonal else None)
            if not diagonal:
                scores(2 * jj + 2, 0)
            update(2 * jj + 1, 1, first + 1, 1 if diagonal else None)
            p_copy(i * (i + 1) + 2 * jj, first).start()
            p_copy(i * (i + 1) + 2 * jj + 1, first + 1).start()

        m_scr[...] = jnp.full((1, tq), -jnp.inf, F32)
        acc_scr[...] = jnp.zeros((VT_ROWS, tq), F32)
        @pl.when((h == 0) & (i == 0))
        def _():
            p_scr[...] = jnp.zeros_like(p_scr)
            for pslot in range(4):
                p_copy(nq * (nq + 1) + pslot, pslot).start()

        scores(0, 0)

        def pair(jj, carry):
            kv_pair(jj, False)
            return carry

        lax.fori_loop(0, i, pair, 0)
        kv_pair(i, True)
        l = acc_scr[V_HEAD:V_HEAD + 1, :]
        o_ref[...] = (acc_scr[0:V_HEAD, :] / l).T.astype(BF16)
        lse_ref[0, 0] = m_scr[...] + jnp.log2(l)

        @pl.when((h == N_HEADS - 1) & (i == nq - 1))
        def _():
            for pslot in range(4):
                p_copy(0, pslot).wait()

    return pl.pallas_call(
        body, name=name, grid=(N_HEADS, nq),
        in_specs=[pl.BlockSpec((1, tq, HEAD_PAD), lambda h, i: (h, i, 0)),
                  pl.BlockSpec((1, n, HEAD_PAD), lambda h, i: (h, 0, 0)),
                  pl.BlockSpec((1, n // tk, VT_ROWS, tk), lambda h, i: (h, 0, 0, 0))],
        out_specs=[pl.BlockSpec((tq, V_HEAD), lambda h, i: (i, h)),
                   pl.BlockSpec((1, 1, 1, tq), lambda h, i: (h, i, 0, 0)),
                   pl.BlockSpec((1, 1, 2 * nq, 1, tq), lambda h, i: (h, i, 0, 0, 0)),
                   pl.BlockSpec(memory_space=pl.ANY)],
        out_shape=[jax.ShapeDtypeStruct((n, N_HEADS * V_HEAD), BF16), jax.ShapeDtypeStruct((N_HEADS, nq, 1, tq), F32),
                   jax.ShapeDtypeStruct((N_HEADS, nq, 2 * nq, 1, tq), F32),
                   jax.ShapeDtypeStruct((N_HEADS, nq * (nq + 1) + 4, tk, tq), BF16)],
        scratch_shapes=[pltpu.VMEM((2, tk, tq), F32), pltpu.VMEM((1, tq), F32), pltpu.VMEM((VT_ROWS, tq), F32),
                        pltpu.VMEM((4, tk, tq), BF16), pltpu.SemaphoreType.DMA((4,))],
        compiler_params=_params("arbitrary", "arbitrary"),
    )(q, k, vt)


def _mla_post_fwd(o, x, wout, name):
    n = x.shape[0]
    tm = _tiles(n)["rows"]

    def body(o_ref, x_ref, w_ref, out_ref):
        out_ref[...] = x_ref[...] + _dot(o_ref[...], w_ref[...])

    row = pl.BlockSpec((tm, D_MODEL), lambda i: (i, 0))
    return pl.pallas_call(
        body, name=name, grid=(n // tm,), in_specs=[row, row, _full((D_MODEL, D_MODEL))], out_specs=row,
        out_shape=jax.ShapeDtypeStruct((n, D_MODEL), F32), compiler_params=_params("arbitrary"),
    )(o, x, wout)


def _mla_out_bwd(dout, o, wout, name):
    n = dout.shape[0]
    t = _tiles(n)["bwd_q"]
    nq = n // t

    def body(do_ref, o_ref, w_ref, dob_ref, dpo_ref, dl_ref):
        dob = do_ref[...].astype(BF16)
        dob_ref[...] = dob
        dpo = _dot_nt(dob, w_ref[...])
        dpo_ref[...] = dpo.astype(BF16)
        ov = o_ref[...].astype(F32)
        for hd in range(N_HEADS):
            lanes = slice(hd * V_HEAD, (hd + 1) * V_HEAD)
            prod = dpo[:, lanes] * ov[:, lanes]
            dl_ref[hd, 0] = jnp.sum(prod.T, axis=0, keepdims=True)

    row = pl.BlockSpec((t, D_MODEL), lambda i: (i, 0))
    return pl.pallas_call(
        body, name=name, grid=(nq,), in_specs=[row, row, _full((D_MODEL, D_MODEL))],
        out_specs=[row, row, pl.BlockSpec((N_HEADS, 1, 1, t), lambda i: (0, i, 0, 0))],
        out_shape=[jax.ShapeDtypeStruct((n, D_MODEL), BF16), jax.ShapeDtypeStruct((n, D_MODEL), BF16),
                   jax.ShapeDtypeStruct((N_HEADS, nq, 1, t), F32)],
        compiler_params=_params("arbitrary"),
    )(dout, o, wout)


def _flash_bwd(q, k, v, dpo, lse, delta, p, m, name):
    n = q.shape[1]
    tq = _tiles(n)["bwd_q"]
    tk = 2 * tq
    nk = n // tk
    nqb = n // tq
    assert p.shape[2:] == (tq, tk) and m.shape[1:] == (nk, 2 * nk, 1, tk), "forward blocks are (tq keys, 2 tq queries)"

    def body(k_ref, v_ref, q_ref, do_ref, lse_ref, dl_ref, m_ref, p_hbm, dk_ref, dv_ref, dq_hbm,
             dq_acc, p_scr, dp_scr, sem, p_sems):
        h = pl.program_id(0)
        j = pl.program_id(1)

        @pl.when(j == 0)
        def _():
            dq_acc[...] = jnp.zeros_like(dq_acc)

        dk_ref[...] = jnp.zeros_like(dk_ref)
        dv_ref[...] = jnp.zeros_like(dv_ref)
        npairs = nk - 1 - j

        def block(t):
            return jnp.where(t < 2 * npairs, 2 * j + 2 + t, 2 * j + (t - 2 * npairs))

        def p_copy(i, first, half, head=h, kv_tile=j):
            tile = i // 2
            return pltpu.make_async_copy(
                p_hbm.at[head, pl.ds(tile * (tile + 1) + 2 * kv_tile, 2), :, pl.ds(half * tq, tq)],
                p_scr.at[first + half], p_sems.at[first + half])

        def fetch_pair(t):
            for half in range(2):
                p_copy(block(2 * t + half), 2 * (t % 2), half).start()

        def dp_ahead(i, half):
            dp_scr[half] = _dot_nt(v_ref[0], do_ref[pl.ds(pl.multiple_of(i * tq, tq), tq), :])

        def update(i, first, half):
            rws = pl.ds(pl.multiple_of(i * tq, tq), tq)
            lse_i = lse_ref[0, i]
            lanes = slice(half * tq, (half + 1) * tq)
            pv = jnp.concatenate(
                [p_scr[first + half, b].astype(F32) * jnp.exp2(m_ref[0, i // 2, 2 * j + b][:, lanes] - lse_i)
                 for b in range(2)], axis=0)
            dv_ref[0] += _dot(pv.astype(BF16), do_ref[rws, :])
            ds = (pv * (dp_scr[half] - dl_ref[0, i])).astype(BF16)
            dk_ref[0] += _dot(ds, q_ref[0, rws, :])
            dq_acc[rws, :] += _dot_tn(ds, k_ref[0])

        def q_pair(t, last):
            first = 2 * (t % 2)
            for half in range(2):
                p_copy(0, first, half).wait()
            if not last:
                fetch_pair(t + 1)
            dp_ahead(block(2 * t + 1), 1)
            update(block(2 * t), first, 0)
            if not last:
                dp_ahead(block(2 * t + 2), 0)
            update(block(2 * t + 1), first, 1)

        @pl.when((h == 0) & (j == 0))
        def _():
            fetch_pair(0)

        dp_ahead(block(0), 0)

        def pair(t, carry):
            q_pair(t, False)
            return carry

        lax.fori_loop(0, npairs, pair, 0)
        q_pair(npairs, True)

        @pl.when((h < N_HEADS - 1) | (j < nk - 1))
        def _():
            next_h = jnp.where(j == nk - 1, h + 1, h)
            next_j = jnp.where(j == nk - 1, 0, j + 1)
            first_block = jnp.where(next_j < nk - 1, 2 * next_j + 2, 2 * next_j)
            for half in range(2):
                p_copy(first_block + half, 0, half, next_h, next_j).start()
        dk_ref[...] = dk_ref[...] * (ATTN_SCALE / LOGIT_SCALE)

        @pl.when(j == nk - 1)
        def _():
            dq_acc[...] = dq_acc[...] * ATTN_SCALE
            cp = pltpu.make_async_copy(dq_acc, dq_hbm.at[h], sem)
            cp.start()
            cp.wait()

    resident = dict(pipeline_mode=pl.Buffered(1))
    return pl.pallas_call(
        body, name=name, grid=(N_HEADS, nk),
        in_specs=[pl.BlockSpec((1, tk, HEAD_PAD), lambda h, j: (h, j, 0)),
                  pl.BlockSpec((1, tk, V_HEAD), lambda h, j: (h, j, 0)),
                  pl.BlockSpec((1, n, HEAD_PAD), lambda h, j: (h, 0, 0), **resident),
                  pl.BlockSpec((n, V_HEAD), lambda h, j: (0, h), **resident),
                  pl.BlockSpec((1, nqb, 1, tq), lambda h, j: (h, 0, 0, 0)),
                  pl.BlockSpec((1, nqb, 1, tq), lambda h, j: (h, 0, 0, 0)),
                  pl.BlockSpec((1, nk, 2 * nk, 1, tk), lambda h, j: (h, 0, 0, 0, 0), **resident),
                  pl.BlockSpec(memory_space=pl.ANY)],
        out_specs=[pl.BlockSpec((1, tk, HEAD_PAD), lambda h, j: (h, j, 0)),
                   pl.BlockSpec((1, tk, V_HEAD), lambda h, j: (h, j, 0)),
                   pl.BlockSpec(memory_space=pl.ANY)],
        out_shape=[jax.ShapeDtypeStruct((N_HEADS, n, HEAD_PAD), F32), jax.ShapeDtypeStruct((N_HEADS, n, V_HEAD), F32),
                   jax.ShapeDtypeStruct((N_HEADS, n, HEAD_PAD), F32)],
        scratch_shapes=[pltpu.VMEM((n, HEAD_PAD), F32), pltpu.VMEM((4, 2, tq, tq), BF16), pltpu.VMEM((2, tk, tq), F32),
                        pltpu.SemaphoreType.DMA(()), pltpu.SemaphoreType.DMA((4,))],
        compiler_params=_params("arbitrary", "arbitrary"),
    )(k, v, q, dpo, lse, delta, m, p)


def _mla_pre_bwd(x, dout, dq, dk, dv, gain, win, gq, gkv, wq, wkv, ghq, ghk, cos, sin_signed, name):
    n = x.shape[0]
    tm = _tiles(n)["mla_bwd"]
    hw = N_HEADS * HEAD_PAD

    def body(x_ref, do_ref, dq_ref, dk_ref, dv_ref, g_ref, win_ref, gq_ref, gkv_ref, wq_ref, wkv_ref, ghq_ref, ghk_ref,
             c_ref, s_ref, dx_ref, h_ref, dlat_ref, cqn_ref, dqp_ref, ckvn_ref, dkv_ref,
             dg_ref, dgq_ref, dgkv_ref, dghq_ref, dghk_ref):
        @pl.when(pl.program_id(0) == 0)
        def _():
            for ref in (dg_ref, dgq_ref, dgkv_ref, dghq_ref, dghk_ref):
                ref[...] = jnp.zeros_like(ref)

        g = g_ref[...]
        xhat, r = _rms_fwd(x_ref[...])
        h = (xhat * g).astype(BF16)
        h_ref[...] = h
        cq, ckv, kpe = _mla_latents(h, win_ref)
        cqhat, rcq = _rms_fwd(cq)
        ckvhat, rckv = _rms_fwd(ckv)
        cqn = (cqhat * gq_ref[...]).astype(BF16)
        ckvn = (ckvhat * gkv_ref[...]).astype(BF16)
        cqn_ref[...] = cqn
        ckvn_ref[...] = ckvn
        cos, sn = c_ref[...], s_ref[...]
        ghq, ghk = ghq_ref[...], ghk_ref[...]
        dkpe = jnp.zeros((tm, 128), F32)
        dghq = jnp.zeros((8, HEAD_PAD), F32)
        dghk = jnp.zeros((8, HEAD_PAD), F32)
        for hd in range(N_HEADS):
            rws = pl.ds(hd * HEAD_PAD, HEAD_PAD)
            lanes = slice(hd * HEAD_PAD, (hd + 1) * HEAD_PAD)
            qhat, rq = _rms_fwd(_dot_nt(cqn, wq_ref[rws, :]), QK_HEAD)
            gqn = _rope_bwd(dq_ref[hd], cos, sn)
            dghq = dghq + _rowsum8(gqn * qhat)
            dqpre = _rms_bwd(gqn, qhat, rq, ghq, QK_HEAD).astype(BF16)
            dqp_ref[:, lanes] = dqpre
            kvh = _dot_nt(ckvn, wkv_ref[rws, :])
            khat, rk = _rms_fwd(jnp.concatenate([kvh[:, :QK_NOPE], kpe], axis=1), QK_HEAD)
            gkn = _rope_bwd(dk_ref[hd], cos, sn)
            dghk = dghk + _rowsum8(gkn * khat)
            dkpre = _rms_bwd(gkn, khat, rk, ghk, QK_HEAD)
            dkpe = dkpe + dkpre[:, QK_NOPE:]
            dkvh = jnp.concatenate([dkpre[:, :QK_NOPE], dv_ref[hd]], axis=1).astype(BF16)
            dkv_ref[:, lanes] = dkvh
        dcqn = _dot(dqp_ref[...], wq_ref[...])
        dckvn = _dot(dkv_ref[...], wkv_ref[...])
        dghq_ref[...] += dghq
        dghk_ref[...] += dghk
        dgq_ref[...] += _rowsum8(dcqn * cqhat)
        dgkv_ref[...] += _rowsum8(dckvn * ckvhat)
        dlat = jnp.concatenate([_rms_bwd(dcqn, cqhat, rcq, gq_ref[...]), _rms_bwd(dckvn, ckvhat, rckv, gkv_ref[...]),
                                dkpe], axis=1).astype(BF16)
        dlat_ref[...] = dlat
        dh = _dot(dlat, win_ref[...])
        dg_ref[...] += _rowsum8(dh * xhat)
        dx_ref[...] = do_ref[...] + _rms_bwd(dh, xhat, r, g)

    row = lambda w: pl.BlockSpec((tm, w), lambda i: (i, 0))
    head = lambda w: pl.BlockSpec((N_HEADS, tm, w), lambda i: (0, i, 0))
    sds = jax.ShapeDtypeStruct
    return pl.pallas_call(
        body, name=name, grid=(n // tm,),
        in_specs=[row(D_MODEL), row(D_MODEL), head(HEAD_PAD), head(HEAD_PAD), head(V_HEAD), _full((1, D_MODEL)),
                  _full((LAT_PAD, D_MODEL)), _full((1, Q_LORA)), _full((1, KV_LORA)), _full((hw, Q_LORA)),
                  _full((hw, KV_LORA)), _full((1, HEAD_PAD)), _full((1, HEAD_PAD)), row(128), row(128)],
        out_specs=[row(D_MODEL), row(D_MODEL), row(LAT_PAD), row(Q_LORA), row(hw), row(KV_LORA), row(hw),
                   _full((8, D_MODEL)), _full((8, Q_LORA)), _full((8, KV_LORA)), _full((8, HEAD_PAD)),
                   _full((8, HEAD_PAD))],
        out_shape=[sds((n, D_MODEL), F32), sds((n, D_MODEL), BF16), sds((n, LAT_PAD), BF16), sds((n, Q_LORA), BF16),
                   sds((n, hw), BF16), sds((n, KV_LORA), BF16), sds((n, hw), BF16), sds((8, D_MODEL), F32),
                   sds((8, Q_LORA), F32), sds((8, KV_LORA), F32), sds((8, HEAD_PAD), F32), sds((8, HEAD_PAD), F32)],
        compiler_params=_params("arbitrary"),
    )(x, dout, dq, dk, dv, gain, win, gq, gkv, wq, wkv, ghq, ghk, cos, sin_signed)


def _adamw(w, g, m, v, behind, name):
    rows, cols = w.shape
    tr = rows
    for cand in (512, 256, 128, 64, 32, 16, 8):
        if rows % cand == 0 and rows > cand:
            tr = cand
            break

    def body(w_ref, g_ref, m_ref, v_ref, behind_ref, d_ref, mo_ref, vo_ref):
        d_ref[...], mo_ref[...], vo_ref[...] = _adamw_step(w_ref[...], g_ref[...], m_ref[...], v_ref[...])

    blk = pl.BlockSpec((tr, cols), lambda i: (i, 0))
    return pl.pallas_call(
        body, name=name, grid=(rows // tr,), in_specs=[blk] * 4 + [pl.BlockSpec(memory_space=pl.ANY)],
        out_specs=[blk] * 3, out_shape=[jax.ShapeDtypeStruct((rows, cols), F32)] * 3,
        compiler_params=_params("arbitrary"),
    )(w, g, m, v, behind)


def _adamw_step(w, g, m, v):
    mn = ADAM_B1 * m + (1.0 - ADAM_B1) * g
    vn = ADAM_B2 * v + (1.0 - ADAM_B2) * (g * g)
    m_hat = mn / (1.0 - ADAM_B1 ** ADAM_STEP)
    v_hat = vn / (1.0 - ADAM_B2 ** ADAM_STEP)
    return -ADAM_LR * (m_hat / (jnp.sqrt(v_hat) + ADAM_EPS) + ADAM_WD * w), mn, vn


def _adamw_layers(w, grads, m, v, behind, name):
    layers, r, c = w.shape
    tr = _row_tile(r, 512)

    def body(w_ref, m_ref, v_ref, *rest):
        g_refs, (d_ref, mo_ref, vo_ref, go_ref) = rest[:layers], rest[layers + 1:]
        layer = pl.program_id(1)
        g = g_refs[0][...]
        for k in range(1, layers):
            g = jnp.where(layer == k, g_refs[k][...], g)
        d_ref[0], mo_ref[0], vo_ref[0] = _adamw_step(w_ref[0], g, m_ref[0], v_ref[0])
        go_ref[0] = g

    stacked = pl.BlockSpec((1, tr, c), lambda i, layer: (layer, i, 0))
    piece = pl.BlockSpec((tr, c), lambda i, layer: (i, 0))
    return pl.pallas_call(
        body, name=name, grid=(r // tr, layers),
        in_specs=[stacked] * 3 + [piece] * layers + [pl.BlockSpec(memory_space=pl.ANY)],
        out_specs=[stacked] * 4, out_shape=[jax.ShapeDtypeStruct((layers, r, c), F32)] * 4,
        compiler_params=_params("arbitrary", "arbitrary"),
    )(w, m, v, *grads, behind)


def _sum_parts(parts, name):
    k, r, c = parts.shape
    tr = min(r, 512)

    def body(p_ref, o_ref):
        acc = p_ref[0]
        for j in range(1, k):
            acc = acc + p_ref[j]
        o_ref[...] = acc

    return pl.pallas_call(
        body, name=name, grid=(r // tr,), in_specs=[pl.BlockSpec((k, tr, c), lambda i: (0, i, 0))],
        out_specs=pl.BlockSpec((tr, c), lambda i: (i, 0)), out_shape=jax.ShapeDtypeStruct((r, c), parts.dtype),
        compiler_params=_params("arbitrary"),
    )(parts)


def _row_tile(r, most=256):
    best = r
    for cand in range(8, most + 1, 8):
        if r % cand == 0:
            best = cand
    return best


def _sum_exchange(mine, landed, me, name):
    _, r, c = mine.shape
    tr = _row_tile(r)

    def body(me_ref, m_ref, l_ref, o_ref):
        acc = m_ref[0]
        for k in range(1, N_DEV):
            acc = acc + l_ref[k]
        o_ref[...] = acc

    return pl.pallas_call(
        body, name=name,
        grid_spec=pltpu.PrefetchScalarGridSpec(
            num_scalar_prefetch=1, grid=(r // tr,),
            in_specs=[pl.BlockSpec((1, tr, c), lambda i, me_ref: (me_ref[0], i, 0)),
                      pl.BlockSpec((N_DEV, tr, c), lambda i, me_ref: (0, i, 0))],
            out_specs=pl.BlockSpec((tr, c), lambda i, me_ref: (i, 0))),
        out_shape=jax.ShapeDtypeStruct((r, c), mine.dtype), compiler_params=_params("arbitrary"),
    )(me, mine, landed)


MESH = pl.DeviceIdType.MESH


def _all_gather(x, name):
    r, c = x.shape

    def body(x_ref, out_ref, send_sems, recv_sems, local_sem):
        mx, my, mc = lax.axis_index("x"), lax.axis_index("y"), lax.axis_index("c")
        me, sibling = (mx, my, mc), (mx, my, 1 - mc)
        chips = [(1 - mx, my), (mx, 1 - my), (1 - mx, 1 - my)]

        def slot(px, py, pc):
            return out_ref.at[4 * px + 2 * py + pc]

        def copy(k, block, to, src=None):
            return pltpu.make_async_remote_copy(
                src_ref=slot(*block) if src is None else src, dst_ref=slot(*block),
                send_sem=send_sems.at[k], recv_sem=recv_sems.at[k], device_id=to, device_id_type=MESH)

        mine = pltpu.make_async_copy(x_ref, slot(*me), local_sem)
        mine.start()
        first = [copy(0, me, sibling, src=x_ref)]
        first += [copy(1 + j, me, (*chip, mc), src=x_ref) for j, chip in enumerate(chips)]
        for cp in first:
            cp.start()
        passed = [copy(4 + j, (*chip, mc), sibling) for j, chip in enumerate(chips)]
        for j, chip in enumerate(chips):
            copy(1 + j, (*chip, mc), me).wait_recv()
            passed[j].start()
        copy(0, sibling, me).wait_recv()
        for j, chip in enumerate(chips):
            copy(4 + j, (*chip, 1 - mc), me).wait_recv()
        for cp in first + passed:
            cp.wait_send()
        mine.wait()

    any_spec = pl.BlockSpec(memory_space=pl.ANY)
    return pl.pallas_call(
        body, name=name, in_specs=[any_spec], out_specs=any_spec,
        out_shape=jax.ShapeDtypeStruct((N_DEV, r, c), x.dtype),
        scratch_shapes=[pltpu.SemaphoreType.DMA((7,)), pltpu.SemaphoreType.DMA((7,)), pltpu.SemaphoreType.DMA(())],
    )(x)


HBM_SPEC = pl.BlockSpec(memory_space=pltpu.HBM)
SEM_SPEC = pl.BlockSpec(memory_space=pltpu.SEMAPHORE)
SPLIT_EFFECT = pltpu.SideEffectType.DATAFLOW_SIDE_EFFECTING


def _exchange_copies(src_ref, land_ref, send_sems, recv_sems, gather):
    mx, my, mc = lax.axis_index("x"), lax.axis_index("y"), lax.axis_index("c")
    me = 4 * mx + 2 * my + mc
    copies = []
    for k in range(1, N_DEV):
        px = 1 - mx if k & 4 else mx
        py = 1 - my if k & 2 else my
        pc = 1 - mc if k & 1 else mc
        src = src_ref if gather else src_ref.at[4 * px + 2 * py + pc]
        dst = land_ref.at[me] if gather else land_ref.at[k]
        copies.append(pltpu.make_async_remote_copy(
            src_ref=src, dst_ref=dst, send_sem=send_sems.at[k - 1], recv_sem=recv_sems.at[k - 1],
            device_id=(px, py, pc), device_id_type=MESH))
    return copies


def _exchange_start(src, after, gather, name):
    land_shape = (N_DEV,) + src.shape[-2:]

    def body(src_ref, land_ref, after_ref, send_sems, recv_sems, src_thru, land_thru, token):
        for cp in _exchange_copies(src_ref, land_ref, send_sems, recv_sems, gather):
            cp.start()
        token[...] = jnp.zeros_like(token)

    return pl.pallas_call(
        body, name=name,
        out_shape=(pltpu.SemaphoreType.DMA((N_DEV - 1,)), pltpu.SemaphoreType.DMA((N_DEV - 1,)),
                   pltpu.HBM(src.shape, src.dtype), pltpu.HBM(land_shape, src.dtype),
                   jax.ShapeDtypeStruct((8, 128), F32)),
        in_specs=(HBM_SPEC, HBM_SPEC, pl.BlockSpec(memory_space=pl.ANY)),
        out_specs=(SEM_SPEC, SEM_SPEC, HBM_SPEC, HBM_SPEC, pl.BlockSpec(memory_space=pltpu.VMEM)),
        input_output_aliases={0: 2, 1: 3},
        compiler_params=pltpu.CompilerParams(has_side_effects=SPLIT_EFFECT),
    )(pltpu.with_memory_space_constraint(src, pltpu.HBM),
      pltpu.with_memory_space_constraint(lax.empty(land_shape, src.dtype), pltpu.HBM), after)


def _exchange_wait(started, after, gather, name):
    send_sems, recv_sems, src_thru, land_thru, _ = started

    def body(src_ref, land_ref, send_sems, recv_sems, after_ref, src_out, land_out):
        for cp in _exchange_copies(src_ref, land_ref, send_sems, recv_sems, gather):
            cp.wait_send()
            cp.wait_recv()

    return pl.pallas_call(
        body, name=name,
        out_shape=(pltpu.HBM(src_thru.shape, src_thru.dtype), pltpu.HBM(land_thru.shape, land_thru.dtype)),
        in_specs=(HBM_SPEC, HBM_SPEC, SEM_SPEC, SEM_SPEC, pl.BlockSpec(memory_space=pl.ANY)),
        out_specs=(HBM_SPEC, HBM_SPEC), input_output_aliases={0: 0, 1: 1},
        compiler_params=pltpu.CompilerParams(has_side_effects=SPLIT_EFFECT),
    )(src_thru, land_thru, send_sems, recv_sems, after)


FFN_ROWS = D_FF // N_DEV
WIN_ROWS = (Q_LORA + KV_LORA + QK_ROPE) // N_DEV
WIN_ROWS_PAD = 144
WQ_ROWS = QK_HEAD * Q_LORA // D_MODEL
WKV_ROWS = 256 * KV_LORA // D_MODEL
WOUT_ROWS = V_HEAD
POOL_ROWS = 4 * 32 * POOL_GROUP // D_MODEL


COLUMN_SHARDED = ("ffn1_w_gate", "ffn1_w_up", "ffn2_w_gate", "ffn2_w_up", "mla_w_in", "mla_w_q_up", "mla_w_kv_up")


def _t(w):
    return jnp.swapaxes(w, -1, -2)


def _ffn_segments(f, i):
    return [(f + "_w_gate", i, FFN_ROWS), (f + "_w_up", i, FFN_ROWS), (f + "_w_down", i, FFN_ROWS)]


def _mixer_segments(i):
    j = i // 2
    if i % 2 == 0:
        return [("pool_w", j, POOL_ROWS)]
    return [("mla_w_in", j, WIN_ROWS_PAD), ("mla_w_q_up", j, WQ_ROWS), ("mla_w_kv_up", j, WKV_ROWS),
            ("mla_w_out", j, WOUT_ROWS)]


def _pack_shards(p, segs, dtype):
    parts = []
    for name, idx, _ in segs:
        w = p[name][idx]
        if name.endswith("w_gate") or name.endswith("w_up"):
            w = _t(w)
        elif name == "mla_w_in":
            w = jnp.pad(_t(w), ((0, WIN_ROWS_PAD - WIN_ROWS), (0, 0)))
        elif name == "mla_w_q_up":
            w = _t(w).reshape(WQ_ROWS, D_MODEL)
        elif name == "mla_w_kv_up":
            w = _t(w).reshape(WKV_ROWS, D_MODEL)
        elif name == "pool_w":
            w = w.reshape(POOL_ROWS, D_MODEL)
        parts.append(w.astype(dtype))
    return jnp.concatenate(parts, axis=0)


def _unpack_gathered(g, segs):
    out = {}
    off = 0
    for name, layer, rows in segs:
        seg = g[:, off:off + rows, :]
        off += rows
        if name.startswith("ffn"):
            out[(name, layer)] = (g, off - rows)
            continue
        if name == "mla_w_in":
            w = seg[:, :WIN_ROWS].reshape(N_DEV * WIN_ROWS, D_MODEL)
            w = jnp.pad(w, ((0, LAT_PAD - N_DEV * WIN_ROWS), (0, 0)))
        elif name == "mla_w_q_up":
            w = seg.reshape(N_HEADS, QK_HEAD, Q_LORA)
            w = jnp.pad(w, ((0, 0), (0, HEAD_PAD - QK_HEAD), (0, 0))).reshape(N_HEADS * HEAD_PAD, Q_LORA)
        elif name == "mla_w_kv_up":
            w = seg.reshape(N_HEADS * 256, KV_LORA)
        elif name == "pool_w":
            w = seg.reshape(N_DEV, 4, 32, POOL_GROUP).transpose(1, 0, 2, 3).reshape(4, POOL_GROUP, POOL_GROUP)
        else:
            w = seg.reshape(N_DEV * rows, D_MODEL)
        out[(name, layer)] = w
    return out


def _pack_grads(gr, segments):
    segs = []
    for name, layer, rows in segments:
        g = gr[(name, layer)]
        if name == "mla_w_in":
            g = g[:N_DEV * WIN_ROWS].reshape(N_DEV, WIN_ROWS, D_MODEL)
            g = jnp.pad(g, ((0, 0), (0, WIN_ROWS_PAD - WIN_ROWS), (0, 0)))
        elif name == "mla_w_q_up":
            g = g.reshape(N_HEADS, HEAD_PAD, Q_LORA)[:, :QK_HEAD].reshape(N_DEV, WQ_ROWS, D_MODEL)
        elif name == "mla_w_kv_up":
            g = g.reshape(N_DEV, WKV_ROWS, D_MODEL)
        elif name == "pool_w":
            g = g.reshape(4, N_DEV, 32, POOL_GROUP).transpose(1, 0, 2, 3).reshape(N_DEV, POOL_ROWS, D_MODEL)
        else:
            g = g.reshape(N_DEV, rows, D_MODEL)
        segs.append(g)
    return jnp.concatenate(segs, axis=1)


def _unpack_shard_grads(flat, segments):
    per = {}
    off = 0
    for name, layer, rows in segments:
        seg = flat[off:off + rows]
        off += rows
        if name == "mla_w_in":
            g = seg[:WIN_ROWS]
        elif name == "mla_w_q_up":
            g = seg.reshape(QK_HEAD, Q_LORA)
        elif name == "mla_w_kv_up":
            g = seg.reshape(256, KV_LORA)
        elif name == "pool_w":
            g = seg.reshape(4, 32, POOL_GROUP)
        else:
            g = seg
        per[(name, layer)] = g
    return per


def _pad_lanes(v, width):
    return jnp.pad(v, ((0, 0), (0, width - v.shape[-1])))


def kernel(x, positions, ffn1_norm, ffn1_w_gate, ffn1_w_up, ffn1_w_down, mix_norm, pool_w, pool_scale, mla_w_in, mla_q_norm, mla_w_q_up, mla_kv_norm, mla_w_kv_up, mla_q_head_norm, mla_k_head_norm, mla_w_out, ffn2_norm, ffn2_w_gate, ffn2_w_up, ffn2_w_down, loss_target, m_ffn1_norm, m_ffn1_w_gate, m_ffn1_w_up, m_ffn1_w_down, m_mix_norm, m_pool_w, m_pool_scale, m_mla_w_in, m_mla_q_norm, m_mla_w_q_up, m_mla_kv_norm, m_mla_w_kv_up, m_mla_q_head_norm, m_mla_k_head_norm, m_mla_w_out, m_ffn2_norm, m_ffn2_w_gate, m_ffn2_w_up, m_ffn2_w_down, v_ffn1_norm, v_ffn1_w_gate, v_ffn1_w_up, v_ffn1_w_down, v_mix_norm, v_pool_w, v_pool_scale, v_mla_w_in, v_mla_q_norm, v_mla_w_q_up, v_mla_kv_norm, v_mla_w_kv_up, v_mla_q_head_norm, v_mla_k_head_norm, v_mla_w_out, v_ffn2_norm, v_ffn2_w_gate, v_ffn2_w_up, v_ffn2_w_down):
    args = (x, positions, ffn1_norm, ffn1_w_gate, ffn1_w_up, ffn1_w_down, mix_norm, pool_w, pool_scale, mla_w_in,
            mla_q_norm, mla_w_q_up, mla_kv_norm, mla_w_kv_up, mla_q_head_norm, mla_k_head_norm, mla_w_out, ffn2_norm,
            ffn2_w_gate, ffn2_w_up, ffn2_w_down)
    p = dict(zip(NAMES, args))
    moments_m = dict(zip(WEIGHTS, (m_ffn1_norm, m_ffn1_w_gate, m_ffn1_w_up, m_ffn1_w_down, m_mix_norm, m_pool_w, m_pool_scale, m_mla_w_in, m_mla_q_norm, m_mla_w_q_up, m_mla_kv_norm, m_mla_w_kv_up, m_mla_q_head_norm, m_mla_k_head_norm, m_mla_w_out, m_ffn2_norm, m_ffn2_w_gate, m_ffn2_w_up, m_ffn2_w_down)))
    moments_v = dict(zip(WEIGHTS, (v_ffn1_norm, v_ffn1_w_gate, v_ffn1_w_up, v_ffn1_w_down, v_mix_norm, v_pool_w, v_pool_scale, v_mla_w_in, v_mla_q_norm, v_mla_w_q_up, v_mla_kv_norm, v_mla_w_kv_up, v_mla_q_head_norm, v_mla_k_head_norm, v_mla_w_out, v_ffn2_norm, v_ffn2_w_gate, v_ffn2_w_up, v_ffn2_w_down)))
    dev = 4 * lax.axis_index("x") + 2 * lax.axis_index("y") + lax.axis_index("c")

    xs = x[0]
    n = xs.shape[0]
    target = loss_target[0]

    inv_freq = 1.0 / (ROPE_THETA ** (jnp.arange(0, QK_ROPE, 2, dtype=F32) / QK_ROPE))
    ang = positions[0].astype(F32)[..., None] * inv_freq
    cos, sin = jnp.cos(ang), jnp.sin(ang)
    zero = jnp.zeros((n, 128 - QK_ROPE), F32)
    rope_cos = jnp.concatenate([cos, cos, zero], axis=1)
    rope_sin = jnp.concatenate([-sin, sin, zero], axis=1)

    ag_groups = [_ffn_segments("ffn1", 0), _mixer_segments(0) + _ffn_segments("ffn2", 0)]
    ag_groups += [_ffn_segments("ffn1", i) + _mixer_segments(i) + _ffn_segments("ffn2", i) for i in range(1, DEPTH)]
    shards = [_pack_shards(p, segs, BF16) for segs in ag_groups]
    w = {}
    no_token = jnp.zeros((8, 128), F32)

    def tied(gain, token):
        return gain + token[0, 0]

    gains_local = jnp.concatenate([_pad_lanes(mla_q_norm, 128), _pad_lanes(mla_kv_norm, 128)], axis=0)
    gains_all = _all_gather(jnp.pad(gains_local, ((0, 4), (0, 0))), "gains_all_gather")
    q_norm_full = gains_all[:, 0:2, :Q_LORA // N_DEV].transpose(1, 0, 2).reshape(2, Q_LORA)
    kv_norm_full = gains_all[:, 2:4, :KV_LORA // N_DEV].transpose(1, 0, 2).reshape(2, KV_LORA)
    ghq = _pad_lanes(mla_q_head_norm, HEAD_PAD)
    ghk = _pad_lanes(mla_k_head_norm, HEAD_PAD)

    def ffn_weights(f, i):
        parts = [w[(f + part, i)] for part in ("_w_gate", "_w_up", "_w_down")]
        return parts[0][0], tuple(off for _, off in parts)

    def mla_weights(j):
        return (w[("mla_w_in", j)], q_norm_full[j:j + 1], kv_norm_full[j:j + 1], w[("mla_w_q_up", j)],
                w[("mla_w_kv_up", j)], ghq[j:j + 1], ghk[j:j + 1], rope_cos, rope_sin)

    saved = []
    cur = xs
    gathers = [_exchange_start(shards[0], gains_all, True, "ag_start_0")]

    def gather_step(after):
        g = len(gathers) - 1
        _, landed = _exchange_wait(gathers[g], after, True, f"ag_wait_{g}")
        w.update(_unpack_gathered(lax.dynamic_update_slice(landed, shards[g][None], (dev, 0, 0)), ag_groups[g]))
        if g + 1 == len(ag_groups):
            return no_token
        gathers.append(_exchange_start(shards[g + 1], landed, True, f"ag_start_{g + 1}"))
        return gathers[-1][4]

    for i in range(DEPTH):
        j = i // 2
        st = {"x0": cur}
        token = gather_step(cur)
        cur, st["a1"], st["u1"] = _ffn_fwd(cur, tied(ffn1_norm[i:i + 1], token), *ffn_weights("ffn1", i),
                                           f"ffn1_fwd_{i}")
        st["x1"] = cur
        if i == 0:
            token = gather_step(cur)
        if i % 2 == 0:
            cur = _pool_fwd(cur, tied(mix_norm[i:i + 1], token), w[("pool_w", j)], pool_scale[j:j + 1],
                            f"pool_fwd_{i}")
        else:
            st["q"], st["k"], st["v"], vt = _mla_pre_fwd(cur, mix_norm[i:i + 1], *mla_weights(j), f"mla_pre_fwd_{i}")
            st["o"], lse, st["m"], st["p"] = _flash_fwd(st["q"], st["k"], vt, f"flash_fwd_{i}")
            tq_bwd = _tiles(n)["bwd_q"]
            st["lse"] = lse.reshape(N_HEADS, n // tq_bwd, 1, tq_bwd)
            cur = _mla_post_fwd(st["o"], cur, w[("mla_w_out", j)], f"mla_post_fwd_{i}")
        st["x2"] = cur
        cur, st["a2"], st["u2"] = _ffn_fwd(cur, ffn2_norm[i:i + 1], *ffn_weights("ffn2", i), f"ffn2_fwd_{i}")
        saved.append(st)

    dcur, sq_err = _loss_head(cur, target, "loss_head")
    loss_part = 0.5 * jnp.sum(sq_err) * (1.0 / D_MODEL)

    gr = {}
    small = {k: [None] * DEPTH for k in ("ffn1_norm", "mix_norm", "ffn2_norm")}
    small.update({k: [None] * (DEPTH // 2) for k in ("pool_scale", "mla_q_norm", "mla_kv_norm", "mla_q_head_norm",
                                                     "mla_k_head_norm")})

    me = jnp.reshape(dev, (1,)).astype(jnp.int32)
    grads = {}
    in_flight = []

    def reduce_start(packed, segs, tag):
        started = _exchange_start(packed, dcur, False, f"rs_start_{tag}")
        reduce_finish(started[4])
        in_flight.append((started, segs, tag))
        return started[4]

    def reduce_finish(after):
        if in_flight:
            started, segs, tag = in_flight.pop()
            mine, landed = _exchange_wait(started, after, False, f"rs_wait_{tag}")
            grads.update(_unpack_shard_grads(_sum_exchange(mine, landed, me, f"rs_sum_{tag}"), segs))

    def ffn_backward(f, i, x_in, a, u, gain, dout, group_rows):
        dx, h, dob, y, da, du, dg = _ffn_bwd(x_in, dout, a, u, gain, *ffn_weights(f, i), f"{f}_bwd_{i}")
        packed = _tn_matmul_packed(da, h, None, 0, group_rows, f"{f}_dgate_{i}")
        packed = _tn_matmul_packed(du, h, packed, 1, group_rows, f"{f}_dup_{i}")
        packed = _tn_matmul_packed(y, dob, packed, 2, group_rows, f"{f}_ddown_{i}")
        small[f + "_norm"][i] = jnp.sum(dg, axis=0)
        return dx, packed

    token = jnp.zeros((8, 128), F32)
    for i in reversed(range(DEPTH)):
        j = i // 2
        st = saved[i]
        mixer_rows = sum(rows for _, _, rows in _mixer_segments(i))
        dcur, packed = ffn_backward("ffn2", i, st["x2"], st["a2"], st["u2"], tied(ffn2_norm[i:i + 1], token), dcur,
                                    3 * FFN_ROWS + mixer_rows)
        if i % 2 == 0:
            dcur, dpw, dsc, dg = _pool_bwd(st["x1"], dcur, mix_norm[i:i + 1], w[("pool_w", j)], pool_scale[j:j + 1],
                                           f"pool_bwd_{i}")
            gr[("pool_w", j)] = dpw
            small["pool_scale"][j] = jnp.sum(dsc, axis=0)
            small["mix_norm"][i] = jnp.sum(dg, axis=0)
        else:
            dob, dpo, delta = _mla_out_bwd(dcur, st["o"], w[("mla_w_out", j)], f"mla_out_bwd_{i}")
            gr[("mla_w_out", j)] = _tn_matmul(st["o"], dob, f"mla_dout_{i}")
            dk, dv, dq = _flash_bwd(st["q"], st["k"], st["v"], dpo, st["lse"], delta, st["p"], st["m"],
                                    f"flash_bwd_{i}")
            (dcur, h, dlat, cqn, dqp, ckvn, dkv, dg, dgq, dgkv, dghq, dghk) = _mla_pre_bwd(
                st["x1"], dcur, dq, dk, dv, mix_norm[i:i + 1], *mla_weights(j), f"mla_pre_bwd_{i}")
            gr[("mla_w_in", j)] = _tn_matmul(dlat, h, f"mla_din_{i}")
            gr[("mla_w_q_up", j)] = _tn_matmul(dqp, cqn, f"mla_dqup_{i}")
            gr[("mla_w_kv_up", j)] = _tn_matmul(dkv, ckvn, f"mla_dkvup_{i}")
            small["mix_norm"][i] = jnp.sum(dg, axis=0)
            small["mla_q_norm"][j] = _pad_lanes(jnp.sum(dgq, axis=0)[None], D_MODEL)[0]
            small["mla_kv_norm"][j] = _pad_lanes(jnp.sum(dgkv, axis=0)[None], D_MODEL)[0]
            small["mla_q_head_norm"][j] = _pad_lanes(jnp.sum(dghq, axis=0)[None], D_MODEL)[0]
            small["mla_k_head_norm"][j] = _pad_lanes(jnp.sum(dghk, axis=0)[None], D_MODEL)[0]
        packed = lax.dynamic_update_slice(packed, _pack_grads(gr, _mixer_segments(i)), (0, 3 * FFN_ROWS, 0))
        token = reduce_start(packed, _ffn_segments("ffn2", i) + _mixer_segments(i), f"a{i}")
        if i > 0:
            dcur, packed = ffn_backward("ffn1", i, st["x0"], st["a1"], st["u1"], tied(ffn1_norm[i:i + 1], token),
                                        dcur, 3 * FFN_ROWS)
            token = reduce_start(packed, _ffn_segments("ffn1", i), f"b{i}")
    st = saved[0]
    dcur, h, dob, y, da, du, dg = _ffn_bwd(st["x0"], dcur, st["a1"], st["u1"], tied(ffn1_norm[0:1], token),
                                           *ffn_weights("ffn1", 0), "ffn1_bwd_0")
    small["ffn1_norm"][0] = jnp.sum(dg, axis=0)
    grad_x = dcur[None]
    small_order = ("ffn1_norm", "mix_norm", "ffn2_norm", "pool_scale", "mla_q_norm", "mla_kv_norm",
                   "mla_q_head_norm", "mla_k_head_norm")
    rows = [r for k in small_order for r in small[k]]
    rows.append(jnp.zeros((D_MODEL,), F32).at[0].set(loss_part))
    rows.append(jnp.zeros((D_MODEL,), F32))
    small_sum = _sum_parts(_all_gather(jnp.stack(rows), "small_all_gather"), "small_sum")
    loss = small_sum[SM_ROWS - 2, 0]
    token = small_sum
    for seg, lhs, rhs in zip(_ffn_segments("ffn1", 0), (da, du, y), (h, h, dob)):
        packed = _tn_matmul_packed(lhs, rhs, None, 0, FFN_ROWS, f"ffn1_d{seg[0][7:]}_0", behind=token)
        token = reduce_start(packed, [seg], f"b0_{seg[0][7:]}")
    last = "ffn1_w_down"

    off = 0
    for k in small_order:
        cnt = len(small[k])
        g = small_sum[off:off + cnt]
        off += cnt
        if k == "mla_q_norm":
            g = lax.dynamic_slice_in_dim(g[:, :Q_LORA], dev * (Q_LORA // N_DEV), Q_LORA // N_DEV, axis=1)
        elif k == "mla_kv_norm":
            g = lax.dynamic_slice_in_dim(g[:, :KV_LORA], dev * (KV_LORA // N_DEV), KV_LORA // N_DEV, axis=1)
        elif k in ("mla_q_head_norm", "mla_k_head_norm"):
            g = g[:, :QK_HEAD]
        grads[k] = g

    deltas, new_m, new_v = {}, {}, {}

    def update(k, behind):
        to_view = _t if k in COLUMN_SHARDED else (lambda a: a)
        shape = to_view(p[k]).shape
        operands = [to_view(a) for a in (p[k], moments_m[k], moments_v[k])]
        if k in grads:
            view = (-1, shape[-1])
            w2, m2, v2 = (a.reshape(view) for a in operands)
            results = (*_adamw(w2, grads[k].reshape(view), m2, v2, behind, "adamw_" + k), grads[k])
        else:
            view = (shape[0], -1, shape[-1])
            pieces = [grads[(k, idx)].reshape(view[1:]) for idx in range(shape[0])]
            w3, m3, v3 = (a.reshape(view) for a in operands)
            results = _adamw_layers(w3, pieces, m3, v3, behind, "adamw_" + k)
        deltas[k], new_m[k], new_v[k], grads[k] = (to_view(a.reshape(shape)) for a in results)
        return results[0]

    done = token
    for k in WEIGHTS:
        if k != last:
            done = update(k, done)
    reduce_finish(done)
    update(last, done)

    return (loss, grad_x, *[grads[k] for k in WEIGHTS], *[deltas[k] for k in WEIGHTS],
            *[new_m[k] for k in WEIGHTS], *[new_v[k] for k in WEIGHTS])
```
